```python
import jax
import jax.numpy as jnp
from jax import lax
import numpy as np

D_MODEL = 1024
BATCH = 8
SEQ = 4096
DEPTH = 2

GLA_HEADS = 4
GLA_DK = 64
GLA_DV = 128
GLA_KW = GLA_HEADS * GLA_DK
GLA_VW = GLA_HEADS * GLA_DV
GLA_DECAY_RANK = 16
GLA_TEMP = 16.0
GLA_CHUNK = 64
LRU_WIDTH = 512
LRU_BLOCKS = 8
LRU_BLOCK_DIM = LRU_WIDTH // LRU_BLOCKS
LRU_CONV = 4
LRU_C = 8.0
RWKV_HEAD = 64
RWKV_WIDTH = 512
RWKV_HEADS = RWKV_WIDTH // RWKV_HEAD
RWKV_DECAY_RANK = 64
RWKV_A_RANK = 64
RWKV_GATE_RANK = 160
RWKV_LNX_EPS = 64e-5
RWKV_IN_SIZES = (RWKV_WIDTH, RWKV_WIDTH, RWKV_WIDTH, RWKV_DECAY_RANK, RWKV_A_RANK, RWKV_GATE_RANK)
RWKV_IN = sum(RWKV_IN_SIZES)
N_BRANCH = 3
IN_SIZES = (GLA_KW, GLA_KW, GLA_VW, GLA_VW, GLA_DECAY_RANK, LRU_WIDTH, LRU_WIDTH, RWKV_IN, N_BRANCH * D_MODEL)
N_IN = sum(IN_SIZES)
FFN_DENSE = 2816
N_EXPERTS = 8
TOP_K = 2
FFN_EXPERT = 3584
MOE_BLOCK = 256
N_DENSE_LAYERS = (DEPTH + 1) // 2
N_MOE_LAYERS = DEPTH // 2
DEEPNORM_ALPHA = (2 * DEPTH) ** 0.25
DEEPNORM_BETA = (8 * DEPTH) ** -0.25
LN_EPS = 1e-5

kernel_name = 'hybrid_gla_rglru_rwkv7_deepnorm_moe'


def split_last(t, sizes):
    points = np.cumsum(np.array(sizes))[:-1].tolist()
    return jnp.split(t, points, axis=-1)


def layer_norm(x, g, b, eps=LN_EPS):
    xf = x.astype(jnp.float32)
    mu = jnp.mean(xf, axis=-1, keepdims=True)
    var = jnp.mean(jnp.square(xf - mu), axis=-1, keepdims=True)
    return ((xf - mu) * lax.rsqrt(var + eps) * g + b).astype(x.dtype)


def head_norm(x, g, b, eps):
    mu = jnp.mean(x, axis=-1, keepdims=True)
    var = jnp.mean(jnp.square(x - mu), axis=-1, keepdims=True)
    y = (x - mu) * lax.rsqrt(var + eps)
    return y.reshape(x.shape[0], x.shape[1], -1) * g + b


def gla_chunked(q, k, v, log_a):
    B, S, H, DK = q.shape
    DV = v.shape[-1]
    C = GLA_CHUNK
    N = S // C

    def chunks(t):
        return t.reshape(B, N, C, H, t.shape[-1]).transpose(0, 3, 1, 2, 4)

    q, k, v, log_a = chunks(q), chunks(k), chunks(v), chunks(log_a)
    b = jnp.cumsum(log_a, axis=3)
    b_end = b[:, :, :, -1:, :]
    q_dec = q * jnp.exp(b)
    k_inv = k * jnp.exp(-b)
    k_end = k * jnp.exp(b_end - b)
    causal = jnp.tril(jnp.ones((C, C), dtype=bool))
    scores = jnp.einsum('bhnid,bhnjd->bhnij', q_dec, k_inv)
    scores = jnp.where(causal, scores, 0.0)
    o_intra = jnp.einsum('bhnij,bhnje->bhnie', scores, v)
    d_state = jnp.einsum('bhnjd,bhnje->bhnde', k_end, v)
    decay_end = jnp.exp(b_end[:, :, :, 0, :])

    def step(state, inp):
        dec, ds = inp
        return state * dec[..., None] + ds, state

    s0 = jnp.zeros((B, H, DK, DV), jnp.float32)
    _, s_before = lax.scan(step, s0, (jnp.moveaxis(decay_end, 2, 0), jnp.moveaxis(d_state, 2, 0)))
    s_before = jnp.moveaxis(s_before, 0, 2)
    o = o_intra + jnp.einsum('bhnid,bhnde->bhnie', q_dec, s_before)
    return o.transpose(0, 2, 3, 1, 4).reshape(B, S, H, DV)


def gla_branch(q, k, v, r, dec_lr, w_decay_up, b_decay, norm_g, norm_b):
    B, S, _ = q.shape
    f32 = jnp.float32
    qh = q.reshape(B, S, GLA_HEADS, GLA_DK).astype(f32) * (GLA_DK ** -0.5)
    kh = k.reshape(B, S, GLA_HEADS, GLA_DK).astype(f32)
    vh = v.reshape(B, S, GLA_HEADS, GLA_DV).astype(f32)
    log_a = jax.nn.log_sigmoid((dec_lr @ w_decay_up + b_decay).astype(f32)) / GLA_TEMP
    o = gla_chunked(qh, kh, vh, log_a.reshape(B, S, GLA_HEADS, GLA_DK))
    o = head_norm(o, norm_g, norm_b, LN_EPS)
    return o.astype(r.dtype) * jax.nn.silu(r)


def rglru_branch(xr, gate, conv_w, conv_b, w_r, b_r, w_i, b_i, lam):
    B, S, W = xr.shape
    xp = jnp.pad(xr, ((0, 0), (LRU_CONV - 1, 0), (0, 0)))
    xc = conv_b + sum(xp[:, j:j + S] * conv_w[j] for j in range(LRU_CONV))
    xb = xc.reshape(B, S, LRU_BLOCKS, LRU_BLOCK_DIM)
    r = jax.nn.sigmoid(jnp.einsum('bsgi,gij->bsgj', xb, w_r).reshape(B, S, W) + b_r)
    i = jax.nn.sigmoid(jnp.einsum('bsgi,gij->bsgj', xb, w_i).reshape(B, S, W) + b_i)
    log_a = (-LRU_C * r * jax.nn.softplus(-lam)).astype(jnp.float32)
    a = jnp.exp(log_a)
    u = jnp.sqrt(-jnp.expm1(2.0 * log_a)) * (i * xc).astype(jnp.float32)

    def combine(c1, c2):
        a1, b1 = c1
        a2, b2 = c2
        return a1 * a2, a2 * b1 + b2

    _, h = lax.associative_scan(combine, (a, u), axis=1)
    return jax.nn.gelu(gate) * h.astype(gate.dtype)


def rwkv7_scan(r, w, k, v, a, b):
    B, S, H, N = r.shape

    def step(state, inp):
        r_t, w_t, k_t, v_t, a_t, b_t = inp
        sa = jnp.einsum('bhij,bhj->bhi', state, a_t)
        state = (state * w_t[:, :, None, :] + sa[..., None] * b_t[:, :, None, :]
                 + v_t[..., None] * k_t[:, :, None, :])
        return state, jnp.einsum('bhij,bhj->bhi', state, r_t)

    xs = tuple(jnp.moveaxis(t, 1, 0) for t in (r, w, k, v, a, b))
    s0 = jnp.zeros((B, H, N, N), jnp.float32)
    _, y = lax.scan(step, s0, xs)
    return jnp.moveaxis(y, 0, 1)


def rwkv7_branch(p, mu, w0, w2, a0, a2, g2, k_k, k_a, r_k, lnx_g, lnx_b):
    B, S, _ = p.shape
    f32 = jnp.float32
    p_prev = jnp.pad(p, ((0, 0), (1, 0), (0, 0)))[:, :-1]
    p = p + (p_prev - p) * mu
    r, k, v, wl, al, gl = split_last(p, RWKV_IN_SIZES)
    w_log = -jax.nn.softplus(-(w0 + jnp.tanh(wl) @ w2)) - 0.5
    decay = jnp.exp(-jnp.exp(w_log.astype(f32)))
    a = jax.nn.sigmoid(a0 + al @ a2)
    g = jax.nn.sigmoid(gl) @ g2

    def heads(t):
        return t.reshape(B, S, RWKV_HEADS, RWKV_HEAD).astype(f32)

    kk = heads(k * k_k)
    kk = kk / jnp.maximum(jnp.linalg.norm(kk, axis=-1, keepdims=True), 1e-12)
    k = k * (1.0 + (a - 1.0) * k_a)
    rh, kh, vh, ah, wh = heads(r), heads(k), heads(v), heads(a), heads(decay)
    y = rwkv7_scan(rh, wh, kh, vh, -kk, kk * ah)
    y = head_norm(y, lnx_g, lnx_b, RWKV_LNX_EPS)
    bonus = jnp.sum(rh * kh * r_k.reshape(RWKV_HEADS, RWKV_HEAD), axis=-1, keepdims=True) * vh
    y = y + bonus.reshape(B, S, RWKV_WIDTH)
    return y.astype(g.dtype) * g


def swiglu(x, w_gate, w_up, w_down):
    return (jax.nn.silu(x @ w_gate) * (x @ w_up)) @ w_down


def moe_swiglu(x, w_router, w_gate, w_up, w_down):
    B, S, D = x.shape
    xt = x.reshape(-1, D)
    T = xt.shape[0]
    TK = T * TOP_K
    logits = (xt @ w_router).astype(jnp.float32)
    top_logit, top_e = lax.top_k(logits, TOP_K)
    top_w = jax.nn.softmax(top_logit, axis=-1)
    flat_e = top_e.reshape(-1)
    flat_tok = jnp.repeat(jnp.arange(T, dtype=jnp.int32), TOP_K)
    order = jnp.argsort(flat_e)
    e_sorted = flat_e[order]
    counts = jnp.bincount(flat_e, length=N_EXPERTS)
    padded = (counts + MOE_BLOCK - 1) // MOE_BLOCK * MOE_BLOCK
    pad_end = jnp.cumsum(padded)
    pad_start = pad_end - padded
    grp_start = jnp.cumsum(counts) - counts
    dest = pad_start[e_sorted] + jnp.arange(TK, dtype=jnp.int32) - grp_start[e_sorted]
    n_blocks = -(-TK // MOE_BLOCK) + N_EXPERTS
    n_buf = n_blocks * MOE_BLOCK
    buf_tok = jnp.zeros((n_buf,), jnp.int32).at[dest].set(flat_tok[order])
    buf_w = jnp.zeros((n_buf,), jnp.float32).at[dest].set(top_w.reshape(-1)[order])
    blk_start = jnp.arange(n_blocks, dtype=jnp.int32) * MOE_BLOCK
    blk_e = jnp.minimum(jnp.sum(blk_start[:, None] >= pad_end[None, :], axis=1), N_EXPERTS - 1)
    xb = xt[buf_tok].reshape(n_blocks, MOE_BLOCK, D)

    def expert_block(args):
        xblk, e = args
        h = jax.nn.silu(xblk @ w_gate[e]) * (xblk @ w_up[e])
        return h @ w_down[e]

    yb = lax.map(expert_block, (xb, blk_e)).reshape(n_buf, D)
    y = jnp.zeros_like(xt).at[buf_tok].add(yb * buf_w[:, None].astype(x.dtype))
    return y.reshape(B, S, D)


def setup_inputs(seed: int = 0) -> dict:
    key = jax.random.key(seed)
    keys = jax.random.split(key, 48)
    it = iter([keys[i] for i in range(48)])
    L, ND, NM, D = DEPTH, N_DENSE_LAYERS, N_MOE_LAYERS, D_MODEL
    f32 = jnp.float32

    def nrm(shape, scale):
        return scale * jax.random.normal(next(it), shape, f32)

    def unif(shape, lo, hi):
        return jax.random.uniform(next(it), shape, f32, lo, hi)

    s = jnp.power(unif((L, LRU_WIDTH), 0.9, 0.999), 1.0 / LRU_C)
    lam = jnp.log(s) - jnp.log1p(-s)
    beta = DEEPNORM_BETA
    return {
        'x': nrm((BATCH, SEQ, D), 1.0),
        'w_in': nrm((L, D, N_IN), D ** -0.5),
        'b_in': nrm((L, N_IN), 0.01),
        'gla_w_decay_up': nrm((L, GLA_DECAY_RANK, GLA_KW), GLA_DECAY_RANK ** -0.5),
        'gla_b_decay': nrm((L, GLA_KW), 0.5),
        'gla_norm_g': 1.0 + nrm((L, GLA_VW), 0.02),
        'gla_norm_b': nrm((L, GLA_VW), 0.02),
        'lru_conv_w': nrm((L, LRU_CONV, LRU_WIDTH), LRU_CONV ** -0.5),
        'lru_conv_b': nrm((L, LRU_WIDTH), 0.01),
        'lru_w_r': nrm((L, LRU_BLOCKS, LRU_BLOCK_DIM, LRU_BLOCK_DIM), LRU_BLOCK_DIM ** -0.5),
        'lru_b_r': nrm((L, LRU_WIDTH), 0.1),
        'lru_w_i': nrm((L, LRU_BLOCKS, LRU_BLOCK_DIM, LRU_BLOCK_DIM), LRU_BLOCK_DIM ** -0.5),
        'lru_b_i': nrm((L, LRU_WIDTH), 0.1),
        'lru_lambda': lam,
        'rwkv_mu': unif((L, RWKV_IN), 0.0, 1.0),
        'rwkv_w0': unif((L, RWKV_WIDTH), -6.0, -1.0),
        'rwkv_w2': nrm((L, RWKV_DECAY_RANK, RWKV_WIDTH), 0.5 * RWKV_DECAY_RANK ** -0.5),
        'rwkv_a0': nrm((L, RWKV_WIDTH), 0.1),
        'rwkv_a2': nrm((L, RWKV_A_RANK, RWKV_WIDTH), 0.5 * RWKV_A_RANK ** -0.5),
        'rwkv_g2': nrm((L, RWKV_GATE_RANK, RWKV_WIDTH), RWKV_GATE_RANK ** -0.5),
        'rwkv_k_k': 0.85 + nrm((L, RWKV_WIDTH), 0.05),
        'rwkv_k_a': 1.0 + nrm((L, RWKV_WIDTH), 0.05),
        'rwkv_r_k': nrm((L, RWKV_WIDTH), 0.1),
        'rwkv_lnx_g': 1.0 + nrm((L, RWKV_WIDTH), 0.02),
        'rwkv_lnx_b': nrm((L, RWKV_WIDTH), 0.02),
        'p_gla': nrm((L, GLA_VW, D), beta * GLA_VW ** -0.5),
        'p_lru': nrm((L, LRU_WIDTH, D), beta * LRU_WIDTH ** -0.5),
        'p_rwkv': nrm((L, RWKV_WIDTH, D), beta * RWKV_WIDTH ** -0.5),
        'w_out': nrm((L, D, D), beta * D ** -0.5),
        'ln_mix_g': 1.0 + nrm((L, D), 0.02),
        'ln_mix_b': nrm((L, D), 0.02),
        'ffn_w_gate': nrm((ND, D, FFN_DENSE), D ** -0.5),
        'ffn_w_up': nrm((ND, D, FFN_DENSE), D ** -0.5),
        'ffn_w_down': nrm((ND, FFN_DENSE, D), beta * FFN_DENSE ** -0.5),
        'moe_w_router': nrm((NM, D, N_EXPERTS), D ** -0.5),
        'moe_w_gate': nrm((NM, N_EXPERTS, D, FFN_EXPERT), D ** -0.5),
        'moe_w_up': nrm((NM, N_EXPERTS, D, FFN_EXPERT), D ** -0.5),
        'moe_w_down': nrm((NM, N_EXPERTS, FFN_EXPERT, D), beta * FFN_EXPERT ** -0.5),
        'ln_ffn_g': 1.0 + nrm((L, D), 0.02),
        'ln_ffn_b': nrm((L, D), 0.02),
    }


def reference(x, w_in, b_in, gla_w_decay_up, gla_b_decay, gla_norm_g, gla_norm_b,
              lru_conv_w, lru_conv_b, lru_w_r, lru_b_r, lru_w_i, lru_b_i, lru_lambda,
              rwkv_mu, rwkv_w0, rwkv_w2, rwkv_a0, rwkv_a2, rwkv_g2, rwkv_k_k, rwkv_k_a, rwkv_r_k,
              rwkv_lnx_g, rwkv_lnx_b, p_gla, p_lru, p_rwkv, w_out, ln_mix_g, ln_mix_b,
              ffn_w_gate, ffn_w_up, ffn_w_down, moe_w_router, moe_w_gate, moe_w_up, moe_w_down,
              ln_ffn_g, ln_ffn_b):
    B, S, D = x.shape
    for l in range(DEPTH):
        h = x @ w_in[l] + b_in[l]
        q, k, v, r, dec_lr, lru_x, lru_g, rw_in, gate_logits = split_last(h, IN_SIZES)
        o_gla = gla_branch(q, k, v, r, dec_lr, gla_w_decay_up[l], gla_b_decay[l],
                           gla_norm_g[l], gla_norm_b[l])
        o_lru = rglru_branch(lru_x, lru_g, lru_conv_w[l], lru_conv_b[l], lru_w_r[l], lru_b_r[l],
                             lru_w_i[l], lru_b_i[l], lru_lambda[l])
        o_rwkv = rwkv7_branch(rw_in, rwkv_mu[l], rwkv_w0[l], rwkv_w2[l], rwkv_a0[l], rwkv_a2[l],
                              rwkv_g2[l], rwkv_k_k[l], rwkv_k_a[l], rwkv_r_k[l],
                              rwkv_lnx_g[l], rwkv_lnx_b[l])
        gates = jax.nn.sigmoid(gate_logits).reshape(B, S, N_BRANCH, D)
        merged = (gates[:, :, 0] * (o_gla @ p_gla[l])
                  + gates[:, :, 1] * (o_lru @ p_lru[l])
                  + gates[:, :, 2] * (o_rwkv @ p_rwkv[l]))
        x = layer_norm(DEEPNORM_ALPHA * x + merged @ w_out[l], ln_mix_g[l], ln_mix_b[l])
        i = l // 2
        if l % 2 == 0:
            f = swiglu(x, ffn_w_gate[i], ffn_w_up[i], ffn_w_down[i])
        else:
            f = moe_swiglu(x, moe_w_router[i], moe_w_gate[i], moe_w_up[i], moe_w_down[i])
        x = layer_norm(DEEPNORM_ALPHA * x + f, ln_ffn_g[l], ln_ffn_b[l])
    return x
```

```python
import functools

import jax
import jax.numpy as jnp
from jax import lax
from jax.experimental import pallas as pl
from jax.experimental.pallas import tpu as pltpu

F32 = jnp.float32
BF16 = jnp.bfloat16
I32 = jnp.int32
HIGHEST = lax.Precision.HIGHEST

D_MODEL = 1024
DEPTH = 2
GLA_HEADS, GLA_DK, GLA_DV = 4, 64, 128
GLA_KW, GLA_VW = GLA_HEADS * GLA_DK, GLA_HEADS * GLA_DV
GLA_DECAY_RANK = 16
GLA_TEMP = 16.0
LRU_WIDTH, LRU_BLOCKS, LRU_CONV, LRU_C = 512, 8, 4, 8.0
RWKV_HEAD, RWKV_WIDTH = 64, 512
RWKV_HEADS = RWKV_WIDTH // RWKV_HEAD
RWKV_DECAY_RANK, RWKV_A_RANK, RWKV_GATE_RANK = 64, 64, 160
RWKV_LNX_EPS = 64e-5
N_BRANCH = 3
N_EXPERTS, TOP_K = 8, 2
DEEPNORM_ALPHA = (2 * DEPTH) ** 0.25
LN_EPS = 1e-5

LANES = 128
SUBLANES = 8
VMEM_LIMIT_BYTES = 56 * 1024 * 1024

COL_GATES = (0, 3072)
COL_RW_RKV = (3072, 1536)
COL_GLA_V = (4608, 512)
COL_GLA_R = (5120, 512)
COL_LRU_X = (5632, 512)
COL_LRU_G = (6144, 512)
COL_GLA_Q = (6656, 256)
COL_GLA_K = (6912, 256)
COL_SMALL = (7168, 512)
N_IN_PAD = 7680
SM_WA = (0, 128)
SM_GL = (128, 256)
SM_DEC = (384, 128)

CHUNK = 64
INV_BLOCK = 16
MOE_ROWS = 512


def _cparams(*sem):
    return pltpu.CompilerParams(dimension_semantics=sem, vmem_limit_bytes=VMEM_LIMIT_BYTES)


def _sigmoid(x):
    return 1.0 / (1.0 + jnp.exp(-x))


def _softplus(x):
    return jnp.maximum(x, 0.0) + jnp.log1p(jnp.exp(-jnp.abs(x)))


def _bdot(a, b):
    return jnp.dot(a.astype(BF16), b.astype(BF16), preferred_element_type=F32)


def _bdot_nt(a, b):
    return lax.dot_general(a.astype(BF16), b.astype(BF16), (((1,), (1,)), ((), ())),
                           preferred_element_type=F32)


def _bdot_tn(a, b):
    return lax.dot_general(a.astype(BF16), b.astype(BF16), (((0,), (0,)), ((), ())),
                           preferred_element_type=F32)


def _fdot(a, b):
    return jnp.dot(a, b, precision=HIGHEST, preferred_element_type=F32)


def _layer_norm_rows(z, g, b, eps):
    mu = jnp.mean(z, axis=-1, keepdims=True)
    zc = z - mu
    var = jnp.mean(zc * zc, axis=-1, keepdims=True)
    return zc * lax.rsqrt(var + eps) * g + b


def _tri_masks(n):
    row = lax.broadcasted_iota(I32, (n, n), 0)
    col = lax.broadcasted_iota(I32, (n, n), 1)
    return row, col


def _proj_kernel(x_ref, w_ref, b_ref, o_ref):
    o_ref[...] = jnp.dot(x_ref[...], w_ref[...], preferred_element_type=F32) + b_ref[...]


def _in_projection(xb, w, b):
    t, d = xb.shape
    n = w.shape[1]
    tm = min(1024, t)
    tn = 1536
    return pl.pallas_call(
        _proj_kernel,
        out_shape=jax.ShapeDtypeStruct((t, n), F32),
        grid=(t // tm, n // tn),
        in_specs=[pl.BlockSpec((tm, d), lambda i, j: (i, 0)),
                  pl.BlockSpec((d, tn), lambda i, j: (0, j)),
                  pl.BlockSpec((1, tn), lambda i, j: (0, j))],
        out_specs=pl.BlockSpec((tm, tn), lambda i, j: (i, j)),
        compiler_params=_cparams("parallel", "arbitrary"),
    )(xb, w, b)


def _gla_kernel(q_ref, k_ref, v_ref, r_ref, sm_ref, wup_ref, bdec_ref, ng_ref, nb_ref,
                o_ref, st_ref, *, chunks):
    @pl.when(pl.program_id(1) == 0)
    def _():
        st_ref[...] = jnp.zeros_like(st_ref)

    row, col = _tri_masks(CHUNK)
    causal = row >= col
    ltri = causal.astype(F32)
    for c in range(chunks):
        rows = pl.ds(c * CHUNK, CHUNK)
        dec = sm_ref[rows, SM_DEC[0]:SM_DEC[0] + SM_DEC[1]]
        logits = _bdot(dec, wup_ref[...]) + bdec_ref[...]
        log_a = -_softplus(-logits) * (1.0 / GLA_TEMP)
        cum = _fdot(ltri, log_a)
        cum_end = cum[CHUNK - 1:CHUNK, :]
        q = q_ref[rows, :] * (GLA_DK ** -0.5)
        k = k_ref[rows, :]
        q_dec = (q * jnp.exp(cum)).astype(BF16)
        k_inv = (k * jnp.exp(-cum)).astype(BF16)
        k_end = (k * jnp.exp(cum_end - cum)).astype(BF16)
        decay_end = jnp.exp(cum_end)
        v = v_ref[rows, :].astype(BF16)
        outs = []
        for h in range(GLA_HEADS):
            ks = slice(h * GLA_DK, (h + 1) * GLA_DK)
            vs = slice(h * GLA_DV, (h + 1) * GLA_DV)
            qh, kih, keh, vh = q_dec[:, ks], k_inv[:, ks], k_end[:, ks], v[:, vs]
            scores = jnp.where(causal, _bdot_nt(qh, kih), 0.0)
            state = st_ref[h]
            o = _bdot(scores, vh) + _bdot_nt(qh, state)
            st_ref[h] = state * decay_end[:, ks] + _bdot_tn(vh, keh)
            mu = jnp.mean(o, axis=-1, keepdims=True)
            oc = o - mu
            var = jnp.mean(oc * oc, axis=-1, keepdims=True)
            outs.append(oc * lax.rsqrt(var + LN_EPS))
        y = jnp.concatenate(outs, axis=1) * ng_ref[...] + nb_ref[...]
        r = r_ref[rows, :]
        o_ref[rows, :] = y * (r * _sigmoid(r))


def _gla_branch(h, batch, seq, wup_pad, b_decay, norm_g, norm_b):
    t = batch * seq
    chunks = 4
    rows = CHUNK * chunks
    nblk = seq // rows

    def col_spec(col):
        off, width = col
        return pl.BlockSpec((rows, width), lambda b, n: (b * nblk + n, off // width))

    def full(a):
        return pl.BlockSpec(a.shape, lambda b, n: (0,) * a.ndim)

    args = (wup_pad, b_decay, norm_g, norm_b)
    return pl.pallas_call(
        functools.partial(_gla_kernel, chunks=chunks),
        out_shape=jax.ShapeDtypeStruct((t, GLA_VW), F32),
        grid=(batch, nblk),
        in_specs=[col_spec(COL_GLA_Q), col_spec(COL_GLA_K), col_spec(COL_GLA_V),
                  col_spec(COL_GLA_R), col_spec(COL_SMALL)] + [full(a) for a in args],
        out_specs=pl.BlockSpec((rows, GLA_VW), lambda b, n: (b * nblk + n, 0)),
        scratch_shapes=[pltpu.VMEM((GLA_HEADS, GLA_DV, GLA_DK), F32)],
        compiler_params=_cparams("parallel", "arbitrary"),
    )(h, h, h, h, h, *args)


def _lru_kernel(x_ref, g_ref, cw_ref, cb_ref, wri_ref, bri_ref, sp_ref,
                o_ref, xbuf_ref, a_ref, u_ref, hs_ref, h_ref):
    tm = x_ref.shape[0]
    pad = SUBLANES

    @pl.when(pl.program_id(1) == 0)
    def _():
        xbuf_ref[0:pad, :] = jnp.zeros((pad, LRU_WIDTH), F32)
        h_ref[...] = jnp.zeros_like(h_ref)

    x = x_ref[...]
    xbuf_ref[pad:pad + tm, :] = x
    xc = cb_ref[...] + x * cw_ref[LRU_CONV - 1:LRU_CONV, :]
    for j in range(LRU_CONV - 1):
        back = LRU_CONV - 1 - j
        xc = xc + xbuf_ref[pl.ds(pad - back, tm), :] * cw_ref[j:j + 1, :]
    xbuf_ref[0:pad, :] = x[tm - pad:tm, :]

    ri = _sigmoid(_bdot(xc, wri_ref[...]) + bri_ref[...])
    rg, ig = ri[:, :LRU_WIDTH], ri[:, LRU_WIDTH:]
    log_a = -LRU_C * rg * sp_ref[...]
    a_ref[...] = jnp.exp(log_a)
    u_ref[...] = jnp.sqrt(1.0 - jnp.exp(2.0 * log_a)) * (ig * xc)

    def step(t, hprev):
        hnew = a_ref[pl.ds(t, 1), :] * hprev + u_ref[pl.ds(t, 1), :]
        hs_ref[pl.ds(t, 1), :] = hnew
        return hnew

    h_ref[...] = lax.fori_loop(0, tm, step, h_ref[...], unroll=8)
    o_ref[...] = jax.nn.gelu(g_ref[...]) * hs_ref[...]


def _lru_branch(h, batch, seq, conv_w, conv_b, w_ri, b_ri, softplus_neg_lam):
    t = batch * seq
    tm = min(512, seq)
    nblk = seq // tm

    def col_spec(col):
        off, width = col
        return pl.BlockSpec((tm, width), lambda b, n: (b * nblk + n, off // width))

    def full(a):
        return pl.BlockSpec(a.shape, lambda b, n: (0,) * a.ndim)

    args = (conv_w, conv_b, w_ri, b_ri, softplus_neg_lam)
    return pl.pallas_call(
        _lru_kernel,
        out_shape=jax.ShapeDtypeStruct((t, LRU_WIDTH), F32),
        grid=(batch, nblk),
        in_specs=[col_spec(COL_LRU_X), col_spec(COL_LRU_G)] + [full(a) for a in args],
        out_specs=pl.BlockSpec((tm, LRU_WIDTH), lambda b, n: (b * nblk + n, 0)),
        scratch_shapes=[pltpu.VMEM((tm + SUBLANES, LRU_WIDTH), F32),
                        pltpu.VMEM((tm, LRU_WIDTH), F32),
                        pltpu.VMEM((tm, LRU_WIDTH), F32),
                        pltpu.VMEM((tm, LRU_WIDTH), F32),
                        pltpu.VMEM((1, LRU_WIDTH), F32)],
        compiler_params=_cparams("parallel", "arbitrary"),
    )(h, h, *args)


def _rwkv_prep_kernel(p_ref, pprev_ref, sm_ref, smprev_ref, mu_ref, musm_ref, w2_ref, a2_ref, g2_ref,
                      w0_ref, a0_ref, kk_ref, ka_ref, rk_ref, bd_ref,
                      r_out, k_out, v_out, lw_out, kk_out, kka_out, bonus_out, g_out, *, blocks_per_seq):
    tm = p_ref.shape[0]
    first = (pl.program_id(0) % blocks_per_seq) == 0
    keep = jnp.where(first, 0.0, 1.0)
    row0 = lax.broadcasted_iota(I32, (tm, 1), 0) == 0

    def token_shift(cur_ref, prev_ref, m_ref):
        cur = cur_ref[...]
        prev_row = prev_ref[SUBLANES - 1:SUBLANES, :] * keep
        prev = jnp.where(row0, prev_row, pltpu.roll(cur, 1, axis=0))
        return cur + (prev - cur) * m_ref[...]

    p = token_shift(p_ref, pprev_ref, mu_ref)
    sm = token_shift(sm_ref, smprev_ref, musm_ref)
    r = p[:, 0:RWKV_WIDTH]
    k = p[:, RWKV_WIDTH:2 * RWKV_WIDTH]
    v = p[:, 2 * RWKV_WIDTH:3 * RWKV_WIDTH]
    wa = sm[:, SM_WA[0]:SM_WA[0] + SM_WA[1]]
    gl = sm[:, SM_GL[0]:SM_GL[0] + SM_GL[1]]

    w_log = -_softplus(-(w0_ref[...] + _bdot(jnp.tanh(wa), w2_ref[...]))) - 0.5
    lw_out[...] = -jnp.exp(w_log)
    a = _sigmoid(a0_ref[...] + _bdot(wa, a2_ref[...]))
    g_out[...] = _bdot(_sigmoid(gl), g2_ref[...])

    kk = k * kk_ref[...]
    norm = jnp.sqrt(_fdot(kk * kk, bd_ref[...]))
    kk = kk / jnp.maximum(norm, 1e-12)
    k2 = k * (1.0 + (a - 1.0) * ka_ref[...])
    bonus_out[...] = _fdot(r * k2 * rk_ref[...], bd_ref[...]) * v
    r_out[...] = r
    k_out[...] = k2
    v_out[...] = v
    kk_out[...] = kk
    kka_out[...] = kk * a


def _rwkv_prep(h, batch, seq, params):
    t = batch * seq
    tm = min(512, seq)
    bps = seq // tm
    sub_per_tile = tm // SUBLANES

    def cur_spec(col):
        off, width = col
        return pl.BlockSpec((tm, width), lambda i: (i, off // width))

    def prev_spec(col):
        off, width = col
        return pl.BlockSpec((SUBLANES, width),
                            lambda i: (jnp.maximum(i * sub_per_tile - 1, 0), off // width))

    def full(a):
        return pl.BlockSpec(a.shape, lambda i: (0,) * a.ndim)

    out = jax.ShapeDtypeStruct((t, RWKV_WIDTH), F32)
    ospec = pl.BlockSpec((tm, RWKV_WIDTH), lambda i: (i, 0))
    return pl.pallas_call(
        functools.partial(_rwkv_prep_kernel, blocks_per_seq=bps),
        out_shape=[out] * 8,
        grid=(t // tm,),
        in_specs=[cur_spec(COL_RW_RKV), prev_spec(COL_RW_RKV), cur_spec(COL_SMALL), prev_spec(COL_SMALL)]
        + [full(a) for a in params],
        out_specs=[ospec] * 8,
        compiler_params=_cparams("parallel"),
    )(h, h, h, h, *params)


def _unit_lower_inverse(a, row, col):
    eye = (row == col).astype(F32)
    same_block = jnp.bitwise_xor(row, col) < INV_BLOCK
    d = jnp.where(same_block, a, 0.0)
    low = a - d
    p = eye + d
    dpow = d
    steps = INV_BLOCK.bit_length() - 2
    for _ in range(steps):
        dpow = _fdot(dpow, dpow)
        p = p + _fdot(p, dpow)
    e = _fdot(p, low)
    q = eye + e
    epow = e
    steps = (CHUNK // INV_BLOCK).bit_length() - 2
    for _ in range(steps):
        epow = _fdot(epow, epow)
        q = q + _fdot(q, epow)
    return q, p


def _rwkv_scan_kernel(r_ref, k_ref, v_ref, lw_ref, kk_ref, kka_ref, bonus_ref, g_ref, lg_ref, lb_ref,
                      o_ref, st_ref, *, chunks):
    @pl.when(pl.program_id(1) == 0)
    def _():
        st_ref[...] = jnp.zeros_like(st_ref)

    row, col = _tri_masks(CHUNK)
    incl = row >= col
    strict = row > col
    ltri = incl.astype(F32)
    hd = RWKV_HEAD
    for c in range(chunks):
        rows = pl.ds(c * CHUNK, CHUNK)
        lw = lw_ref[rows, :]
        cum = _fdot(ltri, lw)
        cum_end = cum[CHUNK - 1:CHUNK, :]
        e_inv = jnp.exp(-cum)
        e_end = jnp.exp(cum_end - cum)
        kka = kka_ref[rows, :]
        k = k_ref[rows, :]
        a_bar = -kk_ref[rows, :] * jnp.exp(cum - lw)
        r_bar = r_ref[rows, :] * jnp.exp(cum)
        b_til = kka * e_inv
        k_til = k * e_inv
        b_end = kka * e_end
        k_end = k * e_end
        gamma = jnp.exp(cum_end)
        v = v_ref[rows, :]
        outs = []
        for h in range(RWKV_HEADS):
            s = slice(h * hd, (h + 1) * hd)
            ah, rh, bth, kth, vh = a_bar[:, s], r_bar[:, s], b_til[:, s], k_til[:, s], v[:, s]
            lhs = jnp.concatenate([ah, rh], axis=0)
            rhs = jnp.concatenate([bth, kth], axis=0)
            blocks = _bdot_nt(lhs, rhs)
            a_ab = jnp.where(strict, blocks[:CHUNK, :CHUNK], 0.0)
            a_ak = jnp.where(strict, blocks[:CHUNK, CHUNK:], 0.0)
            r_ab = jnp.where(incl, blocks[CHUNK:, :CHUNK], 0.0)
            r_ak = jnp.where(incl, blocks[CHUNK:, CHUNK:], 0.0)
            state = st_ref[h]
            from_state = _bdot_nt(lhs, state)
            qm, pm = _unit_lower_inverse(a_ab, row, col)
            u = _fdot(qm, _fdot(pm, from_state[:CHUNK] + _bdot(a_ak, vh)))
            y = from_state[CHUNK:] + _bdot(r_ab, u) + _bdot(r_ak, vh)
            st_ref[h] = state * gamma[:, s] + _bdot_tn(u, b_end[:, s]) + _bdot_tn(vh, k_end[:, s])
            mu = jnp.mean(y, axis=-1, keepdims=True)
            yc = y - mu
            var = jnp.mean(yc * yc, axis=-1, keepdims=True)
            outs.append(yc * lax.rsqrt(var + RWKV_LNX_EPS))
        y = jnp.concatenate(outs, axis=1) * lg_ref[...] + lb_ref[...]
        o_ref[rows, :] = (y + bonus_ref[rows, :]) * g_ref[rows, :]


def _rwkv_scan(prep, batch, seq, lnx_g, lnx_b):
    t = batch * seq
    chunks = 2
    rows = CHUNK * chunks
    nblk = seq // rows
    spec = pl.BlockSpec((rows, RWKV_WIDTH), lambda b, n: (b * nblk + n, 0))
    pspec = pl.BlockSpec((1, RWKV_WIDTH), lambda b, n: (0, 0))
    return pl.pallas_call(
        functools.partial(_rwkv_scan_kernel, chunks=chunks),
        out_shape=jax.ShapeDtypeStruct((t, RWKV_WIDTH), F32),
        grid=(batch, nblk),
        in_specs=[spec] * 8 + [pspec, pspec],
        out_specs=spec,
        scratch_shapes=[pltpu.VMEM((RWKV_HEADS, RWKV_HEAD, RWKV_HEAD), F32)],
        compiler_params=_cparams("parallel", "arbitrary"),
    )(*prep, lnx_g, lnx_b)


def _merge_kernel(gt_ref, og_ref, ol_ref, or_ref, x_ref, pg_ref, pl_ref, pr_ref, wo_ref, lg_ref, lb_ref,
                  xo_ref, xb_ref):
    d = D_MODEL
    merged = (_sigmoid(gt_ref[:, 0:d]) * _bdot(og_ref[...], pg_ref[...])
              + _sigmoid(gt_ref[:, d:2 * d]) * _bdot(ol_ref[...], pl_ref[...])
              + _sigmoid(gt_ref[:, 2 * d:3 * d]) * _bdot(or_ref[...], pr_ref[...]))
    z = DEEPNORM_ALPHA * x_ref[...] + _bdot(merged, wo_ref[...])
    out = _layer_norm_rows(z, lg_ref[...], lb_ref[...], LN_EPS)
    xo_ref[...] = out
    xb_ref[...] = out.astype(BF16)


def _merge(h, o_gla, o_lru, o_rwkv, x, p_gla, p_lru, p_rwkv, w_out, ln_g, ln_b):
    t = x.shape[0]
    tm = min(512, t)

    def rows(width):
        return pl.BlockSpec((tm, width), lambda i: (i, 0))

    def full(a):
        return pl.BlockSpec(a.shape, lambda i: (0,) * a.ndim)

    params = (p_gla, p_lru, p_rwkv, w_out, ln_g, ln_b)
    return pl.pallas_call(
        _merge_kernel,
        out_shape=[jax.ShapeDtypeStruct((t, D_MODEL), F32), jax.ShapeDtypeStruct((t, D_MODEL), BF16)],
        grid=(t // tm,),
        in_specs=[rows(COL_GATES[1]), rows(GLA_VW), rows(LRU_WIDTH), rows(RWKV_WIDTH), rows(D_MODEL)]
        + [full(a) for a in params],
        out_specs=[rows(D_MODEL), rows(D_MODEL)],
        compiler_params=_cparams("parallel"),
    )(h, o_gla, o_lru, o_rwkv, x, *params)


def _ffn_kernel(xb_ref, x_ref, wg_ref, wu_ref, wd_ref, lg_ref, lb_ref, xo_ref, xbo_ref, acc_ref):
    f = pl.program_id(1)

    @pl.when(f == 0)
    def _():
        acc_ref[...] = jnp.zeros_like(acc_ref)

    xb = xb_ref[...]
    g = jnp.dot(xb, wg_ref[...], preferred_element_type=F32)
    u = jnp.dot(xb, wu_ref[...], preferred_element_type=F32)
    acc_ref[...] += _bdot(g * _sigmoid(g) * u, wd_ref[...])

    @pl.when(f == pl.num_programs(1) - 1)
    def _():
        z = DEEPNORM_ALPHA * x_ref[...] + acc_ref[...]
        out = _layer_norm_rows(z, lg_ref[...], lb_ref[...], LN_EPS)
        xo_ref[...] = out
        xbo_ref[...] = out.astype(BF16)


def _dense_ffn(xb, x, wg, wu, wd, ln_g, ln_b):
    t = x.shape[0]
    tm = min(512, t)
    ff = wg.shape[1]
    tf = ff // 2
    rows = lambda width: pl.BlockSpec((tm, width), lambda i, f: (i, 0))
    vec = pl.BlockSpec((1, D_MODEL), lambda i, f: (0, 0))
    return pl.pallas_call(
        _ffn_kernel,
        out_shape=[jax.ShapeDtypeStruct((t, D_MODEL), F32), jax.ShapeDtypeStruct((t, D_MODEL), BF16)],
        grid=(t // tm, ff // tf),
        in_specs=[rows(D_MODEL), rows(D_MODEL),
                  pl.BlockSpec((D_MODEL, tf), lambda i, f: (0, f)),
                  pl.BlockSpec((D_MODEL, tf), lambda i, f: (0, f)),
                  pl.BlockSpec((tf, D_MODEL), lambda i, f: (f, 0)), vec, vec],
        out_specs=[rows(D_MODEL), rows(D_MODEL)],
        scratch_shapes=[pltpu.VMEM((tm, D_MODEL), F32)],
        compiler_params=_cparams("parallel", "arbitrary"),
    )(xb, x, wg, wu, wd, ln_g, ln_b)


def _router_kernel(x_ref, wr_ref, meta_ref, wts_ref, cnt_ref, carry_ref):
    tm = x_ref.shape[0]

    @pl.when(pl.program_id(0) == 0)
    def _():
        carry_ref[...] = jnp.zeros_like(carry_ref)

    lane = lax.broadcasted_iota(I32, (tm, LANES), 1)
    neg = jnp.float32(-jnp.inf)
    logits = jnp.where(lane < N_EXPERTS, _fdot(x_ref[...], wr_ref[...]), neg)
    m1 = jnp.max(logits, axis=-1, keepdims=True)
    e1 = jnp.min(jnp.where(logits == m1, lane, LANES), axis=-1, keepdims=True)
    rest = jnp.where(lane == e1, neg, logits)
    m2 = jnp.max(rest, axis=-1, keepdims=True)
    e2 = jnp.min(jnp.where(rest == m2, lane, LANES), axis=-1, keepdims=True)
    ex = jnp.exp(m2 - m1)
    w1 = 1.0 / (1.0 + ex)
    w2 = ex / (1.0 + ex)

    hot1 = lane == e1
    hot2 = lane == e2
    onehot = jnp.where(hot1 | hot2, 1.0, 0.0)
    row, col = _tri_masks(tm)
    before = (row > col).astype(BF16)
    prefix = jnp.dot(before, onehot.astype(BF16), preferred_element_type=F32) + carry_ref[...]
    rank1 = jnp.sum(jnp.where(hot1, prefix, 0.0), axis=-1, keepdims=True).astype(I32)
    rank2 = jnp.sum(jnp.where(hot2, prefix, 0.0), axis=-1, keepdims=True).astype(I32)
    carry_ref[...] += jnp.sum(onehot, axis=0, keepdims=True)
    cnt_ref[...] = carry_ref[...]

    meta = jnp.where(lane == 0, e1, jnp.where(lane == 1, e2, jnp.where(lane == 2, rank1, rank2)))
    meta_ref[...] = jnp.where(lane < 4, meta, 0)
    wts_ref[...] = jnp.where(lane == 0, w1, jnp.where(lane == 1, w2, 0.0))


def _router(x, wr_pad):
    t = x.shape[0]
    tm = min(512, t)
    return pl.pallas_call(
        _router_kernel,
        out_shape=[jax.ShapeDtypeStruct((t, LANES), I32), jax.ShapeDtypeStruct((t, LANES), F32),
                   jax.ShapeDtypeStruct((1, LANES), F32)],
        grid=(t // tm,),
        in_specs=[pl.BlockSpec((tm, D_MODEL), lambda i: (i, 0)),
                  pl.BlockSpec((D_MODEL, LANES), lambda i: (0, 0))],
        out_specs=[pl.BlockSpec((tm, LANES), lambda i: (i, 0)), pl.BlockSpec((tm, LANES), lambda i: (i, 0)),
                   pl.BlockSpec((1, LANES), lambda i: (0, 0))],
        scratch_shapes=[pltpu.VMEM((1, LANES), F32)],
        compiler_params=_cparams("arbitrary"),
    )(x, wr_pad)


def _dispatch_kernel(dest_ref, x_ref, xs_in_ref, xs_ref, sem):
    del xs_in_ref
    tm = x_ref.shape[0]

    def row_copy(r, j):
        return pltpu.make_async_copy(x_ref.at[pl.ds(r, 1)], xs_ref.at[pl.ds(dest_ref[TOP_K * r + j], 1)], sem)

    def start(r, c):
        for j in range(TOP_K):
            row_copy(r, j).start()
        return c

    def wait(r, c):
        for j in range(TOP_K):
            row_copy(r, j).wait()
        return c

    lax.fori_loop(0, tm, start, 0)
    lax.fori_loop(0, tm, wait, 0)


def _dispatch(x, dest_flat, n_slots):
    t = x.shape[0]
    tm = min(256, t)
    xs0 = jnp.zeros((n_slots, D_MODEL), F32)
    return pl.pallas_call(
        _dispatch_kernel,
        out_shape=jax.ShapeDtypeStruct((n_slots, D_MODEL), F32),
        grid=(t // tm,),
        in_specs=[pl.BlockSpec((tm * TOP_K,), lambda i: (i,), memory_space=pltpu.SMEM),
                  pl.BlockSpec((tm, D_MODEL), lambda i: (i, 0)),
                  pl.BlockSpec(memory_space=pl.ANY)],
        out_specs=pl.BlockSpec(memory_space=pl.ANY),
        scratch_shapes=[pltpu.SemaphoreType.DMA(())],
        input_output_aliases={2: 0},
        compiler_params=_cparams("arbitrary"),
    )(dest_flat, x, xs0)


def _expert_kernel(blk_e_ref, nused_ref, xs_ref, wg_ref, wu_ref, wd_ref, ys_ref, acc_ref):
    del blk_e_ref
    i = pl.program_id(0)
    f = pl.program_id(1)
    last = pl.num_programs(1) - 1
    used = i < nused_ref[0]

    @pl.when(used & (f == 0))
    def _():
        acc_ref[...] = jnp.zeros_like(acc_ref)

    @pl.when(used)
    def _():
        xb = xs_ref[...].astype(BF16)
        g = jnp.dot(xb, wg_ref[0], preferred_element_type=F32)
        u = jnp.dot(xb, wu_ref[0], preferred_element_type=F32)
        acc_ref[...] += _bdot(g * _sigmoid(g) * u, wd_ref[0])

    @pl.when(used & (f == last))
    def _():
        ys_ref[...] = acc_ref[...]

    @pl.when(jnp.logical_not(used) & (f == last))
    def _():
        ys_ref[...] = jnp.zeros_like(ys_ref)


def _experts(xs, blk_e, n_used, wg, wu, wd):
    n_slots = xs.shape[0]
    n_blocks = n_slots // MOE_ROWS
    ff = wg.shape[2]
    tf = ff // 2
    grid_spec = pltpu.PrefetchScalarGridSpec(
        num_scalar_prefetch=2,
        grid=(n_blocks, ff // tf),
        in_specs=[pl.BlockSpec((MOE_ROWS, D_MODEL), lambda i, f, be, nu: (i, 0)),
                  pl.BlockSpec((1, D_MODEL, tf), lambda i, f, be, nu: (be[i], 0, f)),
                  pl.BlockSpec((1, D_MODEL, tf), lambda i, f, be, nu: (be[i], 0, f)),
                  pl.BlockSpec((1, tf, D_MODEL), lambda i, f, be, nu: (be[i], f, 0))],
        out_specs=pl.BlockSpec((MOE_ROWS, D_MODEL), lambda i, f, be, nu: (i, 0)),
        scratch_shapes=[pltpu.VMEM((MOE_ROWS, D_MODEL), F32)],
    )
    return pl.pallas_call(
        _expert_kernel,
        out_shape=jax.ShapeDtypeStruct((n_slots, D_MODEL), F32),
        grid_spec=grid_spec,
        compiler_params=_cparams("arbitrary", "arbitrary"),
    )(blk_e, n_used, xs, wg, wu, wd)


def _combine_kernel(dest_ref, x_ref, wts_ref, ys_ref, lg_ref, lb_ref, xo_ref, xbo_ref, buf_ref, sem):
    tm = x_ref.shape[0]

    def row_copy(r, j):
        return pltpu.make_async_copy(ys_ref.at[pl.ds(dest_ref[TOP_K * r + j], 1)],
                                     buf_ref.at[j, pl.ds(r, 1)], sem)

    def start(r, c):
        for j in range(TOP_K):
            row_copy(r, j).start()
        return c

    def wait(r, c):
        for j in range(TOP_K):
            row_copy(r, j).wait()
        return c

    lax.fori_loop(0, tm, start, 0)
    lax.fori_loop(0, tm, wait, 0)
    w = wts_ref[...]
    f = w[:, 0:1] * buf_ref[0] + w[:, 1:2] * buf_ref[1]
    z = DEEPNORM_ALPHA * x_ref[...] + f
    out = _layer_norm_rows(z, lg_ref[...], lb_ref[...], LN_EPS)
    xo_ref[...] = out
    xbo_ref[...] = out.astype(BF16)


def _combine(x, dest_flat, wts, ys, ln_g, ln_b):
    t = x.shape[0]
    tm = min(256, t)
    rows = lambda width: pl.BlockSpec((tm, width), lambda i: (i, 0))
    vec = pl.BlockSpec((1, D_MODEL), lambda i: (0, 0))
    return pl.pallas_call(
        _combine_kernel,
        out_shape=[jax.ShapeDtypeStruct((t, D_MODEL), F32), jax.ShapeDtypeStruct((t, D_MODEL), BF16)],
        grid=(t // tm,),
        in_specs=[pl.BlockSpec((tm * TOP_K,), lambda i: (i,), memory_space=pltpu.SMEM),
                  rows(D_MODEL), rows(LANES), pl.BlockSpec(memory_space=pl.ANY), vec, vec],
        out_specs=[rows(D_MODEL), rows(D_MODEL)],
        scratch_shapes=[pltpu.VMEM((TOP_K, tm, D_MODEL), F32), pltpu.SemaphoreType.DMA(())],
        compiler_params=_cparams("arbitrary"),
    )(dest_flat, x, wts, ys, ln_g, ln_b)


def _moe_ffn(x, w_router, wg, wu, wd, ln_g, ln_b):
    t = x.shape[0]
    wr_pad = jnp.pad(w_router, ((0, 0), (0, LANES - N_EXPERTS)))
    meta, wts, cnt = _router(x, wr_pad)
    counts = cnt[0, :N_EXPERTS].astype(I32)
    padded = (counts + MOE_ROWS - 1) // MOE_ROWS * MOE_ROWS
    seg_end = jnp.cumsum(padded)
    seg_start = seg_end - padded
    n_blocks = (t * TOP_K) // MOE_ROWS + N_EXPERTS
    dest = seg_start[meta[:, 0:TOP_K]] + meta[:, TOP_K:2 * TOP_K]
    dest_flat = dest.reshape(-1).astype(I32)
    blk_start = jnp.arange(n_blocks, dtype=I32) * MOE_ROWS
    blk_e = jnp.minimum(jnp.sum(blk_start[:, None] >= seg_end[None, :], axis=1), N_EXPERTS - 1).astype(I32)
    n_used = (seg_end[-1:] // MOE_ROWS).astype(I32)
    xs = _dispatch(x, dest_flat, n_blocks * MOE_ROWS)
    ys = _experts(xs, blk_e, n_used, wg, wu, wd)
    return _combine(x, dest_flat, wts, ys, ln_g, ln_b)


def _pad_rows(w, rows, at=0):
    out = jnp.zeros((rows, w.shape[1]), w.dtype)
    return out.at[at:at + w.shape[0]].set(w)


def _reorder_in_projection(w, b):
    sizes = (GLA_KW, GLA_KW, GLA_VW, GLA_VW, GLA_DECAY_RANK, LRU_WIDTH, LRU_WIDTH,
             3 * RWKV_WIDTH, RWKV_DECAY_RANK, RWKV_A_RANK, RWKV_GATE_RANK, N_BRANCH * D_MODEL)
    offs = [0]
    for s in sizes:
        offs.append(offs[-1] + s)
    wb = jnp.concatenate([w, b[None, :]], axis=0)
    piece = lambda i: wb[:, offs[i]:offs[i + 1]]
    zeros = lambda n: jnp.zeros((wb.shape[0], n), wb.dtype)
    q, k, v, r, dec, lx, lg, rkv, wl, al, gl, gates = (piece(i) for i in range(len(sizes)))
    small = jnp.concatenate([wl, al, gl, zeros(SM_GL[1] - RWKV_GATE_RANK),
                             dec, zeros(SM_DEC[1] - GLA_DECAY_RANK)], axis=1)
    out = jnp.concatenate([gates, rkv, v, r, lx, lg, q, k, small], axis=1)
    return out[:-1].astype(BF16), out[-1:]


def _block_diag(blocks):
    n, bi, bo = blocks.shape
    eye = jnp.eye(n, dtype=blocks.dtype)
    return (eye[:, None, :, None] * blocks[:, :, None, :]).reshape(n * bi, n * bo)


def kernel(x, w_in, b_in, gla_w_decay_up, gla_b_decay, gla_norm_g, gla_norm_b, lru_conv_w, lru_conv_b, lru_w_r, lru_b_r, lru_w_i, lru_b_i, lru_lambda, rwkv_mu, rwkv_w0, rwkv_w2, rwkv_a0, rwkv_a2, rwkv_g2, rwkv_k_k, rwkv_k_a, rwkv_r_k, rwkv_lnx_g, rwkv_lnx_b, p_gla, p_lru, p_rwkv, w_out, ln_mix_g, ln_mix_b, ffn_w_gate, ffn_w_up, ffn_w_down, moe_w_router, moe_w_gate, moe_w_up, moe_w_down, ln_ffn_g, ln_ffn_b):
    batch, seq, d = x.shape
    t = batch * seq
    xf = x.reshape(t, d)
    xb = xf.astype(BF16)
    row = lambda a: a.reshape(1, -1)
    head_ones = _block_diag(jnp.ones((RWKV_HEADS, RWKV_HEAD, RWKV_HEAD), F32))
    for l in range(DEPTH):
        w_l, b_l = _reorder_in_projection(w_in[l], b_in[l])
        h = _in_projection(xb, w_l, b_l)

        wup = _pad_rows(gla_w_decay_up[l], SM_DEC[1]).astype(BF16)
        o_gla = _gla_branch(h, batch, seq, wup, row(gla_b_decay[l]), row(gla_norm_g[l]), row(gla_norm_b[l]))

        w_ri = jnp.concatenate([_block_diag(lru_w_r[l]), _block_diag(lru_w_i[l])], axis=1).astype(BF16)
        b_ri = jnp.concatenate([lru_b_r[l], lru_b_i[l]])[None, :]
        o_lru = _lru_branch(h, batch, seq, lru_conv_w[l], row(lru_conv_b[l]), w_ri, b_ri,
                            row(jax.nn.softplus(-lru_lambda[l])))

        mu = rwkv_mu[l]
        nr = 3 * RWKV_WIDTH
        mu_small = jnp.zeros((COL_SMALL[1],), F32).at[0:mu.shape[0] - nr].set(mu[nr:])
        prep_params = (row(mu[:nr]), row(mu_small),
                       _pad_rows(rwkv_w2[l], SM_WA[1], 0).astype(BF16),
                       _pad_rows(rwkv_a2[l], SM_WA[1], RWKV_DECAY_RANK).astype(BF16),
                       _pad_rows(rwkv_g2[l], SM_GL[1], 0).astype(BF16),
                       row(rwkv_w0[l]), row(rwkv_a0[l]), row(rwkv_k_k[l]), row(rwkv_k_a[l]),
                       row(rwkv_r_k[l]), head_ones)
        prep = _rwkv_prep(h, batch, seq, prep_params)
        o_rwkv = _rwkv_scan(prep, batch, seq, row(rwkv_lnx_g[l]), row(rwkv_lnx_b[l]))

        xf, xb = _merge(h, o_gla, o_lru, o_rwkv, xf, p_gla[l].astype(BF16), p_lru[l].astype(BF16),
                        p_rwkv[l].astype(BF16), w_out[l].astype(BF16), row(ln_mix_g[l]), row(ln_mix_b[l]))
        i = l // 2
        if l % 2 == 0:
            xf, xb = _dense_ffn(xb, xf, ffn_w_gate[i].astype(BF16), ffn_w_up[i].astype(BF16),
                                ffn_w_down[i].astype(BF16), row(ln_ffn_g[l]), row(ln_ffn_b[l]))
        else:
            xf, xb = _moe_ffn(xf, moe_w_router[i], moe_w_gate[i].astype(BF16), moe_w_up[i].astype(BF16),
                              moe_w_down[i].astype(BF16), row(ln_ffn_g[l]), row(ln_ffn_b[l]))
    return xf.reshape(batch, seq, d)
```

```python
import functools

import jax
import jax.numpy as jnp
from jax import lax
from jax.experimental import pallas as pl
from jax.experimental.pallas import tpu as pltpu

F32 = jnp.float32
BF16 = jnp.bfloat16
I32 = jnp.int32
HIGHEST = lax.Precision.HIGHEST

D_MODEL = 1024
DEPTH = 2
GLA_HEADS, GLA_DK, GLA_DV = 4, 64, 128
GLA_KW, GLA_VW = GLA_HEADS * GLA_DK, GLA_HEADS * GLA_DV
GLA_DECAY_RANK = 16
GLA_TEMP = 16.0
LRU_WIDTH, LRU_BLOCKS, LRU_CONV, LRU_C = 512, 8, 4, 8.0
RWKV_HEAD, RWKV_WIDTH = 64, 512
RWKV_HEADS = RWKV_WIDTH // RWKV_HEAD
RWKV_DECAY_RANK, RWKV_A_RANK, RWKV_GATE_RANK = 64, 64, 160
RWKV_LNX_EPS = 64e-5
N_BRANCH = 3
N_EXPERTS, TOP_K = 8, 2
DEEPNORM_ALPHA = (2 * DEPTH) ** 0.25
LN_EPS = 1e-5

LANES = 128
SUBLANES = 8
VMEM_LIMIT_BYTES = 56 * 1024 * 1024

COL_GATES = (0, 3072)
COL_RW_RKV = (3072, 1536)
COL_GLA_V = (4608, 512)
COL_GLA_R = (5120, 512)
COL_LRU_X = (5632, 512)
COL_LRU_G = (6144, 512)
COL_GLA_Q = (6656, 256)
COL_GLA_K = (6912, 256)
COL_SMALL = (7168, 512)
N_IN_PAD = 7680
SM_WA = (0, 128)
SM_GL = (128, 256)
SM_DEC = (384, 128)

CHUNK = 64
INV_BLOCK = 16
MOE_ROWS = 512


def _cparams(*sem):
    return pltpu.CompilerParams(dimension_semantics=sem, vmem_limit_bytes=VMEM_LIMIT_BYTES)


def _sigmoid(x):
    return 1.0 / (1.0 + jnp.exp(-x))


def _softplus(x):
    return jnp.maximum(x, 0.0) + jnp.log1p(jnp.exp(-jnp.abs(x)))


def _bdot(a, b):
    return jnp.dot(a.astype(BF16), b.astype(BF16), preferred_element_type=F32)


def _bdot_nt(a, b):
    return lax.dot_general(a.astype(BF16), b.astype(BF16), (((1,), (1,)), ((), ())),
                           preferred_element_type=F32)


def _bdot_tn(a, b):
    return lax.dot_general(a.astype(BF16), b.astype(BF16), (((0,), (0,)), ((), ())),
                           preferred_element_type=F32)


def _fdot(a, b):
    return jnp.dot(a, b, precision=HIGHEST, preferred_element_type=F32)


def _split3(x):
    hi = x.astype(BF16)
    rest = x - hi.astype(F32)
    mid = rest.astype(BF16)
    lo = (rest - mid.astype(F32)).astype(BF16)
    return hi, mid, lo


def _mask_dot(mask_bf16, x):
    hi, mid, lo = _split3(x)
    dot = lambda part: jnp.dot(mask_bf16, part, preferred_element_type=F32)
    return dot(lo) + dot(mid) + dot(hi)


def _dot_mask(x, mask_bf16):
    hi, mid, lo = _split3(x)
    dot = lambda part: jnp.dot(part, mask_bf16, preferred_element_type=F32)
    return dot(lo) + dot(mid) + dot(hi)


def _layer_norm_rows(z, g, b, eps):
    mu = jnp.mean(z, axis=-1, keepdims=True)
    zc = z - mu
    var = jnp.mean(zc * zc, axis=-1, keepdims=True)
    return zc * lax.rsqrt(var + eps) * g + b


def _tri_masks(n):
    row = lax.broadcasted_iota(I32, (n, n), 0)
    col = lax.broadcasted_iota(I32, (n, n), 1)
    return row, col


def _proj_kernel(x_ref, w_ref, b_ref, o_ref):
    o_ref[...] = jnp.dot(x_ref[...], w_ref[...], preferred_element_type=F32) + b_ref[...]


def _in_projection(xb, w, b):
    t, d = xb.shape
    n = w.shape[1]
    tm = min(1024, t)
    tn = 1536
    return pl.pallas_call(
        _proj_kernel,
        out_shape=jax.ShapeDtypeStruct((t, n), F32),
        grid=(t // tm, n // tn),
        in_specs=[pl.BlockSpec((tm, d), lambda i, j: (i, 0)),
                  pl.BlockSpec((d, tn), lambda i, j: (0, j)),
                  pl.BlockSpec((1, tn), lambda i, j: (0, j))],
        out_specs=pl.BlockSpec((tm, tn), lambda i, j: (i, j)),
        compiler_params=_cparams("parallel", "arbitrary"),
    )(xb, w, b)


def _gla_kernel(q_ref, k_ref, v_ref, r_ref, sm_ref, wup_ref, bdec_ref, ng_ref, nb_ref,
                o_ref, st_ref, *, chunks):
    @pl.when(pl.program_id(1) == 0)
    def _():
        st_ref[...] = jnp.zeros_like(st_ref)

    row, col = _tri_masks(CHUNK)
    causal = row >= col
    ltri = causal.astype(BF16)
    for c in range(chunks):
        rows = pl.ds(c * CHUNK, CHUNK)
        dec = sm_ref[rows, SM_DEC[0]:SM_DEC[0] + SM_DEC[1]]
        logits = _bdot(dec, wup_ref[...]) + bdec_ref[...]
        log_a = -_softplus(-logits) * (1.0 / GLA_TEMP)
        cum = _mask_dot(ltri, log_a)
        cum_end = cum[CHUNK - 1:CHUNK, :]
        q = q_ref[rows, :] * (GLA_DK ** -0.5)
        k = k_ref[rows, :]
        q_dec = (q * jnp.exp(cum)).astype(BF16)
        k_inv = (k * jnp.exp(-cum)).astype(BF16)
        k_end = (k * jnp.exp(cum_end - cum)).astype(BF16)
        decay_end = jnp.exp(cum_end)
        v = v_ref[rows, :].astype(BF16)
        outs = []
        for h in range(GLA_HEADS):
            ks = slice(h * GLA_DK, (h + 1) * GLA_DK)
            vs = slice(h * GLA_DV, (h + 1) * GLA_DV)
            qh, kih, keh, vh = q_dec[:, ks], k_inv[:, ks], k_end[:, ks], v[:, vs]
            scores = jnp.where(causal, _bdot_nt(qh, kih), 0.0)
            state = st_ref[h]
            o = _bdot(scores, vh) + _bdot_nt(qh, state)
            st_ref[h] = state * decay_end[:, ks] + _bdot_tn(vh, keh)
            mu = jnp.mean(o, axis=-1, keepdims=True)
            oc = o - mu
            var = jnp.mean(oc * oc, axis=-1, keepdims=True)
            outs.append(oc * lax.rsqrt(var + LN_EPS))
        y = jnp.concatenate(outs, axis=1) * ng_ref[...] + nb_ref[...]
        r = r_ref[rows, :]
        o_ref[rows, :] = (y * (r * _sigmoid(r))).astype(o_ref.dtype)


def _gla_branch(h, batch, seq, wup_pad, b_decay, norm_g, norm_b):
    t = batch * seq
    chunks = 4
    rows = CHUNK * chunks
    nblk = seq // rows

    def col_spec(col):
        off, width = col
        return pl.BlockSpec((rows, width), lambda b, n: (b * nblk + n, off // width))

    def full(a):
        return pl.BlockSpec(a.shape, lambda b, n: (0,) * a.ndim)

    args = (wup_pad, b_decay, norm_g, norm_b)
    return pl.pallas_call(
        functools.partial(_gla_kernel, chunks=chunks),
        out_shape=jax.ShapeDtypeStruct((t, GLA_VW), BF16),
        grid=(batch, nblk),
        in_specs=[col_spec(COL_GLA_Q), col_spec(COL_GLA_K), col_spec(COL_GLA_V),
                  col_spec(COL_GLA_R), col_spec(COL_SMALL)] + [full(a) for a in args],
        out_specs=pl.BlockSpec((rows, GLA_VW), lambda b, n: (b * nblk + n, 0)),
        scratch_shapes=[pltpu.VMEM((GLA_HEADS, GLA_DV, GLA_DK), F32)],
        compiler_params=_cparams("parallel", "arbitrary"),
    )(h, h, h, h, h, *args)


def _lru_kernel(x_ref, g_ref, cw_ref, cb_ref, wri_ref, bri_ref, sp_ref,
                o_ref, xbuf_ref, a_ref, u_ref, hs_ref, h_ref):
    tm = x_ref.shape[0]
    pad = SUBLANES

    @pl.when(pl.program_id(1) == 0)
    def _():
        xbuf_ref[0:pad, :] = jnp.zeros((pad, LRU_WIDTH), F32)
        h_ref[...] = jnp.zeros_like(h_ref)

    x = x_ref[...]
    xbuf_ref[pad:pad + tm, :] = x
    xc = cb_ref[...] + x * cw_ref[LRU_CONV - 1:LRU_CONV, :]
    for j in range(LRU_CONV - 1):
        back = LRU_CONV - 1 - j
        xc = xc + xbuf_ref[pl.ds(pad - back, tm), :] * cw_ref[j:j + 1, :]
    xbuf_ref[0:pad, :] = x[tm - pad:tm, :]

    ri = _sigmoid(_bdot(xc, wri_ref[...]) + bri_ref[...])
    rg, ig = ri[:, :LRU_WIDTH], ri[:, LRU_WIDTH:]
    log_a = -LRU_C * rg * sp_ref[...]
    a_ref[...] = jnp.exp(log_a)
    u_ref[...] = jnp.sqrt(1.0 - jnp.exp(2.0 * log_a)) * (ig * xc)

    def step(t, hprev):
        hnew = a_ref[pl.ds(t, 1), :] * hprev + u_ref[pl.ds(t, 1), :]
        hs_ref[pl.ds(t, 1), :] = hnew
        return hnew

    h_ref[...] = lax.fori_loop(0, tm, step, h_ref[...], unroll=8)
    o_ref[...] = (jax.nn.gelu(g_ref[...]) * hs_ref[...]).astype(o_ref.dtype)


def _lru_branch(h, batch, seq, conv_w, conv_b, w_ri, b_ri, softplus_neg_lam):
    t = batch * seq
    tm = min(512, seq)
    nblk = seq // tm

    def col_spec(col):
        off, width = col
        return pl.BlockSpec((tm, width), lambda b, n: (b * nblk + n, off // width))

    def full(a):
        return pl.BlockSpec(a.shape, lambda b, n: (0,) * a.ndim)

    args = (conv_w, conv_b, w_ri, b_ri, softplus_neg_lam)
    return pl.pallas_call(
        _lru_kernel,
        out_shape=jax.ShapeDtypeStruct((t, LRU_WIDTH), BF16),
        grid=(batch, nblk),
        in_specs=[col_spec(COL_LRU_X), col_spec(COL_LRU_G)] + [full(a) for a in args],
        out_specs=pl.BlockSpec((tm, LRU_WIDTH), lambda b, n: (b * nblk + n, 0)),
        scratch_shapes=[pltpu.VMEM((tm + SUBLANES, LRU_WIDTH), F32),
                        pltpu.VMEM((tm, LRU_WIDTH), F32),
                        pltpu.VMEM((tm, LRU_WIDTH), F32),
                        pltpu.VMEM((tm, LRU_WIDTH), F32),
                        pltpu.VMEM((1, LRU_WIDTH), F32)],
        compiler_params=_cparams("parallel", "arbitrary"),
    )(h, h, *args)


def _rwkv_prep_kernel(p_ref, pprev_ref, sm_ref, smprev_ref, mu_ref, musm_ref, w2_ref, a2_ref, g2_ref,
                      w0_ref, a0_ref, kk_ref, ka_ref, rk_ref, bd_ref,
                      r_out, k_out, v_out, lw_out, kk_out, kka_out, bonus_out, g_out, *, blocks_per_seq):
    tm = p_ref.shape[0]
    first = (pl.program_id(0) % blocks_per_seq) == 0
    keep = jnp.where(first, 0.0, 1.0)
    row0 = lax.broadcasted_iota(I32, (tm, 1), 0) == 0

    def token_shift(cur_ref, prev_ref, m_ref):
        cur = cur_ref[...]
        prev_row = prev_ref[SUBLANES - 1:SUBLANES, :] * keep
        prev = jnp.where(row0, prev_row, pltpu.roll(cur, 1, axis=0))
        return cur + (prev - cur) * m_ref[...]

    p = token_shift(p_ref, pprev_ref, mu_ref)
    sm = token_shift(sm_ref, smprev_ref, musm_ref)
    r = p[:, 0:RWKV_WIDTH]
    k = p[:, RWKV_WIDTH:2 * RWKV_WIDTH]
    v = p[:, 2 * RWKV_WIDTH:3 * RWKV_WIDTH]
    wa = sm[:, SM_WA[0]:SM_WA[0] + SM_WA[1]]
    gl = sm[:, SM_GL[0]:SM_GL[0] + SM_GL[1]]

    w_log = -_softplus(-(w0_ref[...] + _bdot(jnp.tanh(wa), w2_ref[...]))) - 0.5
    lw_out[...] = -jnp.exp(w_log)
    a = _sigmoid(a0_ref[...] + _bdot(wa, a2_ref[...]))
    g_out[...] = _bdot(_sigmoid(gl), g2_ref[...])

    kk = k * kk_ref[...]
    norm = jnp.sqrt(_dot_mask(kk * kk, bd_ref[...]))
    kk = kk / jnp.maximum(norm, 1e-12)
    k2 = k * (1.0 + (a - 1.0) * ka_ref[...])
    bonus_out[...] = _dot_mask(r * k2 * rk_ref[...], bd_ref[...]) * v
    r_out[...] = r
    k_out[...] = k2
    v_out[...] = v
    kk_out[...] = kk
    kka_out[...] = kk * a


def _rwkv_prep(h, batch, seq, params):
    t = batch * seq
    tm = min(512, seq)
    bps = seq // tm
    sub_per_tile = tm // SUBLANES

    def cur_spec(col):
        off, width = col
        return pl.BlockSpec((tm, width), lambda i: (i, off // width))

    def prev_spec(col):
        off, width = col
        return pl.BlockSpec((SUBLANES, width),
                            lambda i: (jnp.maximum(i * sub_per_tile - 1, 0), off // width))

    def full(a):
        return pl.BlockSpec(a.shape, lambda i: (0,) * a.ndim)

    out = jax.ShapeDtypeStruct((t, RWKV_WIDTH), F32)
    ospec = pl.BlockSpec((tm, RWKV_WIDTH), lambda i: (i, 0))
    return pl.pallas_call(
        functools.partial(_rwkv_prep_kernel, blocks_per_seq=bps),
        out_shape=[out] * 8,
        grid=(t // tm,),
        in_specs=[cur_spec(COL_RW_RKV), prev_spec(COL_RW_RKV), cur_spec(COL_SMALL), prev_spec(COL_SMALL)]
        + [full(a) for a in params],
        out_specs=[ospec] * 8,
        compiler_params=_cparams("parallel"),
    )(h, h, h, h, *params)


def _unit_lower_inverses(mats, row, col):
    eye = (row == col).astype(F32)
    same_block = jnp.bitwise_xor(row, col) < INV_BLOCK
    d = [jnp.where(same_block, a, 0.0) for a in mats]
    low = [a - di for a, di in zip(mats, d)]
    p = [eye + di for di in d]
    dpow = d
    for _ in range(INV_BLOCK.bit_length() - 2):
        dpow = [_bdot(x, x) for x in dpow]
        p = [pi + _bdot(pi, x) for pi, x in zip(p, dpow)]
    e = [_bdot(pi, li) for pi, li in zip(p, low)]
    q = [eye + ei for ei in e]
    epow = e
    for _ in range((CHUNK // INV_BLOCK).bit_length() - 2):
        epow = [_bdot(x, x) for x in epow]
        q = [qi + _bdot(qi, x) for qi, x in zip(q, epow)]
    return q, p


def _rwkv_scan_kernel(r_ref, k_ref, v_ref, lw_ref, kk_ref, kka_ref, bonus_ref, g_ref, lg_ref, lb_ref,
                      o_ref, st_ref, *, chunks):
    @pl.when(pl.program_id(1) == 0)
    def _():
        st_ref[...] = jnp.zeros_like(st_ref)

    row, col = _tri_masks(CHUNK)
    incl = row >= col
    strict = row > col
    ltri = incl.astype(BF16)
    heads = range(RWKV_HEADS)
    sl = [slice(h * RWKV_HEAD, (h + 1) * RWKV_HEAD) for h in heads]
    for c in range(chunks):
        rows = pl.ds(c * CHUNK, CHUNK)
        lw = lw_ref[rows, :]
        cum = _mask_dot(ltri, lw)
        cum_end = cum[CHUNK - 1:CHUNK, :]
        e_inv = jnp.exp(-cum)
        e_end = jnp.exp(cum_end - cum)
        kka = kka_ref[rows, :]
        k = k_ref[rows, :]
        a_bar = (-kk_ref[rows, :] * jnp.exp(cum - lw)).astype(BF16)
        r_bar = (r_ref[rows, :] * jnp.exp(cum)).astype(BF16)
        b_til = (kka * e_inv).astype(BF16)
        k_til = (k * e_inv).astype(BF16)
        b_end = (kka * e_end).astype(BF16)
        k_end = (k * e_end).astype(BF16)
        gamma = jnp.exp(cum_end)
        v = v_ref[rows, :].astype(BF16)

        lhs = [jnp.concatenate([a_bar[:, s], r_bar[:, s]], axis=0) for s in sl]
        rhs = [jnp.concatenate([b_til[:, s], k_til[:, s]], axis=0) for s in sl]
        vh = [v[:, s] for s in sl]
        state = [st_ref[h] for h in heads]
        blocks = [_bdot_nt(l, r) for l, r in zip(lhs, rhs)]
        from_state = [_bdot_nt(l, st) for l, st in zip(lhs, state)]
        a_ab = [jnp.where(strict, b[:CHUNK, :CHUNK], 0.0) for b in blocks]
        a_ak = [jnp.where(strict, b[:CHUNK, CHUNK:], 0.0) for b in blocks]
        r_ab = [jnp.where(incl, b[CHUNK:, :CHUNK], 0.0) for b in blocks]
        r_ak = [jnp.where(incl, b[CHUNK:, CHUNK:], 0.0) for b in blocks]
        qm, pm = _unit_lower_inverses(a_ab, row, col)
        u = [fs[:CHUNK] + _bdot(ak, x) for fs, ak, x in zip(from_state, a_ak, vh)]
        u = [_bdot(pi, x) for pi, x in zip(pm, u)]
        u = [_bdot(qi, x) for qi, x in zip(qm, u)]
        y = [fs[CHUNK:] + _bdot(rb, ui) + _bdot(rk, x)
             for fs, rb, ui, rk, x in zip(from_state, r_ab, u, r_ak, vh)]
        for h in heads:
            st_ref[h] = (state[h] * gamma[:, sl[h]] + _bdot_tn(u[h], b_end[:, sl[h]])
                         + _bdot_tn(vh[h], k_end[:, sl[h]]))
        outs = []
        for yh in y:
            mu = jnp.mean(yh, axis=-1, keepdims=True)
            yc = yh - mu
            var = jnp.mean(yc * yc, axis=-1, keepdims=True)
            outs.append(yc * lax.rsqrt(var + RWKV_LNX_EPS))
        yn = jnp.concatenate(outs, axis=1) * lg_ref[...] + lb_ref[...]
        o_ref[rows, :] = ((yn + bonus_ref[rows, :]) * g_ref[rows, :]).astype(o_ref.dtype)


def _rwkv_scan(prep, batch, seq, lnx_g, lnx_b):
    t = batch * seq
    chunks = 2
    rows = CHUNK * chunks
    nblk = seq // rows
    spec = pl.BlockSpec((rows, RWKV_WIDTH), lambda b, n: (b * nblk + n, 0))
    pspec = pl.BlockSpec((1, RWKV_WIDTH), lambda b, n: (0, 0))
    return pl.pallas_call(
        functools.partial(_rwkv_scan_kernel, chunks=chunks),
        out_shape=jax.ShapeDtypeStruct((t, RWKV_WIDTH), BF16),
        grid=(batch, nblk),
        in_specs=[spec] * 8 + [pspec, pspec],
        out_specs=spec,
        scratch_shapes=[pltpu.VMEM((RWKV_HEADS, RWKV_HEAD, RWKV_HEAD), F32)],
        compiler_params=_cparams("parallel", "arbitrary"),
    )(*prep, lnx_g, lnx_b)


def _merge_kernel(gt_ref, og_ref, ol_ref, or_ref, x_ref, pg_ref, pl_ref, pr_ref, wo_ref, lg_ref, lb_ref,
                  xo_ref, xb_ref):
    d = D_MODEL
    merged = (_sigmoid(gt_ref[:, 0:d]) * _bdot(og_ref[...], pg_ref[...])
              + _sigmoid(gt_ref[:, d:2 * d]) * _bdot(ol_ref[...], pl_ref[...])
              + _sigmoid(gt_ref[:, 2 * d:3 * d]) * _bdot(or_ref[...], pr_ref[...]))
    z = DEEPNORM_ALPHA * x_ref[...] + _bdot(merged, wo_ref[...])
    out = _layer_norm_rows(z, lg_ref[...], lb_ref[...], LN_EPS)
    xo_ref[...] = out
    xb_ref[...] = out.astype(BF16)


def _merge(h, o_gla, o_lru, o_rwkv, x, p_gla, p_lru, p_rwkv, w_out, ln_g, ln_b):
    t = x.shape[0]
    tm = min(512, t)

    def rows(width):
        return pl.BlockSpec((tm, width), lambda i: (i, 0))

    def full(a):
        return pl.BlockSpec(a.shape, lambda i: (0,) * a.ndim)

    params = (p_gla, p_lru, p_rwkv, w_out, ln_g, ln_b)
    return pl.pallas_call(
        _merge_kernel,
        out_shape=[jax.ShapeDtypeStruct((t, D_MODEL), F32), jax.ShapeDtypeStruct((t, D_MODEL), BF16)],
        grid=(t // tm,),
        in_specs=[rows(COL_GATES[1]), rows(GLA_VW), rows(LRU_WIDTH), rows(RWKV_WIDTH), rows(D_MODEL)]
        + [full(a) for a in params],
        out_specs=[rows(D_MODEL), rows(D_MODEL)],
        compiler_params=_cparams("parallel"),
    )(h, o_gla, o_lru, o_rwkv, x, *params)


def _ffn_kernel(xb_ref, x_ref, wg_ref, wu_ref, wd_ref, lg_ref, lb_ref, xo_ref, xbo_ref, acc_ref):
    f = pl.program_id(1)

    @pl.when(f == 0)
    def _():
        acc_ref[...] = jnp.zeros_like(acc_ref)

    xb = xb_ref[...]
    g = jnp.dot(xb, wg_ref[...], preferred_element_type=F32)
    u = jnp.dot(xb, wu_ref[...], preferred_element_type=F32)
    acc_ref[...] += _bdot(g * _sigmoid(g) * u, wd_ref[...])

    @pl.when(f == pl.num_programs(1) - 1)
    def _():
        z = DEEPNORM_ALPHA * x_ref[...] + acc_ref[...]
        out = _layer_norm_rows(z, lg_ref[...], lb_ref[...], LN_EPS)
        xo_ref[...] = out
        xbo_ref[...] = out.astype(BF16)


def _dense_ffn(xb, x, wg, wu, wd, ln_g, ln_b):
    t = x.shape[0]
    tm = min(512, t)
    ff = wg.shape[1]
    tf = ff // 2
    rows = lambda width: pl.BlockSpec((tm, width), lambda i, f: (i, 0))
    vec = pl.BlockSpec((1, D_MODEL), lambda i, f: (0, 0))
    return pl.pallas_call(
        _ffn_kernel,
        out_shape=[jax.ShapeDtypeStruct((t, D_MODEL), F32), jax.ShapeDtypeStruct((t, D_MODEL), BF16)],
        grid=(t // tm, ff // tf),
        in_specs=[rows(D_MODEL), rows(D_MODEL),
                  pl.BlockSpec((D_MODEL, tf), lambda i, f: (0, f)),
                  pl.BlockSpec((D_MODEL, tf), lambda i, f: (0, f)),
                  pl.BlockSpec((tf, D_MODEL), lambda i, f: (f, 0)), vec, vec],
        out_specs=[rows(D_MODEL), rows(D_MODEL)],
        scratch_shapes=[pltpu.VMEM((tm, D_MODEL), F32)],
        compiler_params=_cparams("parallel", "arbitrary"),
    )(xb, x, wg, wu, wd, ln_g, ln_b)


def _router_kernel(x_ref, wr_ref, meta_ref, wts_ref, cnt_ref, carry_ref):
    tm = x_ref.shape[0]

    @pl.when(pl.program_id(0) == 0)
    def _():
        carry_ref[...] = jnp.zeros_like(carry_ref)

    lane = lax.broadcasted_iota(I32, (tm, LANES), 1)
    neg = jnp.float32(-jnp.inf)
    logits = jnp.where(lane < N_EXPERTS, _fdot(x_ref[...], wr_ref[...]), neg)
    m1 = jnp.max(logits, axis=-1, keepdims=True)
    e1 = jnp.min(jnp.where(logits == m1, lane, LANES), axis=-1, keepdims=True)
    rest = jnp.where(lane == e1, neg, logits)
    m2 = jnp.max(rest, axis=-1, keepdims=True)
    e2 = jnp.min(jnp.where(rest == m2, lane, LANES), axis=-1, keepdims=True)
    ex = jnp.exp(m2 - m1)
    w1 = 1.0 / (1.0 + ex)
    w2 = ex / (1.0 + ex)

    hot1 = lane == e1
    hot2 = lane == e2
    onehot = jnp.where(hot1 | hot2, 1.0, 0.0)
    row, col = _tri_masks(tm)
    before = (row > col).astype(BF16)
    prefix = jnp.dot(before, onehot.astype(BF16), preferred_element_type=F32) + carry_ref[...]
    rank1 = jnp.sum(jnp.where(hot1, prefix, 0.0), axis=-1, keepdims=True).astype(I32)
    rank2 = jnp.sum(jnp.where(hot2, prefix, 0.0), axis=-1, keepdims=True).astype(I32)
    carry_ref[...] += jnp.sum(onehot, axis=0, keepdims=True)
    cnt_ref[...] = carry_ref[...]

    meta = jnp.where(lane == 0, e1, jnp.where(lane == 1, e2, jnp.where(lane == 2, rank1, rank2)))
    meta_ref[...] = jnp.where(lane < 4, meta, 0)
    wts_ref[...] = jnp.where(lane == 0, w1, jnp.where(lane == 1, w2, 0.0))


def _router(x, wr_pad):
    t = x.shape[0]
    tm = min(512, t)
    return pl.pallas_call(
        _router_kernel,
        out_shape=[jax.ShapeDtypeStruct((t, LANES), I32), jax.ShapeDtypeStruct((t, LANES), F32),
                   jax.ShapeDtypeStruct((1, LANES), F32)],
        grid=(t // tm,),
        in_specs=[pl.BlockSpec((tm, D_MODEL), lambda i: (i, 0)),
                  pl.BlockSpec((D_MODEL, LANES), lambda i: (0, 0))],
        out_specs=[pl.BlockSpec((tm, LANES), lambda i: (i, 0)), pl.BlockSpec((tm, LANES), lambda i: (i, 0)),
                   pl.BlockSpec((1, LANES), lambda i: (0, 0))],
        scratch_shapes=[pltpu.VMEM((1, LANES), F32)],
        compiler_params=_cparams("arbitrary"),
    )(x, wr_pad)


def _dispatch_kernel(dest_ref, x_ref, xs_in_ref, xs_ref, sem):
    del xs_in_ref
    tm = x_ref.shape[0]

    def row_copy(r, j):
        return pltpu.make_async_copy(x_ref.at[pl.ds(r, 1)], xs_ref.at[pl.ds(dest_ref[TOP_K * r + j], 1)], sem)

    def start(r, c):
        for j in range(TOP_K):
            row_copy(r, j).start()
        return c

    def wait(r, c):
        for j in range(TOP_K):
            row_copy(r, j).wait()
        return c

    lax.fori_loop(0, tm, start, 0)
    lax.fori_loop(0, tm, wait, 0)


def _dispatch(x, dest_flat, n_slots):
    t = x.shape[0]
    tm = min(256, t)
    xs0 = jnp.zeros((n_slots, D_MODEL), F32)
    return pl.pallas_call(
        _dispatch_kernel,
        out_shape=jax.ShapeDtypeStruct((n_slots, D_MODEL), F32),
        grid=(t // tm,),
        in_specs=[pl.BlockSpec((tm * TOP_K,), lambda i: (i,), memory_space=pltpu.SMEM),
                  pl.BlockSpec((tm, D_MODEL), lambda i: (i, 0)),
                  pl.BlockSpec(memory_space=pl.ANY)],
        out_specs=pl.BlockSpec(memory_space=pl.ANY),
        scratch_shapes=[pltpu.SemaphoreType.DMA(())],
        input_output_aliases={2: 0},
        compiler_params=_cparams("arbitrary"),
    )(dest_flat, x, xs0)


def _expert_kernel(blk_e_ref, nused_ref, xs_ref, wg_ref, wu_ref, wd_ref, ys_ref, acc_ref):
    del blk_e_ref
    i = pl.program_id(0)
    f = pl.program_id(1)
    last = pl.num_programs(1) - 1
    used = i < nused_ref[0]

    @pl.when(used & (f == 0))
    def _():
        acc_ref[...] = jnp.zeros_like(acc_ref)

    @pl.when(used)
    def _():
        xb = xs_ref[...].astype(BF16)
        g = jnp.dot(xb, wg_ref[0], preferred_element_type=F32)
        u = jnp.dot(xb, wu_ref[0], preferred_element_type=F32)
        acc_ref[...] += _bdot(g * _sigmoid(g) * u, wd_ref[0])

    @pl.when(used & (f == last))
    def _():
        ys_ref[...] = acc_ref[...]

    @pl.when(jnp.logical_not(used) & (f == last))
    def _():
        ys_ref[...] = jnp.zeros_like(ys_ref)


def _experts(xs, blk_e, n_used, wg, wu, wd):
    n_slots = xs.shape[0]
    n_blocks = n_slots // MOE_ROWS
    ff = wg.shape[2]
    tf = ff // 2
    grid_spec = pltpu.PrefetchScalarGridSpec(
        num_scalar_prefetch=2,
        grid=(n_blocks, ff // tf),
        in_specs=[pl.BlockSpec((MOE_ROWS, D_MODEL), lambda i, f, be, nu: (i, 0)),
                  pl.BlockSpec((1, D_MODEL, tf), lambda i, f, be, nu: (be[i], 0, f)),
                  pl.BlockSpec((1, D_MODEL, tf), lambda i, f, be, nu: (be[i], 0, f)),
                  pl.BlockSpec((1, tf, D_MODEL), lambda i, f, be, nu: (be[i], f, 0))],
        out_specs=pl.BlockSpec((MOE_ROWS, D_MODEL), lambda i, f, be, nu: (i, 0)),
        scratch_shapes=[pltpu.VMEM((MOE_ROWS, D_MODEL), F32)],
    )
    return pl.pallas_call(
        _expert_kernel,
        out_shape=jax.ShapeDtypeStruct((n_slots, D_MODEL), F32),
        grid_spec=grid_spec,
        compiler_params=_cparams("arbitrary", "arbitrary"),
    )(blk_e, n_used, xs, wg, wu, wd)


def _combine_kernel(dest_ref, x_ref, wts_ref, ys_ref, lg_ref, lb_ref, xo_ref, xbo_ref, buf_ref, sem):
    tm = x_ref.shape[0]

    def row_copy(r, j):
        return pltpu.make_async_copy(ys_ref.at[pl.ds(dest_ref[TOP_K * r + j], 1)],
                                     buf_ref.at[j, pl.ds(r, 1)], sem)

    def start(r, c):
        for j in range(TOP_K):
            row_copy(r, j).start()
        return c

    def wait(r, c):
        for j in range(TOP_K):
            row_copy(r, j).wait()
        return c

    lax.fori_loop(0, tm, start, 0)
    lax.fori_loop(0, tm, wait, 0)
    w = wts_ref[...]
    f = w[:, 0:1] * buf_ref[0] + w[:, 1:2] * buf_ref[1]
    z = DEEPNORM_ALPHA * x_ref[...] + f
    out = _layer_norm_rows(z, lg_ref[...], lb_ref[...], LN_EPS)
    xo_ref[...] = out
    xbo_ref[...] = out.astype(BF16)


def _combine(x, dest_flat, wts, ys, ln_g, ln_b):
    t = x.shape[0]
    tm = min(256, t)
    rows = lambda width: pl.BlockSpec((tm, width), lambda i: (i, 0))
    vec = pl.BlockSpec((1, D_MODEL), lambda i: (0, 0))
    return pl.pallas_call(
        _combine_kernel,
        out_shape=[jax.ShapeDtypeStruct((t, D_MODEL), F32), jax.ShapeDtypeStruct((t, D_MODEL), BF16)],
        grid=(t // tm,),
        in_specs=[pl.BlockSpec((tm * TOP_K,), lambda i: (i,), memory_space=pltpu.SMEM),
                  rows(D_MODEL), rows(LANES), pl.BlockSpec(memory_space=pl.ANY), vec, vec],
        out_specs=[rows(D_MODEL), rows(D_MODEL)],
        scratch_shapes=[pltpu.VMEM((TOP_K, tm, D_MODEL), F32), pltpu.SemaphoreType.DMA(())],
        compiler_params=_cparams("arbitrary"),
    )(dest_flat, x, wts, ys, ln_g, ln_b)


def _moe_ffn(x, w_router, wg, wu, wd, ln_g, ln_b):
    t = x.shape[0]
    wr_pad = jnp.pad(w_router, ((0, 0), (0, LANES - N_EXPERTS)))
    meta, wts, cnt = _router(x, wr_pad)
    counts = cnt[0, :N_EXPERTS].astype(I32)
    padded = (counts + MOE_ROWS - 1) // MOE_ROWS * MOE_ROWS
    seg_end = jnp.cumsum(padded)
    seg_start = seg_end - padded
    n_blocks = (t * TOP_K) // MOE_ROWS + N_EXPERTS
    dest = seg_start[meta[:, 0:TOP_K]] + meta[:, TOP_K:2 * TOP_K]
    dest_flat = dest.reshape(-1).astype(I32)
    blk_start = jnp.arange(n_blocks, dtype=I32) * MOE_ROWS
    blk_e = jnp.minimum(jnp.sum(blk_start[:, None] >= seg_end[None, :], axis=1), N_EXPERTS - 1).astype(I32)
    n_used = (seg_end[-1:] // MOE_ROWS).astype(I32)
    xs = _dispatch(x, dest_flat, n_blocks * MOE_ROWS)
    ys = _experts(xs, blk_e, n_used, wg, wu, wd)
    return _combine(x, dest_flat, wts, ys, ln_g, ln_b)


def _pad_rows(w, rows, at=0):
    out = jnp.zeros((rows, w.shape[1]), w.dtype)
    return out.at[at:at + w.shape[0]].set(w)


def _reorder_in_projection(w, b):
    sizes = (GLA_KW, GLA_KW, GLA_VW, GLA_VW, GLA_DECAY_RANK, LRU_WIDTH, LRU_WIDTH,
             3 * RWKV_WIDTH, RWKV_DECAY_RANK, RWKV_A_RANK, RWKV_GATE_RANK, N_BRANCH * D_MODEL)
    offs = [0]
    for s in sizes:
        offs.append(offs[-1] + s)
    wb = jnp.concatenate([w, b[None, :]], axis=0)
    piece = lambda i: wb[:, offs[i]:offs[i + 1]]
    zeros = lambda n: jnp.zeros((wb.shape[0], n), wb.dtype)
    q, k, v, r, dec, lx, lg, rkv, wl, al, gl, gates = (piece(i) for i in range(len(sizes)))
    small = jnp.concatenate([wl, al, gl, zeros(SM_GL[1] - RWKV_GATE_RANK),
                             dec, zeros(SM_DEC[1] - GLA_DECAY_RANK)], axis=1)
    out = jnp.concatenate([gates, rkv, v, r, lx, lg, q, k, small], axis=1)
    return out[:-1].astype(BF16), out[-1:]


def _block_diag(blocks):
    n, bi, bo = blocks.shape
    eye = jnp.eye(n, dtype=blocks.dtype)
    return (eye[:, None, :, None] * blocks[:, :, None, :]).reshape(n * bi, n * bo)


def kernel(x, w_in, b_in, gla_w_decay_up, gla_b_decay, gla_norm_g, gla_norm_b, lru_conv_w, lru_conv_b, lru_w_r, lru_b_r, lru_w_i, lru_b_i, lru_lambda, rwkv_mu, rwkv_w0, rwkv_w2, rwkv_a0, rwkv_a2, rwkv_g2, rwkv_k_k, rwkv_k_a, rwkv_r_k, rwkv_lnx_g, rwkv_lnx_b, p_gla, p_lru, p_rwkv, w_out, ln_mix_g, ln_mix_b, ffn_w_gate, ffn_w_up, ffn_w_down, moe_w_router, moe_w_gate, moe_w_up, moe_w_down, ln_ffn_g, ln_ffn_b):
    batch, seq, d = x.shape
    t = batch * seq
    xf = x.reshape(t, d)
    xb = xf.astype(BF16)
    row = lambda a: a.reshape(1, -1)
    head_ones = _block_diag(jnp.ones((RWKV_HEADS, RWKV_HEAD, RWKV_HEAD), BF16))
    for l in range(DEPTH):
        w_l, b_l = _reorder_in_projection(w_in[l], b_in[l])
        h = _in_projection(xb, w_l, b_l)

        wup = _pad_rows(gla_w_decay_up[l], SM_DEC[1]).astype(BF16)
        o_gla = _gla_branch(h, batch, seq, wup, row(gla_b_decay[l]), row(gla_norm_g[l]), row(gla_norm_b[l]))

        w_ri = jnp.concatenate([_block_diag(lru_w_r[l]), _block_diag(lru_w_i[l])], axis=1).astype(BF16)
        b_ri = jnp.concatenate([lru_b_r[l], lru_b_i[l]])[None, :]
        o_lru = _lru_branch(h, batch, seq, lru_conv_w[l], row(lru_conv_b[l]), w_ri, b_ri,
                            row(jax.nn.softplus(-lru_lambda[l])))

        mu = rwkv_mu[l]
        nr = 3 * RWKV_WIDTH
        mu_small = jnp.zeros((COL_SMALL[1],), F32).at[0:mu.shape[0] - nr].set(mu[nr:])
        prep_params = (row(mu[:nr]), row(mu_small),
                       _pad_rows(rwkv_w2[l], SM_WA[1], 0).astype(BF16),
                       _pad_rows(rwkv_a2[l], SM_WA[1], RWKV_DECAY_RANK).astype(BF16),
                       _pad_rows(rwkv_g2[l], SM_GL[1], 0).astype(BF16),
                       row(rwkv_w0[l]), row(rwkv_a0[l]), row(rwkv_k_k[l]), row(rwkv_k_a[l]),
                       row(rwkv_r_k[l]), head_ones)
        prep = _rwkv_prep(h, batch, seq, prep_params)
        o_rwkv = _rwkv_scan(prep, batch, seq, row(rwkv_lnx_g[l]), row(rwkv_lnx_b[l]))

        xf, xb = _merge(h, o_gla, o_lru, o_rwkv, xf, p_gla[l].astype(BF16), p_lru[l].astype(BF16),
                        p_rwkv[l].astype(BF16), w_out[l].astype(BF16), row(ln_mix_g[l]), row(ln_mix_b[l]))
        i = l // 2
        if l % 2 == 0:
            xf, xb = _dense_ffn(xb, xf, ffn_w_gate[i].astype(BF16), ffn_w_up[i].astype(BF16),
                                ffn_w_down[i].astype(BF16), row(ln_ffn_g[l]), row(ln_ffn_b[l]))
        else:
            xf, xb = _moe_ffn(xf, moe_w_router[i], moe_w_gate[i].astype(BF16), moe_w_up[i].astype(BF16),
                              moe_w_down[i].astype(BF16), row(ln_ffn_g[l]), row(ln_ffn_b[l]))
    return xf.reshape(batch, seq, d)
```

```python
import functools

import jax
import jax.numpy as jnp
from jax import lax
from jax.experimental import pallas as pl
from jax.experimental.pallas import tpu as pltpu

F32 = jnp.float32
BF16 = jnp.bfloat16
I32 = jnp.int32
HIGHEST = lax.Precision.HIGHEST

D_MODEL = 1024
DEPTH = 2
GLA_HEADS, GLA_DK, GLA_DV = 4, 64, 128
GLA_KW, GLA_VW = GLA_HEADS * GLA_DK, GLA_HEADS * GLA_DV
GLA_DECAY_RANK = 16
GLA_TEMP = 16.0
LRU_WIDTH, LRU_BLOCKS, LRU_CONV, LRU_C = 512, 8, 4, 8.0
RWKV_HEAD, RWKV_WIDTH = 64, 512
RWKV_HEADS = RWKV_WIDTH // RWKV_HEAD
RWKV_DECAY_RANK, RWKV_A_RANK, RWKV_GATE_RANK = 64, 64, 160
RWKV_LNX_EPS = 64e-5
N_BRANCH = 3
N_EXPERTS, TOP_K = 8, 2
DEEPNORM_ALPHA = (2 * DEPTH) ** 0.25
LN_EPS = 1e-5

LANES = 128
SUBLANES = 8
VMEM_LIMIT_BYTES = 56 * 1024 * 1024

COL_GATES = (0, 3072)
COL_RW_RKV = (3072, 1536)
COL_GLA_V = (4608, 512)
COL_GLA_R = (5120, 512)
COL_LRU_X = (5632, 512)
COL_LRU_G = (6144, 512)
COL_GLA_Q = (6656, 256)
COL_GLA_K = (6912, 256)
COL_SMALL = (7168, 512)
N_IN_PAD = 7680
SM_WA = (0, 128)
SM_GL = (128, 256)
SM_DEC = (384, 128)

CHUNK = 64
INV_BLOCK = 16
MOE_ROWS = 512


def _cparams(*sem):
    return pltpu.CompilerParams(dimension_semantics=sem, vmem_limit_bytes=VMEM_LIMIT_BYTES)


def _sigmoid(x):
    return 1.0 / (1.0 + jnp.exp(-x))


def _softplus(x):
    return jnp.maximum(x, 0.0) + jnp.log1p(jnp.exp(-jnp.abs(x)))


def _bdot(a, b):
    return jnp.dot(a.astype(BF16), b.astype(BF16), preferred_element_type=F32)


def _bdot_nt(a, b):
    return lax.dot_general(a.astype(BF16), b.astype(BF16), (((1,), (1,)), ((), ())),
                           preferred_element_type=F32)


def _bdot_tn(a, b):
    return lax.dot_general(a.astype(BF16), b.astype(BF16), (((0,), (0,)), ((), ())),
                           preferred_element_type=F32)


def _fdot(a, b):
    return jnp.dot(a, b, precision=HIGHEST, preferred_element_type=F32)


def _split3(x):
    hi = x.astype(BF16)
    rest = x - hi.astype(F32)
    mid = rest.astype(BF16)
    lo = (rest - mid.astype(F32)).astype(BF16)
    return hi, mid, lo


def _mask_dot(mask_bf16, x):
    hi, mid, lo = _split3(x)
    dot = lambda part: jnp.dot(mask_bf16, part, preferred_element_type=F32)
    return dot(lo) + dot(mid) + dot(hi)


def _dot_mask(x, mask_bf16):
    hi, mid, lo = _split3(x)
    dot = lambda part: jnp.dot(part, mask_bf16, preferred_element_type=F32)
    return dot(lo) + dot(mid) + dot(hi)


def _layer_norm_rows(z, g, b, eps):
    mu = jnp.mean(z, axis=-1, keepdims=True)
    zc = z - mu
    var = jnp.mean(zc * zc, axis=-1, keepdims=True)
    return zc * lax.rsqrt(var + eps) * g + b


def _tri_masks(n):
    row = lax.broadcasted_iota(I32, (n, n), 0)
    col = lax.broadcasted_iota(I32, (n, n), 1)
    return row, col


def _proj_kernel(x_ref, w_ref, b_ref, o_ref):
    o_ref[...] = jnp.dot(x_ref[...], w_ref[...], preferred_element_type=F32) + b_ref[...]


def _in_projection(xb, w, b):
    t, d = xb.shape
    n = w.shape[1]
    tm = min(1024, t)
    tn = 1536
    return pl.pallas_call(
        _proj_kernel,
        out_shape=jax.ShapeDtypeStruct((t, n), F32),
        grid=(t // tm, n // tn),
        in_specs=[pl.BlockSpec((tm, d), lambda i, j: (i, 0)),
                  pl.BlockSpec((d, tn), lambda i, j: (0, j)),
                  pl.BlockSpec((1, tn), lambda i, j: (0, j))],
        out_specs=pl.BlockSpec((tm, tn), lambda i, j: (i, j)),
        compiler_params=_cparams("parallel", "arbitrary"),
    )(xb, w, b)


def _gla_kernel(q_ref, k_ref, v_ref, r_ref, sm_ref, wup_ref, bdec_ref, ng_ref, nb_ref,
                o_ref, st_ref, *, chunks):
    @pl.when(pl.program_id(1) == 0)
    def _():
        st_ref[...] = jnp.zeros_like(st_ref)

    row, col = _tri_masks(CHUNK)
    ltri = (row >= col).astype(BF16)
    kw, vw = 2 * GLA_DK, 2 * GLA_DV
    key_shift, val_shift = GLA_DK.bit_length() - 1, GLA_DV.bit_length() - 1
    iota = lambda shape, axis: lax.broadcasted_iota(I32, shape, axis)
    causal = iota((CHUNK, kw), 0) >= (iota((CHUNK, kw), 1) & (GLA_DK - 1))
    key_diag = (iota((kw, kw), 0) >> key_shift) == (iota((kw, kw), 1) >> key_shift)
    val_diag = (iota((kw, vw), 0) >> key_shift) == (iota((kw, vw), 1) >> val_shift)
    state_diag = (iota((vw, kw), 0) >> val_shift) == (iota((vw, kw), 1) >> key_shift)
    zero = jnp.zeros((), BF16)
    pairs = range(GLA_HEADS // 2)
    for c in range(chunks):
        rows = pl.ds(c * CHUNK, CHUNK)
        dec = sm_ref[rows, SM_DEC[0]:SM_DEC[0] + SM_DEC[1]]
        logits = _bdot(dec, wup_ref[...]) + bdec_ref[...]
        log_a = -_softplus(-logits) * (1.0 / GLA_TEMP)
        cum = _mask_dot(ltri, log_a)
        cum_end = cum[CHUNK - 1:CHUNK, :]
        q = q_ref[rows, :] * (GLA_DK ** -0.5)
        k = k_ref[rows, :]
        q_dec = (q * jnp.exp(cum)).astype(BF16)
        k_inv = (k * jnp.exp(-cum)).astype(BF16)
        k_end = (k * jnp.exp(cum_end - cum)).astype(BF16)
        decay_end = jnp.exp(cum_end)
        v = v_ref[rows, :].astype(BF16)
        ks = [slice(p * kw, (p + 1) * kw) for p in pairs]
        vs = [slice(p * vw, (p + 1) * vw) for p in pairs]
        qp = [q_dec[:, s] for s in ks]
        vp = [v[:, s] for s in vs]
        state = [st_ref[p] for p in pairs]
        k_diag = [jnp.where(key_diag, jnp.concatenate([k_inv[:, s], k_inv[:, s]], axis=0), zero) for s in ks]
        v_diag = [jnp.where(val_diag, jnp.concatenate([x, x], axis=0), zero) for x in vp]
        scores = [jnp.where(causal, _bdot_nt(q, kd), 0.0) for q, kd in zip(qp, k_diag)]
        o = [_bdot(s, vd) + _bdot_nt(q, st) for s, vd, q, st in zip(scores, v_diag, qp, state)]
        for p in pairs:
            upd = jnp.where(state_diag, _bdot_tn(vp[p], k_end[:, ks[p]]), 0.0)
            st_ref[p] = state[p] * decay_end[:, ks[p]] + upd
        outs = []
        for op in o:
            for oh in (op[:, :GLA_DV], op[:, GLA_DV:]):
                mu = jnp.mean(oh, axis=-1, keepdims=True)
                oc = oh - mu
                var = jnp.mean(oc * oc, axis=-1, keepdims=True)
                outs.append(oc * lax.rsqrt(var + LN_EPS))
        y = jnp.concatenate(outs, axis=1) * ng_ref[...] + nb_ref[...]
        r = r_ref[rows, :]
        o_ref[rows, :] = (y * (r * _sigmoid(r))).astype(o_ref.dtype)


def _gla_branch(h, batch, seq, wup_pad, b_decay, norm_g, norm_b):
    t = batch * seq
    chunks = 4
    rows = CHUNK * chunks
    nblk = seq // rows

    def col_spec(col):
        off, width = col
        return pl.BlockSpec((rows, width), lambda b, n: (b * nblk + n, off // width))

    def full(a):
        return pl.BlockSpec(a.shape, lambda b, n: (0,) * a.ndim)

    args = (wup_pad, b_decay, norm_g, norm_b)
    return pl.pallas_call(
        functools.partial(_gla_kernel, chunks=chunks),
        out_shape=jax.ShapeDtypeStruct((t, GLA_VW), BF16),
        grid=(batch, nblk),
        in_specs=[col_spec(COL_GLA_Q), col_spec(COL_GLA_K), col_spec(COL_GLA_V),
                  col_spec(COL_GLA_R), col_spec(COL_SMALL)] + [full(a) for a in args],
        out_specs=pl.BlockSpec((rows, GLA_VW), lambda b, n: (b * nblk + n, 0)),
        scratch_shapes=[pltpu.VMEM((GLA_HEADS // 2, 2 * GLA_DV, 2 * GLA_DK), F32)],
        compiler_params=_cparams("parallel", "arbitrary"),
    )(h, h, h, h, h, *args)


def _lru_kernel(x_ref, g_ref, cw_ref, cb_ref, wri_ref, bri_ref, sp_ref,
                o_ref, xbuf_ref, a_ref, u_ref, hs_ref, h_ref):
    tm = x_ref.shape[0]
    pad = SUBLANES

    @pl.when(pl.program_id(1) == 0)
    def _():
        xbuf_ref[0:pad, :] = jnp.zeros((pad, LRU_WIDTH), F32)
        h_ref[...] = jnp.zeros_like(h_ref)

    x = x_ref[...]
    xbuf_ref[pad:pad + tm, :] = x
    xc = cb_ref[...] + x * cw_ref[LRU_CONV - 1:LRU_CONV, :]
    for j in range(LRU_CONV - 1):
        back = LRU_CONV - 1 - j
        xc = xc + xbuf_ref[pl.ds(pad - back, tm), :] * cw_ref[j:j + 1, :]
    xbuf_ref[0:pad, :] = x[tm - pad:tm, :]

    ri = _sigmoid(_bdot(xc, wri_ref[...]) + bri_ref[...])
    rg, ig = ri[:, :LRU_WIDTH], ri[:, LRU_WIDTH:]
    log_a = -LRU_C * rg * sp_ref[...]
    a_ref[...] = jnp.exp(log_a)
    u_ref[...] = jnp.sqrt(1.0 - jnp.exp(2.0 * log_a)) * (ig * xc)

    def step(t, hprev):
        hnew = a_ref[pl.ds(t, 1), :] * hprev + u_ref[pl.ds(t, 1), :]
        hs_ref[pl.ds(t, 1), :] = hnew
        return hnew

    h_ref[...] = lax.fori_loop(0, tm, step, h_ref[...], unroll=8)
    o_ref[...] = (jax.nn.gelu(g_ref[...]) * hs_ref[...]).astype(o_ref.dtype)


def _lru_branch(h, batch, seq, conv_w, conv_b, w_ri, b_ri, softplus_neg_lam):
    t = batch * seq
    tm = min(512, seq)
    nblk = seq // tm

    def col_spec(col):
        off, width = col
        return pl.BlockSpec((tm, width), lambda b, n: (b * nblk + n, off // width))

    def full(a):
        return pl.BlockSpec(a.shape, lambda b, n: (0,) * a.ndim)

    args = (conv_w, conv_b, w_ri, b_ri, softplus_neg_lam)
    return pl.pallas_call(
        _lru_kernel,
        out_shape=jax.ShapeDtypeStruct((t, LRU_WIDTH), BF16),
        grid=(batch, nblk),
        in_specs=[col_spec(COL_LRU_X), col_spec(COL_LRU_G)] + [full(a) for a in args],
        out_specs=pl.BlockSpec((tm, LRU_WIDTH), lambda b, n: (b * nblk + n, 0)),
        scratch_shapes=[pltpu.VMEM((tm + SUBLANES, LRU_WIDTH), F32),
                        pltpu.VMEM((tm, LRU_WIDTH), F32),
                        pltpu.VMEM((tm, LRU_WIDTH), F32),
                        pltpu.VMEM((tm, LRU_WIDTH), F32),
                        pltpu.VMEM((1, LRU_WIDTH), F32)],
        compiler_params=_cparams("parallel", "arbitrary"),
    )(h, h, *args)


def _rwkv_prep_kernel(p_ref, pprev_ref, sm_ref, smprev_ref, mu_ref, musm_ref, w2_ref, a2_ref, g2_ref,
                      w0_ref, a0_ref, kk_ref, ka_ref, rk_ref, bd_ref,
                      r_out, k_out, v_out, lw_out, kk_out, kka_out, bonus_out, g_out, *, blocks_per_seq):
    tm = p_ref.shape[0]
    first = (pl.program_id(0) % blocks_per_seq) == 0
    keep = jnp.where(first, 0.0, 1.0)
    row0 = lax.broadcasted_iota(I32, (tm, 1), 0) == 0

    def token_shift(cur_ref, prev_ref, m_ref):
        cur = cur_ref[...]
        prev_row = prev_ref[SUBLANES - 1:SUBLANES, :] * keep
        prev = jnp.where(row0, prev_row, pltpu.roll(cur, 1, axis=0))
        return cur + (prev - cur) * m_ref[...]

    p = token_shift(p_ref, pprev_ref, mu_ref)
    sm = token_shift(sm_ref, smprev_ref, musm_ref)
    r = p[:, 0:RWKV_WIDTH]
    k = p[:, RWKV_WIDTH:2 * RWKV_WIDTH]
    v = p[:, 2 * RWKV_WIDTH:3 * RWKV_WIDTH]
    wa = sm[:, SM_WA[0]:SM_WA[0] + SM_WA[1]]
    gl = sm[:, SM_GL[0]:SM_GL[0] + SM_GL[1]]

    w_log = -_softplus(-(w0_ref[...] + _bdot(jnp.tanh(wa), w2_ref[...]))) - 0.5
    lw_out[...] = -jnp.exp(w_log)
    a = _sigmoid(a0_ref[...] + _bdot(wa, a2_ref[...]))
    g_out[...] = _bdot(_sigmoid(gl), g2_ref[...])

    kk = k * kk_ref[...]
    norm = jnp.sqrt(_dot_mask(kk * kk, bd_ref[...]))
    kk = kk / jnp.maximum(norm, 1e-12)
    k2 = k * (1.0 + (a - 1.0) * ka_ref[...])
    bonus_out[...] = _dot_mask(r * k2 * rk_ref[...], bd_ref[...]) * v
    r_out[...] = r
    k_out[...] = k2
    v_out[...] = v
    kk_out[...] = kk
    kka_out[...] = kk * a


def _rwkv_prep(h, batch, seq, params):
    t = batch * seq
    tm = min(512, seq)
    bps = seq // tm
    sub_per_tile = tm // SUBLANES

    def cur_spec(col):
        off, width = col
        return pl.BlockSpec((tm, width), lambda i: (i, off // width))

    def prev_spec(col):
        off, width = col
        return pl.BlockSpec((SUBLANES, width),
                            lambda i: (jnp.maximum(i * sub_per_tile - 1, 0), off // width))

    def full(a):
        return pl.BlockSpec(a.shape, lambda i: (0,) * a.ndim)

    out = jax.ShapeDtypeStruct((t, RWKV_WIDTH), F32)
    ospec = pl.BlockSpec((tm, RWKV_WIDTH), lambda i: (i, 0))
    return pl.pallas_call(
        functools.partial(_rwkv_prep_kernel, blocks_per_seq=bps),
        out_shape=[out] * 8,
        grid=(t // tm,),
        in_specs=[cur_spec(COL_RW_RKV), prev_spec(COL_RW_RKV), cur_spec(COL_SMALL), prev_spec(COL_SMALL)]
        + [full(a) for a in params],
        out_specs=[ospec] * 8,
        compiler_params=_cparams("parallel"),
    )(h, h, h, h, *params)


PAIR_W = 2 * RWKV_HEAD
N_PAIRS = RWKV_HEADS // 2
HEAD_SHIFT = RWKV_HEAD.bit_length() - 1


def _pair_masks():
    t = lax.broadcasted_iota(I32, (CHUNK, PAIR_W), 0)
    lane = lax.broadcasted_iota(I32, (CHUNK, PAIR_W), 1)
    j = lane & (RWKV_HEAD - 1)
    r = lax.broadcasted_iota(I32, (PAIR_W, PAIR_W), 0)
    c = lax.broadcasted_iota(I32, (PAIR_W, PAIR_W), 1)
    r4 = lax.broadcasted_iota(I32, (2 * PAIR_W, PAIR_W), 0)
    c4 = lax.broadcasted_iota(I32, (2 * PAIR_W, PAIR_W), 1)
    return dict(
        strict=t > j, incl=t >= j, eye=(t == j).astype(F32),
        same_block=jnp.bitwise_xor(t, j) < INV_BLOCK,
        first_head=lane < RWKV_HEAD,
        block_diag=(r >> HEAD_SHIFT) == (c >> HEAD_SHIFT),
        stacked_diag=((r4 >> HEAD_SHIFT) & 1) == (c4 >> HEAD_SHIFT))


def _rwkv_scan_kernel(r_ref, k_ref, v_ref, lw_ref, kk_ref, kka_ref, bonus_ref, g_ref, lg_ref, lb_ref,
                      o_ref, st_ref, *, chunks):
    @pl.when(pl.program_id(1) == 0)
    def _():
        st_ref[...] = jnp.zeros_like(st_ref)

    row, col = _tri_masks(CHUNK)
    ltri = (row >= col).astype(BF16)
    m = _pair_masks()
    zero = jnp.zeros((), BF16)
    pairs = range(N_PAIRS)
    sl = [slice(p * PAIR_W, (p + 1) * PAIR_W) for p in pairs]

    def block_diag(x):
        xb = x.astype(BF16)
        return jnp.where(m["block_diag"], jnp.concatenate([xb, xb], axis=0), zero)

    def pair_mul(lhs, *rhs):
        rb = jnp.concatenate([block_diag(x) for x in rhs], axis=1) if len(rhs) > 1 else block_diag(rhs[0])
        out = jnp.dot(lhs.astype(BF16), rb, preferred_element_type=F32)
        return [out[:, i * PAIR_W:(i + 1) * PAIR_W] for i in range(len(rhs))]

    def each(fn, *lists):
        return [fn(*args) for args in zip(*lists)]

    for c in range(chunks):
        rows = pl.ds(c * CHUNK, CHUNK)
        lw = lw_ref[rows, :]
        cum = _mask_dot(ltri, lw)
        cum_end = cum[CHUNK - 1:CHUNK, :]
        e_inv = jnp.exp(-cum)
        e_end = jnp.exp(cum_end - cum)
        kka = kka_ref[rows, :]
        k = k_ref[rows, :]
        a_bar = (-kk_ref[rows, :] * jnp.exp(cum - lw)).astype(BF16)
        r_bar = (r_ref[rows, :] * jnp.exp(cum)).astype(BF16)
        b_til = (kka * e_inv).astype(BF16)
        k_til = (k * e_inv).astype(BF16)
        b_end = (kka * e_end).astype(BF16)
        k_end = (k * e_end).astype(BF16)
        gamma = jnp.exp(cum_end)
        v = v_ref[rows, :].astype(BF16)

        lhs = [jnp.concatenate([a_bar[:, s], r_bar[:, s]], axis=0) for s in sl]
        rhs = [jnp.where(m["stacked_diag"],
                         jnp.concatenate([b_til[:, s], b_til[:, s], k_til[:, s], k_til[:, s]], axis=0), zero)
               for s in sl]
        vp = [v[:, s] for s in sl]
        state = [st_ref[p] for p in pairs]
        blocks = each(_bdot_nt, lhs, rhs)
        from_state = each(_bdot_nt, lhs, state)
        a_ab = [jnp.where(m["strict"], b[:CHUNK, :PAIR_W], 0.0) for b in blocks]
        a_ak = [jnp.where(m["strict"], b[:CHUNK, PAIR_W:], 0.0) for b in blocks]
        r_ab = [jnp.where(m["incl"], b[CHUNK:, :PAIR_W], 0.0) for b in blocks]
        r_ak = [jnp.where(m["incl"], b[CHUNK:, PAIR_W:], 0.0) for b in blocks]
        from_v = [jnp.dot(jnp.concatenate([ak, rk], axis=0).astype(BF16), block_diag(x),
                          preferred_element_type=F32) for ak, rk, x in zip(a_ak, r_ak, vp)]
        rhs0 = [fs[:CHUNK] + fv[:CHUNK] for fs, fv in zip(from_state, from_v)]

        d = [jnp.where(m["same_block"], a, 0.0) for a in a_ab]
        low = [a - di for a, di in zip(a_ab, d)]
        pm = [m["eye"] + di for di in d]
        dpow = [pair_mul(di, di)[0] for di in d]
        for _ in range(INV_BLOCK.bit_length() - 3):
            both = each(lambda x, p: pair_mul(x, x, p), dpow, pm)
            dpow = [b[0] for b in both]
            pm = [p + b[1] for p, b in zip(pm, both)]
        pm = [p + pair_mul(x, p)[0] for p, x in zip(pm, dpow)]
        both = each(lambda p, lo, r0: pair_mul(p, lo, r0), pm, low, rhs0)
        e = [b[0] for b in both]
        x0 = [b[1] for b in both]
        both = each(lambda ei, xi: pair_mul(ei, ei, xi), e, x0)
        xw = [xi + b[1] for xi, b in zip(x0, both)]
        u = [xi + pair_mul(b[0], xi)[0] for xi, b in zip(xw, both)]

        y = [fs[CHUNK:] + fv[CHUNK:] + pair_mul(rb, ui)[0]
             for fs, fv, rb, ui in zip(from_state, from_v, r_ab, u)]
        for p in pairs:
            upd = _bdot_tn(jnp.concatenate([u[p].astype(BF16), vp[p]], axis=0),
                           jnp.concatenate([b_end[:, sl[p]], k_end[:, sl[p]]], axis=0))
            st_ref[p] = state[p] * gamma[:, sl[p]] + jnp.where(m["block_diag"], upd, 0.0)
        outs = []
        inv_n = 1.0 / RWKV_HEAD
        for yp in y:
            def head_stat(z):
                s0 = jnp.sum(jnp.where(m["first_head"], z, 0.0), axis=-1, keepdims=True)
                s1 = jnp.sum(jnp.where(m["first_head"], 0.0, z), axis=-1, keepdims=True)
                return jnp.where(m["first_head"], s0, s1) * inv_n
            yc = yp - head_stat(yp)
            outs.append(yc * lax.rsqrt(head_stat(yc * yc) + RWKV_LNX_EPS))
        yn = jnp.concatenate(outs, axis=1) * lg_ref[...] + lb_ref[...]
        o_ref[rows, :] = ((yn + bonus_ref[rows, :]) * g_ref[rows, :]).astype(o_ref.dtype)


def _rwkv_scan(prep, batch, seq, lnx_g, lnx_b):
    t = batch * seq
    chunks = 4
    rows = CHUNK * chunks
    nblk = seq // rows
    spec = pl.BlockSpec((rows, RWKV_WIDTH), lambda b, n: (b * nblk + n, 0))
    pspec = pl.BlockSpec((1, RWKV_WIDTH), lambda b, n: (0, 0))
    return pl.pallas_call(
        functools.partial(_rwkv_scan_kernel, chunks=chunks),
        out_shape=jax.ShapeDtypeStruct((t, RWKV_WIDTH), BF16),
        grid=(batch, nblk),
        in_specs=[spec] * 8 + [pspec, pspec],
        out_specs=spec,
        scratch_shapes=[pltpu.VMEM((N_PAIRS, PAIR_W, PAIR_W), F32)],
        compiler_params=_cparams("parallel", "arbitrary"),
    )(*prep, lnx_g, lnx_b)


def _merge_kernel(gt_ref, og_ref, ol_ref, or_ref, x_ref, pg_ref, pl_ref, pr_ref, wo_ref, lg_ref, lb_ref,
                  xo_ref, xb_ref):
    d = D_MODEL
    merged = (_sigmoid(gt_ref[:, 0:d]) * _bdot(og_ref[...], pg_ref[...])
              + _sigmoid(gt_ref[:, d:2 * d]) * _bdot(ol_ref[...], pl_ref[...])
              + _sigmoid(gt_ref[:, 2 * d:3 * d]) * _bdot(or_ref[...], pr_ref[...]))
    z = DEEPNORM_ALPHA * x_ref[...] + _bdot(merged, wo_ref[...])
    out = _layer_norm_rows(z, lg_ref[...], lb_ref[...], LN_EPS)
    xo_ref[...] = out
    xb_ref[...] = out.astype(BF16)


def _merge(h, o_gla, o_lru, o_rwkv, x, p_gla, p_lru, p_rwkv, w_out, ln_g, ln_b):
    t = x.shape[0]
    tm = min(512, t)

    def rows(width):
        return pl.BlockSpec((tm, width), lambda i: (i, 0))

    def full(a):
        return pl.BlockSpec(a.shape, lambda i: (0,) * a.ndim)

    params = (p_gla, p_lru, p_rwkv, w_out, ln_g, ln_b)
    return pl.pallas_call(
        _merge_kernel,
        out_shape=[jax.ShapeDtypeStruct((t, D_MODEL), F32), jax.ShapeDtypeStruct((t, D_MODEL), BF16)],
        grid=(t // tm,),
        in_specs=[rows(COL_GATES[1]), rows(GLA_VW), rows(LRU_WIDTH), rows(RWKV_WIDTH), rows(D_MODEL)]
        + [full(a) for a in params],
        out_specs=[rows(D_MODEL), rows(D_MODEL)],
        compiler_params=_cparams("parallel"),
    )(h, o_gla, o_lru, o_rwkv, x, *params)


def _ffn_kernel(xb_ref, x_ref, wg_ref, wu_ref, wd_ref, lg_ref, lb_ref, xo_ref, xbo_ref, acc_ref):
    f = pl.program_id(1)

    @pl.when(f == 0)
    def _():
        acc_ref[...] = jnp.zeros_like(acc_ref)

    xb = xb_ref[...]
    g = jnp.dot(xb, wg_ref[...], preferred_element_type=F32)
    u = jnp.dot(xb, wu_ref[...], preferred_element_type=F32)
    acc_ref[...] += _bdot(g * _sigmoid(g) * u, wd_ref[...])

    @pl.when(f == pl.num_programs(1) - 1)
    def _():
        z = DEEPNORM_ALPHA * x_ref[...] + acc_ref[...]
        out = _layer_norm_rows(z, lg_ref[...], lb_ref[...], LN_EPS)
        xo_ref[...] = out
        xbo_ref[...] = out.astype(BF16)


def _dense_ffn(xb, x, wg, wu, wd, ln_g, ln_b):
    t = x.shape[0]
    tm = min(512, t)
    ff = wg.shape[1]
    tf = ff // 2
    rows = lambda width: pl.BlockSpec((tm, width), lambda i, f: (i, 0))
    vec = pl.BlockSpec((1, D_MODEL), lambda i, f: (0, 0))
    return pl.pallas_call(
        _ffn_kernel,
        out_shape=[jax.ShapeDtypeStruct((t, D_MODEL), F32), jax.ShapeDtypeStruct((t, D_MODEL), BF16)],
        grid=(t // tm, ff // tf),
        in_specs=[rows(D_MODEL), rows(D_MODEL),
                  pl.BlockSpec((D_MODEL, tf), lambda i, f: (0, f)),
                  pl.BlockSpec((D_MODEL, tf), lambda i, f: (0, f)),
                  pl.BlockSpec((tf, D_MODEL), lambda i, f: (f, 0)), vec, vec],
        out_specs=[rows(D_MODEL), rows(D_MODEL)],
        scratch_shapes=[pltpu.VMEM((tm, D_MODEL), F32)],
        compiler_params=_cparams("parallel", "arbitrary"),
    )(xb, x, wg, wu, wd, ln_g, ln_b)


def _router_kernel(x_ref, wr_ref, meta_ref, wts_ref, cnt_ref, carry_ref):
    tm = x_ref.shape[0]

    @pl.when(pl.program_id(0) == 0)
    def _():
        carry_ref[...] = jnp.zeros_like(carry_ref)

    lane = lax.broadcasted_iota(I32, (tm, LANES), 1)
    neg = jnp.float32(-jnp.inf)
    logits = jnp.where(lane < N_EXPERTS, _fdot(x_ref[...], wr_ref[...]), neg)
    m1 = jnp.max(logits, axis=-1, keepdims=True)
    e1 = jnp.min(jnp.where(logits == m1, lane, LANES), axis=-1, keepdims=True)
    rest = jnp.where(lane == e1, neg, logits)
    m2 = jnp.max(rest, axis=-1, keepdims=True)
    e2 = jnp.min(jnp.where(rest == m2, lane, LANES), axis=-1, keepdims=True)
    ex = jnp.exp(m2 - m1)
    w1 = 1.0 / (1.0 + ex)
    w2 = ex / (1.0 + ex)

    hot1 = lane == e1
    hot2 = lane == e2
    onehot = jnp.where(hot1 | hot2, 1.0, 0.0)
    row, col = _tri_masks(tm)
    before = (row > col).astype(BF16)
    prefix = jnp.dot(before, onehot.astype(BF16), preferred_element_type=F32) + carry_ref[...]
    rank1 = jnp.sum(jnp.where(hot1, prefix, 0.0), axis=-1, keepdims=True).astype(I32)
    rank2 = jnp.sum(jnp.where(hot2, prefix, 0.0), axis=-1, keepdims=True).astype(I32)
    carry_ref[...] += jnp.sum(onehot, axis=0, keepdims=True)
    cnt_ref[...] = carry_ref[...]

    meta = jnp.where(lane == 0, e1, jnp.where(lane == 1, e2, jnp.where(lane == 2, rank1, rank2)))
    meta_ref[...] = jnp.where(lane < 4, meta, 0)
    wts_ref[...] = jnp.where(lane == 0, w1, jnp.where(lane == 1, w2, 0.0))


def _router(x, wr_pad):
    t = x.shape[0]
    tm = min(512, t)
    return pl.pallas_call(
        _router_kernel,
        out_shape=[jax.ShapeDtypeStruct((t, LANES), I32), jax.ShapeDtypeStruct((t, LANES), F32),
                   jax.ShapeDtypeStruct((1, LANES), F32)],
        grid=(t // tm,),
        in_specs=[pl.BlockSpec((tm, D_MODEL), lambda i: (i, 0)),
                  pl.BlockSpec((D_MODEL, LANES), lambda i: (0, 0))],
        out_specs=[pl.BlockSpec((tm, LANES), lambda i: (i, 0)), pl.BlockSpec((tm, LANES), lambda i: (i, 0)),
                   pl.BlockSpec((1, LANES), lambda i: (0, 0))],
        scratch_shapes=[pltpu.VMEM((1, LANES), F32)],
        compiler_params=_cparams("arbitrary"),
    )(x, wr_pad)


def _dispatch_kernel(dest_ref, x_ref, xs_in_ref, xs_ref, sem):
    del xs_in_ref
    tm = x_ref.shape[0]

    def row_copy(r, j):
        return pltpu.make_async_copy(x_ref.at[pl.ds(r, 1)], xs_ref.at[pl.ds(dest_ref[TOP_K * r + j], 1)], sem)

    def start(r, c):
        for j in range(TOP_K):
            row_copy(r, j).start()
        return c

    lax.fori_loop(0, tm, start, 0, unroll=8)
    for j in range(TOP_K):
        pltpu.make_async_copy(x_ref, xs_ref.at[pl.ds(0, tm)], sem).wait()


def _dispatch(x, dest_flat, n_slots):
    t = x.shape[0]
    tm = min(256, t)
    xs0 = jnp.zeros((n_slots, D_MODEL), F32)
    return pl.pallas_call(
        _dispatch_kernel,
        out_shape=jax.ShapeDtypeStruct((n_slots, D_MODEL), F32),
        grid=(t // tm,),
        in_specs=[pl.BlockSpec((tm * TOP_K,), lambda i: (i,), memory_space=pltpu.SMEM),
                  pl.BlockSpec((tm, D_MODEL), lambda i: (i, 0)),
                  pl.BlockSpec(memory_space=pl.ANY)],
        out_specs=pl.BlockSpec(memory_space=pl.ANY),
        scratch_shapes=[pltpu.SemaphoreType.DMA(())],
        input_output_aliases={2: 0},
        compiler_params=_cparams("arbitrary"),
    )(dest_flat, x, xs0)


def _expert_kernel(blk_e_ref, nused_ref, xs_ref, wg_ref, wu_ref, wd_ref, ys_ref, acc_ref):
    del blk_e_ref
    i = pl.program_id(0)
    f = pl.program_id(1)
    last = pl.num_programs(1) - 1
    used = i < nused_ref[0]

    @pl.when(used & (f == 0))
    def _():
        acc_ref[...] = jnp.zeros_like(acc_ref)

    @pl.when(used)
    def _():
        xb = xs_ref[...].astype(BF16)
        g = jnp.dot(xb, wg_ref[0], preferred_element_type=F32)
        u = jnp.dot(xb, wu_ref[0], preferred_element_type=F32)
        acc_ref[...] += _bdot(g * _sigmoid(g) * u, wd_ref[0])

    @pl.when(used & (f == last))
    def _():
        ys_ref[...] = acc_ref[...]

    @pl.when(jnp.logical_not(used) & (f == last))
    def _():
        ys_ref[...] = jnp.zeros_like(ys_ref)


def _experts(xs, blk_e, n_used, wg, wu, wd):
    n_slots = xs.shape[0]
    n_blocks = n_slots // MOE_ROWS
    ff = wg.shape[2]
    tf = ff // 2
    grid_spec = pltpu.PrefetchScalarGridSpec(
        num_scalar_prefetch=2,
        grid=(n_blocks, ff // tf),
        in_specs=[pl.BlockSpec((MOE_ROWS, D_MODEL), lambda i, f, be, nu: (i, 0)),
                  pl.BlockSpec((1, D_MODEL, tf), lambda i, f, be, nu: (be[i], 0, f)),
                  pl.BlockSpec((1, D_MODEL, tf), lambda i, f, be, nu: (be[i], 0, f)),
                  pl.BlockSpec((1, tf, D_MODEL), lambda i, f, be, nu: (be[i], f, 0))],
        out_specs=pl.BlockSpec((MOE_ROWS, D_MODEL), lambda i, f, be, nu: (i, 0)),
        scratch_shapes=[pltpu.VMEM((MOE_ROWS, D_MODEL), F32)],
    )
    return pl.pallas_call(
        _expert_kernel,
        out_shape=jax.ShapeDtypeStruct((n_slots, D_MODEL), F32),
        grid_spec=grid_spec,
        compiler_params=_cparams("arbitrary", "arbitrary"),
    )(blk_e, n_used, xs, wg, wu, wd)


def _combine_kernel(dest_ref, x_ref, wts_ref, ys_ref, lg_ref, lb_ref, xo_ref, xbo_ref, buf_ref, sem):
    tm = x_ref.shape[0]

    def row_copy(r, j):
        return pltpu.make_async_copy(ys_ref.at[pl.ds(dest_ref[TOP_K * r + j], 1)],
                                     buf_ref.at[j, pl.ds(r, 1)], sem)

    def start(r, c):
        for j in range(TOP_K):
            row_copy(r, j).start()
        return c

    lax.fori_loop(0, tm, start, 0, unroll=8)
    for j in range(TOP_K):
        pltpu.make_async_copy(ys_ref.at[pl.ds(0, tm)], buf_ref.at[j], sem).wait()
    w = wts_ref[...]
    f = w[:, 0:1] * buf_ref[0] + w[:, 1:2] * buf_ref[1]
    z = DEEPNORM_ALPHA * x_ref[...] + f
    out = _layer_norm_rows(z, lg_ref[...], lb_ref[...], LN_EPS)
    xo_ref[...] = out
    xbo_ref[...] = out.astype(BF16)


def _combine(x, dest_flat, wts, ys, ln_g, ln_b):
    t = x.shape[0]
    tm = min(256, t)
    rows = lambda width: pl.BlockSpec((tm, width), lambda i: (i, 0))
    vec = pl.BlockSpec((1, D_MODEL), lambda i: (0, 0))
    return pl.pallas_call(
        _combine_kernel,
        out_shape=[jax.ShapeDtypeStruct((t, D_MODEL), F32), jax.ShapeDtypeStruct((t, D_MODEL), BF16)],
        grid=(t // tm,),
        in_specs=[pl.BlockSpec((tm * TOP_K,), lambda i: (i,), memory_space=pltpu.SMEM),
                  rows(D_MODEL), rows(LANES), pl.BlockSpec(memory_space=pl.ANY), vec, vec],
        out_specs=[rows(D_MODEL), rows(D_MODEL)],
        scratch_shapes=[pltpu.VMEM((TOP_K, tm, D_MODEL), F32), pltpu.SemaphoreType.DMA(())],
        compiler_params=_cparams("arbitrary"),
    )(dest_flat, x, wts, ys, ln_g, ln_b)


def _moe_ffn(x, w_router, wg, wu, wd, ln_g, ln_b):
    t = x.shape[0]
    wr_pad = jnp.pad(w_router, ((0, 0), (0, LANES - N_EXPERTS)))
    meta, wts, cnt = _router(x, wr_pad)
    counts = cnt[0, :N_EXPERTS].astype(I32)
    padded = (counts + MOE_ROWS - 1) // MOE_ROWS * MOE_ROWS
    seg_end = jnp.cumsum(padded)
    seg_start = seg_end - padded
    n_blocks = (t * TOP_K) // MOE_ROWS + N_EXPERTS
    dest = seg_start[meta[:, 0:TOP_K]] + meta[:, TOP_K:2 * TOP_K]
    dest_flat = dest.reshape(-1).astype(I32)
    blk_start = jnp.arange(n_blocks, dtype=I32) * MOE_ROWS
    blk_e = jnp.minimum(jnp.sum(blk_start[:, None] >= seg_end[None, :], axis=1), N_EXPERTS - 1).astype(I32)
    n_used = (seg_end[-1:] // MOE_ROWS).astype(I32)
    xs = _dispatch(x, dest_flat, n_blocks * MOE_ROWS)
    ys = _experts(xs, blk_e, n_used, wg, wu, wd)
    return _combine(x, dest_flat, wts, ys, ln_g, ln_b)


def _pad_rows(w, rows, at=0):
    out = jnp.zeros((rows, w.shape[1]), w.dtype)
    return out.at[at:at + w.shape[0]].set(w)


def _reorder_in_projection(w, b):
    sizes = (GLA_KW, GLA_KW, GLA_VW, GLA_VW, GLA_DECAY_RANK, LRU_WIDTH, LRU_WIDTH,
             3 * RWKV_WIDTH, RWKV_DECAY_RANK, RWKV_A_RANK, RWKV_GATE_RANK, N_BRANCH * D_MODEL)
    offs = [0]
    for s in sizes:
        offs.append(offs[-1] + s)
    wb = jnp.concatenate([w, b[None, :]], axis=0)
    piece = lambda i: wb[:, offs[i]:offs[i + 1]]
    zeros = lambda n: jnp.zeros((wb.shape[0], n), wb.dtype)
    q, k, v, r, dec, lx, lg, rkv, wl, al, gl, gates = (piece(i) for i in range(len(sizes)))
    small = jnp.concatenate([wl, al, gl, zeros(SM_GL[1] - RWKV_GATE_RANK),
                             dec, zeros(SM_DEC[1] - GLA_DECAY_RANK)], axis=1)
    out = jnp.concatenate([gates, rkv, v, r, lx, lg, q, k, small], axis=1)
    return out[:-1].astype(BF16), out[-1:]


def _block_diag(blocks):
    n, bi, bo = blocks.shape
    eye = jnp.eye(n, dtype=blocks.dtype)
    return (eye[:, None, :, None] * blocks[:, :, None, :]).reshape(n * bi, n * bo)


def kernel(x, w_in, b_in, gla_w_decay_up, gla_b_decay, gla_norm_g, gla_norm_b, lru_conv_w, lru_conv_b, lru_w_r, lru_b_r, lru_w_i, lru_b_i, lru_lambda, rwkv_mu, rwkv_w0, rwkv_w2, rwkv_a0, rwkv_a2, rwkv_g2, rwkv_k_k, rwkv_k_a, rwkv_r_k, rwkv_lnx_g, rwkv_lnx_b, p_gla, p_lru, p_rwkv, w_out, ln_mix_g, ln_mix_b, ffn_w_gate, ffn_w_up, ffn_w_down, moe_w_router, moe_w_gate, moe_w_up, moe_w_down, ln_ffn_g, ln_ffn_b):
    batch, seq, d = x.shape
    t = batch * seq
    xf = x.reshape(t, d)
    xb = xf.astype(BF16)
    row = lambda a: a.reshape(1, -1)
    head_ones = _block_diag(jnp.ones((RWKV_HEADS, RWKV_HEAD, RWKV_HEAD), BF16))
    for l in range(DEPTH):
        w_l, b_l = _reorder_in_projection(w_in[l], b_in[l])
        h = _in_projection(xb, w_l, b_l)

        wup = _pad_rows(gla_w_decay_up[l], SM_DEC[1]).astype(BF16)
        o_gla = _gla_branch(h, batch, seq, wup, row(gla_b_decay[l]), row(gla_norm_g[l]), row(gla_norm_b[l]))

        w_ri = jnp.concatenate([_block_diag(lru_w_r[l]), _block_diag(lru_w_i[l])], axis=1).astype(BF16)
        b_ri = jnp.concatenate([lru_b_r[l], lru_b_i[l]])[None, :]
        o_lru = _lru_branch(h, batch, seq, lru_conv_w[l], row(lru_conv_b[l]), w_ri, b_ri,
                            row(jax.nn.softplus(-lru_lambda[l])))

        mu = rwkv_mu[l]
        nr = 3 * RWKV_WIDTH
        mu_small = jnp.zeros((COL_SMALL[1],), F32).at[0:mu.shape[0] - nr].set(mu[nr:])
        prep_params = (row(mu[:nr]), row(mu_small),
                       _pad_rows(rwkv_w2[l], SM_WA[1], 0).astype(BF16),
                       _pad_rows(rwkv_a2[l], SM_WA[1], RWKV_DECAY_RANK).astype(BF16),
                       _pad_rows(rwkv_g2[l], SM_GL[1], 0).astype(BF16),
                       row(rwkv_w0[l]), row(rwkv_a0[l]), row(rwkv_k_k[l]), row(rwkv_k_a[l]),
                       row(rwkv_r_k[l]), head_ones)
        prep = _rwkv_prep(h, batch, seq, prep_params)
        o_rwkv = _rwkv_scan(prep, batch, seq, row(rwkv_lnx_g[l]), row(rwkv_lnx_b[l]))

        xf, xb = _merge(h, o_gla, o_lru, o_rwkv, xf, p_gla[l].astype(BF16), p_lru[l].astype(BF16),
                        p_rwkv[l].astype(BF16), w_out[l].astype(BF16), row(ln_mix_g[l]), row(ln_mix_b[l]))
        i = l // 2
        if l % 2 == 0:
            xf, xb = _dense_ffn(xb, xf, ffn_w_gate[i].astype(BF16), ffn_w_up[i].astype(BF16),
                                ffn_w_down[i].astype(BF16), row(ln_ffn_g[l]), row(ln_ffn_b[l]))
        else:
            xf, xb = _moe_ffn(xf, moe_w_router[i], moe_w_gate[i].astype(BF16), moe_w_up[i].astype(BF16),
                              moe_w_down[i].astype(BF16), row(ln_ffn_g[l]), row(ln_ffn_b[l]))
    return xf.reshape(batch, seq, d)
```

```python
import functools

import jax
import jax.numpy as jnp
from jax import lax
from jax.experimental import pallas as pl
from jax.experimental.pallas import tpu as pltpu

F32 = jnp.float32
BF16 = jnp.bfloat16
I32 = jnp.int32
HIGHEST = lax.Precision.HIGHEST

D_MODEL = 1024
DEPTH = 2
GLA_HEADS, GLA_DK, GLA_DV = 4, 64, 128
GLA_KW, GLA_VW = GLA_HEADS * GLA_DK, GLA_HEADS * GLA_DV
GLA_DECAY_RANK = 16
GLA_TEMP = 16.0
LRU_WIDTH, LRU_BLOCKS, LRU_CONV, LRU_C = 512, 8, 4, 8.0
RWKV_HEAD, RWKV_WIDTH = 64, 512
RWKV_HEADS = RWKV_WIDTH // RWKV_HEAD
RWKV_DECAY_RANK, RWKV_A_RANK, RWKV_GATE_RANK = 64, 64, 160
RWKV_LNX_EPS = 64e-5
N_BRANCH = 3
N_EXPERTS, TOP_K = 8, 2
DEEPNORM_ALPHA = (2 * DEPTH) ** 0.25
LN_EPS = 1e-5

LANES = 128
SUBLANES = 8
VMEM_LIMIT_BYTES = 56 * 1024 * 1024

COL_RW_RKV = (0, 1536)
COL_GLA_V = (1536, 512)
COL_GLA_R = (2048, 512)
COL_LRU_X = (2560, 512)
COL_LRU_G = (3072, 512)
COL_GLA_Q = (3584, 256)
COL_GLA_K = (3840, 256)
COL_SMALL = (4096, 512)
N_IN_PAD = 4608
SM_WA = (0, 128)
SM_GL = (128, 256)
SM_DEC = (384, 128)

CHUNK = 64
INV_BLOCK = 16
MOE_ROWS = 512


def _cparams(*sem):
    return pltpu.CompilerParams(dimension_semantics=sem, vmem_limit_bytes=VMEM_LIMIT_BYTES)


def _sigmoid(x):
    return 1.0 / (1.0 + jnp.exp(-x))


def _softplus(x):
    return jnp.maximum(x, 0.0) + jnp.log1p(jnp.exp(-jnp.abs(x)))


def _bdot(a, b):
    return jnp.dot(a.astype(BF16), b.astype(BF16), preferred_element_type=F32)


def _bdot_nt(a, b):
    return lax.dot_general(a.astype(BF16), b.astype(BF16), (((1,), (1,)), ((), ())),
                           preferred_element_type=F32)


def _bdot_tn(a, b):
    return lax.dot_general(a.astype(BF16), b.astype(BF16), (((0,), (0,)), ((), ())),
                           preferred_element_type=F32)


def _fdot(a, b):
    return jnp.dot(a, b, precision=HIGHEST, preferred_element_type=F32)


def _split3(x):
    hi = x.astype(BF16)
    rest = x - hi.astype(F32)
    mid = rest.astype(BF16)
    lo = (rest - mid.astype(F32)).astype(BF16)
    return hi, mid, lo


def _mask_dot(mask_bf16, x):
    hi, mid, lo = _split3(x)
    dot = lambda part: jnp.dot(mask_bf16, part, preferred_element_type=F32)
    return dot(lo) + dot(mid) + dot(hi)


def _dot_mask(x, mask_bf16):
    hi, mid, lo = _split3(x)
    dot = lambda part: jnp.dot(part, mask_bf16, preferred_element_type=F32)
    return dot(lo) + dot(mid) + dot(hi)


def _layer_norm_rows(z, g, b, eps):
    mu = jnp.mean(z, axis=-1, keepdims=True)
    zc = z - mu
    var = jnp.mean(zc * zc, axis=-1, keepdims=True)
    return zc * lax.rsqrt(var + eps) * g + b


def _tri_masks(n):
    row = lax.broadcasted_iota(I32, (n, n), 0)
    col = lax.broadcasted_iota(I32, (n, n), 1)
    return row, col


def _proj_kernel(x_ref, w_ref, b_ref, o_ref):
    o_ref[...] = jnp.dot(x_ref[...], w_ref[...], preferred_element_type=F32) + b_ref[...]


def _in_projection(xb, w, b):
    t, d = xb.shape
    n = w.shape[1]
    tm = min(1024, t)
    tn = 1536
    return pl.pallas_call(
        _proj_kernel,
        out_shape=jax.ShapeDtypeStruct((t, n), F32),
        grid=(t // tm, n // tn),
        in_specs=[pl.BlockSpec((tm, d), lambda i, j: (i, 0)),
                  pl.BlockSpec((d, tn), lambda i, j: (0, j)),
                  pl.BlockSpec((1, tn), lambda i, j: (0, j))],
        out_specs=pl.BlockSpec((tm, tn), lambda i, j: (i, j)),
        compiler_params=_cparams("parallel", "arbitrary"),
    )(xb, w, b)


def _gla_kernel(q_ref, k_ref, v_ref, r_ref, sm_ref, wup_ref, bdec_ref, ng_ref, nb_ref,
                o_ref, st_ref, *, chunks):
    @pl.when(pl.program_id(1) == 0)
    def _():
        st_ref[...] = jnp.zeros_like(st_ref)

    row, col = _tri_masks(CHUNK)
    ltri = (row >= col).astype(BF16)
    kw, vw = 2 * GLA_DK, 2 * GLA_DV
    key_shift, val_shift = GLA_DK.bit_length() - 1, GLA_DV.bit_length() - 1
    iota = lambda shape, axis: lax.broadcasted_iota(I32, shape, axis)
    causal = iota((CHUNK, kw), 0) >= (iota((CHUNK, kw), 1) & (GLA_DK - 1))
    key_diag = (iota((kw, kw), 0) >> key_shift) == (iota((kw, kw), 1) >> key_shift)
    val_diag = (iota((kw, vw), 0) >> key_shift) == (iota((kw, vw), 1) >> val_shift)
    state_diag = (iota((vw, kw), 0) >> val_shift) == (iota((vw, kw), 1) >> key_shift)
    zero = jnp.zeros((), BF16)
    pairs = range(GLA_HEADS // 2)
    qp, vp, k_diag, kep, decays = [], [], [], [], []
    for c in range(chunks):
        rows = pl.ds(c * CHUNK, CHUNK)
        dec = sm_ref[rows, SM_DEC[0]:SM_DEC[0] + SM_DEC[1]]
        logits = _bdot(dec, wup_ref[...]) + bdec_ref[...]
        log_a = -_softplus(-logits) * (1.0 / GLA_TEMP)
        cum = _mask_dot(ltri, log_a)
        cum_end = cum[CHUNK - 1:CHUNK, :]
        q = q_ref[rows, :] * (GLA_DK ** -0.5)
        k = k_ref[rows, :]
        q_dec = (q * jnp.exp(cum)).astype(BF16)
        k_inv = (k * jnp.exp(-cum)).astype(BF16)
        k_end = (k * jnp.exp(cum_end - cum)).astype(BF16)
        decay_end = jnp.exp(cum_end)
        v = v_ref[rows, :].astype(BF16)
        for p in pairs:
            s = slice(p * kw, (p + 1) * kw)
            qp.append(q_dec[:, s])
            vp.append(v[:, p * vw:(p + 1) * vw])
            k_diag.append(jnp.where(key_diag, jnp.concatenate([k_inv[:, s], k_inv[:, s]], axis=0), zero))
            kep.append(k_end[:, s])
            decays.append(decay_end[:, s])

    v_diag = [jnp.where(val_diag, jnp.concatenate([x, x], axis=0), zero) for x in vp]
    scores = [jnp.where(causal, _bdot_nt(q, kd), 0.0) for q, kd in zip(qp, k_diag)]
    o_local = [_bdot(s, vd) for s, vd in zip(scores, v_diag)]
    upd = [jnp.where(state_diag, _bdot_tn(x, ke), 0.0) for x, ke in zip(vp, kep)]

    state = [st_ref[p] for p in pairs]
    n_pairs = len(pairs)
    for c in range(chunks):
        rows = pl.ds(c * CHUNK, CHUNK)
        item = slice(c * n_pairs, (c + 1) * n_pairs)
        o = [ol + _bdot_nt(q, st) for ol, q, st in zip(o_local[item], qp[item], state)]
        state = [st * dcy + u for st, dcy, u in zip(state, decays[item], upd[item])]
        outs = []
        for op in o:
            for oh in (op[:, :GLA_DV], op[:, GLA_DV:]):
                mu = jnp.mean(oh, axis=-1, keepdims=True)
                oc = oh - mu
                var = jnp.mean(oc * oc, axis=-1, keepdims=True)
                outs.append(oc * lax.rsqrt(var + LN_EPS))
        y = jnp.concatenate(outs, axis=1) * ng_ref[...] + nb_ref[...]
        r = r_ref[rows, :]
        o_ref[rows, :] = (y * (r * _sigmoid(r))).astype(o_ref.dtype)
    for p in pairs:
        st_ref[p] = state[p]


def _gla_branch(h, batch, seq, wup_pad, b_decay, norm_g, norm_b):
    t = batch * seq
    chunks = 8
    rows = CHUNK * chunks
    nblk = seq // rows

    def col_spec(col):
        off, width = col
        return pl.BlockSpec((rows, width), lambda b, n: (b * nblk + n, off // width))

    def full(a):
        return pl.BlockSpec(a.shape, lambda b, n: (0,) * a.ndim)

    args = (wup_pad, b_decay, norm_g, norm_b)
    return pl.pallas_call(
        functools.partial(_gla_kernel, chunks=chunks),
        out_shape=jax.ShapeDtypeStruct((t, GLA_VW), BF16),
        grid=(batch, nblk),
        in_specs=[col_spec(COL_GLA_Q), col_spec(COL_GLA_K), col_spec(COL_GLA_V),
                  col_spec(COL_GLA_R), col_spec(COL_SMALL)] + [full(a) for a in args],
        out_specs=pl.BlockSpec((rows, GLA_VW), lambda b, n: (b * nblk + n, 0)),
        scratch_shapes=[pltpu.VMEM((GLA_HEADS // 2, 2 * GLA_DV, 2 * GLA_DK), F32)],
        compiler_params=_cparams("parallel", "arbitrary"),
    )(h, h, h, h, h, *args)


def _lru_kernel(x_ref, g_ref, cw_ref, cb_ref, wri_ref, bri_ref, sp_ref,
                o_ref, xbuf_ref, a_ref, u_ref, hs_ref, h_ref):
    tm = x_ref.shape[0]
    pad = SUBLANES

    @pl.when(pl.program_id(1) == 0)
    def _():
        xbuf_ref[0:pad, :] = jnp.zeros((pad, LRU_WIDTH), F32)
        h_ref[...] = jnp.zeros_like(h_ref)

    x = x_ref[...]
    xbuf_ref[pad:pad + tm, :] = x
    xc = cb_ref[...] + x * cw_ref[LRU_CONV - 1:LRU_CONV, :]
    for j in range(LRU_CONV - 1):
        back = LRU_CONV - 1 - j
        xc = xc + xbuf_ref[pl.ds(pad - back, tm), :] * cw_ref[j:j + 1, :]
    xbuf_ref[0:pad, :] = x[tm - pad:tm, :]

    ri = _sigmoid(_bdot(xc, wri_ref[...]) + bri_ref[...])
    rg, ig = ri[:, :LRU_WIDTH], ri[:, LRU_WIDTH:]
    log_a = -LRU_C * rg * sp_ref[...]
    a_ref[...] = jnp.exp(log_a)
    u_ref[...] = jnp.sqrt(1.0 - jnp.exp(2.0 * log_a)) * (ig * xc)

    def step(t, hprev):
        hnew = a_ref[pl.ds(t, 1), :] * hprev + u_ref[pl.ds(t, 1), :]
        hs_ref[pl.ds(t, 1), :] = hnew
        return hnew

    h_ref[...] = lax.fori_loop(0, tm, step, h_ref[...], unroll=8)
    o_ref[...] = (jax.nn.gelu(g_ref[...]) * hs_ref[...]).astype(o_ref.dtype)


def _lru_branch(h, batch, seq, conv_w, conv_b, w_ri, b_ri, softplus_neg_lam):
    t = batch * seq
    tm = min(512, seq)
    nblk = seq // tm

    def col_spec(col):
        off, width = col
        return pl.BlockSpec((tm, width), lambda b, n: (b * nblk + n, off // width))

    def full(a):
        return pl.BlockSpec(a.shape, lambda b, n: (0,) * a.ndim)

    args = (conv_w, conv_b, w_ri, b_ri, softplus_neg_lam)
    return pl.pallas_call(
        _lru_kernel,
        out_shape=jax.ShapeDtypeStruct((t, LRU_WIDTH), BF16),
        grid=(batch, nblk),
        in_specs=[col_spec(COL_LRU_X), col_spec(COL_LRU_G)] + [full(a) for a in args],
        out_specs=pl.BlockSpec((tm, LRU_WIDTH), lambda b, n: (b * nblk + n, 0)),
        scratch_shapes=[pltpu.VMEM((tm + SUBLANES, LRU_WIDTH), F32),
                        pltpu.VMEM((tm, LRU_WIDTH), F32),
                        pltpu.VMEM((tm, LRU_WIDTH), F32),
                        pltpu.VMEM((tm, LRU_WIDTH), F32),
                        pltpu.VMEM((1, LRU_WIDTH), F32)],
        compiler_params=_cparams("parallel", "arbitrary"),
    )(h, h, *args)


def _rwkv_prep_kernel(p_ref, pprev_ref, sm_ref, smprev_ref, mu_ref, musm_ref, w2_ref, a2_ref, g2_ref,
                      w0_ref, a0_ref, kk_ref, ka_ref, rk_ref, bd_ref,
                      r_out, k_out, v_out, lw_out, kk_out, kka_out, bonus_out, g_out, *, blocks_per_seq):
    tm = p_ref.shape[0]
    first = (pl.program_id(0) % blocks_per_seq) == 0
    keep = jnp.where(first, 0.0, 1.0)
    row0 = lax.broadcasted_iota(I32, (tm, 1), 0) == 0

    def token_shift(cur_ref, prev_ref, m_ref):
        cur = cur_ref[...]
        prev_row = prev_ref[SUBLANES - 1:SUBLANES, :] * keep
        prev = jnp.where(row0, prev_row, pltpu.roll(cur, 1, axis=0))
        return cur + (prev - cur) * m_ref[...]

    p = token_shift(p_ref, pprev_ref, mu_ref)
    sm = token_shift(sm_ref, smprev_ref, musm_ref)
    r = p[:, 0:RWKV_WIDTH]
    k = p[:, RWKV_WIDTH:2 * RWKV_WIDTH]
    v = p[:, 2 * RWKV_WIDTH:3 * RWKV_WIDTH]
    wa = sm[:, SM_WA[0]:SM_WA[0] + SM_WA[1]]
    gl = sm[:, SM_GL[0]:SM_GL[0] + SM_GL[1]]

    w_log = -_softplus(-(w0_ref[...] + _bdot(jnp.tanh(wa), w2_ref[...]))) - 0.5
    lw_out[...] = -jnp.exp(w_log)
    a = _sigmoid(a0_ref[...] + _bdot(wa, a2_ref[...]))
    g_out[...] = _bdot(_sigmoid(gl), g2_ref[...])

    kk = k * kk_ref[...]
    norm = jnp.sqrt(_dot_mask(kk * kk, bd_ref[...]))
    kk = kk / jnp.maximum(norm, 1e-12)
    k2 = k * (1.0 + (a - 1.0) * ka_ref[...])
    bonus_out[...] = _dot_mask(r * k2 * rk_ref[...], bd_ref[...]) * v
    r_out[...] = r
    k_out[...] = k2
    v_out[...] = v
    kk_out[...] = kk
    kka_out[...] = kk * a


def _rwkv_prep(h, batch, seq, params):
    t = batch * seq
    tm = min(512, seq)
    bps = seq // tm
    sub_per_tile = tm // SUBLANES

    def cur_spec(col):
        off, width = col
        return pl.BlockSpec((tm, width), lambda i: (i, off // width))

    def prev_spec(col):
        off, width = col
        return pl.BlockSpec((SUBLANES, width),
                            lambda i: (jnp.maximum(i * sub_per_tile - 1, 0), off // width))

    def full(a):
        return pl.BlockSpec(a.shape, lambda i: (0,) * a.ndim)

    out = jax.ShapeDtypeStruct((t, RWKV_WIDTH), F32)
    ospec = pl.BlockSpec((tm, RWKV_WIDTH), lambda i: (i, 0))
    return pl.pallas_call(
        functools.partial(_rwkv_prep_kernel, blocks_per_seq=bps),
        out_shape=[out] * 8,
        grid=(t // tm,),
        in_specs=[cur_spec(COL_RW_RKV), prev_spec(COL_RW_RKV), cur_spec(COL_SMALL), prev_spec(COL_SMALL)]
        + [full(a) for a in params],
        out_specs=[ospec] * 8,
        compiler_params=_cparams("parallel"),
    )(h, h, h, h, *params)


PAIR_W = 2 * RWKV_HEAD
N_PAIRS = RWKV_HEADS // 2
HEAD_SHIFT = RWKV_HEAD.bit_length() - 1


def _pair_masks():
    t = lax.broadcasted_iota(I32, (CHUNK, PAIR_W), 0)
    lane = lax.broadcasted_iota(I32, (CHUNK, PAIR_W), 1)
    j = lane & (RWKV_HEAD - 1)
    r = lax.broadcasted_iota(I32, (PAIR_W, PAIR_W), 0)
    c = lax.broadcasted_iota(I32, (PAIR_W, PAIR_W), 1)
    r4 = lax.broadcasted_iota(I32, (2 * PAIR_W, PAIR_W), 0)
    c4 = lax.broadcasted_iota(I32, (2 * PAIR_W, PAIR_W), 1)
    return dict(
        strict=t > j, incl=t >= j, eye=(t == j).astype(F32),
        same_block=jnp.bitwise_xor(t, j) < INV_BLOCK,
        first_head=lane < RWKV_HEAD,
        block_diag=(r >> HEAD_SHIFT) == (c >> HEAD_SHIFT),
        stacked_diag=((r4 >> HEAD_SHIFT) & 1) == (c4 >> HEAD_SHIFT))


def _rwkv_scan_kernel(r_ref, k_ref, v_ref, lw_ref, kk_ref, kka_ref, bonus_ref, g_ref, lg_ref, lb_ref,
                      o_ref, st_ref, *, chunks):
    @pl.when(pl.program_id(1) == 0)
    def _():
        st_ref[...] = jnp.zeros_like(st_ref)

    row, col = _tri_masks(CHUNK)
    ltri = (row >= col).astype(BF16)
    m = _pair_masks()
    zero = jnp.zeros((), BF16)
    pairs = range(N_PAIRS)
    sl = [slice(p * PAIR_W, (p + 1) * PAIR_W) for p in pairs]

    def block_diag(x):
        xb = x.astype(BF16)
        return jnp.where(m["block_diag"], jnp.concatenate([xb, xb], axis=0), zero)

    def pair_mul(lhs, *rhs):
        rb = jnp.concatenate([block_diag(x) for x in rhs], axis=1) if len(rhs) > 1 else block_diag(rhs[0])
        out = jnp.dot(lhs.astype(BF16), rb, preferred_element_type=F32)
        return [out[:, i * PAIR_W:(i + 1) * PAIR_W] for i in range(len(rhs))]

    def each(fn, *lists):
        return [fn(*args) for args in zip(*lists)]

    lhs, rhs, vp, ends, gammas = [], [], [], [], []
    for c in range(chunks):
        rows = pl.ds(c * CHUNK, CHUNK)
        lw = lw_ref[rows, :]
        cum = _mask_dot(ltri, lw)
        cum_end = cum[CHUNK - 1:CHUNK, :]
        e_inv = jnp.exp(-cum)
        e_end = jnp.exp(cum_end - cum)
        kka = kka_ref[rows, :]
        k = k_ref[rows, :]
        a_bar = (-kk_ref[rows, :] * jnp.exp(cum - lw)).astype(BF16)
        r_bar = (r_ref[rows, :] * jnp.exp(cum)).astype(BF16)
        b_til = (kka * e_inv).astype(BF16)
        k_til = (k * e_inv).astype(BF16)
        b_end = (kka * e_end).astype(BF16)
        k_end = (k * e_end).astype(BF16)
        gamma = jnp.exp(cum_end)
        v = v_ref[rows, :].astype(BF16)

        for s in sl:
            lhs.append(jnp.concatenate([a_bar[:, s], r_bar[:, s]], axis=0))
            rhs.append(jnp.where(m["stacked_diag"], jnp.concatenate(
                [b_til[:, s], b_til[:, s], k_til[:, s], k_til[:, s]], axis=0), zero))
            vp.append(v[:, s])
            ends.append(jnp.concatenate([b_end[:, s], k_end[:, s]], axis=0))
            gammas.append(gamma[:, s])

    blocks = each(_bdot_nt, lhs, rhs)
    a_ab = [jnp.where(m["strict"], b[:CHUNK, :PAIR_W], 0.0) for b in blocks]
    a_ak = [jnp.where(m["strict"], b[:CHUNK, PAIR_W:], 0.0) for b in blocks]
    r_ab = [jnp.where(m["incl"], b[CHUNK:, :PAIR_W], 0.0) for b in blocks]
    r_ak = [jnp.where(m["incl"], b[CHUNK:, PAIR_W:], 0.0) for b in blocks]
    from_v = [jnp.dot(jnp.concatenate([ak, rk], axis=0).astype(BF16), block_diag(x),
                      preferred_element_type=F32) for ak, rk, x in zip(a_ak, r_ak, vp)]
    d = [jnp.where(m["same_block"], a, 0.0) for a in a_ab]
    low = [a - di for a, di in zip(a_ab, d)]
    pm = [m["eye"] + di for di in d]
    dpow = [pair_mul(di, di)[0] for di in d]
    for _ in range(INV_BLOCK.bit_length() - 3):
        both = each(lambda x, p: pair_mul(x, x, p), dpow, pm)
        dpow = [b[0] for b in both]
        pm = [p + b[1] for p, b in zip(pm, both)]
    pm = [p + pair_mul(x, p)[0] for p, x in zip(pm, dpow)]
    both = each(lambda p, lo, fv, l: pair_mul(p, lo, fv[:CHUNK], l[:CHUNK]), pm, low, from_v, lhs)
    both2 = each(lambda b: pair_mul(b[0], b[0], b[1], b[2]), both)
    xw_v = [b[1] + b2[1] for b, b2 in zip(both, both2)]
    xw_a = [b[2] + b2[2] for b, b2 in zip(both, both2)]
    both3 = each(lambda b2, xv, xa: pair_mul(b2[0], xv, xa), both2, xw_v, xw_a)
    u_v = [xv + b3[0] for xv, b3 in zip(xw_v, both3)]
    wm = [xa + b3[1] for xa, b3 in zip(xw_a, both3)]
    both4 = each(lambda rb, uv, w: pair_mul(rb, uv, w), r_ab, u_v, wm)
    y_local = [fv[CHUNK:] + b4[0] for fv, b4 in zip(from_v, both4)]
    rm = [l[CHUNK:] + b4[1] for l, b4 in zip(lhs, both4)]
    gm = [jnp.where(m["block_diag"], _bdot_tn(en[:CHUNK], w), 0.0) for en, w in zip(ends, wm)]
    qm = [jnp.where(m["block_diag"], _bdot_tn(jnp.concatenate([uv.astype(BF16), x], axis=0), en), 0.0)
          for uv, x, en in zip(u_v, vp, ends)]

    state = [st_ref[p] for p in pairs]
    for c in range(chunks):
        rows = pl.ds(c * CHUNK, CHUNK)
        item = slice(c * N_PAIRS, (c + 1) * N_PAIRS)
        y = [yl + _bdot_nt(r, st) for yl, r, st in zip(y_local[item], rm[item], state)]
        state = [st * g + _bdot_nt(st, gmat) + q
                 for st, g, gmat, q in zip(state, gammas[item], gm[item], qm[item])]
        outs = []
        inv_n = 1.0 / RWKV_HEAD
        for yp in y:
            def head_stat(z):
                s0 = jnp.sum(jnp.where(m["first_head"], z, 0.0), axis=-1, keepdims=True)
                s1 = jnp.sum(jnp.where(m["first_head"], 0.0, z), axis=-1, keepdims=True)
                return jnp.where(m["first_head"], s0, s1) * inv_n
            yc = yp - head_stat(yp)
            outs.append(yc * lax.rsqrt(head_stat(yc * yc) + RWKV_LNX_EPS))
        yn = jnp.concatenate(outs, axis=1) * lg_ref[...] + lb_ref[...]
        o_ref[rows, :] = ((yn + bonus_ref[rows, :]) * g_ref[rows, :]).astype(o_ref.dtype)
    for p in pairs:
        st_ref[p] = state[p]


def _rwkv_scan(prep, batch, seq, lnx_g, lnx_b):
    t = batch * seq
    chunks = 8
    rows = CHUNK * chunks
    nblk = seq // rows
    spec = pl.BlockSpec((rows, RWKV_WIDTH), lambda b, n: (b * nblk + n, 0))
    pspec = pl.BlockSpec((1, RWKV_WIDTH), lambda b, n: (0, 0))
    return pl.pallas_call(
        functools.partial(_rwkv_scan_kernel, chunks=chunks),
        out_shape=jax.ShapeDtypeStruct((t, RWKV_WIDTH), BF16),
        grid=(batch, nblk),
        in_specs=[spec] * 8 + [pspec, pspec],
        out_specs=spec,
        scratch_shapes=[pltpu.VMEM((N_PAIRS, PAIR_W, PAIR_W), F32)],
        compiler_params=_cparams("parallel", "arbitrary"),
    )(*prep, lnx_g, lnx_b)


def _merge_kernel(xin_ref, og_ref, ol_ref, or_ref, x_ref, wgt_ref, bgt_ref, pg_ref, pl_ref, pr_ref, wo_ref,
                  lg_ref, lb_ref, xo_ref, xb_ref):
    d = D_MODEL
    xin = xin_ref[...]
    merged = None
    for b, (o_ref, p_ref) in enumerate(((og_ref, pg_ref), (ol_ref, pl_ref), (or_ref, pr_ref))):
        cols = slice(b * d, (b + 1) * d)
        gate = _sigmoid(jnp.dot(xin, wgt_ref[:, cols], preferred_element_type=F32) + bgt_ref[:, cols])
        term = gate * jnp.dot(o_ref[...], p_ref[...], preferred_element_type=F32)
        merged = term if merged is None else merged + term
    z = DEEPNORM_ALPHA * x_ref[...] + _bdot(merged, wo_ref[...])
    out = _layer_norm_rows(z, lg_ref[...], lb_ref[...], LN_EPS)
    xo_ref[...] = out
    xb_ref[...] = out.astype(BF16)


def _merge(xb_in, o_gla, o_lru, o_rwkv, x, w_gates, b_gates, p_gla, p_lru, p_rwkv, w_out, ln_g, ln_b):
    t = x.shape[0]
    tm = min(512, t)

    def rows(width):
        return pl.BlockSpec((tm, width), lambda i: (i, 0))

    def full(a):
        return pl.BlockSpec(a.shape, lambda i: (0,) * a.ndim)

    params = (w_gates, b_gates, p_gla, p_lru, p_rwkv, w_out, ln_g, ln_b)
    return pl.pallas_call(
        _merge_kernel,
        out_shape=[jax.ShapeDtypeStruct((t, D_MODEL), F32), jax.ShapeDtypeStruct((t, D_MODEL), BF16)],
        grid=(t // tm,),
        in_specs=[rows(D_MODEL), rows(GLA_VW), rows(LRU_WIDTH), rows(RWKV_WIDTH), rows(D_MODEL)]
        + [full(a) for a in params],
        out_specs=[rows(D_MODEL), rows(D_MODEL)],
        compiler_params=_cparams("parallel"),
    )(xb_in, o_gla, o_lru, o_rwkv, x, *params)


def _ffn_kernel(xb_ref, x_ref, wg_ref, wu_ref, wd_ref, lg_ref, lb_ref, xo_ref, xbo_ref, acc_ref):
    f = pl.program_id(1)

    @pl.when(f == 0)
    def _():
        acc_ref[...] = jnp.zeros_like(acc_ref)

    xb = xb_ref[...]
    g = jnp.dot(xb, wg_ref[...], preferred_element_type=F32)
    u = jnp.dot(xb, wu_ref[...], preferred_element_type=F32)
    acc_ref[...] += _bdot(g * _sigmoid(g) * u, wd_ref[...])

    @pl.when(f == pl.num_programs(1) - 1)
    def _():
        z = DEEPNORM_ALPHA * x_ref[...] + acc_ref[...]
        out = _layer_norm_rows(z, lg_ref[...], lb_ref[...], LN_EPS)
        xo_ref[...] = out
        xbo_ref[...] = out.astype(BF16)


def _dense_ffn(xb, x, wg, wu, wd, ln_g, ln_b):
    t = x.shape[0]
    tm = min(512, t)
    ff = wg.shape[1]
    tf = ff // 2
    rows = lambda width: pl.BlockSpec((tm, width), lambda i, f: (i, 0))
    vec = pl.BlockSpec((1, D_MODEL), lambda i, f: (0, 0))
    return pl.pallas_call(
        _ffn_kernel,
        out_shape=[jax.ShapeDtypeStruct((t, D_MODEL), F32), jax.ShapeDtypeStruct((t, D_MODEL), BF16)],
        grid=(t // tm, ff // tf),
        in_specs=[rows(D_MODEL), rows(D_MODEL),
                  pl.BlockSpec((D_MODEL, tf), lambda i, f: (0, f)),
                  pl.BlockSpec((D_MODEL, tf), lambda i, f: (0, f)),
                  pl.BlockSpec((tf, D_MODEL), lambda i, f: (f, 0)), vec, vec],
        out_specs=[rows(D_MODEL), rows(D_MODEL)],
        scratch_shapes=[pltpu.VMEM((tm, D_MODEL), F32)],
        compiler_params=_cparams("parallel", "arbitrary"),
    )(xb, x, wg, wu, wd, ln_g, ln_b)


def _router_kernel(x_ref, wr_ref, meta_ref, wts_ref, cnt_ref, carry_ref):
    tm = x_ref.shape[0]

    @pl.when(pl.program_id(0) == 0)
    def _():
        carry_ref[...] = jnp.zeros_like(carry_ref)

    lane = lax.broadcasted_iota(I32, (tm, LANES), 1)
    neg = jnp.float32(-jnp.inf)
    logits = jnp.where(lane < N_EXPERTS, _fdot(x_ref[...], wr_ref[...]), neg)
    m1 = jnp.max(logits, axis=-1, keepdims=True)
    e1 = jnp.min(jnp.where(logits == m1, lane, LANES), axis=-1, keepdims=True)
    rest = jnp.where(lane == e1, neg, logits)
    m2 = jnp.max(rest, axis=-1, keepdims=True)
    e2 = jnp.min(jnp.where(rest == m2, lane, LANES), axis=-1, keepdims=True)
    ex = jnp.exp(m2 - m1)
    w1 = 1.0 / (1.0 + ex)
    w2 = ex / (1.0 + ex)

    hot1 = lane == e1
    hot2 = lane == e2
    onehot = jnp.where(hot1 | hot2, 1.0, 0.0)
    row, col = _tri_masks(tm)
    before = (row > col).astype(BF16)
    prefix = jnp.dot(before, onehot.astype(BF16), preferred_element_type=F32) + carry_ref[...]
    rank1 = jnp.sum(jnp.where(hot1, prefix, 0.0), axis=-1, keepdims=True).astype(I32)
    rank2 = jnp.sum(jnp.where(hot2, prefix, 0.0), axis=-1, keepdims=True).astype(I32)
    carry_ref[...] += jnp.sum(onehot, axis=0, keepdims=True)
    cnt_ref[...] = carry_ref[...]

    meta = jnp.where(lane == 0, e1, jnp.where(lane == 1, e2, jnp.where(lane == 2, rank1, rank2)))
    meta_ref[...] = jnp.where(lane < 4, meta, 0)
    wts_ref[...] = jnp.where(lane == 0, w1, jnp.where(lane == 1, w2, 0.0))


def _router(x, wr_pad):
    t = x.shape[0]
    tm = min(512, t)
    return pl.pallas_call(
        _router_kernel,
        out_shape=[jax.ShapeDtypeStruct((t, LANES), I32), jax.ShapeDtypeStruct((t, LANES), F32),
                   jax.ShapeDtypeStruct((1, LANES), F32)],
        grid=(t // tm,),
        in_specs=[pl.BlockSpec((tm, D_MODEL), lambda i: (i, 0)),
                  pl.BlockSpec((D_MODEL, LANES), lambda i: (0, 0))],
        out_specs=[pl.BlockSpec((tm, LANES), lambda i: (i, 0)), pl.BlockSpec((tm, LANES), lambda i: (i, 0)),
                   pl.BlockSpec((1, LANES), lambda i: (0, 0))],
        scratch_shapes=[pltpu.VMEM((1, LANES), F32)],
        compiler_params=_cparams("arbitrary"),
    )(x, wr_pad)


def _dispatch_kernel(dest_ref, x_ref, xs_in_ref, xs_ref, sem):
    del xs_in_ref
    tm = x_ref.shape[0]

    def row_copy(r, j):
        return pltpu.make_async_copy(x_ref.at[pl.ds(r, 1)], xs_ref.at[pl.ds(dest_ref[TOP_K * r + j], 1)], sem)

    def start(r, c):
        for j in range(TOP_K):
            row_copy(r, j).start()
        return c

    lax.fori_loop(0, tm, start, 0, unroll=8)
    for j in range(TOP_K):
        pltpu.make_async_copy(x_ref, xs_ref.at[pl.ds(0, tm)], sem).wait()


def _dispatch(x, dest_flat, n_slots):
    t = x.shape[0]
    tm = min(256, t)
    xs0 = jnp.zeros((n_slots, D_MODEL), F32)
    return pl.pallas_call(
        _dispatch_kernel,
        out_shape=jax.ShapeDtypeStruct((n_slots, D_MODEL), F32),
        grid=(t // tm,),
        in_specs=[pl.BlockSpec((tm * TOP_K,), lambda i: (i,), memory_space=pltpu.SMEM),
                  pl.BlockSpec((tm, D_MODEL), lambda i: (i, 0)),
                  pl.BlockSpec(memory_space=pl.ANY)],
        out_specs=pl.BlockSpec(memory_space=pl.ANY),
        scratch_shapes=[pltpu.SemaphoreType.DMA(())],
        input_output_aliases={2: 0},
        compiler_params=_cparams("arbitrary"),
    )(dest_flat, x, xs0)


def _expert_kernel(blk_e_ref, nused_ref, xs_ref, wg_ref, wu_ref, wd_ref, ys_ref, acc_ref):
    del blk_e_ref
    i = pl.program_id(0)
    f = pl.program_id(1)
    last = pl.num_programs(1) - 1
    used = i < nused_ref[0]

    @pl.when(used & (f == 0))
    def _():
        acc_ref[...] = jnp.zeros_like(acc_ref)

    @pl.when(used)
    def _():
        xb = xs_ref[...].astype(BF16)
        g = jnp.dot(xb, wg_ref[0], preferred_element_type=F32)
        u = jnp.dot(xb, wu_ref[0], preferred_element_type=F32)
        acc_ref[...] += _bdot(g * _sigmoid(g) * u, wd_ref[0])

    @pl.when(used & (f == last))
    def _():
        ys_ref[...] = acc_ref[...]

    @pl.when(jnp.logical_not(used) & (f == last))
    def _():
        ys_ref[...] = jnp.zeros_like(ys_ref)


def _experts(xs, blk_e, n_used, wg, wu, wd):
    n_slots = xs.shape[0]
    n_blocks = n_slots // MOE_ROWS
    ff = wg.shape[2]
    tf = ff // 2
    grid_spec = pltpu.PrefetchScalarGridSpec(
        num_scalar_prefetch=2,
        grid=(n_blocks, ff // tf),
        in_specs=[pl.BlockSpec((MOE_ROWS, D_MODEL), lambda i, f, be, nu: (i, 0)),
                  pl.BlockSpec((1, D_MODEL, tf), lambda i, f, be, nu: (be[i], 0, f)),
                  pl.BlockSpec((1, D_MODEL, tf), lambda i, f, be, nu: (be[i], 0, f)),
                  pl.BlockSpec((1, tf, D_MODEL), lambda i, f, be, nu: (be[i], f, 0))],
        out_specs=pl.BlockSpec((MOE_ROWS, D_MODEL), lambda i, f, be, nu: (i, 0)),
        scratch_shapes=[pltpu.VMEM((MOE_ROWS, D_MODEL), F32)],
    )
    return pl.pallas_call(
        _expert_kernel,
        out_shape=jax.ShapeDtypeStruct((n_slots, D_MODEL), F32),
        grid_spec=grid_spec,
        compiler_params=_cparams("arbitrary", "arbitrary"),
    )(blk_e, n_used, xs, wg, wu, wd)


def _combine_kernel(dest_ref, x_ref, wts_ref, ys_ref, lg_ref, lb_ref, xo_ref, xbo_ref, buf_ref, sem):
    tm = x_ref.shape[0]

    def row_copy(r, j):
        return pltpu.make_async_copy(ys_ref.at[pl.ds(dest_ref[TOP_K * r + j], 1)],
                                     buf_ref.at[j, pl.ds(r, 1)], sem)

    def start(r, c):
        for j in range(TOP_K):
            row_copy(r, j).start()
        return c

    lax.fori_loop(0, tm, start, 0, unroll=8)
    for j in range(TOP_K):
        pltpu.make_async_copy(ys_ref.at[pl.ds(0, tm)], buf_ref.at[j], sem).wait()
    w = wts_ref[...]
    f = w[:, 0:1] * buf_ref[0] + w[:, 1:2] * buf_ref[1]
    z = DEEPNORM_ALPHA * x_ref[...] + f
    out = _layer_norm_rows(z, lg_ref[...], lb_ref[...], LN_EPS)
    xo_ref[...] = out
    xbo_ref[...] = out.astype(BF16)


def _combine(x, dest_flat, wts, ys, ln_g, ln_b):
    t = x.shape[0]
    tm = min(256, t)
    rows = lambda width: pl.BlockSpec((tm, width), lambda i: (i, 0))
    vec = pl.BlockSpec((1, D_MODEL), lambda i: (0, 0))
    return pl.pallas_call(
        _combine_kernel,
        out_shape=[jax.ShapeDtypeStruct((t, D_MODEL), F32), jax.ShapeDtypeStruct((t, D_MODEL), BF16)],
        grid=(t // tm,),
        in_specs=[pl.BlockSpec((tm * TOP_K,), lambda i: (i,), memory_space=pltpu.SMEM),
                  rows(D_MODEL), rows(LANES), pl.BlockSpec(memory_space=pl.ANY), vec, vec],
        out_specs=[rows(D_MODEL), rows(D_MODEL)],
        scratch_shapes=[pltpu.VMEM((TOP_K, tm, D_MODEL), F32), pltpu.SemaphoreType.DMA(())],
        compiler_params=_cparams("arbitrary"),
    )(dest_flat, x, wts, ys, ln_g, ln_b)


def _moe_ffn(x, w_router, wg, wu, wd, ln_g, ln_b):
    t = x.shape[0]
    wr_pad = jnp.pad(w_router, ((0, 0), (0, LANES - N_EXPERTS)))
    meta, wts, cnt = _router(x, wr_pad)
    counts = cnt[0, :N_EXPERTS].astype(I32)
    padded = (counts + MOE_ROWS - 1) // MOE_ROWS * MOE_ROWS
    seg_end = jnp.cumsum(padded)
    seg_start = seg_end - padded
    n_blocks = (t * TOP_K) // MOE_ROWS + N_EXPERTS
    dest = seg_start[meta[:, 0:TOP_K]] + meta[:, TOP_K:2 * TOP_K]
    dest_flat = dest.reshape(-1).astype(I32)
    blk_start = jnp.arange(n_blocks, dtype=I32) * MOE_ROWS
    blk_e = jnp.minimum(jnp.sum(blk_start[:, None] >= seg_end[None, :], axis=1), N_EXPERTS - 1).astype(I32)
    n_used = (seg_end[-1:] // MOE_ROWS).astype(I32)
    xs = _dispatch(x, dest_flat, n_blocks * MOE_ROWS)
    ys = _experts(xs, blk_e, n_used, wg, wu, wd)
    return _combine(x, dest_flat, wts, ys, ln_g, ln_b)


def _pad_rows(w, rows, at=0):
    out = jnp.zeros((rows, w.shape[1]), w.dtype)
    return out.at[at:at + w.shape[0]].set(w)


def _reorder_in_projection(w, b):
    sizes = (GLA_KW, GLA_KW, GLA_VW, GLA_VW, GLA_DECAY_RANK, LRU_WIDTH, LRU_WIDTH,
             3 * RWKV_WIDTH, RWKV_DECAY_RANK, RWKV_A_RANK, RWKV_GATE_RANK, N_BRANCH * D_MODEL)
    offs = [0]
    for s in sizes:
        offs.append(offs[-1] + s)
    wb = jnp.concatenate([w, b[None, :]], axis=0)
    piece = lambda i: wb[:, offs[i]:offs[i + 1]]
    zeros = lambda n: jnp.zeros((wb.shape[0], n), wb.dtype)
    q, k, v, r, dec, lx, lg, rkv, wl, al, gl, gates = (piece(i) for i in range(len(sizes)))
    small = jnp.concatenate([wl, al, gl, zeros(SM_GL[1] - RWKV_GATE_RANK),
                             dec, zeros(SM_DEC[1] - GLA_DECAY_RANK)], axis=1)
    out = jnp.concatenate([rkv, v, r, lx, lg, q, k, small], axis=1)
    return out[:-1].astype(BF16), out[-1:], gates[:-1].astype(BF16), gates[-1:]


def _cast_kernel(x_ref, o_ref):
    o_ref[...] = x_ref[...].astype(o_ref.dtype)


CAST_TILE_ELEMS = 2 * 1024 * 1024


def _to_bf16(w):
    cols = w.shape[-1]
    w2 = w.reshape(-1, cols)
    n_rows = w2.shape[0]
    tr = n_rows
    while tr * cols > CAST_TILE_ELEMS and tr % 32 == 0:
        tr //= 2
    out = pl.pallas_call(
        _cast_kernel,
        out_shape=jax.ShapeDtypeStruct(w2.shape, BF16),
        grid=(n_rows // tr,),
        in_specs=[pl.BlockSpec((tr, cols), lambda i: (i, 0))],
        out_specs=pl.BlockSpec((tr, cols), lambda i: (i, 0)),
        compiler_params=_cparams("parallel"),
    )(w2)
    return out.reshape(w.shape)


def _block_diag(blocks):
    n, bi, bo = blocks.shape
    eye = jnp.eye(n, dtype=blocks.dtype)
    return (eye[:, None, :, None] * blocks[:, :, None, :]).reshape(n * bi, n * bo)


def kernel(x, w_in, b_in, gla_w_decay_up, gla_b_decay, gla_norm_g, gla_norm_b, lru_conv_w, lru_conv_b, lru_w_r, lru_b_r, lru_w_i, lru_b_i, lru_lambda, rwkv_mu, rwkv_w0, rwkv_w2, rwkv_a0, rwkv_a2, rwkv_g2, rwkv_k_k, rwkv_k_a, rwkv_r_k, rwkv_lnx_g, rwkv_lnx_b, p_gla, p_lru, p_rwkv, w_out, ln_mix_g, ln_mix_b, ffn_w_gate, ffn_w_up, ffn_w_down, moe_w_router, moe_w_gate, moe_w_up, moe_w_down, ln_ffn_g, ln_ffn_b):
    batch, seq, d = x.shape
    t = batch * seq
    xf = x.reshape(t, d)
    xb = xf.astype(BF16)
    row = lambda a: a.reshape(1, -1)
    head_ones = _block_diag(jnp.ones((RWKV_HEADS, RWKV_HEAD, RWKV_HEAD), BF16))
    for l in range(DEPTH):
        w_l, b_l, w_gates, b_gates = _reorder_in_projection(w_in[l], b_in[l])
        h = _in_projection(xb, w_l, b_l)

        wup = _pad_rows(gla_w_decay_up[l], SM_DEC[1]).astype(BF16)
        o_gla = _gla_branch(h, batch, seq, wup, row(gla_b_decay[l]), row(gla_norm_g[l]), row(gla_norm_b[l]))

        w_ri = jnp.concatenate([_block_diag(lru_w_r[l]), _block_diag(lru_w_i[l])], axis=1).astype(BF16)
        b_ri = jnp.concatenate([lru_b_r[l], lru_b_i[l]])[None, :]
        o_lru = _lru_branch(h, batch, seq, lru_conv_w[l], row(lru_conv_b[l]), w_ri, b_ri,
                            row(jax.nn.softplus(-lru_lambda[l])))

        mu = rwkv_mu[l]
        nr = 3 * RWKV_WIDTH
        mu_small = jnp.zeros((COL_SMALL[1],), F32).at[0:mu.shape[0] - nr].set(mu[nr:])
        prep_params = (row(mu[:nr]), row(mu_small),
                       _pad_rows(rwkv_w2[l], SM_WA[1], 0).astype(BF16),
                       _pad_rows(rwkv_a2[l], SM_WA[1], RWKV_DECAY_RANK).astype(BF16),
                       _pad_rows(rwkv_g2[l], SM_GL[1], 0).astype(BF16),
                       row(rwkv_w0[l]), row(rwkv_a0[l]), row(rwkv_k_k[l]), row(rwkv_k_a[l]),
                       row(rwkv_r_k[l]), head_ones)
        prep = _rwkv_prep(h, batch, seq, prep_params)
        o_rwkv = _rwkv_scan(prep, batch, seq, row(rwkv_lnx_g[l]), row(rwkv_lnx_b[l]))

        xf, xb = _merge(xb, o_gla, o_lru, o_rwkv, xf, w_gates, b_gates, p_gla[l].astype(BF16), p_lru[l].astype(BF16),
                        p_rwkv[l].astype(BF16), w_out[l].astype(BF16), row(ln_mix_g[l]), row(ln_mix_b[l]))
        i = l // 2
        if l % 2 == 0:
            xf, xb = _dense_ffn(xb, xf, ffn_w_gate[i].astype(BF16), ffn_w_up[i].astype(BF16),
                                ffn_w_down[i].astype(BF16), row(ln_ffn_g[l]), row(ln_ffn_b[l]))
        else:
            xf, xb = _moe_ffn(xf, moe_w_router[i], _to_bf16(moe_w_gate[i]), _to_bf16(moe_w_up[i]),
                              _to_bf16(moe_w_down[i]), row(ln_ffn_g[l]), row(ln_ffn_b[l]))
    return xf.reshape(batch, seq, d)
```

```python
import functools

import jax
import jax.numpy as jnp
from jax import lax
from jax.experimental import pallas as pl
from jax.experimental.pallas import tpu as pltpu

F32 = jnp.float32
BF16 = jnp.bfloat16
I32 = jnp.int32
HIGHEST = lax.Precision.HIGHEST

D_MODEL = 1024
DEPTH = 2
GLA_HEADS, GLA_DK, GLA_DV = 4, 64, 128
GLA_KW, GLA_VW = GLA_HEADS * GLA_DK, GLA_HEADS * GLA_DV
GLA_DECAY_RANK = 16
GLA_TEMP = 16.0
LRU_WIDTH, LRU_BLOCKS, LRU_CONV, LRU_C = 512, 8, 4, 8.0
RWKV_HEAD, RWKV_WIDTH = 64, 512
RWKV_HEADS = RWKV_WIDTH // RWKV_HEAD
RWKV_DECAY_RANK, RWKV_A_RANK, RWKV_GATE_RANK = 64, 64, 160
RWKV_LNX_EPS = 64e-5
N_BRANCH = 3
N_EXPERTS, TOP_K = 8, 2
DEEPNORM_ALPHA = (2 * DEPTH) ** 0.25
LN_EPS = 1e-5

LANES = 128
SUBLANES = 8
VMEM_LIMIT_BYTES = 56 * 1024 * 1024

COL_RW_RKV = (0, 1536)
COL_GLA_V = (1536, 512)
COL_GLA_R = (2048, 512)
COL_LRU_X = (2560, 512)
COL_LRU_G = (3072, 512)
COL_GLA_Q = (3584, 256)
COL_GLA_K = (3840, 256)
COL_SMALL = (4096, 512)
N_IN_PAD = 4608
SM_WA = (0, 128)
SM_GL = (128, 256)
SM_DEC = (384, 128)

CHUNK = 64
INV_BLOCK = 16
MOE_ROWS = 512


def _cparams(*sem):
    return pltpu.CompilerParams(dimension_semantics=sem, vmem_limit_bytes=VMEM_LIMIT_BYTES)


def _sigmoid(x):
    return 1.0 / (1.0 + jnp.exp(-x))


def _softplus(x):
    return jnp.maximum(x, 0.0) + jnp.log(1.0 + jnp.exp(-jnp.abs(x)))


def _bdot(a, b):
    return jnp.dot(a.astype(BF16), b.astype(BF16), preferred_element_type=F32)


def _bdot_nt(a, b):
    return lax.dot_general(a.astype(BF16), b.astype(BF16), (((1,), (1,)), ((), ())),
                           preferred_element_type=F32)


def _bdot_tn(a, b):
    return lax.dot_general(a.astype(BF16), b.astype(BF16), (((0,), (0,)), ((), ())),
                           preferred_element_type=F32)


def _fdot(a, b):
    ah = a.astype(BF16)
    al = (a - ah.astype(F32)).astype(BF16)
    bh = b.astype(BF16)
    bl = (b - bh.astype(F32)).astype(BF16)
    dot = lambda u, v: jnp.dot(u, v, preferred_element_type=F32)
    return dot(al, bh) + dot(ah, bl) + dot(ah, bh)


def _split3(x):
    hi = x.astype(BF16)
    rest = x - hi.astype(F32)
    mid = rest.astype(BF16)
    lo = (rest - mid.astype(F32)).astype(BF16)
    return hi, mid, lo


def _mask_dot(mask_bf16, x):
    hi, mid, lo = _split3(x)
    dot = lambda part: jnp.dot(mask_bf16, part, preferred_element_type=F32)
    return dot(lo) + dot(mid) + dot(hi)


def _dot_mask(x, mask_bf16):
    hi, mid, lo = _split3(x)
    dot = lambda part: jnp.dot(part, mask_bf16, preferred_element_type=F32)
    return dot(lo) + dot(mid) + dot(hi)


def _layer_norm_rows(z, g, b, eps):
    mu = jnp.mean(z, axis=-1, keepdims=True)
    zc = z - mu
    var = jnp.mean(zc * zc, axis=-1, keepdims=True)
    return zc * lax.rsqrt(var + eps) * g + b


def _tri_masks(n):
    row = lax.broadcasted_iota(I32, (n, n), 0)
    col = lax.broadcasted_iota(I32, (n, n), 1)
    return row, col


def _proj_kernel(x_ref, w_ref, b_ref, o_ref):
    o_ref[...] = jnp.dot(x_ref[...].astype(BF16), w_ref[...], preferred_element_type=F32) + b_ref[...]


def _in_projection(x, w, b):
    t, d = x.shape
    n = w.shape[1]
    tm = min(1024, t)
    tn = 1536
    return pl.pallas_call(
        _proj_kernel,
        out_shape=jax.ShapeDtypeStruct((t, n), F32),
        grid=(t // tm, n // tn),
        in_specs=[pl.BlockSpec((tm, d), lambda i, j: (i, 0)),
                  pl.BlockSpec((d, tn), lambda i, j: (0, j)),
                  pl.BlockSpec((1, tn), lambda i, j: (0, j))],
        out_specs=pl.BlockSpec((tm, tn), lambda i, j: (i, j)),
        compiler_params=_cparams("parallel", "arbitrary"),
    )(x, w, b)


def _gla_kernel(q_ref, k_ref, v_ref, r_ref, sm_ref, wup_ref, bdec_ref, ng_ref, nb_ref,
                o_ref, st_ref, *, chunks):
    @pl.when(pl.program_id(1) == 0)
    def _():
        st_ref[...] = jnp.zeros_like(st_ref)

    row, col = _tri_masks(CHUNK)
    ltri = (row >= col).astype(BF16)
    kw, vw = 2 * GLA_DK, 2 * GLA_DV
    key_shift, val_shift = GLA_DK.bit_length() - 1, GLA_DV.bit_length() - 1
    iota = lambda shape, axis: lax.broadcasted_iota(I32, shape, axis)
    causal = iota((CHUNK, kw), 0) >= (iota((CHUNK, kw), 1) & (GLA_DK - 1))
    key_diag = (iota((kw, kw), 0) >> key_shift) == (iota((kw, kw), 1) >> key_shift)
    val_diag = (iota((kw, vw), 0) >> key_shift) == (iota((kw, vw), 1) >> val_shift)
    state_diag = (iota((vw, kw), 0) >> val_shift) == (iota((vw, kw), 1) >> key_shift)
    zero = jnp.zeros((), BF16)
    pairs = range(GLA_HEADS // 2)
    qp, vp, k_diag, kep, decays = [], [], [], [], []
    for c in range(chunks):
        rows = pl.ds(c * CHUNK, CHUNK)
        dec = sm_ref[rows, SM_DEC[0]:SM_DEC[0] + SM_DEC[1]]
        logits = _bdot(dec, wup_ref[...]) + bdec_ref[...]
        log_a = -_softplus(-logits) * (1.0 / GLA_TEMP)
        cum = _mask_dot(ltri, log_a)
        cum_end = cum[CHUNK - 1:CHUNK, :]
        q = q_ref[rows, :] * (GLA_DK ** -0.5)
        k = k_ref[rows, :]
        q_dec = (q * jnp.exp(cum)).astype(BF16)
        k_inv = (k * jnp.exp(-cum)).astype(BF16)
        k_end = (k * jnp.exp(cum_end - cum)).astype(BF16)
        decay_end = jnp.exp(cum_end)
        v = v_ref[rows, :].astype(BF16)
        for p in pairs:
            s = slice(p * kw, (p + 1) * kw)
            qp.append(q_dec[:, s])
            vp.append(v[:, p * vw:(p + 1) * vw])
            k_diag.append(jnp.where(key_diag, jnp.concatenate([k_inv[:, s], k_inv[:, s]], axis=0), zero))
            kep.append(k_end[:, s])
            decays.append(decay_end[:, s])

    v_diag = [jnp.where(val_diag, jnp.concatenate([x, x], axis=0), zero) for x in vp]
    scores = [jnp.where(causal, _bdot_nt(q, kd), 0.0) for q, kd in zip(qp, k_diag)]
    o_local = [_bdot(s, vd) for s, vd in zip(scores, v_diag)]
    upd = [jnp.where(state_diag, _bdot_tn(x, ke), 0.0) for x, ke in zip(vp, kep)]

    state = [st_ref[p] for p in pairs]
    n_pairs = len(pairs)
    for c in range(chunks):
        rows = pl.ds(c * CHUNK, CHUNK)
        item = slice(c * n_pairs, (c + 1) * n_pairs)
        o = [ol + _bdot_nt(q, st) for ol, q, st in zip(o_local[item], qp[item], state)]
        state = [st * dcy + u for st, dcy, u in zip(state, decays[item], upd[item])]
        outs = []
        for op in o:
            for oh in (op[:, :GLA_DV], op[:, GLA_DV:]):
                mu = jnp.mean(oh, axis=-1, keepdims=True)
                oc = oh - mu
                var = jnp.mean(oc * oc, axis=-1, keepdims=True)
                outs.append(oc * lax.rsqrt(var + LN_EPS))
        y = jnp.concatenate(outs, axis=1) * ng_ref[...] + nb_ref[...]
        r = r_ref[rows, :]
        o_ref[rows, :] = (y * (r * _sigmoid(r))).astype(o_ref.dtype)
    for p in pairs:
        st_ref[p] = state[p]


def _gla_branch(h, batch, seq, wup_pad, b_decay, norm_g, norm_b):
    t = batch * seq
    chunks = 8
    rows = CHUNK * chunks
    nblk = seq // rows

    def col_spec(col):
        off, width = col
        return pl.BlockSpec((rows, width), lambda b, n: (b * nblk + n, off // width))

    def full(a):
        return pl.BlockSpec(a.shape, lambda b, n: (0,) * a.ndim)

    args = (wup_pad, b_decay, norm_g, norm_b)
    return pl.pallas_call(
        functools.partial(_gla_kernel, chunks=chunks),
        out_shape=jax.ShapeDtypeStruct((t, GLA_VW), BF16),
        grid=(batch, nblk),
        in_specs=[col_spec(COL_GLA_Q), col_spec(COL_GLA_K), col_spec(COL_GLA_V),
                  col_spec(COL_GLA_R), col_spec(COL_SMALL)] + [full(a) for a in args],
        out_specs=pl.BlockSpec((rows, GLA_VW), lambda b, n: (b * nblk + n, 0)),
        scratch_shapes=[pltpu.VMEM((GLA_HEADS // 2, 2 * GLA_DV, 2 * GLA_DK), F32)],
        compiler_params=_cparams("parallel", "arbitrary"),
    )(h, h, h, h, h, *args)


def _lru_kernel(x_ref, g_ref, cw_ref, cb_ref, wri_ref, bri_ref, sp_ref,
                o_ref, xbuf_ref, a_ref, u_ref, hs_ref, h_ref):
    tm = x_ref.shape[0]
    pad = SUBLANES

    @pl.when(pl.program_id(1) == 0)
    def _():
        xbuf_ref[0:pad, :] = jnp.zeros((pad, LRU_WIDTH), F32)
        h_ref[...] = jnp.zeros_like(h_ref)

    x = x_ref[...]
    xbuf_ref[pad:pad + tm, :] = x
    xc = cb_ref[...] + x * cw_ref[LRU_CONV - 1:LRU_CONV, :]
    for j in range(LRU_CONV - 1):
        back = LRU_CONV - 1 - j
        xc = xc + xbuf_ref[pl.ds(pad - back, tm), :] * cw_ref[j:j + 1, :]
    xbuf_ref[0:pad, :] = x[tm - pad:tm, :]

    ri = _sigmoid(_bdot(xc, wri_ref[...]) + bri_ref[...])
    rg, ig = ri[:, :LRU_WIDTH], ri[:, LRU_WIDTH:]
    log_a = -LRU_C * rg * sp_ref[...]
    a = jnp.exp(log_a)
    a_ref[...] = a
    u_ref[...] = jnp.sqrt(1.0 - a * a) * (ig * xc)

    def step(t, hprev):
        hnew = a_ref[pl.ds(t, 1), :] * hprev + u_ref[pl.ds(t, 1), :]
        hs_ref[pl.ds(t, 1), :] = hnew
        return hnew

    h_ref[...] = lax.fori_loop(0, tm, step, h_ref[...], unroll=8)
    o_ref[...] = (jax.nn.gelu(g_ref[...]) * hs_ref[...]).astype(o_ref.dtype)


def _lru_branch(h, batch, seq, conv_w, conv_b, w_ri, b_ri, softplus_neg_lam):
    t = batch * seq
    tm = min(512, seq)
    nblk = seq // tm

    def col_spec(col):
        off, width = col
        return pl.BlockSpec((tm, width), lambda b, n: (b * nblk + n, off // width))

    def full(a):
        return pl.BlockSpec(a.shape, lambda b, n: (0,) * a.ndim)

    args = (conv_w, conv_b, w_ri, b_ri, softplus_neg_lam)
    return pl.pallas_call(
        _lru_kernel,
        out_shape=jax.ShapeDtypeStruct((t, LRU_WIDTH), BF16),
        grid=(batch, nblk),
        in_specs=[col_spec(COL_LRU_X), col_spec(COL_LRU_G)] + [full(a) for a in args],
        out_specs=pl.BlockSpec((tm, LRU_WIDTH), lambda b, n: (b * nblk + n, 0)),
        scratch_shapes=[pltpu.VMEM((tm + SUBLANES, LRU_WIDTH), F32),
                        pltpu.VMEM((tm, LRU_WIDTH), F32),
                        pltpu.VMEM((tm, LRU_WIDTH), F32),
                        pltpu.VMEM((tm, LRU_WIDTH), F32),
                        pltpu.VMEM((1, LRU_WIDTH), F32)],
        compiler_params=_cparams("parallel", "arbitrary"),
    )(h, h, *args)


def _rwkv_prep_kernel(p_ref, pprev_ref, sm_ref, smprev_ref, mu_ref, musm_ref, w2_ref, a2_ref, g2_ref,
                      w0_ref, a0_ref, kk_ref, ka_ref, rk_ref, bd_ref,
                      r_out, k_out, v_out, lw_out, kk_out, kka_out, bonus_out, g_out, *, blocks_per_seq):
    tm = p_ref.shape[0]
    first = (pl.program_id(0) % blocks_per_seq) == 0
    keep = jnp.where(first, 0.0, 1.0)
    row0 = lax.broadcasted_iota(I32, (tm, 1), 0) == 0

    def token_shift(cur_ref, prev_ref, m_ref):
        cur = cur_ref[...]
        prev_row = prev_ref[SUBLANES - 1:SUBLANES, :] * keep
        prev = jnp.where(row0, prev_row, pltpu.roll(cur, 1, axis=0))
        return cur + (prev - cur) * m_ref[...]

    p = token_shift(p_ref, pprev_ref, mu_ref)
    sm = token_shift(sm_ref, smprev_ref, musm_ref)
    r = p[:, 0:RWKV_WIDTH]
    k = p[:, RWKV_WIDTH:2 * RWKV_WIDTH]
    v = p[:, 2 * RWKV_WIDTH:3 * RWKV_WIDTH]
    wa = sm[:, SM_WA[0]:SM_WA[0] + SM_WA[1]]
    gl = sm[:, SM_GL[0]:SM_GL[0] + SM_GL[1]]

    w_log = -_softplus(-(w0_ref[...] + _bdot(jnp.tanh(wa), w2_ref[...]))) - 0.5
    lw_out[...] = -jnp.exp(w_log)
    a = _sigmoid(a0_ref[...] + _bdot(wa, a2_ref[...]))
    g_out[...] = _bdot(_sigmoid(gl), g2_ref[...])

    kk = k * kk_ref[...]
    norm = jnp.sqrt(_dot_mask(kk * kk, bd_ref[...]))
    kk = kk / jnp.maximum(norm, 1e-12)
    k2 = k * (1.0 + (a - 1.0) * ka_ref[...])
    bonus_out[...] = _dot_mask(r * k2 * rk_ref[...], bd_ref[...]) * v
    r_out[...] = r
    k_out[...] = k2
    v_out[...] = v
    kk_out[...] = kk
    kka_out[...] = kk * a


def _rwkv_prep(h, batch, seq, params):
    t = batch * seq
    tm = min(512, seq)
    bps = seq // tm
    sub_per_tile = tm // SUBLANES

    def cur_spec(col):
        off, width = col
        return pl.BlockSpec((tm, width), lambda i: (i, off // width))

    def prev_spec(col):
        off, width = col
        return pl.BlockSpec((SUBLANES, width),
                            lambda i: (jnp.maximum(i * sub_per_tile - 1, 0), off // width))

    def full(a):
        return pl.BlockSpec(a.shape, lambda i: (0,) * a.ndim)

    out = jax.ShapeDtypeStruct((t, RWKV_WIDTH), F32)
    ospec = pl.BlockSpec((tm, RWKV_WIDTH), lambda i: (i, 0))
    return pl.pallas_call(
        functools.partial(_rwkv_prep_kernel, blocks_per_seq=bps),
        out_shape=[out] * 8,
        grid=(t // tm,),
        in_specs=[cur_spec(COL_RW_RKV), prev_spec(COL_RW_RKV), cur_spec(COL_SMALL), prev_spec(COL_SMALL)]
        + [full(a) for a in params],
        out_specs=[ospec] * 8,
        compiler_params=_cparams("parallel"),
    )(h, h, h, h, *params)


PAIR_W = 2 * RWKV_HEAD
N_PAIRS = RWKV_HEADS // 2
HEAD_SHIFT = RWKV_HEAD.bit_length() - 1


def _pair_masks():
    t = lax.broadcasted_iota(I32, (CHUNK, PAIR_W), 0)
    lane = lax.broadcasted_iota(I32, (CHUNK, PAIR_W), 1)
    j = lane & (RWKV_HEAD - 1)
    r = lax.broadcasted_iota(I32, (PAIR_W, PAIR_W), 0)
    c = lax.broadcasted_iota(I32, (PAIR_W, PAIR_W), 1)
    r4 = lax.broadcasted_iota(I32, (2 * PAIR_W, PAIR_W), 0)
    c4 = lax.broadcasted_iota(I32, (2 * PAIR_W, PAIR_W), 1)
    return dict(
        strict=t > j, incl=t >= j, eye=(t == j).astype(F32),
        same_block=jnp.bitwise_xor(t, j) < INV_BLOCK,
        first_head=lane < RWKV_HEAD,
        block_diag=(r >> HEAD_SHIFT) == (c >> HEAD_SHIFT),
        stacked_diag=((r4 >> HEAD_SHIFT) & 1) == (c4 >> HEAD_SHIFT))


def _rwkv_scan_kernel(r_ref, k_ref, v_ref, lw_ref, kk_ref, kka_ref, bonus_ref, g_ref, lg_ref, lb_ref,
                      o_ref, st_ref, *, chunks):
    @pl.when(pl.program_id(1) == 0)
    def _():
        st_ref[...] = jnp.zeros_like(st_ref)

    row, col = _tri_masks(CHUNK)
    ltri = (row >= col).astype(BF16)
    m = _pair_masks()
    zero = jnp.zeros((), BF16)
    pairs = range(N_PAIRS)
    sl = [slice(p * PAIR_W, (p + 1) * PAIR_W) for p in pairs]

    def block_diag(x):
        xb = x.astype(BF16)
        return jnp.where(m["block_diag"], jnp.concatenate([xb, xb], axis=0), zero)

    def pair_mul(lhs, *rhs):
        rb = jnp.concatenate([block_diag(x) for x in rhs], axis=1) if len(rhs) > 1 else block_diag(rhs[0])
        out = jnp.dot(lhs.astype(BF16), rb, preferred_element_type=F32)
        return [out[:, i * PAIR_W:(i + 1) * PAIR_W] for i in range(len(rhs))]

    def each(fn, *lists):
        return [fn(*args) for args in zip(*lists)]

    lhs, rhs, vp, ends, gammas = [], [], [], [], []
    for c in range(chunks):
        rows = pl.ds(c * CHUNK, CHUNK)
        lw = lw_ref[rows, :]
        cum = _mask_dot(ltri, lw)
        cum_end = cum[CHUNK - 1:CHUNK, :]
        e_inv = jnp.exp(-cum)
        e_end = jnp.exp(cum_end - cum)
        kka = kka_ref[rows, :]
        k = k_ref[rows, :]
        a_bar = (-kk_ref[rows, :] * jnp.exp(cum - lw)).astype(BF16)
        r_bar = (r_ref[rows, :] * jnp.exp(cum)).astype(BF16)
        b_til = (kka * e_inv).astype(BF16)
        k_til = (k * e_inv).astype(BF16)
        b_end = (kka * e_end).astype(BF16)
        k_end = (k * e_end).astype(BF16)
        gamma = jnp.exp(cum_end)
        v = v_ref[rows, :].astype(BF16)

        for s in sl:
            lhs.append(jnp.concatenate([a_bar[:, s], r_bar[:, s]], axis=0))
            rhs.append(jnp.where(m["stacked_diag"], jnp.concatenate(
                [b_til[:, s], b_til[:, s], k_til[:, s], k_til[:, s]], axis=0), zero))
            vp.append(v[:, s])
            ends.append(jnp.concatenate([b_end[:, s], k_end[:, s]], axis=0))
            gammas.append(gamma[:, s])

    blocks = each(_bdot_nt, lhs, rhs)
    a_ab = [jnp.where(m["strict"], b[:CHUNK, :PAIR_W], 0.0) for b in blocks]
    a_ak = [jnp.where(m["strict"], b[:CHUNK, PAIR_W:], 0.0) for b in blocks]
    r_ab = [jnp.where(m["incl"], b[CHUNK:, :PAIR_W], 0.0) for b in blocks]
    r_ak = [jnp.where(m["incl"], b[CHUNK:, PAIR_W:], 0.0) for b in blocks]
    from_v = [jnp.dot(jnp.concatenate([ak, rk], axis=0).astype(BF16), block_diag(x),
                      preferred_element_type=F32) for ak, rk, x in zip(a_ak, r_ak, vp)]
    d = [jnp.where(m["same_block"], a, 0.0) for a in a_ab]
    low = [a - di for a, di in zip(a_ab, d)]
    pm = [m["eye"] + di for di in d]
    dpow = [pair_mul(di, di)[0] for di in d]
    for _ in range(INV_BLOCK.bit_length() - 3):
        both = each(lambda x, p: pair_mul(x, x, p), dpow, pm)
        dpow = [b[0] for b in both]
        pm = [p + b[1] for p, b in zip(pm, both)]
    pm = [p + pair_mul(x, p)[0] for p, x in zip(pm, dpow)]
    both = each(lambda p, lo, fv, l: pair_mul(p, lo, fv[:CHUNK], l[:CHUNK]), pm, low, from_v, lhs)
    both2 = each(lambda b: pair_mul(b[0], b[0], b[1], b[2]), both)
    xw_v = [b[1] + b2[1] for b, b2 in zip(both, both2)]
    xw_a = [b[2] + b2[2] for b, b2 in zip(both, both2)]
    both3 = each(lambda b2, xv, xa: pair_mul(b2[0], xv, xa), both2, xw_v, xw_a)
    u_v = [xv + b3[0] for xv, b3 in zip(xw_v, both3)]
    wm = [xa + b3[1] for xa, b3 in zip(xw_a, both3)]
    both4 = each(lambda rb, uv, w: pair_mul(rb, uv, w), r_ab, u_v, wm)
    y_local = [fv[CHUNK:] + b4[0] for fv, b4 in zip(from_v, both4)]
    rm = [l[CHUNK:] + b4[1] for l, b4 in zip(lhs, both4)]
    gm = [jnp.where(m["block_diag"], _bdot_tn(en[:CHUNK], w), 0.0) for en, w in zip(ends, wm)]
    qm = [jnp.where(m["block_diag"], _bdot_tn(jnp.concatenate([uv.astype(BF16), x], axis=0), en), 0.0)
          for uv, x, en in zip(u_v, vp, ends)]

    state = [st_ref[p] for p in pairs]
    for c in range(chunks):
        rows = pl.ds(c * CHUNK, CHUNK)
        item = slice(c * N_PAIRS, (c + 1) * N_PAIRS)
        y = [yl + _bdot_nt(r, st) for yl, r, st in zip(y_local[item], rm[item], state)]
        state = [st * g + _bdot_nt(st, gmat) + q
                 for st, g, gmat, q in zip(state, gammas[item], gm[item], qm[item])]
        outs = []
        inv_n = 1.0 / RWKV_HEAD
        for yp in y:
            def head_stat(z):
                s0 = jnp.sum(jnp.where(m["first_head"], z, 0.0), axis=-1, keepdims=True)
                s1 = jnp.sum(jnp.where(m["first_head"], 0.0, z), axis=-1, keepdims=True)
                return jnp.where(m["first_head"], s0, s1) * inv_n
            yc = yp - head_stat(yp)
            outs.append(yc * lax.rsqrt(head_stat(yc * yc) + RWKV_LNX_EPS))
        yn = jnp.concatenate(outs, axis=1) * lg_ref[...] + lb_ref[...]
        o_ref[rows, :] = ((yn + bonus_ref[rows, :]) * g_ref[rows, :]).astype(o_ref.dtype)
    for p in pairs:
        st_ref[p] = state[p]


def _rwkv_scan(prep, batch, seq, lnx_g, lnx_b):
    t = batch * seq
    chunks = 8
    rows = CHUNK * chunks
    nblk = seq // rows
    spec = pl.BlockSpec((rows, RWKV_WIDTH), lambda b, n: (b * nblk + n, 0))
    pspec = pl.BlockSpec((1, RWKV_WIDTH), lambda b, n: (0, 0))
    return pl.pallas_call(
        functools.partial(_rwkv_scan_kernel, chunks=chunks),
        out_shape=jax.ShapeDtypeStruct((t, RWKV_WIDTH), BF16),
        grid=(batch, nblk),
        in_specs=[spec] * 8 + [pspec, pspec],
        out_specs=spec,
        scratch_shapes=[pltpu.VMEM((N_PAIRS, PAIR_W, PAIR_W), F32)],
        compiler_params=_cparams("parallel", "arbitrary"),
    )(*prep, lnx_g, lnx_b)


def _merge_kernel(og_ref, ol_ref, or_ref, x_ref, wgt_ref, bgt_ref, pg_ref, pl_ref, pr_ref, wo_ref,
                  lg_ref, lb_ref, xo_ref):
    d = D_MODEL
    xin = x_ref[...].astype(BF16)
    merged = None
    for b, (o_ref, p_ref) in enumerate(((og_ref, pg_ref), (ol_ref, pl_ref), (or_ref, pr_ref))):
        cols = slice(b * d, (b + 1) * d)
        gate = _sigmoid(jnp.dot(xin, wgt_ref[:, cols], preferred_element_type=F32) + bgt_ref[:, cols])
        term = gate * jnp.dot(o_ref[...], p_ref[...], preferred_element_type=F32)
        merged = term if merged is None else merged + term
    z = DEEPNORM_ALPHA * x_ref[...] + _bdot(merged, wo_ref[...])
    out = _layer_norm_rows(z, lg_ref[...], lb_ref[...], LN_EPS)
    xo_ref[...] = out


def _merge(o_gla, o_lru, o_rwkv, x, w_gates, b_gates, p_gla, p_lru, p_rwkv, w_out, ln_g, ln_b):
    t = x.shape[0]
    tm = min(512, t)

    def rows(width):
        return pl.BlockSpec((tm, width), lambda i: (i, 0))

    def full(a):
        return pl.BlockSpec(a.shape, lambda i: (0,) * a.ndim)

    params = (w_gates, b_gates, p_gla, p_lru, p_rwkv, w_out, ln_g, ln_b)
    return pl.pallas_call(
        _merge_kernel,
        out_shape=jax.ShapeDtypeStruct((t, D_MODEL), F32),
        grid=(t // tm,),
        in_specs=[rows(GLA_VW), rows(LRU_WIDTH), rows(RWKV_WIDTH), rows(D_MODEL)] + [full(a) for a in params],
        out_specs=rows(D_MODEL),
        compiler_params=_cparams("parallel"),
    )(o_gla, o_lru, o_rwkv, x, *params)


def _ffn_kernel(x_ref, wg_ref, wu_ref, wd_ref, lg_ref, lb_ref, xo_ref, acc_ref):
    f = pl.program_id(1)

    @pl.when(f == 0)
    def _():
        acc_ref[...] = jnp.zeros_like(acc_ref)

    xb = x_ref[...].astype(BF16)
    g = jnp.dot(xb, wg_ref[...], preferred_element_type=F32)
    u = jnp.dot(xb, wu_ref[...], preferred_element_type=F32)
    acc_ref[...] += _bdot(g * _sigmoid(g) * u, wd_ref[...])

    @pl.when(f == pl.num_programs(1) - 1)
    def _():
        z = DEEPNORM_ALPHA * x_ref[...] + acc_ref[...]
        out = _layer_norm_rows(z, lg_ref[...], lb_ref[...], LN_EPS)
        xo_ref[...] = out


def _dense_ffn(x, wg, wu, wd, ln_g, ln_b):
    t = x.shape[0]
    tm = min(512, t)
    ff = wg.shape[1]
    tf = ff // 2
    rows = lambda width: pl.BlockSpec((tm, width), lambda i, f: (i, 0))
    vec = pl.BlockSpec((1, D_MODEL), lambda i, f: (0, 0))
    return pl.pallas_call(
        _ffn_kernel,
        out_shape=jax.ShapeDtypeStruct((t, D_MODEL), F32),
        grid=(t // tm, ff // tf),
        in_specs=[rows(D_MODEL),
                  pl.BlockSpec((D_MODEL, tf), lambda i, f: (0, f)),
                  pl.BlockSpec((D_MODEL, tf), lambda i, f: (0, f)),
                  pl.BlockSpec((tf, D_MODEL), lambda i, f: (f, 0)), vec, vec],
        out_specs=rows(D_MODEL),
        scratch_shapes=[pltpu.VMEM((tm, D_MODEL), F32)],
        compiler_params=_cparams("parallel", "arbitrary"),
    )(x, wg, wu, wd, ln_g, ln_b)


def _router_kernel(x_ref, wr_ref, meta_ref, wts_ref, cnt_ref, carry_ref):
    tm = x_ref.shape[0]

    @pl.when(pl.program_id(0) == 0)
    def _():
        carry_ref[...] = jnp.zeros_like(carry_ref)

    lane = lax.broadcasted_iota(I32, (tm, LANES), 1)
    neg = jnp.float32(-jnp.inf)
    logits = jnp.where(lane < N_EXPERTS, _fdot(x_ref[...], wr_ref[...]), neg)
    m1 = jnp.max(logits, axis=-1, keepdims=True)
    e1 = jnp.min(jnp.where(logits == m1, lane, LANES), axis=-1, keepdims=True)
    rest = jnp.where(lane == e1, neg, logits)
    m2 = jnp.max(rest, axis=-1, keepdims=True)
    e2 = jnp.min(jnp.where(rest == m2, lane, LANES), axis=-1, keepdims=True)
    ex = jnp.exp(m2 - m1)
    w1 = 1.0 / (1.0 + ex)
    w2 = ex / (1.0 + ex)

    hot1 = lane == e1
    hot2 = lane == e2
    onehot = jnp.where(hot1 | hot2, 1.0, 0.0)
    row, col = _tri_masks(tm)
    before = (row > col).astype(BF16)
    prefix = jnp.dot(before, onehot.astype(BF16), preferred_element_type=F32) + carry_ref[...]
    rank1 = jnp.sum(jnp.where(hot1, prefix, 0.0), axis=-1, keepdims=True).astype(I32)
    rank2 = jnp.sum(jnp.where(hot2, prefix, 0.0), axis=-1, keepdims=True).astype(I32)
    carry_ref[...] += jnp.sum(onehot, axis=0, keepdims=True)
    cnt_ref[...] = carry_ref[...]

    meta = jnp.where(lane == 0, e1, jnp.where(lane == 1, e2, jnp.where(lane == 2, rank1, rank2)))
    meta_ref[...] = jnp.where(lane < 4, meta, 0)
    wts_ref[...] = jnp.where(lane == 0, w1, jnp.where(lane == 1, w2, 0.0))


def _router(x, wr_pad):
    t = x.shape[0]
    tm = min(512, t)
    return pl.pallas_call(
        _router_kernel,
        out_shape=[jax.ShapeDtypeStruct((t, LANES), I32), jax.ShapeDtypeStruct((t, LANES), F32),
                   jax.ShapeDtypeStruct((1, LANES), F32)],
        grid=(t // tm,),
        in_specs=[pl.BlockSpec((tm, D_MODEL), lambda i: (i, 0)),
                  pl.BlockSpec((D_MODEL, LANES), lambda i: (0, 0))],
        out_specs=[pl.BlockSpec((tm, LANES), lambda i: (i, 0)), pl.BlockSpec((tm, LANES), lambda i: (i, 0)),
                   pl.BlockSpec((1, LANES), lambda i: (0, 0))],
        scratch_shapes=[pltpu.VMEM((1, LANES), F32)],
        compiler_params=_cparams("arbitrary"),
    )(x, wr_pad)


def _dispatch_kernel(dest_ref, x_ref, xs_in_ref, xs_ref, sem):
    del xs_in_ref
    tm = x_ref.shape[0]

    def row_copy(r, j):
        return pltpu.make_async_copy(x_ref.at[pl.ds(r, 1)], xs_ref.at[pl.ds(dest_ref[TOP_K * r + j], 1)], sem)

    def start(r, c):
        for j in range(TOP_K):
            row_copy(r, j).start()
        return c

    lax.fori_loop(0, tm, start, 0, unroll=8)
    for j in range(TOP_K):
        pltpu.make_async_copy(x_ref, xs_ref.at[pl.ds(0, tm)], sem).wait()


def _dispatch(x, dest_flat, n_slots):
    t = x.shape[0]
    tm = min(256, t)
    xs0 = jnp.zeros((n_slots, D_MODEL), F32)
    return pl.pallas_call(
        _dispatch_kernel,
        out_shape=jax.ShapeDtypeStruct((n_slots, D_MODEL), F32),
        grid=(t // tm,),
        in_specs=[pl.BlockSpec((tm * TOP_K,), lambda i: (i,), memory_space=pltpu.SMEM),
                  pl.BlockSpec((tm, D_MODEL), lambda i: (i, 0)),
                  pl.BlockSpec(memory_space=pl.ANY)],
        out_specs=pl.BlockSpec(memory_space=pl.ANY),
        scratch_shapes=[pltpu.SemaphoreType.DMA(())],
        input_output_aliases={2: 0},
        compiler_params=_cparams("arbitrary"),
    )(dest_flat, x, xs0)


def _expert_kernel(blk_e_ref, nused_ref, xs_ref, wg_ref, wu_ref, wd_ref, ys_ref, acc_ref):
    del blk_e_ref
    i = pl.program_id(0)
    f = pl.program_id(1)
    last = pl.num_programs(1) - 1
    used = i < nused_ref[0]

    @pl.when(used & (f == 0))
    def _():
        acc_ref[...] = jnp.zeros_like(acc_ref)

    @pl.when(used)
    def _():
        xb = xs_ref[...].astype(BF16)
        g = jnp.dot(xb, wg_ref[0], preferred_element_type=F32)
        u = jnp.dot(xb, wu_ref[0], preferred_element_type=F32)
        acc_ref[...] += _bdot(g * _sigmoid(g) * u, wd_ref[0])

    @pl.when(used & (f == last))
    def _():
        ys_ref[...] = acc_ref[...]

    @pl.when(jnp.logical_not(used) & (f == last))
    def _():
        ys_ref[...] = jnp.zeros_like(ys_ref)


def _experts(xs, blk_e, n_used, wg, wu, wd):
    n_slots = xs.shape[0]
    n_blocks = n_slots // MOE_ROWS
    ff = wg.shape[2]
    tf = ff // 2
    grid_spec = pltpu.PrefetchScalarGridSpec(
        num_scalar_prefetch=2,
        grid=(n_blocks, ff // tf),
        in_specs=[pl.BlockSpec((MOE_ROWS, D_MODEL), lambda i, f, be, nu: (i, 0)),
                  pl.BlockSpec((1, D_MODEL, tf), lambda i, f, be, nu: (be[i], 0, f)),
                  pl.BlockSpec((1, D_MODEL, tf), lambda i, f, be, nu: (be[i], 0, f)),
                  pl.BlockSpec((1, tf, D_MODEL), lambda i, f, be, nu: (be[i], f, 0))],
        out_specs=pl.BlockSpec((MOE_ROWS, D_MODEL), lambda i, f, be, nu: (i, 0)),
        scratch_shapes=[pltpu.VMEM((MOE_ROWS, D_MODEL), F32)],
    )
    return pl.pallas_call(
        _expert_kernel,
        out_shape=jax.ShapeDtypeStruct((n_slots, D_MODEL), F32),
        grid_spec=grid_spec,
        compiler_params=_cparams("arbitrary", "arbitrary"),
    )(blk_e, n_used, xs, wg, wu, wd)


def _combine_kernel(dest_ref, dest_next_ref, x_ref, wts_ref, ys_ref, lg_ref, lb_ref, xo_ref, buf_ref, sem):
    tm = x_ref.shape[0]
    i = pl.program_id(0)
    n = pl.num_programs(0)

    def gather(idx_ref, slot):
        def start(r, c):
            for j in range(TOP_K):
                pltpu.make_async_copy(ys_ref.at[pl.ds(idx_ref[TOP_K * r + j], 1)],
                                      buf_ref.at[slot, j, pl.ds(r, 1)], sem.at[slot]).start()
            return c
        lax.fori_loop(0, tm, start, 0, unroll=8)

    slot = i & 1

    @pl.when(i == 0)
    def _():
        gather(dest_ref, 0)

    @pl.when(i + 1 < n)
    def _():
        gather(dest_next_ref, 1 - slot)

    for j in range(TOP_K):
        pltpu.make_async_copy(ys_ref.at[pl.ds(0, tm)], buf_ref.at[slot, j], sem.at[slot]).wait()
    w = wts_ref[...]
    f = w[:, 0:1] * buf_ref[slot, 0] + w[:, 1:2] * buf_ref[slot, 1]
    z = DEEPNORM_ALPHA * x_ref[...] + f
    xo_ref[...] = _layer_norm_rows(z, lg_ref[...], lb_ref[...], LN_EPS)


def _combine(x, dest_flat, wts, ys, ln_g, ln_b):
    t = x.shape[0]
    tm = min(256, t)
    n_tiles = t // tm
    rows = lambda width: pl.BlockSpec((tm, width), lambda i: (i, 0))
    vec = pl.BlockSpec((1, D_MODEL), lambda i: (0, 0))
    return pl.pallas_call(
        _combine_kernel,
        out_shape=jax.ShapeDtypeStruct((t, D_MODEL), F32),
        grid=(n_tiles,),
        in_specs=[pl.BlockSpec((tm * TOP_K,), lambda i: (i,), memory_space=pltpu.SMEM),
                  pl.BlockSpec((tm * TOP_K,), lambda i: (jnp.minimum(i + 1, n_tiles - 1),),
                               memory_space=pltpu.SMEM),
                  rows(D_MODEL), rows(LANES), pl.BlockSpec(memory_space=pl.ANY), vec, vec],
        out_specs=rows(D_MODEL),
        scratch_shapes=[pltpu.VMEM((2, TOP_K, tm, D_MODEL), F32), pltpu.SemaphoreType.DMA((2,))],
        compiler_params=_cparams("arbitrary"),
    )(dest_flat, dest_flat, x, wts, ys, ln_g, ln_b)


def _moe_ffn(x, w_router, wg, wu, wd, ln_g, ln_b):
    t = x.shape[0]
    wr_pad = jnp.pad(w_router, ((0, 0), (0, LANES - N_EXPERTS)))
    meta, wts, cnt = _router(x, wr_pad)
    counts = cnt[0, :N_EXPERTS].astype(I32)
    padded = (counts + MOE_ROWS - 1) // MOE_ROWS * MOE_ROWS
    seg_end = jnp.cumsum(padded)
    seg_start = seg_end - padded
    n_blocks = (t * TOP_K) // MOE_ROWS + N_EXPERTS
    dest = seg_start[meta[:, 0:TOP_K]] + meta[:, TOP_K:2 * TOP_K]
    dest_flat = dest.reshape(-1).astype(I32)
    blk_start = jnp.arange(n_blocks, dtype=I32) * MOE_ROWS
    blk_e = jnp.minimum(jnp.sum(blk_start[:, None] >= seg_end[None, :], axis=1), N_EXPERTS - 1).astype(I32)
    n_used = (seg_end[-1:] // MOE_ROWS).astype(I32)
    xs = _dispatch(x, dest_flat, n_blocks * MOE_ROWS)
    ys = _experts(xs, blk_e, n_used, wg, wu, wd)
    return _combine(x, dest_flat, wts, ys, ln_g, ln_b)


def _pad_rows(w, rows, at=0):
    out = jnp.zeros((rows, w.shape[1]), w.dtype)
    return out.at[at:at + w.shape[0]].set(w)


def _reorder_in_projection(w, b):
    sizes = (GLA_KW, GLA_KW, GLA_VW, GLA_VW, GLA_DECAY_RANK, LRU_WIDTH, LRU_WIDTH,
             3 * RWKV_WIDTH, RWKV_DECAY_RANK, RWKV_A_RANK, RWKV_GATE_RANK, N_BRANCH * D_MODEL)
    offs = [0]
    for s in sizes:
        offs.append(offs[-1] + s)
    wb = jnp.concatenate([w, b[None, :]], axis=0)
    piece = lambda i: wb[:, offs[i]:offs[i + 1]]
    zeros = lambda n: jnp.zeros((wb.shape[0], n), wb.dtype)
    q, k, v, r, dec, lx, lg, rkv, wl, al, gl, gates = (piece(i) for i in range(len(sizes)))
    small = jnp.concatenate([wl, al, gl, zeros(SM_GL[1] - RWKV_GATE_RANK),
                             dec, zeros(SM_DEC[1] - GLA_DECAY_RANK)], axis=1)
    out = jnp.concatenate([rkv, v, r, lx, lg, q, k, small], axis=1)
    return out[:-1].astype(BF16), out[-1:], gates[:-1].astype(BF16), gates[-1:]


def _cast_kernel(x_ref, o_ref):
    o_ref[...] = x_ref[...].astype(o_ref.dtype)


CAST_TILE_ELEMS = 2 * 1024 * 1024


def _to_bf16(w):
    cols = w.shape[-1]
    w2 = w.reshape(-1, cols)
    n_rows = w2.shape[0]
    tr = n_rows
    while tr * cols > CAST_TILE_ELEMS and tr % 32 == 0:
        tr //= 2
    out = pl.pallas_call(
        _cast_kernel,
        out_shape=jax.ShapeDtypeStruct(w2.shape, BF16),
        grid=(n_rows // tr,),
        in_specs=[pl.BlockSpec((tr, cols), lambda i: (i, 0))],
        out_specs=pl.BlockSpec((tr, cols), lambda i: (i, 0)),
        compiler_params=_cparams("parallel"),
    )(w2)
    return out.reshape(w.shape)


def _block_diag(blocks):
    n, bi, bo = blocks.shape
    eye = jnp.eye(n, dtype=blocks.dtype)
    return (eye[:, None, :, None] * blocks[:, :, None, :]).reshape(n * bi, n * bo)


def kernel(x, w_in, b_in, gla_w_decay_up, gla_b_decay, gla_norm_g, gla_norm_b, lru_conv_w, lru_conv_b, lru_w_r, lru_b_r, lru_w_i, lru_b_i, lru_lambda, rwkv_mu, rwkv_w0, rwkv_w2, rwkv_a0, rwkv_a2, rwkv_g2, rwkv_k_k, rwkv_k_a, rwkv_r_k, rwkv_lnx_g, rwkv_lnx_b, p_gla, p_lru, p_rwkv, w_out, ln_mix_g, ln_mix_b, ffn_w_gate, ffn_w_up, ffn_w_down, moe_w_router, moe_w_gate, moe_w_up, moe_w_down, ln_ffn_g, ln_ffn_b):
    batch, seq, d = x.shape
    t = batch * seq
    xf = x.reshape(t, d)
    row = lambda a: a.reshape(1, -1)
    head_ones = _block_diag(jnp.ones((RWKV_HEADS, RWKV_HEAD, RWKV_HEAD), BF16))
    for l in range(DEPTH):
        w_l, b_l, w_gates, b_gates = _reorder_in_projection(w_in[l], b_in[l])
        h = _in_projection(xf, w_l, b_l)

        wup = _pad_rows(gla_w_decay_up[l], SM_DEC[1]).astype(BF16)
        o_gla = _gla_branch(h, batch, seq, wup, row(gla_b_decay[l]), row(gla_norm_g[l]), row(gla_norm_b[l]))

        w_ri = jnp.concatenate([_block_diag(lru_w_r[l]), _block_diag(lru_w_i[l])], axis=1).astype(BF16)
        b_ri = jnp.concatenate([lru_b_r[l], lru_b_i[l]])[None, :]
        o_lru = _lru_branch(h, batch, seq, lru_conv_w[l], row(lru_conv_b[l]), w_ri, b_ri,
                            row(jax.nn.softplus(-lru_lambda[l])))

        mu = rwkv_mu[l]
        nr = 3 * RWKV_WIDTH
        mu_small = jnp.zeros((COL_SMALL[1],), F32).at[0:mu.shape[0] - nr].set(mu[nr:])
        prep_params = (row(mu[:nr]), row(mu_small),
                       _pad_rows(rwkv_w2[l], SM_WA[1], 0).astype(BF16),
                       _pad_rows(rwkv_a2[l], SM_WA[1], RWKV_DECAY_RANK).astype(BF16),
                       _pad_rows(rwkv_g2[l], SM_GL[1], 0).astype(BF16),
                       row(rwkv_w0[l]), row(rwkv_a0[l]), row(rwkv_k_k[l]), row(rwkv_k_a[l]),
                       row(rwkv_r_k[l]), head_ones)
        prep = _rwkv_prep(h, batch, seq, prep_params)
        o_rwkv = _rwkv_scan(prep, batch, seq, row(rwkv_lnx_g[l]), row(rwkv_lnx_b[l]))

        xf = _merge(o_gla, o_lru, o_rwkv, xf, w_gates, b_gates, p_gla[l].astype(BF16), p_lru[l].astype(BF16),
                    p_rwkv[l].astype(BF16), w_out[l].astype(BF16), row(ln_mix_g[l]), row(ln_mix_b[l]))
        i = l // 2
        if l % 2 == 0:
            xf = _dense_ffn(xf, ffn_w_gate[i].astype(BF16), ffn_w_up[i].astype(BF16),
                            ffn_w_down[i].astype(BF16), row(ln_ffn_g[l]), row(ln_ffn_b[l]))
        else:
            xf = _moe_ffn(xf, moe_w_router[i], _to_bf16(moe_w_gate[i]), _to_bf16(moe_w_up[i]),
                          _to_bf16(moe_w_down[i]), row(ln_ffn_g[l]), row(ln_ffn_b[l]))
    return xf.reshape(batch, seq, d)
```

```python
import functools

import jax
import jax.numpy as jnp
from jax import lax
from jax.experimental import pallas as pl
from jax.experimental.pallas import tpu as pltpu

F32 = jnp.float32
BF16 = jnp.bfloat16
I32 = jnp.int32
HIGHEST = lax.Precision.HIGHEST

D_MODEL = 1024
DEPTH = 2
GLA_HEADS, GLA_DK, GLA_DV = 4, 64, 128
GLA_KW, GLA_VW = GLA_HEADS * GLA_DK, GLA_HEADS * GLA_DV
GLA_DECAY_RANK = 16
GLA_TEMP = 16.0
LRU_WIDTH, LRU_BLOCKS, LRU_CONV, LRU_C = 512, 8, 4, 8.0
RWKV_HEAD, RWKV_WIDTH = 64, 512
RWKV_HEADS = RWKV_WIDTH // RWKV_HEAD
RWKV_DECAY_RANK, RWKV_A_RANK, RWKV_GATE_RANK = 64, 64, 160
RWKV_LNX_EPS = 64e-5
N_BRANCH = 3
N_EXPERTS, TOP_K = 8, 2
DEEPNORM_ALPHA = (2 * DEPTH) ** 0.25
LN_EPS = 1e-5

LANES = 128
SUBLANES = 8
VMEM_LIMIT_BYTES = 56 * 1024 * 1024

SM_WA = (0, 128)
SM_GL = (128, 256)
RW_SMALL_W = 512
GLA_DEC_W = LANES
GLA_COLS = {"q": (0, 256), "k": (256, 512), "v": (512, 1024), "r": (1024, 1536), "dec": (1536, 1664)}

CHUNK = 64
INV_BLOCK = 16
MOE_ROWS = 512


def _cparams(*sem):
    return pltpu.CompilerParams(dimension_semantics=sem, vmem_limit_bytes=VMEM_LIMIT_BYTES)


def _sigmoid(x):
    return 1.0 / (1.0 + jnp.exp(-x))


def _softplus(x):
    return jnp.maximum(x, 0.0) + jnp.log(1.0 + jnp.exp(-jnp.abs(x)))


def _bdot(a, b):
    return jnp.dot(a.astype(BF16), b.astype(BF16), preferred_element_type=F32)


def _bdot_nt(a, b):
    return lax.dot_general(a.astype(BF16), b.astype(BF16), (((1,), (1,)), ((), ())),
                           preferred_element_type=F32)


def _bdot_tn(a, b):
    return lax.dot_general(a.astype(BF16), b.astype(BF16), (((0,), (0,)), ((), ())),
                           preferred_element_type=F32)


def _fdot(a, b):
    ah = a.astype(BF16)
    al = (a - ah.astype(F32)).astype(BF16)
    bh = b.astype(BF16)
    bl = (b - bh.astype(F32)).astype(BF16)
    dot = lambda u, v: jnp.dot(u, v, preferred_element_type=F32)
    return dot(al, bh) + dot(ah, bl) + dot(ah, bh)


def _split3(x):
    hi = x.astype(BF16)
    rest = x - hi.astype(F32)
    mid = rest.astype(BF16)
    lo = (rest - mid.astype(F32)).astype(BF16)
    return hi, mid, lo


def _mask_dot(mask_bf16, x):
    hi, mid, lo = _split3(x)
    dot = lambda part: jnp.dot(mask_bf16, part, preferred_element_type=F32)
    return dot(lo) + dot(mid) + dot(hi)


def _dot_mask(x, mask_bf16):
    hi, mid, lo = _split3(x)
    dot = lambda part: jnp.dot(part, mask_bf16, preferred_element_type=F32)
    return dot(lo) + dot(mid) + dot(hi)


def _layer_norm_rows(z, g, b, eps):
    mu = jnp.mean(z, axis=-1, keepdims=True)
    zc = z - mu
    var = jnp.mean(zc * zc, axis=-1, keepdims=True)
    return zc * lax.rsqrt(var + eps) * g + b


def _tri_masks(n):
    row = lax.broadcasted_iota(I32, (n, n), 0)
    col = lax.broadcasted_iota(I32, (n, n), 1)
    return row, col


def _gla_kernel(x_ref, w_ref, b_ref, wup_ref, bdec_ref, ng_ref, nb_ref, o_ref, st_ref, h_ref, *, chunks):
    @pl.when(pl.program_id(1) == 0)
    def _():
        st_ref[...] = jnp.zeros_like(st_ref)

    h_ref[...] = jnp.dot(x_ref[...].astype(BF16), w_ref[...], preferred_element_type=F32) + b_ref[...]
    q_ref = h_ref.at[:, GLA_COLS["q"][0]:GLA_COLS["q"][1]]
    k_ref = h_ref.at[:, GLA_COLS["k"][0]:GLA_COLS["k"][1]]
    v_ref = h_ref.at[:, GLA_COLS["v"][0]:GLA_COLS["v"][1]]
    r_ref = h_ref.at[:, GLA_COLS["r"][0]:GLA_COLS["r"][1]]
    dec_ref = h_ref.at[:, GLA_COLS["dec"][0]:GLA_COLS["dec"][1]]

    row, col = _tri_masks(CHUNK)
    ltri = (row >= col).astype(BF16)
    kw, vw = 2 * GLA_DK, 2 * GLA_DV
    key_shift, val_shift = GLA_DK.bit_length() - 1, GLA_DV.bit_length() - 1
    iota = lambda shape, axis: lax.broadcasted_iota(I32, shape, axis)
    causal = iota((CHUNK, kw), 0) >= (iota((CHUNK, kw), 1) & (GLA_DK - 1))
    key_diag = (iota((kw, kw), 0) >> key_shift) == (iota((kw, kw), 1) >> key_shift)
    val_diag = (iota((kw, vw), 0) >> key_shift) == (iota((kw, vw), 1) >> val_shift)
    state_diag = (iota((vw, kw), 0) >> val_shift) == (iota((vw, kw), 1) >> key_shift)
    zero = jnp.zeros((), BF16)
    pairs = range(GLA_HEADS // 2)
    qp, vp, k_diag, kep, decays = [], [], [], [], []
    for c in range(chunks):
        rows = pl.ds(c * CHUNK, CHUNK)
        logits = _bdot(dec_ref[rows, :], wup_ref[...]) + bdec_ref[...]
        log_a = -_softplus(-logits) * (1.0 / GLA_TEMP)
        cum = _mask_dot(ltri, log_a)
        cum_end = cum[CHUNK - 1:CHUNK, :]
        q = q_ref[rows, :] * (GLA_DK ** -0.5)
        k = k_ref[rows, :]
        q_dec = (q * jnp.exp(cum)).astype(BF16)
        k_inv = (k * jnp.exp(-cum)).astype(BF16)
        k_end = (k * jnp.exp(cum_end - cum)).astype(BF16)
        decay_end = jnp.exp(cum_end)
        v = v_ref[rows, :].astype(BF16)
        for p in pairs:
            s = slice(p * kw, (p + 1) * kw)
            qp.append(q_dec[:, s])
            vp.append(v[:, p * vw:(p + 1) * vw])
            k_diag.append(jnp.where(key_diag, jnp.concatenate([k_inv[:, s], k_inv[:, s]], axis=0), zero))
            kep.append(k_end[:, s])
            decays.append(decay_end[:, s])

    v_diag = [jnp.where(val_diag, jnp.concatenate([x, x], axis=0), zero) for x in vp]
    scores = [jnp.where(causal, _bdot_nt(q, kd), 0.0) for q, kd in zip(qp, k_diag)]
    o_local = [_bdot(s, vd) for s, vd in zip(scores, v_diag)]
    upd = [jnp.where(state_diag, _bdot_tn(x, ke), 0.0) for x, ke in zip(vp, kep)]

    state = [st_ref[p] for p in pairs]
    n_pairs = len(pairs)
    for c in range(chunks):
        rows = pl.ds(c * CHUNK, CHUNK)
        item = slice(c * n_pairs, (c + 1) * n_pairs)
        o = [ol + _bdot_nt(q, st) for ol, q, st in zip(o_local[item], qp[item], state)]
        state = [st * dcy + u for st, dcy, u in zip(state, decays[item], upd[item])]
        outs = []
        for op in o:
            for oh in (op[:, :GLA_DV], op[:, GLA_DV:]):
                mu = jnp.mean(oh, axis=-1, keepdims=True)
                oc = oh - mu
                var = jnp.mean(oc * oc, axis=-1, keepdims=True)
                outs.append(oc * lax.rsqrt(var + LN_EPS))
        y = jnp.concatenate(outs, axis=1) * ng_ref[...] + nb_ref[...]
        r = r_ref[rows, :]
        o_ref[rows, :] = (y * (r * _sigmoid(r))).astype(o_ref.dtype)
    for p in pairs:
        st_ref[p] = state[p]


def _gla_branch(x, batch, seq, w, bias, wup_pad, b_decay, norm_g, norm_b):
    t = batch * seq
    chunks = 8
    rows = CHUNK * chunks
    nblk = seq // rows

    def full(a):
        return pl.BlockSpec(a.shape, lambda b, n: (0,) * a.ndim)

    args = (w, bias, wup_pad, b_decay, norm_g, norm_b)
    return pl.pallas_call(
        functools.partial(_gla_kernel, chunks=chunks),
        out_shape=jax.ShapeDtypeStruct((t, GLA_VW), BF16),
        grid=(batch, nblk),
        in_specs=[pl.BlockSpec((rows, D_MODEL), lambda b, n: (b * nblk + n, 0))] + [full(a) for a in args],
        out_specs=pl.BlockSpec((rows, GLA_VW), lambda b, n: (b * nblk + n, 0)),
        scratch_shapes=[pltpu.VMEM((GLA_HEADS // 2, 2 * GLA_DV, 2 * GLA_DK), F32),
                        pltpu.VMEM((rows, w.shape[1]), F32)],
        compiler_params=_cparams("parallel", "arbitrary"),
    )(x, *args)


def _lru_kernel(xin_ref, w_ref, b_ref, cw_ref, cb_ref, wri_ref, bri_ref, sp_ref,
                o_ref, xbuf_ref, a_ref, u_ref, hs_ref, gate_ref, h_ref):
    tm = xin_ref.shape[0]
    pad = SUBLANES

    @pl.when(pl.program_id(1) == 0)
    def _():
        xbuf_ref[0:pad, :] = jnp.zeros((pad, LRU_WIDTH), F32)
        h_ref[...] = jnp.zeros_like(h_ref)

    proj = jnp.dot(xin_ref[...].astype(BF16), w_ref[...], preferred_element_type=F32) + b_ref[...]
    x = proj[:, :LRU_WIDTH]
    gate_ref[...] = jax.nn.gelu(proj[:, LRU_WIDTH:])
    xbuf_ref[pad:pad + tm, :] = x
    xc = cb_ref[...] + x * cw_ref[LRU_CONV - 1:LRU_CONV, :]
    for j in range(LRU_CONV - 1):
        back = LRU_CONV - 1 - j
        xc = xc + xbuf_ref[pl.ds(pad - back, tm), :] * cw_ref[j:j + 1, :]
    xbuf_ref[0:pad, :] = x[tm - pad:tm, :]

    ri = _sigmoid(_bdot(xc, wri_ref[...]) + bri_ref[...])
    rg, ig = ri[:, :LRU_WIDTH], ri[:, LRU_WIDTH:]
    log_a = -LRU_C * rg * sp_ref[...]
    a = jnp.exp(log_a)
    a_ref[...] = a
    u_ref[...] = jnp.sqrt(1.0 - a * a) * (ig * xc)

    def step(t, hprev):
        hnew = a_ref[pl.ds(t, 1), :] * hprev + u_ref[pl.ds(t, 1), :]
        hs_ref[pl.ds(t, 1), :] = hnew
        return hnew

    h_ref[...] = lax.fori_loop(0, tm, step, h_ref[...], unroll=8)
    o_ref[...] = (gate_ref[...] * hs_ref[...]).astype(o_ref.dtype)


def _lru_branch(x, batch, seq, w, bias, conv_w, conv_b, w_ri, b_ri, softplus_neg_lam):
    t = batch * seq
    tm = min(512, seq)
    nblk = seq // tm

    def full(a):
        return pl.BlockSpec(a.shape, lambda b, n: (0,) * a.ndim)

    args = (w, bias, conv_w, conv_b, w_ri, b_ri, softplus_neg_lam)
    return pl.pallas_call(
        _lru_kernel,
        out_shape=jax.ShapeDtypeStruct((t, LRU_WIDTH), BF16),
        grid=(batch, nblk),
        in_specs=[pl.BlockSpec((tm, D_MODEL), lambda b, n: (b * nblk + n, 0))] + [full(a) for a in args],
        out_specs=pl.BlockSpec((tm, LRU_WIDTH), lambda b, n: (b * nblk + n, 0)),
        scratch_shapes=[pltpu.VMEM((tm + SUBLANES, LRU_WIDTH), F32)]
        + [pltpu.VMEM((tm, LRU_WIDTH), F32)] * 4 + [pltpu.VMEM((1, LRU_WIDTH), F32)],
        compiler_params=_cparams("parallel", "arbitrary"),
    )(x, *args)


def _rwkv_prep_kernel(x_ref, xprev_ref, w_ref, b_ref, mu_ref, w2_ref, a2_ref, g2_ref,
                      w0_ref, a0_ref, kk_ref, ka_ref, rk_ref, bd_ref,
                      r_out, k_out, v_out, lw_out, kk_out, kka_out, bonus_out, g_out, *, blocks_per_seq):
    tm = x_ref.shape[0]
    first = (pl.program_id(0) % blocks_per_seq) == 0
    keep = jnp.where(first, 0.0, 1.0)
    row0 = lax.broadcasted_iota(I32, (tm, 1), 0) == 0

    project = lambda xs: jnp.dot(xs.astype(BF16), w_ref[...], preferred_element_type=F32) + b_ref[...]
    cur = project(x_ref[...])
    prev_row = project(xprev_ref[...])[SUBLANES - 1:SUBLANES, :] * keep
    prev = jnp.where(row0, prev_row, pltpu.roll(cur, 1, axis=0))
    shifted = cur + (prev - cur) * mu_ref[...]
    p = shifted[:, :3 * RWKV_WIDTH]
    sm = shifted[:, 3 * RWKV_WIDTH:]
    r = p[:, 0:RWKV_WIDTH]
    k = p[:, RWKV_WIDTH:2 * RWKV_WIDTH]
    v = p[:, 2 * RWKV_WIDTH:3 * RWKV_WIDTH]
    wa = sm[:, SM_WA[0]:SM_WA[0] + SM_WA[1]]
    gl = sm[:, SM_GL[0]:SM_GL[0] + SM_GL[1]]

    w_log = -_softplus(-(w0_ref[...] + _bdot(jnp.tanh(wa), w2_ref[...]))) - 0.5
    lw_out[...] = -jnp.exp(w_log)
    a = _sigmoid(a0_ref[...] + _bdot(wa, a2_ref[...]))
    g_out[...] = _bdot(_sigmoid(gl), g2_ref[...])

    kk = k * kk_ref[...]
    norm = jnp.sqrt(_dot_mask(kk * kk, bd_ref[...]))
    kk = kk / jnp.maximum(norm, 1e-12)
    k2 = k * (1.0 + (a - 1.0) * ka_ref[...])
    bonus_out[...] = _dot_mask(r * k2 * rk_ref[...], bd_ref[...]) * v
    r_out[...] = r
    k_out[...] = k2
    v_out[...] = v
    kk_out[...] = kk
    kka_out[...] = kk * a


def _rwkv_prep(x, batch, seq, params):
    t = batch * seq
    tm = min(512, seq)
    bps = seq // tm
    sub_per_tile = tm // SUBLANES

    def full(a):
        return pl.BlockSpec(a.shape, lambda i: (0,) * a.ndim)

    out = jax.ShapeDtypeStruct((t, RWKV_WIDTH), F32)
    ospec = pl.BlockSpec((tm, RWKV_WIDTH), lambda i: (i, 0))
    return pl.pallas_call(
        functools.partial(_rwkv_prep_kernel, blocks_per_seq=bps),
        out_shape=[out] * 8,
        grid=(t // tm,),
        in_specs=[pl.BlockSpec((tm, D_MODEL), lambda i: (i, 0)),
                  pl.BlockSpec((SUBLANES, D_MODEL), lambda i: (jnp.maximum(i * sub_per_tile - 1, 0), 0))]
        + [full(a) for a in params],
        out_specs=[ospec] * 8,
        compiler_params=_cparams("parallel"),
    )(x, x, *params)


PAIR_W = 2 * RWKV_HEAD
N_PAIRS = RWKV_HEADS // 2
HEAD_SHIFT = RWKV_HEAD.bit_length() - 1


def _pair_masks():
    t = lax.broadcasted_iota(I32, (CHUNK, PAIR_W), 0)
    lane = lax.broadcasted_iota(I32, (CHUNK, PAIR_W), 1)
    j = lane & (RWKV_HEAD - 1)
    r = lax.broadcasted_iota(I32, (PAIR_W, PAIR_W), 0)
    c = lax.broadcasted_iota(I32, (PAIR_W, PAIR_W), 1)
    r4 = lax.broadcasted_iota(I32, (2 * PAIR_W, PAIR_W), 0)
    c4 = lax.broadcasted_iota(I32, (2 * PAIR_W, PAIR_W), 1)
    return dict(
        strict=t > j, incl=t >= j, eye=(t == j).astype(F32),
        same_block=jnp.bitwise_xor(t, j) < INV_BLOCK,
        first_head=lane < RWKV_HEAD,
        block_diag=(r >> HEAD_SHIFT) == (c >> HEAD_SHIFT),
        stacked_diag=((r4 >> HEAD_SHIFT) & 1) == (c4 >> HEAD_SHIFT))


def _rwkv_scan_kernel(r_ref, k_ref, v_ref, lw_ref, kk_ref, kka_ref, bonus_ref, g_ref, lg_ref, lb_ref,
                      o_ref, st_ref, *, chunks):
    @pl.when(pl.program_id(1) == 0)
    def _():
        st_ref[...] = jnp.zeros_like(st_ref)

    row, col = _tri_masks(CHUNK)
    ltri = (row >= col).astype(BF16)
    m = _pair_masks()
    zero = jnp.zeros((), BF16)
    pairs = range(N_PAIRS)
    sl = [slice(p * PAIR_W, (p + 1) * PAIR_W) for p in pairs]

    def block_diag(x):
        xb = x.astype(BF16)
        return jnp.where(m["block_diag"], jnp.concatenate([xb, xb], axis=0), zero)

    def pair_mul(lhs, *rhs):
        rb = jnp.concatenate([block_diag(x) for x in rhs], axis=1) if len(rhs) > 1 else block_diag(rhs[0])
        out = jnp.dot(lhs.astype(BF16), rb, preferred_element_type=F32)
        return [out[:, i * PAIR_W:(i + 1) * PAIR_W] for i in range(len(rhs))]

    def each(fn, *lists):
        return [fn(*args) for args in zip(*lists)]

    lhs, rhs, vp, ends, gammas = [], [], [], [], []
    for c in range(chunks):
        rows = pl.ds(c * CHUNK, CHUNK)
        lw = lw_ref[rows, :]
        cum = _mask_dot(ltri, lw)
        cum_end = cum[CHUNK - 1:CHUNK, :]
        e_inv = jnp.exp(-cum)
        e_end = jnp.exp(cum_end - cum)
        kka = kka_ref[rows, :]
        k = k_ref[rows, :]
        a_bar = (-kk_ref[rows, :] * jnp.exp(cum - lw)).astype(BF16)
        r_bar = (r_ref[rows, :] * jnp.exp(cum)).astype(BF16)
        b_til = (kka * e_inv).astype(BF16)
        k_til = (k * e_inv).astype(BF16)
        b_end = (kka * e_end).astype(BF16)
        k_end = (k * e_end).astype(BF16)
        gamma = jnp.exp(cum_end)
        v = v_ref[rows, :].astype(BF16)

        for s in sl:
            lhs.append(jnp.concatenate([a_bar[:, s], r_bar[:, s]], axis=0))
            rhs.append(jnp.where(m["stacked_diag"], jnp.concatenate(
                [b_til[:, s], b_til[:, s], k_til[:, s], k_til[:, s]], axis=0), zero))
            vp.append(v[:, s])
            ends.append(jnp.concatenate([b_end[:, s], k_end[:, s]], axis=0))
            gammas.append(gamma[:, s])

    blocks = each(_bdot_nt, lhs, rhs)
    a_ab = [jnp.where(m["strict"], b[:CHUNK, :PAIR_W], 0.0) for b in blocks]
    a_ak = [jnp.where(m["strict"], b[:CHUNK, PAIR_W:], 0.0) for b in blocks]
    r_ab = [jnp.where(m["incl"], b[CHUNK:, :PAIR_W], 0.0) for b in blocks]
    r_ak = [jnp.where(m["incl"], b[CHUNK:, PAIR_W:], 0.0) for b in blocks]
    from_v = [jnp.dot(jnp.concatenate([ak, rk], axis=0).astype(BF16), block_diag(x),
                      preferred_element_type=F32) for ak, rk, x in zip(a_ak, r_ak, vp)]
    d = [jnp.where(m["same_block"], a, 0.0) for a in a_ab]
    low = [a - di for a, di in zip(a_ab, d)]
    pm = [m["eye"] + di for di in d]
    dpow = [pair_mul(di, di)[0] for di in d]
    for _ in range(INV_BLOCK.bit_length() - 3):
        both = each(lambda x, p: pair_mul(x, x, p), dpow, pm)
        dpow = [b[0] for b in both]
        pm = [p + b[1] for p, b in zip(pm, both)]
    pm = [p + pair_mul(x, p)[0] for p, x in zip(pm, dpow)]
    both = each(lambda p, lo, fv, l: pair_mul(p, lo, fv[:CHUNK], l[:CHUNK]), pm, low, from_v, lhs)
    both2 = each(lambda b: pair_mul(b[0], b[0], b[1], b[2]), both)
    xw_v = [b[1] + b2[1] for b, b2 in zip(both, both2)]
    xw_a = [b[2] + b2[2] for b, b2 in zip(both, both2)]
    both3 = each(lambda b2, xv, xa: pair_mul(b2[0], xv, xa), both2, xw_v, xw_a)
    u_v = [xv + b3[0] for xv, b3 in zip(xw_v, both3)]
    wm = [xa + b3[1] for xa, b3 in zip(xw_a, both3)]
    both4 = each(lambda rb, uv, w: pair_mul(rb, uv, w), r_ab, u_v, wm)
    y_local = [fv[CHUNK:] + b4[0] for fv, b4 in zip(from_v, both4)]
    rm = [l[CHUNK:] + b4[1] for l, b4 in zip(lhs, both4)]
    gm = [jnp.where(m["block_diag"], _bdot_tn(en[:CHUNK], w), 0.0) for en, w in zip(ends, wm)]
    qm = [jnp.where(m["block_diag"], _bdot_tn(jnp.concatenate([uv.astype(BF16), x], axis=0), en), 0.0)
          for uv, x, en in zip(u_v, vp, ends)]

    state = [st_ref[p] for p in pairs]
    for c in range(chunks):
        rows = pl.ds(c * CHUNK, CHUNK)
        item = slice(c * N_PAIRS, (c + 1) * N_PAIRS)
        y = [yl + _bdot_nt(r, st) for yl, r, st in zip(y_local[item], rm[item], state)]
        state = [st * g + _bdot_nt(st, gmat) + q
                 for st, g, gmat, q in zip(state, gammas[item], gm[item], qm[item])]
        outs = []
        inv_n = 1.0 / RWKV_HEAD
        for yp in y:
            def head_stat(z):
                s0 = jnp.sum(jnp.where(m["first_head"], z, 0.0), axis=-1, keepdims=True)
                s1 = jnp.sum(jnp.where(m["first_head"], 0.0, z), axis=-1, keepdims=True)
                return jnp.where(m["first_head"], s0, s1) * inv_n
            yc = yp - head_stat(yp)
            outs.append(yc * lax.rsqrt(head_stat(yc * yc) + RWKV_LNX_EPS))
        yn = jnp.concatenate(outs, axis=1) * lg_ref[...] + lb_ref[...]
        o_ref[rows, :] = ((yn + bonus_ref[rows, :]) * g_ref[rows, :]).astype(o_ref.dtype)
    for p in pairs:
        st_ref[p] = state[p]


def _rwkv_scan(prep, batch, seq, lnx_g, lnx_b):
    t = batch * seq
    chunks = 8
    rows = CHUNK * chunks
    nblk = seq // rows
    spec = pl.BlockSpec((rows, RWKV_WIDTH), lambda b, n: (b * nblk + n, 0))
    pspec = pl.BlockSpec((1, RWKV_WIDTH), lambda b, n: (0, 0))
    return pl.pallas_call(
        functools.partial(_rwkv_scan_kernel, chunks=chunks),
        out_shape=jax.ShapeDtypeStruct((t, RWKV_WIDTH), BF16),
        grid=(batch, nblk),
        in_specs=[spec] * 8 + [pspec, pspec],
        out_specs=spec,
        scratch_shapes=[pltpu.VMEM((N_PAIRS, PAIR_W, PAIR_W), F32)],
        compiler_params=_cparams("parallel", "arbitrary"),
    )(*prep, lnx_g, lnx_b)


def _merge_kernel(og_ref, ol_ref, or_ref, x_ref, wgt_ref, bgt_ref, pg_ref, pl_ref, pr_ref, wo_ref,
                  lg_ref, lb_ref, xo_ref):
    d = D_MODEL
    xin = x_ref[...].astype(BF16)
    merged = None
    for b, (o_ref, p_ref) in enumerate(((og_ref, pg_ref), (ol_ref, pl_ref), (or_ref, pr_ref))):
        cols = slice(b * d, (b + 1) * d)
        gate = _sigmoid(jnp.dot(xin, wgt_ref[:, cols], preferred_element_type=F32) + bgt_ref[:, cols])
        term = gate * jnp.dot(o_ref[...], p_ref[...], preferred_element_type=F32)
        merged = term if merged is None else merged + term
    z = DEEPNORM_ALPHA * x_ref[...] + _bdot(merged, wo_ref[...])
    out = _layer_norm_rows(z, lg_ref[...], lb_ref[...], LN_EPS)
    xo_ref[...] = out


def _merge(o_gla, o_lru, o_rwkv, x, w_gates, b_gates, p_gla, p_lru, p_rwkv, w_out, ln_g, ln_b):
    t = x.shape[0]
    tm = min(512, t)

    def rows(width):
        return pl.BlockSpec((tm, width), lambda i: (i, 0))

    def full(a):
        return pl.BlockSpec(a.shape, lambda i: (0,) * a.ndim)

    params = (w_gates, b_gates, p_gla, p_lru, p_rwkv, w_out, ln_g, ln_b)
    return pl.pallas_call(
        _merge_kernel,
        out_shape=jax.ShapeDtypeStruct((t, D_MODEL), F32),
        grid=(t // tm,),
        in_specs=[rows(GLA_VW), rows(LRU_WIDTH), rows(RWKV_WIDTH), rows(D_MODEL)] + [full(a) for a in params],
        out_specs=rows(D_MODEL),
        compiler_params=_cparams("parallel"),
    )(o_gla, o_lru, o_rwkv, x, *params)


def _ffn_kernel(x_ref, wg_ref, wu_ref, wd_ref, lg_ref, lb_ref, xo_ref, acc_ref):
    f = pl.program_id(1)

    @pl.when(f == 0)
    def _():
        acc_ref[...] = jnp.zeros_like(acc_ref)

    xb = x_ref[...].astype(BF16)
    g = jnp.dot(xb, wg_ref[...], preferred_element_type=F32)
    u = jnp.dot(xb, wu_ref[...], preferred_element_type=F32)
    acc_ref[...] += _bdot(g * _sigmoid(g) * u, wd_ref[...])

    @pl.when(f == pl.num_programs(1) - 1)
    def _():
        z = DEEPNORM_ALPHA * x_ref[...] + acc_ref[...]
        out = _layer_norm_rows(z, lg_ref[...], lb_ref[...], LN_EPS)
        xo_ref[...] = out


def _dense_ffn(x, wg, wu, wd, ln_g, ln_b):
    t = x.shape[0]
    tm = min(512, t)
    ff = wg.shape[1]
    tf = ff // 2
    rows = lambda width: pl.BlockSpec((tm, width), lambda i, f: (i, 0))
    vec = pl.BlockSpec((1, D_MODEL), lambda i, f: (0, 0))
    return pl.pallas_call(
        _ffn_kernel,
        out_shape=jax.ShapeDtypeStruct((t, D_MODEL), F32),
        grid=(t // tm, ff // tf),
        in_specs=[rows(D_MODEL),
                  pl.BlockSpec((D_MODEL, tf), lambda i, f: (0, f)),
                  pl.BlockSpec((D_MODEL, tf), lambda i, f: (0, f)),
                  pl.BlockSpec((tf, D_MODEL), lambda i, f: (f, 0)), vec, vec],
        out_specs=rows(D_MODEL),
        scratch_shapes=[pltpu.VMEM((tm, D_MODEL), F32)],
        compiler_params=_cparams("parallel", "arbitrary"),
    )(x, wg, wu, wd, ln_g, ln_b)


def _router_kernel(x_ref, wr_ref, meta_ref, wts_ref, cnt_ref, carry_ref):
    tm = x_ref.shape[0]

    @pl.when(pl.program_id(0) == 0)
    def _():
        carry_ref[...] = jnp.zeros_like(carry_ref)

    lane = lax.broadcasted_iota(I32, (tm, LANES), 1)
    neg = jnp.float32(-jnp.inf)
    logits = jnp.where(lane < N_EXPERTS, _fdot(x_ref[...], wr_ref[...]), neg)
    m1 = jnp.max(logits, axis=-1, keepdims=True)
    e1 = jnp.min(jnp.where(logits == m1, lane, LANES), axis=-1, keepdims=True)
    rest = jnp.where(lane == e1, neg, logits)
    m2 = jnp.max(rest, axis=-1, keepdims=True)
    e2 = jnp.min(jnp.where(rest == m2, lane, LANES), axis=-1, keepdims=True)
    ex = jnp.exp(m2 - m1)
    w1 = 1.0 / (1.0 + ex)
    w2 = ex / (1.0 + ex)

    hot1 = lane == e1
    hot2 = lane == e2
    onehot = jnp.where(hot1 | hot2, 1.0, 0.0)
    row, col = _tri_masks(tm)
    before = (row > col).astype(BF16)
    prefix = jnp.dot(before, onehot.astype(BF16), preferred_element_type=F32) + carry_ref[...]
    rank1 = jnp.sum(jnp.where(hot1, prefix, 0.0), axis=-1, keepdims=True).astype(I32)
    rank2 = jnp.sum(jnp.where(hot2, prefix, 0.0), axis=-1, keepdims=True).astype(I32)
    carry_ref[...] += jnp.sum(onehot, axis=0, keepdims=True)
    cnt_ref[...] = carry_ref[...]

    meta = jnp.where(lane == 0, e1, jnp.where(lane == 1, e2, jnp.where(lane == 2, rank1, rank2)))
    meta_ref[...] = jnp.where(lane < 4, meta, 0)
    wts_ref[...] = jnp.where(lane == 0, w1, jnp.where(lane == 1, w2, 0.0))


def _router(x, wr_pad):
    t = x.shape[0]
    tm = min(512, t)
    return pl.pallas_call(
        _router_kernel,
        out_shape=[jax.ShapeDtypeStruct((t, LANES), I32), jax.ShapeDtypeStruct((t, LANES), F32),
                   jax.ShapeDtypeStruct((1, LANES), F32)],
        grid=(t // tm,),
        in_specs=[pl.BlockSpec((tm, D_MODEL), lambda i: (i, 0)),
                  pl.BlockSpec((D_MODEL, LANES), lambda i: (0, 0))],
        out_specs=[pl.BlockSpec((tm, LANES), lambda i: (i, 0)), pl.BlockSpec((tm, LANES), lambda i: (i, 0)),
                   pl.BlockSpec((1, LANES), lambda i: (0, 0))],
        scratch_shapes=[pltpu.VMEM((1, LANES), F32)],
        compiler_params=_cparams("arbitrary"),
    )(x, wr_pad)


def _dispatch_kernel(dest_ref, x_ref, xs_in_ref, xs_ref, sem):
    del xs_in_ref
    tm = x_ref.shape[0]

    def row_copy(r, j):
        return pltpu.make_async_copy(x_ref.at[pl.ds(r, 1)], xs_ref.at[pl.ds(dest_ref[TOP_K * r + j], 1)], sem)

    def start(r, c):
        for j in range(TOP_K):
            row_copy(r, j).start()
        return c

    lax.fori_loop(0, tm, start, 0, unroll=8)
    for j in range(TOP_K):
        pltpu.make_async_copy(x_ref, xs_ref.at[pl.ds(0, tm)], sem).wait()


def _dispatch(x, dest_flat, n_slots):
    t = x.shape[0]
    tm = min(256, t)
    xs0 = jnp.zeros((n_slots, D_MODEL), F32)
    return pl.pallas_call(
        _dispatch_kernel,
        out_shape=jax.ShapeDtypeStruct((n_slots, D_MODEL), F32),
        grid=(t // tm,),
        in_specs=[pl.BlockSpec((tm * TOP_K,), lambda i: (i,), memory_space=pltpu.SMEM),
                  pl.BlockSpec((tm, D_MODEL), lambda i: (i, 0)),
                  pl.BlockSpec(memory_space=pl.ANY)],
        out_specs=pl.BlockSpec(memory_space=pl.ANY),
        scratch_shapes=[pltpu.SemaphoreType.DMA(())],
        input_output_aliases={2: 0},
        compiler_params=_cparams("arbitrary"),
    )(dest_flat, x, xs0)


def _expert_kernel(blk_e_ref, nused_ref, xs_ref, wg_ref, wu_ref, wd_ref, ys_ref, acc_ref):
    del blk_e_ref
    i = pl.program_id(0)
    f = pl.program_id(1)
    last = pl.num_programs(1) - 1
    used = i < nused_ref[0]

    @pl.when(used & (f == 0))
    def _():
        acc_ref[...] = jnp.zeros_like(acc_ref)

    @pl.when(used)
    def _():
        xb = xs_ref[...].astype(BF16)
        g = jnp.dot(xb, wg_ref[0], preferred_element_type=F32)
        u = jnp.dot(xb, wu_ref[0], preferred_element_type=F32)
        acc_ref[...] += _bdot(g * _sigmoid(g) * u, wd_ref[0])

    @pl.when(used & (f == last))
    def _():
        ys_ref[...] = acc_ref[...]

    @pl.when(jnp.logical_not(used) & (f == last))
    def _():
        ys_ref[...] = jnp.zeros_like(ys_ref)


def _experts(xs, blk_e, n_used, wg, wu, wd):
    n_slots = xs.shape[0]
    n_blocks = n_slots // MOE_ROWS
    ff = wg.shape[2]
    tf = ff // 2
    grid_spec = pltpu.PrefetchScalarGridSpec(
        num_scalar_prefetch=2,
        grid=(n_blocks, ff // tf),
        in_specs=[pl.BlockSpec((MOE_ROWS, D_MODEL), lambda i, f, be, nu: (i, 0)),
                  pl.BlockSpec((1, D_MODEL, tf), lambda i, f, be, nu: (be[i], 0, f)),
                  pl.BlockSpec((1, D_MODEL, tf), lambda i, f, be, nu: (be[i], 0, f)),
                  pl.BlockSpec((1, tf, D_MODEL), lambda i, f, be, nu: (be[i], f, 0))],
        out_specs=pl.BlockSpec((MOE_ROWS, D_MODEL), lambda i, f, be, nu: (i, 0)),
        scratch_shapes=[pltpu.VMEM((MOE_ROWS, D_MODEL), F32)],
    )
    return pl.pallas_call(
        _expert_kernel,
        out_shape=jax.ShapeDtypeStruct((n_slots, D_MODEL), F32),
        grid_spec=grid_spec,
        compiler_params=_cparams("arbitrary", "arbitrary"),
    )(blk_e, n_used, xs, wg, wu, wd)


def _combine_kernel(dest_ref, dest_next_ref, x_ref, wts_ref, ys_ref, lg_ref, lb_ref, xo_ref, buf_ref, sem):
    tm = x_ref.shape[0]
    i = pl.program_id(0)
    n = pl.num_programs(0)

    def gather(idx_ref, slot):
        def start(r, c):
            for j in range(TOP_K):
                pltpu.make_async_copy(ys_ref.at[pl.ds(idx_ref[TOP_K * r + j], 1)],
                                      buf_ref.at[slot, j, pl.ds(r, 1)], sem.at[slot]).start()
            return c
        lax.fori_loop(0, tm, start, 0, unroll=8)

    slot = i & 1

    @pl.when(i == 0)
    def _():
        gather(dest_ref, 0)

    @pl.when(i + 1 < n)
    def _():
        gather(dest_next_ref, 1 - slot)

    for j in range(TOP_K):
        pltpu.make_async_copy(ys_ref.at[pl.ds(0, tm)], buf_ref.at[slot, j], sem.at[slot]).wait()
    w = wts_ref[...]
    f = w[:, 0:1] * buf_ref[slot, 0] + w[:, 1:2] * buf_ref[slot, 1]
    z = DEEPNORM_ALPHA * x_ref[...] + f
    xo_ref[...] = _layer_norm_rows(z, lg_ref[...], lb_ref[...], LN_EPS)


def _combine(x, dest_flat, wts, ys, ln_g, ln_b):
    t = x.shape[0]
    tm = min(256, t)
    n_tiles = t // tm
    rows = lambda width: pl.BlockSpec((tm, width), lambda i: (i, 0))
    vec = pl.BlockSpec((1, D_MODEL), lambda i: (0, 0))
    return pl.pallas_call(
        _combine_kernel,
        out_shape=jax.ShapeDtypeStruct((t, D_MODEL), F32),
        grid=(n_tiles,),
        in_specs=[pl.BlockSpec((tm * TOP_K,), lambda i: (i,), memory_space=pltpu.SMEM),
                  pl.BlockSpec((tm * TOP_K,), lambda i: (jnp.minimum(i + 1, n_tiles - 1),),
                               memory_space=pltpu.SMEM),
                  rows(D_MODEL), rows(LANES), pl.BlockSpec(memory_space=pl.ANY), vec, vec],
        out_specs=rows(D_MODEL),
        scratch_shapes=[pltpu.VMEM((2, TOP_K, tm, D_MODEL), F32), pltpu.SemaphoreType.DMA((2,))],
        compiler_params=_cparams("arbitrary"),
    )(dest_flat, dest_flat, x, wts, ys, ln_g, ln_b)


def _moe_ffn(x, w_router, wg, wu, wd, ln_g, ln_b):
    t = x.shape[0]
    wr_pad = jnp.pad(w_router, ((0, 0), (0, LANES - N_EXPERTS)))
    meta, wts, cnt = _router(x, wr_pad)
    counts = cnt[0, :N_EXPERTS].astype(I32)
    padded = (counts + MOE_ROWS - 1) // MOE_ROWS * MOE_ROWS
    seg_end = jnp.cumsum(padded)
    seg_start = seg_end - padded
    n_blocks = (t * TOP_K) // MOE_ROWS + N_EXPERTS
    dest = seg_start[meta[:, 0:TOP_K]] + meta[:, TOP_K:2 * TOP_K]
    dest_flat = dest.reshape(-1).astype(I32)
    blk_start = jnp.arange(n_blocks, dtype=I32) * MOE_ROWS
    blk_e = jnp.minimum(jnp.sum(blk_start[:, None] >= seg_end[None, :], axis=1), N_EXPERTS - 1).astype(I32)
    n_used = (seg_end[-1:] // MOE_ROWS).astype(I32)
    xs = _dispatch(x, dest_flat, n_blocks * MOE_ROWS)
    ys = _experts(xs, blk_e, n_used, wg, wu, wd)
    return _combine(x, dest_flat, wts, ys, ln_g, ln_b)


def _pad_rows(w, rows, at=0):
    out = jnp.zeros((rows, w.shape[1]), w.dtype)
    return out.at[at:at + w.shape[0]].set(w)


def _reorder_in_projection(w, b):
    sizes = (GLA_KW, GLA_KW, GLA_VW, GLA_VW, GLA_DECAY_RANK, LRU_WIDTH, LRU_WIDTH,
             3 * RWKV_WIDTH, RWKV_DECAY_RANK, RWKV_A_RANK, RWKV_GATE_RANK, N_BRANCH * D_MODEL)
    offs = [0]
    for s in sizes:
        offs.append(offs[-1] + s)
    wb = jnp.concatenate([w, b[None, :]], axis=0)
    piece = lambda i: wb[:, offs[i]:offs[i + 1]]
    zeros = lambda n: jnp.zeros((wb.shape[0], n), wb.dtype)
    q, k, v, r, dec, lx, lg, rkv, wl, al, gl, gates = (piece(i) for i in range(len(sizes)))
    small = jnp.concatenate([wl, al, gl, zeros(RW_SMALL_W - SM_GL[0] - RWKV_GATE_RANK)], axis=1)
    groups = {"gla": jnp.concatenate([q, k, v, r, dec, zeros(GLA_DEC_W - GLA_DECAY_RANK)], axis=1),
              "lru": jnp.concatenate([lx, lg], axis=1),
              "rwkv": jnp.concatenate([rkv, small], axis=1),
              "gates": gates}
    return {name: (g[:-1].astype(BF16), g[-1:]) for name, g in groups.items()}


def _cast_kernel(x_ref, o_ref):
    o_ref[...] = x_ref[...].astype(o_ref.dtype)


CAST_TILE_ELEMS = 2 * 1024 * 1024


def _to_bf16(w):
    cols = w.shape[-1]
    w2 = w.reshape(-1, cols)
    n_rows = w2.shape[0]
    tr = n_rows
    while tr * cols > CAST_TILE_ELEMS and tr % 32 == 0:
        tr //= 2
    out = pl.pallas_call(
        _cast_kernel,
        out_shape=jax.ShapeDtypeStruct(w2.shape, BF16),
        grid=(n_rows // tr,),
        in_specs=[pl.BlockSpec((tr, cols), lambda i: (i, 0))],
        out_specs=pl.BlockSpec((tr, cols), lambda i: (i, 0)),
        compiler_params=_cparams("parallel"),
    )(w2)
    return out.reshape(w.shape)


def _block_diag(blocks):
    n, bi, bo = blocks.shape
    eye = jnp.eye(n, dtype=blocks.dtype)
    return (eye[:, None, :, None] * blocks[:, :, None, :]).reshape(n * bi, n * bo)


def kernel(x, w_in, b_in, gla_w_decay_up, gla_b_decay, gla_norm_g, gla_norm_b, lru_conv_w, lru_conv_b, lru_w_r, lru_b_r, lru_w_i, lru_b_i, lru_lambda, rwkv_mu, rwkv_w0, rwkv_w2, rwkv_a0, rwkv_a2, rwkv_g2, rwkv_k_k, rwkv_k_a, rwkv_r_k, rwkv_lnx_g, rwkv_lnx_b, p_gla, p_lru, p_rwkv, w_out, ln_mix_g, ln_mix_b, ffn_w_gate, ffn_w_up, ffn_w_down, moe_w_router, moe_w_gate, moe_w_up, moe_w_down, ln_ffn_g, ln_ffn_b):
    batch, seq, d = x.shape
    t = batch * seq
    xf = x.reshape(t, d)
    row = lambda a: a.reshape(1, -1)
    head_ones = _block_diag(jnp.ones((RWKV_HEADS, RWKV_HEAD, RWKV_HEAD), BF16))
    for l in range(DEPTH):
        proj = _reorder_in_projection(w_in[l], b_in[l])
        w_gates, b_gates = proj["gates"]

        wup = _pad_rows(gla_w_decay_up[l], GLA_DEC_W).astype(BF16)
        o_gla = _gla_branch(xf, batch, seq, *proj["gla"], wup, row(gla_b_decay[l]), row(gla_norm_g[l]),
                            row(gla_norm_b[l]))

        w_ri = jnp.concatenate([_block_diag(lru_w_r[l]), _block_diag(lru_w_i[l])], axis=1).astype(BF16)
        b_ri = jnp.concatenate([lru_b_r[l], lru_b_i[l]])[None, :]
        o_lru = _lru_branch(xf, batch, seq, *proj["lru"], lru_conv_w[l], row(lru_conv_b[l]), w_ri, b_ri,
                            row(jax.nn.softplus(-lru_lambda[l])))

        mu = rwkv_mu[l]
        mu_all = jnp.zeros((3 * RWKV_WIDTH + RW_SMALL_W,), F32).at[0:mu.shape[0]].set(mu)
        prep_params = (*proj["rwkv"], row(mu_all),
                       _pad_rows(rwkv_w2[l], SM_WA[1], 0).astype(BF16),
                       _pad_rows(rwkv_a2[l], SM_WA[1], RWKV_DECAY_RANK).astype(BF16),
                       _pad_rows(rwkv_g2[l], SM_GL[1], 0).astype(BF16),
                       row(rwkv_w0[l]), row(rwkv_a0[l]), row(rwkv_k_k[l]), row(rwkv_k_a[l]),
                       row(rwkv_r_k[l]), head_ones)
        prep = _rwkv_prep(xf, batch, seq, prep_params)
        o_rwkv = _rwkv_scan(prep, batch, seq, row(rwkv_lnx_g[l]), row(rwkv_lnx_b[l]))

        xf = _merge(o_gla, o_lru, o_rwkv, xf, w_gates, b_gates, p_gla[l].astype(BF16), p_lru[l].astype(BF16),
                    p_rwkv[l].astype(BF16), w_out[l].astype(BF16), row(ln_mix_g[l]), row(ln_mix_b[l]))
        i = l // 2
        if l % 2 == 0:
            xf = _dense_ffn(xf, ffn_w_gate[i].astype(BF16), ffn_w_up[i].astype(BF16),
                            ffn_w_down[i].astype(BF16), row(ln_ffn_g[l]), row(ln_ffn_b[l]))
        else:
            xf = _moe_ffn(xf, moe_w_router[i], _to_bf16(moe_w_gate[i]), _to_bf16(moe_w_up[i]),
                          _to_bf16(moe_w_down[i]), row(ln_ffn_g[l]), row(ln_ffn_b[l]))
    return xf.reshape(batch, seq, d)
```

```python
import functools

import jax
import jax.numpy as jnp
from jax import lax
from jax.experimental import pallas as pl
from jax.experimental.pallas import tpu as pltpu

F32 = jnp.float32
BF16 = jnp.bfloat16
I32 = jnp.int32
HIGHEST = lax.Precision.HIGHEST

D_MODEL = 1024
DEPTH = 2
GLA_HEADS, GLA_DK, GLA_DV = 4, 64, 128
GLA_KW, GLA_VW = GLA_HEADS * GLA_DK, GLA_HEADS * GLA_DV
GLA_DECAY_RANK = 16
GLA_TEMP = 16.0
LRU_WIDTH, LRU_BLOCKS, LRU_CONV, LRU_C = 512, 8, 4, 8.0
RWKV_HEAD, RWKV_WIDTH = 64, 512
RWKV_HEADS = RWKV_WIDTH // RWKV_HEAD
RWKV_DECAY_RANK, RWKV_A_RANK, RWKV_GATE_RANK = 64, 64, 160
RWKV_LNX_EPS = 64e-5
N_BRANCH = 3
N_EXPERTS, TOP_K = 8, 2
DEEPNORM_ALPHA = (2 * DEPTH) ** 0.25
LN_EPS = 1e-5

LANES = 128
SUBLANES = 8
VMEM_LIMIT_BYTES = 56 * 1024 * 1024

SM_WA = (0, 128)
SM_GL = (128, 256)
RW_SMALL_W = 512
GLA_DEC_W = LANES
GLA_COLS = {"q": (0, 256), "k": (256, 512), "v": (512, 1024), "r": (1024, 1536), "dec": (1536, 1664)}

CHUNK = 64
INV_BLOCK = 16
MOE_ROWS = 512


def _cparams(*sem):
    return pltpu.CompilerParams(dimension_semantics=sem, vmem_limit_bytes=VMEM_LIMIT_BYTES)


def _sigmoid(x):
    return 1.0 / (1.0 + jnp.exp(-x))


def _softplus(x):
    return jnp.maximum(x, 0.0) + jnp.log(1.0 + jnp.exp(-jnp.abs(x)))


def _bdot(a, b):
    return jnp.dot(a.astype(BF16), b.astype(BF16), preferred_element_type=F32)


def _bdot_nt(a, b):
    return lax.dot_general(a.astype(BF16), b.astype(BF16), (((1,), (1,)), ((), ())),
                           preferred_element_type=F32)


def _bdot_tn(a, b):
    return lax.dot_general(a.astype(BF16), b.astype(BF16), (((0,), (0,)), ((), ())),
                           preferred_element_type=F32)


def _fdot(a, b):
    ah = a.astype(BF16)
    al = (a - ah.astype(F32)).astype(BF16)
    bh = b.astype(BF16)
    bl = (b - bh.astype(F32)).astype(BF16)
    dot = lambda u, v: jnp.dot(u, v, preferred_element_type=F32)
    return dot(al, bh) + dot(ah, bl) + dot(ah, bh)


def _split3(x):
    hi = x.astype(BF16)
    rest = x - hi.astype(F32)
    mid = rest.astype(BF16)
    lo = (rest - mid.astype(F32)).astype(BF16)
    return hi, mid, lo


def _mask_dot(mask_bf16, x):
    hi, mid, lo = _split3(x)
    dot = lambda part: jnp.dot(mask_bf16, part, preferred_element_type=F32)
    return dot(lo) + dot(mid) + dot(hi)


def _dot_mask(x, mask_bf16):
    hi, mid, lo = _split3(x)
    dot = lambda part: jnp.dot(part, mask_bf16, preferred_element_type=F32)
    return dot(lo) + dot(mid) + dot(hi)


def _layer_norm_rows(z, g, b, eps):
    mu = jnp.mean(z, axis=-1, keepdims=True)
    zc = z - mu
    var = jnp.mean(zc * zc, axis=-1, keepdims=True)
    return zc * lax.rsqrt(var + eps) * g + b


def _tri_masks(n):
    row = lax.broadcasted_iota(I32, (n, n), 0)
    col = lax.broadcasted_iota(I32, (n, n), 1)
    return row, col


def _gla_kernel(x_ref, w_ref, b_ref, wup_ref, bdec_ref, ng_ref, nb_ref, o_ref, st_ref, h_ref, *, chunks):
    @pl.when(pl.program_id(1) == 0)
    def _():
        st_ref[...] = jnp.zeros_like(st_ref)

    h_ref[...] = jnp.dot(x_ref[...].astype(BF16), w_ref[...], preferred_element_type=F32) + b_ref[...]
    q_ref = h_ref.at[:, GLA_COLS["q"][0]:GLA_COLS["q"][1]]
    k_ref = h_ref.at[:, GLA_COLS["k"][0]:GLA_COLS["k"][1]]
    v_ref = h_ref.at[:, GLA_COLS["v"][0]:GLA_COLS["v"][1]]
    r_ref = h_ref.at[:, GLA_COLS["r"][0]:GLA_COLS["r"][1]]
    dec_ref = h_ref.at[:, GLA_COLS["dec"][0]:GLA_COLS["dec"][1]]

    row, col = _tri_masks(CHUNK)
    ltri = (row >= col).astype(BF16)
    kw, vw = 2 * GLA_DK, 2 * GLA_DV
    key_shift, val_shift = GLA_DK.bit_length() - 1, GLA_DV.bit_length() - 1
    iota = lambda shape, axis: lax.broadcasted_iota(I32, shape, axis)
    causal = iota((CHUNK, kw), 0) >= (iota((CHUNK, kw), 1) & (GLA_DK - 1))
    key_diag = (iota((kw, kw), 0) >> key_shift) == (iota((kw, kw), 1) >> key_shift)
    val_diag = (iota((kw, vw), 0) >> key_shift) == (iota((kw, vw), 1) >> val_shift)
    state_diag = (iota((vw, kw), 0) >> val_shift) == (iota((vw, kw), 1) >> key_shift)
    zero = jnp.zeros((), BF16)
    pairs = range(GLA_HEADS // 2)
    qp, vp, k_diag, kep, decays = [], [], [], [], []
    for c in range(chunks):
        rows = pl.ds(c * CHUNK, CHUNK)
        logits = _bdot(dec_ref[rows, :], wup_ref[...]) + bdec_ref[...]
        log_a = -_softplus(-logits) * (1.0 / GLA_TEMP)
        cum = _mask_dot(ltri, log_a)
        cum_end = cum[CHUNK - 1:CHUNK, :]
        q = q_ref[rows, :] * (GLA_DK ** -0.5)
        k = k_ref[rows, :]
        q_dec = (q * jnp.exp(cum)).astype(BF16)
        k_inv = (k * jnp.exp(-cum)).astype(BF16)
        k_end = (k * jnp.exp(cum_end - cum)).astype(BF16)
        decay_end = jnp.exp(cum_end)
        v = v_ref[rows, :].astype(BF16)
        for p in pairs:
            s = slice(p * kw, (p + 1) * kw)
            qp.append(q_dec[:, s])
            vp.append(v[:, p * vw:(p + 1) * vw])
            k_diag.append(jnp.where(key_diag, jnp.concatenate([k_inv[:, s], k_inv[:, s]], axis=0), zero))
            kep.append(k_end[:, s])
            decays.append(decay_end[:, s])

    v_diag = [jnp.where(val_diag, jnp.concatenate([x, x], axis=0), zero) for x in vp]
    scores = [jnp.where(causal, _bdot_nt(q, kd), 0.0) for q, kd in zip(qp, k_diag)]
    o_local = [_bdot(s, vd) for s, vd in zip(scores, v_diag)]
    upd = [jnp.where(state_diag, _bdot_tn(x, ke), 0.0) for x, ke in zip(vp, kep)]

    state = [st_ref[p] for p in pairs]
    n_pairs = len(pairs)
    for c in range(chunks):
        rows = pl.ds(c * CHUNK, CHUNK)
        item = slice(c * n_pairs, (c + 1) * n_pairs)
        o = [ol + _bdot_nt(q, st) for ol, q, st in zip(o_local[item], qp[item], state)]
        state = [st * dcy + u for st, dcy, u in zip(state, decays[item], upd[item])]
        outs = []
        for op in o:
            for oh in (op[:, :GLA_DV], op[:, GLA_DV:]):
                mu = jnp.mean(oh, axis=-1, keepdims=True)
                oc = oh - mu
                var = jnp.mean(oc * oc, axis=-1, keepdims=True)
                outs.append(oc * lax.rsqrt(var + LN_EPS))
        y = jnp.concatenate(outs, axis=1) * ng_ref[...] + nb_ref[...]
        r = r_ref[rows, :]
        o_ref[rows, :] = (y * (r * _sigmoid(r))).astype(o_ref.dtype)
    for p in pairs:
        st_ref[p] = state[p]


def _gla_branch(x, batch, seq, w, bias, wup_pad, b_decay, norm_g, norm_b):
    t = batch * seq
    chunks = 8
    rows = CHUNK * chunks
    nblk = seq // rows

    def full(a):
        return pl.BlockSpec(a.shape, lambda b, n: (0,) * a.ndim)

    args = (w, bias, wup_pad, b_decay, norm_g, norm_b)
    return pl.pallas_call(
        functools.partial(_gla_kernel, chunks=chunks),
        out_shape=jax.ShapeDtypeStruct((t, GLA_VW), BF16),
        grid=(batch, nblk),
        in_specs=[pl.BlockSpec((rows, D_MODEL), lambda b, n: (b * nblk + n, 0))] + [full(a) for a in args],
        out_specs=pl.BlockSpec((rows, GLA_VW), lambda b, n: (b * nblk + n, 0)),
        scratch_shapes=[pltpu.VMEM((GLA_HEADS // 2, 2 * GLA_DV, 2 * GLA_DK), F32),
                        pltpu.VMEM((rows, w.shape[1]), F32)],
        compiler_params=_cparams("parallel", "arbitrary"),
    )(x, *args)


def _lru_kernel(xin_ref, w_ref, b_ref, cw_ref, cb_ref, wri_ref, bri_ref, sp_ref,
                o_ref, xbuf_ref, a_ref, u_ref, hs_ref, gate_ref, h_ref):
    tm = xin_ref.shape[0]
    pad = SUBLANES

    @pl.when(pl.program_id(1) == 0)
    def _():
        xbuf_ref[0:pad, :] = jnp.zeros((pad, LRU_WIDTH), F32)
        h_ref[...] = jnp.zeros_like(h_ref)

    proj = jnp.dot(xin_ref[...].astype(BF16), w_ref[...], preferred_element_type=F32) + b_ref[...]
    x = proj[:, :LRU_WIDTH]
    gate_ref[...] = jax.nn.gelu(proj[:, LRU_WIDTH:])
    xbuf_ref[pad:pad + tm, :] = x
    xc = cb_ref[...] + x * cw_ref[LRU_CONV - 1:LRU_CONV, :]
    for j in range(LRU_CONV - 1):
        back = LRU_CONV - 1 - j
        xc = xc + xbuf_ref[pl.ds(pad - back, tm), :] * cw_ref[j:j + 1, :]
    xbuf_ref[0:pad, :] = x[tm - pad:tm, :]

    ri = _sigmoid(_bdot(xc, wri_ref[...]) + bri_ref[...])
    rg, ig = ri[:, :LRU_WIDTH], ri[:, LRU_WIDTH:]
    log_a = -LRU_C * rg * sp_ref[...]
    a = jnp.exp(log_a)
    a_ref[...] = a
    u_ref[...] = jnp.sqrt(1.0 - a * a) * (ig * xc)

    def step(t, hprev):
        hnew = a_ref[pl.ds(t, 1), :] * hprev + u_ref[pl.ds(t, 1), :]
        hs_ref[pl.ds(t, 1), :] = hnew
        return hnew

    h_ref[...] = lax.fori_loop(0, tm, step, h_ref[...], unroll=8)
    o_ref[...] = (gate_ref[...] * hs_ref[...]).astype(o_ref.dtype)


def _lru_branch(x, batch, seq, w, bias, conv_w, conv_b, w_ri, b_ri, softplus_neg_lam):
    t = batch * seq
    tm = min(512, seq)
    nblk = seq // tm

    def full(a):
        return pl.BlockSpec(a.shape, lambda b, n: (0,) * a.ndim)

    args = (w, bias, conv_w, conv_b, w_ri, b_ri, softplus_neg_lam)
    return pl.pallas_call(
        _lru_kernel,
        out_shape=jax.ShapeDtypeStruct((t, LRU_WIDTH), BF16),
        grid=(batch, nblk),
        in_specs=[pl.BlockSpec((tm, D_MODEL), lambda b, n: (b * nblk + n, 0))] + [full(a) for a in args],
        out_specs=pl.BlockSpec((tm, LRU_WIDTH), lambda b, n: (b * nblk + n, 0)),
        scratch_shapes=[pltpu.VMEM((tm + SUBLANES, LRU_WIDTH), F32)]
        + [pltpu.VMEM((tm, LRU_WIDTH), F32)] * 4 + [pltpu.VMEM((1, LRU_WIDTH), F32)],
        compiler_params=_cparams("parallel", "arbitrary"),
    )(x, *args)


def _rwkv_prep_kernel(*refs, blocks_per_seq):
    *io_refs, ha_ref, hb_ref, carry_ref = refs
    tm = io_refs[0].shape[0]
    s = pl.program_id(0)
    body = functools.partial(_rwkv_prep_body, tm, s, RWKV_WIDTH, blocks_per_seq, (*io_refs, carry_ref))

    @pl.when(s == 0)
    def _():
        hb_ref[...] = jnp.zeros_like(hb_ref)
        carry_ref[...] = jnp.zeros_like(carry_ref)

    @pl.when((s & 1) == 0)
    def _():
        body(ha_ref, hb_ref)

    @pl.when((s & 1) == 1)
    def _():
        body(hb_ref, ha_ref)


def _rwkv_prep_body(tm, s, gw, blocks_per_seq, refs, h_next, h_cur):
    (x_ref, w_ref, b_ref, mu_ref, w2_ref, a2_ref, g2_ref, w0_ref, a0_ref, kk_ref, ka_ref, rk_ref, bd_ref,
     r_out, k_out, v_out, lw_out, kk_out, kka_out, bonus_out, g_out, carry_ref) = refs
    first = ((s + blocks_per_seq - 1) % blocks_per_seq) == 0
    keep = jnp.where(first, 0.0, 1.0)
    row0 = lax.broadcasted_iota(I32, (tm, 1), 0) == 0
    xb = x_ref[...].astype(BF16)

    pw = gw // 2

    def project(j):
        cols = slice(j * pw, (j + 1) * pw)
        h_next[:, cols] = jnp.dot(xb, w_ref[:, cols], preferred_element_type=F32) + b_ref[:, cols]

    def shifted(j):
        cols = slice(j * gw, (j + 1) * gw)
        cur = h_cur[:, cols]
        prev_row = carry_ref[0:1, cols] * keep
        carry_ref[0:1, cols] = cur[tm - 1:tm, :]
        prev = jnp.where(row0, prev_row, pltpu.roll(cur, 1, axis=0))
        return cur + (prev - cur) * mu_ref[:, cols]

    project(0)
    sm = shifted(3)
    wa = sm[:, SM_WA[0]:SM_WA[0] + SM_WA[1]]
    gl = sm[:, SM_GL[0]:SM_GL[0] + SM_GL[1]]
    project(1)
    w_log = -_softplus(-(w0_ref[...] + _bdot(jnp.tanh(wa), w2_ref[...]))) - 0.5
    lw_out[...] = -jnp.exp(w_log)
    project(2)
    a = _sigmoid(a0_ref[...] + _bdot(wa, a2_ref[...]))
    g_out[...] = _bdot(_sigmoid(gl), g2_ref[...])
    project(3)
    k = shifted(1)
    kk = k * kk_ref[...]
    norm = jnp.sqrt(_dot_mask(kk * kk, bd_ref[...]))
    project(4)
    kk = kk / jnp.maximum(norm, 1e-12)
    kk_out[...] = kk
    kka_out[...] = kk * a
    k2 = k * (1.0 + (a - 1.0) * ka_ref[...])
    k_out[...] = k2
    project(5)
    r = shifted(0)
    r_out[...] = r
    project(6)
    v = shifted(2)
    v_out[...] = v
    project(7)
    bonus_out[...] = _dot_mask(r * k2 * rk_ref[...], bd_ref[...]) * v


def _rwkv_prep(x, batch, seq, params):
    t = batch * seq
    tm = min(512, seq)
    bps = seq // tm
    n_tiles = t // tm
    width = params[0].shape[1]

    def full(a):
        return pl.BlockSpec(a.shape, lambda s: (0,) * a.ndim)

    out = jax.ShapeDtypeStruct((t, RWKV_WIDTH), F32)
    ospec = pl.BlockSpec((tm, RWKV_WIDTH), lambda s: (jnp.maximum(s - 1, 0), 0))
    return pl.pallas_call(
        functools.partial(_rwkv_prep_kernel, blocks_per_seq=bps),
        out_shape=[out] * 8,
        grid=(n_tiles + 1,),
        in_specs=[pl.BlockSpec((tm, D_MODEL), lambda s: (jnp.minimum(s, n_tiles - 1), 0))]
        + [full(a) for a in params],
        out_specs=[ospec] * 8,
        scratch_shapes=[pltpu.VMEM((tm, width), F32), pltpu.VMEM((tm, width), F32),
                        pltpu.VMEM((SUBLANES, width), F32)],
        compiler_params=_cparams("arbitrary"),
    )(x, *params)


PAIR_W = 2 * RWKV_HEAD
N_PAIRS = RWKV_HEADS // 2
HEAD_SHIFT = RWKV_HEAD.bit_length() - 1


def _pair_masks():
    t = lax.broadcasted_iota(I32, (CHUNK, PAIR_W), 0)
    lane = lax.broadcasted_iota(I32, (CHUNK, PAIR_W), 1)
    j = lane & (RWKV_HEAD - 1)
    r = lax.broadcasted_iota(I32, (PAIR_W, PAIR_W), 0)
    c = lax.broadcasted_iota(I32, (PAIR_W, PAIR_W), 1)
    r4 = lax.broadcasted_iota(I32, (2 * PAIR_W, PAIR_W), 0)
    c4 = lax.broadcasted_iota(I32, (2 * PAIR_W, PAIR_W), 1)
    return dict(
        strict=t > j, incl=t >= j, eye=(t == j).astype(F32),
        same_block=jnp.bitwise_xor(t, j) < INV_BLOCK,
        first_head=lane < RWKV_HEAD,
        block_diag=(r >> HEAD_SHIFT) == (c >> HEAD_SHIFT),
        stacked_diag=((r4 >> HEAD_SHIFT) & 1) == (c4 >> HEAD_SHIFT))


def _rwkv_scan_kernel(r_ref, k_ref, v_ref, lw_ref, kk_ref, kka_ref, bonus_ref, g_ref, lg_ref, lb_ref,
                      o_ref, st_ref, *, chunks):
    @pl.when(pl.program_id(1) == 0)
    def _():
        st_ref[...] = jnp.zeros_like(st_ref)

    row, col = _tri_masks(CHUNK)
    ltri = (row >= col).astype(BF16)
    m = _pair_masks()
    zero = jnp.zeros((), BF16)
    pairs = range(N_PAIRS)
    sl = [slice(p * PAIR_W, (p + 1) * PAIR_W) for p in pairs]

    def block_diag(x):
        xb = x.astype(BF16)
        return jnp.where(m["block_diag"], jnp.concatenate([xb, xb], axis=0), zero)

    def pair_mul(lhs, *rhs):
        rb = jnp.concatenate([block_diag(x) for x in rhs], axis=1) if len(rhs) > 1 else block_diag(rhs[0])
        out = jnp.dot(lhs.astype(BF16), rb, preferred_element_type=F32)
        return [out[:, i * PAIR_W:(i + 1) * PAIR_W] for i in range(len(rhs))]

    def each(fn, *lists):
        return [fn(*args) for args in zip(*lists)]

    lhs, rhs, vp, ends, gammas = [], [], [], [], []
    for c in range(chunks):
        rows = pl.ds(c * CHUNK, CHUNK)
        lw = lw_ref[rows, :]
        cum = _mask_dot(ltri, lw)
        cum_end = cum[CHUNK - 1:CHUNK, :]
        e_inv = jnp.exp(-cum)
        e_end = jnp.exp(cum_end - cum)
        kka = kka_ref[rows, :]
        k = k_ref[rows, :]
        a_bar = (-kk_ref[rows, :] * jnp.exp(cum - lw)).astype(BF16)
        r_bar = (r_ref[rows, :] * jnp.exp(cum)).astype(BF16)
        b_til = (kka * e_inv).astype(BF16)
        k_til = (k * e_inv).astype(BF16)
        b_end = (kka * e_end).astype(BF16)
        k_end = (k * e_end).astype(BF16)
        gamma = jnp.exp(cum_end)
        v = v_ref[rows, :].astype(BF16)

        for s in sl:
            lhs.append(jnp.concatenate([a_bar[:, s], r_bar[:, s]], axis=0))
            rhs.append(jnp.where(m["stacked_diag"], jnp.concatenate(
                [b_til[:, s], b_til[:, s], k_til[:, s], k_til[:, s]], axis=0), zero))
            vp.append(v[:, s])
            ends.append(jnp.concatenate([b_end[:, s], k_end[:, s]], axis=0))
            gammas.append(gamma[:, s])

    blocks = each(_bdot_nt, lhs, rhs)
    a_ab = [jnp.where(m["strict"], b[:CHUNK, :PAIR_W], 0.0) for b in blocks]
    a_ak = [jnp.where(m["strict"], b[:CHUNK, PAIR_W:], 0.0) for b in blocks]
    r_ab = [jnp.where(m["incl"], b[CHUNK:, :PAIR_W], 0.0) for b in blocks]
    r_ak = [jnp.where(m["incl"], b[CHUNK:, PAIR_W:], 0.0) for b in blocks]
    from_v = [jnp.dot(jnp.concatenate([ak, rk], axis=0).astype(BF16), block_diag(x),
                      preferred_element_type=F32) for ak, rk, x in zip(a_ak, r_ak, vp)]
    d = [jnp.where(m["same_block"], a, 0.0) for a in a_ab]
    low = [a - di for a, di in zip(a_ab, d)]
    pm = [m["eye"] + di for di in d]
    dpow = [pair_mul(di, di)[0] for di in d]
    for _ in range(INV_BLOCK.bit_length() - 3):
        both = each(lambda x, p: pair_mul(x, x, p), dpow, pm)
        dpow = [b[0] for b in both]
        pm = [p + b[1] for p, b in zip(pm, both)]
    pm = [p + pair_mul(x, p)[0] for p, x in zip(pm, dpow)]
    both = each(lambda p, lo, fv, l: pair_mul(p, lo, fv[:CHUNK], l[:CHUNK]), pm, low, from_v, lhs)
    both2 = each(lambda b: pair_mul(b[0], b[0], b[1], b[2]), both)
    xw_v = [b[1] + b2[1] for b, b2 in zip(both, both2)]
    xw_a = [b[2] + b2[2] for b, b2 in zip(both, both2)]
    both3 = each(lambda b2, xv, xa: pair_mul(b2[0], xv, xa), both2, xw_v, xw_a)
    u_v = [xv + b3[0] for xv, b3 in zip(xw_v, both3)]
    wm = [xa + b3[1] for xa, b3 in zip(xw_a, both3)]
    both4 = each(lambda rb, uv, w: pair_mul(rb, uv, w), r_ab, u_v, wm)
    y_local = [fv[CHUNK:] + b4[0] for fv, b4 in zip(from_v, both4)]
    rm = [l[CHUNK:] + b4[1] for l, b4 in zip(lhs, both4)]
    gm = [jnp.where(m["block_diag"], _bdot_tn(en[:CHUNK], w), 0.0) for en, w in zip(ends, wm)]
    qm = [jnp.where(m["block_diag"], _bdot_tn(jnp.concatenate([uv.astype(BF16), x], axis=0), en), 0.0)
          for uv, x, en in zip(u_v, vp, ends)]

    state = [st_ref[p] for p in pairs]
    for c in range(chunks):
        rows = pl.ds(c * CHUNK, CHUNK)
        item = slice(c * N_PAIRS, (c + 1) * N_PAIRS)
        y = [yl + _bdot_nt(r, st) for yl, r, st in zip(y_local[item], rm[item], state)]
        state = [st * g + _bdot_nt(st, gmat) + q
                 for st, g, gmat, q in zip(state, gammas[item], gm[item], qm[item])]
        outs = []
        inv_n = 1.0 / RWKV_HEAD
        for yp in y:
            def head_stat(z):
                s0 = jnp.sum(jnp.where(m["first_head"], z, 0.0), axis=-1, keepdims=True)
                s1 = jnp.sum(jnp.where(m["first_head"], 0.0, z), axis=-1, keepdims=True)
                return jnp.where(m["first_head"], s0, s1) * inv_n
            yc = yp - head_stat(yp)
            outs.append(yc * lax.rsqrt(head_stat(yc * yc) + RWKV_LNX_EPS))
        yn = jnp.concatenate(outs, axis=1) * lg_ref[...] + lb_ref[...]
        o_ref[rows, :] = ((yn + bonus_ref[rows, :]) * g_ref[rows, :]).astype(o_ref.dtype)
    for p in pairs:
        st_ref[p] = state[p]


def _rwkv_scan(prep, batch, seq, lnx_g, lnx_b):
    t = batch * seq
    chunks = 8
    rows = CHUNK * chunks
    nblk = seq // rows
    spec = pl.BlockSpec((rows, RWKV_WIDTH), lambda b, n: (b * nblk + n, 0))
    pspec = pl.BlockSpec((1, RWKV_WIDTH), lambda b, n: (0, 0))
    return pl.pallas_call(
        functools.partial(_rwkv_scan_kernel, chunks=chunks),
        out_shape=jax.ShapeDtypeStruct((t, RWKV_WIDTH), BF16),
        grid=(batch, nblk),
        in_specs=[spec] * 8 + [pspec, pspec],
        out_specs=spec,
        scratch_shapes=[pltpu.VMEM((N_PAIRS, PAIR_W, PAIR_W), F32)],
        compiler_params=_cparams("parallel", "arbitrary"),
    )(*prep, lnx_g, lnx_b)


def _merge_kernel(og_ref, ol_ref, or_ref, x_ref, wgt_ref, bgt_ref, pg_ref, pl_ref, pr_ref, wo_ref,
                  lg_ref, lb_ref, xo_ref):
    d = D_MODEL
    xin = x_ref[...].astype(BF16)
    merged = None
    for b, (o_ref, p_ref) in enumerate(((og_ref, pg_ref), (ol_ref, pl_ref), (or_ref, pr_ref))):
        cols = slice(b * d, (b + 1) * d)
        gate = _sigmoid(jnp.dot(xin, wgt_ref[:, cols], preferred_element_type=F32) + bgt_ref[:, cols])
        term = gate * jnp.dot(o_ref[...], p_ref[...], preferred_element_type=F32)
        merged = term if merged is None else merged + term
    z = DEEPNORM_ALPHA * x_ref[...] + _bdot(merged, wo_ref[...])
    out = _layer_norm_rows(z, lg_ref[...], lb_ref[...], LN_EPS)
    xo_ref[...] = out


def _merge(o_gla, o_lru, o_rwkv, x, w_gates, b_gates, p_gla, p_lru, p_rwkv, w_out, ln_g, ln_b):
    t = x.shape[0]
    tm = min(512, t)

    def rows(width):
        return pl.BlockSpec((tm, width), lambda i: (i, 0))

    def full(a):
        return pl.BlockSpec(a.shape, lambda i: (0,) * a.ndim)

    params = (w_gates, b_gates, p_gla, p_lru, p_rwkv, w_out, ln_g, ln_b)
    return pl.pallas_call(
        _merge_kernel,
        out_shape=jax.ShapeDtypeStruct((t, D_MODEL), F32),
        grid=(t // tm,),
        in_specs=[rows(GLA_VW), rows(LRU_WIDTH), rows(RWKV_WIDTH), rows(D_MODEL)] + [full(a) for a in params],
        out_specs=rows(D_MODEL),
        compiler_params=_cparams("parallel"),
    )(o_gla, o_lru, o_rwkv, x, *params)


FFN_SPLIT = 2


def _ffn_kernel(x_ref, wg_ref, wu_ref, wd_ref, lg_ref, lb_ref, xo_ref):
    x = x_ref[...]
    xb = x.astype(BF16)
    tf = wg_ref.shape[1] // FFN_SPLIT
    acc = None
    for f in range(FFN_SPLIT):
        cols = slice(f * tf, (f + 1) * tf)
        g = jnp.dot(xb, wg_ref[:, cols], preferred_element_type=F32)
        u = jnp.dot(xb, wu_ref[:, cols], preferred_element_type=F32)
        part = _bdot(g * _sigmoid(g) * u, wd_ref[cols, :])
        acc = part if acc is None else acc + part
    z = DEEPNORM_ALPHA * x + acc
    xo_ref[...] = _layer_norm_rows(z, lg_ref[...], lb_ref[...], LN_EPS)


def _dense_ffn(x, wg, wu, wd, ln_g, ln_b):
    t = x.shape[0]
    tm = min(512, t)
    rows = pl.BlockSpec((tm, D_MODEL), lambda i: (i, 0))
    resident = lambda a: pl.BlockSpec(a.shape, lambda i: (0,) * a.ndim, pipeline_mode=pl.Buffered(1))
    return pl.pallas_call(
        _ffn_kernel,
        out_shape=jax.ShapeDtypeStruct((t, D_MODEL), F32),
        grid=(t // tm,),
        in_specs=[rows] + [resident(a) for a in (wg, wu, wd, ln_g, ln_b)],
        out_specs=rows,
        compiler_params=_cparams("parallel"),
    )(x, wg, wu, wd, ln_g, ln_b)


def _router_kernel(x_ref, wr_ref, meta_ref, wts_ref, cnt_ref, carry_ref):
    tm = x_ref.shape[0]

    @pl.when(pl.program_id(0) == 0)
    def _():
        carry_ref[...] = jnp.zeros_like(carry_ref)

    lane = lax.broadcasted_iota(I32, (tm, LANES), 1)
    neg = jnp.float32(-jnp.inf)
    logits = jnp.where(lane < N_EXPERTS, _fdot(x_ref[...], wr_ref[...]), neg)
    m1 = jnp.max(logits, axis=-1, keepdims=True)
    e1 = jnp.min(jnp.where(logits == m1, lane, LANES), axis=-1, keepdims=True)
    rest = jnp.where(lane == e1, neg, logits)
    m2 = jnp.max(rest, axis=-1, keepdims=True)
    e2 = jnp.min(jnp.where(rest == m2, lane, LANES), axis=-1, keepdims=True)
    ex = jnp.exp(m2 - m1)
    w1 = 1.0 / (1.0 + ex)
    w2 = ex / (1.0 + ex)

    hot1 = lane == e1
    hot2 = lane == e2
    onehot = jnp.where(hot1 | hot2, 1.0, 0.0)
    row, col = _tri_masks(tm)
    before = (row > col).astype(BF16)
    prefix = jnp.dot(before, onehot.astype(BF16), preferred_element_type=F32) + carry_ref[...]
    rank1 = jnp.sum(jnp.where(hot1, prefix, 0.0), axis=-1, keepdims=True).astype(I32)
    rank2 = jnp.sum(jnp.where(hot2, prefix, 0.0), axis=-1, keepdims=True).astype(I32)
    carry_ref[...] += jnp.sum(onehot, axis=0, keepdims=True)
    cnt_ref[...] = carry_ref[...]

    meta = jnp.where(lane == 0, e1, jnp.where(lane == 1, e2, jnp.where(lane == 2, rank1, rank2)))
    meta_ref[...] = jnp.where(lane < 4, meta, 0)
    wts_ref[...] = jnp.where(lane == 0, w1, jnp.where(lane == 1, w2, 0.0))


def _router(x, wr_pad):
    t = x.shape[0]
    tm = min(512, t)
    return pl.pallas_call(
        _router_kernel,
        out_shape=[jax.ShapeDtypeStruct((t, LANES), I32), jax.ShapeDtypeStruct((t, LANES), F32),
                   jax.ShapeDtypeStruct((1, LANES), F32)],
        grid=(t // tm,),
        in_specs=[pl.BlockSpec((tm, D_MODEL), lambda i: (i, 0)),
                  pl.BlockSpec((D_MODEL, LANES), lambda i: (0, 0))],
        out_specs=[pl.BlockSpec((tm, LANES), lambda i: (i, 0)), pl.BlockSpec((tm, LANES), lambda i: (i, 0)),
                   pl.BlockSpec((1, LANES), lambda i: (0, 0))],
        scratch_shapes=[pltpu.VMEM((1, LANES), F32)],
        compiler_params=_cparams("arbitrary"),
    )(x, wr_pad)


def _dispatch_kernel(pad_start_ref, pad_len_ref, nused_ref, dest_ref, x_ref, xs_ref, zeros_ref, sem, fill_sem):
    tm = x_ref.shape[0]
    n_blocks = xs_ref.shape[0] // MOE_ROWS

    @pl.when(pl.program_id(0) == pl.num_programs(0) - 1)
    def _():
        zeros_ref[...] = jnp.zeros_like(zeros_ref)
        pad_copy = lambda row: pltpu.make_async_copy(zeros_ref.at[pl.ds(0, 1)], xs_ref.at[pl.ds(row, 1)], fill_sem)
        blk_copy = lambda blk: pltpu.make_async_copy(
            zeros_ref, xs_ref.at[pl.ds(pl.multiple_of(blk * MOE_ROWS, MOE_ROWS), MOE_ROWS)], fill_sem)

        def for_each(fn):
            for e in range(N_EXPERTS):
                base = pad_start_ref[e]
                lax.fori_loop(0, pad_len_ref[e], lambda r, c: (fn(pad_copy(base + r)), c)[1], 0)
            lax.fori_loop(nused_ref[0], n_blocks, lambda b, c: (fn(blk_copy(b)), c)[1], 0)

        for_each(lambda cp: cp.start())
        for_each(lambda cp: cp.wait())

    def row_copy(r, j):
        return pltpu.make_async_copy(x_ref.at[pl.ds(r, 1)], xs_ref.at[pl.ds(dest_ref[TOP_K * r + j], 1)], sem)

    def start(r, c):
        for j in range(TOP_K):
            row_copy(r, j).start(priority=j)
        return c

    lax.fori_loop(0, tm, start, 0, unroll=8)
    for j in range(TOP_K):
        pltpu.make_async_copy(x_ref, xs_ref.at[pl.ds(0, tm)], sem).wait()


def _dispatch(x, dest_flat, pad_start, pad_len, n_used, n_slots):
    t = x.shape[0]
    tm = min(256, t)
    grid_spec = pltpu.PrefetchScalarGridSpec(
        num_scalar_prefetch=3,
        grid=(t // tm,),
        in_specs=[pl.BlockSpec((tm * TOP_K,), lambda i, *_: (i,), memory_space=pltpu.SMEM),
                  pl.BlockSpec((tm, D_MODEL), lambda i, *_: (i, 0))],
        out_specs=pl.BlockSpec(memory_space=pl.ANY),
        scratch_shapes=[pltpu.VMEM((MOE_ROWS, D_MODEL), F32), pltpu.SemaphoreType.DMA(()),
                        pltpu.SemaphoreType.DMA(())],
    )
    return pl.pallas_call(
        _dispatch_kernel,
        out_shape=jax.ShapeDtypeStruct((n_slots, D_MODEL), F32),
        grid_spec=grid_spec,
        compiler_params=_cparams("arbitrary"),
    )(pad_start, pad_len, n_used, dest_flat, x)


def _expert_kernel(blk_e_ref, nused_ref, xs_ref, wg_ref, wu_ref, wd_ref, ys_ref, acc_ref):
    del blk_e_ref
    i = pl.program_id(0)
    f = pl.program_id(1)
    last = pl.num_programs(1) - 1
    used = i < nused_ref[0]

    @pl.when(used & (f == 0))
    def _():
        acc_ref[...] = jnp.zeros_like(acc_ref)

    @pl.when(used)
    def _():
        xb = xs_ref[...].astype(BF16)
        g = jnp.dot(xb, wg_ref[0], preferred_element_type=F32)
        u = jnp.dot(xb, wu_ref[0], preferred_element_type=F32)
        acc_ref[...] += _bdot(g * _sigmoid(g) * u, wd_ref[0])

    @pl.when(used & (f == last))
    def _():
        ys_ref[...] = acc_ref[...]

    @pl.when(jnp.logical_not(used) & (f == last))
    def _():
        ys_ref[...] = jnp.zeros_like(ys_ref)


def _experts(xs, blk_e, n_used, wg, wu, wd):
    n_slots = xs.shape[0]
    n_blocks = n_slots // MOE_ROWS
    ff = wg.shape[2]
    tf = ff // 2
    grid_spec = pltpu.PrefetchScalarGridSpec(
        num_scalar_prefetch=2,
        grid=(n_blocks, ff // tf),
        in_specs=[pl.BlockSpec((MOE_ROWS, D_MODEL), lambda i, f, be, nu: (i, 0)),
                  pl.BlockSpec((1, D_MODEL, tf), lambda i, f, be, nu: (be[i], 0, f)),
                  pl.BlockSpec((1, D_MODEL, tf), lambda i, f, be, nu: (be[i], 0, f)),
                  pl.BlockSpec((1, tf, D_MODEL), lambda i, f, be, nu: (be[i], f, 0))],
        out_specs=pl.BlockSpec((MOE_ROWS, D_MODEL), lambda i, f, be, nu: (i, 0)),
        scratch_shapes=[pltpu.VMEM((MOE_ROWS, D_MODEL), F32)],
    )
    return pl.pallas_call(
        _expert_kernel,
        out_shape=jax.ShapeDtypeStruct((n_slots, D_MODEL), F32),
        grid_spec=grid_spec,
        compiler_params=_cparams("arbitrary", "arbitrary"),
    )(blk_e, n_used, xs, wg, wu, wd)


def _combine_kernel(dest_ref, dest_next_ref, x_ref, wts_ref, ys_ref, lg_ref, lb_ref, xo_ref, buf_ref, sem):
    tm = x_ref.shape[0]
    i = pl.program_id(0)
    n = pl.num_programs(0)

    def gather(idx_ref, slot):
        def start(r, c):
            for j in range(TOP_K):
                pltpu.make_async_copy(ys_ref.at[pl.ds(idx_ref[TOP_K * r + j], 1)],
                                      buf_ref.at[slot, j, pl.ds(r, 1)], sem.at[slot]).start(priority=j)
            return c
        lax.fori_loop(0, tm, start, 0, unroll=8)

    slot = i & 1

    @pl.when(i == 0)
    def _():
        gather(dest_ref, 0)

    @pl.when(i + 1 < n)
    def _():
        gather(dest_next_ref, 1 - slot)

    for j in range(TOP_K):
        pltpu.make_async_copy(ys_ref.at[pl.ds(0, tm)], buf_ref.at[slot, j], sem.at[slot]).wait()
    w = wts_ref[...]
    f = w[:, 0:1] * buf_ref[slot, 0] + w[:, 1:2] * buf_ref[slot, 1]
    z = DEEPNORM_ALPHA * x_ref[...] + f
    xo_ref[...] = _layer_norm_rows(z, lg_ref[...], lb_ref[...], LN_EPS)


def _combine(x, dest_flat, wts, ys, ln_g, ln_b):
    t = x.shape[0]
    tm = min(256, t)
    n_tiles = t // tm
    rows = lambda width: pl.BlockSpec((tm, width), lambda i: (i, 0))
    vec = pl.BlockSpec((1, D_MODEL), lambda i: (0, 0))
    return pl.pallas_call(
        _combine_kernel,
        out_shape=jax.ShapeDtypeStruct((t, D_MODEL), F32),
        grid=(n_tiles,),
        in_specs=[pl.BlockSpec((tm * TOP_K,), lambda i: (i,), memory_space=pltpu.SMEM),
                  pl.BlockSpec((tm * TOP_K,), lambda i: (jnp.minimum(i + 1, n_tiles - 1),),
                               memory_space=pltpu.SMEM),
                  rows(D_MODEL), rows(LANES), pl.BlockSpec(memory_space=pl.ANY), vec, vec],
        out_specs=rows(D_MODEL),
        scratch_shapes=[pltpu.VMEM((2, TOP_K, tm, D_MODEL), F32), pltpu.SemaphoreType.DMA((2,))],
        compiler_params=_cparams("arbitrary"),
    )(dest_flat, dest_flat, x, wts, ys, ln_g, ln_b)


def _moe_ffn(x, w_router, wg, wu, wd, ln_g, ln_b):
    t = x.shape[0]
    wr_pad = jnp.pad(w_router, ((0, 0), (0, LANES - N_EXPERTS)))
    meta, wts, cnt = _router(x, wr_pad)
    counts = cnt[0, :N_EXPERTS].astype(I32)
    padded = (counts + MOE_ROWS - 1) // MOE_ROWS * MOE_ROWS
    seg_end = jnp.cumsum(padded)
    seg_start = seg_end - padded
    n_blocks = (t * TOP_K) // MOE_ROWS + N_EXPERTS
    dest = seg_start[meta[:, 0:TOP_K]] + meta[:, TOP_K:2 * TOP_K]
    dest_flat = dest.reshape(-1).astype(I32)
    blk_start = jnp.arange(n_blocks, dtype=I32) * MOE_ROWS
    blk_e = jnp.minimum(jnp.sum(blk_start[:, None] >= seg_end[None, :], axis=1), N_EXPERTS - 1).astype(I32)
    n_used = (seg_end[-1:] // MOE_ROWS).astype(I32)
    xs = _dispatch(x, dest_flat, (seg_start + counts).astype(I32), (padded - counts).astype(I32), n_used,
                   n_blocks * MOE_ROWS)
    ys = _experts(xs, blk_e, n_used, wg, wu, wd)
    return _combine(x, dest_flat, wts, ys, ln_g, ln_b)


def _pad_rows(w, rows, at=0):
    out = jnp.zeros((rows, w.shape[1]), w.dtype)
    return out.at[at:at + w.shape[0]].set(w)


def _reorder_in_projection(w, b):
    sizes = (GLA_KW, GLA_KW, GLA_VW, GLA_VW, GLA_DECAY_RANK, LRU_WIDTH, LRU_WIDTH,
             3 * RWKV_WIDTH, RWKV_DECAY_RANK, RWKV_A_RANK, RWKV_GATE_RANK, N_BRANCH * D_MODEL)
    offs = [0]
    for s in sizes:
        offs.append(offs[-1] + s)
    wb = jnp.concatenate([w, b[None, :]], axis=0)
    piece = lambda i: wb[:, offs[i]:offs[i + 1]]
    zeros = lambda n: jnp.zeros((wb.shape[0], n), wb.dtype)
    q, k, v, r, dec, lx, lg, rkv, wl, al, gl, gates = (piece(i) for i in range(len(sizes)))
    small = jnp.concatenate([wl, al, gl, zeros(RW_SMALL_W - SM_GL[0] - RWKV_GATE_RANK)], axis=1)
    groups = {"gla": jnp.concatenate([q, k, v, r, dec, zeros(GLA_DEC_W - GLA_DECAY_RANK)], axis=1),
              "lru": jnp.concatenate([lx, lg], axis=1),
              "rwkv": jnp.concatenate([rkv, small], axis=1),
              "gates": gates}
    return {name: (g[:-1].astype(BF16), g[-1:]) for name, g in groups.items()}


def _cast_kernel(x_ref, o_ref):
    o_ref[...] = x_ref[...].astype(o_ref.dtype)


CAST_TILE_ELEMS = 2 * 1024 * 1024


def _to_bf16(w):
    cols = w.shape[-1]
    w2 = w.reshape(-1, cols)
    n_rows = w2.shape[0]
    tr = n_rows
    while tr * cols > CAST_TILE_ELEMS and tr % 32 == 0:
        tr //= 2
    out = pl.pallas_call(
        _cast_kernel,
        out_shape=jax.ShapeDtypeStruct(w2.shape, BF16),
        grid=(n_rows // tr,),
        in_specs=[pl.BlockSpec((tr, cols), lambda i: (i, 0))],
        out_specs=pl.BlockSpec((tr, cols), lambda i: (i, 0)),
        compiler_params=_cparams("parallel"),
    )(w2)
    return out.reshape(w.shape)


def _block_diag(blocks):
    n, bi, bo = blocks.shape
    eye = jnp.eye(n, dtype=blocks.dtype)
    return (eye[:, None, :, None] * blocks[:, :, None, :]).reshape(n * bi, n * bo)


def kernel(x, w_in, b_in, gla_w_decay_up, gla_b_decay, gla_norm_g, gla_norm_b, lru_conv_w, lru_conv_b, lru_w_r, lru_b_r, lru_w_i, lru_b_i, lru_lambda, rwkv_mu, rwkv_w0, rwkv_w2, rwkv_a0, rwkv_a2, rwkv_g2, rwkv_k_k, rwkv_k_a, rwkv_r_k, rwkv_lnx_g, rwkv_lnx_b, p_gla, p_lru, p_rwkv, w_out, ln_mix_g, ln_mix_b, ffn_w_gate, ffn_w_up, ffn_w_down, moe_w_router, moe_w_gate, moe_w_up, moe_w_down, ln_ffn_g, ln_ffn_b):
    batch, seq, d = x.shape
    t = batch * seq
    xf = x.reshape(t, d)
    row = lambda a: a.reshape(1, -1)
    head_ones = _block_diag(jnp.ones((RWKV_HEADS, RWKV_HEAD, RWKV_HEAD), BF16))
    for l in range(DEPTH):
        proj = _reorder_in_projection(w_in[l], b_in[l])
        w_gates, b_gates = proj["gates"]

        wup = _pad_rows(gla_w_decay_up[l], GLA_DEC_W).astype(BF16)
        o_gla = _gla_branch(xf, batch, seq, *proj["gla"], wup, row(gla_b_decay[l]), row(gla_norm_g[l]),
                            row(gla_norm_b[l]))

        w_ri = jnp.concatenate([_block_diag(lru_w_r[l]), _block_diag(lru_w_i[l])], axis=1).astype(BF16)
        b_ri = jnp.concatenate([lru_b_r[l], lru_b_i[l]])[None, :]
        o_lru = _lru_branch(xf, batch, seq, *proj["lru"], lru_conv_w[l], row(lru_conv_b[l]), w_ri, b_ri,
                            row(jax.nn.softplus(-lru_lambda[l])))

        mu = rwkv_mu[l]
        mu_all = jnp.zeros((3 * RWKV_WIDTH + RW_SMALL_W,), F32).at[0:mu.shape[0]].set(mu)
        prep_params = (*proj["rwkv"], row(mu_all),
                       _pad_rows(rwkv_w2[l], SM_WA[1], 0).astype(BF16),
                       _pad_rows(rwkv_a2[l], SM_WA[1], RWKV_DECAY_RANK).astype(BF16),
                       _pad_rows(rwkv_g2[l], SM_GL[1], 0).astype(BF16),
                       row(rwkv_w0[l]), row(rwkv_a0[l]), row(rwkv_k_k[l]), row(rwkv_k_a[l]),
                       row(rwkv_r_k[l]), head_ones)
        prep = _rwkv_prep(xf, batch, seq, prep_params)
        o_rwkv = _rwkv_scan(prep, batch, seq, row(rwkv_lnx_g[l]), row(rwkv_lnx_b[l]))

        xf = _merge(o_gla, o_lru, o_rwkv, xf, w_gates, b_gates, p_gla[l].astype(BF16), p_lru[l].astype(BF16),
                    p_rwkv[l].astype(BF16), w_out[l].astype(BF16), row(ln_mix_g[l]), row(ln_mix_b[l]))
        i = l // 2
        if l % 2 == 0:
            xf = _dense_ffn(xf, ffn_w_gate[i].astype(BF16), ffn_w_up[i].astype(BF16),
                            ffn_w_down[i].astype(BF16), row(ln_ffn_g[l]), row(ln_ffn_b[l]))
        else:
            xf = _moe_ffn(xf, moe_w_router[i], _to_bf16(moe_w_gate[i]), _to_bf16(moe_w_up[i]),
                          _to_bf16(moe_w_down[i]), row(ln_ffn_g[l]), row(ln_ffn_b[l]))
    return xf.reshape(batch, seq, d)
```

```python
import functools

import jax
import jax.numpy as jnp
from jax import lax
from jax.experimental import pallas as pl
from jax.experimental.pallas import tpu as pltpu

F32 = jnp.float32
BF16 = jnp.bfloat16
I32 = jnp.int32
HIGHEST = lax.Precision.HIGHEST

D_MODEL = 1024
DEPTH = 2
GLA_HEADS, GLA_DK, GLA_DV = 4, 64, 128
GLA_KW, GLA_VW = GLA_HEADS * GLA_DK, GLA_HEADS * GLA_DV
GLA_DECAY_RANK = 16
GLA_TEMP = 16.0
LRU_WIDTH, LRU_BLOCKS, LRU_CONV, LRU_C = 512, 8, 4, 8.0
RWKV_HEAD, RWKV_WIDTH = 64, 512
RWKV_HEADS = RWKV_WIDTH // RWKV_HEAD
RWKV_DECAY_RANK, RWKV_A_RANK, RWKV_GATE_RANK = 64, 64, 160
RWKV_LNX_EPS = 64e-5
N_BRANCH = 3
N_EXPERTS, TOP_K = 8, 2
DEEPNORM_ALPHA = (2 * DEPTH) ** 0.25
LN_EPS = 1e-5

LANES = 128
SUBLANES = 8
VMEM_LIMIT_BYTES = 56 * 1024 * 1024

SM_WA = (0, 128)
SM_GL = (128, 256)
RW_SMALL_W = 512
GLA_DEC_W = LANES
GLA_COLS = {"q": (0, 256), "k": (256, 512), "v": (512, 1024), "r": (1024, 1536), "dec": (1536, 1664)}

CHUNK = 64
INV_BLOCK = 16
MOE_ROWS = 512


def _cparams(*sem):
    return pltpu.CompilerParams(dimension_semantics=sem, vmem_limit_bytes=VMEM_LIMIT_BYTES)


def _sigmoid(x):
    return 1.0 / (1.0 + jnp.exp(-x))


def _softplus(x):
    return jnp.maximum(x, 0.0) + jnp.log(1.0 + jnp.exp(-jnp.abs(x)))


def _bdot(a, b):
    return jnp.dot(a.astype(BF16), b.astype(BF16), preferred_element_type=F32)


def _bdot_nt(a, b):
    return lax.dot_general(a.astype(BF16), b.astype(BF16), (((1,), (1,)), ((), ())),
                           preferred_element_type=F32)


def _bdot_tn(a, b):
    return lax.dot_general(a.astype(BF16), b.astype(BF16), (((0,), (0,)), ((), ())),
                           preferred_element_type=F32)


def _fdot(a, b):
    ah = a.astype(BF16)
    al = (a - ah.astype(F32)).astype(BF16)
    bh = b.astype(BF16)
    bl = (b - bh.astype(F32)).astype(BF16)
    dot = lambda u, v: jnp.dot(u, v, preferred_element_type=F32)
    return dot(al, bh) + dot(ah, bl) + dot(ah, bh)


def _split3(x):
    hi = x.astype(BF16)
    rest = x - hi.astype(F32)
    mid = rest.astype(BF16)
    lo = (rest - mid.astype(F32)).astype(BF16)
    return hi, mid, lo


def _mask_dot(mask_bf16, x):
    hi, mid, lo = _split3(x)
    dot = lambda part: jnp.dot(mask_bf16, part, preferred_element_type=F32)
    return dot(lo) + dot(mid) + dot(hi)


def _dot_mask(x, mask_bf16):
    hi, mid, lo = _split3(x)
    dot = lambda part: jnp.dot(part, mask_bf16, preferred_element_type=F32)
    return dot(lo) + dot(mid) + dot(hi)


def _layer_norm_rows(z, g, b, eps):
    mu = jnp.mean(z, axis=-1, keepdims=True)
    zc = z - mu
    var = jnp.mean(zc * zc, axis=-1, keepdims=True)
    return zc * lax.rsqrt(var + eps) * g + b


def _tri_masks(n):
    row = lax.broadcasted_iota(I32, (n, n), 0)
    col = lax.broadcasted_iota(I32, (n, n), 1)
    return row, col


MXU_COLS = 256


def _gla_kernel(*refs, chunks, blocks_per_seq):
    *io_refs, st_ref, ha_ref, hb_ref = refs
    s = pl.program_id(0)

    @pl.when(s == 0)
    def _():
        hb_ref[...] = jnp.zeros_like(hb_ref)

    @pl.when((s == 0) | (((s + blocks_per_seq - 1) % blocks_per_seq) == 0))
    def _():
        st_ref[...] = jnp.zeros_like(st_ref)

    body = functools.partial(_gla_body, chunks, (*io_refs, st_ref))

    @pl.when((s & 1) == 0)
    def _():
        body(ha_ref, hb_ref)

    @pl.when((s & 1) == 1)
    def _():
        body(hb_ref, ha_ref)


def _gla_body(chunks, refs, h_next, h_ref):
    x_ref, w_ref, b_ref, wup_ref, bdec_ref, ng_ref, nb_ref, o_ref, st_ref = refs
    xb = x_ref[...].astype(BF16)
    width = w_ref.shape[1]
    pieces = iter([slice(c0, min(c0 + MXU_COLS, width)) for c0 in range(0, width, MXU_COLS)])

    def project_piece():
        cols = next(pieces, None)
        if cols is not None:
            h_next[:, cols] = jnp.dot(xb, w_ref[:, cols], preferred_element_type=F32) + b_ref[:, cols]

    q_ref = h_ref.at[:, GLA_COLS["q"][0]:GLA_COLS["q"][1]]
    k_ref = h_ref.at[:, GLA_COLS["k"][0]:GLA_COLS["k"][1]]
    v_ref = h_ref.at[:, GLA_COLS["v"][0]:GLA_COLS["v"][1]]
    r_ref = h_ref.at[:, GLA_COLS["r"][0]:GLA_COLS["r"][1]]
    dec_ref = h_ref.at[:, GLA_COLS["dec"][0]:GLA_COLS["dec"][1]]

    row, col = _tri_masks(CHUNK)
    ltri = (row >= col).astype(BF16)
    kw, vw = 2 * GLA_DK, 2 * GLA_DV
    key_shift, val_shift = GLA_DK.bit_length() - 1, GLA_DV.bit_length() - 1
    iota = lambda shape, axis: lax.broadcasted_iota(I32, shape, axis)
    causal = iota((CHUNK, kw), 0) >= (iota((CHUNK, kw), 1) & (GLA_DK - 1))
    key_diag = (iota((kw, kw), 0) >> key_shift) == (iota((kw, kw), 1) >> key_shift)
    val_diag = (iota((kw, vw), 0) >> key_shift) == (iota((kw, vw), 1) >> val_shift)
    state_diag = (iota((vw, kw), 0) >> val_shift) == (iota((vw, kw), 1) >> key_shift)
    zero = jnp.zeros((), BF16)
    pairs = range(GLA_HEADS // 2)
    qp, vp, k_diag, kep, decays = [], [], [], [], []
    for c in range(chunks):
        project_piece()
        rows = pl.ds(c * CHUNK, CHUNK)
        logits = _bdot(dec_ref[rows, :], wup_ref[...]) + bdec_ref[...]
        log_a = -_softplus(-logits) * (1.0 / GLA_TEMP)
        cum = _mask_dot(ltri, log_a)
        cum_end = cum[CHUNK - 1:CHUNK, :]
        q = q_ref[rows, :] * (GLA_DK ** -0.5)
        k = k_ref[rows, :]
        q_dec = (q * jnp.exp(cum)).astype(BF16)
        k_inv = (k * jnp.exp(-cum)).astype(BF16)
        k_end = (k * jnp.exp(cum_end - cum)).astype(BF16)
        decay_end = jnp.exp(cum_end)
        v = v_ref[rows, :].astype(BF16)
        for p in pairs:
            s = slice(p * kw, (p + 1) * kw)
            qp.append(q_dec[:, s])
            vp.append(v[:, p * vw:(p + 1) * vw])
            k_diag.append(jnp.where(key_diag, jnp.concatenate([k_inv[:, s], k_inv[:, s]], axis=0), zero))
            kep.append(k_end[:, s])
            decays.append(decay_end[:, s])

    for _ in range(chunks):
        project_piece()

    v_diag = [jnp.where(val_diag, jnp.concatenate([x, x], axis=0), zero) for x in vp]
    scores = [jnp.where(causal, _bdot_nt(q, kd), 0.0) for q, kd in zip(qp, k_diag)]
    o_local = [_bdot(s, vd) for s, vd in zip(scores, v_diag)]
    upd = [jnp.where(state_diag, _bdot_tn(x, ke), 0.0) for x, ke in zip(vp, kep)]

    state = [st_ref[p] for p in pairs]
    n_pairs = len(pairs)
    for c in range(chunks):
        rows = pl.ds(c * CHUNK, CHUNK)
        item = slice(c * n_pairs, (c + 1) * n_pairs)
        o = [ol + _bdot_nt(q, st) for ol, q, st in zip(o_local[item], qp[item], state)]
        state = [st * dcy + u for st, dcy, u in zip(state, decays[item], upd[item])]
        outs = []
        for op in o:
            for oh in (op[:, :GLA_DV], op[:, GLA_DV:]):
                mu = jnp.mean(oh, axis=-1, keepdims=True)
                oc = oh - mu
                var = jnp.mean(oc * oc, axis=-1, keepdims=True)
                outs.append(oc * lax.rsqrt(var + LN_EPS))
        y = jnp.concatenate(outs, axis=1) * ng_ref[...] + nb_ref[...]
        r = r_ref[rows, :]
        o_ref[rows, :] = (y * (r * _sigmoid(r))).astype(o_ref.dtype)
    for p in pairs:
        st_ref[p] = state[p]


def _gla_branch(x, batch, seq, w, bias, wup_pad, b_decay, norm_g, norm_b):
    t = batch * seq
    chunks = 8
    rows = CHUNK * chunks
    nblk = seq // rows

    n_tiles = batch * nblk

    def full(a):
        return pl.BlockSpec(a.shape, lambda s: (0,) * a.ndim)

    args = (w, bias, wup_pad, b_decay, norm_g, norm_b)
    return pl.pallas_call(
        functools.partial(_gla_kernel, chunks=chunks, blocks_per_seq=nblk),
        out_shape=jax.ShapeDtypeStruct((t, GLA_VW), BF16),
        grid=(n_tiles + 1,),
        in_specs=[pl.BlockSpec((rows, D_MODEL), lambda s: (jnp.minimum(s, n_tiles - 1), 0))]
        + [full(a) for a in args],
        out_specs=pl.BlockSpec((rows, GLA_VW), lambda s: (jnp.maximum(s - 1, 0), 0)),
        scratch_shapes=[pltpu.VMEM((GLA_HEADS // 2, 2 * GLA_DV, 2 * GLA_DK), F32),
                        pltpu.VMEM((rows, w.shape[1]), F32), pltpu.VMEM((rows, w.shape[1]), F32)],
        compiler_params=_cparams("arbitrary"),
    )(x, *args)


def _lru_kernel(*refs, blocks_per_seq):
    *io_refs, xbuf_ref, a_ref, u_ref, hs_ref, gate_ref, h_ref, pa_ref, pb_ref = refs
    s = pl.program_id(0)

    @pl.when(s == 0)
    def _():
        pb_ref[...] = jnp.zeros_like(pb_ref)

    @pl.when((s == 0) | (((s + blocks_per_seq - 1) % blocks_per_seq) == 0))
    def _():
        xbuf_ref[0:SUBLANES, :] = jnp.zeros((SUBLANES, LRU_WIDTH), F32)
        h_ref[...] = jnp.zeros_like(h_ref)

    body = functools.partial(_lru_body, (*io_refs, xbuf_ref, a_ref, u_ref, hs_ref, gate_ref, h_ref))

    @pl.when((s & 1) == 0)
    def _():
        body(pa_ref, pb_ref)

    @pl.when((s & 1) == 1)
    def _():
        body(pb_ref, pa_ref)


def _lru_body(refs, p_next, p_cur):
    (xin_ref, w_ref, b_ref, cw_ref, cb_ref, wri_ref, bri_ref, sp_ref,
     o_ref, xbuf_ref, a_ref, u_ref, hs_ref, gate_ref, h_ref) = refs
    tm = xin_ref.shape[0]
    pad = SUBLANES
    xb = xin_ref[...].astype(BF16)
    width = w_ref.shape[1]
    pieces = iter([slice(c0, min(c0 + MXU_COLS, width)) for c0 in range(0, width, MXU_COLS)])

    def project_piece():
        cols = next(pieces, None)
        if cols is not None:
            p_next[:, cols] = jnp.dot(xb, w_ref[:, cols], preferred_element_type=F32) + b_ref[:, cols]

    project_piece()
    x = p_cur[:, :LRU_WIDTH]
    xbuf_ref[pad:pad + tm, :] = x
    xc = cb_ref[...] + x * cw_ref[LRU_CONV - 1:LRU_CONV, :]
    for j in range(LRU_CONV - 1):
        back = LRU_CONV - 1 - j
        xc = xc + xbuf_ref[pl.ds(pad - back, tm), :] * cw_ref[j:j + 1, :]
    xbuf_ref[0:pad, :] = x[tm - pad:tm, :]

    project_piece()
    ri = _sigmoid(_bdot(xc, wri_ref[...]) + bri_ref[...])
    rg, ig = ri[:, :LRU_WIDTH], ri[:, LRU_WIDTH:]
    project_piece()
    log_a = -LRU_C * rg * sp_ref[...]
    a = jnp.exp(log_a)
    a_ref[...] = a
    u_ref[...] = jnp.sqrt(1.0 - a * a) * (ig * xc)
    project_piece()
    gate_ref[...] = jax.nn.gelu(p_cur[:, LRU_WIDTH:])
    for _ in range(width // MXU_COLS):
        project_piece()

    def step(t, hprev):
        hnew = a_ref[pl.ds(t, 1), :] * hprev + u_ref[pl.ds(t, 1), :]
        hs_ref[pl.ds(t, 1), :] = hnew
        return hnew

    h_ref[...] = lax.fori_loop(0, tm, step, h_ref[...], unroll=8)
    o_ref[...] = (gate_ref[...] * hs_ref[...]).astype(o_ref.dtype)


def _lru_branch(x, batch, seq, w, bias, conv_w, conv_b, w_ri, b_ri, softplus_neg_lam):
    t = batch * seq
    tm = min(512, seq)
    nblk = seq // tm

    n_tiles = batch * nblk

    def full(a):
        return pl.BlockSpec(a.shape, lambda s: (0,) * a.ndim)

    args = (w, bias, conv_w, conv_b, w_ri, b_ri, softplus_neg_lam)
    return pl.pallas_call(
        functools.partial(_lru_kernel, blocks_per_seq=nblk),
        out_shape=jax.ShapeDtypeStruct((t, LRU_WIDTH), BF16),
        grid=(n_tiles + 1,),
        in_specs=[pl.BlockSpec((tm, D_MODEL), lambda s: (jnp.minimum(s, n_tiles - 1), 0))]
        + [full(a) for a in args],
        out_specs=pl.BlockSpec((tm, LRU_WIDTH), lambda s: (jnp.maximum(s - 1, 0), 0)),
        scratch_shapes=[pltpu.VMEM((tm + SUBLANES, LRU_WIDTH), F32)]
        + [pltpu.VMEM((tm, LRU_WIDTH), F32)] * 4 + [pltpu.VMEM((1, LRU_WIDTH), F32)]
        + [pltpu.VMEM((tm, w.shape[1]), F32)] * 2,
        compiler_params=_cparams("arbitrary"),
    )(x, *args)


def _rwkv_prep_kernel(*refs, blocks_per_seq):
    *io_refs, ha_ref, hb_ref, carry_ref = refs
    tm = io_refs[0].shape[0]
    s = pl.program_id(0)
    body = functools.partial(_rwkv_prep_body, tm, s, RWKV_WIDTH, blocks_per_seq, (*io_refs, carry_ref))

    @pl.when(s == 0)
    def _():
        hb_ref[...] = jnp.zeros_like(hb_ref)
        carry_ref[...] = jnp.zeros_like(carry_ref)

    @pl.when((s & 1) == 0)
    def _():
        body(ha_ref, hb_ref)

    @pl.when((s & 1) == 1)
    def _():
        body(hb_ref, ha_ref)


def _rwkv_prep_body(tm, s, gw, blocks_per_seq, refs, h_next, h_cur):
    (x_ref, w_ref, b_ref, mu_ref, w2_ref, a2_ref, g2_ref, w0_ref, a0_ref, kk_ref, ka_ref, rk_ref, bd_ref,
     r_out, k_out, v_out, lw_out, kk_out, kka_out, bonus_out, g_out, carry_ref) = refs
    first = ((s + blocks_per_seq - 1) % blocks_per_seq) == 0
    keep = jnp.where(first, 0.0, 1.0)
    row0 = lax.broadcasted_iota(I32, (tm, 1), 0) == 0
    xb = x_ref[...].astype(BF16)

    pw = gw // 2

    def project(j):
        cols = slice(j * pw, (j + 1) * pw)
        h_next[:, cols] = jnp.dot(xb, w_ref[:, cols], preferred_element_type=F32) + b_ref[:, cols]

    def shifted(j):
        cols = slice(j * gw, (j + 1) * gw)
        cur = h_cur[:, cols]
        prev_row = carry_ref[0:1, cols] * keep
        carry_ref[0:1, cols] = cur[tm - 1:tm, :]
        prev = jnp.where(row0, prev_row, pltpu.roll(cur, 1, axis=0))
        return cur + (prev - cur) * mu_ref[:, cols]

    project(0)
    sm = shifted(3)
    wa = sm[:, SM_WA[0]:SM_WA[0] + SM_WA[1]]
    gl = sm[:, SM_GL[0]:SM_GL[0] + SM_GL[1]]
    project(1)
    w_log = -_softplus(-(w0_ref[...] + _bdot(jnp.tanh(wa), w2_ref[...]))) - 0.5
    lw_out[...] = -jnp.exp(w_log)
    project(2)
    a = _sigmoid(a0_ref[...] + _bdot(wa, a2_ref[...]))
    g_out[...] = _bdot(_sigmoid(gl), g2_ref[...])
    project(3)
    k = shifted(1)
    kk = k * kk_ref[...]
    norm = jnp.sqrt(_dot_mask(kk * kk, bd_ref[...]))
    project(4)
    kk = kk / jnp.maximum(norm, 1e-12)
    kk_out[...] = kk
    kka_out[...] = kk * a
    k2 = k * (1.0 + (a - 1.0) * ka_ref[...])
    k_out[...] = k2
    project(5)
    r = shifted(0)
    r_out[...] = r
    project(6)
    v = shifted(2)
    v_out[...] = v
    project(7)
    bonus_out[...] = _dot_mask(r * k2 * rk_ref[...], bd_ref[...]) * v


def _rwkv_prep(x, batch, seq, params):
    t = batch * seq
    tm = min(512, seq)
    bps = seq // tm
    n_tiles = t // tm
    width = params[0].shape[1]

    def full(a):
        return pl.BlockSpec(a.shape, lambda s: (0,) * a.ndim)

    out = jax.ShapeDtypeStruct((t, RWKV_WIDTH), F32)
    ospec = pl.BlockSpec((tm, RWKV_WIDTH), lambda s: (jnp.maximum(s - 1, 0), 0))
    return pl.pallas_call(
        functools.partial(_rwkv_prep_kernel, blocks_per_seq=bps),
        out_shape=[out] * 8,
        grid=(n_tiles + 1,),
        in_specs=[pl.BlockSpec((tm, D_MODEL), lambda s: (jnp.minimum(s, n_tiles - 1), 0))]
        + [full(a) for a in params],
        out_specs=[ospec] * 8,
        scratch_shapes=[pltpu.VMEM((tm, width), F32), pltpu.VMEM((tm, width), F32),
                        pltpu.VMEM((SUBLANES, width), F32)],
        compiler_params=_cparams("arbitrary"),
    )(x, *params)


PAIR_W = 2 * RWKV_HEAD
N_PAIRS = RWKV_HEADS // 2
HEAD_SHIFT = RWKV_HEAD.bit_length() - 1


def _pair_masks():
    t = lax.broadcasted_iota(I32, (CHUNK, PAIR_W), 0)
    lane = lax.broadcasted_iota(I32, (CHUNK, PAIR_W), 1)
    j = lane & (RWKV_HEAD - 1)
    r = lax.broadcasted_iota(I32, (PAIR_W, PAIR_W), 0)
    c = lax.broadcasted_iota(I32, (PAIR_W, PAIR_W), 1)
    r4 = lax.broadcasted_iota(I32, (2 * PAIR_W, PAIR_W), 0)
    c4 = lax.broadcasted_iota(I32, (2 * PAIR_W, PAIR_W), 1)
    return dict(
        strict=t > j, incl=t >= j, eye=(t == j).astype(F32),
        same_block=jnp.bitwise_xor(t, j) < INV_BLOCK,
        first_head=lane < RWKV_HEAD,
        block_diag=(r >> HEAD_SHIFT) == (c >> HEAD_SHIFT),
        stacked_diag=((r4 >> HEAD_SHIFT) & 1) == (c4 >> HEAD_SHIFT))


def _rwkv_scan_kernel(r_ref, k_ref, v_ref, lw_ref, kk_ref, kka_ref, bonus_ref, g_ref, lg_ref, lb_ref,
                      o_ref, st_ref, *, chunks):
    @pl.when(pl.program_id(1) == 0)
    def _():
        st_ref[...] = jnp.zeros_like(st_ref)

    row, col = _tri_masks(CHUNK)
    ltri = (row >= col).astype(BF16)
    m = _pair_masks()
    zero = jnp.zeros((), BF16)
    pairs = range(N_PAIRS)
    sl = [slice(p * PAIR_W, (p + 1) * PAIR_W) for p in pairs]

    def block_diag(x):
        xb = x.astype(BF16)
        return jnp.where(m["block_diag"], jnp.concatenate([xb, xb], axis=0), zero)

    def pair_mul(lhs, *rhs):
        rb = jnp.concatenate([block_diag(x) for x in rhs], axis=1) if len(rhs) > 1 else block_diag(rhs[0])
        out = jnp.dot(lhs.astype(BF16), rb, preferred_element_type=F32)
        return [out[:, i * PAIR_W:(i + 1) * PAIR_W] for i in range(len(rhs))]

    def each(fn, *lists):
        return [fn(*args) for args in zip(*lists)]

    lhs, rhs, vp, ends, gammas = [], [], [], [], []
    for c in range(chunks):
        rows = pl.ds(c * CHUNK, CHUNK)
        lw = lw_ref[rows, :]
        cum = _mask_dot(ltri, lw)
        cum_end = cum[CHUNK - 1:CHUNK, :]
        e_inv = jnp.exp(-cum)
        e_end = jnp.exp(cum_end - cum)
        kka = kka_ref[rows, :]
        k = k_ref[rows, :]
        a_bar = (-kk_ref[rows, :] * jnp.exp(cum - lw)).astype(BF16)
        r_bar = (r_ref[rows, :] * jnp.exp(cum)).astype(BF16)
        b_til = (kka * e_inv).astype(BF16)
        k_til = (k * e_inv).astype(BF16)
        b_end = (kka * e_end).astype(BF16)
        k_end = (k * e_end).astype(BF16)
        gamma = jnp.exp(cum_end)
        v = v_ref[rows, :].astype(BF16)

        for s in sl:
            lhs.append(jnp.concatenate([a_bar[:, s], r_bar[:, s]], axis=0))
            rhs.append(jnp.where(m["stacked_diag"], jnp.concatenate(
                [b_til[:, s], b_til[:, s], k_til[:, s], k_til[:, s]], axis=0), zero))
            vp.append(v[:, s])
            ends.append(jnp.concatenate([b_end[:, s], k_end[:, s]], axis=0))
            gammas.append(gamma[:, s])

    blocks = each(_bdot_nt, lhs, rhs)
    a_ab = [jnp.where(m["strict"], b[:CHUNK, :PAIR_W], 0.0) for b in blocks]
    a_ak = [jnp.where(m["strict"], b[:CHUNK, PAIR_W:], 0.0) for b in blocks]
    r_ab = [jnp.where(m["incl"], b[CHUNK:, :PAIR_W], 0.0) for b in blocks]
    r_ak = [jnp.where(m["incl"], b[CHUNK:, PAIR_W:], 0.0) for b in blocks]
    from_v = [jnp.dot(jnp.concatenate([ak, rk], axis=0).astype(BF16), block_diag(x),
                      preferred_element_type=F32) for ak, rk, x in zip(a_ak, r_ak, vp)]
    d = [jnp.where(m["same_block"], a, 0.0) for a in a_ab]
    low = [a - di for a, di in zip(a_ab, d)]
    pm = [m["eye"] + di for di in d]
    dpow = [pair_mul(di, di)[0] for di in d]
    for _ in range(INV_BLOCK.bit_length() - 3):
        both = each(lambda x, p: pair_mul(x, x, p), dpow, pm)
        dpow = [b[0] for b in both]
        pm = [p + b[1] for p, b in zip(pm, both)]
    pm = [p + pair_mul(x, p)[0] for p, x in zip(pm, dpow)]
    both = each(lambda p, lo, fv, l: pair_mul(p, lo, fv[:CHUNK], l[:CHUNK]), pm, low, from_v, lhs)
    both2 = each(lambda b: pair_mul(b[0], b[0], b[1], b[2]), both)
    xw_v = [b[1] + b2[1] for b, b2 in zip(both, both2)]
    xw_a = [b[2] + b2[2] for b, b2 in zip(both, both2)]
    both3 = each(lambda b2, xv, xa: pair_mul(b2[0], xv, xa), both2, xw_v, xw_a)
    u_v = [xv + b3[0] for xv, b3 in zip(xw_v, both3)]
    wm = [xa + b3[1] for xa, b3 in zip(xw_a, both3)]
    both4 = each(lambda rb, uv, w: pair_mul(rb, uv, w), r_ab, u_v, wm)
    y_local = [fv[CHUNK:] + b4[0] for fv, b4 in zip(from_v, both4)]
    rm = [l[CHUNK:] + b4[1] for l, b4 in zip(lhs, both4)]
    gm = [jnp.where(m["block_diag"], _bdot_tn(en[:CHUNK], w), 0.0) for en, w in zip(ends, wm)]
    qm = [jnp.where(m["block_diag"], _bdot_tn(jnp.concatenate([uv.astype(BF16), x], axis=0), en), 0.0)
          for uv, x, en in zip(u_v, vp, ends)]

    state = [st_ref[p] for p in pairs]
    for c in range(chunks):
        rows = pl.ds(c * CHUNK, CHUNK)
        item = slice(c * N_PAIRS, (c + 1) * N_PAIRS)
        y = [yl + _bdot_nt(r, st) for yl, r, st in zip(y_local[item], rm[item], state)]
        state = [st * g + _bdot_nt(st, gmat) + q
                 for st, g, gmat, q in zip(state, gammas[item], gm[item], qm[item])]
        outs = []
        inv_n = 1.0 / RWKV_HEAD
        for yp in y:
            def head_stat(z):
                s0 = jnp.sum(jnp.where(m["first_head"], z, 0.0), axis=-1, keepdims=True)
                s1 = jnp.sum(jnp.where(m["first_head"], 0.0, z), axis=-1, keepdims=True)
                return jnp.where(m["first_head"], s0, s1) * inv_n
            yc = yp - head_stat(yp)
            outs.append(yc * lax.rsqrt(head_stat(yc * yc) + RWKV_LNX_EPS))
        yn = jnp.concatenate(outs, axis=1) * lg_ref[...] + lb_ref[...]
        o_ref[rows, :] = ((yn + bonus_ref[rows, :]) * g_ref[rows, :]).astype(o_ref.dtype)
    for p in pairs:
        st_ref[p] = state[p]


def _rwkv_scan(prep, batch, seq, lnx_g, lnx_b):
    t = batch * seq
    chunks = 8
    rows = CHUNK * chunks
    nblk = seq // rows
    spec = pl.BlockSpec((rows, RWKV_WIDTH), lambda b, n: (b * nblk + n, 0))
    pspec = pl.BlockSpec((1, RWKV_WIDTH), lambda b, n: (0, 0))
    return pl.pallas_call(
        functools.partial(_rwkv_scan_kernel, chunks=chunks),
        out_shape=jax.ShapeDtypeStruct((t, RWKV_WIDTH), BF16),
        grid=(batch, nblk),
        in_specs=[spec] * 8 + [pspec, pspec],
        out_specs=spec,
        scratch_shapes=[pltpu.VMEM((N_PAIRS, PAIR_W, PAIR_W), F32)],
        compiler_params=_cparams("parallel", "arbitrary"),
    )(*prep, lnx_g, lnx_b)


def _merge_kernel(og_ref, ol_ref, or_ref, x_ref, wgt_ref, bgt_ref, pg_ref, pl_ref, pr_ref, wo_ref,
                  lg_ref, lb_ref, xo_ref):
    d = D_MODEL
    xin = x_ref[...].astype(BF16)
    merged = None
    for b, (o_ref, p_ref) in enumerate(((og_ref, pg_ref), (ol_ref, pl_ref), (or_ref, pr_ref))):
        cols = slice(b * d, (b + 1) * d)
        gate = _sigmoid(jnp.dot(xin, wgt_ref[:, cols], preferred_element_type=F32) + bgt_ref[:, cols])
        term = gate * jnp.dot(o_ref[...], p_ref[...], preferred_element_type=F32)
        merged = term if merged is None else merged + term
    z = DEEPNORM_ALPHA * x_ref[...] + _bdot(merged, wo_ref[...])
    out = _layer_norm_rows(z, lg_ref[...], lb_ref[...], LN_EPS)
    xo_ref[...] = out


def _merge(o_gla, o_lru, o_rwkv, x, w_gates, b_gates, p_gla, p_lru, p_rwkv, w_out, ln_g, ln_b):
    t = x.shape[0]
    tm = min(512, t)

    def rows(width):
        return pl.BlockSpec((tm, width), lambda i: (i, 0))

    def full(a):
        return pl.BlockSpec(a.shape, lambda i: (0,) * a.ndim)

    params = (w_gates, b_gates, p_gla, p_lru, p_rwkv, w_out, ln_g, ln_b)
    return pl.pallas_call(
        _merge_kernel,
        out_shape=jax.ShapeDtypeStruct((t, D_MODEL), F32),
        grid=(t // tm,),
        in_specs=[rows(GLA_VW), rows(LRU_WIDTH), rows(RWKV_WIDTH), rows(D_MODEL)] + [full(a) for a in params],
        out_specs=rows(D_MODEL),
        compiler_params=_cparams("parallel"),
    )(o_gla, o_lru, o_rwkv, x, *params)


FFN_SPLIT = 2


def _ffn_kernel(x_ref, wg_ref, wu_ref, wd_ref, lg_ref, lb_ref, xo_ref):
    x = x_ref[...]
    xb = x.astype(BF16)
    tf = wg_ref.shape[1] // FFN_SPLIT
    acc = None
    for f in range(FFN_SPLIT):
        cols = slice(f * tf, (f + 1) * tf)
        g = jnp.dot(xb, wg_ref[:, cols], preferred_element_type=F32)
        u = jnp.dot(xb, wu_ref[:, cols], preferred_element_type=F32)
        part = _bdot(g * _sigmoid(g) * u, wd_ref[cols, :])
        acc = part if acc is None else acc + part
    z = DEEPNORM_ALPHA * x + acc
    xo_ref[...] = _layer_norm_rows(z, lg_ref[...], lb_ref[...], LN_EPS)


def _dense_ffn(x, wg, wu, wd, ln_g, ln_b):
    t = x.shape[0]
    tm = min(512, t)
    rows = pl.BlockSpec((tm, D_MODEL), lambda i: (i, 0))
    resident = lambda a: pl.BlockSpec(a.shape, lambda i: (0,) * a.ndim, pipeline_mode=pl.Buffered(1))
    return pl.pallas_call(
        _ffn_kernel,
        out_shape=jax.ShapeDtypeStruct((t, D_MODEL), F32),
        grid=(t // tm,),
        in_specs=[rows] + [resident(a) for a in (wg, wu, wd, ln_g, ln_b)],
        out_specs=rows,
        compiler_params=_cparams("parallel"),
    )(x, wg, wu, wd, ln_g, ln_b)


def _router_kernel(x_ref, wr_ref, meta_ref, wts_ref, cnt_ref, carry_ref):
    tm = x_ref.shape[0]

    @pl.when(pl.program_id(0) == 0)
    def _():
        carry_ref[...] = jnp.zeros_like(carry_ref)

    lane = lax.broadcasted_iota(I32, (tm, LANES), 1)
    neg = jnp.float32(-jnp.inf)
    logits = jnp.where(lane < N_EXPERTS, _fdot(x_ref[...], wr_ref[...]), neg)
    m1 = jnp.max(logits, axis=-1, keepdims=True)
    e1 = jnp.min(jnp.where(logits == m1, lane, LANES), axis=-1, keepdims=True)
    rest = jnp.where(lane == e1, neg, logits)
    m2 = jnp.max(rest, axis=-1, keepdims=True)
    e2 = jnp.min(jnp.where(rest == m2, lane, LANES), axis=-1, keepdims=True)
    ex = jnp.exp(m2 - m1)
    w1 = 1.0 / (1.0 + ex)
    w2 = ex / (1.0 + ex)

    hot1 = lane == e1
    hot2 = lane == e2
    onehot = jnp.where(hot1 | hot2, 1.0, 0.0)
    row, col = _tri_masks(tm)
    before = (row > col).astype(BF16)
    prefix = jnp.dot(before, onehot.astype(BF16), preferred_element_type=F32) + carry_ref[...]
    rank1 = jnp.sum(jnp.where(hot1, prefix, 0.0), axis=-1, keepdims=True).astype(I32)
    rank2 = jnp.sum(jnp.where(hot2, prefix, 0.0), axis=-1, keepdims=True).astype(I32)
    carry_ref[...] += jnp.sum(onehot, axis=0, keepdims=True)
    cnt_ref[...] = carry_ref[...]

    meta = jnp.where(lane == 0, e1, jnp.where(lane == 1, e2, jnp.where(lane == 2, rank1, rank2)))
    meta_ref[...] = jnp.where(lane < 4, meta, 0)
    wts_ref[...] = jnp.where(lane == 0, w1, jnp.where(lane == 1, w2, 0.0))


def _router(x, wr_pad):
    t = x.shape[0]
    tm = min(512, t)
    return pl.pallas_call(
        _router_kernel,
        out_shape=[jax.ShapeDtypeStruct((t, LANES), I32), jax.ShapeDtypeStruct((t, LANES), F32),
                   jax.ShapeDtypeStruct((1, LANES), F32)],
        grid=(t // tm,),
        in_specs=[pl.BlockSpec((tm, D_MODEL), lambda i: (i, 0)),
                  pl.BlockSpec((D_MODEL, LANES), lambda i: (0, 0))],
        out_specs=[pl.BlockSpec((tm, LANES), lambda i: (i, 0)), pl.BlockSpec((tm, LANES), lambda i: (i, 0)),
                   pl.BlockSpec((1, LANES), lambda i: (0, 0))],
        scratch_shapes=[pltpu.VMEM((1, LANES), F32)],
        compiler_params=_cparams("arbitrary"),
    )(x, wr_pad)


def _dispatch_kernel(pad_start_ref, pad_len_ref, nused_ref, dest_ref, x_ref, xs_ref, zeros_ref, sem, fill_sem):
    tm = x_ref.shape[0]
    n_blocks = xs_ref.shape[0] // MOE_ROWS

    @pl.when(pl.program_id(0) == pl.num_programs(0) - 1)
    def _():
        zeros_ref[...] = jnp.zeros_like(zeros_ref)
        pad_copy = lambda row: pltpu.make_async_copy(zeros_ref.at[pl.ds(0, 1)], xs_ref.at[pl.ds(row, 1)], fill_sem)
        blk_copy = lambda blk: pltpu.make_async_copy(
            zeros_ref, xs_ref.at[pl.ds(pl.multiple_of(blk * MOE_ROWS, MOE_ROWS), MOE_ROWS)], fill_sem)

        def for_each(fn):
            for e in range(N_EXPERTS):
                base = pad_start_ref[e]
                lax.fori_loop(0, pad_len_ref[e], lambda r, c: (fn(pad_copy(base + r)), c)[1], 0)
            lax.fori_loop(nused_ref[0], n_blocks, lambda b, c: (fn(blk_copy(b)), c)[1], 0)

        for_each(lambda cp: cp.start())
        for_each(lambda cp: cp.wait())

    def row_copy(r, j):
        return pltpu.make_async_copy(x_ref.at[pl.ds(r, 1)], xs_ref.at[pl.ds(dest_ref[TOP_K * r + j], 1)], sem)

    def start(r, c):
        for j in range(TOP_K):
            row_copy(r, j).start(priority=j)
        return c

    lax.fori_loop(0, tm, start, 0, unroll=8)
    for j in range(TOP_K):
        pltpu.make_async_copy(x_ref, xs_ref.at[pl.ds(0, tm)], sem).wait()


def _dispatch(x, dest_flat, pad_start, pad_len, n_used, n_slots):
    t = x.shape[0]
    tm = min(256, t)
    grid_spec = pltpu.PrefetchScalarGridSpec(
        num_scalar_prefetch=3,
        grid=(t // tm,),
        in_specs=[pl.BlockSpec((tm * TOP_K,), lambda i, *_: (i,), memory_space=pltpu.SMEM),
                  pl.BlockSpec((tm, D_MODEL), lambda i, *_: (i, 0))],
        out_specs=pl.BlockSpec(memory_space=pl.ANY),
        scratch_shapes=[pltpu.VMEM((MOE_ROWS, D_MODEL), F32), pltpu.SemaphoreType.DMA(()),
                        pltpu.SemaphoreType.DMA(())],
    )
    return pl.pallas_call(
        _dispatch_kernel,
        out_shape=jax.ShapeDtypeStruct((n_slots, D_MODEL), F32),
        grid_spec=grid_spec,
        compiler_params=_cparams("arbitrary"),
    )(pad_start, pad_len, n_used, dest_flat, x)


def _expert_kernel(blk_e_ref, nused_ref, xs_ref, wg_ref, wu_ref, wd_ref, ys_ref, acc_ref):
    del blk_e_ref
    i = pl.program_id(0)
    f = pl.program_id(1)
    last = pl.num_programs(1) - 1
    used = i < nused_ref[0]

    @pl.when(used & (f == 0))
    def _():
        acc_ref[...] = jnp.zeros_like(acc_ref)

    @pl.when(used)
    def _():
        xb = xs_ref[...].astype(BF16)
        g = jnp.dot(xb, wg_ref[0], preferred_element_type=F32)
        u = jnp.dot(xb, wu_ref[0], preferred_element_type=F32)
        acc_ref[...] += _bdot(g * _sigmoid(g) * u, wd_ref[0])

    @pl.when(used & (f == last))
    def _():
        ys_ref[...] = acc_ref[...]

    @pl.when(jnp.logical_not(used) & (f == last))
    def _():
        ys_ref[...] = jnp.zeros_like(ys_ref)


def _experts(xs, blk_e, n_used, wg, wu, wd):
    n_slots = xs.shape[0]
    n_blocks = n_slots // MOE_ROWS
    ff = wg.shape[2]
    tf = ff // 2
    grid_spec = pltpu.PrefetchScalarGridSpec(
        num_scalar_prefetch=2,
        grid=(n_blocks, ff // tf),
        in_specs=[pl.BlockSpec((MOE_ROWS, D_MODEL), lambda i, f, be, nu: (i, 0)),
                  pl.BlockSpec((1, D_MODEL, tf), lambda i, f, be, nu: (be[i], 0, f)),
                  pl.BlockSpec((1, D_MODEL, tf), lambda i, f, be, nu: (be[i], 0, f)),
                  pl.BlockSpec((1, tf, D_MODEL), lambda i, f, be, nu: (be[i], f, 0))],
        out_specs=pl.BlockSpec((MOE_ROWS, D_MODEL), lambda i, f, be, nu: (i, 0)),
        scratch_shapes=[pltpu.VMEM((MOE_ROWS, D_MODEL), F32)],
    )
    return pl.pallas_call(
        _expert_kernel,
        out_shape=jax.ShapeDtypeStruct((n_slots, D_MODEL), F32),
        grid_spec=grid_spec,
        compiler_params=_cparams("arbitrary", "arbitrary"),
    )(blk_e, n_used, xs, wg, wu, wd)


def _combine_kernel(dest_ref, dest_next_ref, x_ref, wts_ref, ys_ref, lg_ref, lb_ref, xo_ref, buf_ref, sem):
    tm = x_ref.shape[0]
    i = pl.program_id(0)
    n = pl.num_programs(0)

    def gather(idx_ref, slot):
        def start(r, c):
            for j in range(TOP_K):
                pltpu.make_async_copy(ys_ref.at[pl.ds(idx_ref[TOP_K * r + j], 1)],
                                      buf_ref.at[slot, j, pl.ds(r, 1)], sem.at[slot]).start(priority=j)
            return c
        lax.fori_loop(0, tm, start, 0, unroll=8)

    slot = i & 1

    @pl.when(i == 0)
    def _():
        gather(dest_ref, 0)

    @pl.when(i + 1 < n)
    def _():
        gather(dest_next_ref, 1 - slot)

    for j in range(TOP_K):
        pltpu.make_async_copy(ys_ref.at[pl.ds(0, tm)], buf_ref.at[slot, j], sem.at[slot]).wait()
    w = wts_ref[...]
    f = w[:, 0:1] * buf_ref[slot, 0] + w[:, 1:2] * buf_ref[slot, 1]
    z = DEEPNORM_ALPHA * x_ref[...] + f
    xo_ref[...] = _layer_norm_rows(z, lg_ref[...], lb_ref[...], LN_EPS)


def _combine(x, dest_flat, wts, ys, ln_g, ln_b):
    t = x.shape[0]
    tm = min(256, t)
    n_tiles = t // tm
    rows = lambda width: pl.BlockSpec((tm, width), lambda i: (i, 0))
    vec = pl.BlockSpec((1, D_MODEL), lambda i: (0, 0))
    return pl.pallas_call(
        _combine_kernel,
        out_shape=jax.ShapeDtypeStruct((t, D_MODEL), F32),
        grid=(n_tiles,),
        in_specs=[pl.BlockSpec((tm * TOP_K,), lambda i: (i,), memory_space=pltpu.SMEM),
                  pl.BlockSpec((tm * TOP_K,), lambda i: (jnp.minimum(i + 1, n_tiles - 1),),
                               memory_space=pltpu.SMEM),
                  rows(D_MODEL), rows(LANES), pl.BlockSpec(memory_space=pl.ANY), vec, vec],
        out_specs=rows(D_MODEL),
        scratch_shapes=[pltpu.VMEM((2, TOP_K, tm, D_MODEL), F32), pltpu.SemaphoreType.DMA((2,))],
        compiler_params=_cparams("arbitrary"),
    )(dest_flat, dest_flat, x, wts, ys, ln_g, ln_b)


def _moe_ffn(x, w_router, wg, wu, wd, ln_g, ln_b):
    t = x.shape[0]
    wr_pad = jnp.pad(w_router, ((0, 0), (0, LANES - N_EXPERTS)))
    meta, wts, cnt = _router(x, wr_pad)
    counts = cnt[0, :N_EXPERTS].astype(I32)
    padded = (counts + MOE_ROWS - 1) // MOE_ROWS * MOE_ROWS
    seg_end = jnp.cumsum(padded)
    seg_start = seg_end - padded
    n_blocks = (t * TOP_K) // MOE_ROWS + N_EXPERTS
    dest = seg_start[meta[:, 0:TOP_K]] + meta[:, TOP_K:2 * TOP_K]
    dest_flat = dest.reshape(-1).astype(I32)
    blk_start = jnp.arange(n_blocks, dtype=I32) * MOE_ROWS
    blk_e = jnp.minimum(jnp.sum(blk_start[:, None] >= seg_end[None, :], axis=1), N_EXPERTS - 1).astype(I32)
    n_used = (seg_end[-1:] // MOE_ROWS).astype(I32)
    xs = _dispatch(x, dest_flat, (seg_start + counts).astype(I32), (padded - counts).astype(I32), n_used,
                   n_blocks * MOE_ROWS)
    ys = _experts(xs, blk_e, n_used, wg, wu, wd)
    return _combine(x, dest_flat, wts, ys, ln_g, ln_b)


def _pad_rows(w, rows, at=0):
    out = jnp.zeros((rows, w.shape[1]), w.dtype)
    return out.at[at:at + w.shape[0]].set(w)


def _reorder_in_projection(w, b):
    sizes = (GLA_KW, GLA_KW, GLA_VW, GLA_VW, GLA_DECAY_RANK, LRU_WIDTH, LRU_WIDTH,
             3 * RWKV_WIDTH, RWKV_DECAY_RANK, RWKV_A_RANK, RWKV_GATE_RANK, N_BRANCH * D_MODEL)
    offs = [0]
    for s in sizes:
        offs.append(offs[-1] + s)
    wb = jnp.concatenate([w, b[None, :]], axis=0)
    piece = lambda i: wb[:, offs[i]:offs[i + 1]]
    zeros = lambda n: jnp.zeros((wb.shape[0], n), wb.dtype)
    q, k, v, r, dec, lx, lg, rkv, wl, al, gl, gates = (piece(i) for i in range(len(sizes)))
    small = jnp.concatenate([wl, al, gl, zeros(RW_SMALL_W - SM_GL[0] - RWKV_GATE_RANK)], axis=1)
    groups = {"gla": jnp.concatenate([q, k, v, r, dec, zeros(GLA_DEC_W - GLA_DECAY_RANK)], axis=1),
              "lru": jnp.concatenate([lx, lg], axis=1),
              "rwkv": jnp.concatenate([rkv, small], axis=1),
              "gates": gates}
    return {name: (g[:-1].astype(BF16), g[-1:]) for name, g in groups.items()}


def _cast_kernel(x_ref, o_ref):
    o_ref[...] = x_ref[...].astype(o_ref.dtype)


CAST_TILE_ELEMS = 2 * 1024 * 1024


def _to_bf16(w):
    cols = w.shape[-1]
    w2 = w.reshape(-1, cols)
    n_rows = w2.shape[0]
    tr = n_rows
    while tr * cols > CAST_TILE_ELEMS and tr % 32 == 0:
        tr //= 2
    out = pl.pallas_call(
        _cast_kernel,
        out_shape=jax.ShapeDtypeStruct(w2.shape, BF16),
        grid=(n_rows // tr,),
        in_specs=[pl.BlockSpec((tr, cols), lambda i: (i, 0))],
        out_specs=pl.BlockSpec((tr, cols), lambda i: (i, 0)),
        compiler_params=_cparams("parallel"),
    )(w2)
    return out.reshape(w.shape)


def _block_diag(blocks):
    n, bi, bo = blocks.shape
    eye = jnp.eye(n, dtype=blocks.dtype)
    return (eye[:, None, :, None] * blocks[:, :, None, :]).reshape(n * bi, n * bo)


def kernel(x, w_in, b_in, gla_w_decay_up, gla_b_decay, gla_norm_g, gla_norm_b, lru_conv_w, lru_conv_b, lru_w_r, lru_b_r, lru_w_i, lru_b_i, lru_lambda, rwkv_mu, rwkv_w0, rwkv_w2, rwkv_a0, rwkv_a2, rwkv_g2, rwkv_k_k, rwkv_k_a, rwkv_r_k, rwkv_lnx_g, rwkv_lnx_b, p_gla, p_lru, p_rwkv, w_out, ln_mix_g, ln_mix_b, ffn_w_gate, ffn_w_up, ffn_w_down, moe_w_router, moe_w_gate, moe_w_up, moe_w_down, ln_ffn_g, ln_ffn_b):
    batch, seq, d = x.shape
    t = batch * seq
    xf = x.reshape(t, d)
    row = lambda a: a.reshape(1, -1)
    head_ones = _block_diag(jnp.ones((RWKV_HEADS, RWKV_HEAD, RWKV_HEAD), BF16))
    for l in range(DEPTH):
        proj = _reorder_in_projection(w_in[l], b_in[l])
        w_gates, b_gates = proj["gates"]

        wup = _pad_rows(gla_w_decay_up[l], GLA_DEC_W).astype(BF16)
        o_gla = _gla_branch(xf, batch, seq, *proj["gla"], wup, row(gla_b_decay[l]), row(gla_norm_g[l]),
                            row(gla_norm_b[l]))

        w_ri = jnp.concatenate([_block_diag(lru_w_r[l]), _block_diag(lru_w_i[l])], axis=1).astype(BF16)
        b_ri = jnp.concatenate([lru_b_r[l], lru_b_i[l]])[None, :]
        o_lru = _lru_branch(xf, batch, seq, *proj["lru"], lru_conv_w[l], row(lru_conv_b[l]), w_ri, b_ri,
                            row(jax.nn.softplus(-lru_lambda[l])))

        mu = rwkv_mu[l]
        mu_all = jnp.zeros((3 * RWKV_WIDTH + RW_SMALL_W,), F32).at[0:mu.shape[0]].set(mu)
        prep_params = (*proj["rwkv"], row(mu_all),
                       _pad_rows(rwkv_w2[l], SM_WA[1], 0).astype(BF16),
                       _pad_rows(rwkv_a2[l], SM_WA[1], RWKV_DECAY_RANK).astype(BF16),
                       _pad_rows(rwkv_g2[l], SM_GL[1], 0).astype(BF16),
                       row(rwkv_w0[l]), row(rwkv_a0[l]), row(rwkv_k_k[l]), row(rwkv_k_a[l]),
                       row(rwkv_r_k[l]), head_ones)
        prep = _rwkv_prep(xf, batch, seq, prep_params)
        o_rwkv = _rwkv_scan(prep, batch, seq, row(rwkv_lnx_g[l]), row(rwkv_lnx_b[l]))

        xf = _merge(o_gla, o_lru, o_rwkv, xf, w_gates, b_gates, p_gla[l].astype(BF16), p_lru[l].astype(BF16),
                    p_rwkv[l].astype(BF16), w_out[l].astype(BF16), row(ln_mix_g[l]), row(ln_mix_b[l]))
        i = l // 2
        if l % 2 == 0:
            xf = _dense_ffn(xf, ffn_w_gate[i].astype(BF16), ffn_w_up[i].astype(BF16),
                            ffn_w_down[i].astype(BF16), row(ln_ffn_g[l]), row(ln_ffn_b[l]))
        else:
            xf = _moe_ffn(xf, moe_w_router[i], _to_bf16(moe_w_gate[i]), _to_bf16(moe_w_up[i]),
                          _to_bf16(moe_w_down[i]), row(ln_ffn_g[l]), row(ln_ffn_b[l]))
    return xf.reshape(batch, seq, d)
```

```python
import functools

import jax
import jax.numpy as jnp
from jax import lax
from jax.experimental import pallas as pl
from jax.experimental.pallas import tpu as pltpu

F32 = jnp.float32
BF16 = jnp.bfloat16
I32 = jnp.int32
HIGHEST = lax.Precision.HIGHEST

D_MODEL = 1024
DEPTH = 2
GLA_HEADS, GLA_DK, GLA_DV = 4, 64, 128
GLA_KW, GLA_VW = GLA_HEADS * GLA_DK, GLA_HEADS * GLA_DV
GLA_DECAY_RANK = 16
GLA_TEMP = 16.0
LRU_WIDTH, LRU_BLOCKS, LRU_CONV, LRU_C = 512, 8, 4, 8.0
RWKV_HEAD, RWKV_WIDTH = 64, 512
RWKV_HEADS = RWKV_WIDTH // RWKV_HEAD
RWKV_DECAY_RANK, RWKV_A_RANK, RWKV_GATE_RANK = 64, 64, 160
RWKV_LNX_EPS = 64e-5
N_BRANCH = 3
N_EXPERTS, TOP_K = 8, 2
DEEPNORM_ALPHA = (2 * DEPTH) ** 0.25
LN_EPS = 1e-5

LANES = 128
SUBLANES = 8
VMEM_LIMIT_BYTES = 56 * 1024 * 1024

SM_WA = (0, 128)
SM_GL = (128, 256)
RW_SMALL_W = 512
GLA_DEC_W = LANES
GLA_COLS = {"q": (0, 256), "k": (256, 512), "v": (512, 1024), "r": (1024, 1536), "dec": (1536, 1664)}

CHUNK = 64
INV_BLOCK = 16
MOE_ROWS = 512


def _cparams(*sem):
    return pltpu.CompilerParams(dimension_semantics=sem, vmem_limit_bytes=VMEM_LIMIT_BYTES)


def _sigmoid(x):
    return 1.0 / (1.0 + jnp.exp(-x))


def _softplus(x):
    return jnp.maximum(x, 0.0) + jnp.log(1.0 + jnp.exp(-jnp.abs(x)))


def _bdot(a, b):
    return jnp.dot(a.astype(BF16), b.astype(BF16), preferred_element_type=F32)


def _bdot_nt(a, b):
    return lax.dot_general(a.astype(BF16), b.astype(BF16), (((1,), (1,)), ((), ())),
                           preferred_element_type=F32)


def _bdot_tn(a, b):
    return lax.dot_general(a.astype(BF16), b.astype(BF16), (((0,), (0,)), ((), ())),
                           preferred_element_type=F32)


def _fdot(a, b):
    ah = a.astype(BF16)
    al = (a - ah.astype(F32)).astype(BF16)
    bh = b.astype(BF16)
    bl = (b - bh.astype(F32)).astype(BF16)
    dot = lambda u, v: jnp.dot(u, v, preferred_element_type=F32)
    return dot(al, bh) + dot(ah, bl) + dot(ah, bh)


def _split3(x):
    hi = x.astype(BF16)
    rest = x - hi.astype(F32)
    mid = rest.astype(BF16)
    lo = (rest - mid.astype(F32)).astype(BF16)
    return hi, mid, lo


def _mask_dot(mask_bf16, x):
    hi, mid, lo = _split3(x)
    dot = lambda part: jnp.dot(mask_bf16, part, preferred_element_type=F32)
    return dot(lo) + dot(mid) + dot(hi)


def _dot_mask(x, mask_bf16):
    hi, mid, lo = _split3(x)
    dot = lambda part: jnp.dot(part, mask_bf16, preferred_element_type=F32)
    return dot(lo) + dot(mid) + dot(hi)


def _layer_norm_rows(z, g, b, eps):
    mu = jnp.mean(z, axis=-1, keepdims=True)
    zc = z - mu
    var = jnp.mean(zc * zc, axis=-1, keepdims=True)
    return zc * lax.rsqrt(var + eps) * g + b


def _tri_masks(n):
    row = lax.broadcasted_iota(I32, (n, n), 0)
    col = lax.broadcasted_iota(I32, (n, n), 1)
    return row, col


MXU_COLS = 256


def _gla_kernel(*refs, chunks, blocks_per_seq):
    *io_refs, st_ref, ha_ref, hb_ref = refs
    s = pl.program_id(0)

    @pl.when(s == 0)
    def _():
        hb_ref[...] = jnp.zeros_like(hb_ref)

    @pl.when((s == 0) | (((s + blocks_per_seq - 1) % blocks_per_seq) == 0))
    def _():
        st_ref[...] = jnp.zeros_like(st_ref)

    body = functools.partial(_gla_body, chunks, (*io_refs, st_ref))

    @pl.when((s & 1) == 0)
    def _():
        body(ha_ref, hb_ref)

    @pl.when((s & 1) == 1)
    def _():
        body(hb_ref, ha_ref)


def _gla_body(chunks, refs, h_next, h_ref):
    x_ref, w_ref, b_ref, wup_ref, bdec_ref, ng_ref, nb_ref, o_ref, st_ref = refs
    xb = x_ref[...].astype(BF16)
    width = w_ref.shape[1]
    pieces = iter([slice(c0, min(c0 + MXU_COLS, width)) for c0 in range(0, width, MXU_COLS)])

    def project_piece():
        cols = next(pieces, None)
        if cols is not None:
            h_next[:, cols] = jnp.dot(xb, w_ref[:, cols], preferred_element_type=F32) + b_ref[:, cols]

    q_ref = h_ref.at[:, GLA_COLS["q"][0]:GLA_COLS["q"][1]]
    k_ref = h_ref.at[:, GLA_COLS["k"][0]:GLA_COLS["k"][1]]
    v_ref = h_ref.at[:, GLA_COLS["v"][0]:GLA_COLS["v"][1]]
    r_ref = h_ref.at[:, GLA_COLS["r"][0]:GLA_COLS["r"][1]]
    dec_ref = h_ref.at[:, GLA_COLS["dec"][0]:GLA_COLS["dec"][1]]

    row, col = _tri_masks(CHUNK)
    ltri = (row >= col).astype(BF16)
    kw, vw = 2 * GLA_DK, 2 * GLA_DV
    key_shift, val_shift = GLA_DK.bit_length() - 1, GLA_DV.bit_length() - 1
    iota = lambda shape, axis: lax.broadcasted_iota(I32, shape, axis)
    causal = iota((CHUNK, kw), 0) >= (iota((CHUNK, kw), 1) & (GLA_DK - 1))
    key_diag = (iota((kw, kw), 0) >> key_shift) == (iota((kw, kw), 1) >> key_shift)
    val_diag = (iota((kw, vw), 0) >> key_shift) == (iota((kw, vw), 1) >> val_shift)
    state_diag = (iota((vw, kw), 0) >> val_shift) == (iota((vw, kw), 1) >> key_shift)
    zero = jnp.zeros((), BF16)
    pairs = range(GLA_HEADS // 2)
    qp, vp, k_diag, kep, decays = [], [], [], [], []
    for c in range(chunks):
        project_piece()
        rows = pl.ds(c * CHUNK, CHUNK)
        logits = _bdot(dec_ref[rows, :], wup_ref[...]) + bdec_ref[...]
        log_a = -_softplus(-logits) * (1.0 / GLA_TEMP)
        cum = _mask_dot(ltri, log_a)
        cum_end = cum[CHUNK - 1:CHUNK, :]
        q = q_ref[rows, :] * (GLA_DK ** -0.5)
        k = k_ref[rows, :]
        q_dec = (q * jnp.exp(cum)).astype(BF16)
        k_inv = (k * jnp.exp(-cum)).astype(BF16)
        k_end = (k * jnp.exp(cum_end - cum)).astype(BF16)
        decay_end = jnp.exp(cum_end)
        v = v_ref[rows, :].astype(BF16)
        for p in pairs:
            s = slice(p * kw, (p + 1) * kw)
            qp.append(q_dec[:, s])
            vp.append(v[:, p * vw:(p + 1) * vw])
            k_diag.append(jnp.where(key_diag, jnp.concatenate([k_inv[:, s], k_inv[:, s]], axis=0), zero))
            kep.append(k_end[:, s])
            decays.append(decay_end[:, s])

    for _ in range(chunks):
        project_piece()

    v_diag = [jnp.where(val_diag, jnp.concatenate([x, x], axis=0), zero) for x in vp]
    scores = [jnp.where(causal, _bdot_nt(q, kd), 0.0) for q, kd in zip(qp, k_diag)]
    o_local = [_bdot(s, vd) for s, vd in zip(scores, v_diag)]
    upd = [jnp.where(state_diag, _bdot_tn(x, ke), 0.0) for x, ke in zip(vp, kep)]

    state = [st_ref[p] for p in pairs]
    n_pairs = len(pairs)
    for c in range(chunks):
        rows = pl.ds(c * CHUNK, CHUNK)
        item = slice(c * n_pairs, (c + 1) * n_pairs)
        o = [ol + _bdot_nt(q, st) for ol, q, st in zip(o_local[item], qp[item], state)]
        state = [st * dcy + u for st, dcy, u in zip(state, decays[item], upd[item])]
        outs = []
        for op in o:
            for oh in (op[:, :GLA_DV], op[:, GLA_DV:]):
                mu = jnp.mean(oh, axis=-1, keepdims=True)
                oc = oh - mu
                var = jnp.mean(oc * oc, axis=-1, keepdims=True)
                outs.append(oc * lax.rsqrt(var + LN_EPS))
        y = jnp.concatenate(outs, axis=1) * ng_ref[...] + nb_ref[...]
        r = r_ref[rows, :]
        o_ref[rows, :] = (y * (r * _sigmoid(r))).astype(o_ref.dtype)
    for p in pairs:
        st_ref[p] = state[p]


def _gla_branch(x, batch, seq, w, bias, wup_pad, b_decay, norm_g, norm_b):
    t = batch * seq
    chunks = 8
    rows = CHUNK * chunks
    nblk = seq // rows

    n_tiles = batch * nblk

    def full(a):
        return pl.BlockSpec(a.shape, lambda s: (0,) * a.ndim)

    args = (w, bias, wup_pad, b_decay, norm_g, norm_b)
    return pl.pallas_call(
        functools.partial(_gla_kernel, chunks=chunks, blocks_per_seq=nblk),
        out_shape=jax.ShapeDtypeStruct((t, GLA_VW), BF16),
        grid=(n_tiles + 1,),
        in_specs=[pl.BlockSpec((rows, D_MODEL), lambda s: (jnp.minimum(s, n_tiles - 1), 0))]
        + [full(a) for a in args],
        out_specs=pl.BlockSpec((rows, GLA_VW), lambda s: (jnp.maximum(s - 1, 0), 0)),
        scratch_shapes=[pltpu.VMEM((GLA_HEADS // 2, 2 * GLA_DV, 2 * GLA_DK), F32),
                        pltpu.VMEM((rows, w.shape[1]), F32), pltpu.VMEM((rows, w.shape[1]), F32)],
        compiler_params=_cparams("arbitrary"),
    )(x, *args)


def _lru_kernel(xin_ref, w_ref, b_ref, cw_ref, cb_ref, wri_ref, bri_ref, sp_ref,
                o_ref, xbuf_ref, a_ref, u_ref, hs_ref, gate_ref, h_ref):
    tm = xin_ref.shape[0]
    pad = SUBLANES

    @pl.when(pl.program_id(1) == 0)
    def _():
        xbuf_ref[0:pad, :] = jnp.zeros((pad, LRU_WIDTH), F32)
        h_ref[...] = jnp.zeros_like(h_ref)

    proj = jnp.dot(xin_ref[...].astype(BF16), w_ref[...], preferred_element_type=F32) + b_ref[...]
    x = proj[:, :LRU_WIDTH]
    gate_ref[...] = jax.nn.gelu(proj[:, LRU_WIDTH:])
    xbuf_ref[pad:pad + tm, :] = x
    xc = cb_ref[...] + x * cw_ref[LRU_CONV - 1:LRU_CONV, :]
    for j in range(LRU_CONV - 1):
        back = LRU_CONV - 1 - j
        xc = xc + xbuf_ref[pl.ds(pad - back, tm), :] * cw_ref[j:j + 1, :]
    xbuf_ref[0:pad, :] = x[tm - pad:tm, :]

    ri = _sigmoid(_bdot(xc, wri_ref[...]) + bri_ref[...])
    rg, ig = ri[:, :LRU_WIDTH], ri[:, LRU_WIDTH:]
    log_a = -LRU_C * rg * sp_ref[...]
    a = jnp.exp(log_a)
    a_ref[...] = a
    u_ref[...] = jnp.sqrt(1.0 - a * a) * (ig * xc)

    def step(t, hprev):
        hnew = a_ref[pl.ds(t, 1), :] * hprev + u_ref[pl.ds(t, 1), :]
        hs_ref[pl.ds(t, 1), :] = hnew
        return hnew

    h_ref[...] = lax.fori_loop(0, tm, step, h_ref[...], unroll=8)
    o_ref[...] = (gate_ref[...] * hs_ref[...]).astype(o_ref.dtype)


def _lru_branch(x, batch, seq, w, bias, conv_w, conv_b, w_ri, b_ri, softplus_neg_lam):
    t = batch * seq
    tm = min(512, seq)
    nblk = seq // tm

    def full(a):
        return pl.BlockSpec(a.shape, lambda b, n: (0,) * a.ndim)

    args = (w, bias, conv_w, conv_b, w_ri, b_ri, softplus_neg_lam)
    return pl.pallas_call(
        _lru_kernel,
        out_shape=jax.ShapeDtypeStruct((t, LRU_WIDTH), BF16),
        grid=(batch, nblk),
        in_specs=[pl.BlockSpec((tm, D_MODEL), lambda b, n: (b * nblk + n, 0))] + [full(a) for a in args],
        out_specs=pl.BlockSpec((tm, LRU_WIDTH), lambda b, n: (b * nblk + n, 0)),
        scratch_shapes=[pltpu.VMEM((tm + SUBLANES, LRU_WIDTH), F32)]
        + [pltpu.VMEM((tm, LRU_WIDTH), F32)] * 4 + [pltpu.VMEM((1, LRU_WIDTH), F32)],
        compiler_params=_cparams("parallel", "arbitrary"),
    )(x, *args)


def _rwkv_prep_kernel(*refs, blocks_per_seq):
    *io_refs, ha_ref, hb_ref, carry_ref = refs
    tm = io_refs[0].shape[0]
    s = pl.program_id(0)
    body = functools.partial(_rwkv_prep_body, tm, s, RWKV_WIDTH, blocks_per_seq, (*io_refs, carry_ref))

    @pl.when(s == 0)
    def _():
        hb_ref[...] = jnp.zeros_like(hb_ref)
        carry_ref[...] = jnp.zeros_like(carry_ref)

    @pl.when((s & 1) == 0)
    def _():
        body(ha_ref, hb_ref)

    @pl.when((s & 1) == 1)
    def _():
        body(hb_ref, ha_ref)


def _rwkv_prep_body(tm, s, gw, blocks_per_seq, refs, h_next, h_cur):
    (x_ref, w_ref, b_ref, mu_ref, w2_ref, a2_ref, g2_ref, w0_ref, a0_ref, kk_ref, ka_ref, rk_ref, bd_ref,
     r_out, k_out, v_out, lw_out, kk_out, kka_out, bonus_out, g_out, carry_ref) = refs
    first = ((s + blocks_per_seq - 1) % blocks_per_seq) == 0
    keep = jnp.where(first, 0.0, 1.0)
    row0 = lax.broadcasted_iota(I32, (tm, 1), 0) == 0
    xb = x_ref[...].astype(BF16)

    pw = gw // 2

    def project(j):
        cols = slice(j * pw, (j + 1) * pw)
        h_next[:, cols] = jnp.dot(xb, w_ref[:, cols], preferred_element_type=F32) + b_ref[:, cols]

    def shifted(j):
        cols = slice(j * gw, (j + 1) * gw)
        cur = h_cur[:, cols]
        prev_row = carry_ref[0:1, cols] * keep
        carry_ref[0:1, cols] = cur[tm - 1:tm, :]
        prev = jnp.where(row0, prev_row, pltpu.roll(cur, 1, axis=0))
        return cur + (prev - cur) * mu_ref[:, cols]

    project(0)
    sm = shifted(3)
    wa = sm[:, SM_WA[0]:SM_WA[0] + SM_WA[1]]
    gl = sm[:, SM_GL[0]:SM_GL[0] + SM_GL[1]]
    project(1)
    w_log = -_softplus(-(w0_ref[...] + _bdot(jnp.tanh(wa), w2_ref[...]))) - 0.5
    lw_out[...] = -jnp.exp(w_log)
    project(2)
    a = _sigmoid(a0_ref[...] + _bdot(wa, a2_ref[...]))
    g_out[...] = _bdot(_sigmoid(gl), g2_ref[...])
    project(3)
    k = shifted(1)
    kk = k * kk_ref[...]
    norm = jnp.sqrt(_dot_mask(kk * kk, bd_ref[...]))
    project(4)
    kk = kk / jnp.maximum(norm, 1e-12)
    kk_out[...] = kk
    kka_out[...] = kk * a
    k2 = k * (1.0 + (a - 1.0) * ka_ref[...])
    k_out[...] = k2
    project(5)
    r = shifted(0)
    r_out[...] = r
    project(6)
    v = shifted(2)
    v_out[...] = v
    project(7)
    bonus_out[...] = _dot_mask(r * k2 * rk_ref[...], bd_ref[...]) * v


def _rwkv_prep(x, batch, seq, params):
    t = batch * seq
    tm = min(512, seq)
    bps = seq // tm
    n_tiles = t // tm
    width = params[0].shape[1]

    def full(a):
        return pl.BlockSpec(a.shape, lambda s: (0,) * a.ndim)

    out = jax.ShapeDtypeStruct((t, RWKV_WIDTH), F32)
    ospec = pl.BlockSpec((tm, RWKV_WIDTH), lambda s: (jnp.maximum(s - 1, 0), 0))
    return pl.pallas_call(
        functools.partial(_rwkv_prep_kernel, blocks_per_seq=bps),
        out_shape=[out] * 8,
        grid=(n_tiles + 1,),
        in_specs=[pl.BlockSpec((tm, D_MODEL), lambda s: (jnp.minimum(s, n_tiles - 1), 0))]
        + [full(a) for a in params],
        out_specs=[ospec] * 8,
        scratch_shapes=[pltpu.VMEM((tm, width), F32), pltpu.VMEM((tm, width), F32),
                        pltpu.VMEM((SUBLANES, width), F32)],
        compiler_params=_cparams("arbitrary"),
    )(x, *params)


PAIR_W = 2 * RWKV_HEAD
N_PAIRS = RWKV_HEADS // 2
HEAD_SHIFT = RWKV_HEAD.bit_length() - 1


def _pair_masks():
    t = lax.broadcasted_iota(I32, (CHUNK, PAIR_W), 0)
    lane = lax.broadcasted_iota(I32, (CHUNK, PAIR_W), 1)
    j = lane & (RWKV_HEAD - 1)
    r = lax.broadcasted_iota(I32, (PAIR_W, PAIR_W), 0)
    c = lax.broadcasted_iota(I32, (PAIR_W, PAIR_W), 1)
    r4 = lax.broadcasted_iota(I32, (2 * PAIR_W, PAIR_W), 0)
    c4 = lax.broadcasted_iota(I32, (2 * PAIR_W, PAIR_W), 1)
    return dict(
        strict=t > j, incl=t >= j, eye=(t == j).astype(F32),
        same_block=jnp.bitwise_xor(t, j) < INV_BLOCK,
        first_head=lane < RWKV_HEAD,
        block_diag=(r >> HEAD_SHIFT) == (c >> HEAD_SHIFT),
        stacked_diag=((r4 >> HEAD_SHIFT) & 1) == (c4 >> HEAD_SHIFT))


def _rwkv_scan_kernel(r_ref, k_ref, v_ref, lw_ref, kk_ref, kka_ref, bonus_ref, g_ref, lg_ref, lb_ref,
                      o_ref, st_ref, *, chunks):
    @pl.when(pl.program_id(1) == 0)
    def _():
        st_ref[...] = jnp.zeros_like(st_ref)

    row, col = _tri_masks(CHUNK)
    ltri = (row >= col).astype(BF16)
    m = _pair_masks()
    zero = jnp.zeros((), BF16)
    pairs = range(N_PAIRS)
    sl = [slice(p * PAIR_W, (p + 1) * PAIR_W) for p in pairs]

    def block_diag(x):
        xb = x.astype(BF16)
        return jnp.where(m["block_diag"], jnp.concatenate([xb, xb], axis=0), zero)

    def pair_mul(lhs, *rhs):
        rb = jnp.concatenate([block_diag(x) for x in rhs], axis=1) if len(rhs) > 1 else block_diag(rhs[0])
        out = jnp.dot(lhs.astype(BF16), rb, preferred_element_type=F32)
        return [out[:, i * PAIR_W:(i + 1) * PAIR_W] for i in range(len(rhs))]

    def each(fn, *lists):
        return [fn(*args) for args in zip(*lists)]

    lhs, rhs, vp, ends, gammas = [], [], [], [], []
    for c in range(chunks):
        rows = pl.ds(c * CHUNK, CHUNK)
        lw = lw_ref[rows, :]
        cum = _mask_dot(ltri, lw)
        cum_end = cum[CHUNK - 1:CHUNK, :]
        e_inv = jnp.exp(-cum)
        e_end = jnp.exp(cum_end - cum)
        kka = kka_ref[rows, :]
        k = k_ref[rows, :]
        a_bar = (-kk_ref[rows, :] * jnp.exp(cum - lw)).astype(BF16)
        r_bar = (r_ref[rows, :] * jnp.exp(cum)).astype(BF16)
        b_til = (kka * e_inv).astype(BF16)
        k_til = (k * e_inv).astype(BF16)
        b_end = (kka * e_end).astype(BF16)
        k_end = (k * e_end).astype(BF16)
        gamma = jnp.exp(cum_end)
        v = v_ref[rows, :].astype(BF16)

        for s in sl:
            lhs.append(jnp.concatenate([a_bar[:, s], r_bar[:, s]], axis=0))
            rhs.append(jnp.where(m["stacked_diag"], jnp.concatenate(
                [b_til[:, s], b_til[:, s], k_til[:, s], k_til[:, s]], axis=0), zero))
            vp.append(v[:, s])
            ends.append(jnp.concatenate([b_end[:, s], k_end[:, s]], axis=0))
            gammas.append(gamma[:, s])

    blocks = each(_bdot_nt, lhs, rhs)
    a_ab = [jnp.where(m["strict"], b[:CHUNK, :PAIR_W], 0.0) for b in blocks]
    a_ak = [jnp.where(m["strict"], b[:CHUNK, PAIR_W:], 0.0) for b in blocks]
    r_ab = [jnp.where(m["incl"], b[CHUNK:, :PAIR_W], 0.0) for b in blocks]
    r_ak = [jnp.where(m["incl"], b[CHUNK:, PAIR_W:], 0.0) for b in blocks]
    from_v = [jnp.dot(jnp.concatenate([ak, rk], axis=0).astype(BF16), block_diag(x),
                      preferred_element_type=F32) for ak, rk, x in zip(a_ak, r_ak, vp)]
    d = [jnp.where(m["same_block"], a, 0.0) for a in a_ab]
    low = [a - di for a, di in zip(a_ab, d)]
    pm = [m["eye"] + di for di in d]
    dpow = [pair_mul(di, di)[0] for di in d]
    for _ in range(INV_BLOCK.bit_length() - 3):
        both = each(lambda x, p: pair_mul(x, x, p), dpow, pm)
        dpow = [b[0] for b in both]
        pm = [p + b[1] for p, b in zip(pm, both)]
    pm = [p + pair_mul(x, p)[0] for p, x in zip(pm, dpow)]
    both = each(lambda p, lo, fv, l: pair_mul(p, lo, fv[:CHUNK], l[:CHUNK]), pm, low, from_v, lhs)
    both2 = each(lambda b: pair_mul(b[0], b[0], b[1], b[2]), both)
    xw_v = [b[1] + b2[1] for b, b2 in zip(both, both2)]
    xw_a = [b[2] + b2[2] for b, b2 in zip(both, both2)]
    both3 = each(lambda b2, xv, xa: pair_mul(b2[0], xv, xa), both2, xw_v, xw_a)
    u_v = [xv + b3[0] for xv, b3 in zip(xw_v, both3)]
    wm = [xa + b3[1] for xa, b3 in zip(xw_a, both3)]
    both4 = each(lambda rb, uv, w: pair_mul(rb, uv, w), r_ab, u_v, wm)
    y_local = [fv[CHUNK:] + b4[0] for fv, b4 in zip(from_v, both4)]
    rm = [l[CHUNK:] + b4[1] for l, b4 in zip(lhs, both4)]
    gm = [jnp.where(m["block_diag"], _bdot_tn(en[:CHUNK], w), 0.0) for en, w in zip(ends, wm)]
    qm = [jnp.where(m["block_diag"], _bdot_tn(jnp.concatenate([uv.astype(BF16), x], axis=0), en), 0.0)
          for uv, x, en in zip(u_v, vp, ends)]

    state = [st_ref[p] for p in pairs]
    for c in range(chunks):
        rows = pl.ds(c * CHUNK, CHUNK)
        item = slice(c * N_PAIRS, (c + 1) * N_PAIRS)
        y = [yl + _bdot_nt(r, st) for yl, r, st in zip(y_local[item], rm[item], state)]
        state = [st * g + _bdot_nt(st, gmat) + q
                 for st, g, gmat, q in zip(state, gammas[item], gm[item], qm[item])]
        outs = []
        inv_n = 1.0 / RWKV_HEAD
        for yp in y:
            def head_stat(z):
                s0 = jnp.sum(jnp.where(m["first_head"], z, 0.0), axis=-1, keepdims=True)
                s1 = jnp.sum(jnp.where(m["first_head"], 0.0, z), axis=-1, keepdims=True)
                return jnp.where(m["first_head"], s0, s1) * inv_n
            yc = yp - head_stat(yp)
            outs.append(yc * lax.rsqrt(head_stat(yc * yc) + RWKV_LNX_EPS))
        yn = jnp.concatenate(outs, axis=1) * lg_ref[...] + lb_ref[...]
        o_ref[rows, :] = ((yn + bonus_ref[rows, :]) * g_ref[rows, :]).astype(o_ref.dtype)
    for p in pairs:
        st_ref[p] = state[p]


def _rwkv_scan(prep, batch, seq, lnx_g, lnx_b):
    t = batch * seq
    chunks = 8
    rows = CHUNK * chunks
    nblk = seq // rows
    spec = pl.BlockSpec((rows, RWKV_WIDTH), lambda b, n: (b * nblk + n, 0))
    pspec = pl.BlockSpec((1, RWKV_WIDTH), lambda b, n: (0, 0))
    return pl.pallas_call(
        functools.partial(_rwkv_scan_kernel, chunks=chunks),
        out_shape=jax.ShapeDtypeStruct((t, RWKV_WIDTH), BF16),
        grid=(batch, nblk),
        in_specs=[spec] * 8 + [pspec, pspec],
        out_specs=spec,
        scratch_shapes=[pltpu.VMEM((N_PAIRS, PAIR_W, PAIR_W), F32)],
        compiler_params=_cparams("parallel", "arbitrary"),
    )(*prep, lnx_g, lnx_b)


def _merge_kernel(og_ref, ol_ref, or_ref, x_ref, wgt_ref, bgt_ref, pg_ref, pl_ref, pr_ref, wo_ref,
                  lg_ref, lb_ref, xo_ref):
    d = D_MODEL
    xin = x_ref[...].astype(BF16)
    merged = None
    for b, (o_ref, p_ref) in enumerate(((og_ref, pg_ref), (ol_ref, pl_ref), (or_ref, pr_ref))):
        cols = slice(b * d, (b + 1) * d)
        gate = _sigmoid(jnp.dot(xin, wgt_ref[:, cols], preferred_element_type=F32) + bgt_ref[:, cols])
        term = gate * jnp.dot(o_ref[...], p_ref[...], preferred_element_type=F32)
        merged = term if merged is None else merged + term
    z = DEEPNORM_ALPHA * x_ref[...] + _bdot(merged, wo_ref[...])
    out = _layer_norm_rows(z, lg_ref[...], lb_ref[...], LN_EPS)
    xo_ref[...] = out


def _merge(o_gla, o_lru, o_rwkv, x, w_gates, b_gates, p_gla, p_lru, p_rwkv, w_out, ln_g, ln_b):
    t = x.shape[0]
    tm = min(512, t)

    def rows(width):
        return pl.BlockSpec((tm, width), lambda i: (i, 0))

    def full(a):
        return pl.BlockSpec(a.shape, lambda i: (0,) * a.ndim)

    params = (w_gates, b_gates, p_gla, p_lru, p_rwkv, w_out, ln_g, ln_b)
    return pl.pallas_call(
        _merge_kernel,
        out_shape=jax.ShapeDtypeStruct((t, D_MODEL), F32),
        grid=(t // tm,),
        in_specs=[rows(GLA_VW), rows(LRU_WIDTH), rows(RWKV_WIDTH), rows(D_MODEL)] + [full(a) for a in params],
        out_specs=rows(D_MODEL),
        compiler_params=_cparams("parallel"),
    )(o_gla, o_lru, o_rwkv, x, *params)


FFN_SPLIT = 2


def _ffn_kernel(x_ref, wg_ref, wu_ref, wd_ref, lg_ref, lb_ref, xo_ref):
    x = x_ref[...]
    xb = x.astype(BF16)
    tf = wg_ref.shape[1] // FFN_SPLIT
    acc = None
    for f in range(FFN_SPLIT):
        cols = slice(f * tf, (f + 1) * tf)
        g = jnp.dot(xb, wg_ref[:, cols], preferred_element_type=F32)
        u = jnp.dot(xb, wu_ref[:, cols], preferred_element_type=F32)
        part = _bdot(g * _sigmoid(g) * u, wd_ref[cols, :])
        acc = part if acc is None else acc + part
    z = DEEPNORM_ALPHA * x + acc
    xo_ref[...] = _layer_norm_rows(z, lg_ref[...], lb_ref[...], LN_EPS)


def _dense_ffn(x, wg, wu, wd, ln_g, ln_b):
    t = x.shape[0]
    tm = min(512, t)
    rows = pl.BlockSpec((tm, D_MODEL), lambda i: (i, 0))
    resident = lambda a: pl.BlockSpec(a.shape, lambda i: (0,) * a.ndim, pipeline_mode=pl.Buffered(1))
    return pl.pallas_call(
        _ffn_kernel,
        out_shape=jax.ShapeDtypeStruct((t, D_MODEL), F32),
        grid=(t // tm,),
        in_specs=[rows] + [resident(a) for a in (wg, wu, wd, ln_g, ln_b)],
        out_specs=rows,
        compiler_params=_cparams("parallel"),
    )(x, wg, wu, wd, ln_g, ln_b)


def _router_kernel(x_ref, wr_ref, meta_ref, wts_ref, cnt_ref, carry_ref):
    tm = x_ref.shape[0]

    @pl.when(pl.program_id(0) == 0)
    def _():
        carry_ref[...] = jnp.zeros_like(carry_ref)

    lane = lax.broadcasted_iota(I32, (tm, LANES), 1)
    neg = jnp.float32(-jnp.inf)
    logits = jnp.where(lane < N_EXPERTS, _fdot(x_ref[...], wr_ref[...]), neg)
    m1 = jnp.max(logits, axis=-1, keepdims=True)
    e1 = jnp.min(jnp.where(logits == m1, lane, LANES), axis=-1, keepdims=True)
    rest = jnp.where(lane == e1, neg, logits)
    m2 = jnp.max(rest, axis=-1, keepdims=True)
    e2 = jnp.min(jnp.where(rest == m2, lane, LANES), axis=-1, keepdims=True)
    ex = jnp.exp(m2 - m1)
    w1 = 1.0 / (1.0 + ex)
    w2 = ex / (1.0 + ex)

    hot1 = lane == e1
    hot2 = lane == e2
    onehot = jnp.where(hot1 | hot2, 1.0, 0.0)
    row, col = _tri_masks(tm)
    before = (row > col).astype(BF16)
    prefix = jnp.dot(before, onehot.astype(BF16), preferred_element_type=F32) + carry_ref[...]
    rank1 = jnp.sum(jnp.where(hot1, prefix, 0.0), axis=-1, keepdims=True).astype(I32)
    rank2 = jnp.sum(jnp.where(hot2, prefix, 0.0), axis=-1, keepdims=True).astype(I32)
    carry_ref[...] += jnp.sum(onehot, axis=0, keepdims=True)
    cnt_ref[...] = carry_ref[...]

    meta = jnp.where(lane == 0, e1, jnp.where(lane == 1, e2, jnp.where(lane == 2, rank1, rank2)))
    meta_ref[...] = jnp.where(lane < 4, meta, 0)
    wts_ref[...] = jnp.where(lane == 0, w1, jnp.where(lane == 1, w2, 0.0))


def _router(x, wr_pad):
    t = x.shape[0]
    tm = min(512, t)
    return pl.pallas_call(
        _router_kernel,
        out_shape=[jax.ShapeDtypeStruct((t, LANES), I32), jax.ShapeDtypeStruct((t, LANES), F32),
                   jax.ShapeDtypeStruct((1, LANES), F32)],
        grid=(t // tm,),
        in_specs=[pl.BlockSpec((tm, D_MODEL), lambda i: (i, 0)),
                  pl.BlockSpec((D_MODEL, LANES), lambda i: (0, 0))],
        out_specs=[pl.BlockSpec((tm, LANES), lambda i: (i, 0)), pl.BlockSpec((tm, LANES), lambda i: (i, 0)),
                   pl.BlockSpec((1, LANES), lambda i: (0, 0))],
        scratch_shapes=[pltpu.VMEM((1, LANES), F32)],
        compiler_params=_cparams("arbitrary"),
    )(x, wr_pad)


def _dispatch_kernel(pad_start_ref, pad_len_ref, nused_ref, dest_ref, x_ref, xs_ref, zeros_ref, sem, fill_sem):
    tm = x_ref.shape[0]
    n_blocks = xs_ref.shape[0] // MOE_ROWS

    @pl.when(pl.program_id(0) == pl.num_programs(0) - 1)
    def _():
        zeros_ref[...] = jnp.zeros_like(zeros_ref)
        pad_copy = lambda row: pltpu.make_async_copy(zeros_ref.at[pl.ds(0, 1)], xs_ref.at[pl.ds(row, 1)], fill_sem)
        blk_copy = lambda blk: pltpu.make_async_copy(
            zeros_ref, xs_ref.at[pl.ds(pl.multiple_of(blk * MOE_ROWS, MOE_ROWS), MOE_ROWS)], fill_sem)

        def for_each(fn):
            for e in range(N_EXPERTS):
                base = pad_start_ref[e]
                lax.fori_loop(0, pad_len_ref[e], lambda r, c: (fn(pad_copy(base + r)), c)[1], 0)
            lax.fori_loop(nused_ref[0], n_blocks, lambda b, c: (fn(blk_copy(b)), c)[1], 0)

        for_each(lambda cp: cp.start())
        for_each(lambda cp: cp.wait())

    def row_copy(r, j):
        return pltpu.make_async_copy(x_ref.at[pl.ds(r, 1)], xs_ref.at[pl.ds(dest_ref[TOP_K * r + j], 1)], sem)

    def start(r, c):
        for j in range(TOP_K):
            row_copy(r, j).start(priority=j)
        return c

    lax.fori_loop(0, tm, start, 0, unroll=8)
    for j in range(TOP_K):
        pltpu.make_async_copy(x_ref, xs_ref.at[pl.ds(0, tm)], sem).wait()


def _dispatch(x, dest_flat, pad_start, pad_len, n_used, n_slots):
    t = x.shape[0]
    tm = min(256, t)
    grid_spec = pltpu.PrefetchScalarGridSpec(
        num_scalar_prefetch=3,
        grid=(t // tm,),
        in_specs=[pl.BlockSpec((tm * TOP_K,), lambda i, *_: (i,), memory_space=pltpu.SMEM),
                  pl.BlockSpec((tm, D_MODEL), lambda i, *_: (i, 0))],
        out_specs=pl.BlockSpec(memory_space=pl.ANY),
        scratch_shapes=[pltpu.VMEM((MOE_ROWS, D_MODEL), F32), pltpu.SemaphoreType.DMA(()),
                        pltpu.SemaphoreType.DMA(())],
    )
    return pl.pallas_call(
        _dispatch_kernel,
        out_shape=jax.ShapeDtypeStruct((n_slots, D_MODEL), F32),
        grid_spec=grid_spec,
        compiler_params=_cparams("arbitrary"),
    )(pad_start, pad_len, n_used, dest_flat, x)


def _expert_kernel(blk_e_ref, nused_ref, xs_ref, wg_ref, wu_ref, wd_ref, ys_ref, acc_ref):
    del blk_e_ref
    i = pl.program_id(0)
    f = pl.program_id(1)
    last = pl.num_programs(1) - 1
    used = i < nused_ref[0]

    @pl.when(used & (f == 0))
    def _():
        acc_ref[...] = jnp.zeros_like(acc_ref)

    @pl.when(used)
    def _():
        xb = xs_ref[...].astype(BF16)
        g = jnp.dot(xb, wg_ref[0], preferred_element_type=F32)
        u = jnp.dot(xb, wu_ref[0], preferred_element_type=F32)
        acc_ref[...] += _bdot(g * _sigmoid(g) * u, wd_ref[0])

    @pl.when(used & (f == last))
    def _():
        ys_ref[...] = acc_ref[...]

    @pl.when(jnp.logical_not(used) & (f == last))
    def _():
        ys_ref[...] = jnp.zeros_like(ys_ref)


def _experts(xs, blk_e, n_used, wg, wu, wd):
    n_slots = xs.shape[0]
    n_blocks = n_slots // MOE_ROWS
    ff = wg.shape[2]
    n_f = 2
    tf = ff // n_f
    piece = lambda i, f: jnp.where((i & 1) == 1, n_f - 1 - f, f)
    grid_spec = pltpu.PrefetchScalarGridSpec(
        num_scalar_prefetch=2,
        grid=(n_blocks, n_f),
        in_specs=[pl.BlockSpec((MOE_ROWS, D_MODEL), lambda i, f, be, nu: (i, 0)),
                  pl.BlockSpec((1, D_MODEL, tf), lambda i, f, be, nu: (be[i], 0, piece(i, f))),
                  pl.BlockSpec((1, D_MODEL, tf), lambda i, f, be, nu: (be[i], 0, piece(i, f))),
                  pl.BlockSpec((1, tf, D_MODEL), lambda i, f, be, nu: (be[i], piece(i, f), 0))],
        out_specs=pl.BlockSpec((MOE_ROWS, D_MODEL), lambda i, f, be, nu: (i, 0)),
        scratch_shapes=[pltpu.VMEM((MOE_ROWS, D_MODEL), F32)],
    )
    return pl.pallas_call(
        _expert_kernel,
        out_shape=jax.ShapeDtypeStruct((n_slots, D_MODEL), F32),
        grid_spec=grid_spec,
        compiler_params=_cparams("arbitrary", "arbitrary"),
    )(blk_e, n_used, xs, wg, wu, wd)


def _combine_kernel(dest_ref, dest_next_ref, x_ref, wts_ref, ys_ref, lg_ref, lb_ref, xo_ref, buf_ref, sem):
    tm = x_ref.shape[0]
    i = pl.program_id(0)
    n = pl.num_programs(0)

    def gather(idx_ref, slot):
        def start(r, c):
            for j in range(TOP_K):
                pltpu.make_async_copy(ys_ref.at[pl.ds(idx_ref[TOP_K * r + j], 1)],
                                      buf_ref.at[slot, j, pl.ds(r, 1)], sem.at[slot]).start(priority=j)
            return c
        lax.fori_loop(0, tm, start, 0, unroll=8)

    slot = i & 1

    @pl.when(i == 0)
    def _():
        gather(dest_ref, 0)

    @pl.when(i + 1 < n)
    def _():
        gather(dest_next_ref, 1 - slot)

    for j in range(TOP_K):
        pltpu.make_async_copy(ys_ref.at[pl.ds(0, tm)], buf_ref.at[slot, j], sem.at[slot]).wait()
    w = wts_ref[...]
    f = w[:, 0:1] * buf_ref[slot, 0] + w[:, 1:2] * buf_ref[slot, 1]
    z = DEEPNORM_ALPHA * x_ref[...] + f
    xo_ref[...] = _layer_norm_rows(z, lg_ref[...], lb_ref[...], LN_EPS)


def _combine(x, dest_flat, wts, ys, ln_g, ln_b):
    t = x.shape[0]
    tm = min(256, t)
    n_tiles = t // tm
    rows = lambda width: pl.BlockSpec((tm, width), lambda i: (i, 0))
    vec = pl.BlockSpec((1, D_MODEL), lambda i: (0, 0))
    return pl.pallas_call(
        _combine_kernel,
        out_shape=jax.ShapeDtypeStruct((t, D_MODEL), F32),
        grid=(n_tiles,),
        in_specs=[pl.BlockSpec((tm * TOP_K,), lambda i: (i,), memory_space=pltpu.SMEM),
                  pl.BlockSpec((tm * TOP_K,), lambda i: (jnp.minimum(i + 1, n_tiles - 1),),
                               memory_space=pltpu.SMEM),
                  rows(D_MODEL), rows(LANES), pl.BlockSpec(memory_space=pl.ANY), vec, vec],
        out_specs=rows(D_MODEL),
        scratch_shapes=[pltpu.VMEM((2, TOP_K, tm, D_MODEL), F32), pltpu.SemaphoreType.DMA((2,))],
        compiler_params=_cparams("arbitrary"),
    )(dest_flat, dest_flat, x, wts, ys, ln_g, ln_b)


def _moe_ffn(x, w_router, wg, wu, wd, ln_g, ln_b):
    t = x.shape[0]
    wr_pad = jnp.pad(w_router, ((0, 0), (0, LANES - N_EXPERTS)))
    meta, wts, cnt = _router(x, wr_pad)
    counts = cnt[0, :N_EXPERTS].astype(I32)
    padded = (counts + MOE_ROWS - 1) // MOE_ROWS * MOE_ROWS
    seg_end = jnp.cumsum(padded)
    seg_start = seg_end - padded
    n_blocks = (t * TOP_K) // MOE_ROWS + N_EXPERTS
    dest = seg_start[meta[:, 0:TOP_K]] + meta[:, TOP_K:2 * TOP_K]
    dest_flat = dest.reshape(-1).astype(I32)
    blk_start = jnp.arange(n_blocks, dtype=I32) * MOE_ROWS
    blk_e = jnp.minimum(jnp.sum(blk_start[:, None] >= seg_end[None, :], axis=1), N_EXPERTS - 1).astype(I32)
    n_used = (seg_end[-1:] // MOE_ROWS).astype(I32)
    xs = _dispatch(x, dest_flat, (seg_start + counts).astype(I32), (padded - counts).astype(I32), n_used,
                   n_blocks * MOE_ROWS)
    ys = _experts(xs, blk_e, n_used, wg, wu, wd)
    return _combine(x, dest_flat, wts, ys, ln_g, ln_b)


def _pad_rows(w, rows, at=0):
    out = jnp.zeros((rows, w.shape[1]), w.dtype)
    return out.at[at:at + w.shape[0]].set(w)


def _reorder_in_projection(w, b):
    sizes = (GLA_KW, GLA_KW, GLA_VW, GLA_VW, GLA_DECAY_RANK, LRU_WIDTH, LRU_WIDTH,
             3 * RWKV_WIDTH, RWKV_DECAY_RANK, RWKV_A_RANK, RWKV_GATE_RANK, N_BRANCH * D_MODEL)
    offs = [0]
    for s in sizes:
        offs.append(offs[-1] + s)
    wb = jnp.concatenate([w, b[None, :]], axis=0)
    piece = lambda i: wb[:, offs[i]:offs[i + 1]]
    zeros = lambda n: jnp.zeros((wb.shape[0], n), wb.dtype)
    q, k, v, r, dec, lx, lg, rkv, wl, al, gl, gates = (piece(i) for i in range(len(sizes)))
    small = jnp.concatenate([wl, al, gl, zeros(RW_SMALL_W - SM_GL[0] - RWKV_GATE_RANK)], axis=1)
    groups = {"gla": jnp.concatenate([q, k, v, r, dec, zeros(GLA_DEC_W - GLA_DECAY_RANK)], axis=1),
              "lru": jnp.concatenate([lx, lg], axis=1),
              "rwkv": jnp.concatenate([rkv, small], axis=1),
              "gates": gates}
    return {name: (g[:-1].astype(BF16), g[-1:]) for name, g in groups.items()}


def _cast_kernel(x_ref, o_ref):
    o_ref[...] = x_ref[...].astype(o_ref.dtype)


CAST_TILE_ELEMS = 2 * 1024 * 1024


def _to_bf16(w):
    cols = w.shape[-1]
    w2 = w.reshape(-1, cols)
    n_rows = w2.shape[0]
    tr = n_rows
    while tr * cols > CAST_TILE_ELEMS and tr % 32 == 0:
        tr //= 2
    out = pl.pallas_call(
        _cast_kernel,
        out_shape=jax.ShapeDtypeStruct(w2.shape, BF16),
        grid=(n_rows // tr,),
        in_specs=[pl.BlockSpec((tr, cols), lambda i: (i, 0))],
        out_specs=pl.BlockSpec((tr, cols), lambda i: (i, 0)),
        compiler_params=_cparams("parallel"),
    )(w2)
    return out.reshape(w.shape)


def _block_diag(blocks):
    n, bi, bo = blocks.shape
    eye = jnp.eye(n, dtype=blocks.dtype)
    return (eye[:, None, :, None] * blocks[:, :, None, :]).reshape(n * bi, n * bo)


def kernel(x, w_in, b_in, gla_w_decay_up, gla_b_decay, gla_norm_g, gla_norm_b, lru_conv_w, lru_conv_b, lru_w_r, lru_b_r, lru_w_i, lru_b_i, lru_lambda, rwkv_mu, rwkv_w0, rwkv_w2, rwkv_a0, rwkv_a2, rwkv_g2, rwkv_k_k, rwkv_k_a, rwkv_r_k, rwkv_lnx_g, rwkv_lnx_b, p_gla, p_lru, p_rwkv, w_out, ln_mix_g, ln_mix_b, ffn_w_gate, ffn_w_up, ffn_w_down, moe_w_router, moe_w_gate, moe_w_up, moe_w_down, ln_ffn_g, ln_ffn_b):
    batch, seq, d = x.shape
    t = batch * seq
    xf = x.reshape(t, d)
    row = lambda a: a.reshape(1, -1)
    head_ones = _block_diag(jnp.ones((RWKV_HEADS, RWKV_HEAD, RWKV_HEAD), BF16))
    for l in range(DEPTH):
        proj = _reorder_in_projection(w_in[l], b_in[l])
        w_gates, b_gates = proj["gates"]

        wup = _pad_rows(gla_w_decay_up[l], GLA_DEC_W).astype(BF16)
        o_gla = _gla_branch(xf, batch, seq, *proj["gla"], wup, row(gla_b_decay[l]), row(gla_norm_g[l]),
                            row(gla_norm_b[l]))

        w_ri = jnp.concatenate([_block_diag(lru_w_r[l]), _block_diag(lru_w_i[l])], axis=1).astype(BF16)
        b_ri = jnp.concatenate([lru_b_r[l], lru_b_i[l]])[None, :]
        o_lru = _lru_branch(xf, batch, seq, *proj["lru"], lru_conv_w[l], row(lru_conv_b[l]), w_ri, b_ri,
                            row(jax.nn.softplus(-lru_lambda[l])))

        mu = rwkv_mu[l]
        mu_all = jnp.zeros((3 * RWKV_WIDTH + RW_SMALL_W,), F32).at[0:mu.shape[0]].set(mu)
        prep_params = (*proj["rwkv"], row(mu_all),
                       _pad_rows(rwkv_w2[l], SM_WA[1], 0).astype(BF16),
                       _pad_rows(rwkv_a2[l], SM_WA[1], RWKV_DECAY_RANK).astype(BF16),
                       _pad_rows(rwkv_g2[l], SM_GL[1], 0).astype(BF16),
                       row(rwkv_w0[l]), row(rwkv_a0[l]), row(rwkv_k_k[l]), row(rwkv_k_a[l]),
                       row(rwkv_r_k[l]), head_ones)
        prep = _rwkv_prep(xf, batch, seq, prep_params)
        o_rwkv = _rwkv_scan(prep, batch, seq, row(rwkv_lnx_g[l]), row(rwkv_lnx_b[l]))

        xf = _merge(o_gla, o_lru, o_rwkv, xf, w_gates, b_gates, p_gla[l].astype(BF16), p_lru[l].astype(BF16),
                    p_rwkv[l].astype(BF16), w_out[l].astype(BF16), row(ln_mix_g[l]), row(ln_mix_b[l]))
        i = l // 2
        if l % 2 == 0:
            xf = _dense_ffn(xf, ffn_w_gate[i].astype(BF16), ffn_w_up[i].astype(BF16),
                            ffn_w_down[i].astype(BF16), row(ln_ffn_g[l]), row(ln_ffn_b[l]))
        else:
            xf = _moe_ffn(xf, moe_w_router[i], _to_bf16(moe_w_gate[i]), _to_bf16(moe_w_up[i]),
                          _to_bf16(moe_w_down[i]), row(ln_ffn_g[l]), row(ln_ffn_b[l]))
    return xf.reshape(batch, seq, d)
```

```python
import functools

import jax
import jax.numpy as jnp
from jax import lax
from jax.experimental import pallas as pl
from jax.experimental.pallas import tpu as pltpu

F32 = jnp.float32
BF16 = jnp.bfloat16
I32 = jnp.int32

D_MODEL = 1024
DEPTH = 2
GLA_HEADS, GLA_DK, GLA_DV = 4, 64, 128
GLA_KW, GLA_VW = GLA_HEADS * GLA_DK, GLA_HEADS * GLA_DV
GLA_DECAY_RANK = 16
GLA_TEMP = 16.0
LRU_WIDTH, LRU_BLOCKS, LRU_CONV, LRU_C = 512, 8, 4, 8.0
RWKV_HEAD, RWKV_WIDTH = 64, 512
RWKV_HEADS = RWKV_WIDTH // RWKV_HEAD
RWKV_DECAY_RANK, RWKV_A_RANK, RWKV_GATE_RANK = 64, 64, 160
RWKV_LNX_EPS = 64e-5
N_BRANCH = 3
N_EXPERTS, TOP_K = 8, 2
DEEPNORM_ALPHA = (2 * DEPTH) ** 0.25
LN_EPS = 1e-5

LANES = 128
SUBLANES = 8
VMEM_LIMIT_BYTES = 56 * 1024 * 1024

SM_WA = (0, 128)
SM_GL = (128, 256)
RW_SMALL_W = 512
GLA_DEC_W = LANES
GLA_COLS = {"q": (0, 256), "k": (256, 512), "v": (512, 1024), "r": (1024, 1536), "dec": (1536, 1664)}

CHUNK = 64
INV_BLOCK = 16
MOE_ROWS = 512


def _cparams(*sem):
    return pltpu.CompilerParams(dimension_semantics=sem, vmem_limit_bytes=VMEM_LIMIT_BYTES)


def _sigmoid(x):
    return 1.0 / (1.0 + jnp.exp(-x))


def _softplus(x):
    return jnp.maximum(x, 0.0) + jnp.log(1.0 + jnp.exp(-jnp.abs(x)))


def _bdot(a, b):
    return jnp.dot(a.astype(BF16), b.astype(BF16), preferred_element_type=F32)


def _bdot_nt(a, b):
    return lax.dot_general(a.astype(BF16), b.astype(BF16), (((1,), (1,)), ((), ())),
                           preferred_element_type=F32)


def _bdot_tn(a, b):
    return lax.dot_general(a.astype(BF16), b.astype(BF16), (((0,), (0,)), ((), ())),
                           preferred_element_type=F32)


def _fdot(a, b):
    ah = a.astype(BF16)
    al = (a - ah.astype(F32)).astype(BF16)
    bh = b.astype(BF16)
    bl = (b - bh.astype(F32)).astype(BF16)
    dot = lambda u, v: jnp.dot(u, v, preferred_element_type=F32)
    return dot(al, bh) + dot(ah, bl) + dot(ah, bh)


def _split3(x):
    hi = x.astype(BF16)
    rest = x - hi.astype(F32)
    mid = rest.astype(BF16)
    lo = (rest - mid.astype(F32)).astype(BF16)
    return hi, mid, lo


def _mask_dot(mask_bf16, x):
    hi, mid, lo = _split3(x)
    dot = lambda part: jnp.dot(mask_bf16, part, preferred_element_type=F32)
    return dot(lo) + dot(mid) + dot(hi)


def _dot_mask(x, mask_bf16):
    hi, mid, lo = _split3(x)
    dot = lambda part: jnp.dot(part, mask_bf16, preferred_element_type=F32)
    return dot(lo) + dot(mid) + dot(hi)


def _layer_norm_rows(z, g, b, eps):
    mu = jnp.mean(z, axis=-1, keepdims=True)
    zc = z - mu
    var = jnp.mean(zc * zc, axis=-1, keepdims=True)
    return zc * lax.rsqrt(var + eps) * g + b


def _tri_masks(n):
    row = lax.broadcasted_iota(I32, (n, n), 0)
    col = lax.broadcasted_iota(I32, (n, n), 1)
    return row, col


MXU_COLS = 256


def _gla_kernel(*refs, chunks, blocks_per_seq):
    *io_refs, st_ref, ha_ref, hb_ref = refs
    s = pl.program_id(0)

    @pl.when(s == 0)
    def _():
        hb_ref[...] = jnp.zeros_like(hb_ref)

    @pl.when((s == 0) | (((s + blocks_per_seq - 1) % blocks_per_seq) == 0))
    def _():
        st_ref[...] = jnp.zeros_like(st_ref)

    body = functools.partial(_gla_body, chunks, (*io_refs, st_ref))

    @pl.when((s & 1) == 0)
    def _():
        body(ha_ref, hb_ref)

    @pl.when((s & 1) == 1)
    def _():
        body(hb_ref, ha_ref)


def _gla_body(chunks, refs, h_next, h_ref):
    x_ref, w_ref, b_ref, wup_ref, bdec_ref, ng_ref, nb_ref, o_ref, st_ref = refs
    xb = x_ref[...].astype(BF16)
    width = w_ref.shape[1]
    pieces = iter([slice(c0, min(c0 + MXU_COLS, width)) for c0 in range(0, width, MXU_COLS)])

    def project_piece():
        cols = next(pieces, None)
        if cols is not None:
            h_next[:, cols] = jnp.dot(xb, w_ref[:, cols], preferred_element_type=F32) + b_ref[:, cols]

    q_ref = h_ref.at[:, GLA_COLS["q"][0]:GLA_COLS["q"][1]]
    k_ref = h_ref.at[:, GLA_COLS["k"][0]:GLA_COLS["k"][1]]
    v_ref = h_ref.at[:, GLA_COLS["v"][0]:GLA_COLS["v"][1]]
    r_ref = h_ref.at[:, GLA_COLS["r"][0]:GLA_COLS["r"][1]]
    dec_ref = h_ref.at[:, GLA_COLS["dec"][0]:GLA_COLS["dec"][1]]

    row, col = _tri_masks(CHUNK)
    ltri = (row >= col).astype(BF16)
    kw, vw = 2 * GLA_DK, 2 * GLA_DV
    key_shift, val_shift = GLA_DK.bit_length() - 1, GLA_DV.bit_length() - 1
    iota = lambda shape, axis: lax.broadcasted_iota(I32, shape, axis)
    causal = iota((CHUNK, kw), 0) >= (iota((CHUNK, kw), 1) & (GLA_DK - 1))
    key_diag = (iota((kw, kw), 0) >> key_shift) == (iota((kw, kw), 1) >> key_shift)
    val_diag = (iota((kw, vw), 0) >> key_shift) == (iota((kw, vw), 1) >> val_shift)
    state_diag = (iota((vw, kw), 0) >> val_shift) == (iota((vw, kw), 1) >> key_shift)
    zero = jnp.zeros((), BF16)
    pairs = range(GLA_HEADS // 2)
    qp, vp, k_diag, kep, decays = [], [], [], [], []
    for c in range(chunks):
        project_piece()
        rows = pl.ds(c * CHUNK, CHUNK)
        logits = _bdot(dec_ref[rows, :], wup_ref[...]) + bdec_ref[...]
        log_a = -_softplus(-logits) * (1.0 / GLA_TEMP)
        cum = _mask_dot(ltri, log_a)
        cum_end = cum[CHUNK - 1:CHUNK, :]
        q = q_ref[rows, :] * (GLA_DK ** -0.5)
        k = k_ref[rows, :]
        q_dec = (q * jnp.exp(cum)).astype(BF16)
        k_inv = (k * jnp.exp(-cum)).astype(BF16)
        k_end = (k * jnp.exp(cum_end - cum)).astype(BF16)
        decay_end = jnp.exp(cum_end)
        v = v_ref[rows, :].astype(BF16)
        for p in pairs:
            s = slice(p * kw, (p + 1) * kw)
            qp.append(q_dec[:, s])
            vp.append(v[:, p * vw:(p + 1) * vw])
            k_diag.append(jnp.where(key_diag, jnp.concatenate([k_inv[:, s], k_inv[:, s]], axis=0), zero))
            kep.append(k_end[:, s])
            decays.append(decay_end[:, s])

    for _ in range(chunks):
        project_piece()

    v_diag = [jnp.where(val_diag, jnp.concatenate([x, x], axis=0), zero) for x in vp]
    scores = [jnp.where(causal, _bdot_nt(q, kd), 0.0) for q, kd in zip(qp, k_diag)]
    o_local = [_bdot(s, vd) for s, vd in zip(scores, v_diag)]
    upd = [jnp.where(state_diag, _bdot_tn(x, ke), 0.0) for x, ke in zip(vp, kep)]

    state = [st_ref[p] for p in pairs]
    n_pairs = len(pairs)
    for c in range(chunks):
        rows = pl.ds(c * CHUNK, CHUNK)
        item = slice(c * n_pairs, (c + 1) * n_pairs)
        o = [ol + _bdot_nt(q, st) for ol, q, st in zip(o_local[item], qp[item], state)]
        state = [st * dcy + u for st, dcy, u in zip(state, decays[item], upd[item])]
        outs = []
        for op in o:
            for oh in (op[:, :GLA_DV], op[:, GLA_DV:]):
                mu = jnp.mean(oh, axis=-1, keepdims=True)
                oc = oh - mu
                var = jnp.mean(oc * oc, axis=-1, keepdims=True)
                outs.append(oc * lax.rsqrt(var + LN_EPS))
        y = jnp.concatenate(outs, axis=1) * ng_ref[...] + nb_ref[...]
        r = r_ref[rows, :]
        o_ref[rows, :] = (y * (r * _sigmoid(r))).astype(o_ref.dtype)
    for p in pairs:
        st_ref[p] = state[p]


def _gla_branch(x, batch, seq, w, bias, wup_pad, b_decay, norm_g, norm_b):
    t = batch * seq
    chunks = 8
    rows = CHUNK * chunks
    nblk = seq // rows

    n_tiles = batch * nblk

    def full(a):
        return pl.BlockSpec(a.shape, lambda s: (0,) * a.ndim)

    args = (w, bias, wup_pad, b_decay, norm_g, norm_b)
    return pl.pallas_call(
        functools.partial(_gla_kernel, chunks=chunks, blocks_per_seq=nblk),
        out_shape=jax.ShapeDtypeStruct((t, GLA_VW), BF16),
        grid=(n_tiles + 1,),
        in_specs=[pl.BlockSpec((rows, D_MODEL), lambda s: (jnp.minimum(s, n_tiles - 1), 0))]
        + [full(a) for a in args],
        out_specs=pl.BlockSpec((rows, GLA_VW), lambda s: (jnp.maximum(s - 1, 0), 0)),
        scratch_shapes=[pltpu.VMEM((GLA_HEADS // 2, 2 * GLA_DV, 2 * GLA_DK), F32),
                        pltpu.VMEM((rows, w.shape[1]), F32), pltpu.VMEM((rows, w.shape[1]), F32)],
        compiler_params=_cparams("arbitrary"),
    )(x, *args)


def _lru_kernel(xin_ref, w_ref, b_ref, cw_ref, cb_ref, wri_ref, bri_ref, sp_ref,
                o_ref, xbuf_ref, a_ref, u_ref, hs_ref, gate_ref, h_ref):
    tm = xin_ref.shape[0]
    pad = SUBLANES

    @pl.when(pl.program_id(1) == 0)
    def _():
        xbuf_ref[0:pad, :] = jnp.zeros((pad, LRU_WIDTH), F32)
        h_ref[...] = jnp.zeros_like(h_ref)

    proj = jnp.dot(xin_ref[...].astype(BF16), w_ref[...], preferred_element_type=F32) + b_ref[...]
    x = proj[:, :LRU_WIDTH]
    gate_ref[...] = jax.nn.gelu(proj[:, LRU_WIDTH:])
    xbuf_ref[pad:pad + tm, :] = x
    xc = cb_ref[...] + x * cw_ref[LRU_CONV - 1:LRU_CONV, :]
    for j in range(LRU_CONV - 1):
        back = LRU_CONV - 1 - j
        xc = xc + xbuf_ref[pl.ds(pad - back, tm), :] * cw_ref[j:j + 1, :]
    xbuf_ref[0:pad, :] = x[tm - pad:tm, :]

    ri = _sigmoid(_bdot(xc, wri_ref[...]) + bri_ref[...])
    rg, ig = ri[:, :LRU_WIDTH], ri[:, LRU_WIDTH:]
    log_a = -LRU_C * rg * sp_ref[...]
    a = jnp.exp(log_a)
    a_ref[...] = a
    u_ref[...] = jnp.sqrt(1.0 - a * a) * (ig * xc)

    def step(t, hprev):
        hnew = a_ref[pl.ds(t, 1), :] * hprev + u_ref[pl.ds(t, 1), :]
        hs_ref[pl.ds(t, 1), :] = hnew
        return hnew

    h_ref[...] = lax.fori_loop(0, tm, step, h_ref[...], unroll=8)
    o_ref[...] = (gate_ref[...] * hs_ref[...]).astype(o_ref.dtype)


def _lru_branch(x, batch, seq, w, bias, conv_w, conv_b, w_ri, b_ri, softplus_neg_lam):
    t = batch * seq
    tm = min(512, seq)
    nblk = seq // tm

    def full(a):
        return pl.BlockSpec(a.shape, lambda b, n: (0,) * a.ndim)

    args = (w, bias, conv_w, conv_b, w_ri, b_ri, softplus_neg_lam)
    return pl.pallas_call(
        _lru_kernel,
        out_shape=jax.ShapeDtypeStruct((t, LRU_WIDTH), BF16),
        grid=(batch, nblk),
        in_specs=[pl.BlockSpec((tm, D_MODEL), lambda b, n: (b * nblk + n, 0))] + [full(a) for a in args],
        out_specs=pl.BlockSpec((tm, LRU_WIDTH), lambda b, n: (b * nblk + n, 0)),
        scratch_shapes=[pltpu.VMEM((tm + SUBLANES, LRU_WIDTH), F32)]
        + [pltpu.VMEM((tm, LRU_WIDTH), F32)] * 4 + [pltpu.VMEM((1, LRU_WIDTH), F32)],
        compiler_params=_cparams("parallel", "arbitrary"),
    )(x, *args)


def _rwkv_prep_kernel(*refs, blocks_per_seq):
    *io_refs, ha_ref, hb_ref, carry_ref = refs
    tm = io_refs[0].shape[0]
    s = pl.program_id(0)
    body = functools.partial(_rwkv_prep_body, tm, s, RWKV_WIDTH, blocks_per_seq, (*io_refs, carry_ref))

    @pl.when(s == 0)
    def _():
        hb_ref[...] = jnp.zeros_like(hb_ref)
        carry_ref[...] = jnp.zeros_like(carry_ref)

    @pl.when((s & 1) == 0)
    def _():
        body(ha_ref, hb_ref)

    @pl.when((s & 1) == 1)
    def _():
        body(hb_ref, ha_ref)


def _rwkv_prep_body(tm, s, gw, blocks_per_seq, refs, h_next, h_cur):
    (x_ref, w_ref, b_ref, mu_ref, w2_ref, a2_ref, g2_ref, w0_ref, a0_ref, kk_ref, ka_ref, rk_ref, bd_ref,
     r_out, k_out, v_out, lw_out, kk_out, kka_out, bonus_out, g_out, carry_ref) = refs
    first = ((s + blocks_per_seq - 1) % blocks_per_seq) == 0
    keep = jnp.where(first, 0.0, 1.0)
    row0 = lax.broadcasted_iota(I32, (tm, 1), 0) == 0
    xb = x_ref[...].astype(BF16)

    pw = gw // 2

    def project(j):
        cols = slice(j * pw, (j + 1) * pw)
        h_next[:, cols] = jnp.dot(xb, w_ref[:, cols], preferred_element_type=F32) + b_ref[:, cols]

    def shifted(j):
        cols = slice(j * gw, (j + 1) * gw)
        cur = h_cur[:, cols]
        prev_row = carry_ref[0:1, cols] * keep
        carry_ref[0:1, cols] = cur[tm - 1:tm, :]
        prev = jnp.where(row0, prev_row, pltpu.roll(cur, 1, axis=0))
        return cur + (prev - cur) * mu_ref[:, cols]

    project(0)
    sm = shifted(3)
    wa = sm[:, SM_WA[0]:SM_WA[0] + SM_WA[1]]
    gl = sm[:, SM_GL[0]:SM_GL[0] + SM_GL[1]]
    project(1)
    w_log = -_softplus(-(w0_ref[...] + _bdot(jnp.tanh(wa), w2_ref[...]))) - 0.5
    lw_out[...] = -jnp.exp(w_log)
    project(2)
    a = _sigmoid(a0_ref[...] + _bdot(wa, a2_ref[...]))
    g_out[...] = _bdot(_sigmoid(gl), g2_ref[...])
    project(3)
    k = shifted(1)
    kk = k * kk_ref[...]
    norm = jnp.sqrt(_dot_mask(kk * kk, bd_ref[...]))
    project(4)
    kk = kk / jnp.maximum(norm, 1e-12)
    kk_out[...] = kk
    kka_out[...] = kk * a
    k2 = k * (1.0 + (a - 1.0) * ka_ref[...])
    k_out[...] = k2
    project(5)
    r = shifted(0)
    r_out[...] = r
    project(6)
    v = shifted(2)
    v_out[...] = v.astype(v_out.dtype)
    project(7)
    bonus_out[...] = _dot_mask(r * k2 * rk_ref[...], bd_ref[...]) * v


def _rwkv_prep(x, batch, seq, params):
    t = batch * seq
    tm = min(512, seq)
    bps = seq // tm
    n_tiles = t // tm
    width = params[0].shape[1]

    def full(a):
        return pl.BlockSpec(a.shape, lambda s: (0,) * a.ndim)

    out = lambda dtype: jax.ShapeDtypeStruct((t, RWKV_WIDTH), dtype)
    ospec = pl.BlockSpec((tm, RWKV_WIDTH), lambda s: (jnp.maximum(s - 1, 0), 0))
    out_dtypes = (F32, F32, BF16, F32, F32, F32, F32, F32)
    return pl.pallas_call(
        functools.partial(_rwkv_prep_kernel, blocks_per_seq=bps),
        out_shape=[out(dt) for dt in out_dtypes],
        grid=(n_tiles + 1,),
        in_specs=[pl.BlockSpec((tm, D_MODEL), lambda s: (jnp.minimum(s, n_tiles - 1), 0))]
        + [full(a) for a in params],
        out_specs=[ospec] * 8,
        scratch_shapes=[pltpu.VMEM((tm, width), F32), pltpu.VMEM((tm, width), F32),
                        pltpu.VMEM((SUBLANES, width), F32)],
        compiler_params=_cparams("arbitrary"),
    )(x, *params)


PAIR_W = 2 * RWKV_HEAD
N_PAIRS = RWKV_HEADS // 2
HEAD_SHIFT = RWKV_HEAD.bit_length() - 1


def _pair_masks():
    t = lax.broadcasted_iota(I32, (CHUNK, PAIR_W), 0)
    lane = lax.broadcasted_iota(I32, (CHUNK, PAIR_W), 1)
    j = lane & (RWKV_HEAD - 1)
    r = lax.broadcasted_iota(I32, (PAIR_W, PAIR_W), 0)
    c = lax.broadcasted_iota(I32, (PAIR_W, PAIR_W), 1)
    r4 = lax.broadcasted_iota(I32, (2 * PAIR_W, PAIR_W), 0)
    c4 = lax.broadcasted_iota(I32, (2 * PAIR_W, PAIR_W), 1)
    return dict(
        strict=t > j, incl=t >= j, eye=(t == j).astype(F32),
        same_block=jnp.bitwise_xor(t, j) < INV_BLOCK,
        first_head=lane < RWKV_HEAD,
        block_diag=(r >> HEAD_SHIFT) == (c >> HEAD_SHIFT),
        stacked_diag=((r4 >> HEAD_SHIFT) & 1) == (c4 >> HEAD_SHIFT))


def _rwkv_scan_kernel(r_ref, k_ref, v_ref, lw_ref, kk_ref, kka_ref, bonus_ref, g_ref, lg_ref, lb_ref,
                      o_ref, st_ref, *, chunks):
    @pl.when(pl.program_id(1) == 0)
    def _():
        st_ref[...] = jnp.zeros_like(st_ref)

    row, col = _tri_masks(CHUNK)
    ltri = (row >= col).astype(BF16)
    m = _pair_masks()
    zero = jnp.zeros((), BF16)
    pairs = range(N_PAIRS)
    sl = [slice(p * PAIR_W, (p + 1) * PAIR_W) for p in pairs]

    def block_diag(x):
        xb = x.astype(BF16)
        return jnp.where(m["block_diag"], jnp.concatenate([xb, xb], axis=0), zero)

    def pair_mul(lhs, *rhs):
        rb = jnp.concatenate([block_diag(x) for x in rhs], axis=1) if len(rhs) > 1 else block_diag(rhs[0])
        out = jnp.dot(lhs.astype(BF16), rb, preferred_element_type=F32)
        return [out[:, i * PAIR_W:(i + 1) * PAIR_W] for i in range(len(rhs))]

    def each(fn, *lists):
        return [fn(*args) for args in zip(*lists)]

    lhs, rhs, vp, ends, gammas = [], [], [], [], []
    for c in range(chunks):
        rows = pl.ds(c * CHUNK, CHUNK)
        lw = lw_ref[rows, :]
        cum = _mask_dot(ltri, lw)
        cum_end = cum[CHUNK - 1:CHUNK, :]
        e_inv = jnp.exp(-cum)
        e_end = jnp.exp(cum_end - cum)
        kka = kka_ref[rows, :]
        k = k_ref[rows, :]
        a_bar = (-kk_ref[rows, :] * jnp.exp(cum - lw)).astype(BF16)
        r_bar = (r_ref[rows, :] * jnp.exp(cum)).astype(BF16)
        b_til = (kka * e_inv).astype(BF16)
        k_til = (k * e_inv).astype(BF16)
        b_end = (kka * e_end).astype(BF16)
        k_end = (k * e_end).astype(BF16)
        gamma = jnp.exp(cum_end)
        v = v_ref[rows, :].astype(BF16)

        for s in sl:
            lhs.append(jnp.concatenate([a_bar[:, s], r_bar[:, s]], axis=0))
            rhs.append(jnp.where(m["stacked_diag"], jnp.concatenate(
                [b_til[:, s], b_til[:, s], k_til[:, s], k_til[:, s]], axis=0), zero))
            vp.append(v[:, s])
            ends.append(jnp.concatenate([b_end[:, s], k_end[:, s]], axis=0))
            gammas.append(gamma[:, s])

    blocks = each(_bdot_nt, lhs, rhs)
    a_ab = [jnp.where(m["strict"], b[:CHUNK, :PAIR_W], 0.0) for b in blocks]
    a_ak = [jnp.where(m["strict"], b[:CHUNK, PAIR_W:], 0.0) for b in blocks]
    r_ab = [jnp.where(m["incl"], b[CHUNK:, :PAIR_W], 0.0) for b in blocks]
    r_ak = [jnp.where(m["incl"], b[CHUNK:, PAIR_W:], 0.0) for b in blocks]
    from_v = [jnp.dot(jnp.concatenate([ak, rk], axis=0).astype(BF16), block_diag(x),
                      preferred_element_type=F32) for ak, rk, x in zip(a_ak, r_ak, vp)]
    d = [jnp.where(m["same_block"], a, 0.0) for a in a_ab]
    low = [a - di for a, di in zip(a_ab, d)]
    pm = [m["eye"] + di for di in d]
    dpow = [pair_mul(di, di)[0] for di in d]
    for _ in range(INV_BLOCK.bit_length() - 3):
        both = each(lambda x, p: pair_mul(x, x, p), dpow, pm)
        dpow = [b[0] for b in both]
        pm = [p + b[1] for p, b in zip(pm, both)]
    pm = [p + pair_mul(x, p)[0] for p, x in zip(pm, dpow)]
    both = each(lambda p, lo, fv, l: pair_mul(p, lo, fv[:CHUNK], l[:CHUNK]), pm, low, from_v, lhs)
    both2 = each(lambda b: pair_mul(b[0], b[0], b[1], b[2]), both)
    xw_v = [b[1] + b2[1] for b, b2 in zip(both, both2)]
    xw_a = [b[2] + b2[2] for b, b2 in zip(both, both2)]
    both3 = each(lambda b2, xv, xa: pair_mul(b2[0], xv, xa), both2, xw_v, xw_a)
    u_v = [xv + b3[0] for xv, b3 in zip(xw_v, both3)]
    wm = [xa + b3[1] for xa, b3 in zip(xw_a, both3)]
    both4 = each(lambda rb, uv, w: pair_mul(rb, uv, w), r_ab, u_v, wm)
    y_local = [fv[CHUNK:] + b4[0] for fv, b4 in zip(from_v, both4)]
    rm = [l[CHUNK:] + b4[1] for l, b4 in zip(lhs, both4)]
    gm = [jnp.where(m["block_diag"], _bdot_tn(en[:CHUNK], w), 0.0) for en, w in zip(ends, wm)]
    qm = [jnp.where(m["block_diag"], _bdot_tn(jnp.concatenate([uv.astype(BF16), x], axis=0), en), 0.0)
          for uv, x, en in zip(u_v, vp, ends)]

    state = [st_ref[p] for p in pairs]
    for c in range(chunks):
        rows = pl.ds(c * CHUNK, CHUNK)
        item = slice(c * N_PAIRS, (c + 1) * N_PAIRS)
        y = [yl + _bdot_nt(r, st) for yl, r, st in zip(y_local[item], rm[item], state)]
        state = [st * g + _bdot_nt(st, gmat) + q
                 for st, g, gmat, q in zip(state, gammas[item], gm[item], qm[item])]
        outs = []
        inv_n = 1.0 / RWKV_HEAD
        for yp in y:
            def head_stat(z):
                s0 = jnp.sum(jnp.where(m["first_head"], z, 0.0), axis=-1, keepdims=True)
                s1 = jnp.sum(jnp.where(m["first_head"], 0.0, z), axis=-1, keepdims=True)
                return jnp.where(m["first_head"], s0, s1) * inv_n
            yc = yp - head_stat(yp)
            outs.append(yc * lax.rsqrt(head_stat(yc * yc) + RWKV_LNX_EPS))
        yn = jnp.concatenate(outs, axis=1) * lg_ref[...] + lb_ref[...]
        o_ref[rows, :] = ((yn + bonus_ref[rows, :]) * g_ref[rows, :]).astype(o_ref.dtype)
    for p in pairs:
        st_ref[p] = state[p]


def _rwkv_scan(prep, batch, seq, lnx_g, lnx_b):
    t = batch * seq
    chunks = 8
    rows = CHUNK * chunks
    nblk = seq // rows
    spec = pl.BlockSpec((rows, RWKV_WIDTH), lambda b, n: (b * nblk + n, 0))
    pspec = pl.BlockSpec((1, RWKV_WIDTH), lambda b, n: (0, 0))
    return pl.pallas_call(
        functools.partial(_rwkv_scan_kernel, chunks=chunks),
        out_shape=jax.ShapeDtypeStruct((t, RWKV_WIDTH), BF16),
        grid=(batch, nblk),
        in_specs=[spec] * 8 + [pspec, pspec],
        out_specs=spec,
        scratch_shapes=[pltpu.VMEM((N_PAIRS, PAIR_W, PAIR_W), F32)],
        compiler_params=_cparams("parallel", "arbitrary"),
    )(*prep, lnx_g, lnx_b)


def _merge_kernel(og_ref, ol_ref, or_ref, x_ref, wgt_ref, bgt_ref, pg_ref, pl_ref, pr_ref, wo_ref,
                  lg_ref, lb_ref, xo_ref):
    d = D_MODEL
    xin = x_ref[...].astype(BF16)
    merged = None
    for b, (o_ref, p_ref) in enumerate(((og_ref, pg_ref), (ol_ref, pl_ref), (or_ref, pr_ref))):
        cols = slice(b * d, (b + 1) * d)
        gate = _sigmoid(jnp.dot(xin, wgt_ref[:, cols], preferred_element_type=F32) + bgt_ref[:, cols])
        term = gate * jnp.dot(o_ref[...], p_ref[...], preferred_element_type=F32)
        merged = term if merged is None else merged + term
    z = DEEPNORM_ALPHA * x_ref[...] + _bdot(merged, wo_ref[...])
    out = _layer_norm_rows(z, lg_ref[...], lb_ref[...], LN_EPS)
    xo_ref[...] = out


def _merge(o_gla, o_lru, o_rwkv, x, w_gates, b_gates, p_gla, p_lru, p_rwkv, w_out, ln_g, ln_b):
    t = x.shape[0]
    tm = min(512, t)

    def rows(width):
        return pl.BlockSpec((tm, width), lambda i: (i, 0))

    def full(a):
        return pl.BlockSpec(a.shape, lambda i: (0,) * a.ndim)

    params = (w_gates, b_gates, p_gla, p_lru, p_rwkv, w_out, ln_g, ln_b)
    return pl.pallas_call(
        _merge_kernel,
        out_shape=jax.ShapeDtypeStruct((t, D_MODEL), F32),
        grid=(t // tm,),
        in_specs=[rows(GLA_VW), rows(LRU_WIDTH), rows(RWKV_WIDTH), rows(D_MODEL)] + [full(a) for a in params],
        out_specs=rows(D_MODEL),
        compiler_params=_cparams("parallel"),
    )(o_gla, o_lru, o_rwkv, x, *params)


FFN_SPLIT = 2


def _ffn_kernel(x_ref, wg_ref, wu_ref, wd_ref, lg_ref, lb_ref, xo_ref):
    x = x_ref[...]
    xb = x.astype(BF16)
    tf = wg_ref.shape[1] // FFN_SPLIT
    acc = None
    for f in range(FFN_SPLIT):
        cols = slice(f * tf, (f + 1) * tf)
        g = jnp.dot(xb, wg_ref[:, cols], preferred_element_type=F32)
        u = jnp.dot(xb, wu_ref[:, cols], preferred_element_type=F32)
        part = _bdot(g * _sigmoid(g) * u, wd_ref[cols, :])
        acc = part if acc is None else acc + part
    z = DEEPNORM_ALPHA * x + acc
    xo_ref[...] = _layer_norm_rows(z, lg_ref[...], lb_ref[...], LN_EPS)


def _dense_ffn(x, wg, wu, wd, ln_g, ln_b):
    t = x.shape[0]
    tm = min(512, t)
    rows = pl.BlockSpec((tm, D_MODEL), lambda i: (i, 0))
    resident = lambda a: pl.BlockSpec(a.shape, lambda i: (0,) * a.ndim, pipeline_mode=pl.Buffered(1))
    return pl.pallas_call(
        _ffn_kernel,
        out_shape=jax.ShapeDtypeStruct((t, D_MODEL), F32),
        grid=(t // tm,),
        in_specs=[rows] + [resident(a) for a in (wg, wu, wd, ln_g, ln_b)],
        out_specs=rows,
        compiler_params=_cparams("parallel"),
    )(x, wg, wu, wd, ln_g, ln_b)


def _router_kernel(x_ref, wr_ref, meta_ref, wts_ref, cnt_ref, carry_ref):
    tm = x_ref.shape[0]

    @pl.when(pl.program_id(0) == 0)
    def _():
        carry_ref[...] = jnp.zeros_like(carry_ref)

    lane = lax.broadcasted_iota(I32, (tm, LANES), 1)
    neg = jnp.float32(-jnp.inf)
    logits = jnp.where(lane < N_EXPERTS, _fdot(x_ref[...], wr_ref[...]), neg)
    m1 = jnp.max(logits, axis=-1, keepdims=True)
    e1 = jnp.min(jnp.where(logits == m1, lane, LANES), axis=-1, keepdims=True)
    rest = jnp.where(lane == e1, neg, logits)
    m2 = jnp.max(rest, axis=-1, keepdims=True)
    e2 = jnp.min(jnp.where(rest == m2, lane, LANES), axis=-1, keepdims=True)
    ex = jnp.exp(m2 - m1)
    w1 = 1.0 / (1.0 + ex)
    w2 = ex / (1.0 + ex)

    hot1 = lane == e1
    hot2 = lane == e2
    onehot = jnp.where(hot1 | hot2, 1.0, 0.0)
    row, col = _tri_masks(tm)
    before = (row > col).astype(BF16)
    prefix = jnp.dot(before, onehot.astype(BF16), preferred_element_type=F32) + carry_ref[...]
    rank1 = jnp.sum(jnp.where(hot1, prefix, 0.0), axis=-1, keepdims=True).astype(I32)
    rank2 = jnp.sum(jnp.where(hot2, prefix, 0.0), axis=-1, keepdims=True).astype(I32)
    carry_ref[...] += jnp.sum(onehot, axis=0, keepdims=True)
    cnt_ref[...] = carry_ref[...]

    meta = jnp.where(lane == 0, e1, jnp.where(lane == 1, e2, jnp.where(lane == 2, rank1, rank2)))
    meta_ref[...] = jnp.where(lane < 4, meta, 0)
    wts_ref[...] = jnp.where(lane == 0, w1, jnp.where(lane == 1, w2, 0.0))


def _router(x, wr_pad):
    t = x.shape[0]
    tm = min(512, t)
    return pl.pallas_call(
        _router_kernel,
        out_shape=[jax.ShapeDtypeStruct((t, LANES), I32), jax.ShapeDtypeStruct((t, LANES), F32),
                   jax.ShapeDtypeStruct((1, LANES), F32)],
        grid=(t // tm,),
        in_specs=[pl.BlockSpec((tm, D_MODEL), lambda i: (i, 0)),
                  pl.BlockSpec((D_MODEL, LANES), lambda i: (0, 0))],
        out_specs=[pl.BlockSpec((tm, LANES), lambda i: (i, 0)), pl.BlockSpec((tm, LANES), lambda i: (i, 0)),
                   pl.BlockSpec((1, LANES), lambda i: (0, 0))],
        scratch_shapes=[pltpu.VMEM((1, LANES), F32)],
        compiler_params=_cparams("arbitrary"),
    )(x, wr_pad)


def _dispatch_kernel(pad_start_ref, pad_len_ref, nused_ref, dest_ref, x_ref, *rest):
    n_w = (len(rest) - 4) // 2
    w_refs, xs_ref, wb_refs, (zeros_ref, sem, fill_sem) = rest[:n_w], rest[n_w], rest[n_w + 1:2 * n_w + 1], rest[-3:]
    for w_ref, wb_ref in zip(w_refs, wb_refs):
        wb_ref[...] = w_ref[...].astype(wb_ref.dtype)
    tm = x_ref.shape[0]
    n_blocks = xs_ref.shape[0] // MOE_ROWS

    @pl.when(pl.program_id(0) == pl.num_programs(0) - 1)
    def _():
        zeros_ref[...] = jnp.zeros_like(zeros_ref)
        pad_copy = lambda row: pltpu.make_async_copy(zeros_ref.at[pl.ds(0, 1)], xs_ref.at[pl.ds(row, 1)], fill_sem)
        blk_copy = lambda blk: pltpu.make_async_copy(
            zeros_ref, xs_ref.at[pl.ds(pl.multiple_of(blk * MOE_ROWS, MOE_ROWS), MOE_ROWS)], fill_sem)

        def for_each(fn):
            for e in range(N_EXPERTS):
                base = pad_start_ref[e]
                lax.fori_loop(0, pad_len_ref[e], lambda r, c: (fn(pad_copy(base + r)), c)[1], 0)
            lax.fori_loop(nused_ref[0], n_blocks, lambda b, c: (fn(blk_copy(b)), c)[1], 0)

        for_each(lambda cp: cp.start())
        for_each(lambda cp: cp.wait())

    def row_copy(r, j):
        return pltpu.make_async_copy(x_ref.at[pl.ds(r, 1)], xs_ref.at[pl.ds(dest_ref[TOP_K * r + j], 1)], sem)

    def start(r, c):
        for j in range(TOP_K):
            row_copy(r, j).start(priority=j)
        return c

    lax.fori_loop(0, tm, start, 0, unroll=8)
    for j in range(TOP_K):
        pltpu.make_async_copy(x_ref, xs_ref.at[pl.ds(0, tm)], sem).wait()


def _dispatch(x, dest_flat, pad_start, pad_len, n_used, n_slots, weights):
    t = x.shape[0]
    tm = min(256, t)
    n_steps = t // tm
    flat = [w.reshape(-1, w.shape[-1]) for w in weights]
    slab = lambda w: pl.BlockSpec((w.shape[0] // n_steps, w.shape[1]), lambda i, *_: (i, 0))
    grid_spec = pltpu.PrefetchScalarGridSpec(
        num_scalar_prefetch=3,
        grid=(n_steps,),
        in_specs=[pl.BlockSpec((tm * TOP_K,), lambda i, *_: (i,), memory_space=pltpu.SMEM),
                  pl.BlockSpec((tm, D_MODEL), lambda i, *_: (i, 0))] + [slab(w) for w in flat],
        out_specs=[pl.BlockSpec(memory_space=pl.ANY)] + [slab(w) for w in flat],
        scratch_shapes=[pltpu.VMEM((MOE_ROWS, D_MODEL), F32), pltpu.SemaphoreType.DMA(()),
                        pltpu.SemaphoreType.DMA(())],
    )
    xs, *cast = pl.pallas_call(
        _dispatch_kernel,
        out_shape=[jax.ShapeDtypeStruct((n_slots, D_MODEL), F32)]
        + [jax.ShapeDtypeStruct(w.shape, BF16) for w in flat],
        grid_spec=grid_spec,
        compiler_params=_cparams("arbitrary"),
    )(pad_start, pad_len, n_used, dest_flat, x, *flat)
    return xs, [c.reshape(w.shape) for c, w in zip(cast, weights)]


def _expert_kernel(blk_e_ref, nused_ref, xs_ref, wg_ref, wu_ref, wd_ref, ys_ref, acc_ref):
    del blk_e_ref
    i = pl.program_id(0)
    f = pl.program_id(1)
    last = pl.num_programs(1) - 1
    used = i < nused_ref[0]

    @pl.when(used & (f == 0))
    def _():
        acc_ref[...] = jnp.zeros_like(acc_ref)

    @pl.when(used)
    def _():
        xb = xs_ref[...].astype(BF16)
        g = jnp.dot(xb, wg_ref[0], preferred_element_type=F32)
        u = jnp.dot(xb, wu_ref[0], preferred_element_type=F32)
        acc_ref[...] += _bdot(g * _sigmoid(g) * u, wd_ref[0])

    @pl.when(used & (f == last))
    def _():
        ys_ref[...] = acc_ref[...]

    @pl.when(jnp.logical_not(used) & (f == last))
    def _():
        ys_ref[...] = jnp.zeros_like(ys_ref)


def _experts(xs, blk_e, n_used, wg, wu, wd):
    n_slots = xs.shape[0]
    n_blocks = n_slots // MOE_ROWS
    ff = wg.shape[2]
    n_f = 2
    tf = ff // n_f
    piece = lambda i, f: jnp.where((i & 1) == 1, n_f - 1 - f, f)
    grid_spec = pltpu.PrefetchScalarGridSpec(
        num_scalar_prefetch=2,
        grid=(n_blocks, n_f),
        in_specs=[pl.BlockSpec((MOE_ROWS, D_MODEL), lambda i, f, be, nu: (i, 0)),
                  pl.BlockSpec((1, D_MODEL, tf), lambda i, f, be, nu: (be[i], 0, piece(i, f))),
                  pl.BlockSpec((1, D_MODEL, tf), lambda i, f, be, nu: (be[i], 0, piece(i, f))),
                  pl.BlockSpec((1, tf, D_MODEL), lambda i, f, be, nu: (be[i], piece(i, f), 0))],
        out_specs=pl.BlockSpec((MOE_ROWS, D_MODEL), lambda i, f, be, nu: (i, 0)),
        scratch_shapes=[pltpu.VMEM((MOE_ROWS, D_MODEL), F32)],
    )
    return pl.pallas_call(
        _expert_kernel,
        out_shape=jax.ShapeDtypeStruct((n_slots, D_MODEL), F32),
        grid_spec=grid_spec,
        compiler_params=_cparams("arbitrary", "arbitrary"),
    )(blk_e, n_used, xs, wg, wu, wd)


def _combine_kernel(dest_ref, dest_next_ref, x_ref, wts_ref, ys_ref, lg_ref, lb_ref, xo_ref, buf_ref, sem):
    tm = x_ref.shape[0]
    i = pl.program_id(0)
    n = pl.num_programs(0)

    def gather(idx_ref, slot):
        def start(r, c):
            for j in range(TOP_K):
                pltpu.make_async_copy(ys_ref.at[pl.ds(idx_ref[TOP_K * r + j], 1)],
                                      buf_ref.at[slot, j, pl.ds(r, 1)], sem.at[slot]).start(priority=j)
            return c
        lax.fori_loop(0, tm, start, 0, unroll=8)

    slot = i & 1

    @pl.when(i == 0)
    def _():
        gather(dest_ref, 0)

    @pl.when(i + 1 < n)
    def _():
        gather(dest_next_ref, 1 - slot)

    for j in range(TOP_K):
        pltpu.make_async_copy(ys_ref.at[pl.ds(0, tm)], buf_ref.at[slot, j], sem.at[slot]).wait()
    w = wts_ref[...]
    f = w[:, 0:1] * buf_ref[slot, 0] + w[:, 1:2] * buf_ref[slot, 1]
    z = DEEPNORM_ALPHA * x_ref[...] + f
    xo_ref[...] = _layer_norm_rows(z, lg_ref[...], lb_ref[...], LN_EPS)


def _combine(x, dest_flat, wts, ys, ln_g, ln_b):
    t = x.shape[0]
    tm = min(256, t)
    n_tiles = t // tm
    rows = lambda width: pl.BlockSpec((tm, width), lambda i: (i, 0))
    vec = pl.BlockSpec((1, D_MODEL), lambda i: (0, 0))
    return pl.pallas_call(
        _combine_kernel,
        out_shape=jax.ShapeDtypeStruct((t, D_MODEL), F32),
        grid=(n_tiles,),
        in_specs=[pl.BlockSpec((tm * TOP_K,), lambda i: (i,), memory_space=pltpu.SMEM),
                  pl.BlockSpec((tm * TOP_K,), lambda i: (jnp.minimum(i + 1, n_tiles - 1),),
                               memory_space=pltpu.SMEM),
                  rows(D_MODEL), rows(LANES), pl.BlockSpec(memory_space=pl.ANY), vec, vec],
        out_specs=rows(D_MODEL),
        scratch_shapes=[pltpu.VMEM((2, TOP_K, tm, D_MODEL), F32), pltpu.SemaphoreType.DMA((2,))],
        compiler_params=_cparams("arbitrary"),
    )(dest_flat, dest_flat, x, wts, ys, ln_g, ln_b)


def _moe_ffn(x, w_router, wg, wu, wd, ln_g, ln_b):
    t = x.shape[0]
    wr_pad = jnp.pad(w_router, ((0, 0), (0, LANES - N_EXPERTS)))
    meta, wts, cnt = _router(x, wr_pad)
    counts = cnt[0, :N_EXPERTS].astype(I32)
    padded = (counts + MOE_ROWS - 1) // MOE_ROWS * MOE_ROWS
    seg_end = jnp.cumsum(padded)
    seg_start = seg_end - padded
    n_blocks = (t * TOP_K) // MOE_ROWS + N_EXPERTS
    dest = seg_start[meta[:, 0:TOP_K]] + meta[:, TOP_K:2 * TOP_K]
    dest_flat = dest.reshape(-1).astype(I32)
    blk_start = jnp.arange(n_blocks, dtype=I32) * MOE_ROWS
    blk_e = jnp.minimum(jnp.sum(blk_start[:, None] >= seg_end[None, :], axis=1), N_EXPERTS - 1).astype(I32)
    n_used = (seg_end[-1:] // MOE_ROWS).astype(I32)
    xs, (wg, wu, wd) = _dispatch(x, dest_flat, (seg_start + counts).astype(I32), (padded - counts).astype(I32),
                                 n_used, n_blocks * MOE_ROWS, (wg, wu, wd))
    ys = _experts(xs, blk_e, n_used, wg, wu, wd)
    return _combine(x, dest_flat, wts, ys, ln_g, ln_b)


def _pad_rows(w, rows, at=0):
    out = jnp.zeros((rows, w.shape[1]), w.dtype)
    return out.at[at:at + w.shape[0]].set(w)


def _reorder_in_projection(w, b):
    sizes = (GLA_KW, GLA_KW, GLA_VW, GLA_VW, GLA_DECAY_RANK, LRU_WIDTH, LRU_WIDTH,
             3 * RWKV_WIDTH, RWKV_DECAY_RANK, RWKV_A_RANK, RWKV_GATE_RANK, N_BRANCH * D_MODEL)
    offs = [0]
    for s in sizes:
        offs.append(offs[-1] + s)
    wb = jnp.concatenate([w, b[None, :]], axis=0)
    piece = lambda i: wb[:, offs[i]:offs[i + 1]]
    zeros = lambda n: jnp.zeros((wb.shape[0], n), wb.dtype)
    q, k, v, r, dec, lx, lg, rkv, wl, al, gl, gates = (piece(i) for i in range(len(sizes)))
    small = jnp.concatenate([wl, al, gl, zeros(RW_SMALL_W - SM_GL[0] - RWKV_GATE_RANK)], axis=1)
    groups = {"gla": jnp.concatenate([q, k, v, r, dec, zeros(GLA_DEC_W - GLA_DECAY_RANK)], axis=1),
              "lru": jnp.concatenate([lx, lg], axis=1),
              "rwkv": jnp.concatenate([rkv, small], axis=1),
              "gates": gates}
    return {name: (g[:-1].astype(BF16), g[-1:]) for name, g in groups.items()}


def _block_diag(blocks):
    n, bi, bo = blocks.shape
    eye = jnp.eye(n, dtype=blocks.dtype)
    return (eye[:, None, :, None] * blocks[:, :, None, :]).reshape(n * bi, n * bo)


def kernel(x, w_in, b_in, gla_w_decay_up, gla_b_decay, gla_norm_g, gla_norm_b, lru_conv_w, lru_conv_b, lru_w_r, lru_b_r, lru_w_i, lru_b_i, lru_lambda, rwkv_mu, rwkv_w0, rwkv_w2, rwkv_a0, rwkv_a2, rwkv_g2, rwkv_k_k, rwkv_k_a, rwkv_r_k, rwkv_lnx_g, rwkv_lnx_b, p_gla, p_lru, p_rwkv, w_out, ln_mix_g, ln_mix_b, ffn_w_gate, ffn_w_up, ffn_w_down, moe_w_router, moe_w_gate, moe_w_up, moe_w_down, ln_ffn_g, ln_ffn_b):
    batch, seq, d = x.shape
    t = batch * seq
    xf = x.reshape(t, d)
    row = lambda a: a.reshape(1, -1)
    head_ones = _block_diag(jnp.ones((RWKV_HEADS, RWKV_HEAD, RWKV_HEAD), BF16))
    for l in range(DEPTH):
        proj = _reorder_in_projection(w_in[l], b_in[l])
        w_gates, b_gates = proj["gates"]

        wup = _pad_rows(gla_w_decay_up[l], GLA_DEC_W).astype(BF16)
        o_gla = _gla_branch(xf, batch, seq, *proj["gla"], wup, row(gla_b_decay[l]), row(gla_norm_g[l]),
                            row(gla_norm_b[l]))

        w_ri = jnp.concatenate([_block_diag(lru_w_r[l]), _block_diag(lru_w_i[l])], axis=1).astype(BF16)
        b_ri = jnp.concatenate([lru_b_r[l], lru_b_i[l]])[None, :]
        o_lru = _lru_branch(xf, batch, seq, *proj["lru"], lru_conv_w[l], row(lru_conv_b[l]), w_ri, b_ri,
                            row(jax.nn.softplus(-lru_lambda[l])))

        mu = rwkv_mu[l]
        mu_all = jnp.zeros((3 * RWKV_WIDTH + RW_SMALL_W,), F32).at[0:mu.shape[0]].set(mu)
        prep_params = (*proj["rwkv"], row(mu_all),
                       _pad_rows(rwkv_w2[l], SM_WA[1], 0).astype(BF16),
                       _pad_rows(rwkv_a2[l], SM_WA[1], RWKV_DECAY_RANK).astype(BF16),
                       _pad_rows(rwkv_g2[l], SM_GL[1], 0).astype(BF16),
                       row(rwkv_w0[l]), row(rwkv_a0[l]), row(rwkv_k_k[l]), row(rwkv_k_a[l]),
                       row(rwkv_r_k[l]), head_ones)
        prep = _rwkv_prep(xf, batch, seq, prep_params)
        o_rwkv = _rwkv_scan(prep, batch, seq, row(rwkv_lnx_g[l]), row(rwkv_lnx_b[l]))

        xf = _merge(o_gla, o_lru, o_rwkv, xf, w_gates, b_gates, p_gla[l].astype(BF16), p_lru[l].astype(BF16),
                    p_rwkv[l].astype(BF16), w_out[l].astype(BF16), row(ln_mix_g[l]), row(ln_mix_b[l]))
        i = l // 2
        if l % 2 == 0:
            xf = _dense_ffn(xf, ffn_w_gate[i].astype(BF16), ffn_w_up[i].astype(BF16),
                            ffn_w_down[i].astype(BF16), row(ln_ffn_g[l]), row(ln_ffn_b[l]))
        else:
            xf = _moe_ffn(xf, moe_w_router[i], moe_w_gate[i], moe_w_up[i], moe_w_down[i],
                          row(ln_ffn_g[l]), row(ln_ffn_b[l]))
    return xf.reshape(batch, seq, d)
```

```python
import functools

import jax
import jax.numpy as jnp
from jax import lax
from jax.experimental import pallas as pl
from jax.experimental.pallas import tpu as pltpu

F32 = jnp.float32
BF16 = jnp.bfloat16
I32 = jnp.int32

D_MODEL = 1024
DEPTH = 2
GLA_HEADS, GLA_DK, GLA_DV = 4, 64, 128
GLA_KW, GLA_VW = GLA_HEADS * GLA_DK, GLA_HEADS * GLA_DV
GLA_DECAY_RANK = 16
GLA_TEMP = 16.0
LRU_WIDTH, LRU_BLOCKS, LRU_CONV, LRU_C = 512, 8, 4, 8.0
RWKV_HEAD, RWKV_WIDTH = 64, 512
RWKV_HEADS = RWKV_WIDTH // RWKV_HEAD
RWKV_DECAY_RANK, RWKV_A_RANK, RWKV_GATE_RANK = 64, 64, 160
RWKV_LNX_EPS = 64e-5
N_BRANCH = 3
N_EXPERTS, TOP_K = 8, 2
DEEPNORM_ALPHA = (2 * DEPTH) ** 0.25
LN_EPS = 1e-5

LANES = 128
SUBLANES = 8
VMEM_LIMIT_BYTES = 56 * 1024 * 1024

SM_WA = (0, 128)
SM_GL = (128, 256)
RW_SMALL_W = 512
GLA_DEC_W = LANES
GLA_COLS = {"q": (0, 256), "k": (256, 512), "v": (512, 1024), "r": (1024, 1536), "dec": (1536, 1664)}

CHUNK = 64
INV_BLOCK = 16
MOE_ROWS = 512


def _cparams(*sem):
    return pltpu.CompilerParams(dimension_semantics=sem, vmem_limit_bytes=VMEM_LIMIT_BYTES)


def _sigmoid(x):
    return 1.0 / (1.0 + jnp.exp(-x))


def _softplus(x):
    return jnp.maximum(x, 0.0) + jnp.log(1.0 + jnp.exp(-jnp.abs(x)))


def _bdot(a, b):
    return jnp.dot(a.astype(BF16), b.astype(BF16), preferred_element_type=F32)


def _bdot_nt(a, b):
    return lax.dot_general(a.astype(BF16), b.astype(BF16), (((1,), (1,)), ((), ())),
                           preferred_element_type=F32)


def _bdot_tn(a, b):
    return lax.dot_general(a.astype(BF16), b.astype(BF16), (((0,), (0,)), ((), ())),
                           preferred_element_type=F32)


def _fdot(a, b):
    ah = a.astype(BF16)
    al = (a - ah.astype(F32)).astype(BF16)
    bh = b.astype(BF16)
    bl = (b - bh.astype(F32)).astype(BF16)
    dot = lambda u, v: jnp.dot(u, v, preferred_element_type=F32)
    return dot(al, bh) + dot(ah, bl) + dot(ah, bh)


def _split3(x):
    hi = x.astype(BF16)
    rest = x - hi.astype(F32)
    mid = rest.astype(BF16)
    lo = (rest - mid.astype(F32)).astype(BF16)
    return hi, mid, lo


def _mask_dot(mask_bf16, x):
    hi, mid, lo = _split3(x)
    dot = lambda part: jnp.dot(mask_bf16, part, preferred_element_type=F32)
    return dot(lo) + dot(mid) + dot(hi)


def _dot_mask(x, mask_bf16):
    hi, mid, lo = _split3(x)
    dot = lambda part: jnp.dot(part, mask_bf16, preferred_element_type=F32)
    return dot(lo) + dot(mid) + dot(hi)


def _layer_norm_rows(z, g, b, eps):
    mu = jnp.mean(z, axis=-1, keepdims=True)
    zc = z - mu
    var = jnp.mean(zc * zc, axis=-1, keepdims=True)
    return zc * lax.rsqrt(var + eps) * g + b


def _tri_masks(n):
    row = lax.broadcasted_iota(I32, (n, n), 0)
    col = lax.broadcasted_iota(I32, (n, n), 1)
    return row, col


MXU_COLS = 256


def _gla_kernel(*refs, chunks, blocks_per_seq):
    *io_refs, st_ref, ha_ref, hb_ref = refs
    s = pl.program_id(0)

    @pl.when(s == 0)
    def _():
        hb_ref[...] = jnp.zeros_like(hb_ref)

    @pl.when((s == 0) | (((s + blocks_per_seq - 1) % blocks_per_seq) == 0))
    def _():
        st_ref[...] = jnp.zeros_like(st_ref)

    body = functools.partial(_gla_body, chunks, (*io_refs, st_ref))

    @pl.when((s & 1) == 0)
    def _():
        body(ha_ref, hb_ref)

    @pl.when((s & 1) == 1)
    def _():
        body(hb_ref, ha_ref)


def _gla_body(chunks, refs, h_next, h_ref):
    x_ref, w_ref, b_ref, wup_ref, bdec_ref, ng_ref, nb_ref, o_ref, st_ref = refs
    xb = x_ref[...].astype(BF16)
    width = w_ref.shape[1]
    pieces = iter([slice(c0, min(c0 + MXU_COLS, width)) for c0 in range(0, width, MXU_COLS)])

    def project_piece():
        cols = next(pieces, None)
        if cols is not None:
            h_next[:, cols] = jnp.dot(xb, w_ref[:, cols], preferred_element_type=F32) + b_ref[:, cols]

    q_ref = h_ref.at[:, GLA_COLS["q"][0]:GLA_COLS["q"][1]]
    k_ref = h_ref.at[:, GLA_COLS["k"][0]:GLA_COLS["k"][1]]
    v_ref = h_ref.at[:, GLA_COLS["v"][0]:GLA_COLS["v"][1]]
    r_ref = h_ref.at[:, GLA_COLS["r"][0]:GLA_COLS["r"][1]]
    dec_ref = h_ref.at[:, GLA_COLS["dec"][0]:GLA_COLS["dec"][1]]

    row, col = _tri_masks(CHUNK)
    ltri = (row >= col).astype(BF16)
    kw, vw = 2 * GLA_DK, 2 * GLA_DV
    key_shift, val_shift = GLA_DK.bit_length() - 1, GLA_DV.bit_length() - 1
    iota = lambda shape, axis: lax.broadcasted_iota(I32, shape, axis)
    causal = iota((CHUNK, kw), 0) >= (iota((CHUNK, kw), 1) & (GLA_DK - 1))
    key_diag = (iota((kw, kw), 0) >> key_shift) == (iota((kw, kw), 1) >> key_shift)
    val_diag = (iota((kw, vw), 0) >> key_shift) == (iota((kw, vw), 1) >> val_shift)
    state_diag = (iota((vw, kw), 0) >> val_shift) == (iota((vw, kw), 1) >> key_shift)
    zero = jnp.zeros((), BF16)
    pairs = range(GLA_HEADS // 2)
    qp, vp, k_diag, kep, decays = [], [], [], [], []
    for c in range(chunks):
        project_piece()
        rows = pl.ds(c * CHUNK, CHUNK)
        logits = _bdot(dec_ref[rows, :], wup_ref[...]) + bdec_ref[...]
        log_a = -_softplus(-logits) * (1.0 / GLA_TEMP)
        cum = _mask_dot(ltri, log_a)
        cum_end = cum[CHUNK - 1:CHUNK, :]
        q = q_ref[rows, :] * (GLA_DK ** -0.5)
        k = k_ref[rows, :]
        q_dec = (q * jnp.exp(cum)).astype(BF16)
        k_inv = (k * jnp.exp(-cum)).astype(BF16)
        k_end = (k * jnp.exp(cum_end - cum)).astype(BF16)
        decay_end = jnp.exp(cum_end)
        v = v_ref[rows, :].astype(BF16)
        for p in pairs:
            s = slice(p * kw, (p + 1) * kw)
            qp.append(q_dec[:, s])
            vp.append(v[:, p * vw:(p + 1) * vw])
            k_diag.append(jnp.where(key_diag, jnp.concatenate([k_inv[:, s], k_inv[:, s]], axis=0), zero))
            kep.append(k_end[:, s])
            decays.append(decay_end[:, s])

    for _ in range(chunks):
        project_piece()

    v_diag = [jnp.where(val_diag, jnp.concatenate([x, x], axis=0), zero) for x in vp]
    scores = [jnp.where(causal, _bdot_nt(q, kd), 0.0) for q, kd in zip(qp, k_diag)]
    o_local = [_bdot(s, vd) for s, vd in zip(scores, v_diag)]
    upd = [jnp.where(state_diag, _bdot_tn(x, ke), 0.0) for x, ke in zip(vp, kep)]

    state = [st_ref[p] for p in pairs]
    n_pairs = len(pairs)
    for c in range(chunks):
        rows = pl.ds(c * CHUNK, CHUNK)
        item = slice(c * n_pairs, (c + 1) * n_pairs)
        o = [ol + _bdot_nt(q, st) for ol, q, st in zip(o_local[item], qp[item], state)]
        state = [st * dcy + u for st, dcy, u in zip(state, decays[item], upd[item])]
        outs = []
        for op in o:
            for oh in (op[:, :GLA_DV], op[:, GLA_DV:]):
                mu = jnp.mean(oh, axis=-1, keepdims=True)
                oc = oh - mu
                var = jnp.mean(oc * oc, axis=-1, keepdims=True)
                outs.append(oc * lax.rsqrt(var + LN_EPS))
        y = jnp.concatenate(outs, axis=1) * ng_ref[...] + nb_ref[...]
        r = r_ref[rows, :]
        o_ref[rows, :] = (y * (r * _sigmoid(r))).astype(o_ref.dtype)
    for p in pairs:
        st_ref[p] = state[p]


def _gla_branch(x, batch, seq, w, bias, wup_pad, b_decay, norm_g, norm_b):
    t = batch * seq
    chunks = 8
    rows = CHUNK * chunks
    nblk = seq // rows

    n_tiles = batch * nblk

    def full(a):
        return pl.BlockSpec(a.shape, lambda s: (0,) * a.ndim)

    args = (w, bias, wup_pad, b_decay, norm_g, norm_b)
    return pl.pallas_call(
        functools.partial(_gla_kernel, chunks=chunks, blocks_per_seq=nblk),
        out_shape=jax.ShapeDtypeStruct((t, GLA_VW), BF16),
        grid=(n_tiles + 1,),
        in_specs=[pl.BlockSpec((rows, D_MODEL), lambda s: (jnp.minimum(s, n_tiles - 1), 0))]
        + [full(a) for a in args],
        out_specs=pl.BlockSpec((rows, GLA_VW), lambda s: (jnp.maximum(s - 1, 0), 0)),
        scratch_shapes=[pltpu.VMEM((GLA_HEADS // 2, 2 * GLA_DV, 2 * GLA_DK), F32),
                        pltpu.VMEM((rows, w.shape[1]), F32), pltpu.VMEM((rows, w.shape[1]), F32)],
        compiler_params=_cparams("arbitrary"),
    )(x, *args)


def _lru_kernel(xin_ref, w_ref, b_ref, cw_ref, cb_ref, wri_ref, bri_ref, sp_ref,
                o_ref, xbuf_ref, a_ref, u_ref, hs_ref, gate_ref, h_ref):
    tm = xin_ref.shape[0]
    pad = SUBLANES

    @pl.when(pl.program_id(1) == 0)
    def _():
        xbuf_ref[0:pad, :] = jnp.zeros((pad, LRU_WIDTH), F32)
        h_ref[...] = jnp.zeros_like(h_ref)

    proj = jnp.dot(xin_ref[...].astype(BF16), w_ref[...], preferred_element_type=F32) + b_ref[...]
    x = proj[:, :LRU_WIDTH]
    gate_ref[...] = jax.nn.gelu(proj[:, LRU_WIDTH:])
    xbuf_ref[pad:pad + tm, :] = x
    xc = cb_ref[...] + x * cw_ref[LRU_CONV - 1:LRU_CONV, :]
    for j in range(LRU_CONV - 1):
        back = LRU_CONV - 1 - j
        xc = xc + xbuf_ref[pl.ds(pad - back, tm), :] * cw_ref[j:j + 1, :]
    xbuf_ref[0:pad, :] = x[tm - pad:tm, :]

    ri = _sigmoid(_bdot(xc, wri_ref[...]) + bri_ref[...])
    rg, ig = ri[:, :LRU_WIDTH], ri[:, LRU_WIDTH:]
    log_a = -LRU_C * rg * sp_ref[...]
    a = jnp.exp(log_a)
    a_ref[...] = a
    u_ref[...] = jnp.sqrt(1.0 - a * a) * (ig * xc)

    def step(t, hprev):
        hnew = a_ref[pl.ds(t, 1), :] * hprev + u_ref[pl.ds(t, 1), :]
        hs_ref[pl.ds(t, 1), :] = hnew
        return hnew

    h_ref[...] = lax.fori_loop(0, tm, step, h_ref[...], unroll=8)
    o_ref[...] = (gate_ref[...] * hs_ref[...]).astype(o_ref.dtype)


def _lru_branch(x, batch, seq, w, bias, conv_w, conv_b, w_ri, b_ri, softplus_neg_lam):
    t = batch * seq
    tm = min(512, seq)
    nblk = seq // tm

    def full(a):
        return pl.BlockSpec(a.shape, lambda b, n: (0,) * a.ndim)

    args = (w, bias, conv_w, conv_b, w_ri, b_ri, softplus_neg_lam)
    return pl.pallas_call(
        _lru_kernel,
        out_shape=jax.ShapeDtypeStruct((t, LRU_WIDTH), BF16),
        grid=(batch, nblk),
        in_specs=[pl.BlockSpec((tm, D_MODEL), lambda b, n: (b * nblk + n, 0))] + [full(a) for a in args],
        out_specs=pl.BlockSpec((tm, LRU_WIDTH), lambda b, n: (b * nblk + n, 0)),
        scratch_shapes=[pltpu.VMEM((tm + SUBLANES, LRU_WIDTH), F32)]
        + [pltpu.VMEM((tm, LRU_WIDTH), F32)] * 4 + [pltpu.VMEM((1, LRU_WIDTH), F32)],
        compiler_params=_cparams("parallel", "arbitrary"),
    )(x, *args)


def _rwkv_prep_kernel(*refs, blocks_per_seq):
    *io_refs, ha_ref, hb_ref, carry_ref = refs
    tm = io_refs[0].shape[0]
    s = pl.program_id(0)
    body = functools.partial(_rwkv_prep_body, tm, s, RWKV_WIDTH, blocks_per_seq, (*io_refs, carry_ref))

    @pl.when(s == 0)
    def _():
        hb_ref[...] = jnp.zeros_like(hb_ref)
        carry_ref[...] = jnp.zeros_like(carry_ref)

    @pl.when((s & 1) == 0)
    def _():
        body(ha_ref, hb_ref)

    @pl.when((s & 1) == 1)
    def _():
        body(hb_ref, ha_ref)


def _rwkv_prep_body(tm, s, gw, blocks_per_seq, refs, h_next, h_cur):
    (x_ref, w_ref, b_ref, mu_ref, w2_ref, a2_ref, g2_ref, w0_ref, a0_ref, kk_ref, ka_ref, rk_ref, bd_ref,
     r_out, k_out, v_out, lw_out, kk_out, kka_out, bonus_out, g_out, carry_ref) = refs
    first = ((s + blocks_per_seq - 1) % blocks_per_seq) == 0
    keep = jnp.where(first, 0.0, 1.0)
    row0 = lax.broadcasted_iota(I32, (tm, 1), 0) == 0
    xb = x_ref[...].astype(BF16)

    pw = gw // 2

    def project(j):
        cols = slice(j * pw, (j + 1) * pw)
        h_next[:, cols] = jnp.dot(xb, w_ref[:, cols], preferred_element_type=F32) + b_ref[:, cols]

    def shifted(j):
        cols = slice(j * gw, (j + 1) * gw)
        cur = h_cur[:, cols]
        prev_row = carry_ref[0:1, cols] * keep
        carry_ref[0:1, cols] = cur[tm - 1:tm, :]
        prev = jnp.where(row0, prev_row, pltpu.roll(cur, 1, axis=0))
        return cur + (prev - cur) * mu_ref[:, cols]

    project(0)
    sm = shifted(3)
    wa = sm[:, SM_WA[0]:SM_WA[0] + SM_WA[1]]
    gl = sm[:, SM_GL[0]:SM_GL[0] + SM_GL[1]]
    project(1)
    w_log = -_softplus(-(w0_ref[...] + _bdot(jnp.tanh(wa), w2_ref[...]))) - 0.5
    lw_out[...] = -jnp.exp(w_log)
    project(2)
    a = _sigmoid(a0_ref[...] + _bdot(wa, a2_ref[...]))
    g_out[...] = _bdot(_sigmoid(gl), g2_ref[...])
    project(3)
    k = shifted(1)
    kk = k * kk_ref[...]
    norm = jnp.sqrt(_dot_mask(kk * kk, bd_ref[...]))
    project(4)
    kk = kk / jnp.maximum(norm, 1e-12)
    kk_out[...] = kk
    kka_out[...] = kk * a
    k2 = k * (1.0 + (a - 1.0) * ka_ref[...])
    k_out[...] = k2
    project(5)
    r = shifted(0)
    r_out[...] = r
    project(6)
    v = shifted(2)
    v_out[...] = v.astype(v_out.dtype)
    project(7)
    bonus_out[...] = _dot_mask(r * k2 * rk_ref[...], bd_ref[...]) * v


def _rwkv_prep(x, batch, seq, params):
    t = batch * seq
    tm = min(512, seq)
    bps = seq // tm
    n_tiles = t // tm
    width = params[0].shape[1]

    def full(a):
        return pl.BlockSpec(a.shape, lambda s: (0,) * a.ndim)

    out = lambda dtype: jax.ShapeDtypeStruct((t, RWKV_WIDTH), dtype)
    ospec = pl.BlockSpec((tm, RWKV_WIDTH), lambda s: (jnp.maximum(s - 1, 0), 0))
    out_dtypes = (F32, F32, BF16, F32, F32, F32, F32, F32)
    return pl.pallas_call(
        functools.partial(_rwkv_prep_kernel, blocks_per_seq=bps),
        out_shape=[out(dt) for dt in out_dtypes],
        grid=(n_tiles + 1,),
        in_specs=[pl.BlockSpec((tm, D_MODEL), lambda s: (jnp.minimum(s, n_tiles - 1), 0))]
        + [full(a) for a in params],
        out_specs=[ospec] * 8,
        scratch_shapes=[pltpu.VMEM((tm, width), F32), pltpu.VMEM((tm, width), F32),
                        pltpu.VMEM((SUBLANES, width), F32)],
        compiler_params=_cparams("arbitrary"),
    )(x, *params)


PAIR_W = 2 * RWKV_HEAD
N_PAIRS = RWKV_HEADS // 2
HEAD_SHIFT = RWKV_HEAD.bit_length() - 1


def _pair_masks():
    t = lax.broadcasted_iota(I32, (CHUNK, PAIR_W), 0)
    lane = lax.broadcasted_iota(I32, (CHUNK, PAIR_W), 1)
    j = lane & (RWKV_HEAD - 1)
    r = lax.broadcasted_iota(I32, (PAIR_W, PAIR_W), 0)
    c = lax.broadcasted_iota(I32, (PAIR_W, PAIR_W), 1)
    r4 = lax.broadcasted_iota(I32, (2 * PAIR_W, PAIR_W), 0)
    c4 = lax.broadcasted_iota(I32, (2 * PAIR_W, PAIR_W), 1)
    return dict(
        strict=t > j, incl=t >= j, eye=(t == j).astype(F32),
        same_block=jnp.bitwise_xor(t, j) < INV_BLOCK,
        first_head=lane < RWKV_HEAD,
        block_diag=(r >> HEAD_SHIFT) == (c >> HEAD_SHIFT),
        stacked_diag=((r4 >> HEAD_SHIFT) & 1) == (c4 >> HEAD_SHIFT))


N_SLOT_REFS = 5


def _rwkv_scan_kernel(*refs, chunks, blocks_per_seq):
    *io_refs, st_ref = refs[:-2 * N_SLOT_REFS]
    slot_a, slot_b = refs[-2 * N_SLOT_REFS:-N_SLOT_REFS], refs[-N_SLOT_REFS:]
    s = pl.program_id(0)

    @pl.when(s == 0)
    def _():
        for ref in slot_b:
            ref[...] = jnp.zeros_like(ref)

    @pl.when((s == 0) | (((s + blocks_per_seq - 1) % blocks_per_seq) == 0))
    def _():
        st_ref[...] = jnp.zeros_like(st_ref)

    body = functools.partial(_rwkv_scan_body, chunks, (*io_refs, st_ref))

    @pl.when((s & 1) == 0)
    def _():
        body(slot_a, slot_b)

    @pl.when((s & 1) == 1)
    def _():
        body(slot_b, slot_a)


def _rwkv_scan_body(chunks, refs, wr, rd):
    r_ref, k_ref, v_ref, lw_ref, kk_ref, kka_ref, bonus_ref, g_ref, lg_ref, lb_ref, o_ref, st_ref = refs
    yl_w, rm_w, gm_w, qm_w, gam_w = wr
    yl_r, rm_r, gm_r, qm_r, gam_r = rd
    row, col = _tri_masks(CHUNK)
    ltri = (row >= col).astype(BF16)
    m = _pair_masks()
    zero = jnp.zeros((), BF16)
    pairs = range(N_PAIRS)
    sl = [slice(p * PAIR_W, (p + 1) * PAIR_W) for p in pairs]

    def block_diag(x):
        xb = x.astype(BF16)
        return jnp.where(m["block_diag"], jnp.concatenate([xb, xb], axis=0), zero)

    def pair_mul(lhs, *rhs):
        rb = jnp.concatenate([block_diag(x) for x in rhs], axis=1) if len(rhs) > 1 else block_diag(rhs[0])
        out = jnp.dot(lhs.astype(BF16), rb, preferred_element_type=F32)
        return [out[:, i * PAIR_W:(i + 1) * PAIR_W] for i in range(len(rhs))]

    def each(fn, *lists):
        return [fn(*args) for args in zip(*lists)]

    carried = {"state": [st_ref[p] for p in pairs]}
    pending = iter(range(chunks))
    inv_n = 1.0 / RWKV_HEAD

    def head_stat(z):
        s0 = jnp.sum(jnp.where(m["first_head"], z, 0.0), axis=-1, keepdims=True)
        s1 = jnp.sum(jnp.where(m["first_head"], 0.0, z), axis=-1, keepdims=True)
        return jnp.where(m["first_head"], s0, s1) * inv_n

    def previous_tile_chunk():
        c = next(pending, None)
        if c is None:
            return
        rows = pl.ds(c * CHUNK, CHUNK)
        items = range(c * N_PAIRS, (c + 1) * N_PAIRS)
        state = carried["state"]
        y = [yl_r[i] + _bdot_nt(rm_r[i], st) for i, st in zip(items, state)]
        carried["state"] = [st * gam_r[i, 0:1, :] + _bdot_nt(st, gm_r[i]) + qm_r[i] for i, st in zip(items, state)]
        outs = []
        for yp in y:
            yc = yp - head_stat(yp)
            outs.append(yc * lax.rsqrt(head_stat(yc * yc) + RWKV_LNX_EPS))
        yn = jnp.concatenate(outs, axis=1) * lg_ref[...] + lb_ref[...]
        o_ref[rows, :] = ((yn + bonus_ref[rows, :]) * g_ref[rows, :]).astype(o_ref.dtype)

    lhs, rhs, vp, ends, gammas = [], [], [], [], []
    for c in range(chunks):
        rows = pl.ds(c * CHUNK, CHUNK)
        lw = lw_ref[rows, :]
        cum = _mask_dot(ltri, lw)
        cum_end = cum[CHUNK - 1:CHUNK, :]
        e_inv = jnp.exp(-cum)
        e_end = jnp.exp(cum_end - cum)
        kka = kka_ref[rows, :]
        k = k_ref[rows, :]
        a_bar = (-kk_ref[rows, :] * jnp.exp(cum - lw)).astype(BF16)
        r_bar = (r_ref[rows, :] * jnp.exp(cum)).astype(BF16)
        b_til = (kka * e_inv).astype(BF16)
        k_til = (k * e_inv).astype(BF16)
        b_end = (kka * e_end).astype(BF16)
        k_end = (k * e_end).astype(BF16)
        gamma = jnp.exp(cum_end)
        v = v_ref[rows, :].astype(BF16)

        for s in sl:
            lhs.append(jnp.concatenate([a_bar[:, s], r_bar[:, s]], axis=0))
            rhs.append(jnp.where(m["stacked_diag"], jnp.concatenate(
                [b_til[:, s], b_til[:, s], k_til[:, s], k_til[:, s]], axis=0), zero))
            vp.append(v[:, s])
            ends.append(jnp.concatenate([b_end[:, s], k_end[:, s]], axis=0))
            gammas.append(gamma[:, s])

    blocks = each(_bdot_nt, lhs, rhs)
    previous_tile_chunk()
    a_ab = [jnp.where(m["strict"], b[:CHUNK, :PAIR_W], 0.0) for b in blocks]
    a_ak = [jnp.where(m["strict"], b[:CHUNK, PAIR_W:], 0.0) for b in blocks]
    r_ab = [jnp.where(m["incl"], b[CHUNK:, :PAIR_W], 0.0) for b in blocks]
    r_ak = [jnp.where(m["incl"], b[CHUNK:, PAIR_W:], 0.0) for b in blocks]
    from_v = [jnp.dot(jnp.concatenate([ak, rk], axis=0).astype(BF16), block_diag(x),
                      preferred_element_type=F32) for ak, rk, x in zip(a_ak, r_ak, vp)]
    previous_tile_chunk()
    d = [jnp.where(m["same_block"], a, 0.0) for a in a_ab]
    low = [a - di for a, di in zip(a_ab, d)]
    pm = [m["eye"] + di for di in d]
    dpow = [pair_mul(di, di)[0] for di in d]
    previous_tile_chunk()
    for _ in range(INV_BLOCK.bit_length() - 3):
        both = each(lambda x, p: pair_mul(x, x, p), dpow, pm)
        dpow = [b[0] for b in both]
        pm = [p + b[1] for p, b in zip(pm, both)]
        previous_tile_chunk()
    pm = [p + pair_mul(x, p)[0] for p, x in zip(pm, dpow)]
    previous_tile_chunk()
    both = each(lambda p, lo, fv, l: pair_mul(p, lo, fv[:CHUNK], l[:CHUNK]), pm, low, from_v, lhs)
    previous_tile_chunk()
    both2 = each(lambda b: pair_mul(b[0], b[0], b[1], b[2]), both)
    previous_tile_chunk()
    xw_v = [b[1] + b2[1] for b, b2 in zip(both, both2)]
    xw_a = [b[2] + b2[2] for b, b2 in zip(both, both2)]
    both3 = each(lambda b2, xv, xa: pair_mul(b2[0], xv, xa), both2, xw_v, xw_a)
    u_v = [xv + b3[0] for xv, b3 in zip(xw_v, both3)]
    wm = [xa + b3[1] for xa, b3 in zip(xw_a, both3)]
    both4 = each(lambda rb, uv, w: pair_mul(rb, uv, w), r_ab, u_v, wm)
    y_local = [fv[CHUNK:] + b4[0] for fv, b4 in zip(from_v, both4)]
    rm = [l[CHUNK:] + b4[1] for l, b4 in zip(lhs, both4)]
    gm = [jnp.where(m["block_diag"], _bdot_tn(en[:CHUNK], w), 0.0) for en, w in zip(ends, wm)]
    qm = [jnp.where(m["block_diag"], _bdot_tn(jnp.concatenate([uv.astype(BF16), x], axis=0), en), 0.0)
          for uv, x, en in zip(u_v, vp, ends)]
    for _ in range(chunks):
        previous_tile_chunk()
    for p in pairs:
        st_ref[p] = carried["state"][p]

    for i in range(chunks * N_PAIRS):
        yl_w[i] = y_local[i]
        rm_w[i] = rm[i]
        gm_w[i] = gm[i]
        qm_w[i] = qm[i]
        gam_w[i, 0:1, :] = gammas[i]


def _rwkv_scan(prep, batch, seq, lnx_g, lnx_b):
    t = batch * seq
    chunks = 8
    rows = CHUNK * chunks
    nblk = seq // rows
    n_tiles = batch * nblk
    n_items = chunks * N_PAIRS
    cur = pl.BlockSpec((rows, RWKV_WIDTH), lambda s: (jnp.minimum(s, n_tiles - 1), 0))
    prev = pl.BlockSpec((rows, RWKV_WIDTH), lambda s: (jnp.maximum(s - 1, 0), 0))
    pspec = pl.BlockSpec((1, RWKV_WIDTH), lambda s: (0, 0))
    slot = [pltpu.VMEM((n_items, CHUNK, PAIR_W), F32), pltpu.VMEM((n_items, CHUNK, PAIR_W), F32),
            pltpu.VMEM((n_items, PAIR_W, PAIR_W), F32), pltpu.VMEM((n_items, PAIR_W, PAIR_W), F32),
            pltpu.VMEM((n_items, SUBLANES, PAIR_W), F32)]
    return pl.pallas_call(
        functools.partial(_rwkv_scan_kernel, chunks=chunks, blocks_per_seq=nblk),
        out_shape=jax.ShapeDtypeStruct((t, RWKV_WIDTH), BF16),
        grid=(n_tiles + 1,),
        in_specs=[cur] * 6 + [prev, prev, pspec, pspec],
        out_specs=prev,
        scratch_shapes=[pltpu.VMEM((N_PAIRS, PAIR_W, PAIR_W), F32)] + slot + slot,
        compiler_params=_cparams("arbitrary"),
    )(*prep, lnx_g, lnx_b)


def _merge_kernel(og_ref, ol_ref, or_ref, x_ref, wgt_ref, bgt_ref, pg_ref, pl_ref, pr_ref, wo_ref,
                  lg_ref, lb_ref, xo_ref):
    d = D_MODEL
    xin = x_ref[...].astype(BF16)
    merged = None
    for b, (o_ref, p_ref) in enumerate(((og_ref, pg_ref), (ol_ref, pl_ref), (or_ref, pr_ref))):
        cols = slice(b * d, (b + 1) * d)
        gate = _sigmoid(jnp.dot(xin, wgt_ref[:, cols], preferred_element_type=F32) + bgt_ref[:, cols])
        term = gate * jnp.dot(o_ref[...], p_ref[...], preferred_element_type=F32)
        merged = term if merged is None else merged + term
    z = DEEPNORM_ALPHA * x_ref[...] + _bdot(merged, wo_ref[...])
    out = _layer_norm_rows(z, lg_ref[...], lb_ref[...], LN_EPS)
    xo_ref[...] = out


def _merge(o_gla, o_lru, o_rwkv, x, w_gates, b_gates, p_gla, p_lru, p_rwkv, w_out, ln_g, ln_b):
    t = x.shape[0]
    tm = min(512, t)

    def rows(width):
        return pl.BlockSpec((tm, width), lambda i: (i, 0))

    def full(a):
        return pl.BlockSpec(a.shape, lambda i: (0,) * a.ndim)

    params = (w_gates, b_gates, p_gla, p_lru, p_rwkv, w_out, ln_g, ln_b)
    return pl.pallas_call(
        _merge_kernel,
        out_shape=jax.ShapeDtypeStruct((t, D_MODEL), F32),
        grid=(t // tm,),
        in_specs=[rows(GLA_VW), rows(LRU_WIDTH), rows(RWKV_WIDTH), rows(D_MODEL)] + [full(a) for a in params],
        out_specs=rows(D_MODEL),
        compiler_params=_cparams("parallel"),
    )(o_gla, o_lru, o_rwkv, x, *params)


FFN_SPLIT = 2


def _ffn_kernel(x_ref, wg_ref, wu_ref, wd_ref, lg_ref, lb_ref, xo_ref):
    x = x_ref[...]
    xb = x.astype(BF16)
    tf = wg_ref.shape[1] // FFN_SPLIT
    acc = None
    for f in range(FFN_SPLIT):
        cols = slice(f * tf, (f + 1) * tf)
        g = jnp.dot(xb, wg_ref[:, cols], preferred_element_type=F32)
        u = jnp.dot(xb, wu_ref[:, cols], preferred_element_type=F32)
        part = _bdot(g * _sigmoid(g) * u, wd_ref[cols, :])
        acc = part if acc is None else acc + part
    z = DEEPNORM_ALPHA * x + acc
    xo_ref[...] = _layer_norm_rows(z, lg_ref[...], lb_ref[...], LN_EPS)


def _dense_ffn(x, wg, wu, wd, ln_g, ln_b):
    t = x.shape[0]
    tm = min(512, t)
    rows = pl.BlockSpec((tm, D_MODEL), lambda i: (i, 0))
    resident = lambda a: pl.BlockSpec(a.shape, lambda i: (0,) * a.ndim, pipeline_mode=pl.Buffered(1))
    return pl.pallas_call(
        _ffn_kernel,
        out_shape=jax.ShapeDtypeStruct((t, D_MODEL), F32),
        grid=(t // tm,),
        in_specs=[rows] + [resident(a) for a in (wg, wu, wd, ln_g, ln_b)],
        out_specs=rows,
        compiler_params=_cparams("parallel"),
    )(x, wg, wu, wd, ln_g, ln_b)


def _router_kernel(x_ref, wr_ref, meta_ref, wts_ref, cnt_ref, carry_ref):
    tm = x_ref.shape[0]

    @pl.when(pl.program_id(0) == 0)
    def _():
        carry_ref[...] = jnp.zeros_like(carry_ref)

    lane = lax.broadcasted_iota(I32, (tm, LANES), 1)
    neg = jnp.float32(-jnp.inf)
    logits = jnp.where(lane < N_EXPERTS, _fdot(x_ref[...], wr_ref[...]), neg)
    m1 = jnp.max(logits, axis=-1, keepdims=True)
    e1 = jnp.min(jnp.where(logits == m1, lane, LANES), axis=-1, keepdims=True)
    rest = jnp.where(lane == e1, neg, logits)
    m2 = jnp.max(rest, axis=-1, keepdims=True)
    e2 = jnp.min(jnp.where(rest == m2, lane, LANES), axis=-1, keepdims=True)
    ex = jnp.exp(m2 - m1)
    w1 = 1.0 / (1.0 + ex)
    w2 = ex / (1.0 + ex)

    hot1 = lane == e1
    hot2 = lane == e2
    onehot = jnp.where(hot1 | hot2, 1.0, 0.0)
    row, col = _tri_masks(tm)
    before = (row > col).astype(BF16)
    prefix = jnp.dot(before, onehot.astype(BF16), preferred_element_type=F32) + carry_ref[...]
    rank1 = jnp.sum(jnp.where(hot1, prefix, 0.0), axis=-1, keepdims=True).astype(I32)
    rank2 = jnp.sum(jnp.where(hot2, prefix, 0.0), axis=-1, keepdims=True).astype(I32)
    carry_ref[...] += jnp.sum(onehot, axis=0, keepdims=True)
    cnt_ref[...] = carry_ref[...]

    meta = jnp.where(lane == 0, e1, jnp.where(lane == 1, e2, jnp.where(lane == 2, rank1, rank2)))
    meta_ref[...] = jnp.where(lane < 4, meta, 0)
    wts_ref[...] = jnp.where(lane == 0, w1, jnp.where(lane == 1, w2, 0.0))


def _router(x, wr_pad):
    t = x.shape[0]
    tm = min(512, t)
    return pl.pallas_call(
        _router_kernel,
        out_shape=[jax.ShapeDtypeStruct((t, LANES), I32), jax.ShapeDtypeStruct((t, LANES), F32),
                   jax.ShapeDtypeStruct((1, LANES), F32)],
        grid=(t // tm,),
        in_specs=[pl.BlockSpec((tm, D_MODEL), lambda i: (i, 0)),
                  pl.BlockSpec((D_MODEL, LANES), lambda i: (0, 0))],
        out_specs=[pl.BlockSpec((tm, LANES), lambda i: (i, 0)), pl.BlockSpec((tm, LANES), lambda i: (i, 0)),
                   pl.BlockSpec((1, LANES), lambda i: (0, 0))],
        scratch_shapes=[pltpu.VMEM((1, LANES), F32)],
        compiler_params=_cparams("arbitrary"),
    )(x, wr_pad)


def _dispatch_kernel(pad_start_ref, pad_len_ref, nused_ref, dest_ref, x_ref, *rest):
    n_w = (len(rest) - 4) // 2
    w_refs, xs_ref, wb_refs, (zeros_ref, sem, fill_sem) = rest[:n_w], rest[n_w], rest[n_w + 1:2 * n_w + 1], rest[-3:]
    tm = x_ref.shape[0]
    n_blocks = xs_ref.shape[0] // MOE_ROWS

    @pl.when(pl.program_id(0) == pl.num_programs(0) - 1)
    def _():
        zeros_ref[...] = jnp.zeros_like(zeros_ref)
        pad_copy = lambda row: pltpu.make_async_copy(zeros_ref.at[pl.ds(0, 1)], xs_ref.at[pl.ds(row, 1)], fill_sem)
        blk_copy = lambda blk: pltpu.make_async_copy(
            zeros_ref, xs_ref.at[pl.ds(pl.multiple_of(blk * MOE_ROWS, MOE_ROWS), MOE_ROWS)], fill_sem)

        def for_each(fn):
            for e in range(N_EXPERTS):
                base = pad_start_ref[e]
                lax.fori_loop(0, pad_len_ref[e], lambda r, c: (fn(pad_copy(base + r)), c)[1], 0)
            lax.fori_loop(nused_ref[0], n_blocks, lambda b, c: (fn(blk_copy(b)), c)[1], 0)

        for_each(lambda cp: cp.start())
        for_each(lambda cp: cp.wait())

    def row_copy(r, j):
        return pltpu.make_async_copy(x_ref.at[pl.ds(r, 1)], xs_ref.at[pl.ds(dest_ref[TOP_K * r + j], 1)], sem)

    def start(r, c):
        for j in range(TOP_K):
            row_copy(r, j).start(priority=j)
        return c

    lax.fori_loop(0, tm, start, 0, unroll=8)
    for w_ref, wb_ref in zip(w_refs, wb_refs):
        wb_ref[...] = w_ref[...].astype(wb_ref.dtype)
    for j in range(TOP_K):
        pltpu.make_async_copy(x_ref, xs_ref.at[pl.ds(0, tm)], sem).wait()


def _dispatch(x, dest_flat, pad_start, pad_len, n_used, n_slots, weights):
    t = x.shape[0]
    tm = min(256, t)
    n_steps = t // tm
    flat = [w.reshape(-1, w.shape[-1]) for w in weights]
    slab = lambda w: pl.BlockSpec((w.shape[0] // n_steps, w.shape[1]), lambda i, *_: (i, 0))
    grid_spec = pltpu.PrefetchScalarGridSpec(
        num_scalar_prefetch=3,
        grid=(n_steps,),
        in_specs=[pl.BlockSpec((tm * TOP_K,), lambda i, *_: (i,), memory_space=pltpu.SMEM),
                  pl.BlockSpec((tm, D_MODEL), lambda i, *_: (i, 0))] + [slab(w) for w in flat],
        out_specs=[pl.BlockSpec(memory_space=pl.ANY)] + [slab(w) for w in flat],
        scratch_shapes=[pltpu.VMEM((MOE_ROWS, D_MODEL), F32), pltpu.SemaphoreType.DMA(()),
                        pltpu.SemaphoreType.DMA(())],
    )
    xs, *cast = pl.pallas_call(
        _dispatch_kernel,
        out_shape=[jax.ShapeDtypeStruct((n_slots, D_MODEL), F32)]
        + [jax.ShapeDtypeStruct(w.shape, BF16) for w in flat],
        grid_spec=grid_spec,
        compiler_params=_cparams("arbitrary"),
    )(pad_start, pad_len, n_used, dest_flat, x, *flat)
    return xs, [c.reshape(w.shape) for c, w in zip(cast, weights)]


def _expert_kernel(blk_e_ref, nused_ref, xs_ref, wg_ref, wu_ref, wd_ref, ys_ref, acc_ref):
    del blk_e_ref
    i = pl.program_id(0)
    f = pl.program_id(1)
    last = pl.num_programs(1) - 1
    used = i < nused_ref[0]

    @pl.when(used & (f == 0))
    def _():
        acc_ref[...] = jnp.zeros_like(acc_ref)

    @pl.when(used)
    def _():
        xb = xs_ref[...].astype(BF16)
        g = jnp.dot(xb, wg_ref[0], preferred_element_type=F32)
        u = jnp.dot(xb, wu_ref[0], preferred_element_type=F32)
        acc_ref[...] += _bdot(g * _sigmoid(g) * u, wd_ref[0])

    @pl.when(used & (f == last))
    def _():
        ys_ref[...] = acc_ref[...]

    @pl.when(jnp.logical_not(used) & (f == last))
    def _():
        ys_ref[...] = jnp.zeros_like(ys_ref)


def _experts(xs, blk_e, n_used, wg, wu, wd):
    n_slots = xs.shape[0]
    n_blocks = n_slots // MOE_ROWS
    ff = wg.shape[2]
    n_f = 2
    tf = ff // n_f
    piece = lambda i, f: jnp.where((i & 1) == 1, n_f - 1 - f, f)
    grid_spec = pltpu.PrefetchScalarGridSpec(
        num_scalar_prefetch=2,
        grid=(n_blocks, n_f),
        in_specs=[pl.BlockSpec((MOE_ROWS, D_MODEL), lambda i, f, be, nu: (i, 0)),
                  pl.BlockSpec((1, D_MODEL, tf), lambda i, f, be, nu: (be[i], 0, piece(i, f))),
                  pl.BlockSpec((1, D_MODEL, tf), lambda i, f, be, nu: (be[i], 0, piece(i, f))),
                  pl.BlockSpec((1, tf, D_MODEL), lambda i, f, be, nu: (be[i], piece(i, f), 0))],
        out_specs=pl.BlockSpec((MOE_ROWS, D_MODEL), lambda i, f, be, nu: (i, 0)),
        scratch_shapes=[pltpu.VMEM((MOE_ROWS, D_MODEL), F32)],
    )
    return pl.pallas_call(
        _expert_kernel,
        out_shape=jax.ShapeDtypeStruct((n_slots, D_MODEL), F32),
        grid_spec=grid_spec,
        compiler_params=_cparams("arbitrary", "arbitrary"),
    )(blk_e, n_used, xs, wg, wu, wd)


def _combine_kernel(dest_ref, dest_next_ref, x_ref, wts_ref, ys_ref, lg_ref, lb_ref, xo_ref, buf_ref, sem):
    tm = x_ref.shape[0]
    i = pl.program_id(0)
    n = pl.num_programs(0)

    def gather(idx_ref, slot):
        def start(r, c):
            for j in range(TOP_K):
                pltpu.make_async_copy(ys_ref.at[pl.ds(idx_ref[TOP_K * r + j], 1)],
                                      buf_ref.at[slot, j, pl.ds(r, 1)], sem.at[slot]).start(priority=j)
            return c
        lax.fori_loop(0, tm, start, 0, unroll=8)

    slot = i & 1

    @pl.when(i == 0)
    def _():
        gather(dest_ref, 0)

    @pl.when(i + 1 < n)
    def _():
        gather(dest_next_ref, 1 - slot)

    for j in range(TOP_K):
        pltpu.make_async_copy(ys_ref.at[pl.ds(0, tm)], buf_ref.at[slot, j], sem.at[slot]).wait()
    w = wts_ref[...]
    f = w[:, 0:1] * buf_ref[slot, 0] + w[:, 1:2] * buf_ref[slot, 1]
    z = DEEPNORM_ALPHA * x_ref[...] + f
    xo_ref[...] = _layer_norm_rows(z, lg_ref[...], lb_ref[...], LN_EPS)


def _combine(x, dest_flat, wts, ys, ln_g, ln_b):
    t = x.shape[0]
    tm = min(256, t)
    n_tiles = t // tm
    rows = lambda width: pl.BlockSpec((tm, width), lambda i: (i, 0))
    vec = pl.BlockSpec((1, D_MODEL), lambda i: (0, 0))
    return pl.pallas_call(
        _combine_kernel,
        out_shape=jax.ShapeDtypeStruct((t, D_MODEL), F32),
        grid=(n_tiles,),
        in_specs=[pl.BlockSpec((tm * TOP_K,), lambda i: (i,), memory_space=pltpu.SMEM),
                  pl.BlockSpec((tm * TOP_K,), lambda i: (jnp.minimum(i + 1, n_tiles - 1),),
                               memory_space=pltpu.SMEM),
                  rows(D_MODEL), rows(LANES), pl.BlockSpec(memory_space=pl.ANY), vec, vec],
        out_specs=rows(D_MODEL),
        scratch_shapes=[pltpu.VMEM((2, TOP_K, tm, D_MODEL), F32), pltpu.SemaphoreType.DMA((2,))],
        compiler_params=_cparams("arbitrary"),
    )(dest_flat, dest_flat, x, wts, ys, ln_g, ln_b)


def _moe_ffn(x, w_router, wg, wu, wd, ln_g, ln_b):
    t = x.shape[0]
    wr_pad = jnp.pad(w_router, ((0, 0), (0, LANES - N_EXPERTS)))
    meta, wts, cnt = _router(x, wr_pad)
    counts = cnt[0, :N_EXPERTS].astype(I32)
    padded = (counts + MOE_ROWS - 1) // MOE_ROWS * MOE_ROWS
    seg_end = jnp.cumsum(padded)
    seg_start = seg_end - padded
    n_blocks = (t * TOP_K) // MOE_ROWS + N_EXPERTS
    dest = seg_start[meta[:, 0:TOP_K]] + meta[:, TOP_K:2 * TOP_K]
    dest_flat = dest.reshape(-1).astype(I32)
    blk_start = jnp.arange(n_blocks, dtype=I32) * MOE_ROWS
    blk_e = jnp.minimum(jnp.sum(blk_start[:, None] >= seg_end[None, :], axis=1), N_EXPERTS - 1).astype(I32)
    n_used = (seg_end[-1:] // MOE_ROWS).astype(I32)
    xs, (wg, wu, wd) = _dispatch(x, dest_flat, (seg_start + counts).astype(I32), (padded - counts).astype(I32),
                                 n_used, n_blocks * MOE_ROWS, (wg, wu, wd))
    ys = _experts(xs, blk_e, n_used, wg, wu, wd)
    return _combine(x, dest_flat, wts, ys, ln_g, ln_b)


def _pad_rows(w, rows, at=0):
    out = jnp.zeros((rows, w.shape[1]), w.dtype)
    return out.at[at:at + w.shape[0]].set(w)


def _reorder_in_projection(w, b):
    sizes = (GLA_KW, GLA_KW, GLA_VW, GLA_VW, GLA_DECAY_RANK, LRU_WIDTH, LRU_WIDTH,
             3 * RWKV_WIDTH, RWKV_DECAY_RANK, RWKV_A_RANK, RWKV_GATE_RANK, N_BRANCH * D_MODEL)
    offs = [0]
    for s in sizes:
        offs.append(offs[-1] + s)
    wb = jnp.concatenate([w, b[None, :]], axis=0)
    piece = lambda i: wb[:, offs[i]:offs[i + 1]]
    zeros = lambda n: jnp.zeros((wb.shape[0], n), wb.dtype)
    q, k, v, r, dec, lx, lg, rkv, wl, al, gl, gates = (piece(i) for i in range(len(sizes)))
    small = jnp.concatenate([wl, al, gl, zeros(RW_SMALL_W - SM_GL[0] - RWKV_GATE_RANK)], axis=1)
    groups = {"gla": jnp.concatenate([q, k, v, r, dec, zeros(GLA_DEC_W - GLA_DECAY_RANK)], axis=1),
              "lru": jnp.concatenate([lx, lg], axis=1),
              "rwkv": jnp.concatenate([rkv, small], axis=1),
              "gates": gates}
    return {name: (g[:-1].astype(BF16), g[-1:]) for name, g in groups.items()}


def _block_diag(blocks):
    n, bi, bo = blocks.shape
    eye = jnp.eye(n, dtype=blocks.dtype)
    return (eye[:, None, :, None] * blocks[:, :, None, :]).reshape(n * bi, n * bo)


def kernel(x, w_in, b_in, gla_w_decay_up, gla_b_decay, gla_norm_g, gla_norm_b, lru_conv_w, lru_conv_b, lru_w_r, lru_b_r, lru_w_i, lru_b_i, lru_lambda, rwkv_mu, rwkv_w0, rwkv_w2, rwkv_a0, rwkv_a2, rwkv_g2, rwkv_k_k, rwkv_k_a, rwkv_r_k, rwkv_lnx_g, rwkv_lnx_b, p_gla, p_lru, p_rwkv, w_out, ln_mix_g, ln_mix_b, ffn_w_gate, ffn_w_up, ffn_w_down, moe_w_router, moe_w_gate, moe_w_up, moe_w_down, ln_ffn_g, ln_ffn_b):
    batch, seq, d = x.shape
    t = batch * seq
    xf = x.reshape(t, d)
    row = lambda a: a.reshape(1, -1)
    head_ones = _block_diag(jnp.ones((RWKV_HEADS, RWKV_HEAD, RWKV_HEAD), BF16))
    for l in range(DEPTH):
        proj = _reorder_in_projection(w_in[l], b_in[l])
        w_gates, b_gates = proj["gates"]

        wup = _pad_rows(gla_w_decay_up[l], GLA_DEC_W).astype(BF16)
        o_gla = _gla_branch(xf, batch, seq, *proj["gla"], wup, row(gla_b_decay[l]), row(gla_norm_g[l]),
                            row(gla_norm_b[l]))

        w_ri = jnp.concatenate([_block_diag(lru_w_r[l]), _block_diag(lru_w_i[l])], axis=1).astype(BF16)
        b_ri = jnp.concatenate([lru_b_r[l], lru_b_i[l]])[None, :]
        o_lru = _lru_branch(xf, batch, seq, *proj["lru"], lru_conv_w[l], row(lru_conv_b[l]), w_ri, b_ri,
                            row(jax.nn.softplus(-lru_lambda[l])))

        mu = rwkv_mu[l]
        mu_all = jnp.zeros((3 * RWKV_WIDTH + RW_SMALL_W,), F32).at[0:mu.shape[0]].set(mu)
        prep_params = (*proj["rwkv"], row(mu_all),
                       _pad_rows(rwkv_w2[l], SM_WA[1], 0).astype(BF16),
                       _pad_rows(rwkv_a2[l], SM_WA[1], RWKV_DECAY_RANK).astype(BF16),
                       _pad_rows(rwkv_g2[l], SM_GL[1], 0).astype(BF16),
                       row(rwkv_w0[l]), row(rwkv_a0[l]), row(rwkv_k_k[l]), row(rwkv_k_a[l]),
                       row(rwkv_r_k[l]), head_ones)
        prep = _rwkv_prep(xf, batch, seq, prep_params)
        o_rwkv = _rwkv_scan(prep, batch, seq, row(rwkv_lnx_g[l]), row(rwkv_lnx_b[l]))

        xf = _merge(o_gla, o_lru, o_rwkv, xf, w_gates, b_gates, p_gla[l].astype(BF16), p_lru[l].astype(BF16),
                    p_rwkv[l].astype(BF16), w_out[l].astype(BF16), row(ln_mix_g[l]), row(ln_mix_b[l]))
        i = l // 2
        if l % 2 == 0:
            xf = _dense_ffn(xf, ffn_w_gate[i].astype(BF16), ffn_w_up[i].astype(BF16),
                            ffn_w_down[i].astype(BF16), row(ln_ffn_g[l]), row(ln_ffn_b[l]))
        else:
            xf = _moe_ffn(xf, moe_w_router[i], moe_w_gate[i], moe_w_up[i], moe_w_down[i],
                          row(ln_ffn_g[l]), row(ln_ffn_b[l]))
    return xf.reshape(batch, seq, d)
```

```python
import functools

import jax
import jax.numpy as jnp
from jax import lax
from jax.experimental import pallas as pl
from jax.experimental.pallas import tpu as pltpu

F32 = jnp.float32
BF16 = jnp.bfloat16
I32 = jnp.int32

D_MODEL = 1024
DEPTH = 2
GLA_HEADS, GLA_DK, GLA_DV = 4, 64, 128
GLA_KW, GLA_VW = GLA_HEADS * GLA_DK, GLA_HEADS * GLA_DV
GLA_DECAY_RANK = 16
GLA_TEMP = 16.0
LRU_WIDTH, LRU_BLOCKS, LRU_CONV, LRU_C = 512, 8, 4, 8.0
RWKV_HEAD, RWKV_WIDTH = 64, 512
RWKV_HEADS = RWKV_WIDTH // RWKV_HEAD
RWKV_DECAY_RANK, RWKV_A_RANK, RWKV_GATE_RANK = 64, 64, 160
RWKV_LNX_EPS = 64e-5
N_BRANCH = 3
N_EXPERTS, TOP_K = 8, 2
DEEPNORM_ALPHA = (2 * DEPTH) ** 0.25
LN_EPS = 1e-5

LANES = 128
SUBLANES = 8
VMEM_LIMIT_BYTES = 56 * 1024 * 1024

SM_WA = (0, 128)
SM_GL = (128, 256)
RW_SMALL_W = 512
GLA_DEC_W = LANES
GLA_COLS = {"q": (0, 256), "k": (256, 512), "v": (512, 1024), "r": (1024, 1536), "dec": (1536, 1664)}

CHUNK = 64
INV_BLOCK = 16
MOE_ROWS = 512


def _cparams(*sem):
    return pltpu.CompilerParams(dimension_semantics=sem, vmem_limit_bytes=VMEM_LIMIT_BYTES)


def _sigmoid(x):
    return 1.0 / (1.0 + jnp.exp(-x))


def _softplus(x):
    return jnp.maximum(x, 0.0) + jnp.log(1.0 + jnp.exp(-jnp.abs(x)))


def _bdot(a, b):
    return jnp.dot(a.astype(BF16), b.astype(BF16), preferred_element_type=F32)


def _bdot_nt(a, b):
    return lax.dot_general(a.astype(BF16), b.astype(BF16), (((1,), (1,)), ((), ())),
                           preferred_element_type=F32)


def _bdot_tn(a, b):
    return lax.dot_general(a.astype(BF16), b.astype(BF16), (((0,), (0,)), ((), ())),
                           preferred_element_type=F32)


def _fdot(a, b):
    ah = a.astype(BF16)
    al = (a - ah.astype(F32)).astype(BF16)
    bh = b.astype(BF16)
    bl = (b - bh.astype(F32)).astype(BF16)
    dot = lambda u, v: jnp.dot(u, v, preferred_element_type=F32)
    return dot(al, bh) + dot(ah, bl) + dot(ah, bh)


def _split3(x):
    hi = x.astype(BF16)
    rest = x - hi.astype(F32)
    mid = rest.astype(BF16)
    lo = (rest - mid.astype(F32)).astype(BF16)
    return hi, mid, lo


def _mask_dot(mask_bf16, x):
    hi, mid, lo = _split3(x)
    dot = lambda part: jnp.dot(mask_bf16, part, preferred_element_type=F32)
    return dot(lo) + dot(mid) + dot(hi)


def _dot_mask(x, mask_bf16):
    hi, mid, lo = _split3(x)
    dot = lambda part: jnp.dot(part, mask_bf16, preferred_element_type=F32)
    return dot(lo) + dot(mid) + dot(hi)


def _layer_norm_rows(z, g, b, eps):
    mu = jnp.mean(z, axis=-1, keepdims=True)
    zc = z - mu
    var = jnp.mean(zc * zc, axis=-1, keepdims=True)
    return zc * lax.rsqrt(var + eps) * g + b


def _tri_masks(n):
    row = lax.broadcasted_iota(I32, (n, n), 0)
    col = lax.broadcasted_iota(I32, (n, n), 1)
    return row, col


MXU_COLS = 256


def _gla_kernel(*refs, chunks, blocks_per_seq):
    *io_refs, st_ref, ha_ref, hb_ref = refs
    s = pl.program_id(0)

    @pl.when(s == 0)
    def _():
        hb_ref[...] = jnp.zeros_like(hb_ref)

    @pl.when((s == 0) | (((s + blocks_per_seq - 1) % blocks_per_seq) == 0))
    def _():
        st_ref[...] = jnp.zeros_like(st_ref)

    body = functools.partial(_gla_body, chunks, (*io_refs, st_ref))

    @pl.when((s & 1) == 0)
    def _():
        body(ha_ref, hb_ref)

    @pl.when((s & 1) == 1)
    def _():
        body(hb_ref, ha_ref)


def _gla_body(chunks, refs, h_next, h_ref):
    x_ref, w_ref, b_ref, wup_ref, bdec_ref, ng_ref, nb_ref, o_ref, st_ref = refs
    xb = x_ref[...].astype(BF16)
    width = w_ref.shape[1]
    pieces = iter([slice(c0, min(c0 + MXU_COLS, width)) for c0 in range(0, width, MXU_COLS)])

    def project_piece():
        cols = next(pieces, None)
        if cols is not None:
            h_next[:, cols] = jnp.dot(xb, w_ref[:, cols], preferred_element_type=F32) + b_ref[:, cols]

    q_ref = h_ref.at[:, GLA_COLS["q"][0]:GLA_COLS["q"][1]]
    k_ref = h_ref.at[:, GLA_COLS["k"][0]:GLA_COLS["k"][1]]
    v_ref = h_ref.at[:, GLA_COLS["v"][0]:GLA_COLS["v"][1]]
    r_ref = h_ref.at[:, GLA_COLS["r"][0]:GLA_COLS["r"][1]]
    dec_ref = h_ref.at[:, GLA_COLS["dec"][0]:GLA_COLS["dec"][1]]

    row, col = _tri_masks(CHUNK)
    ltri = (row >= col).astype(BF16)
    kw, vw = 2 * GLA_DK, 2 * GLA_DV
    key_shift, val_shift = GLA_DK.bit_length() - 1, GLA_DV.bit_length() - 1
    iota = lambda shape, axis: lax.broadcasted_iota(I32, shape, axis)
    causal = iota((CHUNK, kw), 0) >= (iota((CHUNK, kw), 1) & (GLA_DK - 1))
    key_diag = (iota((kw, kw), 0) >> key_shift) == (iota((kw, kw), 1) >> key_shift)
    val_diag = (iota((kw, vw), 0) >> key_shift) == (iota((kw, vw), 1) >> val_shift)
    state_diag = (iota((vw, kw), 0) >> val_shift) == (iota((vw, kw), 1) >> key_shift)
    zero = jnp.zeros((), BF16)
    pairs = range(GLA_HEADS // 2)
    qp, vp, k_diag, kep, decays = [], [], [], [], []
    for c in range(chunks):
        project_piece()
        rows = pl.ds(c * CHUNK, CHUNK)
        logits = _bdot(dec_ref[rows, :], wup_ref[...]) + bdec_ref[...]
        log_a = -_softplus(-logits) * (1.0 / GLA_TEMP)
        cum = _mask_dot(ltri, log_a)
        cum_end = cum[CHUNK - 1:CHUNK, :]
        q = q_ref[rows, :] * (GLA_DK ** -0.5)
        k = k_ref[rows, :]
        q_dec = (q * jnp.exp(cum)).astype(BF16)
        k_inv = (k * jnp.exp(-cum)).astype(BF16)
        k_end = (k * jnp.exp(cum_end - cum)).astype(BF16)
        decay_end = jnp.exp(cum_end)
        v = v_ref[rows, :].astype(BF16)
        for p in pairs:
            s = slice(p * kw, (p + 1) * kw)
            qp.append(q_dec[:, s])
            vp.append(v[:, p * vw:(p + 1) * vw])
            k_diag.append(jnp.where(key_diag, jnp.concatenate([k_inv[:, s], k_inv[:, s]], axis=0), zero))
            kep.append(k_end[:, s])
            decays.append(decay_end[:, s])

    for _ in range(chunks):
        project_piece()

    v_diag = [jnp.where(val_diag, jnp.concatenate([x, x], axis=0), zero) for x in vp]
    scores = [jnp.where(causal, _bdot_nt(q, kd), 0.0) for q, kd in zip(qp, k_diag)]
    o_local = [_bdot(s, vd) for s, vd in zip(scores, v_diag)]
    upd = [jnp.where(state_diag, _bdot_tn(x, ke), 0.0) for x, ke in zip(vp, kep)]

    state = [st_ref[p] for p in pairs]
    n_pairs = len(pairs)
    for c in range(chunks):
        rows = pl.ds(c * CHUNK, CHUNK)
        item = slice(c * n_pairs, (c + 1) * n_pairs)
        o = [ol + _bdot_nt(q, st) for ol, q, st in zip(o_local[item], qp[item], state)]
        state = [st * dcy + u for st, dcy, u in zip(state, decays[item], upd[item])]
        outs = []
        for op in o:
            for oh in (op[:, :GLA_DV], op[:, GLA_DV:]):
                mu = jnp.mean(oh, axis=-1, keepdims=True)
                oc = oh - mu
                var = jnp.mean(oc * oc, axis=-1, keepdims=True)
                outs.append(oc * lax.rsqrt(var + LN_EPS))
        y = jnp.concatenate(outs, axis=1) * ng_ref[...] + nb_ref[...]
        r = r_ref[rows, :]
        o_ref[rows, :] = (y * (r * _sigmoid(r))).astype(o_ref.dtype)
    for p in pairs:
        st_ref[p] = state[p]


def _gla_branch(x, batch, seq, w, bias, wup_pad, b_decay, norm_g, norm_b):
    t = batch * seq
    chunks = 8
    rows = CHUNK * chunks
    nblk = seq // rows

    n_tiles = batch * nblk

    def full(a):
        return pl.BlockSpec(a.shape, lambda s: (0,) * a.ndim)

    args = (w, bias, wup_pad, b_decay, norm_g, norm_b)
    return pl.pallas_call(
        functools.partial(_gla_kernel, chunks=chunks, blocks_per_seq=nblk),
        out_shape=jax.ShapeDtypeStruct((t, GLA_VW), BF16),
        grid=(n_tiles + 1,),
        in_specs=[pl.BlockSpec((rows, D_MODEL), lambda s: (jnp.minimum(s, n_tiles - 1), 0))]
        + [full(a) for a in args],
        out_specs=pl.BlockSpec((rows, GLA_VW), lambda s: (jnp.maximum(s - 1, 0), 0)),
        scratch_shapes=[pltpu.VMEM((GLA_HEADS // 2, 2 * GLA_DV, 2 * GLA_DK), F32),
                        pltpu.VMEM((rows, w.shape[1]), F32), pltpu.VMEM((rows, w.shape[1]), F32)],
        compiler_params=_cparams("arbitrary"),
    )(x, *args)


def _lru_kernel(xin_ref, w_ref, b_ref, cw_ref, cb_ref, wri_ref, bri_ref, sp_ref,
                o_ref, xbuf_ref, a_ref, u_ref, hs_ref, gate_ref, h_ref):
    tm = xin_ref.shape[0]
    pad = SUBLANES

    @pl.when(pl.program_id(1) == 0)
    def _():
        xbuf_ref[0:pad, :] = jnp.zeros((pad, LRU_WIDTH), F32)
        h_ref[...] = jnp.zeros_like(h_ref)

    proj = jnp.dot(xin_ref[...].astype(BF16), w_ref[...], preferred_element_type=F32) + b_ref[...]
    x = proj[:, :LRU_WIDTH]
    gate_ref[...] = jax.nn.gelu(proj[:, LRU_WIDTH:])
    xbuf_ref[pad:pad + tm, :] = x
    xc = cb_ref[...] + x * cw_ref[LRU_CONV - 1:LRU_CONV, :]
    for j in range(LRU_CONV - 1):
        back = LRU_CONV - 1 - j
        xc = xc + xbuf_ref[pl.ds(pad - back, tm), :] * cw_ref[j:j + 1, :]
    xbuf_ref[0:pad, :] = x[tm - pad:tm, :]

    ri = _sigmoid(_bdot(xc, wri_ref[...]) + bri_ref[...])
    rg, ig = ri[:, :LRU_WIDTH], ri[:, LRU_WIDTH:]
    log_a = -LRU_C * rg * sp_ref[...]
    a = jnp.exp(log_a)
    a_ref[...] = a
    u_ref[...] = jnp.sqrt(1.0 - a * a) * (ig * xc)

    def step(t, hprev):
        hnew = a_ref[pl.ds(t, 1), :] * hprev + u_ref[pl.ds(t, 1), :]
        hs_ref[pl.ds(t, 1), :] = hnew
        return hnew

    h_ref[...] = lax.fori_loop(0, tm, step, h_ref[...], unroll=8)
    o_ref[...] = (gate_ref[...] * hs_ref[...]).astype(o_ref.dtype)


def _lru_branch(x, batch, seq, w, bias, conv_w, conv_b, w_ri, b_ri, softplus_neg_lam):
    t = batch * seq
    tm = min(512, seq)
    nblk = seq // tm

    def full(a):
        return pl.BlockSpec(a.shape, lambda b, n: (0,) * a.ndim)

    args = (w, bias, conv_w, conv_b, w_ri, b_ri, softplus_neg_lam)
    return pl.pallas_call(
        _lru_kernel,
        out_shape=jax.ShapeDtypeStruct((t, LRU_WIDTH), BF16),
        grid=(batch, nblk),
        in_specs=[pl.BlockSpec((tm, D_MODEL), lambda b, n: (b * nblk + n, 0))] + [full(a) for a in args],
        out_specs=pl.BlockSpec((tm, LRU_WIDTH), lambda b, n: (b * nblk + n, 0)),
        scratch_shapes=[pltpu.VMEM((tm + SUBLANES, LRU_WIDTH), F32)]
        + [pltpu.VMEM((tm, LRU_WIDTH), F32)] * 4 + [pltpu.VMEM((1, LRU_WIDTH), F32)],
        compiler_params=_cparams("parallel", "arbitrary"),
    )(x, *args)


def _rwkv_prep_kernel(*refs, blocks_per_seq):
    *io_refs, ha_ref, hb_ref, carry_ref = refs
    tm = io_refs[0].shape[0]
    s = pl.program_id(0)
    body = functools.partial(_rwkv_prep_body, tm, s, RWKV_WIDTH, blocks_per_seq, (*io_refs, carry_ref))

    @pl.when(s == 0)
    def _():
        hb_ref[...] = jnp.zeros_like(hb_ref)
        carry_ref[...] = jnp.zeros_like(carry_ref)

    @pl.when((s & 1) == 0)
    def _():
        body(ha_ref, hb_ref)

    @pl.when((s & 1) == 1)
    def _():
        body(hb_ref, ha_ref)


def _rwkv_prep_body(tm, s, gw, blocks_per_seq, refs, h_next, h_cur):
    (x_ref, w_ref, b_ref, mu_ref, w2_ref, a2_ref, g2_ref, w0_ref, a0_ref, kk_ref, ka_ref, rk_ref, bd_ref,
     r_out, k_out, v_out, lw_out, kk_out, kka_out, bonus_out, g_out, carry_ref) = refs
    first = ((s + blocks_per_seq - 1) % blocks_per_seq) == 0
    keep = jnp.where(first, 0.0, 1.0)
    row0 = lax.broadcasted_iota(I32, (tm, 1), 0) == 0
    xb = x_ref[...].astype(BF16)

    pw = gw // 2

    def project(j):
        cols = slice(j * pw, (j + 1) * pw)
        h_next[:, cols] = jnp.dot(xb, w_ref[:, cols], preferred_element_type=F32) + b_ref[:, cols]

    def shifted(j):
        cols = slice(j * gw, (j + 1) * gw)
        cur = h_cur[:, cols]
        prev_row = carry_ref[0:1, cols] * keep
        carry_ref[0:1, cols] = cur[tm - 1:tm, :]
        prev = jnp.where(row0, prev_row, pltpu.roll(cur, 1, axis=0))
        return cur + (prev - cur) * mu_ref[:, cols]

    project(0)
    sm = shifted(3)
    wa = sm[:, SM_WA[0]:SM_WA[0] + SM_WA[1]]
    gl = sm[:, SM_GL[0]:SM_GL[0] + SM_GL[1]]
    project(1)
    w_log = -_softplus(-(w0_ref[...] + _bdot(jnp.tanh(wa), w2_ref[...]))) - 0.5
    lw_out[...] = -jnp.exp(w_log)
    project(2)
    a = _sigmoid(a0_ref[...] + _bdot(wa, a2_ref[...]))
    g_out[...] = _bdot(_sigmoid(gl), g2_ref[...])
    project(3)
    k = shifted(1)
    kk = k * kk_ref[...]
    norm = jnp.sqrt(_dot_mask(kk * kk, bd_ref[...]))
    project(4)
    kk = kk / jnp.maximum(norm, 1e-12)
    kk_out[...] = kk
    kka_out[...] = kk * a
    k2 = k * (1.0 + (a - 1.0) * ka_ref[...])
    k_out[...] = k2
    project(5)
    r = shifted(0)
    r_out[...] = r
    project(6)
    v = shifted(2)
    v_out[...] = v.astype(v_out.dtype)
    project(7)
    bonus_out[...] = _dot_mask(r * k2 * rk_ref[...], bd_ref[...]) * v


def _rwkv_prep(x, batch, seq, params):
    t = batch * seq
    tm = min(512, seq)
    bps = seq // tm
    n_tiles = t // tm
    width = params[0].shape[1]

    def full(a):
        return pl.BlockSpec(a.shape, lambda s: (0,) * a.ndim)

    out = lambda dtype: jax.ShapeDtypeStruct((t, RWKV_WIDTH), dtype)
    ospec = pl.BlockSpec((tm, RWKV_WIDTH), lambda s: (jnp.maximum(s - 1, 0), 0))
    out_dtypes = (F32, F32, BF16, F32, F32, F32, F32, F32)
    return pl.pallas_call(
        functools.partial(_rwkv_prep_kernel, blocks_per_seq=bps),
        out_shape=[out(dt) for dt in out_dtypes],
        grid=(n_tiles + 1,),
        in_specs=[pl.BlockSpec((tm, D_MODEL), lambda s: (jnp.minimum(s, n_tiles - 1), 0))]
        + [full(a) for a in params],
        out_specs=[ospec] * 8,
        scratch_shapes=[pltpu.VMEM((tm, width), F32), pltpu.VMEM((tm, width), F32),
                        pltpu.VMEM((SUBLANES, width), F32)],
        compiler_params=_cparams("arbitrary"),
    )(x, *params)


PAIR_W = 2 * RWKV_HEAD
N_PAIRS = RWKV_HEADS // 2
HEAD_SHIFT = RWKV_HEAD.bit_length() - 1


def _pair_masks():
    t = lax.broadcasted_iota(I32, (CHUNK, PAIR_W), 0)
    lane = lax.broadcasted_iota(I32, (CHUNK, PAIR_W), 1)
    j = lane & (RWKV_HEAD - 1)
    r = lax.broadcasted_iota(I32, (PAIR_W, PAIR_W), 0)
    c = lax.broadcasted_iota(I32, (PAIR_W, PAIR_W), 1)
    r4 = lax.broadcasted_iota(I32, (2 * PAIR_W, PAIR_W), 0)
    c4 = lax.broadcasted_iota(I32, (2 * PAIR_W, PAIR_W), 1)
    return dict(
        strict=t > j, incl=t >= j, eye=(t == j).astype(F32),
        same_block=jnp.bitwise_xor(t, j) < INV_BLOCK,
        first_head=lane < RWKV_HEAD,
        block_diag=(r >> HEAD_SHIFT) == (c >> HEAD_SHIFT),
        stacked_diag=((r4 >> HEAD_SHIFT) & 1) == (c4 >> HEAD_SHIFT))


N_SLOT_REFS = 5


def _rwkv_scan_kernel(*refs, chunks, blocks_per_seq):
    *io_refs, st_ref = refs[:-2 * N_SLOT_REFS]
    slot_a, slot_b = refs[-2 * N_SLOT_REFS:-N_SLOT_REFS], refs[-N_SLOT_REFS:]
    s = pl.program_id(0)

    @pl.when(s == 0)
    def _():
        for ref in slot_b:
            ref[...] = jnp.zeros_like(ref)

    @pl.when((s == 0) | (((s + blocks_per_seq - 1) % blocks_per_seq) == 0))
    def _():
        st_ref[...] = jnp.zeros_like(st_ref)

    body = functools.partial(_rwkv_scan_body, chunks, (*io_refs, st_ref))

    @pl.when((s & 1) == 0)
    def _():
        body(slot_a, slot_b)

    @pl.when((s & 1) == 1)
    def _():
        body(slot_b, slot_a)


def _rwkv_scan_body(chunks, refs, wr, rd):
    r_ref, k_ref, v_ref, lw_ref, kk_ref, kka_ref, bonus_ref, g_ref, lg_ref, lb_ref, o_ref, st_ref = refs
    yl_w, rm_w, gm_w, qm_w, gam_w = wr
    yl_r, rm_r, gm_r, qm_r, gam_r = rd
    row, col = _tri_masks(CHUNK)
    ltri = (row >= col).astype(BF16)
    m = _pair_masks()
    zero = jnp.zeros((), BF16)
    pairs = range(N_PAIRS)
    sl = [slice(p * PAIR_W, (p + 1) * PAIR_W) for p in pairs]

    def block_diag(x):
        xb = x.astype(BF16)
        return jnp.where(m["block_diag"], jnp.concatenate([xb, xb], axis=0), zero)

    def pair_mul(lhs, *rhs):
        rb = jnp.concatenate([block_diag(x) for x in rhs], axis=1) if len(rhs) > 1 else block_diag(rhs[0])
        out = jnp.dot(lhs.astype(BF16), rb, preferred_element_type=F32)
        return [out[:, i * PAIR_W:(i + 1) * PAIR_W] for i in range(len(rhs))]

    def each(fn, *lists):
        return [fn(*args) for args in zip(*lists)]

    carried = {"state": [st_ref[p] for p in pairs]}
    pending = iter(range(chunks))
    inv_n = 1.0 / RWKV_HEAD

    def head_stat(z):
        s0 = jnp.sum(jnp.where(m["first_head"], z, 0.0), axis=-1, keepdims=True)
        s1 = jnp.sum(jnp.where(m["first_head"], 0.0, z), axis=-1, keepdims=True)
        return jnp.where(m["first_head"], s0, s1) * inv_n

    def previous_tile_chunk():
        c = next(pending, None)
        if c is None:
            return
        rows = pl.ds(c * CHUNK, CHUNK)
        items = range(c * N_PAIRS, (c + 1) * N_PAIRS)
        state = carried["state"]
        y = [yl_r[i] + _bdot_nt(rm_r[i], st) for i, st in zip(items, state)]
        carried["state"] = [st * gam_r[i, 0:1, :] + _bdot_nt(st, gm_r[i]) + qm_r[i] for i, st in zip(items, state)]
        outs = []
        for yp in y:
            yc = yp - head_stat(yp)
            outs.append(yc * lax.rsqrt(head_stat(yc * yc) + RWKV_LNX_EPS))
        yn = jnp.concatenate(outs, axis=1) * lg_ref[...] + lb_ref[...]
        o_ref[rows, :] = ((yn + bonus_ref[rows, :]) * g_ref[rows, :]).astype(o_ref.dtype)

    lhs, rhs, vp, ends, gammas = [], [], [], [], []
    for c in range(chunks):
        rows = pl.ds(c * CHUNK, CHUNK)
        lw = lw_ref[rows, :]
        cum = _mask_dot(ltri, lw)
        cum_end = cum[CHUNK - 1:CHUNK, :]
        e_inv = jnp.exp(-cum)
        e_end = jnp.exp(cum_end - cum)
        kka = kka_ref[rows, :]
        k = k_ref[rows, :]
        a_bar = (-kk_ref[rows, :] * jnp.exp(cum - lw)).astype(BF16)
        r_bar = (r_ref[rows, :] * jnp.exp(cum)).astype(BF16)
        b_til = (kka * e_inv).astype(BF16)
        k_til = (k * e_inv).astype(BF16)
        b_end = (kka * e_end).astype(BF16)
        k_end = (k * e_end).astype(BF16)
        gamma = jnp.exp(cum_end)
        v = v_ref[rows, :].astype(BF16)

        for s in sl:
            lhs.append(jnp.concatenate([a_bar[:, s], r_bar[:, s]], axis=0))
            rhs.append(jnp.where(m["stacked_diag"], jnp.concatenate(
                [b_til[:, s], b_til[:, s], k_til[:, s], k_til[:, s]], axis=0), zero))
            vp.append(v[:, s])
            ends.append(jnp.concatenate([b_end[:, s], k_end[:, s]], axis=0))
            gammas.append(gamma[:, s])

    blocks = each(_bdot_nt, lhs, rhs)
    previous_tile_chunk()
    a_ab = [jnp.where(m["strict"], b[:CHUNK, :PAIR_W], 0.0) for b in blocks]
    a_ak = [jnp.where(m["strict"], b[:CHUNK, PAIR_W:], 0.0) for b in blocks]
    r_ab = [jnp.where(m["incl"], b[CHUNK:, :PAIR_W], 0.0) for b in blocks]
    r_ak = [jnp.where(m["incl"], b[CHUNK:, PAIR_W:], 0.0) for b in blocks]
    from_v = [jnp.dot(jnp.concatenate([ak, rk], axis=0).astype(BF16), block_diag(x),
                      preferred_element_type=F32) for ak, rk, x in zip(a_ak, r_ak, vp)]
    previous_tile_chunk()
    d = [jnp.where(m["same_block"], a, 0.0) for a in a_ab]
    low = [a - di for a, di in zip(a_ab, d)]
    pm = [m["eye"] + di for di in d]
    dpow = [pair_mul(di, di)[0] for di in d]
    previous_tile_chunk()
    for _ in range(INV_BLOCK.bit_length() - 3):
        both = each(lambda x, p: pair_mul(x, x, p), dpow, pm)
        dpow = [b[0] for b in both]
        pm = [p + b[1] for p, b in zip(pm, both)]
        previous_tile_chunk()
    pm = [p + pair_mul(x, p)[0] for p, x in zip(pm, dpow)]
    previous_tile_chunk()
    both = each(lambda p, lo, fv, l: pair_mul(p, lo, fv[:CHUNK], l[:CHUNK]), pm, low, from_v, lhs)
    previous_tile_chunk()
    both2 = each(lambda b: pair_mul(b[0], b[0], b[1], b[2]), both)
    previous_tile_chunk()
    xw_v = [b[1] + b2[1] for b, b2 in zip(both, both2)]
    xw_a = [b[2] + b2[2] for b, b2 in zip(both, both2)]
    both3 = each(lambda b2, xv, xa: pair_mul(b2[0], xv, xa), both2, xw_v, xw_a)
    u_v = [xv + b3[0] for xv, b3 in zip(xw_v, both3)]
    wm = [xa + b3[1] for xa, b3 in zip(xw_a, both3)]
    both4 = each(lambda rb, uv, w: pair_mul(rb, uv, w), r_ab, u_v, wm)
    y_local = [fv[CHUNK:] + b4[0] for fv, b4 in zip(from_v, both4)]
    rm = [l[CHUNK:] + b4[1] for l, b4 in zip(lhs, both4)]
    gm = [jnp.where(m["block_diag"], _bdot_tn(en[:CHUNK], w), 0.0) for en, w in zip(ends, wm)]
    qm = [jnp.where(m["block_diag"], _bdot_tn(jnp.concatenate([uv.astype(BF16), x], axis=0), en), 0.0)
          for uv, x, en in zip(u_v, vp, ends)]
    for _ in range(chunks):
        previous_tile_chunk()
    for p in pairs:
        st_ref[p] = carried["state"][p]

    for i in range(chunks * N_PAIRS):
        yl_w[i] = y_local[i]
        rm_w[i] = rm[i]
        gm_w[i] = gm[i]
        qm_w[i] = qm[i]
        gam_w[i, 0:1, :] = gammas[i]


def _rwkv_scan(prep, batch, seq, lnx_g, lnx_b):
    t = batch * seq
    chunks = 8
    rows = CHUNK * chunks
    nblk = seq // rows
    n_tiles = batch * nblk
    n_items = chunks * N_PAIRS
    cur = pl.BlockSpec((rows, RWKV_WIDTH), lambda s: (jnp.minimum(s, n_tiles - 1), 0))
    prev = pl.BlockSpec((rows, RWKV_WIDTH), lambda s: (jnp.maximum(s - 1, 0), 0))
    pspec = pl.BlockSpec((1, RWKV_WIDTH), lambda s: (0, 0))
    slot = [pltpu.VMEM((n_items, CHUNK, PAIR_W), F32), pltpu.VMEM((n_items, CHUNK, PAIR_W), F32),
            pltpu.VMEM((n_items, PAIR_W, PAIR_W), F32), pltpu.VMEM((n_items, PAIR_W, PAIR_W), F32),
            pltpu.VMEM((n_items, SUBLANES, PAIR_W), F32)]
    return pl.pallas_call(
        functools.partial(_rwkv_scan_kernel, chunks=chunks, blocks_per_seq=nblk),
        out_shape=jax.ShapeDtypeStruct((t, RWKV_WIDTH), BF16),
        grid=(n_tiles + 1,),
        in_specs=[cur] * 6 + [prev, prev, pspec, pspec],
        out_specs=prev,
        scratch_shapes=[pltpu.VMEM((N_PAIRS, PAIR_W, PAIR_W), F32)] + slot + slot,
        compiler_params=_cparams("arbitrary"),
    )(*prep, lnx_g, lnx_b)


def _merge_kernel(og_ref, ol_ref, or_ref, x_ref, wgt_ref, bgt_ref, pg_ref, pl_ref, pr_ref, wo_ref,
                  lg_ref, lb_ref, xo_ref):
    d = D_MODEL
    xin = x_ref[...].astype(BF16)
    merged = None
    for b, (o_ref, p_ref) in enumerate(((og_ref, pg_ref), (ol_ref, pl_ref), (or_ref, pr_ref))):
        cols = slice(b * d, (b + 1) * d)
        gate = _sigmoid(jnp.dot(xin, wgt_ref[:, cols], preferred_element_type=F32) + bgt_ref[:, cols])
        term = gate * jnp.dot(o_ref[...], p_ref[...], preferred_element_type=F32)
        merged = term if merged is None else merged + term
    z = DEEPNORM_ALPHA * x_ref[...] + _bdot(merged, wo_ref[...])
    out = _layer_norm_rows(z, lg_ref[...], lb_ref[...], LN_EPS)
    xo_ref[...] = out


def _merge(o_gla, o_lru, o_rwkv, x, w_gates, b_gates, p_gla, p_lru, p_rwkv, w_out, ln_g, ln_b):
    t = x.shape[0]
    tm = min(1024, t)

    def rows(width):
        return pl.BlockSpec((tm, width), lambda i: (i, 0))

    def full(a):
        return pl.BlockSpec(a.shape, lambda i: (0,) * a.ndim, pipeline_mode=pl.Buffered(1))

    params = (w_gates, b_gates, p_gla, p_lru, p_rwkv, w_out, ln_g, ln_b)
    return pl.pallas_call(
        _merge_kernel,
        out_shape=jax.ShapeDtypeStruct((t, D_MODEL), F32),
        grid=(t // tm,),
        in_specs=[rows(GLA_VW), rows(LRU_WIDTH), rows(RWKV_WIDTH), rows(D_MODEL)] + [full(a) for a in params],
        out_specs=rows(D_MODEL),
        compiler_params=_cparams("parallel"),
    )(o_gla, o_lru, o_rwkv, x, *params)


FFN_SPLIT = 2


def _ffn_kernel(x_ref, wg_ref, wu_ref, wd_ref, lg_ref, lb_ref, xo_ref):
    x = x_ref[...]
    xb = x.astype(BF16)
    tf = wg_ref.shape[1] // FFN_SPLIT
    acc = None
    for f in range(FFN_SPLIT):
        cols = slice(f * tf, (f + 1) * tf)
        g = jnp.dot(xb, wg_ref[:, cols], preferred_element_type=F32)
        u = jnp.dot(xb, wu_ref[:, cols], preferred_element_type=F32)
        part = _bdot(g * _sigmoid(g) * u, wd_ref[cols, :])
        acc = part if acc is None else acc + part
    z = DEEPNORM_ALPHA * x + acc
    xo_ref[...] = _layer_norm_rows(z, lg_ref[...], lb_ref[...], LN_EPS)


def _dense_ffn(x, wg, wu, wd, ln_g, ln_b):
    t = x.shape[0]
    tm = min(1024, t)
    rows = pl.BlockSpec((tm, D_MODEL), lambda i: (i, 0))
    resident = lambda a: pl.BlockSpec(a.shape, lambda i: (0,) * a.ndim, pipeline_mode=pl.Buffered(1))
    return pl.pallas_call(
        _ffn_kernel,
        out_shape=jax.ShapeDtypeStruct((t, D_MODEL), F32),
        grid=(t // tm,),
        in_specs=[rows] + [resident(a) for a in (wg, wu, wd, ln_g, ln_b)],
        out_specs=rows,
        compiler_params=_cparams("parallel"),
    )(x, wg, wu, wd, ln_g, ln_b)


def _router_kernel(x_ref, wr_ref, meta_ref, wts_ref, cnt_ref, carry_ref):
    tm = x_ref.shape[0]

    @pl.when(pl.program_id(0) == 0)
    def _():
        carry_ref[...] = jnp.zeros_like(carry_ref)

    lane = lax.broadcasted_iota(I32, (tm, LANES), 1)
    neg = jnp.float32(-jnp.inf)
    logits = jnp.where(lane < N_EXPERTS, _fdot(x_ref[...], wr_ref[...]), neg)
    m1 = jnp.max(logits, axis=-1, keepdims=True)
    e1 = jnp.min(jnp.where(logits == m1, lane, LANES), axis=-1, keepdims=True)
    rest = jnp.where(lane == e1, neg, logits)
    m2 = jnp.max(rest, axis=-1, keepdims=True)
    e2 = jnp.min(jnp.where(rest == m2, lane, LANES), axis=-1, keepdims=True)
    ex = jnp.exp(m2 - m1)
    w1 = 1.0 / (1.0 + ex)
    w2 = ex / (1.0 + ex)

    hot1 = lane == e1
    hot2 = lane == e2
    onehot = jnp.where(hot1 | hot2, 1.0, 0.0)
    row, col = _tri_masks(tm)
    before = (row > col).astype(BF16)
    prefix = jnp.dot(before, onehot.astype(BF16), preferred_element_type=F32) + carry_ref[...]
    rank1 = jnp.sum(jnp.where(hot1, prefix, 0.0), axis=-1, keepdims=True).astype(I32)
    rank2 = jnp.sum(jnp.where(hot2, prefix, 0.0), axis=-1, keepdims=True).astype(I32)
    carry_ref[...] += jnp.sum(onehot, axis=0, keepdims=True)
    cnt_ref[...] = carry_ref[...]

    meta = jnp.where(lane == 0, e1, jnp.where(lane == 1, e2, jnp.where(lane == 2, rank1, rank2)))
    meta_ref[...] = jnp.where(lane < 4, meta, 0)
    wts_ref[...] = jnp.where(lane == 0, w1, jnp.where(lane == 1, w2, 0.0))


def _router(x, wr_pad):
    t = x.shape[0]
    tm = min(512, t)
    return pl.pallas_call(
        _router_kernel,
        out_shape=[jax.ShapeDtypeStruct((t, LANES), I32), jax.ShapeDtypeStruct((t, LANES), F32),
                   jax.ShapeDtypeStruct((1, LANES), F32)],
        grid=(t // tm,),
        in_specs=[pl.BlockSpec((tm, D_MODEL), lambda i: (i, 0)),
                  pl.BlockSpec((D_MODEL, LANES), lambda i: (0, 0))],
        out_specs=[pl.BlockSpec((tm, LANES), lambda i: (i, 0)), pl.BlockSpec((tm, LANES), lambda i: (i, 0)),
                   pl.BlockSpec((1, LANES), lambda i: (0, 0))],
        scratch_shapes=[pltpu.VMEM((1, LANES), F32)],
        compiler_params=_cparams("arbitrary"),
    )(x, wr_pad)


def _dispatch_kernel(pad_start_ref, pad_len_ref, nused_ref, dest_ref, x_ref, *rest):
    n_w = (len(rest) - 4) // 2
    w_refs, xs_ref, wb_refs, (zeros_ref, sem, fill_sem) = rest[:n_w], rest[n_w], rest[n_w + 1:2 * n_w + 1], rest[-3:]
    tm = x_ref.shape[0]
    n_blocks = xs_ref.shape[0] // MOE_ROWS

    @pl.when(pl.program_id(0) == pl.num_programs(0) - 1)
    def _():
        zeros_ref[...] = jnp.zeros_like(zeros_ref)
        pad_copy = lambda row: pltpu.make_async_copy(zeros_ref.at[pl.ds(0, 1)], xs_ref.at[pl.ds(row, 1)], fill_sem)
        blk_copy = lambda blk: pltpu.make_async_copy(
            zeros_ref, xs_ref.at[pl.ds(pl.multiple_of(blk * MOE_ROWS, MOE_ROWS), MOE_ROWS)], fill_sem)

        def for_each(fn):
            for e in range(N_EXPERTS):
                base = pad_start_ref[e]
                lax.fori_loop(0, pad_len_ref[e], lambda r, c: (fn(pad_copy(base + r)), c)[1], 0)
            lax.fori_loop(nused_ref[0], n_blocks, lambda b, c: (fn(blk_copy(b)), c)[1], 0)

        for_each(lambda cp: cp.start())
        for_each(lambda cp: cp.wait())

    def row_copy(r, j):
        return pltpu.make_async_copy(x_ref.at[pl.ds(r, 1)], xs_ref.at[pl.ds(dest_ref[TOP_K * r + j], 1)], sem)

    def start(r, c):
        for j in range(TOP_K):
            row_copy(r, j).start(priority=j)
        return c

    lax.fori_loop(0, tm, start, 0, unroll=8)
    for w_ref, wb_ref in zip(w_refs, wb_refs):
        wb_ref[...] = w_ref[...].astype(wb_ref.dtype)
    for j in range(TOP_K):
        pltpu.make_async_copy(x_ref, xs_ref.at[pl.ds(0, tm)], sem).wait()


def _dispatch(x, dest_flat, pad_start, pad_len, n_used, n_slots, weights):
    t = x.shape[0]
    tm = min(256, t)
    n_steps = t // tm
    flat = [w.reshape(-1, w.shape[-1]) for w in weights]
    slab = lambda w: pl.BlockSpec((w.shape[0] // n_steps, w.shape[1]), lambda i, *_: (i, 0))
    grid_spec = pltpu.PrefetchScalarGridSpec(
        num_scalar_prefetch=3,
        grid=(n_steps,),
        in_specs=[pl.BlockSpec((tm * TOP_K,), lambda i, *_: (i,), memory_space=pltpu.SMEM),
                  pl.BlockSpec((tm, D_MODEL), lambda i, *_: (i, 0))] + [slab(w) for w in flat],
        out_specs=[pl.BlockSpec(memory_space=pl.ANY)] + [slab(w) for w in flat],
        scratch_shapes=[pltpu.VMEM((MOE_ROWS, D_MODEL), F32), pltpu.SemaphoreType.DMA(()),
                        pltpu.SemaphoreType.DMA(())],
    )
    xs, *cast = pl.pallas_call(
        _dispatch_kernel,
        out_shape=[jax.ShapeDtypeStruct((n_slots, D_MODEL), F32)]
        + [jax.ShapeDtypeStruct(w.shape, BF16) for w in flat],
        grid_spec=grid_spec,
        compiler_params=_cparams("arbitrary"),
    )(pad_start, pad_len, n_used, dest_flat, x, *flat)
    return xs, [c.reshape(w.shape) for c, w in zip(cast, weights)]


def _expert_kernel(blk_e_ref, nused_ref, xs_ref, wg_ref, wu_ref, wd_ref, ys_ref, acc_ref):
    del blk_e_ref
    i = pl.program_id(0)
    f = pl.program_id(1)
    last = pl.num_programs(1) - 1
    used = i < nused_ref[0]

    @pl.when(used & (f == 0))
    def _():
        acc_ref[...] = jnp.zeros_like(acc_ref)

    @pl.when(used)
    def _():
        xb = xs_ref[...].astype(BF16)
        g = jnp.dot(xb, wg_ref[0], preferred_element_type=F32)
        u = jnp.dot(xb, wu_ref[0], preferred_element_type=F32)
        acc_ref[...] += _bdot(g * _sigmoid(g) * u, wd_ref[0])

    @pl.when(used & (f == last))
    def _():
        ys_ref[...] = acc_ref[...]

    @pl.when(jnp.logical_not(used) & (f == last))
    def _():
        ys_ref[...] = jnp.zeros_like(ys_ref)


def _experts(xs, blk_e, n_used, wg, wu, wd):
    n_slots = xs.shape[0]
    n_blocks = n_slots // MOE_ROWS
    ff = wg.shape[2]
    n_f = 2
    tf = ff // n_f
    piece = lambda i, f: jnp.where((i & 1) == 1, n_f - 1 - f, f)
    grid_spec = pltpu.PrefetchScalarGridSpec(
        num_scalar_prefetch=2,
        grid=(n_blocks, n_f),
        in_specs=[pl.BlockSpec((MOE_ROWS, D_MODEL), lambda i, f, be, nu: (i, 0)),
                  pl.BlockSpec((1, D_MODEL, tf), lambda i, f, be, nu: (be[i], 0, piece(i, f))),
                  pl.BlockSpec((1, D_MODEL, tf), lambda i, f, be, nu: (be[i], 0, piece(i, f))),
                  pl.BlockSpec((1, tf, D_MODEL), lambda i, f, be, nu: (be[i], piece(i, f), 0))],
        out_specs=pl.BlockSpec((MOE_ROWS, D_MODEL), lambda i, f, be, nu: (i, 0)),
        scratch_shapes=[pltpu.VMEM((MOE_ROWS, D_MODEL), F32)],
    )
    return pl.pallas_call(
        _expert_kernel,
        out_shape=jax.ShapeDtypeStruct((n_slots, D_MODEL), F32),
        grid_spec=grid_spec,
        compiler_params=_cparams("arbitrary", "arbitrary"),
    )(blk_e, n_used, xs, wg, wu, wd)


def _combine_kernel(dest_ref, dest_next_ref, x_ref, wts_ref, ys_ref, lg_ref, lb_ref, xo_ref, buf_ref, sem):
    tm = x_ref.shape[0]
    i = pl.program_id(0)
    n = pl.num_programs(0)

    def gather(idx_ref, slot):
        def start(r, c):
            for j in range(TOP_K):
                pltpu.make_async_copy(ys_ref.at[pl.ds(idx_ref[TOP_K * r + j], 1)],
                                      buf_ref.at[slot, j, pl.ds(r, 1)], sem.at[slot]).start(priority=j)
            return c
        lax.fori_loop(0, tm, start, 0, unroll=8)

    slot = i & 1

    @pl.when(i == 0)
    def _():
        gather(dest_ref, 0)

    @pl.when(i + 1 < n)
    def _():
        gather(dest_next_ref, 1 - slot)

    for j in range(TOP_K):
        pltpu.make_async_copy(ys_ref.at[pl.ds(0, tm)], buf_ref.at[slot, j], sem.at[slot]).wait()
    w = wts_ref[...]
    f = w[:, 0:1] * buf_ref[slot, 0] + w[:, 1:2] * buf_ref[slot, 1]
    z = DEEPNORM_ALPHA * x_ref[...] + f
    xo_ref[...] = _layer_norm_rows(z, lg_ref[...], lb_ref[...], LN_EPS)


def _combine(x, dest_flat, wts, ys, ln_g, ln_b):
    t = x.shape[0]
    tm = min(512, t)
    n_tiles = t // tm
    rows = lambda width: pl.BlockSpec((tm, width), lambda i: (i, 0))
    vec = pl.BlockSpec((1, D_MODEL), lambda i: (0, 0))
    return pl.pallas_call(
        _combine_kernel,
        out_shape=jax.ShapeDtypeStruct((t, D_MODEL), F32),
        grid=(n_tiles,),
        in_specs=[pl.BlockSpec((tm * TOP_K,), lambda i: (i,), memory_space=pltpu.SMEM),
                  pl.BlockSpec((tm * TOP_K,), lambda i: (jnp.minimum(i + 1, n_tiles - 1),),
                               memory_space=pltpu.SMEM),
                  rows(D_MODEL), rows(LANES), pl.BlockSpec(memory_space=pl.ANY), vec, vec],
        out_specs=rows(D_MODEL),
        scratch_shapes=[pltpu.VMEM((2, TOP_K, tm, D_MODEL), F32), pltpu.SemaphoreType.DMA((2,))],
        compiler_params=_cparams("arbitrary"),
    )(dest_flat, dest_flat, x, wts, ys, ln_g, ln_b)


def _moe_ffn(x, w_router, wg, wu, wd, ln_g, ln_b):
    t = x.shape[0]
    wr_pad = jnp.pad(w_router, ((0, 0), (0, LANES - N_EXPERTS)))
    meta, wts, cnt = _router(x, wr_pad)
    counts = cnt[0, :N_EXPERTS].astype(I32)
    padded = (counts + MOE_ROWS - 1) // MOE_ROWS * MOE_ROWS
    seg_end = jnp.cumsum(padded)
    seg_start = seg_end - padded
    n_blocks = (t * TOP_K) // MOE_ROWS + N_EXPERTS
    dest = seg_start[meta[:, 0:TOP_K]] + meta[:, TOP_K:2 * TOP_K]
    dest_flat = dest.reshape(-1).astype(I32)
    blk_start = jnp.arange(n_blocks, dtype=I32) * MOE_ROWS
    blk_e = jnp.minimum(jnp.sum(blk_start[:, None] >= seg_end[None, :], axis=1), N_EXPERTS - 1).astype(I32)
    n_used = (seg_end[-1:] // MOE_ROWS).astype(I32)
    xs, (wg, wu, wd) = _dispatch(x, dest_flat, (seg_start + counts).astype(I32), (padded - counts).astype(I32),
                                 n_used, n_blocks * MOE_ROWS, (wg, wu, wd))
    ys = _experts(xs, blk_e, n_used, wg, wu, wd)
    return _combine(x, dest_flat, wts, ys, ln_g, ln_b)


def _pad_rows(w, rows, at=0):
    out = jnp.zeros((rows, w.shape[1]), w.dtype)
    return out.at[at:at + w.shape[0]].set(w)


def _reorder_in_projection(w, b):
    sizes = (GLA_KW, GLA_KW, GLA_VW, GLA_VW, GLA_DECAY_RANK, LRU_WIDTH, LRU_WIDTH,
             3 * RWKV_WIDTH, RWKV_DECAY_RANK, RWKV_A_RANK, RWKV_GATE_RANK, N_BRANCH * D_MODEL)
    offs = [0]
    for s in sizes:
        offs.append(offs[-1] + s)
    wb = jnp.concatenate([w, b[None, :]], axis=0)
    piece = lambda i: wb[:, offs[i]:offs[i + 1]]
    zeros = lambda n: jnp.zeros((wb.shape[0], n), wb.dtype)
    q, k, v, r, dec, lx, lg, rkv, wl, al, gl, gates = (piece(i) for i in range(len(sizes)))
    small = jnp.concatenate([wl, al, gl, zeros(RW_SMALL_W - SM_GL[0] - RWKV_GATE_RANK)], axis=1)
    groups = {"gla": jnp.concatenate([q, k, v, r, dec, zeros(GLA_DEC_W - GLA_DECAY_RANK)], axis=1),
              "lru": jnp.concatenate([lx, lg], axis=1),
              "rwkv": jnp.concatenate([rkv, small], axis=1),
              "gates": gates}
    return {name: (g[:-1].astype(BF16), g[-1:]) for name, g in groups.items()}


def _block_diag(blocks):
    n, bi, bo = blocks.shape
    eye = jnp.eye(n, dtype=blocks.dtype)
    return (eye[:, None, :, None] * blocks[:, :, None, :]).reshape(n * bi, n * bo)


def kernel(x, w_in, b_in, gla_w_decay_up, gla_b_decay, gla_norm_g, gla_norm_b, lru_conv_w, lru_conv_b, lru_w_r, lru_b_r, lru_w_i, lru_b_i, lru_lambda, rwkv_mu, rwkv_w0, rwkv_w2, rwkv_a0, rwkv_a2, rwkv_g2, rwkv_k_k, rwkv_k_a, rwkv_r_k, rwkv_lnx_g, rwkv_lnx_b, p_gla, p_lru, p_rwkv, w_out, ln_mix_g, ln_mix_b, ffn_w_gate, ffn_w_up, ffn_w_down, moe_w_router, moe_w_gate, moe_w_up, moe_w_down, ln_ffn_g, ln_ffn_b):
    batch, seq, d = x.shape
    t = batch * seq
    xf = x.reshape(t, d)
    row = lambda a: a.reshape(1, -1)
    head_ones = _block_diag(jnp.ones((RWKV_HEADS, RWKV_HEAD, RWKV_HEAD), BF16))
    for l in range(DEPTH):
        proj = _reorder_in_projection(w_in[l], b_in[l])
        w_gates, b_gates = proj["gates"]

        wup = _pad_rows(gla_w_decay_up[l], GLA_DEC_W).astype(BF16)
        o_gla = _gla_branch(xf, batch, seq, *proj["gla"], wup, row(gla_b_decay[l]), row(gla_norm_g[l]),
                            row(gla_norm_b[l]))

        w_ri = jnp.concatenate([_block_diag(lru_w_r[l]), _block_diag(lru_w_i[l])], axis=1).astype(BF16)
        b_ri = jnp.concatenate([lru_b_r[l], lru_b_i[l]])[None, :]
        o_lru = _lru_branch(xf, batch, seq, *proj["lru"], lru_conv_w[l], row(lru_conv_b[l]), w_ri, b_ri,
                            row(jax.nn.softplus(-lru_lambda[l])))

        mu = rwkv_mu[l]
        mu_all = jnp.zeros((3 * RWKV_WIDTH + RW_SMALL_W,), F32).at[0:mu.shape[0]].set(mu)
        prep_params = (*proj["rwkv"], row(mu_all),
                       _pad_rows(rwkv_w2[l], SM_WA[1], 0).astype(BF16),
                       _pad_rows(rwkv_a2[l], SM_WA[1], RWKV_DECAY_RANK).astype(BF16),
                       _pad_rows(rwkv_g2[l], SM_GL[1], 0).astype(BF16),
                       row(rwkv_w0[l]), row(rwkv_a0[l]), row(rwkv_k_k[l]), row(rwkv_k_a[l]),
                       row(rwkv_r_k[l]), head_ones)
        prep = _rwkv_prep(xf, batch, seq, prep_params)
        o_rwkv = _rwkv_scan(prep, batch, seq, row(rwkv_lnx_g[l]), row(rwkv_lnx_b[l]))

        xf = _merge(o_gla, o_lru, o_rwkv, xf, w_gates, b_gates, p_gla[l].astype(BF16), p_lru[l].astype(BF16),
                    p_rwkv[l].astype(BF16), w_out[l].astype(BF16), row(ln_mix_g[l]), row(ln_mix_b[l]))
        i = l // 2
        if l % 2 == 0:
            xf = _dense_ffn(xf, ffn_w_gate[i].astype(BF16), ffn_w_up[i].astype(BF16),
                            ffn_w_down[i].astype(BF16), row(ln_ffn_g[l]), row(ln_ffn_b[l]))
        else:
            xf = _moe_ffn(xf, moe_w_router[i], moe_w_gate[i], moe_w_up[i], moe_w_down[i],
                          row(ln_ffn_g[l]), row(ln_ffn_b[l]))
    return xf.reshape(batch, seq, d)
```

```python
import functools

import jax
import jax.numpy as jnp
from jax import lax
from jax.experimental import pallas as pl
from jax.experimental.pallas import tpu as pltpu

F32 = jnp.float32
BF16 = jnp.bfloat16
I32 = jnp.int32

D_MODEL = 1024
DEPTH = 2
GLA_HEADS, GLA_DK, GLA_DV = 4, 64, 128
GLA_KW, GLA_VW = GLA_HEADS * GLA_DK, GLA_HEADS * GLA_DV
GLA_DECAY_RANK = 16
GLA_TEMP = 16.0
LRU_WIDTH, LRU_BLOCKS, LRU_CONV, LRU_C = 512, 8, 4, 8.0
RWKV_HEAD, RWKV_WIDTH = 64, 512
RWKV_HEADS = RWKV_WIDTH // RWKV_HEAD
RWKV_DECAY_RANK, RWKV_A_RANK, RWKV_GATE_RANK = 64, 64, 160
RWKV_LNX_EPS = 64e-5
N_BRANCH = 3
N_EXPERTS, TOP_K = 8, 2
DEEPNORM_ALPHA = (2 * DEPTH) ** 0.25
LN_EPS = 1e-5

LANES = 128
SUBLANES = 8
VMEM_LIMIT_BYTES = 56 * 1024 * 1024

SM_WA = (0, 128)
SM_GL = (128, 256)
RW_SMALL_W = 512
GLA_DEC_W = LANES
GLA_COLS = {"q": (0, 256), "k": (256, 512), "v": (512, 1024), "r": (1024, 1536), "dec": (1536, 1664)}

CHUNK = 64
INV_BLOCK = 16
MOE_ROWS = 512


def _cparams(*sem):
    return pltpu.CompilerParams(dimension_semantics=sem, vmem_limit_bytes=VMEM_LIMIT_BYTES)


def _sigmoid(x):
    return 1.0 / (1.0 + jnp.exp(-x))


def _softplus(x):
    return jnp.maximum(x, 0.0) + jnp.log(1.0 + jnp.exp(-jnp.abs(x)))


def _bdot(a, b):
    return jnp.dot(a.astype(BF16), b.astype(BF16), preferred_element_type=F32)


def _bdot_nt(a, b):
    return lax.dot_general(a.astype(BF16), b.astype(BF16), (((1,), (1,)), ((), ())),
                           preferred_element_type=F32)


def _bdot_tn(a, b):
    return lax.dot_general(a.astype(BF16), b.astype(BF16), (((0,), (0,)), ((), ())),
                           preferred_element_type=F32)


def _fdot(a, b):
    ah = a.astype(BF16)
    al = (a - ah.astype(F32)).astype(BF16)
    bh = b.astype(BF16)
    bl = (b - bh.astype(F32)).astype(BF16)
    dot = lambda u, v: jnp.dot(u, v, preferred_element_type=F32)
    return dot(al, bh) + dot(ah, bl) + dot(ah, bh)


def _split3(x):
    hi = x.astype(BF16)
    rest = x - hi.astype(F32)
    mid = rest.astype(BF16)
    lo = (rest - mid.astype(F32)).astype(BF16)
    return hi, mid, lo


def _mask_dot(mask_bf16, x):
    hi, mid, lo = _split3(x)
    dot = lambda part: jnp.dot(mask_bf16, part, preferred_element_type=F32)
    return dot(lo) + dot(mid) + dot(hi)


def _dot_mask(x, mask_bf16):
    hi, mid, lo = _split3(x)
    dot = lambda part: jnp.dot(part, mask_bf16, preferred_element_type=F32)
    return dot(lo) + dot(mid) + dot(hi)


def _layer_norm_rows(z, g, b, eps):
    mu = jnp.mean(z, axis=-1, keepdims=True)
    zc = z - mu
    var = jnp.mean(zc * zc, axis=-1, keepdims=True)
    return zc * lax.rsqrt(var + eps) * g + b


def _tri_masks(n):
    row = lax.broadcasted_iota(I32, (n, n), 0)
    col = lax.broadcasted_iota(I32, (n, n), 1)
    return row, col


MXU_COLS = 256


def _gla_kernel(*refs, chunks, blocks_per_seq):
    *io_refs, st_ref, ha_ref, hb_ref = refs
    s = pl.program_id(0)

    @pl.when(s == 0)
    def _():
        hb_ref[...] = jnp.zeros_like(hb_ref)

    @pl.when((s == 0) | (((s + blocks_per_seq - 1) % blocks_per_seq) == 0))
    def _():
        st_ref[...] = jnp.zeros_like(st_ref)

    body = functools.partial(_gla_body, chunks, (*io_refs, st_ref))

    @pl.when((s & 1) == 0)
    def _():
        body(ha_ref, hb_ref)

    @pl.when((s & 1) == 1)
    def _():
        body(hb_ref, ha_ref)


def _gla_body(chunks, refs, h_next, h_ref):
    x_ref, w_ref, b_ref, wup_ref, bdec_ref, ng_ref, nb_ref, o_ref, st_ref = refs
    xb = x_ref[...].astype(BF16)
    width = w_ref.shape[1]
    pieces = iter([slice(c0, min(c0 + MXU_COLS, width)) for c0 in range(0, width, MXU_COLS)])

    def project_piece():
        cols = next(pieces, None)
        if cols is not None:
            h_next[:, cols] = jnp.dot(xb, w_ref[:, cols], preferred_element_type=F32) + b_ref[:, cols]

    q_ref = h_ref.at[:, GLA_COLS["q"][0]:GLA_COLS["q"][1]]
    k_ref = h_ref.at[:, GLA_COLS["k"][0]:GLA_COLS["k"][1]]
    v_ref = h_ref.at[:, GLA_COLS["v"][0]:GLA_COLS["v"][1]]
    r_ref = h_ref.at[:, GLA_COLS["r"][0]:GLA_COLS["r"][1]]
    dec_ref = h_ref.at[:, GLA_COLS["dec"][0]:GLA_COLS["dec"][1]]

    row, col = _tri_masks(CHUNK)
    ltri = (row >= col).astype(BF16)
    kw, vw = 2 * GLA_DK, 2 * GLA_DV
    key_shift, val_shift = GLA_DK.bit_length() - 1, GLA_DV.bit_length() - 1
    iota = lambda shape, axis: lax.broadcasted_iota(I32, shape, axis)
    causal = iota((CHUNK, kw), 0) >= (iota((CHUNK, kw), 1) & (GLA_DK - 1))
    key_diag = (iota((kw, kw), 0) >> key_shift) == (iota((kw, kw), 1) >> key_shift)
    val_diag = (iota((kw, vw), 0) >> key_shift) == (iota((kw, vw), 1) >> val_shift)
    state_diag = (iota((vw, kw), 0) >> val_shift) == (iota((vw, kw), 1) >> key_shift)
    zero = jnp.zeros((), BF16)
    pairs = range(GLA_HEADS // 2)
    qp, vp, k_diag, kep, decays = [], [], [], [], []
    for c in range(chunks):
        project_piece()
        rows = pl.ds(c * CHUNK, CHUNK)
        logits = _bdot(dec_ref[rows, :], wup_ref[...]) + bdec_ref[...]
        log_a = -_softplus(-logits) * (1.0 / GLA_TEMP)
        cum = _mask_dot(ltri, log_a)
        cum_end = cum[CHUNK - 1:CHUNK, :]
        q = q_ref[rows, :] * (GLA_DK ** -0.5)
        k = k_ref[rows, :]
        q_dec = (q * jnp.exp(cum)).astype(BF16)
        k_inv = (k * jnp.exp(-cum)).astype(BF16)
        k_end = (k * jnp.exp(cum_end - cum)).astype(BF16)
        decay_end = jnp.exp(cum_end)
        v = v_ref[rows, :].astype(BF16)
        for p in pairs:
            s = slice(p * kw, (p + 1) * kw)
            qp.append(q_dec[:, s])
            vp.append(v[:, p * vw:(p + 1) * vw])
            k_diag.append(jnp.where(key_diag, jnp.concatenate([k_inv[:, s], k_inv[:, s]], axis=0), zero))
            kep.append(k_end[:, s])
            decays.append(decay_end[:, s])

    for _ in range(chunks):
        project_piece()

    v_diag = [jnp.where(val_diag, jnp.concatenate([x, x], axis=0), zero) for x in vp]
    scores = [jnp.where(causal, _bdot_nt(q, kd), 0.0) for q, kd in zip(qp, k_diag)]
    o_local = [_bdot(s, vd) for s, vd in zip(scores, v_diag)]
    upd = [jnp.where(state_diag, _bdot_tn(x, ke), 0.0) for x, ke in zip(vp, kep)]

    state = [st_ref[p] for p in pairs]
    n_pairs = len(pairs)
    for c in range(chunks):
        rows = pl.ds(c * CHUNK, CHUNK)
        item = slice(c * n_pairs, (c + 1) * n_pairs)
        o = [ol + _bdot_nt(q, st) for ol, q, st in zip(o_local[item], qp[item], state)]
        state = [st * dcy + u for st, dcy, u in zip(state, decays[item], upd[item])]
        outs = []
        for op in o:
            for oh in (op[:, :GLA_DV], op[:, GLA_DV:]):
                mu = jnp.mean(oh, axis=-1, keepdims=True)
                oc = oh - mu
                var = jnp.mean(oc * oc, axis=-1, keepdims=True)
                outs.append(oc * lax.rsqrt(var + LN_EPS))
        y = jnp.concatenate(outs, axis=1) * ng_ref[...] + nb_ref[...]
        r = r_ref[rows, :]
        o_ref[rows, :] = (y * (r * _sigmoid(r))).astype(o_ref.dtype)
    for p in pairs:
        st_ref[p] = state[p]


def _gla_branch(x, batch, seq, w, bias, wup_pad, b_decay, norm_g, norm_b):
    t = batch * seq
    chunks = 8
    rows = CHUNK * chunks
    nblk = seq // rows

    n_tiles = batch * nblk

    def full(a):
        return pl.BlockSpec(a.shape, lambda s: (0,) * a.ndim)

    args = (w, bias, wup_pad, b_decay, norm_g, norm_b)
    return pl.pallas_call(
        functools.partial(_gla_kernel, chunks=chunks, blocks_per_seq=nblk),
        out_shape=jax.ShapeDtypeStruct((t, GLA_VW), BF16),
        grid=(n_tiles + 1,),
        in_specs=[pl.BlockSpec((rows, D_MODEL), lambda s: (jnp.minimum(s, n_tiles - 1), 0))]
        + [full(a) for a in args],
        out_specs=pl.BlockSpec((rows, GLA_VW), lambda s: (jnp.maximum(s - 1, 0), 0)),
        scratch_shapes=[pltpu.VMEM((GLA_HEADS // 2, 2 * GLA_DV, 2 * GLA_DK), F32),
                        pltpu.VMEM((rows, w.shape[1]), F32), pltpu.VMEM((rows, w.shape[1]), F32)],
        compiler_params=_cparams("arbitrary"),
    )(x, *args)


def _lru_kernel(xin_ref, w_ref, b_ref, cw_ref, cb_ref, wri_ref, bri_ref, sp_ref,
                o_ref, xbuf_ref, a_ref, u_ref, hs_ref, gate_ref, h_ref):
    tm = xin_ref.shape[0]
    pad = SUBLANES

    @pl.when(pl.program_id(1) == 0)
    def _():
        xbuf_ref[0:pad, :] = jnp.zeros((pad, LRU_WIDTH), F32)
        h_ref[...] = jnp.zeros_like(h_ref)

    proj = jnp.dot(xin_ref[...].astype(BF16), w_ref[...], preferred_element_type=F32) + b_ref[...]
    x = proj[:, :LRU_WIDTH]
    gate_ref[...] = jax.nn.gelu(proj[:, LRU_WIDTH:])
    xbuf_ref[pad:pad + tm, :] = x
    xc = cb_ref[...] + x * cw_ref[LRU_CONV - 1:LRU_CONV, :]
    for j in range(LRU_CONV - 1):
        back = LRU_CONV - 1 - j
        xc = xc + xbuf_ref[pl.ds(pad - back, tm), :] * cw_ref[j:j + 1, :]
    xbuf_ref[0:pad, :] = x[tm - pad:tm, :]

    ri = _sigmoid(_bdot(xc, wri_ref[...]) + bri_ref[...])
    rg, ig = ri[:, :LRU_WIDTH], ri[:, LRU_WIDTH:]
    log_a = -LRU_C * rg * sp_ref[...]
    a = jnp.exp(log_a)
    a_ref[...] = a
    u_ref[...] = jnp.sqrt(1.0 - a * a) * (ig * xc)

    def step(t, hprev):
        hnew = a_ref[pl.ds(t, 1), :] * hprev + u_ref[pl.ds(t, 1), :]
        hs_ref[pl.ds(t, 1), :] = hnew
        return hnew

    h_ref[...] = lax.fori_loop(0, tm, step, h_ref[...], unroll=8)
    o_ref[...] = (gate_ref[...] * hs_ref[...]).astype(o_ref.dtype)


def _lru_branch(x, batch, seq, w, bias, conv_w, conv_b, w_ri, b_ri, softplus_neg_lam):
    t = batch * seq
    tm = min(1024, seq)
    nblk = seq // tm

    def full(a):
        return pl.BlockSpec(a.shape, lambda b, n: (0,) * a.ndim)

    args = (w, bias, conv_w, conv_b, w_ri, b_ri, softplus_neg_lam)
    return pl.pallas_call(
        _lru_kernel,
        out_shape=jax.ShapeDtypeStruct((t, LRU_WIDTH), BF16),
        grid=(batch, nblk),
        in_specs=[pl.BlockSpec((tm, D_MODEL), lambda b, n: (b * nblk + n, 0))] + [full(a) for a in args],
        out_specs=pl.BlockSpec((tm, LRU_WIDTH), lambda b, n: (b * nblk + n, 0)),
        scratch_shapes=[pltpu.VMEM((tm + SUBLANES, LRU_WIDTH), F32)]
        + [pltpu.VMEM((tm, LRU_WIDTH), F32)] * 4 + [pltpu.VMEM((1, LRU_WIDTH), F32)],
        compiler_params=_cparams("parallel", "arbitrary"),
    )(x, *args)


def _rwkv_prep_kernel(*refs, blocks_per_seq):
    *io_refs, ha_ref, hb_ref, carry_ref = refs
    tm = io_refs[0].shape[0]
    s = pl.program_id(0)
    body = functools.partial(_rwkv_prep_body, tm, s, RWKV_WIDTH, blocks_per_seq, (*io_refs, carry_ref))

    @pl.when(s == 0)
    def _():
        hb_ref[...] = jnp.zeros_like(hb_ref)
        carry_ref[...] = jnp.zeros_like(carry_ref)

    @pl.when((s & 1) == 0)
    def _():
        body(ha_ref, hb_ref)

    @pl.when((s & 1) == 1)
    def _():
        body(hb_ref, ha_ref)


def _rwkv_prep_body(tm, s, gw, blocks_per_seq, refs, h_next, h_cur):
    (x_ref, w_ref, b_ref, mu_ref, w2_ref, a2_ref, g2_ref, w0_ref, a0_ref, kk_ref, ka_ref, rk_ref, bd_ref,
     r_out, k_out, v_out, lw_out, kk_out, kka_out, bonus_out, g_out, carry_ref) = refs
    first = ((s + blocks_per_seq - 1) % blocks_per_seq) == 0
    keep = jnp.where(first, 0.0, 1.0)
    row0 = lax.broadcasted_iota(I32, (tm, 1), 0) == 0
    xb = x_ref[...].astype(BF16)

    pw = gw // 2

    def project(j):
        cols = slice(j * pw, (j + 1) * pw)
        h_next[:, cols] = jnp.dot(xb, w_ref[:, cols], preferred_element_type=F32) + b_ref[:, cols]

    def shifted(j):
        cols = slice(j * gw, (j + 1) * gw)
        cur = h_cur[:, cols]
        prev_row = carry_ref[0:1, cols] * keep
        carry_ref[0:1, cols] = cur[tm - 1:tm, :]
        prev = jnp.where(row0, prev_row, pltpu.roll(cur, 1, axis=0))
        return cur + (prev - cur) * mu_ref[:, cols]

    project(0)
    sm = shifted(3)
    wa = sm[:, SM_WA[0]:SM_WA[0] + SM_WA[1]]
    gl = sm[:, SM_GL[0]:SM_GL[0] + SM_GL[1]]
    project(1)
    w_log = -_softplus(-(w0_ref[...] + _bdot(jnp.tanh(wa), w2_ref[...]))) - 0.5
    lw_out[...] = -jnp.exp(w_log)
    project(2)
    a = _sigmoid(a0_ref[...] + _bdot(wa, a2_ref[...]))
    g_out[...] = _bdot(_sigmoid(gl), g2_ref[...])
    project(3)
    k = shifted(1)
    kk = k * kk_ref[...]
    norm = jnp.sqrt(_dot_mask(kk * kk, bd_ref[...]))
    project(4)
    kk = kk / jnp.maximum(norm, 1e-12)
    kk_out[...] = kk
    kka_out[...] = kk * a
    k2 = k * (1.0 + (a - 1.0) * ka_ref[...])
    k_out[...] = k2
    project(5)
    r = shifted(0)
    r_out[...] = r
    project(6)
    v = shifted(2)
    v_out[...] = v.astype(v_out.dtype)
    project(7)
    bonus_out[...] = _dot_mask(r * k2 * rk_ref[...], bd_ref[...]) * v


def _rwkv_prep(x, batch, seq, params):
    t = batch * seq
    tm = min(512, seq)
    bps = seq // tm
    n_tiles = t // tm
    width = params[0].shape[1]

    def full(a):
        return pl.BlockSpec(a.shape, lambda s: (0,) * a.ndim)

    out = lambda dtype: jax.ShapeDtypeStruct((t, RWKV_WIDTH), dtype)
    ospec = pl.BlockSpec((tm, RWKV_WIDTH), lambda s: (jnp.maximum(s - 1, 0), 0))
    out_dtypes = (F32, F32, BF16, F32, F32, F32, F32, F32)
    return pl.pallas_call(
        functools.partial(_rwkv_prep_kernel, blocks_per_seq=bps),
        out_shape=[out(dt) for dt in out_dtypes],
        grid=(n_tiles + 1,),
        in_specs=[pl.BlockSpec((tm, D_MODEL), lambda s: (jnp.minimum(s, n_tiles - 1), 0))]
        + [full(a) for a in params],
        out_specs=[ospec] * 8,
        scratch_shapes=[pltpu.VMEM((tm, width), F32), pltpu.VMEM((tm, width), F32),
                        pltpu.VMEM((SUBLANES, width), F32)],
        compiler_params=_cparams("arbitrary"),
    )(x, *params)


PAIR_W = 2 * RWKV_HEAD
N_PAIRS = RWKV_HEADS // 2
HEAD_SHIFT = RWKV_HEAD.bit_length() - 1


def _pair_masks():
    t = lax.broadcasted_iota(I32, (CHUNK, PAIR_W), 0)
    lane = lax.broadcasted_iota(I32, (CHUNK, PAIR_W), 1)
    j = lane & (RWKV_HEAD - 1)
    r = lax.broadcasted_iota(I32, (PAIR_W, PAIR_W), 0)
    c = lax.broadcasted_iota(I32, (PAIR_W, PAIR_W), 1)
    r4 = lax.broadcasted_iota(I32, (2 * PAIR_W, PAIR_W), 0)
    c4 = lax.broadcasted_iota(I32, (2 * PAIR_W, PAIR_W), 1)
    return dict(
        strict=t > j, incl=t >= j, eye=(t == j).astype(F32),
        same_block=jnp.bitwise_xor(t, j) < INV_BLOCK,
        first_head=lane < RWKV_HEAD,
        block_diag=(r >> HEAD_SHIFT) == (c >> HEAD_SHIFT),
        stacked_diag=((r4 >> HEAD_SHIFT) & 1) == (c4 >> HEAD_SHIFT))


N_SLOT_REFS = 5


def _rwkv_scan_kernel(*refs, chunks, blocks_per_seq):
    *io_refs, st_ref = refs[:-2 * N_SLOT_REFS]
    slot_a, slot_b = refs[-2 * N_SLOT_REFS:-N_SLOT_REFS], refs[-N_SLOT_REFS:]
    s = pl.program_id(0)

    @pl.when(s == 0)
    def _():
        for ref in slot_b:
            ref[...] = jnp.zeros_like(ref)

    @pl.when((s == 0) | (((s + blocks_per_seq - 1) % blocks_per_seq) == 0))
    def _():
        st_ref[...] = jnp.zeros_like(st_ref)

    body = functools.partial(_rwkv_scan_body, chunks, (*io_refs, st_ref))

    @pl.when((s & 1) == 0)
    def _():
        body(slot_a, slot_b)

    @pl.when((s & 1) == 1)
    def _():
        body(slot_b, slot_a)


def _rwkv_scan_body(chunks, refs, wr, rd):
    r_ref, k_ref, v_ref, lw_ref, kk_ref, kka_ref, bonus_ref, g_ref, lg_ref, lb_ref, o_ref, st_ref = refs
    yl_w, rm_w, gm_w, qm_w, gam_w = wr
    yl_r, rm_r, gm_r, qm_r, gam_r = rd
    row, col = _tri_masks(CHUNK)
    ltri = (row >= col).astype(BF16)
    m = _pair_masks()
    zero = jnp.zeros((), BF16)
    pairs = range(N_PAIRS)
    sl = [slice(p * PAIR_W, (p + 1) * PAIR_W) for p in pairs]

    def block_diag(x):
        xb = x.astype(BF16)
        return jnp.where(m["block_diag"], jnp.concatenate([xb, xb], axis=0), zero)

    def pair_mul(lhs, *rhs):
        rb = jnp.concatenate([block_diag(x) for x in rhs], axis=1) if len(rhs) > 1 else block_diag(rhs[0])
        out = jnp.dot(lhs.astype(BF16), rb, preferred_element_type=F32)
        return [out[:, i * PAIR_W:(i + 1) * PAIR_W] for i in range(len(rhs))]

    def each(fn, *lists):
        return [fn(*args) for args in zip(*lists)]

    carried = {"state": [st_ref[p] for p in pairs]}
    pending = iter(range(chunks))
    inv_n = 1.0 / RWKV_HEAD

    def head_stat(z):
        s0 = jnp.sum(jnp.where(m["first_head"], z, 0.0), axis=-1, keepdims=True)
        s1 = jnp.sum(jnp.where(m["first_head"], 0.0, z), axis=-1, keepdims=True)
        return jnp.where(m["first_head"], s0, s1) * inv_n

    def previous_tile_chunk():
        c = next(pending, None)
        if c is None:
            return
        rows = pl.ds(c * CHUNK, CHUNK)
        items = range(c * N_PAIRS, (c + 1) * N_PAIRS)
        state = carried["state"]
        y = [yl_r[i] + _bdot_nt(rm_r[i], st) for i, st in zip(items, state)]
        carried["state"] = [st * gam_r[i, 0:1, :] + _bdot_nt(st, gm_r[i]) + qm_r[i] for i, st in zip(items, state)]
        outs = []
        for yp in y:
            yc = yp - head_stat(yp)
            outs.append(yc * lax.rsqrt(head_stat(yc * yc) + RWKV_LNX_EPS))
        yn = jnp.concatenate(outs, axis=1) * lg_ref[...] + lb_ref[...]
        o_ref[rows, :] = ((yn + bonus_ref[rows, :]) * g_ref[rows, :]).astype(o_ref.dtype)

    lhs, rhs, vp, ends, gammas = [], [], [], [], []
    for c in range(chunks):
        rows = pl.ds(c * CHUNK, CHUNK)
        lw = lw_ref[rows, :]
        cum = _mask_dot(ltri, lw)
        cum_end = cum[CHUNK - 1:CHUNK, :]
        e_inv = jnp.exp(-cum)
        e_end = jnp.exp(cum_end - cum)
        kka = kka_ref[rows, :]
        k = k_ref[rows, :]
        a_bar = (-kk_ref[rows, :] * jnp.exp(cum - lw)).astype(BF16)
        r_bar = (r_ref[rows, :] * jnp.exp(cum)).astype(BF16)
        b_til = (kka * e_inv).astype(BF16)
        k_til = (k * e_inv).astype(BF16)
        b_end = (kka * e_end).astype(BF16)
        k_end = (k * e_end).astype(BF16)
        gamma = jnp.exp(cum_end)
        v = v_ref[rows, :].astype(BF16)

        for s in sl:
            lhs.append(jnp.concatenate([a_bar[:, s], r_bar[:, s]], axis=0))
            rhs.append(jnp.where(m["stacked_diag"], jnp.concatenate(
                [b_til[:, s], b_til[:, s], k_til[:, s], k_til[:, s]], axis=0), zero))
            vp.append(v[:, s])
            ends.append(jnp.concatenate([b_end[:, s], k_end[:, s]], axis=0))
            gammas.append(gamma[:, s])

    blocks = each(_bdot_nt, lhs, rhs)
    previous_tile_chunk()
    a_ab = [jnp.where(m["strict"], b[:CHUNK, :PAIR_W], 0.0) for b in blocks]
    a_ak = [jnp.where(m["strict"], b[:CHUNK, PAIR_W:], 0.0) for b in blocks]
    r_ab = [jnp.where(m["incl"], b[CHUNK:, :PAIR_W], 0.0) for b in blocks]
    r_ak = [jnp.where(m["incl"], b[CHUNK:, PAIR_W:], 0.0) for b in blocks]
    from_v = [jnp.dot(jnp.concatenate([ak, rk], axis=0).astype(BF16), block_diag(x),
                      preferred_element_type=F32) for ak, rk, x in zip(a_ak, r_ak, vp)]
    previous_tile_chunk()
    d = [jnp.where(m["same_block"], a, 0.0) for a in a_ab]
    low = [a - di for a, di in zip(a_ab, d)]
    pm = [m["eye"] + di for di in d]
    dpow = [pair_mul(di, di)[0] for di in d]
    previous_tile_chunk()
    for _ in range(INV_BLOCK.bit_length() - 3):
        both = each(lambda x, p: pair_mul(x, x, p), dpow, pm)
        dpow = [b[0] for b in both]
        pm = [p + b[1] for p, b in zip(pm, both)]
        previous_tile_chunk()
    pm = [p + pair_mul(x, p)[0] for p, x in zip(pm, dpow)]
    previous_tile_chunk()
    both = each(lambda p, lo, fv, l: pair_mul(p, lo, fv[:CHUNK], l[:CHUNK]), pm, low, from_v, lhs)
    previous_tile_chunk()
    both2 = each(lambda b: pair_mul(b[0], b[0], b[1], b[2]), both)
    previous_tile_chunk()
    xw_v = [b[1] + b2[1] for b, b2 in zip(both, both2)]
    xw_a = [b[2] + b2[2] for b, b2 in zip(both, both2)]
    both3 = each(lambda b2, xv, xa: pair_mul(b2[0], xv, xa), both2, xw_v, xw_a)
    u_v = [xv + b3[0] for xv, b3 in zip(xw_v, both3)]
    wm = [xa + b3[1] for xa, b3 in zip(xw_a, both3)]
    both4 = each(lambda rb, uv, w: pair_mul(rb, uv, w), r_ab, u_v, wm)
    y_local = [fv[CHUNK:] + b4[0] for fv, b4 in zip(from_v, both4)]
    rm = [l[CHUNK:] + b4[1] for l, b4 in zip(lhs, both4)]
    gm = [jnp.where(m["block_diag"], _bdot_tn(en[:CHUNK], w), 0.0) for en, w in zip(ends, wm)]
    qm = [jnp.where(m["block_diag"], _bdot_tn(jnp.concatenate([uv.astype(BF16), x], axis=0), en), 0.0)
          for uv, x, en in zip(u_v, vp, ends)]
    for _ in range(chunks):
        previous_tile_chunk()
    for p in pairs:
        st_ref[p] = carried["state"][p]

    for i in range(chunks * N_PAIRS):
        yl_w[i] = y_local[i]
        rm_w[i] = rm[i]
        gm_w[i] = gm[i]
        qm_w[i] = qm[i]
        gam_w[i, 0:1, :] = gammas[i]


def _rwkv_scan(prep, batch, seq, lnx_g, lnx_b):
    t = batch * seq
    chunks = 8
    rows = CHUNK * chunks
    nblk = seq // rows
    n_tiles = batch * nblk
    n_items = chunks * N_PAIRS
    cur = pl.BlockSpec((rows, RWKV_WIDTH), lambda s: (jnp.minimum(s, n_tiles - 1), 0))
    prev = pl.BlockSpec((rows, RWKV_WIDTH), lambda s: (jnp.maximum(s - 1, 0), 0))
    pspec = pl.BlockSpec((1, RWKV_WIDTH), lambda s: (0, 0))
    slot = [pltpu.VMEM((n_items, CHUNK, PAIR_W), F32), pltpu.VMEM((n_items, CHUNK, PAIR_W), F32),
            pltpu.VMEM((n_items, PAIR_W, PAIR_W), F32), pltpu.VMEM((n_items, PAIR_W, PAIR_W), F32),
            pltpu.VMEM((n_items, SUBLANES, PAIR_W), F32)]
    return pl.pallas_call(
        functools.partial(_rwkv_scan_kernel, chunks=chunks, blocks_per_seq=nblk),
        out_shape=jax.ShapeDtypeStruct((t, RWKV_WIDTH), BF16),
        grid=(n_tiles + 1,),
        in_specs=[cur] * 6 + [prev, prev, pspec, pspec],
        out_specs=prev,
        scratch_shapes=[pltpu.VMEM((N_PAIRS, PAIR_W, PAIR_W), F32)] + slot + slot,
        compiler_params=_cparams("arbitrary"),
    )(*prep, lnx_g, lnx_b)


def _merge_kernel(og_ref, ol_ref, or_ref, x_ref, wgt_ref, bgt_ref, pg_ref, pl_ref, pr_ref, wo_ref,
                  lg_ref, lb_ref, xo_ref):
    d = D_MODEL
    xin = x_ref[...].astype(BF16)
    merged = None
    for b, (o_ref, p_ref) in enumerate(((og_ref, pg_ref), (ol_ref, pl_ref), (or_ref, pr_ref))):
        cols = slice(b * d, (b + 1) * d)
        gate = _sigmoid(jnp.dot(xin, wgt_ref[:, cols], preferred_element_type=F32) + bgt_ref[:, cols])
        term = gate * jnp.dot(o_ref[...], p_ref[...], preferred_element_type=F32)
        merged = term if merged is None else merged + term
    z = DEEPNORM_ALPHA * x_ref[...] + _bdot(merged, wo_ref[...])
    out = _layer_norm_rows(z, lg_ref[...], lb_ref[...], LN_EPS)
    xo_ref[...] = out


def _merge(o_gla, o_lru, o_rwkv, x, w_gates, b_gates, p_gla, p_lru, p_rwkv, w_out, ln_g, ln_b):
    t = x.shape[0]
    tm = min(1024, t)

    def rows(width):
        return pl.BlockSpec((tm, width), lambda i: (i, 0))

    def full(a):
        return pl.BlockSpec(a.shape, lambda i: (0,) * a.ndim, pipeline_mode=pl.Buffered(1))

    params = (w_gates, b_gates, p_gla, p_lru, p_rwkv, w_out, ln_g, ln_b)
    return pl.pallas_call(
        _merge_kernel,
        out_shape=jax.ShapeDtypeStruct((t, D_MODEL), F32),
        grid=(t // tm,),
        in_specs=[rows(GLA_VW), rows(LRU_WIDTH), rows(RWKV_WIDTH), rows(D_MODEL)] + [full(a) for a in params],
        out_specs=rows(D_MODEL),
        compiler_params=_cparams("parallel"),
    )(o_gla, o_lru, o_rwkv, x, *params)


FFN_SPLIT = 2


def _ffn_kernel(x_ref, wg_ref, wu_ref, wd_ref, lg_ref, lb_ref, xo_ref):
    x = x_ref[...]
    xb = x.astype(BF16)
    tf = wg_ref.shape[1] // FFN_SPLIT
    acc = None
    for f in range(FFN_SPLIT):
        cols = slice(f * tf, (f + 1) * tf)
        g = jnp.dot(xb, wg_ref[:, cols], preferred_element_type=F32)
        u = jnp.dot(xb, wu_ref[:, cols], preferred_element_type=F32)
        part = _bdot(g * _sigmoid(g) * u, wd_ref[cols, :])
        acc = part if acc is None else acc + part
    z = DEEPNORM_ALPHA * x + acc
    xo_ref[...] = _layer_norm_rows(z, lg_ref[...], lb_ref[...], LN_EPS)


def _dense_ffn(x, wg, wu, wd, ln_g, ln_b):
    t = x.shape[0]
    tm = min(1024, t)
    rows = pl.BlockSpec((tm, D_MODEL), lambda i: (i, 0))
    resident = lambda a: pl.BlockSpec(a.shape, lambda i: (0,) * a.ndim, pipeline_mode=pl.Buffered(1))
    return pl.pallas_call(
        _ffn_kernel,
        out_shape=jax.ShapeDtypeStruct((t, D_MODEL), F32),
        grid=(t // tm,),
        in_specs=[rows] + [resident(a) for a in (wg, wu, wd, ln_g, ln_b)],
        out_specs=rows,
        compiler_params=_cparams("parallel"),
    )(x, wg, wu, wd, ln_g, ln_b)


def _router_kernel(x_ref, wr_ref, meta_ref, wts_ref, cnt_ref, carry_ref):
    tm = x_ref.shape[0]

    @pl.when(pl.program_id(0) == 0)
    def _():
        carry_ref[...] = jnp.zeros_like(carry_ref)

    lane = lax.broadcasted_iota(I32, (tm, LANES), 1)
    neg = jnp.float32(-jnp.inf)
    logits = jnp.where(lane < N_EXPERTS, _fdot(x_ref[...], wr_ref[...]), neg)
    m1 = jnp.max(logits, axis=-1, keepdims=True)
    e1 = jnp.min(jnp.where(logits == m1, lane, LANES), axis=-1, keepdims=True)
    rest = jnp.where(lane == e1, neg, logits)
    m2 = jnp.max(rest, axis=-1, keepdims=True)
    e2 = jnp.min(jnp.where(rest == m2, lane, LANES), axis=-1, keepdims=True)
    ex = jnp.exp(m2 - m1)
    w1 = 1.0 / (1.0 + ex)
    w2 = ex / (1.0 + ex)

    hot1 = lane == e1
    hot2 = lane == e2
    onehot = jnp.where(hot1 | hot2, 1.0, 0.0)
    row, col = _tri_masks(tm)
    before = (row > col).astype(BF16)
    prefix = jnp.dot(before, onehot.astype(BF16), preferred_element_type=F32) + carry_ref[...]
    rank1 = jnp.sum(jnp.where(hot1, prefix, 0.0), axis=-1, keepdims=True).astype(I32)
    rank2 = jnp.sum(jnp.where(hot2, prefix, 0.0), axis=-1, keepdims=True).astype(I32)
    carry_ref[...] += jnp.sum(onehot, axis=0, keepdims=True)
    cnt_ref[...] = carry_ref[...]

    meta = jnp.where(lane == 0, e1, jnp.where(lane == 1, e2, jnp.where(lane == 2, rank1, rank2)))
    meta_ref[...] = jnp.where(lane < 4, meta, 0)
    wts_ref[...] = jnp.where(lane == 0, w1, jnp.where(lane == 1, w2, 0.0))


def _router(x, wr_pad):
    t = x.shape[0]
    tm = min(512, t)
    return pl.pallas_call(
        _router_kernel,
        out_shape=[jax.ShapeDtypeStruct((t, LANES), I32), jax.ShapeDtypeStruct((t, LANES), F32),
                   jax.ShapeDtypeStruct((1, LANES), F32)],
        grid=(t // tm,),
        in_specs=[pl.BlockSpec((tm, D_MODEL), lambda i: (i, 0)),
                  pl.BlockSpec((D_MODEL, LANES), lambda i: (0, 0))],
        out_specs=[pl.BlockSpec((tm, LANES), lambda i: (i, 0)), pl.BlockSpec((tm, LANES), lambda i: (i, 0)),
                   pl.BlockSpec((1, LANES), lambda i: (0, 0))],
        scratch_shapes=[pltpu.VMEM((1, LANES), F32)],
        compiler_params=_cparams("arbitrary"),
    )(x, wr_pad)


def _dispatch_kernel(pad_start_ref, pad_len_ref, nused_ref, dest_ref, x_ref, *rest):
    n_w = (len(rest) - 4) // 2
    w_refs, xs_ref, wb_refs, (zeros_ref, sem, fill_sem) = rest[:n_w], rest[n_w], rest[n_w + 1:2 * n_w + 1], rest[-3:]
    tm = x_ref.shape[0]
    n_blocks = xs_ref.shape[0] // MOE_ROWS

    @pl.when(pl.program_id(0) == pl.num_programs(0) - 1)
    def _():
        zeros_ref[...] = jnp.zeros_like(zeros_ref)
        pad_copy = lambda row: pltpu.make_async_copy(zeros_ref.at[pl.ds(0, 1)], xs_ref.at[pl.ds(row, 1)], fill_sem)
        blk_copy = lambda blk: pltpu.make_async_copy(
            zeros_ref, xs_ref.at[pl.ds(pl.multiple_of(blk * MOE_ROWS, MOE_ROWS), MOE_ROWS)], fill_sem)

        def for_each(fn):
            for e in range(N_EXPERTS):
                base = pad_start_ref[e]
                lax.fori_loop(0, pad_len_ref[e], lambda r, c: (fn(pad_copy(base + r)), c)[1], 0)
            lax.fori_loop(nused_ref[0], n_blocks, lambda b, c: (fn(blk_copy(b)), c)[1], 0)

        for_each(lambda cp: cp.start())
        for_each(lambda cp: cp.wait())

    def row_copy(r, j):
        return pltpu.make_async_copy(x_ref.at[pl.ds(r, 1)], xs_ref.at[pl.ds(dest_ref[TOP_K * r + j], 1)], sem)

    def start(r, c):
        for j in range(TOP_K):
            row_copy(r, j).start(priority=j)
        return c

    lax.fori_loop(0, tm, start, 0, unroll=8)
    for w_ref, wb_ref in zip(w_refs, wb_refs):
        wb_ref[...] = w_ref[...].astype(wb_ref.dtype)
    for j in range(TOP_K):
        pltpu.make_async_copy(x_ref, xs_ref.at[pl.ds(0, tm)], sem).wait()


def _dispatch(x, dest_flat, pad_start, pad_len, n_used, n_slots, weights):
    t = x.shape[0]
    tm = min(256, t)
    n_steps = t // tm
    flat = [w.reshape(-1, w.shape[-1]) for w in weights]
    slab = lambda w: pl.BlockSpec((w.shape[0] // n_steps, w.shape[1]), lambda i, *_: (i, 0))
    grid_spec = pltpu.PrefetchScalarGridSpec(
        num_scalar_prefetch=3,
        grid=(n_steps,),
        in_specs=[pl.BlockSpec((tm * TOP_K,), lambda i, *_: (i,), memory_space=pltpu.SMEM),
                  pl.BlockSpec((tm, D_MODEL), lambda i, *_: (i, 0))] + [slab(w) for w in flat],
        out_specs=[pl.BlockSpec(memory_space=pl.ANY)] + [slab(w) for w in flat],
        scratch_shapes=[pltpu.VMEM((MOE_ROWS, D_MODEL), F32), pltpu.SemaphoreType.DMA(()),
                        pltpu.SemaphoreType.DMA(())],
    )
    xs, *cast = pl.pallas_call(
        _dispatch_kernel,
        out_shape=[jax.ShapeDtypeStruct((n_slots, D_MODEL), F32)]
        + [jax.ShapeDtypeStruct(w.shape, BF16) for w in flat],
        grid_spec=grid_spec,
        compiler_params=_cparams("arbitrary"),
    )(pad_start, pad_len, n_used, dest_flat, x, *flat)
    return xs, [c.reshape(w.shape) for c, w in zip(cast, weights)]


def _expert_kernel(blk_e_ref, nused_ref, xs_ref, wg_ref, wu_ref, wd_ref, ys_ref, acc_ref, *, n_f):
    del blk_e_ref
    i = pl.program_id(0)
    f = pl.program_id(1)
    last = n_f - 1
    used = i < nused_ref[0]

    def piece():
        xb = xs_ref[...].astype(BF16)
        g = jnp.dot(xb, wg_ref[0], preferred_element_type=F32)
        u = jnp.dot(xb, wu_ref[0], preferred_element_type=F32)
        return _bdot(g * _sigmoid(g) * u, wd_ref[0])

    if n_f == 1:
        @pl.when(used)
        def _():
            ys_ref[...] = piece()
    else:
        @pl.when(used & (f == 0))
        def _():
            acc_ref[...] = piece()

        if n_f > 2:
            @pl.when(used & (f != 0) & (f != last))
            def _():
                acc_ref[...] += piece()

        @pl.when(used & (f == last))
        def _():
            ys_ref[...] = acc_ref[...] + piece()

    @pl.when(jnp.logical_not(used) & (f == last))
    def _():
        ys_ref[...] = jnp.zeros_like(ys_ref)


def _experts(xs, blk_e, n_used, wg, wu, wd):
    n_slots = xs.shape[0]
    n_blocks = n_slots // MOE_ROWS
    ff = wg.shape[2]
    n_f = 2
    tf = ff // n_f
    piece = lambda i, f: jnp.where((i & 1) == 1, n_f - 1 - f, f)
    grid_spec = pltpu.PrefetchScalarGridSpec(
        num_scalar_prefetch=2,
        grid=(n_blocks, n_f),
        in_specs=[pl.BlockSpec((MOE_ROWS, D_MODEL), lambda i, f, be, nu: (i, 0)),
                  pl.BlockSpec((1, D_MODEL, tf), lambda i, f, be, nu: (be[i], 0, piece(i, f))),
                  pl.BlockSpec((1, D_MODEL, tf), lambda i, f, be, nu: (be[i], 0, piece(i, f))),
                  pl.BlockSpec((1, tf, D_MODEL), lambda i, f, be, nu: (be[i], piece(i, f), 0))],
        out_specs=pl.BlockSpec((MOE_ROWS, D_MODEL), lambda i, f, be, nu: (i, 0)),
        scratch_shapes=[pltpu.VMEM((MOE_ROWS, D_MODEL), F32)],
    )
    return pl.pallas_call(
        functools.partial(_expert_kernel, n_f=n_f),
        out_shape=jax.ShapeDtypeStruct((n_slots, D_MODEL), F32),
        grid_spec=grid_spec,
        compiler_params=_cparams("arbitrary", "arbitrary"),
    )(blk_e, n_used, xs, wg, wu, wd)


def _combine_kernel(dest_ref, dest_next_ref, x_ref, wts_ref, ys_ref, lg_ref, lb_ref, xo_ref, buf_ref, sem):
    tm = x_ref.shape[0]
    i = pl.program_id(0)
    n = pl.num_programs(0)

    def gather(idx_ref, slot):
        def start(r, c):
            for j in range(TOP_K):
                pltpu.make_async_copy(ys_ref.at[pl.ds(idx_ref[TOP_K * r + j], 1)],
                                      buf_ref.at[slot, j, pl.ds(r, 1)], sem.at[slot]).start(priority=j)
            return c
        lax.fori_loop(0, tm, start, 0, unroll=8)

    slot = i & 1

    @pl.when(i == 0)
    def _():
        gather(dest_ref, 0)

    @pl.when(i + 1 < n)
    def _():
        gather(dest_next_ref, 1 - slot)

    for j in range(TOP_K):
        pltpu.make_async_copy(ys_ref.at[pl.ds(0, tm)], buf_ref.at[slot, j], sem.at[slot]).wait()
    w = wts_ref[...]
    f = w[:, 0:1] * buf_ref[slot, 0] + w[:, 1:2] * buf_ref[slot, 1]
    z = DEEPNORM_ALPHA * x_ref[...] + f
    xo_ref[...] = _layer_norm_rows(z, lg_ref[...], lb_ref[...], LN_EPS)


def _combine(x, dest_flat, wts, ys, ln_g, ln_b):
    t = x.shape[0]
    tm = min(512, t)
    n_tiles = t // tm
    rows = lambda width: pl.BlockSpec((tm, width), lambda i: (i, 0))
    vec = pl.BlockSpec((1, D_MODEL), lambda i: (0, 0))
    return pl.pallas_call(
        _combine_kernel,
        out_shape=jax.ShapeDtypeStruct((t, D_MODEL), F32),
        grid=(n_tiles,),
        in_specs=[pl.BlockSpec((tm * TOP_K,), lambda i: (i,), memory_space=pltpu.SMEM),
                  pl.BlockSpec((tm * TOP_K,), lambda i: (jnp.minimum(i + 1, n_tiles - 1),),
                               memory_space=pltpu.SMEM),
                  rows(D_MODEL), rows(LANES), pl.BlockSpec(memory_space=pl.ANY), vec, vec],
        out_specs=rows(D_MODEL),
        scratch_shapes=[pltpu.VMEM((2, TOP_K, tm, D_MODEL), F32), pltpu.SemaphoreType.DMA((2,))],
        compiler_params=_cparams("arbitrary"),
    )(dest_flat, dest_flat, x, wts, ys, ln_g, ln_b)


def _moe_ffn(x, w_router, wg, wu, wd, ln_g, ln_b):
    t = x.shape[0]
    wr_pad = jnp.pad(w_router, ((0, 0), (0, LANES - N_EXPERTS)))
    meta, wts, cnt = _router(x, wr_pad)
    counts = cnt[0, :N_EXPERTS].astype(I32)
    padded = (counts + MOE_ROWS - 1) // MOE_ROWS * MOE_ROWS
    seg_end = jnp.cumsum(padded)
    seg_start = seg_end - padded
    n_blocks = (t * TOP_K) // MOE_ROWS + N_EXPERTS
    dest = seg_start[meta[:, 0:TOP_K]] + meta[:, TOP_K:2 * TOP_K]
    dest_flat = dest.reshape(-1).astype(I32)
    blk_start = jnp.arange(n_blocks, dtype=I32) * MOE_ROWS
    blk_e = jnp.minimum(jnp.sum(blk_start[:, None] >= seg_end[None, :], axis=1), N_EXPERTS - 1).astype(I32)
    n_used = (seg_end[-1:] // MOE_ROWS).astype(I32)
    xs, (wg, wu, wd) = _dispatch(x, dest_flat, (seg_start + counts).astype(I32), (padded - counts).astype(I32),
                                 n_used, n_blocks * MOE_ROWS, (wg, wu, wd))
    ys = _experts(xs, blk_e, n_used, wg, wu, wd)
    return _combine(x, dest_flat, wts, ys, ln_g, ln_b)


def _pad_rows(w, rows, at=0):
    out = jnp.zeros((rows, w.shape[1]), w.dtype)
    return out.at[at:at + w.shape[0]].set(w)


def _reorder_in_projection(w, b):
    sizes = (GLA_KW, GLA_KW, GLA_VW, GLA_VW, GLA_DECAY_RANK, LRU_WIDTH, LRU_WIDTH,
             3 * RWKV_WIDTH, RWKV_DECAY_RANK, RWKV_A_RANK, RWKV_GATE_RANK, N_BRANCH * D_MODEL)
    offs = [0]
    for s in sizes:
        offs.append(offs[-1] + s)
    wb = jnp.concatenate([w, b[None, :]], axis=0)
    piece = lambda i: wb[:, offs[i]:offs[i + 1]]
    zeros = lambda n: jnp.zeros((wb.shape[0], n), wb.dtype)
    q, k, v, r, dec, lx, lg, rkv, wl, al, gl, gates = (piece(i) for i in range(len(sizes)))
    small = jnp.concatenate([wl, al, gl, zeros(RW_SMALL_W - SM_GL[0] - RWKV_GATE_RANK)], axis=1)
    groups = {"gla": jnp.concatenate([q, k, v, r, dec, zeros(GLA_DEC_W - GLA_DECAY_RANK)], axis=1),
              "lru": jnp.concatenate([lx, lg], axis=1),
              "rwkv": jnp.concatenate([rkv, small], axis=1),
              "gates": gates}
    return {name: (g[:-1].astype(BF16), g[-1:]) for name, g in groups.items()}


def _block_diag(blocks):
    n, bi, bo = blocks.shape
    eye = jnp.eye(n, dtype=blocks.dtype)
    return (eye[:, None, :, None] * blocks[:, :, None, :]).reshape(n * bi, n * bo)


def kernel(x, w_in, b_in, gla_w_decay_up, gla_b_decay, gla_norm_g, gla_norm_b, lru_conv_w, lru_conv_b, lru_w_r, lru_b_r, lru_w_i, lru_b_i, lru_lambda, rwkv_mu, rwkv_w0, rwkv_w2, rwkv_a0, rwkv_a2, rwkv_g2, rwkv_k_k, rwkv_k_a, rwkv_r_k, rwkv_lnx_g, rwkv_lnx_b, p_gla, p_lru, p_rwkv, w_out, ln_mix_g, ln_mix_b, ffn_w_gate, ffn_w_up, ffn_w_down, moe_w_router, moe_w_gate, moe_w_up, moe_w_down, ln_ffn_g, ln_ffn_b):
    batch, seq, d = x.shape
    t = batch * seq
    xf = x.reshape(t, d)
    row = lambda a: a.reshape(1, -1)
    head_ones = _block_diag(jnp.ones((RWKV_HEADS, RWKV_HEAD, RWKV_HEAD), BF16))
    for l in range(DEPTH):
        proj = _reorder_in_projection(w_in[l], b_in[l])
        w_gates, b_gates = proj["gates"]

        wup = _pad_rows(gla_w_decay_up[l], GLA_DEC_W).astype(BF16)
        o_gla = _gla_branch(xf, batch, seq, *proj["gla"], wup, row(gla_b_decay[l]), row(gla_norm_g[l]),
                            row(gla_norm_b[l]))

        w_ri = jnp.concatenate([_block_diag(lru_w_r[l]), _block_diag(lru_w_i[l])], axis=1).astype(BF16)
        b_ri = jnp.concatenate([lru_b_r[l], lru_b_i[l]])[None, :]
        o_lru = _lru_branch(xf, batch, seq, *proj["lru"], lru_conv_w[l], row(lru_conv_b[l]), w_ri, b_ri,
                            row(jax.nn.softplus(-lru_lambda[l])))

        mu = rwkv_mu[l]
        mu_all = jnp.zeros((3 * RWKV_WIDTH + RW_SMALL_W,), F32).at[0:mu.shape[0]].set(mu)
        prep_params = (*proj["rwkv"], row(mu_all),
                       _pad_rows(rwkv_w2[l], SM_WA[1], 0).astype(BF16),
                       _pad_rows(rwkv_a2[l], SM_WA[1], RWKV_DECAY_RANK).astype(BF16),
                       _pad_rows(rwkv_g2[l], SM_GL[1], 0).astype(BF16),
                       row(rwkv_w0[l]), row(rwkv_a0[l]), row(rwkv_k_k[l]), row(rwkv_k_a[l]),
                       row(rwkv_r_k[l]), head_ones)
        prep = _rwkv_prep(xf, batch, seq, prep_params)
        o_rwkv = _rwkv_scan(prep, batch, seq, row(rwkv_lnx_g[l]), row(rwkv_lnx_b[l]))

        xf = _merge(o_gla, o_lru, o_rwkv, xf, w_gates, b_gates, p_gla[l].astype(BF16), p_lru[l].astype(BF16),
                    p_rwkv[l].astype(BF16), w_out[l].astype(BF16), row(ln_mix_g[l]), row(ln_mix_b[l]))
        i = l // 2
        if l % 2 == 0:
            xf = _dense_ffn(xf, ffn_w_gate[i].astype(BF16), ffn_w_up[i].astype(BF16),
                            ffn_w_down[i].astype(BF16), row(ln_ffn_g[l]), row(ln_ffn_b[l]))
        else:
            xf = _moe_ffn(xf, moe_w_router[i], moe_w_gate[i], moe_w_up[i], moe_w_down[i],
                          row(ln_ffn_g[l]), row(ln_ffn_b[l]))
    return xf.reshape(batch, seq, d)
```

```python
import functools

import jax
import jax.numpy as jnp
from jax import lax
from jax.experimental import pallas as pl
from jax.experimental.pallas import tpu as pltpu

F32 = jnp.float32
BF16 = jnp.bfloat16
I32 = jnp.int32

D_MODEL = 1024
DEPTH = 2
GLA_HEADS, GLA_DK, GLA_DV = 4, 64, 128
GLA_KW, GLA_VW = GLA_HEADS * GLA_DK, GLA_HEADS * GLA_DV
GLA_DECAY_RANK = 16
GLA_TEMP = 16.0
LRU_WIDTH, LRU_BLOCKS, LRU_CONV, LRU_C = 512, 8, 4, 8.0
RWKV_HEAD, RWKV_WIDTH = 64, 512
RWKV_HEADS = RWKV_WIDTH // RWKV_HEAD
RWKV_DECAY_RANK, RWKV_A_RANK, RWKV_GATE_RANK = 64, 64, 160
RWKV_LNX_EPS = 64e-5
N_BRANCH = 3
N_EXPERTS, TOP_K = 8, 2
DEEPNORM_ALPHA = (2 * DEPTH) ** 0.25
LN_EPS = 1e-5

LANES = 128
SUBLANES = 8
VMEM_LIMIT_BYTES = 56 * 1024 * 1024

SM_WA = (0, 128)
SM_GL = (128, 256)
RW_SMALL_W = 512
GLA_DEC_W = LANES
GLA_COLS = {"q": (0, 256), "k": (256, 512), "v": (512, 1024), "r": (1024, 1536), "dec": (1536, 1664)}

CHUNK = 64
INV_BLOCK = 16
MOE_ROWS = 512


def _cparams(*sem):
    return pltpu.CompilerParams(dimension_semantics=sem, vmem_limit_bytes=VMEM_LIMIT_BYTES)


def _sigmoid(x):
    return 1.0 / (1.0 + jnp.exp(-x))


def _softplus(x):
    return jnp.maximum(x, 0.0) + jnp.log(1.0 + jnp.exp(-jnp.abs(x)))


def _bdot(a, b):
    return jnp.dot(a.astype(BF16), b.astype(BF16), preferred_element_type=F32)


def _bdot_nt(a, b):
    return lax.dot_general(a.astype(BF16), b.astype(BF16), (((1,), (1,)), ((), ())),
                           preferred_element_type=F32)


def _bdot_tn(a, b):
    return lax.dot_general(a.astype(BF16), b.astype(BF16), (((0,), (0,)), ((), ())),
                           preferred_element_type=F32)


def _fdot(a, b):
    ah = a.astype(BF16)
    al = (a - ah.astype(F32)).astype(BF16)
    bh = b.astype(BF16)
    bl = (b - bh.astype(F32)).astype(BF16)
    dot = lambda u, v: jnp.dot(u, v, preferred_element_type=F32)
    return dot(al, bh) + dot(ah, bl) + dot(ah, bh)


def _split3(x):
    hi = x.astype(BF16)
    rest = x - hi.astype(F32)
    mid = rest.astype(BF16)
    lo = (rest - mid.astype(F32)).astype(BF16)
    return hi, mid, lo


def _mask_dot(mask_bf16, x):
    hi, mid, lo = _split3(x)
    dot = lambda part: jnp.dot(mask_bf16, part, preferred_element_type=F32)
    return dot(lo) + dot(mid) + dot(hi)


def _dot_mask(x, mask_bf16):
    hi, mid, lo = _split3(x)
    dot = lambda part: jnp.dot(part, mask_bf16, preferred_element_type=F32)
    return dot(lo) + dot(mid) + dot(hi)


def _layer_norm_rows(z, g, b, eps):
    mu = jnp.mean(z, axis=-1, keepdims=True)
    zc = z - mu
    var = jnp.mean(zc * zc, axis=-1, keepdims=True)
    return zc * lax.rsqrt(var + eps) * g + b


def _tri_masks(n):
    row = lax.broadcasted_iota(I32, (n, n), 0)
    col = lax.broadcasted_iota(I32, (n, n), 1)
    return row, col


MXU_COLS = 256


def _gla_kernel(*refs, chunks, blocks_per_seq):
    *io_refs, st_ref, ha_ref, hb_ref = refs
    s = pl.program_id(0)

    @pl.when(s == 0)
    def _():
        hb_ref[...] = jnp.zeros_like(hb_ref)

    @pl.when((s == 0) | (((s + blocks_per_seq - 1) % blocks_per_seq) == 0))
    def _():
        st_ref[...] = jnp.zeros_like(st_ref)

    body = functools.partial(_gla_body, chunks, (*io_refs, st_ref))

    @pl.when((s & 1) == 0)
    def _():
        body(ha_ref, hb_ref)

    @pl.when((s & 1) == 1)
    def _():
        body(hb_ref, ha_ref)


def _gla_body(chunks, refs, h_next, h_ref):
    x_ref, w_ref, b_ref, wup_ref, bdec_ref, ng_ref, nb_ref, o_ref, st_ref = refs
    xb = x_ref[...].astype(BF16)
    width = w_ref.shape[1]
    pieces = iter([slice(c0, min(c0 + MXU_COLS, width)) for c0 in range(0, width, MXU_COLS)])

    def project_piece():
        cols = next(pieces, None)
        if cols is not None:
            h_next[:, cols] = jnp.dot(xb, w_ref[:, cols], preferred_element_type=F32) + b_ref[:, cols]

    q_ref = h_ref.at[:, GLA_COLS["q"][0]:GLA_COLS["q"][1]]
    k_ref = h_ref.at[:, GLA_COLS["k"][0]:GLA_COLS["k"][1]]
    v_ref = h_ref.at[:, GLA_COLS["v"][0]:GLA_COLS["v"][1]]
    r_ref = h_ref.at[:, GLA_COLS["r"][0]:GLA_COLS["r"][1]]
    dec_ref = h_ref.at[:, GLA_COLS["dec"][0]:GLA_COLS["dec"][1]]

    row, col = _tri_masks(CHUNK)
    ltri = (row >= col).astype(BF16)
    kw, vw = 2 * GLA_DK, 2 * GLA_DV
    key_shift, val_shift = GLA_DK.bit_length() - 1, GLA_DV.bit_length() - 1
    iota = lambda shape, axis: lax.broadcasted_iota(I32, shape, axis)
    causal = iota((CHUNK, kw), 0) >= (iota((CHUNK, kw), 1) & (GLA_DK - 1))
    key_diag = (iota((kw, kw), 0) >> key_shift) == (iota((kw, kw), 1) >> key_shift)
    val_diag = (iota((kw, vw), 0) >> key_shift) == (iota((kw, vw), 1) >> val_shift)
    state_diag = (iota((vw, kw), 0) >> val_shift) == (iota((vw, kw), 1) >> key_shift)
    zero = jnp.zeros((), BF16)
    pairs = range(GLA_HEADS // 2)
    qp, vp, k_diag, kep, decays = [], [], [], [], []
    for c in range(chunks):
        project_piece()
        rows = pl.ds(c * CHUNK, CHUNK)
        logits = _bdot(dec_ref[rows, :], wup_ref[...]) + bdec_ref[...]
        log_a = -_softplus(-logits) * (1.0 / GLA_TEMP)
        cum = _mask_dot(ltri, log_a)
        cum_end = cum[CHUNK - 1:CHUNK, :]
        q = q_ref[rows, :] * (GLA_DK ** -0.5)
        k = k_ref[rows, :]
        q_dec = (q * jnp.exp(cum)).astype(BF16)
        k_inv = (k * jnp.exp(-cum)).astype(BF16)
        k_end = (k * jnp.exp(cum_end - cum)).astype(BF16)
        decay_end = jnp.exp(cum_end)
        v = v_ref[rows, :].astype(BF16)
        for p in pairs:
            s = slice(p * kw, (p + 1) * kw)
            qp.append(q_dec[:, s])
            vp.append(v[:, p * vw:(p + 1) * vw])
            k_diag.append(jnp.where(key_diag, jnp.concatenate([k_inv[:, s], k_inv[:, s]], axis=0), zero))
            kep.append(k_end[:, s])
            decays.append(decay_end[:, s])

    for _ in range(chunks):
        project_piece()

    v_diag = [jnp.where(val_diag, jnp.concatenate([x, x], axis=0), zero) for x in vp]
    scores = [jnp.where(causal, _bdot_nt(q, kd), 0.0) for q, kd in zip(qp, k_diag)]
    o_local = [_bdot(s, vd) for s, vd in zip(scores, v_diag)]
    upd = [jnp.where(state_diag, _bdot_tn(x, ke), 0.0) for x, ke in zip(vp, kep)]

    state = [st_ref[p] for p in pairs]
    n_pairs = len(pairs)
    for c in range(chunks):
        rows = pl.ds(c * CHUNK, CHUNK)
        item = slice(c * n_pairs, (c + 1) * n_pairs)
        o = [ol + _bdot_nt(q, st) for ol, q, st in zip(o_local[item], qp[item], state)]
        state = [st * dcy + u for st, dcy, u in zip(state, decays[item], upd[item])]
        outs = []
        for op in o:
            for oh in (op[:, :GLA_DV], op[:, GLA_DV:]):
                mu = jnp.mean(oh, axis=-1, keepdims=True)
                oc = oh - mu
                var = jnp.mean(oc * oc, axis=-1, keepdims=True)
                outs.append(oc * lax.rsqrt(var + LN_EPS))
        y = jnp.concatenate(outs, axis=1) * ng_ref[...] + nb_ref[...]
        r = r_ref[rows, :]
        o_ref[rows, :] = (y * (r * _sigmoid(r))).astype(o_ref.dtype)
    for p in pairs:
        st_ref[p] = state[p]


def _gla_branch(x, batch, seq, w, bias, wup_pad, b_decay, norm_g, norm_b):
    t = batch * seq
    chunks = 8
    rows = CHUNK * chunks
    nblk = seq // rows

    n_tiles = batch * nblk

    def full(a):
        return pl.BlockSpec(a.shape, lambda s: (0,) * a.ndim)

    args = (w, bias, wup_pad, b_decay, norm_g, norm_b)
    return pl.pallas_call(
        functools.partial(_gla_kernel, chunks=chunks, blocks_per_seq=nblk),
        out_shape=jax.ShapeDtypeStruct((t, GLA_VW), BF16),
        grid=(n_tiles + 1,),
        in_specs=[pl.BlockSpec((rows, D_MODEL), lambda s: (jnp.minimum(s, n_tiles - 1), 0))]
        + [full(a) for a in args],
        out_specs=pl.BlockSpec((rows, GLA_VW), lambda s: (jnp.maximum(s - 1, 0), 0)),
        scratch_shapes=[pltpu.VMEM((GLA_HEADS // 2, 2 * GLA_DV, 2 * GLA_DK), F32),
                        pltpu.VMEM((rows, w.shape[1]), F32), pltpu.VMEM((rows, w.shape[1]), F32)],
        compiler_params=_cparams("arbitrary"),
    )(x, *args)


def _lru_kernel(xin_ref, w_ref, b_ref, cw_ref, cb_ref, wri_ref, bri_ref, sp_ref,
                o_ref, xbuf_ref, a_ref, u_ref, hs_ref, gate_ref, h_ref):
    tm = xin_ref.shape[0]
    pad = SUBLANES

    @pl.when(pl.program_id(1) == 0)
    def _():
        xbuf_ref[0:pad, :] = jnp.zeros((pad, LRU_WIDTH), F32)
        h_ref[...] = jnp.zeros_like(h_ref)

    proj = jnp.dot(xin_ref[...].astype(BF16), w_ref[...], preferred_element_type=F32) + b_ref[...]
    x = proj[:, :LRU_WIDTH]
    gate_ref[...] = jax.nn.gelu(proj[:, LRU_WIDTH:])
    xbuf_ref[pad:pad + tm, :] = x
    xc = cb_ref[...] + x * cw_ref[LRU_CONV - 1:LRU_CONV, :]
    for j in range(LRU_CONV - 1):
        back = LRU_CONV - 1 - j
        xc = xc + xbuf_ref[pl.ds(pad - back, tm), :] * cw_ref[j:j + 1, :]
    xbuf_ref[0:pad, :] = x[tm - pad:tm, :]

    ri = _sigmoid(_bdot(xc, wri_ref[...]) + bri_ref[...])
    rg, ig = ri[:, :LRU_WIDTH], ri[:, LRU_WIDTH:]
    log_a = -LRU_C * rg * sp_ref[...]
    a = jnp.exp(log_a)
    a_ref[...] = a
    u_ref[...] = jnp.sqrt(1.0 - a * a) * (ig * xc)

    def step(t, hprev):
        hnew = a_ref[pl.ds(t, 1), :] * hprev + u_ref[pl.ds(t, 1), :]
        hs_ref[pl.ds(t, 1), :] = hnew
        return hnew

    h_ref[...] = lax.fori_loop(0, tm, step, h_ref[...], unroll=8)
    o_ref[...] = (gate_ref[...] * hs_ref[...]).astype(o_ref.dtype)


def _lru_branch(x, batch, seq, w, bias, conv_w, conv_b, w_ri, b_ri, softplus_neg_lam):
    t = batch * seq
    tm = min(1024, seq)
    nblk = seq // tm

    def full(a):
        return pl.BlockSpec(a.shape, lambda b, n: (0,) * a.ndim)

    args = (w, bias, conv_w, conv_b, w_ri, b_ri, softplus_neg_lam)
    return pl.pallas_call(
        _lru_kernel,
        out_shape=jax.ShapeDtypeStruct((t, LRU_WIDTH), BF16),
        grid=(batch, nblk),
        in_specs=[pl.BlockSpec((tm, D_MODEL), lambda b, n: (b * nblk + n, 0))] + [full(a) for a in args],
        out_specs=pl.BlockSpec((tm, LRU_WIDTH), lambda b, n: (b * nblk + n, 0)),
        scratch_shapes=[pltpu.VMEM((tm + SUBLANES, LRU_WIDTH), F32)]
        + [pltpu.VMEM((tm, LRU_WIDTH), F32)] * 4 + [pltpu.VMEM((1, LRU_WIDTH), F32)],
        compiler_params=_cparams("parallel", "arbitrary"),
    )(x, *args)


def _rwkv_prep_kernel(*refs, blocks_per_seq):
    *io_refs, ha_ref, hb_ref, carry_ref = refs
    tm = io_refs[0].shape[0]
    s = pl.program_id(0)
    body = functools.partial(_rwkv_prep_body, tm, s, RWKV_WIDTH, blocks_per_seq, (*io_refs, carry_ref))

    @pl.when(s == 0)
    def _():
        hb_ref[...] = jnp.zeros_like(hb_ref)
        carry_ref[...] = jnp.zeros_like(carry_ref)

    @pl.when((s & 1) == 0)
    def _():
        body(ha_ref, hb_ref)

    @pl.when((s & 1) == 1)
    def _():
        body(hb_ref, ha_ref)


def _rwkv_prep_body(tm, s, gw, blocks_per_seq, refs, h_next, h_cur):
    (x_ref, w_ref, b_ref, mu_ref, w2_ref, a2_ref, g2_ref, w0_ref, a0_ref, kk_ref, ka_ref, rk_ref, bd_ref,
     r_out, k_out, v_out, lw_out, kk_out, kka_out, bonus_out, g_out, carry_ref) = refs
    first = ((s + blocks_per_seq - 1) % blocks_per_seq) == 0
    keep = jnp.where(first, 0.0, 1.0)
    row0 = lax.broadcasted_iota(I32, (tm, 1), 0) == 0
    xb = x_ref[...].astype(BF16)

    pw = gw // 2

    def project(j):
        cols = slice(j * pw, (j + 1) * pw)
        h_next[:, cols] = jnp.dot(xb, w_ref[:, cols], preferred_element_type=F32) + b_ref[:, cols]

    def shifted(j):
        cols = slice(j * gw, (j + 1) * gw)
        cur = h_cur[:, cols]
        prev_row = carry_ref[0:1, cols] * keep
        carry_ref[0:1, cols] = cur[tm - 1:tm, :]
        prev = jnp.where(row0, prev_row, pltpu.roll(cur, 1, axis=0))
        return cur + (prev - cur) * mu_ref[:, cols]

    project(0)
    sm = shifted(3)
    wa = sm[:, SM_WA[0]:SM_WA[0] + SM_WA[1]]
    gl = sm[:, SM_GL[0]:SM_GL[0] + SM_GL[1]]
    project(1)
    w_log = -_softplus(-(w0_ref[...] + _bdot(jnp.tanh(wa), w2_ref[...]))) - 0.5
    lw_out[...] = -jnp.exp(w_log)
    project(2)
    a = _sigmoid(a0_ref[...] + _bdot(wa, a2_ref[...]))
    g_out[...] = _bdot(_sigmoid(gl), g2_ref[...])
    project(3)
    k = shifted(1)
    kk = k * kk_ref[...]
    norm = jnp.sqrt(_dot_mask(kk * kk, bd_ref[...]))
    project(4)
    kk = kk / jnp.maximum(norm, 1e-12)
    kk_out[...] = kk
    kka_out[...] = kk * a
    k2 = k * (1.0 + (a - 1.0) * ka_ref[...])
    k_out[...] = k2
    project(5)
    r = shifted(0)
    r_out[...] = r
    project(6)
    v = shifted(2)
    v_out[...] = v.astype(v_out.dtype)
    project(7)
    bonus_out[...] = _dot_mask(r * k2 * rk_ref[...], bd_ref[...]) * v


def _rwkv_prep(x, batch, seq, params):
    t = batch * seq
    tm = min(512, seq)
    bps = seq // tm
    n_tiles = t // tm
    width = params[0].shape[1]

    def full(a):
        return pl.BlockSpec(a.shape, lambda s: (0,) * a.ndim)

    out = lambda dtype: jax.ShapeDtypeStruct((t, RWKV_WIDTH), dtype)
    ospec = pl.BlockSpec((tm, RWKV_WIDTH), lambda s: (jnp.maximum(s - 1, 0), 0))
    out_dtypes = (F32, F32, BF16, F32, F32, F32, F32, F32)
    return pl.pallas_call(
        functools.partial(_rwkv_prep_kernel, blocks_per_seq=bps),
        out_shape=[out(dt) for dt in out_dtypes],
        grid=(n_tiles + 1,),
        in_specs=[pl.BlockSpec((tm, D_MODEL), lambda s: (jnp.minimum(s, n_tiles - 1), 0))]
        + [full(a) for a in params],
        out_specs=[ospec] * 8,
        scratch_shapes=[pltpu.VMEM((tm, width), F32), pltpu.VMEM((tm, width), F32),
                        pltpu.VMEM((SUBLANES, width), F32)],
        compiler_params=_cparams("arbitrary"),
    )(x, *params)


PAIR_W = 2 * RWKV_HEAD
N_PAIRS = RWKV_HEADS // 2
HEAD_SHIFT = RWKV_HEAD.bit_length() - 1


def _pair_masks():
    t = lax.broadcasted_iota(I32, (CHUNK, PAIR_W), 0)
    lane = lax.broadcasted_iota(I32, (CHUNK, PAIR_W), 1)
    j = lane & (RWKV_HEAD - 1)
    r = lax.broadcasted_iota(I32, (PAIR_W, PAIR_W), 0)
    c = lax.broadcasted_iota(I32, (PAIR_W, PAIR_W), 1)
    r4 = lax.broadcasted_iota(I32, (2 * PAIR_W, PAIR_W), 0)
    c4 = lax.broadcasted_iota(I32, (2 * PAIR_W, PAIR_W), 1)
    return dict(
        strict=t > j, incl=t >= j, eye=(t == j).astype(F32),
        same_block=jnp.bitwise_xor(t, j) < INV_BLOCK,
        first_head=lane < RWKV_HEAD,
        block_diag=(r >> HEAD_SHIFT) == (c >> HEAD_SHIFT),
        stacked_diag=((r4 >> HEAD_SHIFT) & 1) == (c4 >> HEAD_SHIFT))


N_SLOT_REFS = 5


def _rwkv_scan_kernel(*refs, chunks, blocks_per_seq):
    *io_refs, st_ref = refs[:-2 * N_SLOT_REFS]
    slot_a, slot_b = refs[-2 * N_SLOT_REFS:-N_SLOT_REFS], refs[-N_SLOT_REFS:]
    s = pl.program_id(0)

    @pl.when(s == 0)
    def _():
        for ref in slot_b:
            ref[...] = jnp.zeros_like(ref)

    @pl.when((s == 0) | (((s + blocks_per_seq - 1) % blocks_per_seq) == 0))
    def _():
        st_ref[...] = jnp.zeros_like(st_ref)

    body = functools.partial(_rwkv_scan_body, chunks, (*io_refs, st_ref))

    @pl.when((s & 1) == 0)
    def _():
        body(slot_a, slot_b)

    @pl.when((s & 1) == 1)
    def _():
        body(slot_b, slot_a)


def _rwkv_scan_body(chunks, refs, wr, rd):
    r_ref, k_ref, v_ref, lw_ref, kk_ref, kka_ref, bonus_ref, g_ref, lg_ref, lb_ref, o_ref, st_ref = refs
    yl_w, rm_w, gm_w, qm_w, gam_w = wr
    yl_r, rm_r, gm_r, qm_r, gam_r = rd
    row, col = _tri_masks(CHUNK)
    ltri = (row >= col).astype(BF16)
    m = _pair_masks()
    zero = jnp.zeros((), BF16)
    pairs = range(N_PAIRS)
    sl = [slice(p * PAIR_W, (p + 1) * PAIR_W) for p in pairs]

    def block_diag(x):
        xb = x.astype(BF16)
        return jnp.where(m["block_diag"], jnp.concatenate([xb, xb], axis=0), zero)

    def pair_mul(lhs, *rhs):
        rb = jnp.concatenate([block_diag(x) for x in rhs], axis=1) if len(rhs) > 1 else block_diag(rhs[0])
        out = jnp.dot(lhs.astype(BF16), rb, preferred_element_type=F32)
        return [out[:, i * PAIR_W:(i + 1) * PAIR_W] for i in range(len(rhs))]

    def each(fn, *lists):
        return [fn(*args) for args in zip(*lists)]

    carried = {"state": [st_ref[p] for p in pairs]}
    pending = iter(range(chunks))
    inv_n = 1.0 / RWKV_HEAD

    def head_stat(z):
        s0 = jnp.sum(jnp.where(m["first_head"], z, 0.0), axis=-1, keepdims=True)
        s1 = jnp.sum(jnp.where(m["first_head"], 0.0, z), axis=-1, keepdims=True)
        return jnp.where(m["first_head"], s0, s1) * inv_n

    def previous_tile_chunk():
        c = next(pending, None)
        if c is None:
            return
        rows = pl.ds(c * CHUNK, CHUNK)
        items = range(c * N_PAIRS, (c + 1) * N_PAIRS)
        state = carried["state"]
        y = [yl_r[i] + _bdot_nt(rm_r[i], st) for i, st in zip(items, state)]
        carried["state"] = [st * gam_r[i, 0:1, :] + _bdot_nt(st, gm_r[i]) + qm_r[i] for i, st in zip(items, state)]
        outs = []
        for yp in y:
            yc = yp - head_stat(yp)
            outs.append(yc * lax.rsqrt(head_stat(yc * yc) + RWKV_LNX_EPS))
        yn = jnp.concatenate(outs, axis=1) * lg_ref[...] + lb_ref[...]
        o_ref[rows, :] = ((yn + bonus_ref[rows, :]) * g_ref[rows, :]).astype(o_ref.dtype)

    lhs, rhs, vp, ends, gammas = [], [], [], [], []
    for c in range(chunks):
        rows = pl.ds(c * CHUNK, CHUNK)
        lw = lw_ref[rows, :]
        cum = _mask_dot(ltri, lw)
        cum_end = cum[CHUNK - 1:CHUNK, :]
        e_inv = jnp.exp(-cum)
        e_end = jnp.exp(cum_end - cum)
        kka = kka_ref[rows, :]
        k = k_ref[rows, :]
        a_bar = (-kk_ref[rows, :] * jnp.exp(cum - lw)).astype(BF16)
        r_bar = (r_ref[rows, :] * jnp.exp(cum)).astype(BF16)
        b_til = (kka * e_inv).astype(BF16)
        k_til = (k * e_inv).astype(BF16)
        b_end = (kka * e_end).astype(BF16)
        k_end = (k * e_end).astype(BF16)
        gamma = jnp.exp(cum_end)
        v = v_ref[rows, :].astype(BF16)

        for s in sl:
            lhs.append(jnp.concatenate([a_bar[:, s], r_bar[:, s]], axis=0))
            rhs.append(jnp.where(m["stacked_diag"], jnp.concatenate(
                [b_til[:, s], b_til[:, s], k_til[:, s], k_til[:, s]], axis=0), zero))
            vp.append(v[:, s])
            ends.append(jnp.concatenate([b_end[:, s], k_end[:, s]], axis=0))
            gammas.append(gamma[:, s])

    blocks = each(_bdot_nt, lhs, rhs)
    previous_tile_chunk()
    a_ab = [jnp.where(m["strict"], b[:CHUNK, :PAIR_W], 0.0) for b in blocks]
    a_ak = [jnp.where(m["strict"], b[:CHUNK, PAIR_W:], 0.0) for b in blocks]
    r_ab = [jnp.where(m["incl"], b[CHUNK:, :PAIR_W], 0.0) for b in blocks]
    r_ak = [jnp.where(m["incl"], b[CHUNK:, PAIR_W:], 0.0) for b in blocks]
    from_v = [jnp.dot(jnp.concatenate([ak, rk], axis=0).astype(BF16), block_diag(x),
                      preferred_element_type=F32) for ak, rk, x in zip(a_ak, r_ak, vp)]
    previous_tile_chunk()
    d = [jnp.where(m["same_block"], a, 0.0) for a in a_ab]
    low = [a - di for a, di in zip(a_ab, d)]
    pm = [m["eye"] + di for di in d]
    dpow = [pair_mul(di, di)[0] for di in d]
    previous_tile_chunk()
    for _ in range(INV_BLOCK.bit_length() - 3):
        both = each(lambda x, p: pair_mul(x, x, p), dpow, pm)
        dpow = [b[0] for b in both]
        pm = [p + b[1] for p, b in zip(pm, both)]
        previous_tile_chunk()
    pm = [p + pair_mul(x, p)[0] for p, x in zip(pm, dpow)]
    previous_tile_chunk()
    both = each(lambda p, lo, fv, l: pair_mul(p, lo, fv[:CHUNK], l[:CHUNK]), pm, low, from_v, lhs)
    previous_tile_chunk()
    both2 = each(lambda b: pair_mul(b[0], b[0], b[1], b[2]), both)
    previous_tile_chunk()
    xw_v = [b[1] + b2[1] for b, b2 in zip(both, both2)]
    xw_a = [b[2] + b2[2] for b, b2 in zip(both, both2)]
    both3 = each(lambda b2, xv, xa: pair_mul(b2[0], xv, xa), both2, xw_v, xw_a)
    u_v = [xv + b3[0] for xv, b3 in zip(xw_v, both3)]
    wm = [xa + b3[1] for xa, b3 in zip(xw_a, both3)]
    both4 = each(lambda rb, uv, w: pair_mul(rb, uv, w), r_ab, u_v, wm)
    y_local = [fv[CHUNK:] + b4[0] for fv, b4 in zip(from_v, both4)]
    rm = [l[CHUNK:] + b4[1] for l, b4 in zip(lhs, both4)]
    gm = [jnp.where(m["block_diag"], _bdot_tn(en[:CHUNK], w), 0.0) for en, w in zip(ends, wm)]
    qm = [jnp.where(m["block_diag"], _bdot_tn(jnp.concatenate([uv.astype(BF16), x], axis=0), en), 0.0)
          for uv, x, en in zip(u_v, vp, ends)]
    for _ in range(chunks):
        previous_tile_chunk()
    for p in pairs:
        st_ref[p] = carried["state"][p]

    for i in range(chunks * N_PAIRS):
        yl_w[i] = y_local[i]
        rm_w[i] = rm[i]
        gm_w[i] = gm[i]
        qm_w[i] = qm[i]
        gam_w[i, 0:1, :] = gammas[i]


def _rwkv_scan(prep, batch, seq, lnx_g, lnx_b):
    t = batch * seq
    chunks = 8
    rows = CHUNK * chunks
    nblk = seq // rows
    n_tiles = batch * nblk
    n_items = chunks * N_PAIRS
    cur = pl.BlockSpec((rows, RWKV_WIDTH), lambda s: (jnp.minimum(s, n_tiles - 1), 0))
    prev = pl.BlockSpec((rows, RWKV_WIDTH), lambda s: (jnp.maximum(s - 1, 0), 0))
    pspec = pl.BlockSpec((1, RWKV_WIDTH), lambda s: (0, 0))
    slot = [pltpu.VMEM((n_items, CHUNK, PAIR_W), F32), pltpu.VMEM((n_items, CHUNK, PAIR_W), F32),
            pltpu.VMEM((n_items, PAIR_W, PAIR_W), F32), pltpu.VMEM((n_items, PAIR_W, PAIR_W), F32),
            pltpu.VMEM((n_items, SUBLANES, PAIR_W), F32)]
    return pl.pallas_call(
        functools.partial(_rwkv_scan_kernel, chunks=chunks, blocks_per_seq=nblk),
        out_shape=jax.ShapeDtypeStruct((t, RWKV_WIDTH), BF16),
        grid=(n_tiles + 1,),
        in_specs=[cur] * 6 + [prev, prev, pspec, pspec],
        out_specs=prev,
        scratch_shapes=[pltpu.VMEM((N_PAIRS, PAIR_W, PAIR_W), F32)] + slot + slot,
        compiler_params=_cparams("arbitrary"),
    )(*prep, lnx_g, lnx_b)


def _merge_kernel(og_ref, ol_ref, or_ref, x_ref, wgt_ref, bgt_ref, pg_ref, pl_ref, pr_ref, wo_ref,
                  lg_ref, lb_ref, xo_ref):
    d = D_MODEL
    xin = x_ref[...].astype(BF16)
    merged = None
    for b, (o_ref, p_ref) in enumerate(((og_ref, pg_ref), (ol_ref, pl_ref), (or_ref, pr_ref))):
        cols = slice(b * d, (b + 1) * d)
        gate = _sigmoid(jnp.dot(xin, wgt_ref[:, cols], preferred_element_type=F32) + bgt_ref[:, cols])
        term = gate * jnp.dot(o_ref[...], p_ref[...], preferred_element_type=F32)
        merged = term if merged is None else merged + term
    z = DEEPNORM_ALPHA * x_ref[...] + _bdot(merged, wo_ref[...])
    out = _layer_norm_rows(z, lg_ref[...], lb_ref[...], LN_EPS)
    xo_ref[...] = out


def _merge(o_gla, o_lru, o_rwkv, x, w_gates, b_gates, p_gla, p_lru, p_rwkv, w_out, ln_g, ln_b):
    t = x.shape[0]
    tm = min(1024, t)

    def rows(width):
        return pl.BlockSpec((tm, width), lambda i: (i, 0))

    def full(a):
        return pl.BlockSpec(a.shape, lambda i: (0,) * a.ndim, pipeline_mode=pl.Buffered(1))

    params = (w_gates, b_gates, p_gla, p_lru, p_rwkv, w_out, ln_g, ln_b)
    return pl.pallas_call(
        _merge_kernel,
        out_shape=jax.ShapeDtypeStruct((t, D_MODEL), F32),
        grid=(t // tm,),
        in_specs=[rows(GLA_VW), rows(LRU_WIDTH), rows(RWKV_WIDTH), rows(D_MODEL)] + [full(a) for a in params],
        out_specs=rows(D_MODEL),
        compiler_params=_cparams("parallel"),
    )(o_gla, o_lru, o_rwkv, x, *params)


FFN_SPLIT = 2


def _ffn_kernel(x_ref, wg_ref, wu_ref, wd_ref, lg_ref, lb_ref, xo_ref):
    x = x_ref[...]
    xb = x.astype(BF16)
    tf = wg_ref.shape[1] // FFN_SPLIT
    acc = None
    for f in range(FFN_SPLIT):
        cols = slice(f * tf, (f + 1) * tf)
        g = jnp.dot(xb, wg_ref[:, cols], preferred_element_type=F32)
        u = jnp.dot(xb, wu_ref[:, cols], preferred_element_type=F32)
        part = _bdot(g * _sigmoid(g) * u, wd_ref[cols, :])
        acc = part if acc is None else acc + part
    z = DEEPNORM_ALPHA * x + acc
    xo_ref[...] = _layer_norm_rows(z, lg_ref[...], lb_ref[...], LN_EPS)


def _dense_ffn(x, wg, wu, wd, ln_g, ln_b):
    t = x.shape[0]
    tm = min(1024, t)
    rows = pl.BlockSpec((tm, D_MODEL), lambda i: (i, 0))
    resident = lambda a: pl.BlockSpec(a.shape, lambda i: (0,) * a.ndim, pipeline_mode=pl.Buffered(1))
    return pl.pallas_call(
        _ffn_kernel,
        out_shape=jax.ShapeDtypeStruct((t, D_MODEL), F32),
        grid=(t // tm,),
        in_specs=[rows] + [resident(a) for a in (wg, wu, wd, ln_g, ln_b)],
        out_specs=rows,
        compiler_params=_cparams("parallel"),
    )(x, wg, wu, wd, ln_g, ln_b)


ROW_TILE = (D_MODEL // LANES, LANES)
assert ROW_TILE[0] == SUBLANES


def _rows_to_tiles(x, ref3):
    ref3[...] = x.reshape(x.shape[0], *ROW_TILE)


def _tiles_to_rows(ref3):
    return ref3[...].reshape(ref3.shape[0], D_MODEL)


def _router_kernel(x_ref, wr_ref, meta_ref, wts_ref, cnt_ref, carry_ref):
    tm = x_ref.shape[0]

    @pl.when(pl.program_id(0) == 0)
    def _():
        carry_ref[...] = jnp.zeros_like(carry_ref)

    lane = lax.broadcasted_iota(I32, (tm, LANES), 1)
    neg = jnp.float32(-jnp.inf)
    logits = jnp.where(lane < N_EXPERTS, _fdot(x_ref[...], wr_ref[...]), neg)
    m1 = jnp.max(logits, axis=-1, keepdims=True)
    e1 = jnp.min(jnp.where(logits == m1, lane, LANES), axis=-1, keepdims=True)
    rest = jnp.where(lane == e1, neg, logits)
    m2 = jnp.max(rest, axis=-1, keepdims=True)
    e2 = jnp.min(jnp.where(rest == m2, lane, LANES), axis=-1, keepdims=True)
    ex = jnp.exp(m2 - m1)
    w1 = 1.0 / (1.0 + ex)
    w2 = ex / (1.0 + ex)

    hot1 = lane == e1
    hot2 = lane == e2
    onehot = jnp.where(hot1 | hot2, 1.0, 0.0)
    row, col = _tri_masks(tm)
    before = (row > col).astype(BF16)
    prefix = jnp.dot(before, onehot.astype(BF16), preferred_element_type=F32) + carry_ref[...]
    rank1 = jnp.sum(jnp.where(hot1, prefix, 0.0), axis=-1, keepdims=True).astype(I32)
    rank2 = jnp.sum(jnp.where(hot2, prefix, 0.0), axis=-1, keepdims=True).astype(I32)
    carry_ref[...] += jnp.sum(onehot, axis=0, keepdims=True)
    cnt_ref[...] = carry_ref[...]

    meta = jnp.where(lane == 0, e1, jnp.where(lane == 1, e2, jnp.where(lane == 2, rank1, rank2)))
    meta_ref[...] = jnp.where(lane < 4, meta, 0)
    wts_ref[...] = jnp.where(lane == 0, w1, jnp.where(lane == 1, w2, 0.0))


def _router(x, wr_pad):
    t = x.shape[0]
    tm = min(512, t)
    return pl.pallas_call(
        _router_kernel,
        out_shape=[jax.ShapeDtypeStruct((t, LANES), I32), jax.ShapeDtypeStruct((t, LANES), F32),
                   jax.ShapeDtypeStruct((1, LANES), F32)],
        grid=(t // tm,),
        in_specs=[pl.BlockSpec((tm, D_MODEL), lambda i: (i, 0)),
                  pl.BlockSpec((D_MODEL, LANES), lambda i: (0, 0))],
        out_specs=[pl.BlockSpec((tm, LANES), lambda i: (i, 0)), pl.BlockSpec((tm, LANES), lambda i: (i, 0)),
                   pl.BlockSpec((1, LANES), lambda i: (0, 0))],
        scratch_shapes=[pltpu.VMEM((1, LANES), F32)],
        compiler_params=_cparams("arbitrary"),
    )(x, wr_pad)


def _dispatch_kernel(pad_start_ref, pad_len_ref, nused_ref, dest_ref, x_ref, *rest):
    n_w = (len(rest) - 5) // 2
    w_refs, xs_ref, wb_refs = rest[:n_w], rest[n_w], rest[n_w + 1:2 * n_w + 1]
    x3_ref, zeros_ref, sem, fill_sem = rest[-4:]
    tm = x_ref.shape[0]
    n_blocks = xs_ref.shape[0] // MOE_ROWS
    _rows_to_tiles(x_ref[...], x3_ref)

    @pl.when(pl.program_id(0) == pl.num_programs(0) - 1)
    def _():
        zeros_ref[...] = jnp.zeros_like(zeros_ref)
        pad_copy = lambda row: pltpu.make_async_copy(zeros_ref.at[0], xs_ref.at[row], fill_sem)
        blk_copy = lambda blk: pltpu.make_async_copy(
            zeros_ref, xs_ref.at[pl.ds(pl.multiple_of(blk * MOE_ROWS, MOE_ROWS), MOE_ROWS)], fill_sem)

        def for_each(fn):
            for e in range(N_EXPERTS):
                base = pad_start_ref[e]
                lax.fori_loop(0, pad_len_ref[e], lambda r, c: (fn(pad_copy(base + r)), c)[1], 0)
            lax.fori_loop(nused_ref[0], n_blocks, lambda b, c: (fn(blk_copy(b)), c)[1], 0)

        for_each(lambda cp: cp.start())
        for_each(lambda cp: cp.wait())

    def row_copy(r, j):
        return pltpu.make_async_copy(x3_ref.at[r], xs_ref.at[dest_ref[TOP_K * r + j]], sem)

    def start(r, c):
        for j in range(TOP_K):
            row_copy(r, j).start(priority=j)
        return c

    lax.fori_loop(0, tm, start, 0, unroll=8)
    for w_ref, wb_ref in zip(w_refs, wb_refs):
        wb_ref[...] = w_ref[...].astype(wb_ref.dtype)
    for j in range(TOP_K):
        pltpu.make_async_copy(x3_ref, xs_ref.at[pl.ds(0, tm)], sem).wait()


def _dispatch(x, dest_flat, pad_start, pad_len, n_used, n_slots, weights):
    t = x.shape[0]
    tm = min(256, t)
    n_steps = t // tm
    flat = [w.reshape(-1, w.shape[-1]) for w in weights]
    slab = lambda w: pl.BlockSpec((w.shape[0] // n_steps, w.shape[1]), lambda i, *_: (i, 0))
    grid_spec = pltpu.PrefetchScalarGridSpec(
        num_scalar_prefetch=3,
        grid=(n_steps,),
        in_specs=[pl.BlockSpec((tm * TOP_K,), lambda i, *_: (i,), memory_space=pltpu.SMEM),
                  pl.BlockSpec((tm, D_MODEL), lambda i, *_: (i, 0))] + [slab(w) for w in flat],
        out_specs=[pl.BlockSpec(memory_space=pl.ANY)] + [slab(w) for w in flat],
        scratch_shapes=[pltpu.VMEM((tm, *ROW_TILE), F32), pltpu.VMEM((MOE_ROWS, *ROW_TILE), F32),
                        pltpu.SemaphoreType.DMA(()), pltpu.SemaphoreType.DMA(())],
    )
    xs, *cast = pl.pallas_call(
        _dispatch_kernel,
        out_shape=[jax.ShapeDtypeStruct((n_slots, *ROW_TILE), F32)]
        + [jax.ShapeDtypeStruct(w.shape, BF16) for w in flat],
        grid_spec=grid_spec,
        compiler_params=_cparams("arbitrary"),
    )(pad_start, pad_len, n_used, dest_flat, x, *flat)
    return xs, [c.reshape(w.shape) for c, w in zip(cast, weights)]


def _expert_kernel(blk_e_ref, nused_ref, xs_ref, wg_ref, wu_ref, wd_ref, ys_ref, acc_ref, *, n_f):
    del blk_e_ref
    i = pl.program_id(0)
    f = pl.program_id(1)
    last = n_f - 1
    used = i < nused_ref[0]

    def piece():
        xb = _tiles_to_rows(xs_ref).astype(BF16)
        g = jnp.dot(xb, wg_ref[0], preferred_element_type=F32)
        u = jnp.dot(xb, wu_ref[0], preferred_element_type=F32)
        return _bdot(g * _sigmoid(g) * u, wd_ref[0])

    if n_f == 1:
        @pl.when(used)
        def _():
            _rows_to_tiles(piece(), ys_ref)
    else:
        @pl.when(used & (f == 0))
        def _():
            acc_ref[...] = piece()

        if n_f > 2:
            @pl.when(used & (f != 0) & (f != last))
            def _():
                acc_ref[...] += piece()

        @pl.when(used & (f == last))
        def _():
            _rows_to_tiles(acc_ref[...] + piece(), ys_ref)

    @pl.when(jnp.logical_not(used) & (f == last))
    def _():
        ys_ref[...] = jnp.zeros_like(ys_ref)


def _experts(xs, blk_e, n_used, wg, wu, wd):
    n_slots = xs.shape[0]
    n_blocks = n_slots // MOE_ROWS
    ff = wg.shape[2]
    n_f = 2
    tf = ff // n_f
    piece = lambda i, f: jnp.where((i & 1) == 1, n_f - 1 - f, f)
    grid_spec = pltpu.PrefetchScalarGridSpec(
        num_scalar_prefetch=2,
        grid=(n_blocks, n_f),
        in_specs=[pl.BlockSpec((MOE_ROWS, *ROW_TILE), lambda i, f, be, nu: (i, 0, 0)),
                  pl.BlockSpec((1, D_MODEL, tf), lambda i, f, be, nu: (be[i], 0, piece(i, f))),
                  pl.BlockSpec((1, D_MODEL, tf), lambda i, f, be, nu: (be[i], 0, piece(i, f))),
                  pl.BlockSpec((1, tf, D_MODEL), lambda i, f, be, nu: (be[i], piece(i, f), 0))],
        out_specs=pl.BlockSpec((MOE_ROWS, *ROW_TILE), lambda i, f, be, nu: (i, 0, 0)),
        scratch_shapes=[pltpu.VMEM((MOE_ROWS, D_MODEL), F32)],
    )
    return pl.pallas_call(
        functools.partial(_expert_kernel, n_f=n_f),
        out_shape=jax.ShapeDtypeStruct((n_slots, *ROW_TILE), F32),
        grid_spec=grid_spec,
        compiler_params=_cparams("arbitrary", "arbitrary"),
    )(blk_e, n_used, xs, wg, wu, wd)


def _combine_kernel(dest_ref, dest_next_ref, x_ref, wts_ref, ys_ref, lg_ref, lb_ref, xo_ref, buf_ref, sem):
    tm = x_ref.shape[0]
    i = pl.program_id(0)
    n = pl.num_programs(0)

    def gather(idx_ref, slot):
        def start(r, c):
            for j in range(TOP_K):
                pltpu.make_async_copy(ys_ref.at[idx_ref[TOP_K * r + j]],
                                      buf_ref.at[slot, j, r], sem.at[slot]).start(priority=j)
            return c
        lax.fori_loop(0, tm, start, 0, unroll=8)

    slot = i & 1

    @pl.when(i == 0)
    def _():
        gather(dest_ref, 0)

    @pl.when(i + 1 < n)
    def _():
        gather(dest_next_ref, 1 - slot)

    for j in range(TOP_K):
        pltpu.make_async_copy(ys_ref.at[pl.ds(0, tm)], buf_ref.at[slot, j], sem.at[slot]).wait()
    w = wts_ref[...]
    f = w[:, 0:1] * _tiles_to_rows(buf_ref.at[slot, 0]) + w[:, 1:2] * _tiles_to_rows(buf_ref.at[slot, 1])
    z = DEEPNORM_ALPHA * x_ref[...] + f
    xo_ref[...] = _layer_norm_rows(z, lg_ref[...], lb_ref[...], LN_EPS)


def _combine(x, dest_flat, wts, ys, ln_g, ln_b):
    t = x.shape[0]
    tm = min(512, t)
    n_tiles = t // tm
    rows = lambda width: pl.BlockSpec((tm, width), lambda i: (i, 0))
    vec = pl.BlockSpec((1, D_MODEL), lambda i: (0, 0))
    return pl.pallas_call(
        _combine_kernel,
        out_shape=jax.ShapeDtypeStruct((t, D_MODEL), F32),
        grid=(n_tiles,),
        in_specs=[pl.BlockSpec((tm * TOP_K,), lambda i: (i,), memory_space=pltpu.SMEM),
                  pl.BlockSpec((tm * TOP_K,), lambda i: (jnp.minimum(i + 1, n_tiles - 1),),
                               memory_space=pltpu.SMEM),
                  rows(D_MODEL), rows(LANES), pl.BlockSpec(memory_space=pl.ANY), vec, vec],
        out_specs=rows(D_MODEL),
        scratch_shapes=[pltpu.VMEM((2, TOP_K, tm, *ROW_TILE), F32), pltpu.SemaphoreType.DMA((2,))],
        compiler_params=_cparams("arbitrary"),
    )(dest_flat, dest_flat, x, wts, ys, ln_g, ln_b)


def _moe_ffn(x, w_router, wg, wu, wd, ln_g, ln_b):
    t = x.shape[0]
    wr_pad = jnp.pad(w_router, ((0, 0), (0, LANES - N_EXPERTS)))
    meta, wts, cnt = _router(x, wr_pad)
    counts = cnt[0, :N_EXPERTS].astype(I32)
    padded = (counts + MOE_ROWS - 1) // MOE_ROWS * MOE_ROWS
    seg_end = jnp.cumsum(padded)
    seg_start = seg_end - padded
    n_blocks = (t * TOP_K) // MOE_ROWS + N_EXPERTS
    dest = seg_start[meta[:, 0:TOP_K]] + meta[:, TOP_K:2 * TOP_K]
    dest_flat = dest.reshape(-1).astype(I32)
    blk_start = jnp.arange(n_blocks, dtype=I32) * MOE_ROWS
    blk_e = jnp.minimum(jnp.sum(blk_start[:, None] >= seg_end[None, :], axis=1), N_EXPERTS - 1).astype(I32)
    n_used = (seg_end[-1:] // MOE_ROWS).astype(I32)
    xs, (wg, wu, wd) = _dispatch(x, dest_flat, (seg_start + counts).astype(I32), (padded - counts).astype(I32),
                                 n_used, n_blocks * MOE_ROWS, (wg, wu, wd))
    ys = _experts(xs, blk_e, n_used, wg, wu, wd)
    return _combine(x, dest_flat, wts, ys, ln_g, ln_b)


def _pad_rows(w, rows, at=0):
    out = jnp.zeros((rows, w.shape[1]), w.dtype)
    return out.at[at:at + w.shape[0]].set(w)


def _reorder_in_projection(w, b):
    sizes = (GLA_KW, GLA_KW, GLA_VW, GLA_VW, GLA_DECAY_RANK, LRU_WIDTH, LRU_WIDTH,
             3 * RWKV_WIDTH, RWKV_DECAY_RANK, RWKV_A_RANK, RWKV_GATE_RANK, N_BRANCH * D_MODEL)
    offs = [0]
    for s in sizes:
        offs.append(offs[-1] + s)
    wb = jnp.concatenate([w, b[None, :]], axis=0)
    piece = lambda i: wb[:, offs[i]:offs[i + 1]]
    zeros = lambda n: jnp.zeros((wb.shape[0], n), wb.dtype)
    q, k, v, r, dec, lx, lg, rkv, wl, al, gl, gates = (piece(i) for i in range(len(sizes)))
    small = jnp.concatenate([wl, al, gl, zeros(RW_SMALL_W - SM_GL[0] - RWKV_GATE_RANK)], axis=1)
    groups = {"gla": jnp.concatenate([q, k, v, r, dec, zeros(GLA_DEC_W - GLA_DECAY_RANK)], axis=1),
              "lru": jnp.concatenate([lx, lg], axis=1),
              "rwkv": jnp.concatenate([rkv, small], axis=1),
              "gates": gates}
    return {name: (g[:-1].astype(BF16), g[-1:]) for name, g in groups.items()}


def _block_diag(blocks):
    n, bi, bo = blocks.shape
    eye = jnp.eye(n, dtype=blocks.dtype)
    return (eye[:, None, :, None] * blocks[:, :, None, :]).reshape(n * bi, n * bo)


def kernel(x, w_in, b_in, gla_w_decay_up, gla_b_decay, gla_norm_g, gla_norm_b, lru_conv_w, lru_conv_b, lru_w_r, lru_b_r, lru_w_i, lru_b_i, lru_lambda, rwkv_mu, rwkv_w0, rwkv_w2, rwkv_a0, rwkv_a2, rwkv_g2, rwkv_k_k, rwkv_k_a, rwkv_r_k, rwkv_lnx_g, rwkv_lnx_b, p_gla, p_lru, p_rwkv, w_out, ln_mix_g, ln_mix_b, ffn_w_gate, ffn_w_up, ffn_w_down, moe_w_router, moe_w_gate, moe_w_up, moe_w_down, ln_ffn_g, ln_ffn_b):
    batch, seq, d = x.shape
    t = batch * seq
    xf = x.reshape(t, d)
    row = lambda a: a.reshape(1, -1)
    head_ones = _block_diag(jnp.ones((RWKV_HEADS, RWKV_HEAD, RWKV_HEAD), BF16))
    for l in range(DEPTH):
        proj = _reorder_in_projection(w_in[l], b_in[l])
        w_gates, b_gates = proj["gates"]

        wup = _pad_rows(gla_w_decay_up[l], GLA_DEC_W).astype(BF16)
        o_gla = _gla_branch(xf, batch, seq, *proj["gla"], wup, row(gla_b_decay[l]), row(gla_norm_g[l]),
                            row(gla_norm_b[l]))

        w_ri = jnp.concatenate([_block_diag(lru_w_r[l]), _block_diag(lru_w_i[l])], axis=1).astype(BF16)
        b_ri = jnp.concatenate([lru_b_r[l], lru_b_i[l]])[None, :]
        o_lru = _lru_branch(xf, batch, seq, *proj["lru"], lru_conv_w[l], row(lru_conv_b[l]), w_ri, b_ri,
                            row(jax.nn.softplus(-lru_lambda[l])))

        mu = rwkv_mu[l]
        mu_all = jnp.zeros((3 * RWKV_WIDTH + RW_SMALL_W,), F32).at[0:mu.shape[0]].set(mu)
        prep_params = (*proj["rwkv"], row(mu_all),
                       _pad_rows(rwkv_w2[l], SM_WA[1], 0).astype(BF16),
                       _pad_rows(rwkv_a2[l], SM_WA[1], RWKV_DECAY_RANK).astype(BF16),
                       _pad_rows(rwkv_g2[l], SM_GL[1], 0).astype(BF16),
                       row(rwkv_w0[l]), row(rwkv_a0[l]), row(rwkv_k_k[l]), row(rwkv_k_a[l]),
                       row(rwkv_r_k[l]), head_ones)
        prep = _rwkv_prep(xf, batch, seq, prep_params)
        o_rwkv = _rwkv_scan(prep, batch, seq, row(rwkv_lnx_g[l]), row(rwkv_lnx_b[l]))

        xf = _merge(o_gla, o_lru, o_rwkv, xf, w_gates, b_gates, p_gla[l].astype(BF16), p_lru[l].astype(BF16),
                    p_rwkv[l].astype(BF16), w_out[l].astype(BF16), row(ln_mix_g[l]), row(ln_mix_b[l]))
        i = l // 2
        if l % 2 == 0:
            xf = _dense_ffn(xf, ffn_w_gate[i].astype(BF16), ffn_w_up[i].astype(BF16),
                            ffn_w_down[i].astype(BF16), row(ln_ffn_g[l]), row(ln_ffn_b[l]))
        else:
            xf = _moe_ffn(xf, moe_w_router[i], moe_w_gate[i], moe_w_up[i], moe_w_down[i],
                          row(ln_ffn_g[l]), row(ln_ffn_b[l]))
    return xf.reshape(batch, seq, d)
```

```python
import functools

import jax
import jax.numpy as jnp
from jax import lax
from jax.experimental import pallas as pl
from jax.experimental.pallas import tpu as pltpu

F32 = jnp.float32
BF16 = jnp.bfloat16
I32 = jnp.int32

D_MODEL = 1024
DEPTH = 2
GLA_HEADS, GLA_DK, GLA_DV = 4, 64, 128
GLA_KW, GLA_VW = GLA_HEADS * GLA_DK, GLA_HEADS * GLA_DV
GLA_DECAY_RANK = 16
GLA_TEMP = 16.0
LRU_WIDTH, LRU_BLOCKS, LRU_CONV, LRU_C = 512, 8, 4, 8.0
RWKV_HEAD, RWKV_WIDTH = 64, 512
RWKV_HEADS = RWKV_WIDTH // RWKV_HEAD
RWKV_DECAY_RANK, RWKV_A_RANK, RWKV_GATE_RANK = 64, 64, 160
RWKV_LNX_EPS = 64e-5
N_BRANCH = 3
N_EXPERTS, TOP_K = 8, 2
DEEPNORM_ALPHA = (2 * DEPTH) ** 0.25
LN_EPS = 1e-5

LANES = 128
SUBLANES = 8
VMEM_LIMIT_BYTES = 56 * 1024 * 1024

SM_WA = (0, 128)
SM_GL = (128, 256)
RW_SMALL_W = 512
GLA_DEC_W = LANES
GLA_COLS = {"q": (0, 256), "k": (256, 512), "v": (512, 1024), "r": (1024, 1536), "dec": (1536, 1664)}

CHUNK = 64
INV_BLOCK = 16
MOE_ROWS = 512


def _cparams(*sem):
    return pltpu.CompilerParams(dimension_semantics=sem, vmem_limit_bytes=VMEM_LIMIT_BYTES)


def _sigmoid(x):
    return 1.0 / (1.0 + jnp.exp(-x))


def _softplus(x):
    return jnp.maximum(x, 0.0) + jnp.log(1.0 + jnp.exp(-jnp.abs(x)))


def _bdot(a, b):
    return jnp.dot(a.astype(BF16), b.astype(BF16), preferred_element_type=F32)


def _bdot_nt(a, b):
    return lax.dot_general(a.astype(BF16), b.astype(BF16), (((1,), (1,)), ((), ())),
                           preferred_element_type=F32)


def _bdot_tn(a, b):
    return lax.dot_general(a.astype(BF16), b.astype(BF16), (((0,), (0,)), ((), ())),
                           preferred_element_type=F32)


def _fdot(a, b):
    ah = a.astype(BF16)
    al = (a - ah.astype(F32)).astype(BF16)
    bh = b.astype(BF16)
    bl = (b - bh.astype(F32)).astype(BF16)
    dot = lambda u, v: jnp.dot(u, v, preferred_element_type=F32)
    return dot(al, bh) + dot(ah, bl) + dot(ah, bh)


def _split3(x):
    hi = x.astype(BF16)
    rest = x - hi.astype(F32)
    mid = rest.astype(BF16)
    lo = (rest - mid.astype(F32)).astype(BF16)
    return hi, mid, lo


def _mask_dot(mask_bf16, x):
    hi, mid, lo = _split3(x)
    dot = lambda part: jnp.dot(mask_bf16, part, preferred_element_type=F32)
    return dot(lo) + dot(mid) + dot(hi)


def _dot_mask(x, mask_bf16):
    hi, mid, lo = _split3(x)
    dot = lambda part: jnp.dot(part, mask_bf16, preferred_element_type=F32)
    return dot(lo) + dot(mid) + dot(hi)


def _layer_norm_rows(z, g, b, eps):
    mu = jnp.mean(z, axis=-1, keepdims=True)
    zc = z - mu
    var = jnp.mean(zc * zc, axis=-1, keepdims=True)
    return zc * lax.rsqrt(var + eps) * g + b


def _tri_masks(n):
    row = lax.broadcasted_iota(I32, (n, n), 0)
    col = lax.broadcasted_iota(I32, (n, n), 1)
    return row, col


MXU_COLS = 256


def _gla_kernel(*refs, chunks, blocks_per_seq):
    *io_refs, st_ref, ha_ref, hb_ref = refs
    s = pl.program_id(0)

    @pl.when(s == 0)
    def _():
        hb_ref[...] = jnp.zeros_like(hb_ref)

    @pl.when((s == 0) | (((s + blocks_per_seq - 1) % blocks_per_seq) == 0))
    def _():
        st_ref[...] = jnp.zeros_like(st_ref)

    body = functools.partial(_gla_body, chunks, (*io_refs, st_ref))

    @pl.when((s & 1) == 0)
    def _():
        body(ha_ref, hb_ref)

    @pl.when((s & 1) == 1)
    def _():
        body(hb_ref, ha_ref)


def _gla_body(chunks, refs, h_next, h_ref):
    x_ref, w_ref, b_ref, wup_ref, bdec_ref, ng_ref, nb_ref, o_ref, st_ref = refs
    xb = x_ref[...].astype(BF16)
    width = w_ref.shape[1]
    pieces = iter([slice(c0, min(c0 + MXU_COLS, width)) for c0 in range(0, width, MXU_COLS)])

    def project_piece():
        cols = next(pieces, None)
        if cols is not None:
            h_next[:, cols] = jnp.dot(xb, w_ref[:, cols], preferred_element_type=F32) + b_ref[:, cols]

    q_ref = h_ref.at[:, GLA_COLS["q"][0]:GLA_COLS["q"][1]]
    k_ref = h_ref.at[:, GLA_COLS["k"][0]:GLA_COLS["k"][1]]
    v_ref = h_ref.at[:, GLA_COLS["v"][0]:GLA_COLS["v"][1]]
    r_ref = h_ref.at[:, GLA_COLS["r"][0]:GLA_COLS["r"][1]]
    dec_ref = h_ref.at[:, GLA_COLS["dec"][0]:GLA_COLS["dec"][1]]

    row, col = _tri_masks(CHUNK)
    ltri = (row >= col).astype(BF16)
    kw, vw = 2 * GLA_DK, 2 * GLA_DV
    key_shift, val_shift = GLA_DK.bit_length() - 1, GLA_DV.bit_length() - 1
    iota = lambda shape, axis: lax.broadcasted_iota(I32, shape, axis)
    causal = iota((CHUNK, kw), 0) >= (iota((CHUNK, kw), 1) & (GLA_DK - 1))
    key_diag = (iota((kw, kw), 0) >> key_shift) == (iota((kw, kw), 1) >> key_shift)
    val_diag = (iota((kw, vw), 0) >> key_shift) == (iota((kw, vw), 1) >> val_shift)
    state_diag = (iota((vw, kw), 0) >> val_shift) == (iota((vw, kw), 1) >> key_shift)
    zero = jnp.zeros((), BF16)
    pairs = range(GLA_HEADS // 2)
    qp, vp, k_diag, kep, decays = [], [], [], [], []
    for c in range(chunks):
        project_piece()
        rows = pl.ds(c * CHUNK, CHUNK)
        logits = _bdot(dec_ref[rows, :], wup_ref[...]) + bdec_ref[...]
        log_a = -_softplus(-logits) * (1.0 / GLA_TEMP)
        cum = _mask_dot(ltri, log_a)
        cum_end = cum[CHUNK - 1:CHUNK, :]
        q = q_ref[rows, :] * (GLA_DK ** -0.5)
        k = k_ref[rows, :]
        q_dec = (q * jnp.exp(cum)).astype(BF16)
        k_inv = (k * jnp.exp(-cum)).astype(BF16)
        k_end = (k * jnp.exp(cum_end - cum)).astype(BF16)
        decay_end = jnp.exp(cum_end)
        v = v_ref[rows, :].astype(BF16)
        for p in pairs:
            s = slice(p * kw, (p + 1) * kw)
            qp.append(q_dec[:, s])
            vp.append(v[:, p * vw:(p + 1) * vw])
            k_diag.append(jnp.where(key_diag, jnp.concatenate([k_inv[:, s], k_inv[:, s]], axis=0), zero))
            kep.append(k_end[:, s])
            decays.append(decay_end[:, s])

    for _ in range(chunks):
        project_piece()

    v_diag = [jnp.where(val_diag, jnp.concatenate([x, x], axis=0), zero) for x in vp]
    scores = [jnp.where(causal, _bdot_nt(q, kd), 0.0) for q, kd in zip(qp, k_diag)]
    o_local = [_bdot(s, vd) for s, vd in zip(scores, v_diag)]
    upd = [jnp.where(state_diag, _bdot_tn(x, ke), 0.0) for x, ke in zip(vp, kep)]

    state = [st_ref[p] for p in pairs]
    n_pairs = len(pairs)
    for c in range(chunks):
        rows = pl.ds(c * CHUNK, CHUNK)
        item = slice(c * n_pairs, (c + 1) * n_pairs)
        o = [ol + _bdot_nt(q, st) for ol, q, st in zip(o_local[item], qp[item], state)]
        state = [st * dcy + u for st, dcy, u in zip(state, decays[item], upd[item])]
        outs = []
        for op in o:
            for oh in (op[:, :GLA_DV], op[:, GLA_DV:]):
                mu = jnp.mean(oh, axis=-1, keepdims=True)
                oc = oh - mu
                var = jnp.mean(oc * oc, axis=-1, keepdims=True)
                outs.append(oc * lax.rsqrt(var + LN_EPS))
        y = jnp.concatenate(outs, axis=1) * ng_ref[...] + nb_ref[...]
        r = r_ref[rows, :]
        o_ref[rows, :] = (y * (r * _sigmoid(r))).astype(o_ref.dtype)
    for p in pairs:
        st_ref[p] = state[p]


def _gla_branch(x, batch, seq, w, bias, wup_pad, b_decay, norm_g, norm_b):
    t = batch * seq
    chunks = 8
    rows = CHUNK * chunks
    nblk = seq // rows

    n_tiles = batch * nblk

    def full(a):
        return pl.BlockSpec(a.shape, lambda s: (0,) * a.ndim)

    args = (w, bias, wup_pad, b_decay, norm_g, norm_b)
    return pl.pallas_call(
        functools.partial(_gla_kernel, chunks=chunks, blocks_per_seq=nblk),
        out_shape=jax.ShapeDtypeStruct((t, GLA_VW), BF16),
        grid=(n_tiles + 1,),
        in_specs=[pl.BlockSpec((rows, D_MODEL), lambda s: (jnp.minimum(s, n_tiles - 1), 0))]
        + [full(a) for a in args],
        out_specs=pl.BlockSpec((rows, GLA_VW), lambda s: (jnp.maximum(s - 1, 0), 0)),
        scratch_shapes=[pltpu.VMEM((GLA_HEADS // 2, 2 * GLA_DV, 2 * GLA_DK), F32),
                        pltpu.VMEM((rows, w.shape[1]), F32), pltpu.VMEM((rows, w.shape[1]), F32)],
        compiler_params=_cparams("arbitrary"),
    )(x, *args)


def _lru_kernel(xin_ref, w_ref, b_ref, cw_ref, cb_ref, wri_ref, bri_ref, sp_ref,
                o_ref, xbuf_ref, a_ref, u_ref, hs_ref, gate_ref, h_ref):
    tm = xin_ref.shape[0]
    pad = SUBLANES

    @pl.when(pl.program_id(1) == 0)
    def _():
        xbuf_ref[0:pad, :] = jnp.zeros((pad, LRU_WIDTH), F32)
        h_ref[...] = jnp.zeros_like(h_ref)

    proj = jnp.dot(xin_ref[...].astype(BF16), w_ref[...], preferred_element_type=F32) + b_ref[...]
    x = proj[:, :LRU_WIDTH]
    gate_ref[...] = jax.nn.gelu(proj[:, LRU_WIDTH:])
    xbuf_ref[pad:pad + tm, :] = x
    xc = cb_ref[...] + x * cw_ref[LRU_CONV - 1:LRU_CONV, :]
    for j in range(LRU_CONV - 1):
        back = LRU_CONV - 1 - j
        xc = xc + xbuf_ref[pl.ds(pad - back, tm), :] * cw_ref[j:j + 1, :]
    xbuf_ref[0:pad, :] = x[tm - pad:tm, :]

    ri = _sigmoid(_bdot(xc, wri_ref[...]) + bri_ref[...])
    rg, ig = ri[:, :LRU_WIDTH], ri[:, LRU_WIDTH:]
    log_a = -LRU_C * rg * sp_ref[...]
    a = jnp.exp(log_a)
    a_ref[...] = a
    u_ref[...] = jnp.sqrt(1.0 - a * a) * (ig * xc)

    def step(t, hprev):
        hnew = a_ref[pl.ds(t, 1), :] * hprev + u_ref[pl.ds(t, 1), :]
        hs_ref[pl.ds(t, 1), :] = hnew
        return hnew

    h_ref[...] = lax.fori_loop(0, tm, step, h_ref[...], unroll=8)
    o_ref[...] = (gate_ref[...] * hs_ref[...]).astype(o_ref.dtype)


def _lru_branch(x, batch, seq, w, bias, conv_w, conv_b, w_ri, b_ri, softplus_neg_lam):
    t = batch * seq
    tm = min(1024, seq)
    nblk = seq // tm

    def full(a):
        return pl.BlockSpec(a.shape, lambda b, n: (0,) * a.ndim)

    args = (w, bias, conv_w, conv_b, w_ri, b_ri, softplus_neg_lam)
    return pl.pallas_call(
        _lru_kernel,
        out_shape=jax.ShapeDtypeStruct((t, LRU_WIDTH), BF16),
        grid=(batch, nblk),
        in_specs=[pl.BlockSpec((tm, D_MODEL), lambda b, n: (b * nblk + n, 0))] + [full(a) for a in args],
        out_specs=pl.BlockSpec((tm, LRU_WIDTH), lambda b, n: (b * nblk + n, 0)),
        scratch_shapes=[pltpu.VMEM((tm + SUBLANES, LRU_WIDTH), F32)]
        + [pltpu.VMEM((tm, LRU_WIDTH), F32)] * 4 + [pltpu.VMEM((1, LRU_WIDTH), F32)],
        compiler_params=_cparams("parallel", "arbitrary"),
    )(x, *args)


def _rwkv_prep_kernel(*refs, blocks_per_seq):
    *io_refs, ha_ref, hb_ref, carry_ref = refs
    tm = io_refs[0].shape[0]
    s = pl.program_id(0)
    body = functools.partial(_rwkv_prep_body, tm, s, RWKV_WIDTH, blocks_per_seq, (*io_refs, carry_ref))

    @pl.when(s == 0)
    def _():
        hb_ref[...] = jnp.zeros_like(hb_ref)
        carry_ref[...] = jnp.zeros_like(carry_ref)

    @pl.when((s & 1) == 0)
    def _():
        body(ha_ref, hb_ref)

    @pl.when((s & 1) == 1)
    def _():
        body(hb_ref, ha_ref)


def _rwkv_prep_body(tm, s, gw, blocks_per_seq, refs, h_next, h_cur):
    (x_ref, w_ref, b_ref, mu_ref, w2_ref, a2_ref, g2_ref, w0_ref, a0_ref, kk_ref, ka_ref, rk_ref, bd_ref,
     r_out, k_out, v_out, lw_out, kk_out, kka_out, bonus_out, g_out, carry_ref) = refs
    first = ((s + blocks_per_seq - 1) % blocks_per_seq) == 0
    keep = jnp.where(first, 0.0, 1.0)
    row0 = lax.broadcasted_iota(I32, (tm, 1), 0) == 0
    xb = x_ref[...].astype(BF16)

    pw = gw // 2

    def project(j):
        cols = slice(j * pw, (j + 1) * pw)
        h_next[:, cols] = jnp.dot(xb, w_ref[:, cols], preferred_element_type=F32) + b_ref[:, cols]

    def shifted(j):
        cols = slice(j * gw, (j + 1) * gw)
        cur = h_cur[:, cols]
        prev_row = carry_ref[0:1, cols] * keep
        carry_ref[0:1, cols] = cur[tm - 1:tm, :]
        prev = jnp.where(row0, prev_row, pltpu.roll(cur, 1, axis=0))
        return cur + (prev - cur) * mu_ref[:, cols]

    project(0)
    sm = shifted(3)
    wa = sm[:, SM_WA[0]:SM_WA[0] + SM_WA[1]]
    gl = sm[:, SM_GL[0]:SM_GL[0] + SM_GL[1]]
    project(1)
    w_log = -_softplus(-(w0_ref[...] + _bdot(jnp.tanh(wa), w2_ref[...]))) - 0.5
    lw_out[...] = -jnp.exp(w_log)
    project(2)
    a = _sigmoid(a0_ref[...] + _bdot(wa, a2_ref[...]))
    g_out[...] = _bdot(_sigmoid(gl), g2_ref[...])
    project(3)
    k = shifted(1)
    kk = k * kk_ref[...]
    norm = jnp.sqrt(_dot_mask(kk * kk, bd_ref[...]))
    project(4)
    kk = kk / jnp.maximum(norm, 1e-12)
    kk_out[...] = kk
    kka_out[...] = kk * a
    k2 = k * (1.0 + (a - 1.0) * ka_ref[...])
    k_out[...] = k2
    project(5)
    r = shifted(0)
    r_out[...] = r
    project(6)
    v = shifted(2)
    v_out[...] = v.astype(v_out.dtype)
    project(7)
    bonus_out[...] = _dot_mask(r * k2 * rk_ref[...], bd_ref[...]) * v


def _rwkv_prep(x, batch, seq, params):
    t = batch * seq
    tm = min(512, seq)
    bps = seq // tm
    n_tiles = t // tm
    width = params[0].shape[1]

    def full(a):
        return pl.BlockSpec(a.shape, lambda s: (0,) * a.ndim)

    out = lambda dtype: jax.ShapeDtypeStruct((t, RWKV_WIDTH), dtype)
    ospec = pl.BlockSpec((tm, RWKV_WIDTH), lambda s: (jnp.maximum(s - 1, 0), 0))
    out_dtypes = (F32, F32, BF16, F32, F32, F32, F32, F32)
    return pl.pallas_call(
        functools.partial(_rwkv_prep_kernel, blocks_per_seq=bps),
        out_shape=[out(dt) for dt in out_dtypes],
        grid=(n_tiles + 1,),
        in_specs=[pl.BlockSpec((tm, D_MODEL), lambda s: (jnp.minimum(s, n_tiles - 1), 0))]
        + [full(a) for a in params],
        out_specs=[ospec] * 8,
        scratch_shapes=[pltpu.VMEM((tm, width), F32), pltpu.VMEM((tm, width), F32),
                        pltpu.VMEM((SUBLANES, width), F32)],
        compiler_params=_cparams("arbitrary"),
    )(x, *params)


PAIR_W = 2 * RWKV_HEAD
N_PAIRS = RWKV_HEADS // 2
HEAD_SHIFT = RWKV_HEAD.bit_length() - 1


def _pair_masks():
    t = lax.broadcasted_iota(I32, (CHUNK, PAIR_W), 0)
    lane = lax.broadcasted_iota(I32, (CHUNK, PAIR_W), 1)
    j = lane & (RWKV_HEAD - 1)
    r = lax.broadcasted_iota(I32, (PAIR_W, PAIR_W), 0)
    c = lax.broadcasted_iota(I32, (PAIR_W, PAIR_W), 1)
    r4 = lax.broadcasted_iota(I32, (2 * PAIR_W, PAIR_W), 0)
    c4 = lax.broadcasted_iota(I32, (2 * PAIR_W, PAIR_W), 1)
    return dict(
        strict=t > j, incl=t >= j, eye=(t == j).astype(F32),
        same_block=jnp.bitwise_xor(t, j) < INV_BLOCK,
        first_head=lane < RWKV_HEAD,
        block_diag=(r >> HEAD_SHIFT) == (c >> HEAD_SHIFT),
        stacked_diag=((r4 >> HEAD_SHIFT) & 1) == (c4 >> HEAD_SHIFT))


N_SLOT_REFS = 5


def _rwkv_scan_kernel(*refs, chunks, blocks_per_seq):
    *io_refs, st_ref = refs[:-2 * N_SLOT_REFS]
    slot_a, slot_b = refs[-2 * N_SLOT_REFS:-N_SLOT_REFS], refs[-N_SLOT_REFS:]
    s = pl.program_id(0)

    @pl.when(s == 0)
    def _():
        for ref in slot_b:
            ref[...] = jnp.zeros_like(ref)

    @pl.when((s == 0) | (((s + blocks_per_seq - 1) % blocks_per_seq) == 0))
    def _():
        st_ref[...] = jnp.zeros_like(st_ref)

    body = functools.partial(_rwkv_scan_body, chunks, (*io_refs, st_ref))

    @pl.when((s & 1) == 0)
    def _():
        body(slot_a, slot_b)

    @pl.when((s & 1) == 1)
    def _():
        body(slot_b, slot_a)


def _rwkv_scan_body(chunks, refs, wr, rd):
    r_ref, k_ref, v_ref, lw_ref, kk_ref, kka_ref, bonus_ref, g_ref, lg_ref, lb_ref, o_ref, st_ref = refs
    yl_w, rm_w, gm_w, qm_w, gam_w = wr
    yl_r, rm_r, gm_r, qm_r, gam_r = rd
    row, col = _tri_masks(CHUNK)
    ltri = (row >= col).astype(BF16)
    m = _pair_masks()
    zero = jnp.zeros((), BF16)
    pairs = range(N_PAIRS)
    sl = [slice(p * PAIR_W, (p + 1) * PAIR_W) for p in pairs]

    def block_diag(x):
        xb = x.astype(BF16)
        return jnp.where(m["block_diag"], jnp.concatenate([xb, xb], axis=0), zero)

    def pair_mul(lhs, *rhs):
        rb = jnp.concatenate([block_diag(x) for x in rhs], axis=1) if len(rhs) > 1 else block_diag(rhs[0])
        out = jnp.dot(lhs.astype(BF16), rb, preferred_element_type=F32)
        return [out[:, i * PAIR_W:(i + 1) * PAIR_W] for i in range(len(rhs))]

    def each(fn, *lists):
        return [fn(*args) for args in zip(*lists)]

    carried = {"state": [st_ref[p] for p in pairs]}
    pending = iter(range(chunks))
    inv_n = 1.0 / RWKV_HEAD

    def head_stat(z):
        s0 = jnp.sum(jnp.where(m["first_head"], z, 0.0), axis=-1, keepdims=True)
        s1 = jnp.sum(jnp.where(m["first_head"], 0.0, z), axis=-1, keepdims=True)
        return jnp.where(m["first_head"], s0, s1) * inv_n

    def previous_tile_chunk():
        c = next(pending, None)
        if c is None:
            return
        rows = pl.ds(c * CHUNK, CHUNK)
        items = range(c * N_PAIRS, (c + 1) * N_PAIRS)
        state = carried["state"]
        y = [yl_r[i] + _bdot_nt(rm_r[i], st) for i, st in zip(items, state)]
        carried["state"] = [st * gam_r[i, 0:1, :] + _bdot_nt(st, gm_r[i]) + qm_r[i] for i, st in zip(items, state)]
        outs = []
        for yp in y:
            yc = yp - head_stat(yp)
            outs.append(yc * lax.rsqrt(head_stat(yc * yc) + RWKV_LNX_EPS))
        yn = jnp.concatenate(outs, axis=1) * lg_ref[...] + lb_ref[...]
        o_ref[rows, :] = ((yn + bonus_ref[rows, :]) * g_ref[rows, :]).astype(o_ref.dtype)

    lhs, rhs, vp, ends, gammas = [], [], [], [], []
    for c in range(chunks):
        rows = pl.ds(c * CHUNK, CHUNK)
        lw = lw_ref[rows, :]
        cum = _mask_dot(ltri, lw)
        cum_end = cum[CHUNK - 1:CHUNK, :]
        e_inv = jnp.exp(-cum)
        e_end = jnp.exp(cum_end - cum)
        kka = kka_ref[rows, :]
        k = k_ref[rows, :]
        a_bar = (-kk_ref[rows, :] * jnp.exp(cum - lw)).astype(BF16)
        r_bar = (r_ref[rows, :] * jnp.exp(cum)).astype(BF16)
        b_til = (kka * e_inv).astype(BF16)
        k_til = (k * e_inv).astype(BF16)
        b_end = (kka * e_end).astype(BF16)
        k_end = (k * e_end).astype(BF16)
        gamma = jnp.exp(cum_end)
        v = v_ref[rows, :].astype(BF16)

        for s in sl:
            lhs.append(jnp.concatenate([a_bar[:, s], r_bar[:, s]], axis=0))
            rhs.append(jnp.where(m["stacked_diag"], jnp.concatenate(
                [b_til[:, s], b_til[:, s], k_til[:, s], k_til[:, s]], axis=0), zero))
            vp.append(v[:, s])
            ends.append(jnp.concatenate([b_end[:, s], k_end[:, s]], axis=0))
            gammas.append(gamma[:, s])

    blocks = each(_bdot_nt, lhs, rhs)
    previous_tile_chunk()
    a_ab = [jnp.where(m["strict"], b[:CHUNK, :PAIR_W], 0.0) for b in blocks]
    a_ak = [jnp.where(m["strict"], b[:CHUNK, PAIR_W:], 0.0) for b in blocks]
    r_ab = [jnp.where(m["incl"], b[CHUNK:, :PAIR_W], 0.0) for b in blocks]
    r_ak = [jnp.where(m["incl"], b[CHUNK:, PAIR_W:], 0.0) for b in blocks]
    from_v = [jnp.dot(jnp.concatenate([ak, rk], axis=0).astype(BF16), block_diag(x),
                      preferred_element_type=F32) for ak, rk, x in zip(a_ak, r_ak, vp)]
    previous_tile_chunk()
    d = [jnp.where(m["same_block"], a, 0.0) for a in a_ab]
    low = [a - di for a, di in zip(a_ab, d)]
    pm = [m["eye"] + di for di in d]
    dpow = [pair_mul(di, di)[0] for di in d]
    previous_tile_chunk()
    for _ in range(INV_BLOCK.bit_length() - 3):
        both = each(lambda x, p: pair_mul(x, x, p), dpow, pm)
        dpow = [b[0] for b in both]
        pm = [p + b[1] for p, b in zip(pm, both)]
        previous_tile_chunk()
    pm = [p + pair_mul(x, p)[0] for p, x in zip(pm, dpow)]
    previous_tile_chunk()
    both = each(lambda p, lo, fv, l: pair_mul(p, lo, fv[:CHUNK], l[:CHUNK]), pm, low, from_v, lhs)
    previous_tile_chunk()
    both2 = each(lambda b: pair_mul(b[0], b[0], b[1], b[2]), both)
    previous_tile_chunk()
    xw_v = [b[1] + b2[1] for b, b2 in zip(both, both2)]
    xw_a = [b[2] + b2[2] for b, b2 in zip(both, both2)]
    both3 = each(lambda b2, xv, xa: pair_mul(b2[0], xv, xa), both2, xw_v, xw_a)
    u_v = [xv + b3[0] for xv, b3 in zip(xw_v, both3)]
    wm = [xa + b3[1] for xa, b3 in zip(xw_a, both3)]
    both4 = each(lambda rb, uv, w: pair_mul(rb, uv, w), r_ab, u_v, wm)
    y_local = [fv[CHUNK:] + b4[0] for fv, b4 in zip(from_v, both4)]
    rm = [l[CHUNK:] + b4[1] for l, b4 in zip(lhs, both4)]
    gm = [jnp.where(m["block_diag"], _bdot_tn(en[:CHUNK], w), 0.0) for en, w in zip(ends, wm)]
    qm = [jnp.where(m["block_diag"], _bdot_tn(jnp.concatenate([uv.astype(BF16), x], axis=0), en), 0.0)
          for uv, x, en in zip(u_v, vp, ends)]
    for _ in range(chunks):
        previous_tile_chunk()
    for p in pairs:
        st_ref[p] = carried["state"][p]

    for i in range(chunks * N_PAIRS):
        yl_w[i] = y_local[i]
        rm_w[i] = rm[i]
        gm_w[i] = gm[i]
        qm_w[i] = qm[i]
        gam_w[i, 0:1, :] = gammas[i]


def _rwkv_scan(prep, batch, seq, lnx_g, lnx_b):
    t = batch * seq
    chunks = 8
    rows = CHUNK * chunks
    nblk = seq // rows
    n_tiles = batch * nblk
    n_items = chunks * N_PAIRS
    cur = pl.BlockSpec((rows, RWKV_WIDTH), lambda s: (jnp.minimum(s, n_tiles - 1), 0))
    prev = pl.BlockSpec((rows, RWKV_WIDTH), lambda s: (jnp.maximum(s - 1, 0), 0))
    pspec = pl.BlockSpec((1, RWKV_WIDTH), lambda s: (0, 0))
    slot = [pltpu.VMEM((n_items, CHUNK, PAIR_W), F32), pltpu.VMEM((n_items, CHUNK, PAIR_W), F32),
            pltpu.VMEM((n_items, PAIR_W, PAIR_W), F32), pltpu.VMEM((n_items, PAIR_W, PAIR_W), F32),
            pltpu.VMEM((n_items, SUBLANES, PAIR_W), F32)]
    return pl.pallas_call(
        functools.partial(_rwkv_scan_kernel, chunks=chunks, blocks_per_seq=nblk),
        out_shape=jax.ShapeDtypeStruct((t, RWKV_WIDTH), BF16),
        grid=(n_tiles + 1,),
        in_specs=[cur] * 6 + [prev, prev, pspec, pspec],
        out_specs=prev,
        scratch_shapes=[pltpu.VMEM((N_PAIRS, PAIR_W, PAIR_W), F32)] + slot + slot,
        compiler_params=_cparams("arbitrary"),
    )(*prep, lnx_g, lnx_b)


def _merge_kernel(og_ref, ol_ref, or_ref, x_ref, wgt_ref, bgt_ref, pg_ref, pl_ref, pr_ref, wo_ref,
                  lg_ref, lb_ref, xo_ref):
    d = D_MODEL
    xin = x_ref[...].astype(BF16)
    merged = None
    for b, (o_ref, p_ref) in enumerate(((og_ref, pg_ref), (ol_ref, pl_ref), (or_ref, pr_ref))):
        cols = slice(b * d, (b + 1) * d)
        gate = _sigmoid(jnp.dot(xin, wgt_ref[:, cols], preferred_element_type=F32) + bgt_ref[:, cols])
        term = gate * jnp.dot(o_ref[...], p_ref[...], preferred_element_type=F32)
        merged = term if merged is None else merged + term
    z = DEEPNORM_ALPHA * x_ref[...] + _bdot(merged, wo_ref[...])
    out = _layer_norm_rows(z, lg_ref[...], lb_ref[...], LN_EPS)
    xo_ref[...] = out


def _merge(o_gla, o_lru, o_rwkv, x, w_gates, b_gates, p_gla, p_lru, p_rwkv, w_out, ln_g, ln_b):
    t = x.shape[0]
    tm = min(1024, t)

    def rows(width):
        return pl.BlockSpec((tm, width), lambda i: (i, 0))

    def full(a):
        return pl.BlockSpec(a.shape, lambda i: (0,) * a.ndim, pipeline_mode=pl.Buffered(1))

    params = (w_gates, b_gates, p_gla, p_lru, p_rwkv, w_out, ln_g, ln_b)
    return pl.pallas_call(
        _merge_kernel,
        out_shape=jax.ShapeDtypeStruct((t, D_MODEL), F32),
        grid=(t // tm,),
        in_specs=[rows(GLA_VW), rows(LRU_WIDTH), rows(RWKV_WIDTH), rows(D_MODEL)] + [full(a) for a in params],
        out_specs=rows(D_MODEL),
        compiler_params=_cparams("parallel"),
    )(o_gla, o_lru, o_rwkv, x, *params)


FFN_SPLIT = 2


def _ffn_kernel(x_ref, wg_ref, wu_ref, wd_ref, lg_ref, lb_ref, xo_ref):
    x = x_ref[...]
    xb = x.astype(BF16)
    tf = wg_ref.shape[1] // FFN_SPLIT
    acc = None
    for f in range(FFN_SPLIT):
        cols = slice(f * tf, (f + 1) * tf)
        g = jnp.dot(xb, wg_ref[:, cols], preferred_element_type=F32)
        u = jnp.dot(xb, wu_ref[:, cols], preferred_element_type=F32)
        part = _bdot(g * _sigmoid(g) * u, wd_ref[cols, :])
        acc = part if acc is None else acc + part
    z = DEEPNORM_ALPHA * x + acc
    xo_ref[...] = _layer_norm_rows(z, lg_ref[...], lb_ref[...], LN_EPS)


def _dense_ffn(x, wg, wu, wd, ln_g, ln_b):
    t = x.shape[0]
    tm = min(1024, t)
    rows = pl.BlockSpec((tm, D_MODEL), lambda i: (i, 0))
    resident = lambda a: pl.BlockSpec(a.shape, lambda i: (0,) * a.ndim, pipeline_mode=pl.Buffered(1))
    return pl.pallas_call(
        _ffn_kernel,
        out_shape=jax.ShapeDtypeStruct((t, D_MODEL), F32),
        grid=(t // tm,),
        in_specs=[rows] + [resident(a) for a in (wg, wu, wd, ln_g, ln_b)],
        out_specs=rows,
        compiler_params=_cparams("parallel"),
    )(x, wg, wu, wd, ln_g, ln_b)


ROW_TILE = (D_MODEL // LANES, LANES)
assert ROW_TILE[0] == SUBLANES


def _rows_to_tiles(x, ref3):
    ref3[...] = x.reshape(x.shape[0], *ROW_TILE)


def _tiles_to_rows(ref3):
    return ref3[...].reshape(ref3.shape[0], D_MODEL)


def _router_kernel(x_ref, wr_ref, meta_ref, wts_ref, cnt_ref, carry_ref):
    tm = x_ref.shape[0]

    @pl.when(pl.program_id(0) == 0)
    def _():
        carry_ref[...] = jnp.zeros_like(carry_ref)

    lane = lax.broadcasted_iota(I32, (tm, LANES), 1)
    neg = jnp.float32(-jnp.inf)
    logits = jnp.where(lane < N_EXPERTS, _fdot(x_ref[...], wr_ref[...]), neg)
    m1 = jnp.max(logits, axis=-1, keepdims=True)
    e1 = jnp.min(jnp.where(logits == m1, lane, LANES), axis=-1, keepdims=True)
    rest = jnp.where(lane == e1, neg, logits)
    m2 = jnp.max(rest, axis=-1, keepdims=True)
    e2 = jnp.min(jnp.where(rest == m2, lane, LANES), axis=-1, keepdims=True)
    ex = jnp.exp(m2 - m1)
    w1 = 1.0 / (1.0 + ex)
    w2 = ex / (1.0 + ex)

    hot1 = lane == e1
    hot2 = lane == e2
    onehot = jnp.where(hot1 | hot2, 1.0, 0.0)
    row, col = _tri_masks(tm)
    before = (row > col).astype(BF16)
    prefix = jnp.dot(before, onehot.astype(BF16), preferred_element_type=F32) + carry_ref[...]
    rank1 = jnp.sum(jnp.where(hot1, prefix, 0.0), axis=-1, keepdims=True).astype(I32)
    rank2 = jnp.sum(jnp.where(hot2, prefix, 0.0), axis=-1, keepdims=True).astype(I32)
    carry_ref[...] += jnp.sum(onehot, axis=0, keepdims=True)
    cnt_ref[...] = carry_ref[...]

    meta = jnp.where(lane == 0, e1, jnp.where(lane == 1, e2, jnp.where(lane == 2, rank1, rank2)))
    meta_ref[...] = jnp.where(lane < 4, meta, 0)
    wts_ref[...] = jnp.where(lane == 0, w1, jnp.where(lane == 1, w2, 0.0))


def _router(x, wr_pad):
    t = x.shape[0]
    tm = min(512, t)
    return pl.pallas_call(
        _router_kernel,
        out_shape=[jax.ShapeDtypeStruct((t, LANES), I32), jax.ShapeDtypeStruct((t, LANES), F32),
                   jax.ShapeDtypeStruct((1, LANES), F32)],
        grid=(t // tm,),
        in_specs=[pl.BlockSpec((tm, D_MODEL), lambda i: (i, 0)),
                  pl.BlockSpec((D_MODEL, LANES), lambda i: (0, 0))],
        out_specs=[pl.BlockSpec((tm, LANES), lambda i: (i, 0)), pl.BlockSpec((tm, LANES), lambda i: (i, 0)),
                   pl.BlockSpec((1, LANES), lambda i: (0, 0))],
        scratch_shapes=[pltpu.VMEM((1, LANES), F32)],
        compiler_params=_cparams("arbitrary"),
    )(x, wr_pad)


def _dispatch_kernel(pad_start_ref, pad_len_ref, nused_ref, dest_ref, x_ref, *rest):
    n_w = (len(rest) - 5) // 2
    w_refs, xs_ref, wb_refs = rest[:n_w], rest[n_w], rest[n_w + 1:2 * n_w + 1]
    x3_ref, zeros_ref, sem, fill_sem = rest[-4:]
    tm = x_ref.shape[0]
    n_blocks = xs_ref.shape[0] // MOE_ROWS
    _rows_to_tiles(x_ref[...], x3_ref)

    @pl.when(pl.program_id(0) == pl.num_programs(0) - 1)
    def _():
        zeros_ref[...] = jnp.zeros_like(zeros_ref)
        pad_copy = lambda row: pltpu.make_async_copy(zeros_ref.at[0], xs_ref.at[row], fill_sem)
        blk_copy = lambda blk: pltpu.make_async_copy(
            zeros_ref, xs_ref.at[pl.ds(pl.multiple_of(blk * MOE_ROWS, MOE_ROWS), MOE_ROWS)], fill_sem)

        def for_each(fn):
            for e in range(N_EXPERTS):
                base = pad_start_ref[e]
                lax.fori_loop(0, pad_len_ref[e], lambda r, c: (fn(pad_copy(base + r)), c)[1], 0)
            lax.fori_loop(nused_ref[0], n_blocks, lambda b, c: (fn(blk_copy(b)), c)[1], 0)

        for_each(lambda cp: cp.start())
        for_each(lambda cp: cp.wait())

    def row_copy(r, j):
        return pltpu.make_async_copy(x3_ref.at[r], xs_ref.at[dest_ref[TOP_K * r + j]], sem)

    def start(r, c):
        for j in range(TOP_K):
            row_copy(r, j).start(priority=j)
        return c

    lax.fori_loop(0, tm, start, 0, unroll=8)
    for w_ref, wb_ref in zip(w_refs, wb_refs):
        wb_ref[...] = w_ref[...].astype(wb_ref.dtype)
    for j in range(TOP_K):
        pltpu.make_async_copy(x3_ref, xs_ref.at[pl.ds(0, tm)], sem).wait()


def _dispatch(x, dest_flat, pad_start, pad_len, n_used, n_slots, weights):
    t = x.shape[0]
    tm = min(512, t)
    n_steps = t // tm
    flat = [w.reshape(-1, w.shape[-1]) for w in weights]
    slab = lambda w: pl.BlockSpec((w.shape[0] // n_steps, w.shape[1]), lambda i, *_: (i, 0))
    grid_spec = pltpu.PrefetchScalarGridSpec(
        num_scalar_prefetch=3,
        grid=(n_steps,),
        in_specs=[pl.BlockSpec((tm * TOP_K,), lambda i, *_: (i,), memory_space=pltpu.SMEM),
                  pl.BlockSpec((tm, D_MODEL), lambda i, *_: (i, 0))] + [slab(w) for w in flat],
        out_specs=[pl.BlockSpec(memory_space=pl.ANY)] + [slab(w) for w in flat],
        scratch_shapes=[pltpu.VMEM((tm, *ROW_TILE), F32), pltpu.VMEM((MOE_ROWS, *ROW_TILE), F32),
                        pltpu.SemaphoreType.DMA(()), pltpu.SemaphoreType.DMA(())],
    )
    xs, *cast = pl.pallas_call(
        _dispatch_kernel,
        out_shape=[jax.ShapeDtypeStruct((n_slots, *ROW_TILE), F32)]
        + [jax.ShapeDtypeStruct(w.shape, BF16) for w in flat],
        grid_spec=grid_spec,
        compiler_params=_cparams("arbitrary"),
    )(pad_start, pad_len, n_used, dest_flat, x, *flat)
    return xs, [c.reshape(w.shape) for c, w in zip(cast, weights)]


def _expert_kernel(blk_e_ref, nused_ref, xs_ref, wg_ref, wu_ref, wd_ref, ys_ref, acc_ref, *, n_f):
    del blk_e_ref
    i = pl.program_id(0)
    f = pl.program_id(1)
    last = n_f - 1
    used = i < nused_ref[0]

    def piece():
        xb = _tiles_to_rows(xs_ref).astype(BF16)
        g = jnp.dot(xb, wg_ref[0], preferred_element_type=F32)
        u = jnp.dot(xb, wu_ref[0], preferred_element_type=F32)
        return _bdot(g * _sigmoid(g) * u, wd_ref[0])

    if n_f == 1:
        @pl.when(used)
        def _():
            _rows_to_tiles(piece(), ys_ref)
    else:
        @pl.when(used & (f == 0))
        def _():
            acc_ref[...] = piece()

        if n_f > 2:
            @pl.when(used & (f != 0) & (f != last))
            def _():
                acc_ref[...] += piece()

        @pl.when(used & (f == last))
        def _():
            _rows_to_tiles(acc_ref[...] + piece(), ys_ref)

    @pl.when(jnp.logical_not(used) & (f == last))
    def _():
        ys_ref[...] = jnp.zeros_like(ys_ref)


def _experts(xs, blk_e, n_used, wg, wu, wd):
    n_slots = xs.shape[0]
    n_blocks = n_slots // MOE_ROWS
    ff = wg.shape[2]
    n_f = 2
    tf = ff // n_f
    piece = lambda i, f: jnp.where((i & 1) == 1, n_f - 1 - f, f)
    grid_spec = pltpu.PrefetchScalarGridSpec(
        num_scalar_prefetch=2,
        grid=(n_blocks, n_f),
        in_specs=[pl.BlockSpec((MOE_ROWS, *ROW_TILE), lambda i, f, be, nu: (i, 0, 0)),
                  pl.BlockSpec((1, D_MODEL, tf), lambda i, f, be, nu: (be[i], 0, piece(i, f))),
                  pl.BlockSpec((1, D_MODEL, tf), lambda i, f, be, nu: (be[i], 0, piece(i, f))),
                  pl.BlockSpec((1, tf, D_MODEL), lambda i, f, be, nu: (be[i], piece(i, f), 0))],
        out_specs=pl.BlockSpec((MOE_ROWS, *ROW_TILE), lambda i, f, be, nu: (i, 0, 0)),
        scratch_shapes=[pltpu.VMEM((MOE_ROWS, D_MODEL), F32)],
    )
    return pl.pallas_call(
        functools.partial(_expert_kernel, n_f=n_f),
        out_shape=jax.ShapeDtypeStruct((n_slots, *ROW_TILE), F32),
        grid_spec=grid_spec,
        compiler_params=_cparams("arbitrary", "arbitrary"),
    )(blk_e, n_used, xs, wg, wu, wd)


def _combine_kernel(dest_ref, dest_next_ref, x_ref, wts_ref, ys_ref, lg_ref, lb_ref, xo_ref, buf_ref, sem):
    tm = x_ref.shape[0]
    i = pl.program_id(0)
    n = pl.num_programs(0)

    def gather(idx_ref, slot):
        def start(r, c):
            for j in range(TOP_K):
                pltpu.make_async_copy(ys_ref.at[idx_ref[TOP_K * r + j]],
                                      buf_ref.at[slot, j, r], sem.at[slot]).start(priority=j)
            return c
        lax.fori_loop(0, tm, start, 0, unroll=8)

    slot = i & 1

    @pl.when(i == 0)
    def _():
        gather(dest_ref, 0)

    @pl.when(i + 1 < n)
    def _():
        gather(dest_next_ref, 1 - slot)

    for j in range(TOP_K):
        pltpu.make_async_copy(ys_ref.at[pl.ds(0, tm)], buf_ref.at[slot, j], sem.at[slot]).wait()
    w = wts_ref[...]
    f = w[:, 0:1] * _tiles_to_rows(buf_ref.at[slot, 0]) + w[:, 1:2] * _tiles_to_rows(buf_ref.at[slot, 1])
    z = DEEPNORM_ALPHA * x_ref[...] + f
    xo_ref[...] = _layer_norm_rows(z, lg_ref[...], lb_ref[...], LN_EPS)


def _combine(x, dest_flat, wts, ys, ln_g, ln_b):
    t = x.shape[0]
    tm = min(512, t)
    n_tiles = t // tm
    rows = lambda width: pl.BlockSpec((tm, width), lambda i: (i, 0))
    vec = pl.BlockSpec((1, D_MODEL), lambda i: (0, 0))
    return pl.pallas_call(
        _combine_kernel,
        out_shape=jax.ShapeDtypeStruct((t, D_MODEL), F32),
        grid=(n_tiles,),
        in_specs=[pl.BlockSpec((tm * TOP_K,), lambda i: (i,), memory_space=pltpu.SMEM),
                  pl.BlockSpec((tm * TOP_K,), lambda i: (jnp.minimum(i + 1, n_tiles - 1),),
                               memory_space=pltpu.SMEM),
                  rows(D_MODEL), rows(LANES), pl.BlockSpec(memory_space=pl.ANY), vec, vec],
        out_specs=rows(D_MODEL),
        scratch_shapes=[pltpu.VMEM((2, TOP_K, tm, *ROW_TILE), F32), pltpu.SemaphoreType.DMA((2,))],
        compiler_params=_cparams("arbitrary"),
    )(dest_flat, dest_flat, x, wts, ys, ln_g, ln_b)


def _moe_ffn(x, w_router, wg, wu, wd, ln_g, ln_b):
    t = x.shape[0]
    wr_pad = jnp.pad(w_router, ((0, 0), (0, LANES - N_EXPERTS)))
    meta, wts, cnt = _router(x, wr_pad)
    counts = cnt[0, :N_EXPERTS].astype(I32)
    padded = (counts + MOE_ROWS - 1) // MOE_ROWS * MOE_ROWS
    seg_end = jnp.cumsum(padded)
    seg_start = seg_end - padded
    n_blocks = (t * TOP_K) // MOE_ROWS + N_EXPERTS
    dest = seg_start[meta[:, 0:TOP_K]] + meta[:, TOP_K:2 * TOP_K]
    dest_flat = dest.reshape(-1).astype(I32)
    blk_start = jnp.arange(n_blocks, dtype=I32) * MOE_ROWS
    blk_e = jnp.minimum(jnp.sum(blk_start[:, None] >= seg_end[None, :], axis=1), N_EXPERTS - 1).astype(I32)
    n_used = (seg_end[-1:] // MOE_ROWS).astype(I32)
    xs, (wg, wu, wd) = _dispatch(x, dest_flat, (seg_start + counts).astype(I32), (padded - counts).astype(I32),
                                 n_used, n_blocks * MOE_ROWS, (wg, wu, wd))
    ys = _experts(xs, blk_e, n_used, wg, wu, wd)
    return _combine(x, dest_flat, wts, ys, ln_g, ln_b)


def _pad_rows(w, rows, at=0):
    out = jnp.zeros((rows, w.shape[1]), w.dtype)
    return out.at[at:at + w.shape[0]].set(w)


def _reorder_in_projection(w, b):
    sizes = (GLA_KW, GLA_KW, GLA_VW, GLA_VW, GLA_DECAY_RANK, LRU_WIDTH, LRU_WIDTH,
             3 * RWKV_WIDTH, RWKV_DECAY_RANK, RWKV_A_RANK, RWKV_GATE_RANK, N_BRANCH * D_MODEL)
    offs = [0]
    for s in sizes:
        offs.append(offs[-1] + s)
    wb = jnp.concatenate([w, b[None, :]], axis=0)
    piece = lambda i: wb[:, offs[i]:offs[i + 1]]
    zeros = lambda n: jnp.zeros((wb.shape[0], n), wb.dtype)
    q, k, v, r, dec, lx, lg, rkv, wl, al, gl, gates = (piece(i) for i in range(len(sizes)))
    small = jnp.concatenate([wl, al, gl, zeros(RW_SMALL_W - SM_GL[0] - RWKV_GATE_RANK)], axis=1)
    groups = {"gla": jnp.concatenate([q, k, v, r, dec, zeros(GLA_DEC_W - GLA_DECAY_RANK)], axis=1),
              "lru": jnp.concatenate([lx, lg], axis=1),
              "rwkv": jnp.concatenate([rkv, small], axis=1),
              "gates": gates}
    return {name: (g[:-1].astype(BF16), g[-1:]) for name, g in groups.items()}


def _block_diag(blocks):
    n, bi, bo = blocks.shape
    eye = jnp.eye(n, dtype=blocks.dtype)
    return (eye[:, None, :, None] * blocks[:, :, None, :]).reshape(n * bi, n * bo)


def kernel(x, w_in, b_in, gla_w_decay_up, gla_b_decay, gla_norm_g, gla_norm_b, lru_conv_w, lru_conv_b, lru_w_r, lru_b_r, lru_w_i, lru_b_i, lru_lambda, rwkv_mu, rwkv_w0, rwkv_w2, rwkv_a0, rwkv_a2, rwkv_g2, rwkv_k_k, rwkv_k_a, rwkv_r_k, rwkv_lnx_g, rwkv_lnx_b, p_gla, p_lru, p_rwkv, w_out, ln_mix_g, ln_mix_b, ffn_w_gate, ffn_w_up, ffn_w_down, moe_w_router, moe_w_gate, moe_w_up, moe_w_down, ln_ffn_g, ln_ffn_b):
    batch, seq, d = x.shape
    t = batch * seq
    xf = x.reshape(t, d)
    row = lambda a: a.reshape(1, -1)
    head_ones = _block_diag(jnp.ones((RWKV_HEADS, RWKV_HEAD, RWKV_HEAD), BF16))
    for l in range(DEPTH):
        proj = _reorder_in_projection(w_in[l], b_in[l])
        w_gates, b_gates = proj["gates"]

        wup = _pad_rows(gla_w_decay_up[l], GLA_DEC_W).astype(BF16)
        o_gla = _gla_branch(xf, batch, seq, *proj["gla"], wup, row(gla_b_decay[l]), row(gla_norm_g[l]),
                            row(gla_norm_b[l]))

        w_ri = jnp.concatenate([_block_diag(lru_w_r[l]), _block_diag(lru_w_i[l])], axis=1).astype(BF16)
        b_ri = jnp.concatenate([lru_b_r[l], lru_b_i[l]])[None, :]
        o_lru = _lru_branch(xf, batch, seq, *proj["lru"], lru_conv_w[l], row(lru_conv_b[l]), w_ri, b_ri,
                            row(jax.nn.softplus(-lru_lambda[l])))

        mu = rwkv_mu[l]
        mu_all = jnp.zeros((3 * RWKV_WIDTH + RW_SMALL_W,), F32).at[0:mu.shape[0]].set(mu)
        prep_params = (*proj["rwkv"], row(mu_all),
                       _pad_rows(rwkv_w2[l], SM_WA[1], 0).astype(BF16),
                       _pad_rows(rwkv_a2[l], SM_WA[1], RWKV_DECAY_RANK).astype(BF16),
                       _pad_rows(rwkv_g2[l], SM_GL[1], 0).astype(BF16),
                       row(rwkv_w0[l]), row(rwkv_a0[l]), row(rwkv_k_k[l]), row(rwkv_k_a[l]),
                       row(rwkv_r_k[l]), head_ones)
        prep = _rwkv_prep(xf, batch, seq, prep_params)
        o_rwkv = _rwkv_scan(prep, batch, seq, row(rwkv_lnx_g[l]), row(rwkv_lnx_b[l]))

        xf = _merge(o_gla, o_lru, o_rwkv, xf, w_gates, b_gates, p_gla[l].astype(BF16), p_lru[l].astype(BF16),
                    p_rwkv[l].astype(BF16), w_out[l].astype(BF16), row(ln_mix_g[l]), row(ln_mix_b[l]))
        i = l // 2
        if l % 2 == 0:
            xf = _dense_ffn(xf, ffn_w_gate[i].astype(BF16), ffn_w_up[i].astype(BF16),
                            ffn_w_down[i].astype(BF16), row(ln_ffn_g[l]), row(ln_ffn_b[l]))
        else:
            xf = _moe_ffn(xf, moe_w_router[i], moe_w_gate[i], moe_w_up[i], moe_w_down[i],
                          row(ln_ffn_g[l]), row(ln_ffn_b[l]))
    return xf.reshape(batch, seq, d)
```

```python
import functools

import jax
import jax.numpy as jnp
from jax import lax
from jax.experimental import pallas as pl
from jax.experimental.pallas import tpu as pltpu

F32 = jnp.float32
BF16 = jnp.bfloat16
I32 = jnp.int32

D_MODEL = 1024
DEPTH = 2
GLA_HEADS, GLA_DK, GLA_DV = 4, 64, 128
GLA_KW, GLA_VW = GLA_HEADS * GLA_DK, GLA_HEADS * GLA_DV
GLA_DECAY_RANK = 16
GLA_TEMP = 16.0
LRU_WIDTH, LRU_BLOCKS, LRU_CONV, LRU_C = 512, 8, 4, 8.0
RWKV_HEAD, RWKV_WIDTH = 64, 512
RWKV_HEADS = RWKV_WIDTH // RWKV_HEAD
RWKV_DECAY_RANK, RWKV_A_RANK, RWKV_GATE_RANK = 64, 64, 160
RWKV_LNX_EPS = 64e-5
N_BRANCH = 3
N_EXPERTS, TOP_K = 8, 2
DEEPNORM_ALPHA = (2 * DEPTH) ** 0.25
LN_EPS = 1e-5

LANES = 128
SUBLANES = 8
VMEM_LIMIT_BYTES = 56 * 1024 * 1024

SM_WA = (0, 128)
SM_GL = (128, 256)
RW_SMALL_W = 512
GLA_DEC_W = LANES
GLA_COLS = {"q": (0, 256), "k": (256, 512), "v": (512, 1024), "r": (1024, 1536), "dec": (1536, 1664)}

CHUNK = 64
INV_BLOCK = 16
MOE_ROWS = 512


def _cparams(*sem):
    return pltpu.CompilerParams(dimension_semantics=sem, vmem_limit_bytes=VMEM_LIMIT_BYTES)


def _sigmoid(x):
    return 1.0 / (1.0 + jnp.exp(-x))


def _softplus(x):
    return jnp.maximum(x, 0.0) + jnp.log(1.0 + jnp.exp(-jnp.abs(x)))


def _bdot(a, b):
    return jnp.dot(a.astype(BF16), b.astype(BF16), preferred_element_type=F32)


def _bdot_nt(a, b):
    return lax.dot_general(a.astype(BF16), b.astype(BF16), (((1,), (1,)), ((), ())),
                           preferred_element_type=F32)


def _bdot_tn(a, b):
    return lax.dot_general(a.astype(BF16), b.astype(BF16), (((0,), (0,)), ((), ())),
                           preferred_element_type=F32)


def _fdot(a, b):
    ah = a.astype(BF16)
    al = (a - ah.astype(F32)).astype(BF16)
    bh = b.astype(BF16)
    bl = (b - bh.astype(F32)).astype(BF16)
    dot = lambda u, v: jnp.dot(u, v, preferred_element_type=F32)
    return dot(al, bh) + dot(ah, bl) + dot(ah, bh)


def _split3(x):
    hi = x.astype(BF16)
    rest = x - hi.astype(F32)
    mid = rest.astype(BF16)
    lo = (rest - mid.astype(F32)).astype(BF16)
    return hi, mid, lo


def _mask_dot(mask_bf16, x):
    hi, mid, lo = _split3(x)
    dot = lambda part: jnp.dot(mask_bf16, part, preferred_element_type=F32)
    return dot(lo) + dot(mid) + dot(hi)


def _dot_mask(x, mask_bf16):
    hi, mid, lo = _split3(x)
    dot = lambda part: jnp.dot(part, mask_bf16, preferred_element_type=F32)
    return dot(lo) + dot(mid) + dot(hi)


def _layer_norm_rows(z, g, b, eps):
    mu = jnp.mean(z, axis=-1, keepdims=True)
    zc = z - mu
    var = jnp.mean(zc * zc, axis=-1, keepdims=True)
    return zc * lax.rsqrt(var + eps) * g + b


def _tri_masks(n):
    row = lax.broadcasted_iota(I32, (n, n), 0)
    col = lax.broadcasted_iota(I32, (n, n), 1)
    return row, col


MXU_COLS = 256


def _gla_kernel(*refs, chunks, blocks_per_seq):
    *io_refs, st_ref, ha_ref, hb_ref = refs
    s = pl.program_id(0)

    @pl.when(s == 0)
    def _():
        hb_ref[...] = jnp.zeros_like(hb_ref)

    @pl.when((s == 0) | (((s + blocks_per_seq - 1) % blocks_per_seq) == 0))
    def _():
        st_ref[...] = jnp.zeros_like(st_ref)

    body = functools.partial(_gla_body, chunks, (*io_refs, st_ref))

    @pl.when((s & 1) == 0)
    def _():
        body(ha_ref, hb_ref)

    @pl.when((s & 1) == 1)
    def _():
        body(hb_ref, ha_ref)


def _gla_body(chunks, refs, h_next, h_ref):
    x_ref, w_ref, b_ref, wup_ref, bdec_ref, ng_ref, nb_ref, o_ref, st_ref = refs
    xb = x_ref[...].astype(BF16)
    width = w_ref.shape[1]
    pieces = iter([slice(c0, min(c0 + MXU_COLS, width)) for c0 in range(0, width, MXU_COLS)])

    def project_piece():
        cols = next(pieces, None)
        if cols is not None:
            h_next[:, cols] = jnp.dot(xb, w_ref[:, cols], preferred_element_type=F32) + b_ref[:, cols]

    q_ref = h_ref.at[:, GLA_COLS["q"][0]:GLA_COLS["q"][1]]
    k_ref = h_ref.at[:, GLA_COLS["k"][0]:GLA_COLS["k"][1]]
    v_ref = h_ref.at[:, GLA_COLS["v"][0]:GLA_COLS["v"][1]]
    r_ref = h_ref.at[:, GLA_COLS["r"][0]:GLA_COLS["r"][1]]
    dec_ref = h_ref.at[:, GLA_COLS["dec"][0]:GLA_COLS["dec"][1]]

    row, col = _tri_masks(CHUNK)
    ltri = (row >= col).astype(BF16)
    kw, vw = 2 * GLA_DK, 2 * GLA_DV
    key_shift, val_shift = GLA_DK.bit_length() - 1, GLA_DV.bit_length() - 1
    iota = lambda shape, axis: lax.broadcasted_iota(I32, shape, axis)
    causal = iota((CHUNK, kw), 0) >= (iota((CHUNK, kw), 1) & (GLA_DK - 1))
    key_diag = (iota((kw, kw), 0) >> key_shift) == (iota((kw, kw), 1) >> key_shift)
    val_diag = (iota((kw, vw), 0) >> key_shift) == (iota((kw, vw), 1) >> val_shift)
    state_diag = (iota((vw, kw), 0) >> val_shift) == (iota((vw, kw), 1) >> key_shift)
    zero = jnp.zeros((), BF16)
    pairs = range(GLA_HEADS // 2)
    qp, vp, k_diag, kep, decays = [], [], [], [], []
    for c in range(chunks):
        project_piece()
        rows = pl.ds(c * CHUNK, CHUNK)
        logits = _bdot(dec_ref[rows, :], wup_ref[...]) + bdec_ref[...]
        log_a = -_softplus(-logits) * (1.0 / GLA_TEMP)
        cum = _mask_dot(ltri, log_a)
        cum_end = cum[CHUNK - 1:CHUNK, :]
        q = q_ref[rows, :] * (GLA_DK ** -0.5)
        k = k_ref[rows, :]
        q_dec = (q * jnp.exp(cum)).astype(BF16)
        k_inv = (k * jnp.exp(-cum)).astype(BF16)
        k_end = (k * jnp.exp(cum_end - cum)).astype(BF16)
        decay_end = jnp.exp(cum_end)
        v = v_ref[rows, :].astype(BF16)
        for p in pairs:
            s = slice(p * kw, (p + 1) * kw)
            qp.append(q_dec[:, s])
            vp.append(v[:, p * vw:(p + 1) * vw])
            k_diag.append(jnp.where(key_diag, jnp.concatenate([k_inv[:, s], k_inv[:, s]], axis=0), zero))
            kep.append(k_end[:, s])
            decays.append(decay_end[:, s])

    for _ in range(chunks):
        project_piece()

    v_diag = [jnp.where(val_diag, jnp.concatenate([x, x], axis=0), zero) for x in vp]
    scores = [jnp.where(causal, _bdot_nt(q, kd), 0.0) for q, kd in zip(qp, k_diag)]
    o_local = [_bdot(s, vd) for s, vd in zip(scores, v_diag)]
    upd = [jnp.where(state_diag, _bdot_tn(x, ke), 0.0) for x, ke in zip(vp, kep)]

    state = [st_ref[p] for p in pairs]
    n_pairs = len(pairs)
    for c in range(chunks):
        rows = pl.ds(c * CHUNK, CHUNK)
        item = slice(c * n_pairs, (c + 1) * n_pairs)
        o = [ol + _bdot_nt(q, st) for ol, q, st in zip(o_local[item], qp[item], state)]
        state = [st * dcy + u for st, dcy, u in zip(state, decays[item], upd[item])]
        outs = []
        for op in o:
            for oh in (op[:, :GLA_DV], op[:, GLA_DV:]):
                mu = jnp.mean(oh, axis=-1, keepdims=True)
                oc = oh - mu
                var = jnp.mean(oc * oc, axis=-1, keepdims=True)
                outs.append(oc * lax.rsqrt(var + LN_EPS))
        y = jnp.concatenate(outs, axis=1) * ng_ref[...] + nb_ref[...]
        r = r_ref[rows, :]
        o_ref[rows, :] = (y * (r * _sigmoid(r))).astype(o_ref.dtype)
    for p in pairs:
        st_ref[p] = state[p]


def _gla_branch(x, batch, seq, w, bias, wup_pad, b_decay, norm_g, norm_b):
    t = batch * seq
    chunks = 8
    rows = CHUNK * chunks
    nblk = seq // rows

    n_tiles = batch * nblk

    def full(a):
        return pl.BlockSpec(a.shape, lambda s: (0,) * a.ndim)

    args = (w, bias, wup_pad, b_decay, norm_g, norm_b)
    return pl.pallas_call(
        functools.partial(_gla_kernel, chunks=chunks, blocks_per_seq=nblk),
        out_shape=jax.ShapeDtypeStruct((t, GLA_VW), BF16),
        grid=(n_tiles + 1,),
        in_specs=[pl.BlockSpec((rows, D_MODEL), lambda s: (jnp.minimum(s, n_tiles - 1), 0))]
        + [full(a) for a in args],
        out_specs=pl.BlockSpec((rows, GLA_VW), lambda s: (jnp.maximum(s - 1, 0), 0)),
        scratch_shapes=[pltpu.VMEM((GLA_HEADS // 2, 2 * GLA_DV, 2 * GLA_DK), F32),
                        pltpu.VMEM((rows, w.shape[1]), F32), pltpu.VMEM((rows, w.shape[1]), F32)],
        compiler_params=_cparams("arbitrary"),
    )(x, *args)


def _lru_kernel(xin_ref, w_ref, b_ref, cw_ref, cb_ref, wri_ref, bri_ref, sp_ref,
                o_ref, xbuf_ref, a_ref, u_ref, hs_ref, gate_ref, h_ref):
    tm = xin_ref.shape[0]
    pad = SUBLANES

    @pl.when(pl.program_id(1) == 0)
    def _():
        xbuf_ref[0:pad, :] = jnp.zeros((pad, LRU_WIDTH), F32)
        h_ref[...] = jnp.zeros_like(h_ref)

    proj = jnp.dot(xin_ref[...].astype(BF16), w_ref[...], preferred_element_type=F32) + b_ref[...]
    x = proj[:, :LRU_WIDTH]
    gate_ref[...] = jax.nn.gelu(proj[:, LRU_WIDTH:])
    xbuf_ref[pad:pad + tm, :] = x
    xc = cb_ref[...] + x * cw_ref[LRU_CONV - 1:LRU_CONV, :]
    for j in range(LRU_CONV - 1):
        back = LRU_CONV - 1 - j
        xc = xc + xbuf_ref[pl.ds(pad - back, tm), :] * cw_ref[j:j + 1, :]
    xbuf_ref[0:pad, :] = x[tm - pad:tm, :]

    ri = _sigmoid(_bdot(xc, wri_ref[...]) + bri_ref[...])
    rg, ig = ri[:, :LRU_WIDTH], ri[:, LRU_WIDTH:]
    log_a = -LRU_C * rg * sp_ref[...]
    a = jnp.exp(log_a)
    a_ref[...] = a
    u_ref[...] = jnp.sqrt(1.0 - a * a) * (ig * xc)

    def step(t, hprev):
        hnew = a_ref[pl.ds(t, 1), :] * hprev + u_ref[pl.ds(t, 1), :]
        hs_ref[pl.ds(t, 1), :] = hnew
        return hnew

    h_ref[...] = lax.fori_loop(0, tm, step, h_ref[...], unroll=8)
    o_ref[...] = (gate_ref[...] * hs_ref[...]).astype(o_ref.dtype)


def _lru_branch(x, batch, seq, w, bias, conv_w, conv_b, w_ri, b_ri, softplus_neg_lam):
    t = batch * seq
    tm = min(1024, seq)
    nblk = seq // tm

    def full(a):
        return pl.BlockSpec(a.shape, lambda b, n: (0,) * a.ndim)

    args = (w, bias, conv_w, conv_b, w_ri, b_ri, softplus_neg_lam)
    return pl.pallas_call(
        _lru_kernel,
        out_shape=jax.ShapeDtypeStruct((t, LRU_WIDTH), BF16),
        grid=(batch, nblk),
        in_specs=[pl.BlockSpec((tm, D_MODEL), lambda b, n: (b * nblk + n, 0))] + [full(a) for a in args],
        out_specs=pl.BlockSpec((tm, LRU_WIDTH), lambda b, n: (b * nblk + n, 0)),
        scratch_shapes=[pltpu.VMEM((tm + SUBLANES, LRU_WIDTH), F32)]
        + [pltpu.VMEM((tm, LRU_WIDTH), F32)] * 4 + [pltpu.VMEM((1, LRU_WIDTH), F32)],
        compiler_params=_cparams("parallel", "arbitrary"),
    )(x, *args)


def _rwkv_prep_kernel(*refs, blocks_per_seq):
    *io_refs, ha_ref, hb_ref, carry_ref = refs
    tm = io_refs[0].shape[0]
    s = pl.program_id(0)
    body = functools.partial(_rwkv_prep_body, tm, s, RWKV_WIDTH, blocks_per_seq, (*io_refs, carry_ref))

    @pl.when(s == 0)
    def _():
        hb_ref[...] = jnp.zeros_like(hb_ref)
        carry_ref[...] = jnp.zeros_like(carry_ref)

    @pl.when((s & 1) == 0)
    def _():
        body(ha_ref, hb_ref)

    @pl.when((s & 1) == 1)
    def _():
        body(hb_ref, ha_ref)


def _rwkv_prep_body(tm, s, gw, blocks_per_seq, refs, h_next, h_cur):
    (x_ref, w_ref, b_ref, mu_ref, w2_ref, a2_ref, g2_ref, w0_ref, a0_ref, kk_ref, ka_ref, rk_ref, bd_ref,
     r_out, k_out, v_out, lw_out, kk_out, kka_out, bonus_out, g_out, carry_ref) = refs
    first = ((s + blocks_per_seq - 1) % blocks_per_seq) == 0
    keep = jnp.where(first, 0.0, 1.0)
    row0 = lax.broadcasted_iota(I32, (tm, 1), 0) == 0
    xb = x_ref[...].astype(BF16)

    pw = gw // 2

    def project(j):
        cols = slice(j * pw, (j + 1) * pw)
        h_next[:, cols] = jnp.dot(xb, w_ref[:, cols], preferred_element_type=F32) + b_ref[:, cols]

    def shifted(j):
        cols = slice(j * gw, (j + 1) * gw)
        cur = h_cur[:, cols]
        prev_row = carry_ref[0:1, cols] * keep
        carry_ref[0:1, cols] = cur[tm - 1:tm, :]
        prev = jnp.where(row0, prev_row, pltpu.roll(cur, 1, axis=0))
        return cur + (prev - cur) * mu_ref[:, cols]

    project(0)
    sm = shifted(3)
    wa = sm[:, SM_WA[0]:SM_WA[0] + SM_WA[1]]
    gl = sm[:, SM_GL[0]:SM_GL[0] + SM_GL[1]]
    project(1)
    w_log = -_softplus(-(w0_ref[...] + _bdot(jnp.tanh(wa), w2_ref[...]))) - 0.5
    lw_out[...] = -jnp.exp(w_log)
    project(2)
    a = _sigmoid(a0_ref[...] + _bdot(wa, a2_ref[...]))
    g_out[...] = _bdot(_sigmoid(gl), g2_ref[...])
    project(3)
    k = shifted(1)
    kk = k * kk_ref[...]
    norm = jnp.sqrt(_dot_mask(kk * kk, bd_ref[...]))
    project(4)
    kk = kk / jnp.maximum(norm, 1e-12)
    kk_out[...] = kk
    kka_out[...] = kk * a
    k2 = k * (1.0 + (a - 1.0) * ka_ref[...])
    k_out[...] = k2
    project(5)
    r = shifted(0)
    r_out[...] = r
    project(6)
    v = shifted(2)
    v_out[...] = v.astype(v_out.dtype)
    project(7)
    bonus_out[...] = _dot_mask(r * k2 * rk_ref[...], bd_ref[...]) * v


def _rwkv_prep(x, batch, seq, params):
    t = batch * seq
    tm = min(512, seq)
    bps = seq // tm
    n_tiles = t // tm
    width = params[0].shape[1]

    def full(a):
        return pl.BlockSpec(a.shape, lambda s: (0,) * a.ndim)

    out = lambda dtype: jax.ShapeDtypeStruct((t, RWKV_WIDTH), dtype)
    ospec = pl.BlockSpec((tm, RWKV_WIDTH), lambda s: (jnp.maximum(s - 1, 0), 0))
    out_dtypes = (F32, F32, BF16, F32, F32, F32, F32, F32)
    return pl.pallas_call(
        functools.partial(_rwkv_prep_kernel, blocks_per_seq=bps),
        out_shape=[out(dt) for dt in out_dtypes],
        grid=(n_tiles + 1,),
        in_specs=[pl.BlockSpec((tm, D_MODEL), lambda s: (jnp.minimum(s, n_tiles - 1), 0))]
        + [full(a) for a in params],
        out_specs=[ospec] * 8,
        scratch_shapes=[pltpu.VMEM((tm, width), F32), pltpu.VMEM((tm, width), F32),
                        pltpu.VMEM((SUBLANES, width), F32)],
        compiler_params=_cparams("arbitrary"),
    )(x, *params)


PAIR_W = 2 * RWKV_HEAD
N_PAIRS = RWKV_HEADS // 2
HEAD_SHIFT = RWKV_HEAD.bit_length() - 1


def _pair_masks():
    t = lax.broadcasted_iota(I32, (CHUNK, PAIR_W), 0)
    lane = lax.broadcasted_iota(I32, (CHUNK, PAIR_W), 1)
    j = lane & (RWKV_HEAD - 1)
    r = lax.broadcasted_iota(I32, (PAIR_W, PAIR_W), 0)
    c = lax.broadcasted_iota(I32, (PAIR_W, PAIR_W), 1)
    r4 = lax.broadcasted_iota(I32, (2 * PAIR_W, PAIR_W), 0)
    c4 = lax.broadcasted_iota(I32, (2 * PAIR_W, PAIR_W), 1)
    return dict(
        strict=t > j, incl=t >= j, eye=(t == j).astype(F32),
        same_block=jnp.bitwise_xor(t, j) < INV_BLOCK,
        first_head=lane < RWKV_HEAD,
        block_diag=(r >> HEAD_SHIFT) == (c >> HEAD_SHIFT),
        stacked_diag=((r4 >> HEAD_SHIFT) & 1) == (c4 >> HEAD_SHIFT))


N_SLOT_REFS = 5


def _rwkv_scan_kernel(*refs, chunks, blocks_per_seq):
    *io_refs, st_ref = refs[:-2 * N_SLOT_REFS]
    slot_a, slot_b = refs[-2 * N_SLOT_REFS:-N_SLOT_REFS], refs[-N_SLOT_REFS:]
    s = pl.program_id(0)

    @pl.when(s == 0)
    def _():
        for ref in slot_b:
            ref[...] = jnp.zeros_like(ref)

    @pl.when((s == 0) | (((s + blocks_per_seq - 1) % blocks_per_seq) == 0))
    def _():
        st_ref[...] = jnp.zeros_like(st_ref)

    body = functools.partial(_rwkv_scan_body, chunks, (*io_refs, st_ref))

    @pl.when((s & 1) == 0)
    def _():
        body(slot_a, slot_b)

    @pl.when((s & 1) == 1)
    def _():
        body(slot_b, slot_a)


def _rwkv_scan_body(chunks, refs, wr, rd):
    r_ref, k_ref, v_ref, lw_ref, kk_ref, kka_ref, bonus_ref, g_ref, lg_ref, lb_ref, o_ref, st_ref = refs
    yl_w, rm_w, gm_w, qm_w, gam_w = wr
    yl_r, rm_r, gm_r, qm_r, gam_r = rd
    row, col = _tri_masks(CHUNK)
    ltri = (row >= col).astype(BF16)
    m = _pair_masks()
    zero = jnp.zeros((), BF16)
    pairs = range(N_PAIRS)
    sl = [slice(p * PAIR_W, (p + 1) * PAIR_W) for p in pairs]

    def block_diag(x):
        xb = x.astype(BF16)
        return jnp.where(m["block_diag"], jnp.concatenate([xb, xb], axis=0), zero)

    def pair_mul(lhs, *rhs):
        rb = jnp.concatenate([block_diag(x) for x in rhs], axis=1) if len(rhs) > 1 else block_diag(rhs[0])
        out = jnp.dot(lhs.astype(BF16), rb, preferred_element_type=F32)
        return [out[:, i * PAIR_W:(i + 1) * PAIR_W] for i in range(len(rhs))]

    def each(fn, *lists):
        return [fn(*args) for args in zip(*lists)]

    carried = {"state": [st_ref[p] for p in pairs]}
    pending = iter(range(chunks))
    inv_n = 1.0 / RWKV_HEAD

    def head_stat(z):
        s0 = jnp.sum(jnp.where(m["first_head"], z, 0.0), axis=-1, keepdims=True)
        s1 = jnp.sum(jnp.where(m["first_head"], 0.0, z), axis=-1, keepdims=True)
        return jnp.where(m["first_head"], s0, s1) * inv_n

    def previous_tile_chunk():
        c = next(pending, None)
        if c is None:
            return
        rows = pl.ds(c * CHUNK, CHUNK)
        items = range(c * N_PAIRS, (c + 1) * N_PAIRS)
        state = carried["state"]
        y = [yl_r[i] + _bdot_nt(rm_r[i], st) for i, st in zip(items, state)]
        carried["state"] = [st * gam_r[i, 0:1, :] + _bdot_nt(st, gm_r[i]) + qm_r[i] for i, st in zip(items, state)]
        outs = []
        for yp in y:
            yc = yp - head_stat(yp)
            outs.append(yc * lax.rsqrt(head_stat(yc * yc) + RWKV_LNX_EPS))
        yn = jnp.concatenate(outs, axis=1) * lg_ref[...] + lb_ref[...]
        o_ref[rows, :] = ((yn + bonus_ref[rows, :]) * g_ref[rows, :]).astype(o_ref.dtype)

    lhs, rhs, vp, ends, gammas = [], [], [], [], []
    for c in range(chunks):
        rows = pl.ds(c * CHUNK, CHUNK)
        lw = lw_ref[rows, :]
        cum = _mask_dot(ltri, lw)
        cum_end = cum[CHUNK - 1:CHUNK, :]
        e_inv = jnp.exp(-cum)
        e_end = jnp.exp(cum_end - cum)
        kka = kka_ref[rows, :]
        k = k_ref[rows, :]
        a_bar = (-kk_ref[rows, :] * jnp.exp(cum - lw)).astype(BF16)
        r_bar = (r_ref[rows, :] * jnp.exp(cum)).astype(BF16)
        b_til = (kka * e_inv).astype(BF16)
        k_til = (k * e_inv).astype(BF16)
        b_end = (kka * e_end).astype(BF16)
        k_end = (k * e_end).astype(BF16)
        gamma = jnp.exp(cum_end)
        v = v_ref[rows, :].astype(BF16)

        for s in sl:
            lhs.append(jnp.concatenate([a_bar[:, s], r_bar[:, s]], axis=0))
            rhs.append(jnp.where(m["stacked_diag"], jnp.concatenate(
                [b_til[:, s], b_til[:, s], k_til[:, s], k_til[:, s]], axis=0), zero))
            vp.append(v[:, s])
            ends.append(jnp.concatenate([b_end[:, s], k_end[:, s]], axis=0))
            gammas.append(gamma[:, s])

    blocks = each(_bdot_nt, lhs, rhs)
    previous_tile_chunk()
    a_ab = [jnp.where(m["strict"], b[:CHUNK, :PAIR_W], 0.0) for b in blocks]
    a_ak = [jnp.where(m["strict"], b[:CHUNK, PAIR_W:], 0.0) for b in blocks]
    r_ab = [jnp.where(m["incl"], b[CHUNK:, :PAIR_W], 0.0) for b in blocks]
    r_ak = [jnp.where(m["incl"], b[CHUNK:, PAIR_W:], 0.0) for b in blocks]
    from_v = [jnp.dot(jnp.concatenate([ak, rk], axis=0).astype(BF16), block_diag(x),
                      preferred_element_type=F32) for ak, rk, x in zip(a_ak, r_ak, vp)]
    previous_tile_chunk()
    d = [jnp.where(m["same_block"], a, 0.0) for a in a_ab]
    low = [a - di for a, di in zip(a_ab, d)]
    pm = [m["eye"] + di for di in d]
    dpow = [pair_mul(di, di)[0] for di in d]
    previous_tile_chunk()
    for _ in range(INV_BLOCK.bit_length() - 3):
        both = each(lambda x, p: pair_mul(x, x, p), dpow, pm)
        dpow = [b[0] for b in both]
        pm = [p + b[1] for p, b in zip(pm, both)]
        previous_tile_chunk()
    pm = [p + pair_mul(x, p)[0] for p, x in zip(pm, dpow)]
    previous_tile_chunk()
    both = each(lambda p, lo, fv, l: pair_mul(p, lo, fv[:CHUNK], l[:CHUNK]), pm, low, from_v, lhs)
    previous_tile_chunk()
    both2 = each(lambda b: pair_mul(b[0], b[0], b[1], b[2]), both)
    previous_tile_chunk()
    xw_v = [b[1] + b2[1] for b, b2 in zip(both, both2)]
    xw_a = [b[2] + b2[2] for b, b2 in zip(both, both2)]
    both3 = each(lambda b2, xv, xa: pair_mul(b2[0], xv, xa), both2, xw_v, xw_a)
    u_v = [xv + b3[0] for xv, b3 in zip(xw_v, both3)]
    wm = [xa + b3[1] for xa, b3 in zip(xw_a, both3)]
    both4 = each(lambda rb, uv, w: pair_mul(rb, uv, w), r_ab, u_v, wm)
    y_local = [fv[CHUNK:] + b4[0] for fv, b4 in zip(from_v, both4)]
    rm = [l[CHUNK:] + b4[1] for l, b4 in zip(lhs, both4)]
    gm = [jnp.where(m["block_diag"], _bdot_tn(en[:CHUNK], w), 0.0) for en, w in zip(ends, wm)]
    qm = [jnp.where(m["block_diag"], _bdot_tn(jnp.concatenate([uv.astype(BF16), x], axis=0), en), 0.0)
          for uv, x, en in zip(u_v, vp, ends)]
    for _ in range(chunks):
        previous_tile_chunk()
    for p in pairs:
        st_ref[p] = carried["state"][p]

    for i in range(chunks * N_PAIRS):
        yl_w[i] = y_local[i]
        rm_w[i] = rm[i]
        gm_w[i] = gm[i]
        qm_w[i] = qm[i]
        gam_w[i, 0:1, :] = gammas[i]


def _rwkv_scan(prep, batch, seq, lnx_g, lnx_b):
    t = batch * seq
    chunks = 8
    rows = CHUNK * chunks
    nblk = seq // rows
    n_tiles = batch * nblk
    n_items = chunks * N_PAIRS
    cur = pl.BlockSpec((rows, RWKV_WIDTH), lambda s: (jnp.minimum(s, n_tiles - 1), 0))
    prev = pl.BlockSpec((rows, RWKV_WIDTH), lambda s: (jnp.maximum(s - 1, 0), 0))
    pspec = pl.BlockSpec((1, RWKV_WIDTH), lambda s: (0, 0))
    slot = [pltpu.VMEM((n_items, CHUNK, PAIR_W), F32), pltpu.VMEM((n_items, CHUNK, PAIR_W), F32),
            pltpu.VMEM((n_items, PAIR_W, PAIR_W), F32), pltpu.VMEM((n_items, PAIR_W, PAIR_W), F32),
            pltpu.VMEM((n_items, SUBLANES, PAIR_W), F32)]
    return pl.pallas_call(
        functools.partial(_rwkv_scan_kernel, chunks=chunks, blocks_per_seq=nblk),
        out_shape=jax.ShapeDtypeStruct((t, RWKV_WIDTH), BF16),
        grid=(n_tiles + 1,),
        in_specs=[cur] * 6 + [prev, prev, pspec, pspec],
        out_specs=prev,
        scratch_shapes=[pltpu.VMEM((N_PAIRS, PAIR_W, PAIR_W), F32)] + slot + slot,
        compiler_params=_cparams("arbitrary"),
    )(*prep, lnx_g, lnx_b)


def _merge_kernel(og_ref, ol_ref, or_ref, x_ref, wgt_ref, bgt_ref, pg_ref, pl_ref, pr_ref, wo_ref,
                  lg_ref, lb_ref, xo_ref):
    d = D_MODEL
    xin = x_ref[...].astype(BF16)
    merged = None
    for b, (o_ref, p_ref) in enumerate(((og_ref, pg_ref), (ol_ref, pl_ref), (or_ref, pr_ref))):
        cols = slice(b * d, (b + 1) * d)
        gate = _sigmoid(jnp.dot(xin, wgt_ref[:, cols], preferred_element_type=F32) + bgt_ref[:, cols])
        term = gate * jnp.dot(o_ref[...], p_ref[...], preferred_element_type=F32)
        merged = term if merged is None else merged + term
    z = DEEPNORM_ALPHA * x_ref[...] + _bdot(merged, wo_ref[...])
    out = _layer_norm_rows(z, lg_ref[...], lb_ref[...], LN_EPS)
    xo_ref[...] = out


def _merge(o_gla, o_lru, o_rwkv, x, w_gates, b_gates, p_gla, p_lru, p_rwkv, w_out, ln_g, ln_b):
    t = x.shape[0]
    tm = min(1024, t)

    def rows(width):
        return pl.BlockSpec((tm, width), lambda i: (i, 0))

    def full(a):
        return pl.BlockSpec(a.shape, lambda i: (0,) * a.ndim, pipeline_mode=pl.Buffered(1))

    params = (w_gates, b_gates, p_gla, p_lru, p_rwkv, w_out, ln_g, ln_b)
    return pl.pallas_call(
        _merge_kernel,
        out_shape=jax.ShapeDtypeStruct((t, D_MODEL), F32),
        grid=(t // tm,),
        in_specs=[rows(GLA_VW), rows(LRU_WIDTH), rows(RWKV_WIDTH), rows(D_MODEL)] + [full(a) for a in params],
        out_specs=rows(D_MODEL),
        compiler_params=_cparams("parallel"),
    )(o_gla, o_lru, o_rwkv, x, *params)


FFN_SPLIT = 2


def _ffn_kernel(x_ref, wg_ref, wu_ref, wd_ref, lg_ref, lb_ref, xo_ref):
    x = x_ref[...]
    xb = x.astype(BF16)
    tf = wg_ref.shape[1] // FFN_SPLIT
    acc = None
    for f in range(FFN_SPLIT):
        cols = slice(f * tf, (f + 1) * tf)
        g = jnp.dot(xb, wg_ref[:, cols], preferred_element_type=F32)
        u = jnp.dot(xb, wu_ref[:, cols], preferred_element_type=F32)
        part = _bdot(g * _sigmoid(g) * u, wd_ref[cols, :])
        acc = part if acc is None else acc + part
    z = DEEPNORM_ALPHA * x + acc
    xo_ref[...] = _layer_norm_rows(z, lg_ref[...], lb_ref[...], LN_EPS)


def _dense_ffn(x, wg, wu, wd, ln_g, ln_b):
    t = x.shape[0]
    tm = min(1024, t)
    rows = pl.BlockSpec((tm, D_MODEL), lambda i: (i, 0))
    resident = lambda a: pl.BlockSpec(a.shape, lambda i: (0,) * a.ndim, pipeline_mode=pl.Buffered(1))
    return pl.pallas_call(
        _ffn_kernel,
        out_shape=jax.ShapeDtypeStruct((t, D_MODEL), F32),
        grid=(t // tm,),
        in_specs=[rows] + [resident(a) for a in (wg, wu, wd, ln_g, ln_b)],
        out_specs=rows,
        compiler_params=_cparams("parallel"),
    )(x, wg, wu, wd, ln_g, ln_b)


ROW_TILE = (D_MODEL // LANES, LANES)
assert ROW_TILE[0] == SUBLANES


def _rows_to_tiles(x, ref3):
    ref3[...] = x.reshape(x.shape[0], *ROW_TILE)


def _tiles_to_rows(ref3):
    return ref3[...].reshape(ref3.shape[0], D_MODEL)


def _router_kernel(x_ref, wr_ref, meta_ref, wts_ref, cnt_ref, carry_ref):
    tm = x_ref.shape[0]

    @pl.when(pl.program_id(0) == 0)
    def _():
        carry_ref[...] = jnp.zeros_like(carry_ref)

    lane = lax.broadcasted_iota(I32, (tm, LANES), 1)
    neg = jnp.float32(-jnp.inf)
    logits = jnp.where(lane < N_EXPERTS, _fdot(x_ref[...], wr_ref[...]), neg)
    m1 = jnp.max(logits, axis=-1, keepdims=True)
    e1 = jnp.min(jnp.where(logits == m1, lane, LANES), axis=-1, keepdims=True)
    rest = jnp.where(lane == e1, neg, logits)
    m2 = jnp.max(rest, axis=-1, keepdims=True)
    e2 = jnp.min(jnp.where(rest == m2, lane, LANES), axis=-1, keepdims=True)
    ex = jnp.exp(m2 - m1)
    w1 = 1.0 / (1.0 + ex)
    w2 = ex / (1.0 + ex)

    hot1 = lane == e1
    hot2 = lane == e2
    onehot = jnp.where(hot1 | hot2, 1.0, 0.0)
    row, col = _tri_masks(tm)
    before = (row > col).astype(BF16)
    prefix = jnp.dot(before, onehot.astype(BF16), preferred_element_type=F32) + carry_ref[...]
    rank1 = jnp.sum(jnp.where(hot1, prefix, 0.0), axis=-1, keepdims=True).astype(I32)
    rank2 = jnp.sum(jnp.where(hot2, prefix, 0.0), axis=-1, keepdims=True).astype(I32)
    carry_ref[...] += jnp.sum(onehot, axis=0, keepdims=True)
    cnt_ref[...] = carry_ref[...]

    meta = jnp.where(lane == 0, e1, jnp.where(lane == 1, e2, jnp.where(lane == 2, rank1, rank2)))
    meta_ref[...] = jnp.where(lane < 4, meta, 0)
    wts_ref[...] = jnp.where(lane == 0, w1, jnp.where(lane == 1, w2, 0.0))


def _router(x, wr_pad):
    t = x.shape[0]
    tm = min(512, t)
    return pl.pallas_call(
        _router_kernel,
        out_shape=[jax.ShapeDtypeStruct((t, LANES), I32), jax.ShapeDtypeStruct((t, LANES), F32),
                   jax.ShapeDtypeStruct((1, LANES), F32)],
        grid=(t // tm,),
        in_specs=[pl.BlockSpec((tm, D_MODEL), lambda i: (i, 0)),
                  pl.BlockSpec((D_MODEL, LANES), lambda i: (0, 0))],
        out_specs=[pl.BlockSpec((tm, LANES), lambda i: (i, 0)), pl.BlockSpec((tm, LANES), lambda i: (i, 0)),
                   pl.BlockSpec((1, LANES), lambda i: (0, 0))],
        scratch_shapes=[pltpu.VMEM((1, LANES), F32)],
        compiler_params=_cparams("arbitrary"),
    )(x, wr_pad)


def _dispatch_kernel(pad_start_ref, pad_len_ref, nused_ref, dest_ref, x_ref, *rest):
    n_w = (len(rest) - 5) // 2
    w_refs, xs_ref, wb_refs = rest[:n_w], rest[n_w], rest[n_w + 1:2 * n_w + 1]
    x3_ref, zeros_ref, sem, fill_sem = rest[-4:]
    tm = x_ref.shape[0]
    n_blocks = xs_ref.shape[0] // MOE_ROWS
    _rows_to_tiles(x_ref[...], x3_ref)

    @pl.when(pl.program_id(0) == pl.num_programs(0) - 1)
    def _():
        zeros_ref[...] = jnp.zeros_like(zeros_ref)
        pad_copy = lambda row: pltpu.make_async_copy(zeros_ref.at[0], xs_ref.at[row], fill_sem)
        blk_copy = lambda blk: pltpu.make_async_copy(
            zeros_ref, xs_ref.at[pl.ds(pl.multiple_of(blk * MOE_ROWS, MOE_ROWS), MOE_ROWS)], fill_sem)

        def for_each(fn):
            for e in range(N_EXPERTS):
                base = pad_start_ref[e]
                lax.fori_loop(0, pad_len_ref[e], lambda r, c: (fn(pad_copy(base + r)), c)[1], 0)
            lax.fori_loop(nused_ref[0], n_blocks, lambda b, c: (fn(blk_copy(b)), c)[1], 0)

        for_each(lambda cp: cp.start())
        for_each(lambda cp: cp.wait())

    def row_copy(r, j):
        return pltpu.make_async_copy(x3_ref.at[r], xs_ref.at[dest_ref[TOP_K * r + j]], sem)

    def start(r, c):
        for j in range(TOP_K):
            row_copy(r, j).start(priority=j)
        return c

    lax.fori_loop(0, tm, start, 0, unroll=8)
    for w_ref, wb_ref in zip(w_refs, wb_refs):
        wb_ref[...] = w_ref[...].astype(wb_ref.dtype)
    for j in range(TOP_K):
        pltpu.make_async_copy(x3_ref, xs_ref.at[pl.ds(0, tm)], sem).wait()


def _dispatch(x, dest_flat, pad_start, pad_len, n_used, n_slots, weights):
    t = x.shape[0]
    tm = min(1024, t)
    n_steps = t // tm
    flat = [w.reshape(-1, w.shape[-1]) for w in weights]
    slab = lambda w: pl.BlockSpec((w.shape[0] // n_steps, w.shape[1]), lambda i, *_: (i, 0))
    grid_spec = pltpu.PrefetchScalarGridSpec(
        num_scalar_prefetch=3,
        grid=(n_steps,),
        in_specs=[pl.BlockSpec((tm * TOP_K,), lambda i, *_: (i,), memory_space=pltpu.SMEM),
                  pl.BlockSpec((tm, D_MODEL), lambda i, *_: (i, 0))] + [slab(w) for w in flat],
        out_specs=[pl.BlockSpec(memory_space=pl.ANY)] + [slab(w) for w in flat],
        scratch_shapes=[pltpu.VMEM((tm, *ROW_TILE), F32), pltpu.VMEM((MOE_ROWS, *ROW_TILE), F32),
                        pltpu.SemaphoreType.DMA(()), pltpu.SemaphoreType.DMA(())],
    )
    xs, *cast = pl.pallas_call(
        _dispatch_kernel,
        out_shape=[jax.ShapeDtypeStruct((n_slots, *ROW_TILE), F32)]
        + [jax.ShapeDtypeStruct(w.shape, BF16) for w in flat],
        grid_spec=grid_spec,
        compiler_params=_cparams("arbitrary"),
    )(pad_start, pad_len, n_used, dest_flat, x, *flat)
    return xs, [c.reshape(w.shape) for c, w in zip(cast, weights)]


def _expert_kernel(blk_e_ref, nused_ref, xs_ref, wg_ref, wu_ref, wd_ref, ys_ref, acc_ref, *, n_f):
    del blk_e_ref
    i = pl.program_id(0)
    f = pl.program_id(1)
    last = n_f - 1
    used = i < nused_ref[0]

    def piece():
        xb = _tiles_to_rows(xs_ref).astype(BF16)
        g = jnp.dot(xb, wg_ref[0], preferred_element_type=F32)
        u = jnp.dot(xb, wu_ref[0], preferred_element_type=F32)
        return _bdot(g * _sigmoid(g) * u, wd_ref[0])

    if n_f == 1:
        @pl.when(used)
        def _():
            _rows_to_tiles(piece(), ys_ref)
    else:
        @pl.when(used & (f == 0))
        def _():
            acc_ref[...] = piece()

        if n_f > 2:
            @pl.when(used & (f != 0) & (f != last))
            def _():
                acc_ref[...] += piece()

        @pl.when(used & (f == last))
        def _():
            _rows_to_tiles(acc_ref[...] + piece(), ys_ref)

    @pl.when(jnp.logical_not(used) & (f == last))
    def _():
        ys_ref[...] = jnp.zeros_like(ys_ref)


def _experts(xs, blk_e, n_used, wg, wu, wd):
    n_slots = xs.shape[0]
    n_blocks = n_slots // MOE_ROWS
    ff = wg.shape[2]
    n_f = 2
    tf = ff // n_f
    piece = lambda i, f: jnp.where((i & 1) == 1, n_f - 1 - f, f)
    grid_spec = pltpu.PrefetchScalarGridSpec(
        num_scalar_prefetch=2,
        grid=(n_blocks, n_f),
        in_specs=[pl.BlockSpec((MOE_ROWS, *ROW_TILE), lambda i, f, be, nu: (i, 0, 0)),
                  pl.BlockSpec((1, D_MODEL, tf), lambda i, f, be, nu: (be[i], 0, piece(i, f))),
                  pl.BlockSpec((1, D_MODEL, tf), lambda i, f, be, nu: (be[i], 0, piece(i, f))),
                  pl.BlockSpec((1, tf, D_MODEL), lambda i, f, be, nu: (be[i], piece(i, f), 0))],
        out_specs=pl.BlockSpec((MOE_ROWS, *ROW_TILE), lambda i, f, be, nu: (i, 0, 0)),
        scratch_shapes=[pltpu.VMEM((MOE_ROWS, D_MODEL), F32)],
    )
    return pl.pallas_call(
        functools.partial(_expert_kernel, n_f=n_f),
        out_shape=jax.ShapeDtypeStruct((n_slots, *ROW_TILE), F32),
        grid_spec=grid_spec,
        compiler_params=_cparams("arbitrary", "arbitrary"),
    )(blk_e, n_used, xs, wg, wu, wd)


def _combine_kernel(dest_ref, dest_next_ref, x_ref, wts_ref, ys_ref, lg_ref, lb_ref, xo_ref, buf_ref, sem):
    tm = x_ref.shape[0]
    i = pl.program_id(0)
    n = pl.num_programs(0)

    def gather(idx_ref, slot):
        def start(r, c):
            for j in range(TOP_K):
                pltpu.make_async_copy(ys_ref.at[idx_ref[TOP_K * r + j]],
                                      buf_ref.at[slot, j, r], sem.at[slot]).start(priority=j)
            return c
        lax.fori_loop(0, tm, start, 0, unroll=8)

    slot = i & 1

    @pl.when(i == 0)
    def _():
        gather(dest_ref, 0)

    @pl.when(i + 1 < n)
    def _():
        gather(dest_next_ref, 1 - slot)

    for j in range(TOP_K):
        pltpu.make_async_copy(ys_ref.at[pl.ds(0, tm)], buf_ref.at[slot, j], sem.at[slot]).wait()
    w = wts_ref[...]
    f = w[:, 0:1] * _tiles_to_rows(buf_ref.at[slot, 0]) + w[:, 1:2] * _tiles_to_rows(buf_ref.at[slot, 1])
    z = DEEPNORM_ALPHA * x_ref[...] + f
    xo_ref[...] = _layer_norm_rows(z, lg_ref[...], lb_ref[...], LN_EPS)


def _combine(x, dest_flat, wts, ys, ln_g, ln_b):
    t = x.shape[0]
    tm = min(1024, t)
    n_tiles = t // tm
    rows = lambda width: pl.BlockSpec((tm, width), lambda i: (i, 0))
    vec = pl.BlockSpec((1, D_MODEL), lambda i: (0, 0))
    return pl.pallas_call(
        _combine_kernel,
        out_shape=jax.ShapeDtypeStruct((t, D_MODEL), F32),
        grid=(n_tiles,),
        in_specs=[pl.BlockSpec((tm * TOP_K,), lambda i: (i,), memory_space=pltpu.SMEM),
                  pl.BlockSpec((tm * TOP_K,), lambda i: (jnp.minimum(i + 1, n_tiles - 1),),
                               memory_space=pltpu.SMEM),
                  rows(D_MODEL), rows(LANES), pl.BlockSpec(memory_space=pl.ANY), vec, vec],
        out_specs=rows(D_MODEL),
        scratch_shapes=[pltpu.VMEM((2, TOP_K, tm, *ROW_TILE), F32), pltpu.SemaphoreType.DMA((2,))],
        compiler_params=_cparams("arbitrary"),
    )(dest_flat, dest_flat, x, wts, ys, ln_g, ln_b)


def _moe_ffn(x, w_router, wg, wu, wd, ln_g, ln_b):
    t = x.shape[0]
    wr_pad = jnp.pad(w_router, ((0, 0), (0, LANES - N_EXPERTS)))
    meta, wts, cnt = _router(x, wr_pad)
    counts = cnt[0, :N_EXPERTS].astype(I32)
    padded = (counts + MOE_ROWS - 1) // MOE_ROWS * MOE_ROWS
    seg_end = jnp.cumsum(padded)
    seg_start = seg_end - padded
    n_blocks = (t * TOP_K) // MOE_ROWS + N_EXPERTS
    dest = seg_start[meta[:, 0:TOP_K]] + meta[:, TOP_K:2 * TOP_K]
    dest_flat = dest.reshape(-1).astype(I32)
    blk_start = jnp.arange(n_blocks, dtype=I32) * MOE_ROWS
    blk_e = jnp.minimum(jnp.sum(blk_start[:, None] >= seg_end[None, :], axis=1), N_EXPERTS - 1).astype(I32)
    n_used = (seg_end[-1:] // MOE_ROWS).astype(I32)
    xs, (wg, wu, wd) = _dispatch(x, dest_flat, (seg_start + counts).astype(I32), (padded - counts).astype(I32),
                                 n_used, n_blocks * MOE_ROWS, (wg, wu, wd))
    ys = _experts(xs, blk_e, n_used, wg, wu, wd)
    return _combine(x, dest_flat, wts, ys, ln_g, ln_b)


def _pad_rows(w, rows, at=0):
    out = jnp.zeros((rows, w.shape[1]), w.dtype)
    return out.at[at:at + w.shape[0]].set(w)


def _reorder_in_projection(w, b):
    sizes = (GLA_KW, GLA_KW, GLA_VW, GLA_VW, GLA_DECAY_RANK, LRU_WIDTH, LRU_WIDTH,
             3 * RWKV_WIDTH, RWKV_DECAY_RANK, RWKV_A_RANK, RWKV_GATE_RANK, N_BRANCH * D_MODEL)
    offs = [0]
    for s in sizes:
        offs.append(offs[-1] + s)
    wb = jnp.concatenate([w, b[None, :]], axis=0)
    piece = lambda i: wb[:, offs[i]:offs[i + 1]]
    zeros = lambda n: jnp.zeros((wb.shape[0], n), wb.dtype)
    q, k, v, r, dec, lx, lg, rkv, wl, al, gl, gates = (piece(i) for i in range(len(sizes)))
    small = jnp.concatenate([wl, al, gl, zeros(RW_SMALL_W - SM_GL[0] - RWKV_GATE_RANK)], axis=1)
    groups = {"gla": jnp.concatenate([q, k, v, r, dec, zeros(GLA_DEC_W - GLA_DECAY_RANK)], axis=1),
              "lru": jnp.concatenate([lx, lg], axis=1),
              "rwkv": jnp.concatenate([rkv, small], axis=1),
              "gates": gates}
    return {name: (g[:-1].astype(BF16), g[-1:]) for name, g in groups.items()}


def _block_diag(blocks):
    n, bi, bo = blocks.shape
    eye = jnp.eye(n, dtype=blocks.dtype)
    return (eye[:, None, :, None] * blocks[:, :, None, :]).reshape(n * bi, n * bo)


def kernel(x, w_in, b_in, gla_w_decay_up, gla_b_decay, gla_norm_g, gla_norm_b, lru_conv_w, lru_conv_b, lru_w_r, lru_b_r, lru_w_i, lru_b_i, lru_lambda, rwkv_mu, rwkv_w0, rwkv_w2, rwkv_a0, rwkv_a2, rwkv_g2, rwkv_k_k, rwkv_k_a, rwkv_r_k, rwkv_lnx_g, rwkv_lnx_b, p_gla, p_lru, p_rwkv, w_out, ln_mix_g, ln_mix_b, ffn_w_gate, ffn_w_up, ffn_w_down, moe_w_router, moe_w_gate, moe_w_up, moe_w_down, ln_ffn_g, ln_ffn_b):
    batch, seq, d = x.shape
    t = batch * seq
    xf = x.reshape(t, d)
    row = lambda a: a.reshape(1, -1)
    head_ones = _block_diag(jnp.ones((RWKV_HEADS, RWKV_HEAD, RWKV_HEAD), BF16))
    for l in range(DEPTH):
        proj = _reorder_in_projection(w_in[l], b_in[l])
        w_gates, b_gates = proj["gates"]

        wup = _pad_rows(gla_w_decay_up[l], GLA_DEC_W).astype(BF16)
        o_gla = _gla_branch(xf, batch, seq, *proj["gla"], wup, row(gla_b_decay[l]), row(gla_norm_g[l]),
                            row(gla_norm_b[l]))

        w_ri = jnp.concatenate([_block_diag(lru_w_r[l]), _block_diag(lru_w_i[l])], axis=1).astype(BF16)
        b_ri = jnp.concatenate([lru_b_r[l], lru_b_i[l]])[None, :]
        o_lru = _lru_branch(xf, batch, seq, *proj["lru"], lru_conv_w[l], row(lru_conv_b[l]), w_ri, b_ri,
                            row(jax.nn.softplus(-lru_lambda[l])))

        mu = rwkv_mu[l]
        mu_all = jnp.zeros((3 * RWKV_WIDTH + RW_SMALL_W,), F32).at[0:mu.shape[0]].set(mu)
        prep_params = (*proj["rwkv"], row(mu_all),
                       _pad_rows(rwkv_w2[l], SM_WA[1], 0).astype(BF16),
                       _pad_rows(rwkv_a2[l], SM_WA[1], RWKV_DECAY_RANK).astype(BF16),
                       _pad_rows(rwkv_g2[l], SM_GL[1], 0).astype(BF16),
                       row(rwkv_w0[l]), row(rwkv_a0[l]), row(rwkv_k_k[l]), row(rwkv_k_a[l]),
                       row(rwkv_r_k[l]), head_ones)
        prep = _rwkv_prep(xf, batch, seq, prep_params)
        o_rwkv = _rwkv_scan(prep, batch, seq, row(rwkv_lnx_g[l]), row(rwkv_lnx_b[l]))

        xf = _merge(o_gla, o_lru, o_rwkv, xf, w_gates, b_gates, p_gla[l].astype(BF16), p_lru[l].astype(BF16),
                    p_rwkv[l].astype(BF16), w_out[l].astype(BF16), row(ln_mix_g[l]), row(ln_mix_b[l]))
        i = l // 2
        if l % 2 == 0:
            xf = _dense_ffn(xf, ffn_w_gate[i].astype(BF16), ffn_w_up[i].astype(BF16),
                            ffn_w_down[i].astype(BF16), row(ln_ffn_g[l]), row(ln_ffn_b[l]))
        else:
            xf = _moe_ffn(xf, moe_w_router[i], moe_w_gate[i], moe_w_up[i], moe_w_down[i],
                          row(ln_ffn_g[l]), row(ln_ffn_b[l]))
    return xf.reshape(batch, seq, d)
```

```python
import functools

import jax
import jax.numpy as jnp
from jax import lax
from jax.experimental import pallas as pl
from jax.experimental.pallas import tpu as pltpu

F32 = jnp.float32
BF16 = jnp.bfloat16
I32 = jnp.int32

D_MODEL = 1024
DEPTH = 2
GLA_HEADS, GLA_DK, GLA_DV = 4, 64, 128
GLA_KW, GLA_VW = GLA_HEADS * GLA_DK, GLA_HEADS * GLA_DV
GLA_DECAY_RANK = 16
GLA_TEMP = 16.0
LRU_WIDTH, LRU_BLOCKS, LRU_CONV, LRU_C = 512, 8, 4, 8.0
RWKV_HEAD, RWKV_WIDTH = 64, 512
RWKV_HEADS = RWKV_WIDTH // RWKV_HEAD
RWKV_DECAY_RANK, RWKV_A_RANK, RWKV_GATE_RANK = 64, 64, 160
RWKV_LNX_EPS = 64e-5
N_BRANCH = 3
N_EXPERTS, TOP_K = 8, 2
DEEPNORM_ALPHA = (2 * DEPTH) ** 0.25
LN_EPS = 1e-5

LANES = 128
SUBLANES = 8
VMEM_LIMIT_BYTES = 56 * 1024 * 1024

SM_WA = (0, 128)
SM_GL = (128, 256)
RW_SMALL_W = 512
GLA_DEC_W = LANES
GLA_COLS = {"q": (0, 256), "k": (256, 512), "v": (512, 1024), "r": (1024, 1536), "dec": (1536, 1664)}

CHUNK = 64
CHUNKS_PER_STEP = 8
INV_BLOCK = 16
MOE_ROWS = 512
TILE_ROWS = 512
WIDE_TILE_ROWS = 1024
RWKV_DECAY_OFFSET = 0.5
KK_NORM_FLOOR = 1e-12


def _cparams(*sem):
    return pltpu.CompilerParams(dimension_semantics=sem, vmem_limit_bytes=VMEM_LIMIT_BYTES)


def _sigmoid(x):
    return 1.0 / (1.0 + jnp.exp(-x))


def _softplus(x):
    return jnp.maximum(x, 0.0) + jnp.log(1.0 + jnp.exp(-jnp.abs(x)))


def _bdot(a, b):
    return jnp.dot(a.astype(BF16), b.astype(BF16), preferred_element_type=F32)


def _bdot_nt(a, b):
    return lax.dot_general(a.astype(BF16), b.astype(BF16), (((1,), (1,)), ((), ())),
                           preferred_element_type=F32)


def _bdot_tn(a, b):
    return lax.dot_general(a.astype(BF16), b.astype(BF16), (((0,), (0,)), ((), ())),
                           preferred_element_type=F32)


def _fdot(a, b):
    ah = a.astype(BF16)
    al = (a - ah.astype(F32)).astype(BF16)
    bh = b.astype(BF16)
    bl = (b - bh.astype(F32)).astype(BF16)
    dot = lambda u, v: jnp.dot(u, v, preferred_element_type=F32)
    return dot(al, bh) + dot(ah, bl) + dot(ah, bh)


def _split3(x):
    hi = x.astype(BF16)
    rest = x - hi.astype(F32)
    mid = rest.astype(BF16)
    lo = (rest - mid.astype(F32)).astype(BF16)
    return hi, mid, lo


def _mask_dot(mask_bf16, x):
    hi, mid, lo = _split3(x)
    dot = lambda part: jnp.dot(mask_bf16, part, preferred_element_type=F32)
    return dot(lo) + dot(mid) + dot(hi)


def _dot_mask(x, mask_bf16):
    hi, mid, lo = _split3(x)
    dot = lambda part: jnp.dot(part, mask_bf16, preferred_element_type=F32)
    return dot(lo) + dot(mid) + dot(hi)


def _layer_norm_rows(z, g, b, eps):
    mu = jnp.mean(z, axis=-1, keepdims=True)
    zc = z - mu
    var = jnp.mean(zc * zc, axis=-1, keepdims=True)
    return zc * lax.rsqrt(var + eps) * g + b


def _tri_masks(n):
    row = lax.broadcasted_iota(I32, (n, n), 0)
    col = lax.broadcasted_iota(I32, (n, n), 1)
    return row, col


MXU_COLS = 256


def _gla_kernel(*refs, chunks, blocks_per_seq):
    *io_refs, st_ref, ha_ref, hb_ref = refs
    s = pl.program_id(0)

    @pl.when(s == 0)
    def _():
        hb_ref[...] = jnp.zeros_like(hb_ref)

    @pl.when((s == 0) | (((s + blocks_per_seq - 1) % blocks_per_seq) == 0))
    def _():
        st_ref[...] = jnp.zeros_like(st_ref)

    body = functools.partial(_gla_body, chunks, (*io_refs, st_ref))

    @pl.when((s & 1) == 0)
    def _():
        body(ha_ref, hb_ref)

    @pl.when((s & 1) == 1)
    def _():
        body(hb_ref, ha_ref)


def _gla_body(chunks, refs, h_next, h_ref):
    x_ref, w_ref, b_ref, wup_ref, bdec_ref, ng_ref, nb_ref, o_ref, st_ref = refs
    xb = x_ref[...].astype(BF16)
    width = w_ref.shape[1]
    pieces = iter([slice(c0, min(c0 + MXU_COLS, width)) for c0 in range(0, width, MXU_COLS)])

    def project_piece():
        cols = next(pieces, None)
        if cols is not None:
            h_next[:, cols] = jnp.dot(xb, w_ref[:, cols], preferred_element_type=F32) + b_ref[:, cols]

    q_ref = h_ref.at[:, GLA_COLS["q"][0]:GLA_COLS["q"][1]]
    k_ref = h_ref.at[:, GLA_COLS["k"][0]:GLA_COLS["k"][1]]
    v_ref = h_ref.at[:, GLA_COLS["v"][0]:GLA_COLS["v"][1]]
    r_ref = h_ref.at[:, GLA_COLS["r"][0]:GLA_COLS["r"][1]]
    dec_ref = h_ref.at[:, GLA_COLS["dec"][0]:GLA_COLS["dec"][1]]

    row, col = _tri_masks(CHUNK)
    ltri = (row >= col).astype(BF16)
    kw, vw = 2 * GLA_DK, 2 * GLA_DV
    key_shift, val_shift = GLA_DK.bit_length() - 1, GLA_DV.bit_length() - 1
    iota = lambda shape, axis: lax.broadcasted_iota(I32, shape, axis)
    causal = iota((CHUNK, kw), 0) >= (iota((CHUNK, kw), 1) & (GLA_DK - 1))
    key_diag = (iota((kw, kw), 0) >> key_shift) == (iota((kw, kw), 1) >> key_shift)
    val_diag = (iota((kw, vw), 0) >> key_shift) == (iota((kw, vw), 1) >> val_shift)
    state_diag = (iota((vw, kw), 0) >> val_shift) == (iota((vw, kw), 1) >> key_shift)
    zero = jnp.zeros((), BF16)
    pairs = range(GLA_HEADS // 2)
    qp, vp, k_diag, kep, decays = [], [], [], [], []
    for c in range(chunks):
        project_piece()
        rows = pl.ds(c * CHUNK, CHUNK)
        logits = _bdot(dec_ref[rows, :], wup_ref[...]) + bdec_ref[...]
        log_a = -_softplus(-logits) * (1.0 / GLA_TEMP)
        cum = _mask_dot(ltri, log_a)
        cum_end = cum[CHUNK - 1:CHUNK, :]
        q = q_ref[rows, :] * (GLA_DK ** -0.5)
        k = k_ref[rows, :]
        q_dec = (q * jnp.exp(cum)).astype(BF16)
        k_inv = (k * jnp.exp(-cum)).astype(BF16)
        k_end = (k * jnp.exp(cum_end - cum)).astype(BF16)
        decay_end = jnp.exp(cum_end)
        v = v_ref[rows, :].astype(BF16)
        for p in pairs:
            s = slice(p * kw, (p + 1) * kw)
            qp.append(q_dec[:, s])
            vp.append(v[:, p * vw:(p + 1) * vw])
            k_diag.append(jnp.where(key_diag, jnp.concatenate([k_inv[:, s], k_inv[:, s]], axis=0), zero))
            kep.append(k_end[:, s])
            decays.append(decay_end[:, s])

    for _ in range(chunks):
        project_piece()

    v_diag = [jnp.where(val_diag, jnp.concatenate([x, x], axis=0), zero) for x in vp]
    scores = [jnp.where(causal, _bdot_nt(q, kd), 0.0) for q, kd in zip(qp, k_diag)]
    o_local = [_bdot(s, vd) for s, vd in zip(scores, v_diag)]
    upd = [jnp.where(state_diag, _bdot_tn(x, ke), 0.0) for x, ke in zip(vp, kep)]

    state = [st_ref[p] for p in pairs]
    n_pairs = len(pairs)
    for c in range(chunks):
        rows = pl.ds(c * CHUNK, CHUNK)
        item = slice(c * n_pairs, (c + 1) * n_pairs)
        o = [ol + _bdot_nt(q, st) for ol, q, st in zip(o_local[item], qp[item], state)]
        state = [st * dcy + u for st, dcy, u in zip(state, decays[item], upd[item])]
        outs = []
        for op in o:
            for oh in (op[:, :GLA_DV], op[:, GLA_DV:]):
                mu = jnp.mean(oh, axis=-1, keepdims=True)
                oc = oh - mu
                var = jnp.mean(oc * oc, axis=-1, keepdims=True)
                outs.append(oc * lax.rsqrt(var + LN_EPS))
        y = jnp.concatenate(outs, axis=1) * ng_ref[...] + nb_ref[...]
        r = r_ref[rows, :]
        o_ref[rows, :] = (y * (r * _sigmoid(r))).astype(o_ref.dtype)
    for p in pairs:
        st_ref[p] = state[p]


def _gla_branch(x, batch, seq, w, bias, wup_pad, b_decay, norm_g, norm_b):
    t = batch * seq
    chunks = CHUNKS_PER_STEP
    rows = CHUNK * chunks
    nblk = seq // rows

    n_tiles = batch * nblk

    def full(a):
        return pl.BlockSpec(a.shape, lambda s: (0,) * a.ndim)

    args = (w, bias, wup_pad, b_decay, norm_g, norm_b)
    return pl.pallas_call(
        functools.partial(_gla_kernel, chunks=chunks, blocks_per_seq=nblk),
        out_shape=jax.ShapeDtypeStruct((t, GLA_VW), BF16),
        grid=(n_tiles + 1,),
        in_specs=[pl.BlockSpec((rows, D_MODEL), lambda s: (jnp.minimum(s, n_tiles - 1), 0))]
        + [full(a) for a in args],
        out_specs=pl.BlockSpec((rows, GLA_VW), lambda s: (jnp.maximum(s - 1, 0), 0)),
        scratch_shapes=[pltpu.VMEM((GLA_HEADS // 2, 2 * GLA_DV, 2 * GLA_DK), F32),
                        pltpu.VMEM((rows, w.shape[1]), F32), pltpu.VMEM((rows, w.shape[1]), F32)],
        compiler_params=_cparams("arbitrary"),
    )(x, *args)


def _lru_kernel(xin_ref, w_ref, b_ref, cw_ref, cb_ref, wri_ref, bri_ref, sp_ref,
                o_ref, xbuf_ref, a_ref, u_ref, hs_ref, gate_ref, h_ref):
    tm = xin_ref.shape[0]
    pad = SUBLANES

    @pl.when(pl.program_id(1) == 0)
    def _():
        xbuf_ref[0:pad, :] = jnp.zeros((pad, LRU_WIDTH), F32)
        h_ref[...] = jnp.zeros_like(h_ref)

    proj = jnp.dot(xin_ref[...].astype(BF16), w_ref[...], preferred_element_type=F32) + b_ref[...]
    x = proj[:, :LRU_WIDTH]
    gate_ref[...] = jax.nn.gelu(proj[:, LRU_WIDTH:])
    xbuf_ref[pad:pad + tm, :] = x
    xc = cb_ref[...] + x * cw_ref[LRU_CONV - 1:LRU_CONV, :]
    for j in range(LRU_CONV - 1):
        back = LRU_CONV - 1 - j
        xc = xc + xbuf_ref[pl.ds(pad - back, tm), :] * cw_ref[j:j + 1, :]
    xbuf_ref[0:pad, :] = x[tm - pad:tm, :]

    ri = _sigmoid(_bdot(xc, wri_ref[...]) + bri_ref[...])
    rg, ig = ri[:, :LRU_WIDTH], ri[:, LRU_WIDTH:]
    log_a = -LRU_C * rg * sp_ref[...]
    a = jnp.exp(log_a)
    a_ref[...] = a
    u_ref[...] = jnp.sqrt(1.0 - a * a) * (ig * xc)

    def step(t, hprev):
        hnew = a_ref[pl.ds(t, 1), :] * hprev + u_ref[pl.ds(t, 1), :]
        hs_ref[pl.ds(t, 1), :] = hnew
        return hnew

    h_ref[...] = lax.fori_loop(0, tm, step, h_ref[...], unroll=8)
    o_ref[...] = (gate_ref[...] * hs_ref[...]).astype(o_ref.dtype)


def _lru_branch(x, batch, seq, w, bias, conv_w, conv_b, w_ri, b_ri, softplus_neg_lam):
    t = batch * seq
    tm = min(WIDE_TILE_ROWS, seq)
    nblk = seq // tm

    def full(a):
        return pl.BlockSpec(a.shape, lambda b, n: (0,) * a.ndim)

    args = (w, bias, conv_w, conv_b, w_ri, b_ri, softplus_neg_lam)
    return pl.pallas_call(
        _lru_kernel,
        out_shape=jax.ShapeDtypeStruct((t, LRU_WIDTH), BF16),
        grid=(batch, nblk),
        in_specs=[pl.BlockSpec((tm, D_MODEL), lambda b, n: (b * nblk + n, 0))] + [full(a) for a in args],
        out_specs=pl.BlockSpec((tm, LRU_WIDTH), lambda b, n: (b * nblk + n, 0)),
        scratch_shapes=[pltpu.VMEM((tm + SUBLANES, LRU_WIDTH), F32)]
        + [pltpu.VMEM((tm, LRU_WIDTH), F32)] * 4 + [pltpu.VMEM((1, LRU_WIDTH), F32)],
        compiler_params=_cparams("parallel", "arbitrary"),
    )(x, *args)


def _rwkv_prep_kernel(*refs, blocks_per_seq):
    *io_refs, ha_ref, hb_ref, carry_ref = refs
    tm = io_refs[0].shape[0]
    s = pl.program_id(0)
    body = functools.partial(_rwkv_prep_body, tm, s, RWKV_WIDTH, blocks_per_seq, (*io_refs, carry_ref))

    @pl.when(s == 0)
    def _():
        hb_ref[...] = jnp.zeros_like(hb_ref)
        carry_ref[...] = jnp.zeros_like(carry_ref)

    @pl.when((s & 1) == 0)
    def _():
        body(ha_ref, hb_ref)

    @pl.when((s & 1) == 1)
    def _():
        body(hb_ref, ha_ref)


def _rwkv_prep_body(tm, s, gw, blocks_per_seq, refs, h_next, h_cur):
    (x_ref, w_ref, b_ref, mu_ref, w2_ref, a2_ref, g2_ref, w0_ref, a0_ref, kk_ref, ka_ref, rk_ref, bd_ref,
     r_out, k_out, v_out, lw_out, kk_out, kka_out, bonus_out, g_out, carry_ref) = refs
    first = ((s + blocks_per_seq - 1) % blocks_per_seq) == 0
    keep = jnp.where(first, 0.0, 1.0)
    row0 = lax.broadcasted_iota(I32, (tm, 1), 0) == 0
    xb = x_ref[...].astype(BF16)

    pw = gw // 2

    def project(j):
        cols = slice(j * pw, (j + 1) * pw)
        h_next[:, cols] = jnp.dot(xb, w_ref[:, cols], preferred_element_type=F32) + b_ref[:, cols]

    def shifted(j):
        cols = slice(j * gw, (j + 1) * gw)
        cur = h_cur[:, cols]
        prev_row = carry_ref[0:1, cols] * keep
        carry_ref[0:1, cols] = cur[tm - 1:tm, :]
        prev = jnp.where(row0, prev_row, pltpu.roll(cur, 1, axis=0))
        return cur + (prev - cur) * mu_ref[:, cols]

    project(0)
    sm = shifted(3)
    wa = sm[:, SM_WA[0]:SM_WA[0] + SM_WA[1]]
    gl = sm[:, SM_GL[0]:SM_GL[0] + SM_GL[1]]
    project(1)
    w_log = -_softplus(-(w0_ref[...] + _bdot(jnp.tanh(wa), w2_ref[...]))) - RWKV_DECAY_OFFSET
    lw_out[...] = -jnp.exp(w_log)
    project(2)
    a = _sigmoid(a0_ref[...] + _bdot(wa, a2_ref[...]))
    g_out[...] = _bdot(_sigmoid(gl), g2_ref[...])
    project(3)
    k = shifted(1)
    kk = k * kk_ref[...]
    norm = jnp.sqrt(_dot_mask(kk * kk, bd_ref[...]))
    project(4)
    kk = kk / jnp.maximum(norm, KK_NORM_FLOOR)
    kk_out[...] = kk
    kka_out[...] = kk * a
    k2 = k * (1.0 + (a - 1.0) * ka_ref[...])
    k_out[...] = k2
    project(5)
    r = shifted(0)
    r_out[...] = r
    project(6)
    v = shifted(2)
    v_out[...] = v.astype(v_out.dtype)
    project(7)
    bonus_out[...] = _dot_mask(r * k2 * rk_ref[...], bd_ref[...]) * v


def _rwkv_prep(x, batch, seq, params):
    t = batch * seq
    tm = min(TILE_ROWS, seq)
    bps = seq // tm
    n_tiles = t // tm
    width = params[0].shape[1]

    def full(a):
        return pl.BlockSpec(a.shape, lambda s: (0,) * a.ndim)

    out = lambda dtype: jax.ShapeDtypeStruct((t, RWKV_WIDTH), dtype)
    ospec = pl.BlockSpec((tm, RWKV_WIDTH), lambda s: (jnp.maximum(s - 1, 0), 0))
    out_dtypes = (F32, F32, BF16, F32, F32, F32, F32, F32)
    return pl.pallas_call(
        functools.partial(_rwkv_prep_kernel, blocks_per_seq=bps),
        out_shape=[out(dt) for dt in out_dtypes],
        grid=(n_tiles + 1,),
        in_specs=[pl.BlockSpec((tm, D_MODEL), lambda s: (jnp.minimum(s, n_tiles - 1), 0))]
        + [full(a) for a in params],
        out_specs=[ospec] * 8,
        scratch_shapes=[pltpu.VMEM((tm, width), F32), pltpu.VMEM((tm, width), F32),
                        pltpu.VMEM((SUBLANES, width), F32)],
        compiler_params=_cparams("arbitrary"),
    )(x, *params)


PAIR_W = 2 * RWKV_HEAD
N_PAIRS = RWKV_HEADS // 2
HEAD_SHIFT = RWKV_HEAD.bit_length() - 1


def _pair_masks():
    t = lax.broadcasted_iota(I32, (CHUNK, PAIR_W), 0)
    lane = lax.broadcasted_iota(I32, (CHUNK, PAIR_W), 1)
    j = lane & (RWKV_HEAD - 1)
    r = lax.broadcasted_iota(I32, (PAIR_W, PAIR_W), 0)
    c = lax.broadcasted_iota(I32, (PAIR_W, PAIR_W), 1)
    r4 = lax.broadcasted_iota(I32, (2 * PAIR_W, PAIR_W), 0)
    c4 = lax.broadcasted_iota(I32, (2 * PAIR_W, PAIR_W), 1)
    return dict(
        strict=t > j, incl=t >= j, eye=(t == j).astype(F32),
        same_block=jnp.bitwise_xor(t, j) < INV_BLOCK,
        first_head=lane < RWKV_HEAD,
        block_diag=(r >> HEAD_SHIFT) == (c >> HEAD_SHIFT),
        stacked_diag=((r4 >> HEAD_SHIFT) & 1) == (c4 >> HEAD_SHIFT))


N_SLOT_REFS = 5


def _rwkv_scan_kernel(*refs, chunks, blocks_per_seq):
    *io_refs, st_ref = refs[:-2 * N_SLOT_REFS]
    slot_a, slot_b = refs[-2 * N_SLOT_REFS:-N_SLOT_REFS], refs[-N_SLOT_REFS:]
    s = pl.program_id(0)

    @pl.when(s == 0)
    def _():
        for ref in slot_b:
            ref[...] = jnp.zeros_like(ref)

    @pl.when((s == 0) | (((s + blocks_per_seq - 1) % blocks_per_seq) == 0))
    def _():
        st_ref[...] = jnp.zeros_like(st_ref)

    body = functools.partial(_rwkv_scan_body, chunks, (*io_refs, st_ref))

    @pl.when((s & 1) == 0)
    def _():
        body(slot_a, slot_b)

    @pl.when((s & 1) == 1)
    def _():
        body(slot_b, slot_a)


def _rwkv_scan_body(chunks, refs, wr, rd):
    r_ref, k_ref, v_ref, lw_ref, kk_ref, kka_ref, bonus_ref, g_ref, lg_ref, lb_ref, o_ref, st_ref = refs
    yl_w, rm_w, gm_w, qm_w, gam_w = wr
    yl_r, rm_r, gm_r, qm_r, gam_r = rd
    row, col = _tri_masks(CHUNK)
    ltri = (row >= col).astype(BF16)
    m = _pair_masks()
    zero = jnp.zeros((), BF16)
    pairs = range(N_PAIRS)
    sl = [slice(p * PAIR_W, (p + 1) * PAIR_W) for p in pairs]

    def block_diag(x):
        xb = x.astype(BF16)
        return jnp.where(m["block_diag"], jnp.concatenate([xb, xb], axis=0), zero)

    def pair_mul(lhs, *rhs):
        rb = jnp.concatenate([block_diag(x) for x in rhs], axis=1) if len(rhs) > 1 else block_diag(rhs[0])
        out = jnp.dot(lhs.astype(BF16), rb, preferred_element_type=F32)
        return [out[:, i * PAIR_W:(i + 1) * PAIR_W] for i in range(len(rhs))]

    def each(fn, *lists):
        return [fn(*args) for args in zip(*lists)]

    carried = {"state": [st_ref[p] for p in pairs]}
    pending = iter(range(chunks))
    inv_n = 1.0 / RWKV_HEAD

    def head_stat(z):
        s0 = jnp.sum(jnp.where(m["first_head"], z, 0.0), axis=-1, keepdims=True)
        s1 = jnp.sum(jnp.where(m["first_head"], 0.0, z), axis=-1, keepdims=True)
        return jnp.where(m["first_head"], s0, s1) * inv_n

    def previous_tile_chunk():
        c = next(pending, None)
        if c is None:
            return
        rows = pl.ds(c * CHUNK, CHUNK)
        items = range(c * N_PAIRS, (c + 1) * N_PAIRS)
        state = carried["state"]
        y = [yl_r[i] + _bdot_nt(rm_r[i], st) for i, st in zip(items, state)]
        carried["state"] = [st * gam_r[i, 0:1, :] + _bdot_nt(st, gm_r[i]) + qm_r[i] for i, st in zip(items, state)]
        outs = []
        for yp in y:
            yc = yp - head_stat(yp)
            outs.append(yc * lax.rsqrt(head_stat(yc * yc) + RWKV_LNX_EPS))
        yn = jnp.concatenate(outs, axis=1) * lg_ref[...] + lb_ref[...]
        o_ref[rows, :] = ((yn + bonus_ref[rows, :]) * g_ref[rows, :]).astype(o_ref.dtype)

    lhs, rhs, vp, ends, gammas = [], [], [], [], []
    for c in range(chunks):
        rows = pl.ds(c * CHUNK, CHUNK)
        lw = lw_ref[rows, :]
        cum = _mask_dot(ltri, lw)
        cum_end = cum[CHUNK - 1:CHUNK, :]
        e_inv = jnp.exp(-cum)
        e_end = jnp.exp(cum_end - cum)
        kka = kka_ref[rows, :]
        k = k_ref[rows, :]
        a_bar = (-kk_ref[rows, :] * jnp.exp(cum - lw)).astype(BF16)
        r_bar = (r_ref[rows, :] * jnp.exp(cum)).astype(BF16)
        b_til = (kka * e_inv).astype(BF16)
        k_til = (k * e_inv).astype(BF16)
        b_end = (kka * e_end).astype(BF16)
        k_end = (k * e_end).astype(BF16)
        gamma = jnp.exp(cum_end)
        v = v_ref[rows, :].astype(BF16)

        for s in sl:
            lhs.append(jnp.concatenate([a_bar[:, s], r_bar[:, s]], axis=0))
            rhs.append(jnp.where(m["stacked_diag"], jnp.concatenate(
                [b_til[:, s], b_til[:, s], k_til[:, s], k_til[:, s]], axis=0), zero))
            vp.append(v[:, s])
            ends.append(jnp.concatenate([b_end[:, s], k_end[:, s]], axis=0))
            gammas.append(gamma[:, s])

    blocks = each(_bdot_nt, lhs, rhs)
    previous_tile_chunk()
    a_ab = [jnp.where(m["strict"], b[:CHUNK, :PAIR_W], 0.0) for b in blocks]
    a_ak = [jnp.where(m["strict"], b[:CHUNK, PAIR_W:], 0.0) for b in blocks]
    r_ab = [jnp.where(m["incl"], b[CHUNK:, :PAIR_W], 0.0) for b in blocks]
    r_ak = [jnp.where(m["incl"], b[CHUNK:, PAIR_W:], 0.0) for b in blocks]
    from_v = [jnp.dot(jnp.concatenate([ak, rk], axis=0).astype(BF16), block_diag(x),
                      preferred_element_type=F32) for ak, rk, x in zip(a_ak, r_ak, vp)]
    previous_tile_chunk()
    d = [jnp.where(m["same_block"], a, 0.0) for a in a_ab]
    low = [a - di for a, di in zip(a_ab, d)]
    pm = [m["eye"] + di for di in d]
    dpow = [pair_mul(di, di)[0] for di in d]
    previous_tile_chunk()
    for _ in range(INV_BLOCK.bit_length() - 3):
        both = each(lambda x, p: pair_mul(x, x, p), dpow, pm)
        dpow = [b[0] for b in both]
        pm = [p + b[1] for p, b in zip(pm, both)]
        previous_tile_chunk()
    pm = [p + pair_mul(x, p)[0] for p, x in zip(pm, dpow)]
    previous_tile_chunk()
    both = each(lambda p, lo, fv, l: pair_mul(p, lo, fv[:CHUNK], l[:CHUNK]), pm, low, from_v, lhs)
    previous_tile_chunk()
    both2 = each(lambda b: pair_mul(b[0], b[0], b[1], b[2]), both)
    previous_tile_chunk()
    xw_v = [b[1] + b2[1] for b, b2 in zip(both, both2)]
    xw_a = [b[2] + b2[2] for b, b2 in zip(both, both2)]
    both3 = each(lambda b2, xv, xa: pair_mul(b2[0], xv, xa), both2, xw_v, xw_a)
    u_v = [xv + b3[0] for xv, b3 in zip(xw_v, both3)]
    wm = [xa + b3[1] for xa, b3 in zip(xw_a, both3)]
    both4 = each(lambda rb, uv, w: pair_mul(rb, uv, w), r_ab, u_v, wm)
    y_local = [fv[CHUNK:] + b4[0] for fv, b4 in zip(from_v, both4)]
    rm = [l[CHUNK:] + b4[1] for l, b4 in zip(lhs, both4)]
    gm = [jnp.where(m["block_diag"], _bdot_tn(en[:CHUNK], w), 0.0) for en, w in zip(ends, wm)]
    qm = [jnp.where(m["block_diag"], _bdot_tn(jnp.concatenate([uv.astype(BF16), x], axis=0), en), 0.0)
          for uv, x, en in zip(u_v, vp, ends)]
    for _ in range(chunks):
        previous_tile_chunk()
    for p in pairs:
        st_ref[p] = carried["state"][p]

    for i in range(chunks * N_PAIRS):
        yl_w[i] = y_local[i]
        rm_w[i] = rm[i]
        gm_w[i] = gm[i]
        qm_w[i] = qm[i]
        gam_w[i, 0:1, :] = gammas[i]


def _rwkv_scan(prep, batch, seq, lnx_g, lnx_b):
    t = batch * seq
    chunks = CHUNKS_PER_STEP
    rows = CHUNK * chunks
    nblk = seq // rows
    n_tiles = batch * nblk
    n_items = chunks * N_PAIRS
    cur = pl.BlockSpec((rows, RWKV_WIDTH), lambda s: (jnp.minimum(s, n_tiles - 1), 0))
    prev = pl.BlockSpec((rows, RWKV_WIDTH), lambda s: (jnp.maximum(s - 1, 0), 0))
    pspec = pl.BlockSpec((1, RWKV_WIDTH), lambda s: (0, 0))
    slot = [pltpu.VMEM((n_items, CHUNK, PAIR_W), F32), pltpu.VMEM((n_items, CHUNK, PAIR_W), F32),
            pltpu.VMEM((n_items, PAIR_W, PAIR_W), F32), pltpu.VMEM((n_items, PAIR_W, PAIR_W), F32),
            pltpu.VMEM((n_items, SUBLANES, PAIR_W), F32)]
    return pl.pallas_call(
        functools.partial(_rwkv_scan_kernel, chunks=chunks, blocks_per_seq=nblk),
        out_shape=jax.ShapeDtypeStruct((t, RWKV_WIDTH), BF16),
        grid=(n_tiles + 1,),
        in_specs=[cur] * 6 + [prev, prev, pspec, pspec],
        out_specs=prev,
        scratch_shapes=[pltpu.VMEM((N_PAIRS, PAIR_W, PAIR_W), F32)] + slot + slot,
        compiler_params=_cparams("arbitrary"),
    )(*prep, lnx_g, lnx_b)


def _merge_kernel(og_ref, ol_ref, or_ref, x_ref, wgt_ref, bgt_ref, pg_ref, pl_ref, pr_ref, wo_ref,
                  lg_ref, lb_ref, xo_ref):
    d = D_MODEL
    xin = x_ref[...].astype(BF16)
    merged = None
    for b, (o_ref, p_ref) in enumerate(((og_ref, pg_ref), (ol_ref, pl_ref), (or_ref, pr_ref))):
        cols = slice(b * d, (b + 1) * d)
        gate = _sigmoid(jnp.dot(xin, wgt_ref[:, cols], preferred_element_type=F32) + bgt_ref[:, cols])
        term = gate * jnp.dot(o_ref[...], p_ref[...], preferred_element_type=F32)
        merged = term if merged is None else merged + term
    z = DEEPNORM_ALPHA * x_ref[...] + _bdot(merged, wo_ref[...])
    out = _layer_norm_rows(z, lg_ref[...], lb_ref[...], LN_EPS)
    xo_ref[...] = out


def _merge(o_gla, o_lru, o_rwkv, x, w_gates, b_gates, p_gla, p_lru, p_rwkv, w_out, ln_g, ln_b):
    t = x.shape[0]
    tm = min(WIDE_TILE_ROWS, t)

    def rows(width):
        return pl.BlockSpec((tm, width), lambda i: (i, 0))

    def full(a):
        return pl.BlockSpec(a.shape, lambda i: (0,) * a.ndim, pipeline_mode=pl.Buffered(1))

    params = (w_gates, b_gates, p_gla, p_lru, p_rwkv, w_out, ln_g, ln_b)
    return pl.pallas_call(
        _merge_kernel,
        out_shape=jax.ShapeDtypeStruct((t, D_MODEL), F32),
        grid=(t // tm,),
        in_specs=[rows(GLA_VW), rows(LRU_WIDTH), rows(RWKV_WIDTH), rows(D_MODEL)] + [full(a) for a in params],
        out_specs=rows(D_MODEL),
        compiler_params=_cparams("parallel"),
    )(o_gla, o_lru, o_rwkv, x, *params)


FFN_SPLIT = 2


def _ffn_kernel(x_ref, wg_ref, wu_ref, wd_ref, lg_ref, lb_ref, xo_ref):
    x = x_ref[...]
    xb = x.astype(BF16)
    tf = wg_ref.shape[1] // FFN_SPLIT
    acc = None
    for f in range(FFN_SPLIT):
        cols = slice(f * tf, (f + 1) * tf)
        g = jnp.dot(xb, wg_ref[:, cols], preferred_element_type=F32)
        u = jnp.dot(xb, wu_ref[:, cols], preferred_element_type=F32)
        part = _bdot(g * _sigmoid(g) * u, wd_ref[cols, :])
        acc = part if acc is None else acc + part
    z = DEEPNORM_ALPHA * x + acc
    xo_ref[...] = _layer_norm_rows(z, lg_ref[...], lb_ref[...], LN_EPS)


def _dense_ffn(x, wg, wu, wd, ln_g, ln_b):
    t = x.shape[0]
    tm = min(WIDE_TILE_ROWS, t)
    rows = pl.BlockSpec((tm, D_MODEL), lambda i: (i, 0))
    resident = lambda a: pl.BlockSpec(a.shape, lambda i: (0,) * a.ndim, pipeline_mode=pl.Buffered(1))
    return pl.pallas_call(
        _ffn_kernel,
        out_shape=jax.ShapeDtypeStruct((t, D_MODEL), F32),
        grid=(t // tm,),
        in_specs=[rows] + [resident(a) for a in (wg, wu, wd, ln_g, ln_b)],
        out_specs=rows,
        compiler_params=_cparams("parallel"),
    )(x, wg, wu, wd, ln_g, ln_b)


ROW_TILE = (D_MODEL // LANES, LANES)
assert ROW_TILE[0] == SUBLANES


def _rows_to_tiles(x, ref3):
    ref3[...] = x.reshape(x.shape[0], *ROW_TILE)


def _tiles_to_rows(ref3):
    return ref3[...].reshape(ref3.shape[0], D_MODEL)


def _router_kernel(x_ref, wr_ref, meta_ref, wts_ref, cnt_ref, carry_ref):
    tm = x_ref.shape[0]

    @pl.when(pl.program_id(0) == 0)
    def _():
        carry_ref[...] = jnp.zeros_like(carry_ref)

    lane = lax.broadcasted_iota(I32, (tm, LANES), 1)
    neg = jnp.float32(-jnp.inf)
    logits = jnp.where(lane < N_EXPERTS, _fdot(x_ref[...], wr_ref[...]), neg)
    m1 = jnp.max(logits, axis=-1, keepdims=True)
    e1 = jnp.min(jnp.where(logits == m1, lane, LANES), axis=-1, keepdims=True)
    rest = jnp.where(lane == e1, neg, logits)
    m2 = jnp.max(rest, axis=-1, keepdims=True)
    e2 = jnp.min(jnp.where(rest == m2, lane, LANES), axis=-1, keepdims=True)
    ex = jnp.exp(m2 - m1)
    w1 = 1.0 / (1.0 + ex)
    w2 = ex / (1.0 + ex)

    hot1 = lane == e1
    hot2 = lane == e2
    onehot = jnp.where(hot1 | hot2, 1.0, 0.0)
    row, col = _tri_masks(tm)
    before = (row > col).astype(BF16)
    prefix = jnp.dot(before, onehot.astype(BF16), preferred_element_type=F32) + carry_ref[...]
    rank1 = jnp.sum(jnp.where(hot1, prefix, 0.0), axis=-1, keepdims=True).astype(I32)
    rank2 = jnp.sum(jnp.where(hot2, prefix, 0.0), axis=-1, keepdims=True).astype(I32)
    carry_ref[...] += jnp.sum(onehot, axis=0, keepdims=True)
    cnt_ref[...] = carry_ref[...]

    meta = jnp.where(lane == 0, e1, jnp.where(lane == 1, e2, jnp.where(lane == 2, rank1, rank2)))
    meta_ref[...] = jnp.where(lane < 4, meta, 0)
    wts_ref[...] = jnp.where(lane == 0, w1, jnp.where(lane == 1, w2, 0.0))


def _router(x, wr_pad):
    t = x.shape[0]
    tm = min(TILE_ROWS, t)
    return pl.pallas_call(
        _router_kernel,
        out_shape=[jax.ShapeDtypeStruct((t, LANES), I32), jax.ShapeDtypeStruct((t, LANES), F32),
                   jax.ShapeDtypeStruct((1, LANES), F32)],
        grid=(t // tm,),
        in_specs=[pl.BlockSpec((tm, D_MODEL), lambda i: (i, 0)),
                  pl.BlockSpec((D_MODEL, LANES), lambda i: (0, 0))],
        out_specs=[pl.BlockSpec((tm, LANES), lambda i: (i, 0)), pl.BlockSpec((tm, LANES), lambda i: (i, 0)),
                   pl.BlockSpec((1, LANES), lambda i: (0, 0))],
        scratch_shapes=[pltpu.VMEM((1, LANES), F32)],
        compiler_params=_cparams("arbitrary"),
    )(x, wr_pad)


def _dispatch_kernel(pad_start_ref, pad_len_ref, nused_ref, dest_ref, x_ref, *rest):
    n_w = (len(rest) - 5) // 2
    w_refs, xs_ref, wb_refs = rest[:n_w], rest[n_w], rest[n_w + 1:2 * n_w + 1]
    x3_ref, zeros_ref, sem, fill_sem = rest[-4:]
    tm = x_ref.shape[0]
    n_blocks = xs_ref.shape[0] // MOE_ROWS
    _rows_to_tiles(x_ref[...], x3_ref)

    @pl.when(pl.program_id(0) == pl.num_programs(0) - 1)
    def _():
        zeros_ref[...] = jnp.zeros_like(zeros_ref)
        pad_copy = lambda row: pltpu.make_async_copy(zeros_ref.at[0], xs_ref.at[row], fill_sem)
        blk_copy = lambda blk: pltpu.make_async_copy(
            zeros_ref, xs_ref.at[pl.ds(pl.multiple_of(blk * MOE_ROWS, MOE_ROWS), MOE_ROWS)], fill_sem)

        def for_each(fn):
            for e in range(N_EXPERTS):
                base = pad_start_ref[e]
                lax.fori_loop(0, pad_len_ref[e], lambda r, c: (fn(pad_copy(base + r)), c)[1], 0)
            lax.fori_loop(nused_ref[0], n_blocks, lambda b, c: (fn(blk_copy(b)), c)[1], 0)

        for_each(lambda cp: cp.start())
        for_each(lambda cp: cp.wait())

    def row_copy(r, j):
        return pltpu.make_async_copy(x3_ref.at[r], xs_ref.at[dest_ref[TOP_K * r + j]], sem)

    def start(r, c):
        for j in range(TOP_K):
            row_copy(r, j).start(priority=j)
        return c

    lax.fori_loop(0, tm, start, 0, unroll=8)
    for w_ref, wb_ref in zip(w_refs, wb_refs):
        wb_ref[...] = w_ref[...].astype(wb_ref.dtype)
    for j in range(TOP_K):
        pltpu.make_async_copy(x3_ref, xs_ref.at[pl.ds(0, tm)], sem).wait()


def _dispatch(x, dest_flat, pad_start, pad_len, n_used, n_slots, weights):
    t = x.shape[0]
    tm = min(WIDE_TILE_ROWS, t)
    n_steps = t // tm
    flat = [w.reshape(-1, w.shape[-1]) for w in weights]
    slab = lambda w: pl.BlockSpec((w.shape[0] // n_steps, w.shape[1]), lambda i, *_: (i, 0))
    grid_spec = pltpu.PrefetchScalarGridSpec(
        num_scalar_prefetch=3,
        grid=(n_steps,),
        in_specs=[pl.BlockSpec((tm * TOP_K,), lambda i, *_: (i,), memory_space=pltpu.SMEM),
                  pl.BlockSpec((tm, D_MODEL), lambda i, *_: (i, 0))] + [slab(w) for w in flat],
        out_specs=[pl.BlockSpec(memory_space=pl.ANY)] + [slab(w) for w in flat],
        scratch_shapes=[pltpu.VMEM((tm, *ROW_TILE), F32), pltpu.VMEM((MOE_ROWS, *ROW_TILE), F32),
                        pltpu.SemaphoreType.DMA(()), pltpu.SemaphoreType.DMA(())],
    )
    xs, *cast = pl.pallas_call(
        _dispatch_kernel,
        out_shape=[jax.ShapeDtypeStruct((n_slots, *ROW_TILE), F32)]
        + [jax.ShapeDtypeStruct(w.shape, BF16) for w in flat],
        grid_spec=grid_spec,
        compiler_params=_cparams("arbitrary"),
    )(pad_start, pad_len, n_used, dest_flat, x, *flat)
    return xs, [c.reshape(w.shape) for c, w in zip(cast, weights)]


def _expert_kernel(blk_e_ref, nused_ref, xs_ref, wg_ref, wu_ref, wd_ref, ys_ref, acc_ref, *, n_f):
    del blk_e_ref
    i = pl.program_id(0)
    f = pl.program_id(1)
    last = n_f - 1
    used = i < nused_ref[0]

    def piece():
        xb = _tiles_to_rows(xs_ref).astype(BF16)
        g = jnp.dot(xb, wg_ref[0], preferred_element_type=F32)
        u = jnp.dot(xb, wu_ref[0], preferred_element_type=F32)
        return _bdot(g * _sigmoid(g) * u, wd_ref[0])

    if n_f == 1:
        @pl.when(used)
        def _():
            _rows_to_tiles(piece(), ys_ref)
    else:
        @pl.when(used & (f == 0))
        def _():
            acc_ref[...] = piece()

        if n_f > 2:
            @pl.when(used & (f != 0) & (f != last))
            def _():
                acc_ref[...] += piece()

        @pl.when(used & (f == last))
        def _():
            _rows_to_tiles(acc_ref[...] + piece(), ys_ref)

    @pl.when(jnp.logical_not(used) & (f == last))
    def _():
        ys_ref[...] = jnp.zeros_like(ys_ref)


def _experts(xs, blk_e, n_used, wg, wu, wd):
    n_slots = xs.shape[0]
    n_blocks = n_slots // MOE_ROWS
    ff = wg.shape[2]
    n_f = 2
    tf = ff // n_f
    piece = lambda i, f: jnp.where((i & 1) == 1, n_f - 1 - f, f)
    grid_spec = pltpu.PrefetchScalarGridSpec(
        num_scalar_prefetch=2,
        grid=(n_blocks, n_f),
        in_specs=[pl.BlockSpec((MOE_ROWS, *ROW_TILE), lambda i, f, be, nu: (i, 0, 0)),
                  pl.BlockSpec((1, D_MODEL, tf), lambda i, f, be, nu: (be[i], 0, piece(i, f))),
                  pl.BlockSpec((1, D_MODEL, tf), lambda i, f, be, nu: (be[i], 0, piece(i, f))),
                  pl.BlockSpec((1, tf, D_MODEL), lambda i, f, be, nu: (be[i], piece(i, f), 0))],
        out_specs=pl.BlockSpec((MOE_ROWS, *ROW_TILE), lambda i, f, be, nu: (i, 0, 0)),
        scratch_shapes=[pltpu.VMEM((MOE_ROWS, D_MODEL), F32)],
    )
    return pl.pallas_call(
        functools.partial(_expert_kernel, n_f=n_f),
        out_shape=jax.ShapeDtypeStruct((n_slots, *ROW_TILE), F32),
        grid_spec=grid_spec,
        compiler_params=_cparams("arbitrary", "arbitrary"),
    )(blk_e, n_used, xs, wg, wu, wd)


def _combine_kernel(dest_ref, dest_next_ref, x_ref, wts_ref, ys_ref, lg_ref, lb_ref, xo_ref, buf_ref, sem):
    tm = x_ref.shape[0]
    i = pl.program_id(0)
    n = pl.num_programs(0)

    def gather(idx_ref, slot):
        def start(r, c):
            for j in range(TOP_K):
                pltpu.make_async_copy(ys_ref.at[idx_ref[TOP_K * r + j]],
                                      buf_ref.at[slot, j, r], sem.at[slot]).start(priority=j)
            return c
        lax.fori_loop(0, tm, start, 0, unroll=8)

    slot = i & 1

    @pl.when(i == 0)
    def _():
        gather(dest_ref, 0)

    @pl.when(i + 1 < n)
    def _():
        gather(dest_next_ref, 1 - slot)

    for j in range(TOP_K):
        pltpu.make_async_copy(ys_ref.at[pl.ds(0, tm)], buf_ref.at[slot, j], sem.at[slot]).wait()
    w = wts_ref[...]
    f = w[:, 0:1] * _tiles_to_rows(buf_ref.at[slot, 0]) + w[:, 1:2] * _tiles_to_rows(buf_ref.at[slot, 1])
    z = DEEPNORM_ALPHA * x_ref[...] + f
    xo_ref[...] = _layer_norm_rows(z, lg_ref[...], lb_ref[...], LN_EPS)


def _combine(x, dest_flat, wts, ys, ln_g, ln_b):
    t = x.shape[0]
    tm = min(TILE_ROWS, t)
    n_tiles = t // tm
    rows = lambda width: pl.BlockSpec((tm, width), lambda i: (i, 0))
    vec = pl.BlockSpec((1, D_MODEL), lambda i: (0, 0))
    return pl.pallas_call(
        _combine_kernel,
        out_shape=jax.ShapeDtypeStruct((t, D_MODEL), F32),
        grid=(n_tiles,),
        in_specs=[pl.BlockSpec((tm * TOP_K,), lambda i: (i,), memory_space=pltpu.SMEM),
                  pl.BlockSpec((tm * TOP_K,), lambda i: (jnp.minimum(i + 1, n_tiles - 1),),
                               memory_space=pltpu.SMEM),
                  rows(D_MODEL), rows(LANES), pl.BlockSpec(memory_space=pl.ANY), vec, vec],
        out_specs=rows(D_MODEL),
        scratch_shapes=[pltpu.VMEM((2, TOP_K, tm, *ROW_TILE), F32), pltpu.SemaphoreType.DMA((2,))],
        compiler_params=_cparams("arbitrary"),
    )(dest_flat, dest_flat, x, wts, ys, ln_g, ln_b)


def _moe_ffn(x, w_router, wg, wu, wd, ln_g, ln_b):
    t = x.shape[0]
    wr_pad = jnp.pad(w_router, ((0, 0), (0, LANES - N_EXPERTS)))
    meta, wts, cnt = _router(x, wr_pad)
    counts = cnt[0, :N_EXPERTS].astype(I32)
    padded = (counts + MOE_ROWS - 1) // MOE_ROWS * MOE_ROWS
    seg_end = jnp.cumsum(padded)
    seg_start = seg_end - padded
    n_blocks = (t * TOP_K) // MOE_ROWS + N_EXPERTS
    dest = seg_start[meta[:, 0:TOP_K]] + meta[:, TOP_K:2 * TOP_K]
    dest_flat = dest.reshape(-1).astype(I32)
    blk_start = jnp.arange(n_blocks, dtype=I32) * MOE_ROWS
    blk_e = jnp.minimum(jnp.sum(blk_start[:, None] >= seg_end[None, :], axis=1), N_EXPERTS - 1).astype(I32)
    n_used = (seg_end[-1:] // MOE_ROWS).astype(I32)
    xs, (wg, wu, wd) = _dispatch(x, dest_flat, (seg_start + counts).astype(I32), (padded - counts).astype(I32),
                                 n_used, n_blocks * MOE_ROWS, (wg, wu, wd))
    ys = _experts(xs, blk_e, n_used, wg, wu, wd)
    return _combine(x, dest_flat, wts, ys, ln_g, ln_b)


def _pad_rows(w, rows, at=0):
    out = jnp.zeros((rows, w.shape[1]), w.dtype)
    return out.at[at:at + w.shape[0]].set(w)


def _reorder_in_projection(w, b):
    sizes = (GLA_KW, GLA_KW, GLA_VW, GLA_VW, GLA_DECAY_RANK, LRU_WIDTH, LRU_WIDTH,
             3 * RWKV_WIDTH, RWKV_DECAY_RANK, RWKV_A_RANK, RWKV_GATE_RANK, N_BRANCH * D_MODEL)
    offs = [0]
    for s in sizes:
        offs.append(offs[-1] + s)
    wb = jnp.concatenate([w, b[None, :]], axis=0)
    piece = lambda i: wb[:, offs[i]:offs[i + 1]]
    zeros = lambda n: jnp.zeros((wb.shape[0], n), wb.dtype)
    q, k, v, r, dec, lx, lg, rkv, wl, al, gl, gates = (piece(i) for i in range(len(sizes)))
    small = jnp.concatenate([wl, al, gl, zeros(RW_SMALL_W - SM_GL[0] - RWKV_GATE_RANK)], axis=1)
    groups = {"gla": jnp.concatenate([q, k, v, r, dec, zeros(GLA_DEC_W - GLA_DECAY_RANK)], axis=1),
              "lru": jnp.concatenate([lx, lg], axis=1),
              "rwkv": jnp.concatenate([rkv, small], axis=1),
              "gates": gates}
    return {name: (g[:-1].astype(BF16), g[-1:]) for name, g in groups.items()}


def _block_diag(blocks):
    n, bi, bo = blocks.shape
    eye = jnp.eye(n, dtype=blocks.dtype)
    return (eye[:, None, :, None] * blocks[:, :, None, :]).reshape(n * bi, n * bo)


def kernel(x, w_in, b_in, gla_w_decay_up, gla_b_decay, gla_norm_g, gla_norm_b, lru_conv_w, lru_conv_b, lru_w_r, lru_b_r, lru_w_i, lru_b_i, lru_lambda, rwkv_mu, rwkv_w0, rwkv_w2, rwkv_a0, rwkv_a2, rwkv_g2, rwkv_k_k, rwkv_k_a, rwkv_r_k, rwkv_lnx_g, rwkv_lnx_b, p_gla, p_lru, p_rwkv, w_out, ln_mix_g, ln_mix_b, ffn_w_gate, ffn_w_up, ffn_w_down, moe_w_router, moe_w_gate, moe_w_up, moe_w_down, ln_ffn_g, ln_ffn_b):
    batch, seq, d = x.shape
    t = batch * seq
    xf = x.reshape(t, d)
    row = lambda a: a.reshape(1, -1)
    head_ones = _block_diag(jnp.ones((RWKV_HEADS, RWKV_HEAD, RWKV_HEAD), BF16))
    for l in range(DEPTH):
        proj = _reorder_in_projection(w_in[l], b_in[l])
        w_gates, b_gates = proj["gates"]

        wup = _pad_rows(gla_w_decay_up[l], GLA_DEC_W).astype(BF16)
        o_gla = _gla_branch(xf, batch, seq, *proj["gla"], wup, row(gla_b_decay[l]), row(gla_norm_g[l]),
                            row(gla_norm_b[l]))

        w_ri = jnp.concatenate([_block_diag(lru_w_r[l]), _block_diag(lru_w_i[l])], axis=1).astype(BF16)
        b_ri = jnp.concatenate([lru_b_r[l], lru_b_i[l]])[None, :]
        o_lru = _lru_branch(xf, batch, seq, *proj["lru"], lru_conv_w[l], row(lru_conv_b[l]), w_ri, b_ri,
                            row(jax.nn.softplus(-lru_lambda[l])))

        mu = rwkv_mu[l]
        mu_all = jnp.zeros((3 * RWKV_WIDTH + RW_SMALL_W,), F32).at[0:mu.shape[0]].set(mu)
        prep_params = (*proj["rwkv"], row(mu_all),
                       _pad_rows(rwkv_w2[l], SM_WA[1], 0).astype(BF16),
                       _pad_rows(rwkv_a2[l], SM_WA[1], RWKV_DECAY_RANK).astype(BF16),
                       _pad_rows(rwkv_g2[l], SM_GL[1], 0).astype(BF16),
                       row(rwkv_w0[l]), row(rwkv_a0[l]), row(rwkv_k_k[l]), row(rwkv_k_a[l]),
                       row(rwkv_r_k[l]), head_ones)
        prep = _rwkv_prep(xf, batch, seq, prep_params)
        o_rwkv = _rwkv_scan(prep, batch, seq, row(rwkv_lnx_g[l]), row(rwkv_lnx_b[l]))

        xf = _merge(o_gla, o_lru, o_rwkv, xf, w_gates, b_gates, p_gla[l].astype(BF16), p_lru[l].astype(BF16),
                    p_rwkv[l].astype(BF16), w_out[l].astype(BF16), row(ln_mix_g[l]), row(ln_mix_b[l]))
        i = l // 2
        if l % 2 == 0:
            xf = _dense_ffn(xf, ffn_w_gate[i].astype(BF16), ffn_w_up[i].astype(BF16),
                            ffn_w_down[i].astype(BF16), row(ln_ffn_g[l]), row(ln_ffn_b[l]))
        else:
            xf = _moe_ffn(xf, moe_w_router[i], moe_w_gate[i], moe_w_up[i], moe_w_down[i],
                          row(ln_ffn_g[l]), row(ln_ffn_b[l]))
    return xf.reshape(batch, seq, d)
```

```python
import functools

import jax
import jax.numpy as jnp
from jax import lax
from jax.experimental import pallas as pl
from jax.experimental.pallas import tpu as pltpu

F32 = jnp.float32
BF16 = jnp.bfloat16
I32 = jnp.int32

D_MODEL = 1024
DEPTH = 2
GLA_HEADS, GLA_DK, GLA_DV = 4, 64, 128
GLA_KW, GLA_VW = GLA_HEADS * GLA_DK, GLA_HEADS * GLA_DV
GLA_DECAY_RANK = 16
GLA_TEMP = 16.0
LRU_WIDTH, LRU_BLOCKS, LRU_CONV, LRU_C = 512, 8, 4, 8.0
RWKV_HEAD, RWKV_WIDTH = 64, 512
RWKV_HEADS = RWKV_WIDTH // RWKV_HEAD
RWKV_DECAY_RANK, RWKV_A_RANK, RWKV_GATE_RANK = 64, 64, 160
RWKV_LNX_EPS = 64e-5
N_BRANCH = 3
N_EXPERTS, TOP_K = 8, 2
DEEPNORM_ALPHA = (2 * DEPTH) ** 0.25
LN_EPS = 1e-5

LANES = 128
SUBLANES = 8
VMEM_LIMIT_BYTES = 56 * 1024 * 1024

SM_WA = (0, 128)
SM_GL = (128, 256)
RW_SMALL_W = 512
GLA_DEC_W = LANES
GLA_COLS = {"q": (0, 256), "k": (256, 512), "v": (512, 1024), "r": (1024, 1536), "dec": (1536, 1664)}

CHUNK = 64
CHUNKS_PER_STEP = 8
INV_BLOCK = 16
MOE_ROWS = 512
TILE_ROWS = 512
WIDE_TILE_ROWS = 1024
RWKV_DECAY_OFFSET = 0.5
KK_NORM_FLOOR = 1e-12


def _cparams(*sem):
    return pltpu.CompilerParams(dimension_semantics=sem, vmem_limit_bytes=VMEM_LIMIT_BYTES)


def _sigmoid(x):
    return 1.0 / (1.0 + jnp.exp(-x))


def _softplus(x):
    return jnp.maximum(x, 0.0) + jnp.log(1.0 + jnp.exp(-jnp.abs(x)))


def _bdot(a, b):
    return jnp.dot(a.astype(BF16), b.astype(BF16), preferred_element_type=F32)


def _bdot_nt(a, b):
    return lax.dot_general(a.astype(BF16), b.astype(BF16), (((1,), (1,)), ((), ())),
                           preferred_element_type=F32)


def _bdot_tn(a, b):
    return lax.dot_general(a.astype(BF16), b.astype(BF16), (((0,), (0,)), ((), ())),
                           preferred_element_type=F32)


def _fdot(a, b):
    ah = a.astype(BF16)
    al = (a - ah.astype(F32)).astype(BF16)
    bh = b.astype(BF16)
    bl = (b - bh.astype(F32)).astype(BF16)
    dot = lambda u, v: jnp.dot(u, v, preferred_element_type=F32)
    return dot(al, bh) + dot(ah, bl) + dot(ah, bh)


def _split3(x):
    hi = x.astype(BF16)
    rest = x - hi.astype(F32)
    mid = rest.astype(BF16)
    lo = (rest - mid.astype(F32)).astype(BF16)
    return hi, mid, lo


def _mask_dot(mask_bf16, x):
    hi, mid, lo = _split3(x)
    dot = lambda part: jnp.dot(mask_bf16, part, preferred_element_type=F32)
    return dot(lo) + dot(mid) + dot(hi)


def _dot_mask(x, mask_bf16):
    hi, mid, lo = _split3(x)
    dot = lambda part: jnp.dot(part, mask_bf16, preferred_element_type=F32)
    return dot(lo) + dot(mid) + dot(hi)


def _layer_norm_rows(z, g, b, eps):
    mu = jnp.mean(z, axis=-1, keepdims=True)
    zc = z - mu
    var = jnp.mean(zc * zc, axis=-1, keepdims=True)
    return zc * lax.rsqrt(var + eps) * g + b


def _tri_masks(n):
    row = lax.broadcasted_iota(I32, (n, n), 0)
    col = lax.broadcasted_iota(I32, (n, n), 1)
    return row, col


MXU_COLS = 256


def _gla_kernel(*refs, chunks, blocks_per_seq):
    *io_refs, st_ref, ha_ref, hb_ref = refs
    s = pl.program_id(0)

    @pl.when(s == 0)
    def _():
        hb_ref[...] = jnp.zeros_like(hb_ref)

    @pl.when((s == 0) | (((s + blocks_per_seq - 1) % blocks_per_seq) == 0))
    def _():
        st_ref[...] = jnp.zeros_like(st_ref)

    body = functools.partial(_gla_body, chunks, (*io_refs, st_ref))

    @pl.when((s & 1) == 0)
    def _():
        body(ha_ref, hb_ref)

    @pl.when((s & 1) == 1)
    def _():
        body(hb_ref, ha_ref)


def _gla_body(chunks, refs, h_next, h_ref):
    x_ref, w_ref, b_ref, wup_ref, bdec_ref, ng_ref, nb_ref, o_ref, st_ref = refs
    xb = x_ref[...].astype(BF16)
    width = w_ref.shape[1]
    pieces = iter([slice(c0, min(c0 + MXU_COLS, width)) for c0 in range(0, width, MXU_COLS)])

    def project_piece():
        cols = next(pieces, None)
        if cols is not None:
            h_next[:, cols] = jnp.dot(xb, w_ref[:, cols], preferred_element_type=F32) + b_ref[:, cols]

    q_ref = h_ref.at[:, GLA_COLS["q"][0]:GLA_COLS["q"][1]]
    k_ref = h_ref.at[:, GLA_COLS["k"][0]:GLA_COLS["k"][1]]
    v_ref = h_ref.at[:, GLA_COLS["v"][0]:GLA_COLS["v"][1]]
    r_ref = h_ref.at[:, GLA_COLS["r"][0]:GLA_COLS["r"][1]]
    dec_ref = h_ref.at[:, GLA_COLS["dec"][0]:GLA_COLS["dec"][1]]

    row, col = _tri_masks(CHUNK)
    ltri = (row >= col).astype(BF16)
    kw, vw = 2 * GLA_DK, 2 * GLA_DV
    key_shift, val_shift = GLA_DK.bit_length() - 1, GLA_DV.bit_length() - 1
    iota = lambda shape, axis: lax.broadcasted_iota(I32, shape, axis)
    causal = iota((CHUNK, kw), 0) >= (iota((CHUNK, kw), 1) & (GLA_DK - 1))
    key_diag = (iota((kw, kw), 0) >> key_shift) == (iota((kw, kw), 1) >> key_shift)
    val_diag = (iota((kw, vw), 0) >> key_shift) == (iota((kw, vw), 1) >> val_shift)
    state_diag = (iota((vw, kw), 0) >> val_shift) == (iota((vw, kw), 1) >> key_shift)
    zero = jnp.zeros((), BF16)
    pairs = range(GLA_HEADS // 2)
    qp, vp, k_diag, kep, decays = [], [], [], [], []
    for c in range(chunks):
        project_piece()
        rows = pl.ds(c * CHUNK, CHUNK)
        logits = _bdot(dec_ref[rows, :], wup_ref[...]) + bdec_ref[...]
        log_a = -_softplus(-logits) * (1.0 / GLA_TEMP)
        cum = _mask_dot(ltri, log_a)
        cum_end = cum[CHUNK - 1:CHUNK, :]
        q = q_ref[rows, :] * (GLA_DK ** -0.5)
        k = k_ref[rows, :]
        q_dec = (q * jnp.exp(cum)).astype(BF16)
        k_inv = (k * jnp.exp(-cum)).astype(BF16)
        k_end = (k * jnp.exp(cum_end - cum)).astype(BF16)
        decay_end = jnp.exp(cum_end)
        v = v_ref[rows, :].astype(BF16)
        for p in pairs:
            s = slice(p * kw, (p + 1) * kw)
            qp.append(q_dec[:, s])
            vp.append(v[:, p * vw:(p + 1) * vw])
            k_diag.append(jnp.where(key_diag, jnp.concatenate([k_inv[:, s], k_inv[:, s]], axis=0), zero))
            kep.append(k_end[:, s])
            decays.append(decay_end[:, s])

    for _ in range(chunks):
        project_piece()

    v_diag = [jnp.where(val_diag, jnp.concatenate([x, x], axis=0), zero) for x in vp]
    scores = [jnp.where(causal, _bdot_nt(q, kd), 0.0) for q, kd in zip(qp, k_diag)]
    o_local = [_bdot(s, vd) for s, vd in zip(scores, v_diag)]
    upd = [jnp.where(state_diag, _bdot_tn(x, ke), 0.0) for x, ke in zip(vp, kep)]

    state = [st_ref[p] for p in pairs]
    n_pairs = len(pairs)
    for c in range(chunks):
        rows = pl.ds(c * CHUNK, CHUNK)
        item = slice(c * n_pairs, (c + 1) * n_pairs)
        o = [ol + _bdot_nt(q, st) for ol, q, st in zip(o_local[item], qp[item], state)]
        state = [st * dcy + u for st, dcy, u in zip(state, decays[item], upd[item])]
        outs = []
        for op in o:
            for oh in (op[:, :GLA_DV], op[:, GLA_DV:]):
                mu = jnp.mean(oh, axis=-1, keepdims=True)
                oc = oh - mu
                var = jnp.mean(oc * oc, axis=-1, keepdims=True)
                outs.append(oc * lax.rsqrt(var + LN_EPS))
        y = jnp.concatenate(outs, axis=1) * ng_ref[...] + nb_ref[...]
        r = r_ref[rows, :]
        o_ref[rows, :] = (y * (r * _sigmoid(r))).astype(o_ref.dtype)
    for p in pairs:
        st_ref[p] = state[p]


def _gla_branch(x, batch, seq, w, bias, wup_pad, b_decay, norm_g, norm_b):
    t = batch * seq
    chunks = CHUNKS_PER_STEP
    rows = CHUNK * chunks
    nblk = seq // rows

    n_tiles = batch * nblk

    def full(a):
        return pl.BlockSpec(a.shape, lambda s: (0,) * a.ndim)

    args = (w, bias, wup_pad, b_decay, norm_g, norm_b)
    return pl.pallas_call(
        functools.partial(_gla_kernel, chunks=chunks, blocks_per_seq=nblk),
        out_shape=jax.ShapeDtypeStruct((t, GLA_VW), BF16),
        grid=(n_tiles + 1,),
        in_specs=[pl.BlockSpec((rows, D_MODEL), lambda s: (jnp.minimum(s, n_tiles - 1), 0))]
        + [full(a) for a in args],
        out_specs=pl.BlockSpec((rows, GLA_VW), lambda s: (jnp.maximum(s - 1, 0), 0)),
        scratch_shapes=[pltpu.VMEM((GLA_HEADS // 2, 2 * GLA_DV, 2 * GLA_DK), F32),
                        pltpu.VMEM((rows, w.shape[1]), F32), pltpu.VMEM((rows, w.shape[1]), F32)],
        compiler_params=_cparams("arbitrary"),
    )(x, *args)


def _lru_kernel(xin_ref, w_ref, b_ref, cw_ref, cb_ref, wri_ref, bri_ref, sp_ref,
                o_ref, xbuf_ref, a_ref, u_ref, hs_ref, gate_ref, h_ref):
    tm = xin_ref.shape[0]
    pad = SUBLANES

    @pl.when(pl.program_id(1) == 0)
    def _():
        xbuf_ref[0:pad, :] = jnp.zeros((pad, LRU_WIDTH), F32)
        h_ref[...] = jnp.zeros_like(h_ref)

    proj = jnp.dot(xin_ref[...].astype(BF16), w_ref[...], preferred_element_type=F32) + b_ref[...]
    x = proj[:, :LRU_WIDTH]
    gate_ref[...] = jax.nn.gelu(proj[:, LRU_WIDTH:])
    xbuf_ref[pad:pad + tm, :] = x
    xc = cb_ref[...] + x * cw_ref[LRU_CONV - 1:LRU_CONV, :]
    for j in range(LRU_CONV - 1):
        back = LRU_CONV - 1 - j
        xc = xc + xbuf_ref[pl.ds(pad - back, tm), :] * cw_ref[j:j + 1, :]
    xbuf_ref[0:pad, :] = x[tm - pad:tm, :]

    ri = _sigmoid(_bdot(xc, wri_ref[...]) + bri_ref[...])
    rg, ig = ri[:, :LRU_WIDTH], ri[:, LRU_WIDTH:]
    log_a = -LRU_C * rg * sp_ref[...]
    a = jnp.exp(log_a)
    a_ref[...] = a
    u_ref[...] = jnp.sqrt(1.0 - a * a) * (ig * xc)

    def step(t, hprev):
        hnew = a_ref[pl.ds(t, 1), :] * hprev + u_ref[pl.ds(t, 1), :]
        hs_ref[pl.ds(t, 1), :] = hnew
        return hnew

    h_ref[...] = lax.fori_loop(0, tm, step, h_ref[...], unroll=8)
    o_ref[...] = (gate_ref[...] * hs_ref[...]).astype(o_ref.dtype)


def _lru_branch(x, batch, seq, w, bias, conv_w, conv_b, w_ri, b_ri, softplus_neg_lam):
    t = batch * seq
    tm = min(WIDE_TILE_ROWS, seq)
    nblk = seq // tm

    def full(a):
        return pl.BlockSpec(a.shape, lambda b, n: (0,) * a.ndim)

    args = (w, bias, conv_w, conv_b, w_ri, b_ri, softplus_neg_lam)
    return pl.pallas_call(
        _lru_kernel,
        out_shape=jax.ShapeDtypeStruct((t, LRU_WIDTH), BF16),
        grid=(batch, nblk),
        in_specs=[pl.BlockSpec((tm, D_MODEL), lambda b, n: (b * nblk + n, 0))] + [full(a) for a in args],
        out_specs=pl.BlockSpec((tm, LRU_WIDTH), lambda b, n: (b * nblk + n, 0)),
        scratch_shapes=[pltpu.VMEM((tm + SUBLANES, LRU_WIDTH), F32)]
        + [pltpu.VMEM((tm, LRU_WIDTH), F32)] * 4 + [pltpu.VMEM((1, LRU_WIDTH), F32)],
        compiler_params=_cparams("parallel", "arbitrary"),
    )(x, *args)


def _rwkv_prep_kernel(*refs, blocks_per_seq):
    *io_refs, ha_ref, hb_ref, carry_ref = refs
    tm = io_refs[0].shape[0]
    s = pl.program_id(0)
    body = functools.partial(_rwkv_prep_body, tm, s, RWKV_WIDTH, blocks_per_seq, (*io_refs, carry_ref))

    @pl.when(s == 0)
    def _():
        hb_ref[...] = jnp.zeros_like(hb_ref)
        carry_ref[...] = jnp.zeros_like(carry_ref)

    @pl.when((s & 1) == 0)
    def _():
        body(ha_ref, hb_ref)

    @pl.when((s & 1) == 1)
    def _():
        body(hb_ref, ha_ref)


def _rwkv_prep_body(tm, s, gw, blocks_per_seq, refs, h_next, h_cur):
    (x_ref, w_ref, b_ref, mu_ref, w2_ref, a2_ref, g2_ref, w0_ref, a0_ref, kk_ref, ka_ref, rk_ref, bd_ref,
     r_out, k_out, v_out, lw_out, kk_out, kka_out, bonus_out, g_out, carry_ref) = refs
    first = ((s + blocks_per_seq - 1) % blocks_per_seq) == 0
    keep = jnp.where(first, 0.0, 1.0)
    row0 = lax.broadcasted_iota(I32, (tm, 1), 0) == 0
    xb = x_ref[...].astype(BF16)

    pw = gw // 2

    def project(j):
        cols = slice(j * pw, (j + 1) * pw)
        h_next[:, cols] = jnp.dot(xb, w_ref[:, cols], preferred_element_type=F32) + b_ref[:, cols]

    def shifted(j):
        cols = slice(j * gw, (j + 1) * gw)
        cur = h_cur[:, cols]
        prev_row = carry_ref[0:1, cols] * keep
        carry_ref[0:1, cols] = cur[tm - 1:tm, :]
        prev = jnp.where(row0, prev_row, pltpu.roll(cur, 1, axis=0))
        return cur + (prev - cur) * mu_ref[:, cols]

    project(0)
    sm = shifted(3)
    wa = sm[:, SM_WA[0]:SM_WA[0] + SM_WA[1]]
    gl = sm[:, SM_GL[0]:SM_GL[0] + SM_GL[1]]
    project(1)
    w_log = -_softplus(-(w0_ref[...] + _bdot(jnp.tanh(wa), w2_ref[...]))) - RWKV_DECAY_OFFSET
    lw_out[...] = -jnp.exp(w_log)
    project(2)
    a = _sigmoid(a0_ref[...] + _bdot(wa, a2_ref[...]))
    g_out[...] = _bdot(_sigmoid(gl), g2_ref[...])
    project(3)
    k = shifted(1)
    kk = k * kk_ref[...]
    norm = jnp.sqrt(_dot_mask(kk * kk, bd_ref[...]))
    project(4)
    kk = kk / jnp.maximum(norm, KK_NORM_FLOOR)
    kk_out[...] = kk
    kka_out[...] = kk * a
    k2 = k * (1.0 + (a - 1.0) * ka_ref[...])
    k_out[...] = k2
    project(5)
    r = shifted(0)
    r_out[...] = r
    project(6)
    v = shifted(2)
    v_out[...] = v.astype(v_out.dtype)
    project(7)
    bonus_out[...] = _dot_mask(r * k2 * rk_ref[...], bd_ref[...]) * v


def _rwkv_prep(x, batch, seq, params):
    t = batch * seq
    tm = min(TILE_ROWS, seq)
    bps = seq // tm
    n_tiles = t // tm
    width = params[0].shape[1]

    def full(a):
        return pl.BlockSpec(a.shape, lambda s: (0,) * a.ndim)

    out = lambda dtype: jax.ShapeDtypeStruct((t, RWKV_WIDTH), dtype)
    ospec = pl.BlockSpec((tm, RWKV_WIDTH), lambda s: (jnp.maximum(s - 1, 0), 0))
    out_dtypes = (F32, F32, BF16, F32, F32, F32, F32, F32)
    return pl.pallas_call(
        functools.partial(_rwkv_prep_kernel, blocks_per_seq=bps),
        out_shape=[out(dt) for dt in out_dtypes],
        grid=(n_tiles + 1,),
        in_specs=[pl.BlockSpec((tm, D_MODEL), lambda s: (jnp.minimum(s, n_tiles - 1), 0))]
        + [full(a) for a in params],
        out_specs=[ospec] * 8,
        scratch_shapes=[pltpu.VMEM((tm, width), F32), pltpu.VMEM((tm, width), F32),
                        pltpu.VMEM((SUBLANES, width), F32)],
        compiler_params=_cparams("arbitrary"),
    )(x, *params)


PAIR_W = 2 * RWKV_HEAD
N_PAIRS = RWKV_HEADS // 2
HEAD_SHIFT = RWKV_HEAD.bit_length() - 1


def _pair_masks():
    t = lax.broadcasted_iota(I32, (CHUNK, PAIR_W), 0)
    lane = lax.broadcasted_iota(I32, (CHUNK, PAIR_W), 1)
    j = lane & (RWKV_HEAD - 1)
    r = lax.broadcasted_iota(I32, (PAIR_W, PAIR_W), 0)
    c = lax.broadcasted_iota(I32, (PAIR_W, PAIR_W), 1)
    r4 = lax.broadcasted_iota(I32, (2 * PAIR_W, PAIR_W), 0)
    c4 = lax.broadcasted_iota(I32, (2 * PAIR_W, PAIR_W), 1)
    return dict(
        strict=t > j, incl=t >= j, eye=(t == j).astype(F32),
        same_block=jnp.bitwise_xor(t, j) < INV_BLOCK,
        first_head=lane < RWKV_HEAD,
        block_diag=(r >> HEAD_SHIFT) == (c >> HEAD_SHIFT),
        stacked_diag=((r4 >> HEAD_SHIFT) & 1) == (c4 >> HEAD_SHIFT))


N_SLOT_REFS = 5


def _rwkv_scan_kernel(*refs, chunks, blocks_per_seq):
    *io_refs, st_ref = refs[:-2 * N_SLOT_REFS]
    slot_a, slot_b = refs[-2 * N_SLOT_REFS:-N_SLOT_REFS], refs[-N_SLOT_REFS:]
    s = pl.program_id(0)

    @pl.when(s == 0)
    def _():
        for ref in slot_b:
            ref[...] = jnp.zeros_like(ref)

    @pl.when((s == 0) | (((s + blocks_per_seq - 1) % blocks_per_seq) == 0))
    def _():
        st_ref[...] = jnp.zeros_like(st_ref)

    body = functools.partial(_rwkv_scan_body, chunks, (*io_refs, st_ref))

    @pl.when((s & 1) == 0)
    def _():
        body(slot_a, slot_b)

    @pl.when((s & 1) == 1)
    def _():
        body(slot_b, slot_a)


def _rwkv_scan_body(chunks, refs, wr, rd):
    r_ref, k_ref, v_ref, lw_ref, kk_ref, kka_ref, bonus_ref, g_ref, lg_ref, lb_ref, o_ref, st_ref = refs
    yl_w, rm_w, gm_w, qm_w, gam_w = wr
    yl_r, rm_r, gm_r, qm_r, gam_r = rd
    row, col = _tri_masks(CHUNK)
    ltri = (row >= col).astype(BF16)
    m = _pair_masks()
    zero = jnp.zeros((), BF16)
    pairs = range(N_PAIRS)
    sl = [slice(p * PAIR_W, (p + 1) * PAIR_W) for p in pairs]

    def block_diag(x):
        xb = x.astype(BF16)
        return jnp.where(m["block_diag"], jnp.concatenate([xb, xb], axis=0), zero)

    def pair_mul(lhs, *rhs):
        rb = jnp.concatenate([block_diag(x) for x in rhs], axis=1) if len(rhs) > 1 else block_diag(rhs[0])
        out = jnp.dot(lhs.astype(BF16), rb, preferred_element_type=F32)
        return [out[:, i * PAIR_W:(i + 1) * PAIR_W] for i in range(len(rhs))]

    def each(fn, *lists):
        return [fn(*args) for args in zip(*lists)]

    carried = {"state": [st_ref[p] for p in pairs]}
    pending = iter(range(chunks))
    inv_n = 1.0 / RWKV_HEAD

    def head_stat(z):
        s0 = jnp.sum(jnp.where(m["first_head"], z, 0.0), axis=-1, keepdims=True)
        s1 = jnp.sum(jnp.where(m["first_head"], 0.0, z), axis=-1, keepdims=True)
        return jnp.where(m["first_head"], s0, s1) * inv_n

    def previous_tile_chunk():
        c = next(pending, None)
        if c is None:
            return
        rows = pl.ds(c * CHUNK, CHUNK)
        items = range(c * N_PAIRS, (c + 1) * N_PAIRS)
        state = carried["state"]
        y = [yl_r[i] + _bdot_nt(rm_r[i], st) for i, st in zip(items, state)]
        carried["state"] = [st * gam_r[i, 0:1, :] + _bdot_nt(st, gm_r[i]) + qm_r[i] for i, st in zip(items, state)]
        outs = []
        for yp in y:
            yc = yp - head_stat(yp)
            outs.append(yc * lax.rsqrt(head_stat(yc * yc) + RWKV_LNX_EPS))
        yn = jnp.concatenate(outs, axis=1) * lg_ref[...] + lb_ref[...]
        o_ref[rows, :] = ((yn + bonus_ref[rows, :]) * g_ref[rows, :]).astype(o_ref.dtype)

    lhs, rhs, vp, ends, gammas = [], [], [], [], []
    for c in range(chunks):
        rows = pl.ds(c * CHUNK, CHUNK)
        lw = lw_ref[rows, :]
        cum = _mask_dot(ltri, lw)
        cum_end = cum[CHUNK - 1:CHUNK, :]
        e_inv = jnp.exp(-cum)
        e_end = jnp.exp(cum_end - cum)
        kka = kka_ref[rows, :]
        k = k_ref[rows, :]
        a_bar = (-kk_ref[rows, :] * jnp.exp(cum - lw)).astype(BF16)
        r_bar = (r_ref[rows, :] * jnp.exp(cum)).astype(BF16)
        b_til = (kka * e_inv).astype(BF16)
        k_til = (k * e_inv).astype(BF16)
        b_end = (kka * e_end).astype(BF16)
        k_end = (k * e_end).astype(BF16)
        gamma = jnp.exp(cum_end)
        v = v_ref[rows, :].astype(BF16)

        for s in sl:
            lhs.append(jnp.concatenate([a_bar[:, s], r_bar[:, s]], axis=0))
            rhs.append(jnp.where(m["stacked_diag"], jnp.concatenate(
                [b_til[:, s], b_til[:, s], k_til[:, s], k_til[:, s]], axis=0), zero))
            vp.append(v[:, s])
            ends.append(jnp.concatenate([b_end[:, s], k_end[:, s]], axis=0))
            gammas.append(gamma[:, s])

    blocks = each(_bdot_nt, lhs, rhs)
    previous_tile_chunk()
    a_ab = [jnp.where(m["strict"], b[:CHUNK, :PAIR_W], 0.0) for b in blocks]
    a_ak = [jnp.where(m["strict"], b[:CHUNK, PAIR_W:], 0.0) for b in blocks]
    r_ab = [jnp.where(m["incl"], b[CHUNK:, :PAIR_W], 0.0) for b in blocks]
    r_ak = [jnp.where(m["incl"], b[CHUNK:, PAIR_W:], 0.0) for b in blocks]
    from_v = [jnp.dot(jnp.concatenate([ak, rk], axis=0).astype(BF16), block_diag(x),
                      preferred_element_type=F32) for ak, rk, x in zip(a_ak, r_ak, vp)]
    previous_tile_chunk()
    d = [jnp.where(m["same_block"], a, 0.0) for a in a_ab]
    low = [a - di for a, di in zip(a_ab, d)]
    pm = [m["eye"] + di for di in d]
    dpow = [pair_mul(di, di)[0] for di in d]
    previous_tile_chunk()
    for _ in range(INV_BLOCK.bit_length() - 3):
        both = each(lambda x, p: pair_mul(x, x, p), dpow, pm)
        dpow = [b[0] for b in both]
        pm = [p + b[1] for p, b in zip(pm, both)]
        previous_tile_chunk()
    pm = [p + pair_mul(x, p)[0] for p, x in zip(pm, dpow)]
    previous_tile_chunk()
    both = each(lambda p, lo, fv, l: pair_mul(p, lo, fv[:CHUNK], l[:CHUNK]), pm, low, from_v, lhs)
    previous_tile_chunk()
    both2 = each(lambda b: pair_mul(b[0], b[0], b[1], b[2]), both)
    previous_tile_chunk()
    xw_v = [b[1] + b2[1] for b, b2 in zip(both, both2)]
    xw_a = [b[2] + b2[2] for b, b2 in zip(both, both2)]
    both3 = each(lambda b2, xv, xa: pair_mul(b2[0], xv, xa), both2, xw_v, xw_a)
    u_v = [xv + b3[0] for xv, b3 in zip(xw_v, both3)]
    wm = [xa + b3[1] for xa, b3 in zip(xw_a, both3)]
    both4 = each(lambda rb, uv, w: pair_mul(rb, uv, w), r_ab, u_v, wm)
    y_local = [fv[CHUNK:] + b4[0] for fv, b4 in zip(from_v, both4)]
    rm = [l[CHUNK:] + b4[1] for l, b4 in zip(lhs, both4)]
    gm = [jnp.where(m["block_diag"], _bdot_tn(en[:CHUNK], w), 0.0) for en, w in zip(ends, wm)]
    qm = [jnp.where(m["block_diag"], _bdot_tn(jnp.concatenate([uv.astype(BF16), x], axis=0), en), 0.0)
          for uv, x, en in zip(u_v, vp, ends)]
    for _ in range(chunks):
        previous_tile_chunk()
    for p in pairs:
        st_ref[p] = carried["state"][p]

    for i in range(chunks * N_PAIRS):
        yl_w[i] = y_local[i]
        rm_w[i] = rm[i]
        gm_w[i] = gm[i]
        qm_w[i] = qm[i]
        gam_w[i, 0:1, :] = gammas[i]


def _rwkv_scan(prep, batch, seq, lnx_g, lnx_b):
    t = batch * seq
    chunks = CHUNKS_PER_STEP
    rows = CHUNK * chunks
    nblk = seq // rows
    n_tiles = batch * nblk
    n_items = chunks * N_PAIRS
    cur = pl.BlockSpec((rows, RWKV_WIDTH), lambda s: (jnp.minimum(s, n_tiles - 1), 0))
    prev = pl.BlockSpec((rows, RWKV_WIDTH), lambda s: (jnp.maximum(s - 1, 0), 0))
    pspec = pl.BlockSpec((1, RWKV_WIDTH), lambda s: (0, 0))
    slot = [pltpu.VMEM((n_items, CHUNK, PAIR_W), F32), pltpu.VMEM((n_items, CHUNK, PAIR_W), F32),
            pltpu.VMEM((n_items, PAIR_W, PAIR_W), F32), pltpu.VMEM((n_items, PAIR_W, PAIR_W), F32),
            pltpu.VMEM((n_items, SUBLANES, PAIR_W), F32)]
    return pl.pallas_call(
        functools.partial(_rwkv_scan_kernel, chunks=chunks, blocks_per_seq=nblk),
        out_shape=jax.ShapeDtypeStruct((t, RWKV_WIDTH), BF16),
        grid=(n_tiles + 1,),
        in_specs=[cur] * 6 + [prev, prev, pspec, pspec],
        out_specs=prev,
        scratch_shapes=[pltpu.VMEM((N_PAIRS, PAIR_W, PAIR_W), F32)] + slot + slot,
        compiler_params=_cparams("arbitrary"),
    )(*prep, lnx_g, lnx_b)


def _merge_kernel(og_ref, ol_ref, or_ref, x_ref, wgt_ref, bgt_ref, pg_ref, pl_ref, pr_ref, wo_ref,
                  lg_ref, lb_ref, xo_ref):
    d = D_MODEL
    xin = x_ref[...].astype(BF16)
    merged = None
    for b, (o_ref, p_ref) in enumerate(((og_ref, pg_ref), (ol_ref, pl_ref), (or_ref, pr_ref))):
        cols = slice(b * d, (b + 1) * d)
        gate = _sigmoid(jnp.dot(xin, wgt_ref[:, cols], preferred_element_type=F32) + bgt_ref[:, cols])
        term = gate * jnp.dot(o_ref[...], p_ref[...], preferred_element_type=F32)
        merged = term if merged is None else merged + term
    z = DEEPNORM_ALPHA * x_ref[...] + _bdot(merged, wo_ref[...])
    out = _layer_norm_rows(z, lg_ref[...], lb_ref[...], LN_EPS)
    xo_ref[...] = out


def _merge(o_gla, o_lru, o_rwkv, x, w_gates, b_gates, p_gla, p_lru, p_rwkv, w_out, ln_g, ln_b):
    t = x.shape[0]
    tm = min(WIDE_TILE_ROWS, t)

    def rows(width):
        return pl.BlockSpec((tm, width), lambda i: (i, 0))

    def full(a):
        return pl.BlockSpec(a.shape, lambda i: (0,) * a.ndim, pipeline_mode=pl.Buffered(1))

    params = (w_gates, b_gates, p_gla, p_lru, p_rwkv, w_out, ln_g, ln_b)
    return pl.pallas_call(
        _merge_kernel,
        out_shape=jax.ShapeDtypeStruct((t, D_MODEL), F32),
        grid=(t // tm,),
        in_specs=[rows(GLA_VW), rows(LRU_WIDTH), rows(RWKV_WIDTH), rows(D_MODEL)] + [full(a) for a in params],
        out_specs=rows(D_MODEL),
        compiler_params=_cparams("parallel"),
    )(o_gla, o_lru, o_rwkv, x, *params)


FFN_SPLIT = 2


def _ffn_kernel(x_ref, wg_ref, wu_ref, wd_ref, lg_ref, lb_ref, xo_ref):
    x = x_ref[...]
    xb = x.astype(BF16)
    tf = wg_ref.shape[1] // FFN_SPLIT
    acc = None
    for f in range(FFN_SPLIT):
        cols = slice(f * tf, (f + 1) * tf)
        g = jnp.dot(xb, wg_ref[:, cols], preferred_element_type=F32)
        u = jnp.dot(xb, wu_ref[:, cols], preferred_element_type=F32)
        part = _bdot(g * _sigmoid(g) * u, wd_ref[cols, :])
        acc = part if acc is None else acc + part
    z = DEEPNORM_ALPHA * x + acc
    xo_ref[...] = _layer_norm_rows(z, lg_ref[...], lb_ref[...], LN_EPS)


def _dense_ffn(x, wg, wu, wd, ln_g, ln_b):
    t = x.shape[0]
    tm = min(WIDE_TILE_ROWS, t)
    rows = pl.BlockSpec((tm, D_MODEL), lambda i: (i, 0))
    resident = lambda a: pl.BlockSpec(a.shape, lambda i: (0,) * a.ndim, pipeline_mode=pl.Buffered(1))
    return pl.pallas_call(
        _ffn_kernel,
        out_shape=jax.ShapeDtypeStruct((t, D_MODEL), F32),
        grid=(t // tm,),
        in_specs=[rows] + [resident(a) for a in (wg, wu, wd, ln_g, ln_b)],
        out_specs=rows,
        compiler_params=_cparams("parallel"),
    )(x, wg, wu, wd, ln_g, ln_b)


ROW_TILE = (D_MODEL // LANES, LANES)
assert ROW_TILE[0] == SUBLANES


def _rows_to_tiles(x, ref3):
    ref3[...] = x.reshape(x.shape[0], *ROW_TILE)


def _tiles_to_rows(ref3):
    return ref3[...].reshape(ref3.shape[0], D_MODEL)


def _router_kernel(x_ref, wr_ref, meta_ref, wts_ref, cnt_ref, carry_ref):
    tm = x_ref.shape[0]

    @pl.when(pl.program_id(0) == 0)
    def _():
        carry_ref[...] = jnp.zeros_like(carry_ref)

    lane = lax.broadcasted_iota(I32, (tm, LANES), 1)
    neg = jnp.float32(-jnp.inf)
    logits = jnp.where(lane < N_EXPERTS, _fdot(x_ref[...], wr_ref[...]), neg)
    m1 = jnp.max(logits, axis=-1, keepdims=True)
    e1 = jnp.min(jnp.where(logits == m1, lane, LANES), axis=-1, keepdims=True)
    rest = jnp.where(lane == e1, neg, logits)
    m2 = jnp.max(rest, axis=-1, keepdims=True)
    e2 = jnp.min(jnp.where(rest == m2, lane, LANES), axis=-1, keepdims=True)
    ex = jnp.exp(m2 - m1)
    w1 = 1.0 / (1.0 + ex)
    w2 = ex / (1.0 + ex)

    hot1 = lane == e1
    hot2 = lane == e2
    onehot = jnp.where(hot1 | hot2, 1.0, 0.0)
    row, col = _tri_masks(tm)
    before = (row > col).astype(BF16)
    prefix = jnp.dot(before, onehot.astype(BF16), preferred_element_type=F32) + carry_ref[...]
    rank1 = jnp.sum(jnp.where(hot1, prefix, 0.0), axis=-1, keepdims=True).astype(I32)
    rank2 = jnp.sum(jnp.where(hot2, prefix, 0.0), axis=-1, keepdims=True).astype(I32)
    carry_ref[...] += jnp.sum(onehot, axis=0, keepdims=True)
    cnt_ref[...] = carry_ref[...]

    meta = jnp.where(lane == 0, e1, jnp.where(lane == 1, e2, jnp.where(lane == 2, rank1, rank2)))
    meta_ref[...] = jnp.where(lane < 4, meta, 0)
    wts_ref[...] = jnp.where(lane == 0, w1, jnp.where(lane == 1, w2, 0.0))


def _router(x, wr_pad):
    t = x.shape[0]
    tm = min(TILE_ROWS, t)
    return pl.pallas_call(
        _router_kernel,
        out_shape=[jax.ShapeDtypeStruct((t, LANES), I32), jax.ShapeDtypeStruct((t, LANES), F32),
                   jax.ShapeDtypeStruct((1, LANES), F32)],
        grid=(t // tm,),
        in_specs=[pl.BlockSpec((tm, D_MODEL), lambda i: (i, 0)),
                  pl.BlockSpec((D_MODEL, LANES), lambda i: (0, 0))],
        out_specs=[pl.BlockSpec((tm, LANES), lambda i: (i, 0)), pl.BlockSpec((tm, LANES), lambda i: (i, 0)),
                   pl.BlockSpec((1, LANES), lambda i: (0, 0))],
        scratch_shapes=[pltpu.VMEM((1, LANES), F32)],
        compiler_params=_cparams("arbitrary"),
    )(x, wr_pad)


def _dispatch_kernel(pad_start_ref, pad_len_ref, nused_ref, dest_ref, x_ref, *rest):
    n_w = (len(rest) - 5) // 2
    w_refs, xs_ref, wb_refs = rest[:n_w], rest[n_w], rest[n_w + 1:2 * n_w + 1]
    x3_ref, zeros_ref, sem, fill_sem = rest[-4:]
    tm = x_ref.shape[0]
    n_blocks = xs_ref.shape[0] // MOE_ROWS
    _rows_to_tiles(x_ref[...], x3_ref)

    @pl.when(pl.program_id(0) == pl.num_programs(0) - 1)
    def _():
        zeros_ref[...] = jnp.zeros_like(zeros_ref)
        pad_copy = lambda row: pltpu.make_async_copy(zeros_ref.at[0], xs_ref.at[row], fill_sem)
        blk_copy = lambda blk: pltpu.make_async_copy(
            zeros_ref, xs_ref.at[pl.ds(pl.multiple_of(blk * MOE_ROWS, MOE_ROWS), MOE_ROWS)], fill_sem)

        def for_each(fn):
            for e in range(N_EXPERTS):
                base = pad_start_ref[e]
                lax.fori_loop(0, pad_len_ref[e], lambda r, c: (fn(pad_copy(base + r)), c)[1], 0)
            lax.fori_loop(nused_ref[0], n_blocks, lambda b, c: (fn(blk_copy(b)), c)[1], 0)

        for_each(lambda cp: cp.start())
        for_each(lambda cp: cp.wait())

    def row_copy(r, j):
        return pltpu.make_async_copy(x3_ref.at[r], xs_ref.at[dest_ref[TOP_K * r + j]], sem)

    def start(r, c):
        for j in range(TOP_K):
            row_copy(r, j).start(priority=j)
        return c

    lax.fori_loop(0, tm, start, 0, unroll=8)
    for w_ref, wb_ref in zip(w_refs, wb_refs):
        wb_ref[...] = w_ref[...].astype(wb_ref.dtype)
    for j in range(TOP_K):
        pltpu.make_async_copy(x3_ref, xs_ref.at[pl.ds(0, tm)], sem).wait()


def _dispatch(x, dest_flat, pad_start, pad_len, n_used, n_slots, weights):
    t = x.shape[0]
    tm = min(WIDE_TILE_ROWS, t)
    n_steps = t // tm
    flat = [w.reshape(-1, w.shape[-1]) for w in weights]
    slab = lambda w: pl.BlockSpec((w.shape[0] // n_steps, w.shape[1]), lambda i, *_: (i, 0))
    grid_spec = pltpu.PrefetchScalarGridSpec(
        num_scalar_prefetch=3,
        grid=(n_steps,),
        in_specs=[pl.BlockSpec((tm * TOP_K,), lambda i, *_: (i,), memory_space=pltpu.SMEM),
                  pl.BlockSpec((tm, D_MODEL), lambda i, *_: (i, 0))] + [slab(w) for w in flat],
        out_specs=[pl.BlockSpec(memory_space=pl.ANY)] + [slab(w) for w in flat],
        scratch_shapes=[pltpu.VMEM((tm, *ROW_TILE), F32), pltpu.VMEM((MOE_ROWS, *ROW_TILE), F32),
                        pltpu.SemaphoreType.DMA(()), pltpu.SemaphoreType.DMA(())],
    )
    xs, *cast = pl.pallas_call(
        _dispatch_kernel,
        out_shape=[jax.ShapeDtypeStruct((n_slots, *ROW_TILE), F32)]
        + [jax.ShapeDtypeStruct(w.shape, BF16) for w in flat],
        grid_spec=grid_spec,
        compiler_params=_cparams("arbitrary"),
    )(pad_start, pad_len, n_used, dest_flat, x, *flat)
    return xs, [c.reshape(w.shape) for c, w in zip(cast, weights)]


def _expert_kernel(blk_e_ref, nused_ref, xs_ref, wg_ref, wu_ref, wd_ref, ys_ref, acc_ref, xb_ref, *, n_f):
    del blk_e_ref
    i = pl.program_id(0)
    f = pl.program_id(1)
    last = n_f - 1
    used = i < nused_ref[0]

    @pl.when(used & (f == 0))
    def _():
        xb_ref[...] = _tiles_to_rows(xs_ref).astype(BF16)

    def piece():
        xb = xb_ref[...]
        g = jnp.dot(xb, wg_ref[0], preferred_element_type=F32)
        u = jnp.dot(xb, wu_ref[0], preferred_element_type=F32)
        return _bdot(g * _sigmoid(g) * u, wd_ref[0])

    if n_f == 1:
        @pl.when(used)
        def _():
            _rows_to_tiles(piece(), ys_ref)
    else:
        @pl.when(used & (f == 0))
        def _():
            acc_ref[...] = piece()

        if n_f > 2:
            @pl.when(used & (f != 0) & (f != last))
            def _():
                acc_ref[...] += piece()

        @pl.when(used & (f == last))
        def _():
            _rows_to_tiles(acc_ref[...] + piece(), ys_ref)

    @pl.when(jnp.logical_not(used) & (f == last))
    def _():
        ys_ref[...] = jnp.zeros_like(ys_ref)


def _experts(xs, blk_e, n_used, wg, wu, wd):
    n_slots = xs.shape[0]
    n_blocks = n_slots // MOE_ROWS
    ff = wg.shape[2]
    n_f = 2
    tf = ff // n_f
    piece = lambda i, f: jnp.where((i & 1) == 1, n_f - 1 - f, f)
    grid_spec = pltpu.PrefetchScalarGridSpec(
        num_scalar_prefetch=2,
        grid=(n_blocks, n_f),
        in_specs=[pl.BlockSpec((MOE_ROWS, *ROW_TILE), lambda i, f, be, nu: (i, 0, 0)),
                  pl.BlockSpec((1, D_MODEL, tf), lambda i, f, be, nu: (be[i], 0, piece(i, f))),
                  pl.BlockSpec((1, D_MODEL, tf), lambda i, f, be, nu: (be[i], 0, piece(i, f))),
                  pl.BlockSpec((1, tf, D_MODEL), lambda i, f, be, nu: (be[i], piece(i, f), 0))],
        out_specs=pl.BlockSpec((MOE_ROWS, *ROW_TILE), lambda i, f, be, nu: (i, 0, 0)),
        scratch_shapes=[pltpu.VMEM((MOE_ROWS, D_MODEL), F32), pltpu.VMEM((MOE_ROWS, D_MODEL), BF16)],
    )
    return pl.pallas_call(
        functools.partial(_expert_kernel, n_f=n_f),
        out_shape=jax.ShapeDtypeStruct((n_slots, *ROW_TILE), F32),
        grid_spec=grid_spec,
        compiler_params=_cparams("arbitrary", "arbitrary"),
    )(blk_e, n_used, xs, wg, wu, wd)


def _combine_kernel(dest_ref, dest_next_ref, x_ref, wts_ref, ys_ref, lg_ref, lb_ref, xo_ref, buf_ref, sem):
    tm = x_ref.shape[0]
    i = pl.program_id(0)
    n = pl.num_programs(0)

    def gather(idx_ref, slot):
        def start(r, c):
            for j in range(TOP_K):
                pltpu.make_async_copy(ys_ref.at[idx_ref[TOP_K * r + j]],
                                      buf_ref.at[slot, j, r], sem.at[slot]).start(priority=j)
            return c
        lax.fori_loop(0, tm, start, 0, unroll=8)

    slot = i & 1

    @pl.when(i == 0)
    def _():
        gather(dest_ref, 0)

    @pl.when(i + 1 < n)
    def _():
        gather(dest_next_ref, 1 - slot)

    for j in range(TOP_K):
        pltpu.make_async_copy(ys_ref.at[pl.ds(0, tm)], buf_ref.at[slot, j], sem.at[slot]).wait()
    w = wts_ref[...]
    f = w[:, 0:1] * _tiles_to_rows(buf_ref.at[slot, 0]) + w[:, 1:2] * _tiles_to_rows(buf_ref.at[slot, 1])
    z = DEEPNORM_ALPHA * x_ref[...] + f
    xo_ref[...] = _layer_norm_rows(z, lg_ref[...], lb_ref[...], LN_EPS)


def _combine(x, dest_flat, wts, ys, ln_g, ln_b):
    t = x.shape[0]
    tm = min(TILE_ROWS, t)
    n_tiles = t // tm
    rows = lambda width: pl.BlockSpec((tm, width), lambda i: (i, 0))
    vec = pl.BlockSpec((1, D_MODEL), lambda i: (0, 0))
    return pl.pallas_call(
        _combine_kernel,
        out_shape=jax.ShapeDtypeStruct((t, D_MODEL), F32),
        grid=(n_tiles,),
        in_specs=[pl.BlockSpec((tm * TOP_K,), lambda i: (i,), memory_space=pltpu.SMEM),
                  pl.BlockSpec((tm * TOP_K,), lambda i: (jnp.minimum(i + 1, n_tiles - 1),),
                               memory_space=pltpu.SMEM),
                  rows(D_MODEL), rows(LANES), pl.BlockSpec(memory_space=pl.ANY), vec, vec],
        out_specs=rows(D_MODEL),
        scratch_shapes=[pltpu.VMEM((2, TOP_K, tm, *ROW_TILE), F32), pltpu.SemaphoreType.DMA((2,))],
        compiler_params=_cparams("arbitrary"),
    )(dest_flat, dest_flat, x, wts, ys, ln_g, ln_b)


def _moe_ffn(x, w_router, wg, wu, wd, ln_g, ln_b):
    t = x.shape[0]
    wr_pad = jnp.pad(w_router, ((0, 0), (0, LANES - N_EXPERTS)))
    meta, wts, cnt = _router(x, wr_pad)
    counts = cnt[0, :N_EXPERTS].astype(I32)
    padded = (counts + MOE_ROWS - 1) // MOE_ROWS * MOE_ROWS
    seg_end = jnp.cumsum(padded)
    seg_start = seg_end - padded
    n_blocks = (t * TOP_K) // MOE_ROWS + N_EXPERTS
    dest = seg_start[meta[:, 0:TOP_K]] + meta[:, TOP_K:2 * TOP_K]
    dest_flat = dest.reshape(-1).astype(I32)
    blk_start = jnp.arange(n_blocks, dtype=I32) * MOE_ROWS
    blk_e = jnp.minimum(jnp.sum(blk_start[:, None] >= seg_end[None, :], axis=1), N_EXPERTS - 1).astype(I32)
    n_used = (seg_end[-1:] // MOE_ROWS).astype(I32)
    xs, (wg, wu, wd) = _dispatch(x, dest_flat, (seg_start + counts).astype(I32), (padded - counts).astype(I32),
                                 n_used, n_blocks * MOE_ROWS, (wg, wu, wd))
    ys = _experts(xs, blk_e, n_used, wg, wu, wd)
    return _combine(x, dest_flat, wts, ys, ln_g, ln_b)


def _pad_rows(w, rows, at=0):
    out = jnp.zeros((rows, w.shape[1]), w.dtype)
    return out.at[at:at + w.shape[0]].set(w)


def _reorder_in_projection(w, b):
    sizes = (GLA_KW, GLA_KW, GLA_VW, GLA_VW, GLA_DECAY_RANK, LRU_WIDTH, LRU_WIDTH,
             3 * RWKV_WIDTH, RWKV_DECAY_RANK, RWKV_A_RANK, RWKV_GATE_RANK, N_BRANCH * D_MODEL)
    offs = [0]
    for s in sizes:
        offs.append(offs[-1] + s)
    wb = jnp.concatenate([w, b[None, :]], axis=0)
    piece = lambda i: wb[:, offs[i]:offs[i + 1]]
    zeros = lambda n: jnp.zeros((wb.shape[0], n), wb.dtype)
    q, k, v, r, dec, lx, lg, rkv, wl, al, gl, gates = (piece(i) for i in range(len(sizes)))
    small = jnp.concatenate([wl, al, gl, zeros(RW_SMALL_W - SM_GL[0] - RWKV_GATE_RANK)], axis=1)
    groups = {"gla": jnp.concatenate([q, k, v, r, dec, zeros(GLA_DEC_W - GLA_DECAY_RANK)], axis=1),
              "lru": jnp.concatenate([lx, lg], axis=1),
              "rwkv": jnp.concatenate([rkv, small], axis=1),
              "gates": gates}
    return {name: (g[:-1].astype(BF16), g[-1:]) for name, g in groups.items()}


def _block_diag(blocks):
    n, bi, bo = blocks.shape
    eye = jnp.eye(n, dtype=blocks.dtype)
    return (eye[:, None, :, None] * blocks[:, :, None, :]).reshape(n * bi, n * bo)


def kernel(x, w_in, b_in, gla_w_decay_up, gla_b_decay, gla_norm_g, gla_norm_b, lru_conv_w, lru_conv_b, lru_w_r, lru_b_r, lru_w_i, lru_b_i, lru_lambda, rwkv_mu, rwkv_w0, rwkv_w2, rwkv_a0, rwkv_a2, rwkv_g2, rwkv_k_k, rwkv_k_a, rwkv_r_k, rwkv_lnx_g, rwkv_lnx_b, p_gla, p_lru, p_rwkv, w_out, ln_mix_g, ln_mix_b, ffn_w_gate, ffn_w_up, ffn_w_down, moe_w_router, moe_w_gate, moe_w_up, moe_w_down, ln_ffn_g, ln_ffn_b):
    batch, seq, d = x.shape
    t = batch * seq
    xf = x.reshape(t, d)
    row = lambda a: a.reshape(1, -1)
    head_ones = _block_diag(jnp.ones((RWKV_HEADS, RWKV_HEAD, RWKV_HEAD), BF16))
    for l in range(DEPTH):
        proj = _reorder_in_projection(w_in[l], b_in[l])
        w_gates, b_gates = proj["gates"]

        wup = _pad_rows(gla_w_decay_up[l], GLA_DEC_W).astype(BF16)
        o_gla = _gla_branch(xf, batch, seq, *proj["gla"], wup, row(gla_b_decay[l]), row(gla_norm_g[l]),
                            row(gla_norm_b[l]))

        w_ri = jnp.concatenate([_block_diag(lru_w_r[l]), _block_diag(lru_w_i[l])], axis=1).astype(BF16)
        b_ri = jnp.concatenate([lru_b_r[l], lru_b_i[l]])[None, :]
        o_lru = _lru_branch(xf, batch, seq, *proj["lru"], lru_conv_w[l], row(lru_conv_b[l]), w_ri, b_ri,
                            row(jax.nn.softplus(-lru_lambda[l])))

        mu = rwkv_mu[l]
        mu_all = jnp.zeros((3 * RWKV_WIDTH + RW_SMALL_W,), F32).at[0:mu.shape[0]].set(mu)
        prep_params = (*proj["rwkv"], row(mu_all),
                       _pad_rows(rwkv_w2[l], SM_WA[1], 0).astype(BF16),
                       _pad_rows(rwkv_a2[l], SM_WA[1], RWKV_DECAY_RANK).astype(BF16),
                       _pad_rows(rwkv_g2[l], SM_GL[1], 0).astype(BF16),
                       row(rwkv_w0[l]), row(rwkv_a0[l]), row(rwkv_k_k[l]), row(rwkv_k_a[l]),
                       row(rwkv_r_k[l]), head_ones)
        prep = _rwkv_prep(xf, batch, seq, prep_params)
        o_rwkv = _rwkv_scan(prep, batch, seq, row(rwkv_lnx_g[l]), row(rwkv_lnx_b[l]))

        xf = _merge(o_gla, o_lru, o_rwkv, xf, w_gates, b_gates, p_gla[l].astype(BF16), p_lru[l].astype(BF16),
                    p_rwkv[l].astype(BF16), w_out[l].astype(BF16), row(ln_mix_g[l]), row(ln_mix_b[l]))
        i = l // 2
        if l % 2 == 0:
            xf = _dense_ffn(xf, ffn_w_gate[i].astype(BF16), ffn_w_up[i].astype(BF16),
                            ffn_w_down[i].astype(BF16), row(ln_ffn_g[l]), row(ln_ffn_b[l]))
        else:
            xf = _moe_ffn(xf, moe_w_router[i], moe_w_gate[i], moe_w_up[i], moe_w_down[i],
                          row(ln_ffn_g[l]), row(ln_ffn_b[l]))
    return xf.reshape(batch, seq, d)
```

```python
import functools

import jax
import jax.numpy as jnp
from jax import lax
from jax.experimental import pallas as pl
from jax.experimental.pallas import tpu as pltpu

F32 = jnp.float32
BF16 = jnp.bfloat16
I32 = jnp.int32

D_MODEL = 1024
DEPTH = 2
GLA_HEADS, GLA_DK, GLA_DV = 4, 64, 128
GLA_KW, GLA_VW = GLA_HEADS * GLA_DK, GLA_HEADS * GLA_DV
GLA_DECAY_RANK = 16
GLA_TEMP = 16.0
LRU_WIDTH, LRU_BLOCKS, LRU_CONV, LRU_C = 512, 8, 4, 8.0
RWKV_HEAD, RWKV_WIDTH = 64, 512
RWKV_HEADS = RWKV_WIDTH // RWKV_HEAD
RWKV_DECAY_RANK, RWKV_A_RANK, RWKV_GATE_RANK = 64, 64, 160
RWKV_LNX_EPS = 64e-5
N_BRANCH = 3
N_EXPERTS, TOP_K = 8, 2
DEEPNORM_ALPHA = (2 * DEPTH) ** 0.25
LN_EPS = 1e-5

LANES = 128
SUBLANES = 8
VMEM_LIMIT_BYTES = 56 * 1024 * 1024

SM_WA = (0, 128)
SM_GL = (128, 256)
RW_SMALL_W = 512
GLA_DEC_W = LANES
GLA_COLS = {"q": (0, 256), "k": (256, 512), "v": (512, 1024), "r": (1024, 1536), "dec": (1536, 1664)}

CHUNK = 64
CHUNKS_PER_STEP = 8
INV_BLOCK = 16
MOE_ROWS = 512
TILE_ROWS = 512
WIDE_TILE_ROWS = 1024
RWKV_DECAY_OFFSET = 0.5
KK_NORM_FLOOR = 1e-12


def _cparams(*sem):
    return pltpu.CompilerParams(dimension_semantics=sem, vmem_limit_bytes=VMEM_LIMIT_BYTES)


def _sigmoid(x):
    return 1.0 / (1.0 + jnp.exp(-x))


def _softplus(x):
    return jnp.maximum(x, 0.0) + jnp.log(1.0 + jnp.exp(-jnp.abs(x)))


def _bdot(a, b):
    return jnp.dot(a.astype(BF16), b.astype(BF16), preferred_element_type=F32)


def _bdot_nt(a, b):
    return lax.dot_general(a.astype(BF16), b.astype(BF16), (((1,), (1,)), ((), ())),
                           preferred_element_type=F32)


def _bdot_tn(a, b):
    return lax.dot_general(a.astype(BF16), b.astype(BF16), (((0,), (0,)), ((), ())),
                           preferred_element_type=F32)


def _fdot(a, b):
    ah = a.astype(BF16)
    al = (a - ah.astype(F32)).astype(BF16)
    bh = b.astype(BF16)
    bl = (b - bh.astype(F32)).astype(BF16)
    dot = lambda u, v: jnp.dot(u, v, preferred_element_type=F32)
    return dot(al, bh) + dot(ah, bl) + dot(ah, bh)


def _split3(x):
    hi = x.astype(BF16)
    rest = x - hi.astype(F32)
    mid = rest.astype(BF16)
    lo = (rest - mid.astype(F32)).astype(BF16)
    return hi, mid, lo


def _mask_dot(mask_bf16, x):
    hi, mid, lo = _split3(x)
    dot = lambda part: jnp.dot(mask_bf16, part, preferred_element_type=F32)
    return dot(lo) + dot(mid) + dot(hi)


def _dot_mask(x, mask_bf16):
    hi = x.astype(BF16)
    lo = (x - hi.astype(F32)).astype(BF16)
    dot = lambda part: jnp.dot(part, mask_bf16, preferred_element_type=F32)
    return dot(lo) + dot(hi)


def _layer_norm_rows(z, g, b, eps):
    mu = jnp.mean(z, axis=-1, keepdims=True)
    zc = z - mu
    var = jnp.mean(zc * zc, axis=-1, keepdims=True)
    return zc * lax.rsqrt(var + eps) * g + b


def _tri_masks(n):
    row = lax.broadcasted_iota(I32, (n, n), 0)
    col = lax.broadcasted_iota(I32, (n, n), 1)
    return row, col


MXU_COLS = 256


def _gla_kernel(*refs, chunks, blocks_per_seq):
    *io_refs, st_ref, ha_ref, hb_ref = refs
    s = pl.program_id(0)

    @pl.when(s == 0)
    def _():
        hb_ref[...] = jnp.zeros_like(hb_ref)

    @pl.when((s == 0) | (((s + blocks_per_seq - 1) % blocks_per_seq) == 0))
    def _():
        st_ref[...] = jnp.zeros_like(st_ref)

    body = functools.partial(_gla_body, chunks, (*io_refs, st_ref))

    @pl.when((s & 1) == 0)
    def _():
        body(ha_ref, hb_ref)

    @pl.when((s & 1) == 1)
    def _():
        body(hb_ref, ha_ref)


def _gla_body(chunks, refs, h_next, h_ref):
    x_ref, w_ref, b_ref, wup_ref, bdec_ref, ng_ref, nb_ref, o_ref, st_ref = refs
    xb = x_ref[...].astype(BF16)
    width = w_ref.shape[1]
    pieces = iter([slice(c0, min(c0 + MXU_COLS, width)) for c0 in range(0, width, MXU_COLS)])

    def project_piece():
        cols = next(pieces, None)
        if cols is not None:
            h_next[:, cols] = jnp.dot(xb, w_ref[:, cols], preferred_element_type=F32) + b_ref[:, cols]

    q_ref = h_ref.at[:, GLA_COLS["q"][0]:GLA_COLS["q"][1]]
    k_ref = h_ref.at[:, GLA_COLS["k"][0]:GLA_COLS["k"][1]]
    v_ref = h_ref.at[:, GLA_COLS["v"][0]:GLA_COLS["v"][1]]
    r_ref = h_ref.at[:, GLA_COLS["r"][0]:GLA_COLS["r"][1]]
    dec_ref = h_ref.at[:, GLA_COLS["dec"][0]:GLA_COLS["dec"][1]]

    row, col = _tri_masks(CHUNK)
    ltri = (row >= col).astype(BF16)
    kw, vw = 2 * GLA_DK, 2 * GLA_DV
    key_shift, val_shift = GLA_DK.bit_length() - 1, GLA_DV.bit_length() - 1
    iota = lambda shape, axis: lax.broadcasted_iota(I32, shape, axis)
    causal = iota((CHUNK, kw), 0) >= (iota((CHUNK, kw), 1) & (GLA_DK - 1))
    key_diag = (iota((kw, kw), 0) >> key_shift) == (iota((kw, kw), 1) >> key_shift)
    val_diag = (iota((kw, vw), 0) >> key_shift) == (iota((kw, vw), 1) >> val_shift)
    state_diag = (iota((vw, kw), 0) >> val_shift) == (iota((vw, kw), 1) >> key_shift)
    zero = jnp.zeros((), BF16)
    pairs = range(GLA_HEADS // 2)
    qp, vp, k_diag, kep, decays = [], [], [], [], []
    for c in range(chunks):
        project_piece()
        rows = pl.ds(c * CHUNK, CHUNK)
        logits = _bdot(dec_ref[rows, :], wup_ref[...]) + bdec_ref[...]
        log_a = -_softplus(-logits) * (1.0 / GLA_TEMP)
        cum = _mask_dot(ltri, log_a)
        cum_end = cum[CHUNK - 1:CHUNK, :]
        q = q_ref[rows, :] * (GLA_DK ** -0.5)
        k = k_ref[rows, :]
        q_dec = (q * jnp.exp(cum)).astype(BF16)
        k_inv = (k * jnp.exp(-cum)).astype(BF16)
        k_end = (k * jnp.exp(cum_end - cum)).astype(BF16)
        decay_end = jnp.exp(cum_end)
        v = v_ref[rows, :].astype(BF16)
        for p in pairs:
            s = slice(p * kw, (p + 1) * kw)
            qp.append(q_dec[:, s])
            vp.append(v[:, p * vw:(p + 1) * vw])
            k_diag.append(jnp.where(key_diag, jnp.concatenate([k_inv[:, s], k_inv[:, s]], axis=0), zero))
            kep.append(k_end[:, s])
            decays.append(decay_end[:, s])

    for _ in range(chunks):
        project_piece()

    v_diag = [jnp.where(val_diag, jnp.concatenate([x, x], axis=0), zero) for x in vp]
    scores = [jnp.where(causal, _bdot_nt(q, kd), 0.0) for q, kd in zip(qp, k_diag)]
    o_local = [_bdot(s, vd) for s, vd in zip(scores, v_diag)]
    upd = [jnp.where(state_diag, _bdot_tn(x, ke), 0.0) for x, ke in zip(vp, kep)]

    state = [st_ref[p] for p in pairs]
    n_pairs = len(pairs)
    for c in range(chunks):
        rows = pl.ds(c * CHUNK, CHUNK)
        item = slice(c * n_pairs, (c + 1) * n_pairs)
        o = [ol + _bdot_nt(q, st) for ol, q, st in zip(o_local[item], qp[item], state)]
        state = [st * dcy + u for st, dcy, u in zip(state, decays[item], upd[item])]
        outs = []
        for op in o:
            for oh in (op[:, :GLA_DV], op[:, GLA_DV:]):
                mu = jnp.mean(oh, axis=-1, keepdims=True)
                oc = oh - mu
                var = jnp.mean(oc * oc, axis=-1, keepdims=True)
                outs.append(oc * lax.rsqrt(var + LN_EPS))
        y = jnp.concatenate(outs, axis=1) * ng_ref[...] + nb_ref[...]
        r = r_ref[rows, :]
        o_ref[rows, :] = (y * (r * _sigmoid(r))).astype(o_ref.dtype)
    for p in pairs:
        st_ref[p] = state[p]


def _gla_branch(x, batch, seq, w, bias, wup_pad, b_decay, norm_g, norm_b):
    t = batch * seq
    chunks = CHUNKS_PER_STEP
    rows = CHUNK * chunks
    nblk = seq // rows

    n_tiles = batch * nblk

    def full(a):
        return pl.BlockSpec(a.shape, lambda s: (0,) * a.ndim)

    args = (w, bias, wup_pad, b_decay, norm_g, norm_b)
    return pl.pallas_call(
        functools.partial(_gla_kernel, chunks=chunks, blocks_per_seq=nblk),
        out_shape=jax.ShapeDtypeStruct((t, GLA_VW), BF16),
        grid=(n_tiles + 1,),
        in_specs=[pl.BlockSpec((rows, D_MODEL), lambda s: (jnp.minimum(s, n_tiles - 1), 0))]
        + [full(a) for a in args],
        out_specs=pl.BlockSpec((rows, GLA_VW), lambda s: (jnp.maximum(s - 1, 0), 0)),
        scratch_shapes=[pltpu.VMEM((GLA_HEADS // 2, 2 * GLA_DV, 2 * GLA_DK), F32),
                        pltpu.VMEM((rows, w.shape[1]), F32), pltpu.VMEM((rows, w.shape[1]), F32)],
        compiler_params=_cparams("arbitrary"),
    )(x, *args)


def _lru_kernel(xin_ref, w_ref, b_ref, cw_ref, cb_ref, wri_ref, bri_ref, sp_ref,
                o_ref, xbuf_ref, a_ref, u_ref, hs_ref, gate_ref, h_ref):
    tm = xin_ref.shape[0]
    pad = SUBLANES

    @pl.when(pl.program_id(1) == 0)
    def _():
        xbuf_ref[0:pad, :] = jnp.zeros((pad, LRU_WIDTH), F32)
        h_ref[...] = jnp.zeros_like(h_ref)

    proj = jnp.dot(xin_ref[...].astype(BF16), w_ref[...], preferred_element_type=F32) + b_ref[...]
    x = proj[:, :LRU_WIDTH]
    gate_ref[...] = jax.nn.gelu(proj[:, LRU_WIDTH:])
    xbuf_ref[pad:pad + tm, :] = x
    xc = cb_ref[...] + x * cw_ref[LRU_CONV - 1:LRU_CONV, :]
    for j in range(LRU_CONV - 1):
        back = LRU_CONV - 1 - j
        xc = xc + xbuf_ref[pl.ds(pad - back, tm), :] * cw_ref[j:j + 1, :]
    xbuf_ref[0:pad, :] = x[tm - pad:tm, :]

    ri = _sigmoid(_bdot(xc, wri_ref[...]) + bri_ref[...])
    rg, ig = ri[:, :LRU_WIDTH], ri[:, LRU_WIDTH:]
    log_a = -LRU_C * rg * sp_ref[...]
    a = jnp.exp(log_a)
    a_ref[...] = a
    u_ref[...] = jnp.sqrt(1.0 - a * a) * (ig * xc)

    def step(t, hprev):
        hnew = a_ref[pl.ds(t, 1), :] * hprev + u_ref[pl.ds(t, 1), :]
        hs_ref[pl.ds(t, 1), :] = hnew
        return hnew

    h_ref[...] = lax.fori_loop(0, tm, step, h_ref[...], unroll=8)
    o_ref[...] = (gate_ref[...] * hs_ref[...]).astype(o_ref.dtype)


def _lru_branch(x, batch, seq, w, bias, conv_w, conv_b, w_ri, b_ri, softplus_neg_lam):
    t = batch * seq
    tm = min(WIDE_TILE_ROWS, seq)
    nblk = seq // tm

    def full(a):
        return pl.BlockSpec(a.shape, lambda b, n: (0,) * a.ndim)

    args = (w, bias, conv_w, conv_b, w_ri, b_ri, softplus_neg_lam)
    return pl.pallas_call(
        _lru_kernel,
        out_shape=jax.ShapeDtypeStruct((t, LRU_WIDTH), BF16),
        grid=(batch, nblk),
        in_specs=[pl.BlockSpec((tm, D_MODEL), lambda b, n: (b * nblk + n, 0))] + [full(a) for a in args],
        out_specs=pl.BlockSpec((tm, LRU_WIDTH), lambda b, n: (b * nblk + n, 0)),
        scratch_shapes=[pltpu.VMEM((tm + SUBLANES, LRU_WIDTH), F32)]
        + [pltpu.VMEM((tm, LRU_WIDTH), F32)] * 4 + [pltpu.VMEM((1, LRU_WIDTH), F32)],
        compiler_params=_cparams("parallel", "arbitrary"),
    )(x, *args)


def _rwkv_prep_kernel(*refs, blocks_per_seq):
    *io_refs, ha_ref, hb_ref, carry_ref = refs
    tm = io_refs[0].shape[0]
    s = pl.program_id(0)
    body = functools.partial(_rwkv_prep_body, tm, s, RWKV_WIDTH, blocks_per_seq, (*io_refs, carry_ref))

    @pl.when(s == 0)
    def _():
        hb_ref[...] = jnp.zeros_like(hb_ref)
        carry_ref[...] = jnp.zeros_like(carry_ref)

    @pl.when((s & 1) == 0)
    def _():
        body(ha_ref, hb_ref)

    @pl.when((s & 1) == 1)
    def _():
        body(hb_ref, ha_ref)


def _rwkv_prep_body(tm, s, gw, blocks_per_seq, refs, h_next, h_cur):
    (x_ref, w_ref, b_ref, mu_ref, w2_ref, a2_ref, g2_ref, w0_ref, a0_ref, kk_ref, ka_ref, rk_ref, bd_ref,
     r_out, k_out, v_out, lw_out, kk_out, kka_out, bonus_out, g_out, carry_ref) = refs
    first = ((s + blocks_per_seq - 1) % blocks_per_seq) == 0
    keep = jnp.where(first, 0.0, 1.0)
    row0 = lax.broadcasted_iota(I32, (tm, 1), 0) == 0
    xb = x_ref[...].astype(BF16)

    pw = gw // 2

    def project(j):
        cols = slice(j * pw, (j + 1) * pw)
        h_next[:, cols] = jnp.dot(xb, w_ref[:, cols], preferred_element_type=F32) + b_ref[:, cols]

    def shifted(j):
        cols = slice(j * gw, (j + 1) * gw)
        cur = h_cur[:, cols]
        prev_row = carry_ref[0:1, cols] * keep
        carry_ref[0:1, cols] = cur[tm - 1:tm, :]
        prev = jnp.where(row0, prev_row, pltpu.roll(cur, 1, axis=0))
        return cur + (prev - cur) * mu_ref[:, cols]

    project(0)
    sm = shifted(3)
    wa = sm[:, SM_WA[0]:SM_WA[0] + SM_WA[1]]
    gl = sm[:, SM_GL[0]:SM_GL[0] + SM_GL[1]]
    project(1)
    w_log = -_softplus(-(w0_ref[...] + _bdot(jnp.tanh(wa), w2_ref[...]))) - RWKV_DECAY_OFFSET
    lw_out[...] = -jnp.exp(w_log)
    project(2)
    a = _sigmoid(a0_ref[...] + _bdot(wa, a2_ref[...]))
    g_out[...] = _bdot(_sigmoid(gl), g2_ref[...])
    project(3)
    k = shifted(1)
    kk = k * kk_ref[...]
    norm = jnp.sqrt(_dot_mask(kk * kk, bd_ref[...]))
    project(4)
    kk = kk / jnp.maximum(norm, KK_NORM_FLOOR)
    kk_out[...] = kk
    kka_out[...] = kk * a
    k2 = k * (1.0 + (a - 1.0) * ka_ref[...])
    k_out[...] = k2
    project(5)
    r = shifted(0)
    r_out[...] = r
    project(6)
    v = shifted(2)
    v_out[...] = v.astype(v_out.dtype)
    project(7)
    bonus_out[...] = _dot_mask(r * k2 * rk_ref[...], bd_ref[...]) * v


def _rwkv_prep(x, batch, seq, params):
    t = batch * seq
    tm = min(TILE_ROWS, seq)
    bps = seq // tm
    n_tiles = t // tm
    width = params[0].shape[1]

    def full(a):
        return pl.BlockSpec(a.shape, lambda s: (0,) * a.ndim)

    out = lambda dtype: jax.ShapeDtypeStruct((t, RWKV_WIDTH), dtype)
    ospec = pl.BlockSpec((tm, RWKV_WIDTH), lambda s: (jnp.maximum(s - 1, 0), 0))
    out_dtypes = (F32, F32, BF16, F32, F32, F32, F32, F32)
    return pl.pallas_call(
        functools.partial(_rwkv_prep_kernel, blocks_per_seq=bps),
        out_shape=[out(dt) for dt in out_dtypes],
        grid=(n_tiles + 1,),
        in_specs=[pl.BlockSpec((tm, D_MODEL), lambda s: (jnp.minimum(s, n_tiles - 1), 0))]
        + [full(a) for a in params],
        out_specs=[ospec] * 8,
        scratch_shapes=[pltpu.VMEM((tm, width), F32), pltpu.VMEM((tm, width), F32),
                        pltpu.VMEM((SUBLANES, width), F32)],
        compiler_params=_cparams("arbitrary"),
    )(x, *params)


PAIR_W = 2 * RWKV_HEAD
N_PAIRS = RWKV_HEADS // 2
HEAD_SHIFT = RWKV_HEAD.bit_length() - 1


def _pair_masks():
    t = lax.broadcasted_iota(I32, (CHUNK, PAIR_W), 0)
    lane = lax.broadcasted_iota(I32, (CHUNK, PAIR_W), 1)
    j = lane & (RWKV_HEAD - 1)
    r = lax.broadcasted_iota(I32, (PAIR_W, PAIR_W), 0)
    c = lax.broadcasted_iota(I32, (PAIR_W, PAIR_W), 1)
    r4 = lax.broadcasted_iota(I32, (2 * PAIR_W, PAIR_W), 0)
    c4 = lax.broadcasted_iota(I32, (2 * PAIR_W, PAIR_W), 1)
    return dict(
        strict=t > j, incl=t >= j, eye=(t == j).astype(F32),
        same_block=jnp.bitwise_xor(t, j) < INV_BLOCK,
        first_head=lane < RWKV_HEAD,
        block_diag=(r >> HEAD_SHIFT) == (c >> HEAD_SHIFT),
        stacked_diag=((r4 >> HEAD_SHIFT) & 1) == (c4 >> HEAD_SHIFT))


N_SLOT_REFS = 5


def _rwkv_scan_kernel(*refs, chunks, blocks_per_seq):
    *io_refs, st_ref = refs[:-2 * N_SLOT_REFS]
    slot_a, slot_b = refs[-2 * N_SLOT_REFS:-N_SLOT_REFS], refs[-N_SLOT_REFS:]
    s = pl.program_id(0)

    @pl.when(s == 0)
    def _():
        for ref in slot_b:
            ref[...] = jnp.zeros_like(ref)

    @pl.when((s == 0) | (((s + blocks_per_seq - 1) % blocks_per_seq) == 0))
    def _():
        st_ref[...] = jnp.zeros_like(st_ref)

    body = functools.partial(_rwkv_scan_body, chunks, (*io_refs, st_ref))

    @pl.when((s & 1) == 0)
    def _():
        body(slot_a, slot_b)

    @pl.when((s & 1) == 1)
    def _():
        body(slot_b, slot_a)


def _rwkv_scan_body(chunks, refs, wr, rd):
    r_ref, k_ref, v_ref, lw_ref, kk_ref, kka_ref, bonus_ref, g_ref, lg_ref, lb_ref, o_ref, st_ref = refs
    yl_w, rm_w, gm_w, qm_w, gam_w = wr
    yl_r, rm_r, gm_r, qm_r, gam_r = rd
    row, col = _tri_masks(CHUNK)
    ltri = (row >= col).astype(BF16)
    m = _pair_masks()
    zero = jnp.zeros((), BF16)
    pairs = range(N_PAIRS)
    sl = [slice(p * PAIR_W, (p + 1) * PAIR_W) for p in pairs]

    def block_diag(x):
        xb = x.astype(BF16)
        return jnp.where(m["block_diag"], jnp.concatenate([xb, xb], axis=0), zero)

    def pair_mul(lhs, *rhs):
        rb = jnp.concatenate([block_diag(x) for x in rhs], axis=1) if len(rhs) > 1 else block_diag(rhs[0])
        out = jnp.dot(lhs.astype(BF16), rb, preferred_element_type=F32)
        return [out[:, i * PAIR_W:(i + 1) * PAIR_W] for i in range(len(rhs))]

    def each(fn, *lists):
        return [fn(*args) for args in zip(*lists)]

    carried = {"state": [st_ref[p] for p in pairs]}
    pending = iter(range(chunks))
    inv_n = 1.0 / RWKV_HEAD

    def head_stat(z):
        s0 = jnp.sum(jnp.where(m["first_head"], z, 0.0), axis=-1, keepdims=True)
        s1 = jnp.sum(jnp.where(m["first_head"], 0.0, z), axis=-1, keepdims=True)
        return jnp.where(m["first_head"], s0, s1) * inv_n

    def previous_tile_chunk():
        c = next(pending, None)
        if c is None:
            return
        rows = pl.ds(c * CHUNK, CHUNK)
        items = range(c * N_PAIRS, (c + 1) * N_PAIRS)
        state = carried["state"]
        y = [yl_r[i] + _bdot_nt(rm_r[i], st) for i, st in zip(items, state)]
        carried["state"] = [st * gam_r[i, 0:1, :] + _bdot_nt(st, gm_r[i]) + qm_r[i] for i, st in zip(items, state)]
        outs = []
        for yp in y:
            yc = yp - head_stat(yp)
            outs.append(yc * lax.rsqrt(head_stat(yc * yc) + RWKV_LNX_EPS))
        yn = jnp.concatenate(outs, axis=1) * lg_ref[...] + lb_ref[...]
        o_ref[rows, :] = ((yn + bonus_ref[rows, :]) * g_ref[rows, :]).astype(o_ref.dtype)

    lhs, rhs, vp, ends, gammas = [], [], [], [], []
    for c in range(chunks):
        rows = pl.ds(c * CHUNK, CHUNK)
        lw = lw_ref[rows, :]
        cum = _mask_dot(ltri, lw)
        cum_end = cum[CHUNK - 1:CHUNK, :]
        e_inv = jnp.exp(-cum)
        e_end = jnp.exp(cum_end - cum)
        kka = kka_ref[rows, :]
        k = k_ref[rows, :]
        a_bar = (-kk_ref[rows, :] * jnp.exp(cum - lw)).astype(BF16)
        r_bar = (r_ref[rows, :] * jnp.exp(cum)).astype(BF16)
        b_til = (kka * e_inv).astype(BF16)
        k_til = (k * e_inv).astype(BF16)
        b_end = (kka * e_end).astype(BF16)
        k_end = (k * e_end).astype(BF16)
        gamma = jnp.exp(cum_end)
        v = v_ref[rows, :].astype(BF16)

        for s in sl:
            lhs.append(jnp.concatenate([a_bar[:, s], r_bar[:, s]], axis=0))
            rhs.append(jnp.where(m["stacked_diag"], jnp.concatenate(
                [b_til[:, s], b_til[:, s], k_til[:, s], k_til[:, s]], axis=0), zero))
            vp.append(v[:, s])
            ends.append(jnp.concatenate([b_end[:, s], k_end[:, s]], axis=0))
            gammas.append(gamma[:, s])

    blocks = each(_bdot_nt, lhs, rhs)
    previous_tile_chunk()
    a_ab = [jnp.where(m["strict"], b[:CHUNK, :PAIR_W], 0.0) for b in blocks]
    a_ak = [jnp.where(m["strict"], b[:CHUNK, PAIR_W:], 0.0) for b in blocks]
    r_ab = [jnp.where(m["incl"], b[CHUNK:, :PAIR_W], 0.0) for b in blocks]
    r_ak = [jnp.where(m["incl"], b[CHUNK:, PAIR_W:], 0.0) for b in blocks]
    from_v = [jnp.dot(jnp.concatenate([ak, rk], axis=0).astype(BF16), block_diag(x),
                      preferred_element_type=F32) for ak, rk, x in zip(a_ak, r_ak, vp)]
    previous_tile_chunk()
    d = [jnp.where(m["same_block"], a, 0.0) for a in a_ab]
    low = [a - di for a, di in zip(a_ab, d)]
    pm = [m["eye"] + di for di in d]
    dpow = [pair_mul(di, di)[0] for di in d]
    previous_tile_chunk()
    for _ in range(INV_BLOCK.bit_length() - 3):
        both = each(lambda x, p: pair_mul(x, x, p), dpow, pm)
        dpow = [b[0] for b in both]
        pm = [p + b[1] for p, b in zip(pm, both)]
        previous_tile_chunk()
    pm = [p + pair_mul(x, p)[0] for p, x in zip(pm, dpow)]
    previous_tile_chunk()
    both = each(lambda p, lo, fv, l: pair_mul(p, lo, fv[:CHUNK], l[:CHUNK]), pm, low, from_v, lhs)
    previous_tile_chunk()
    both2 = each(lambda b: pair_mul(b[0], b[0], b[1], b[2]), both)
    previous_tile_chunk()
    xw_v = [b[1] + b2[1] for b, b2 in zip(both, both2)]
    xw_a = [b[2] + b2[2] for b, b2 in zip(both, both2)]
    both3 = each(lambda b2, xv, xa: pair_mul(b2[0], xv, xa), both2, xw_v, xw_a)
    u_v = [xv + b3[0] for xv, b3 in zip(xw_v, both3)]
    wm = [xa + b3[1] for xa, b3 in zip(xw_a, both3)]
    both4 = each(lambda rb, uv, w: pair_mul(rb, uv, w), r_ab, u_v, wm)
    y_local = [fv[CHUNK:] + b4[0] for fv, b4 in zip(from_v, both4)]
    rm = [l[CHUNK:] + b4[1] for l, b4 in zip(lhs, both4)]
    gm = [jnp.where(m["block_diag"], _bdot_tn(en[:CHUNK], w), 0.0) for en, w in zip(ends, wm)]
    qm = [jnp.where(m["block_diag"], _bdot_tn(jnp.concatenate([uv.astype(BF16), x], axis=0), en), 0.0)
          for uv, x, en in zip(u_v, vp, ends)]
    for _ in range(chunks):
        previous_tile_chunk()
    for p in pairs:
        st_ref[p] = carried["state"][p]

    for i in range(chunks * N_PAIRS):
        yl_w[i] = y_local[i]
        rm_w[i] = rm[i]
        gm_w[i] = gm[i]
        qm_w[i] = qm[i]
        gam_w[i, 0:1, :] = gammas[i]


def _rwkv_scan(prep, batch, seq, lnx_g, lnx_b):
    t = batch * seq
    chunks = CHUNKS_PER_STEP
    rows = CHUNK * chunks
    nblk = seq // rows
    n_tiles = batch * nblk
    n_items = chunks * N_PAIRS
    cur = pl.BlockSpec((rows, RWKV_WIDTH), lambda s: (jnp.minimum(s, n_tiles - 1), 0))
    prev = pl.BlockSpec((rows, RWKV_WIDTH), lambda s: (jnp.maximum(s - 1, 0), 0))
    pspec = pl.BlockSpec((1, RWKV_WIDTH), lambda s: (0, 0))
    slot = [pltpu.VMEM((n_items, CHUNK, PAIR_W), F32), pltpu.VMEM((n_items, CHUNK, PAIR_W), F32),
            pltpu.VMEM((n_items, PAIR_W, PAIR_W), F32), pltpu.VMEM((n_items, PAIR_W, PAIR_W), F32),
            pltpu.VMEM((n_items, SUBLANES, PAIR_W), F32)]
    return pl.pallas_call(
        functools.partial(_rwkv_scan_kernel, chunks=chunks, blocks_per_seq=nblk),
        out_shape=jax.ShapeDtypeStruct((t, RWKV_WIDTH), BF16),
        grid=(n_tiles + 1,),
        in_specs=[cur] * 6 + [prev, prev, pspec, pspec],
        out_specs=prev,
        scratch_shapes=[pltpu.VMEM((N_PAIRS, PAIR_W, PAIR_W), F32)] + slot + slot,
        compiler_params=_cparams("arbitrary"),
    )(*prep, lnx_g, lnx_b)


def _merge_kernel(og_ref, ol_ref, or_ref, x_ref, wgt_ref, bgt_ref, pg_ref, pl_ref, pr_ref, wo_ref,
                  lg_ref, lb_ref, xo_ref):
    d = D_MODEL
    xin = x_ref[...].astype(BF16)
    merged = None
    for b, (o_ref, p_ref) in enumerate(((og_ref, pg_ref), (ol_ref, pl_ref), (or_ref, pr_ref))):
        cols = slice(b * d, (b + 1) * d)
        gate = _sigmoid(jnp.dot(xin, wgt_ref[:, cols], preferred_element_type=F32) + bgt_ref[:, cols])
        term = gate * jnp.dot(o_ref[...], p_ref[...], preferred_element_type=F32)
        merged = term if merged is None else merged + term
    z = DEEPNORM_ALPHA * x_ref[...] + _bdot(merged, wo_ref[...])
    out = _layer_norm_rows(z, lg_ref[...], lb_ref[...], LN_EPS)
    xo_ref[...] = out


def _merge(o_gla, o_lru, o_rwkv, x, w_gates, b_gates, p_gla, p_lru, p_rwkv, w_out, ln_g, ln_b):
    t = x.shape[0]
    tm = min(WIDE_TILE_ROWS, t)

    def rows(width):
        return pl.BlockSpec((tm, width), lambda i: (i, 0))

    def full(a):
        return pl.BlockSpec(a.shape, lambda i: (0,) * a.ndim, pipeline_mode=pl.Buffered(1))

    params = (w_gates, b_gates, p_gla, p_lru, p_rwkv, w_out, ln_g, ln_b)
    return pl.pallas_call(
        _merge_kernel,
        out_shape=jax.ShapeDtypeStruct((t, D_MODEL), F32),
        grid=(t // tm,),
        in_specs=[rows(GLA_VW), rows(LRU_WIDTH), rows(RWKV_WIDTH), rows(D_MODEL)] + [full(a) for a in params],
        out_specs=rows(D_MODEL),
        compiler_params=_cparams("parallel"),
    )(o_gla, o_lru, o_rwkv, x, *params)


FFN_SPLIT = 2


def _ffn_kernel(x_ref, wg_ref, wu_ref, wd_ref, lg_ref, lb_ref, xo_ref):
    x = x_ref[...]
    xb = x.astype(BF16)
    tf = wg_ref.shape[1] // FFN_SPLIT
    acc = None
    for f in range(FFN_SPLIT):
        cols = slice(f * tf, (f + 1) * tf)
        g = jnp.dot(xb, wg_ref[:, cols], preferred_element_type=F32)
        u = jnp.dot(xb, wu_ref[:, cols], preferred_element_type=F32)
        part = _bdot(g * _sigmoid(g) * u, wd_ref[cols, :])
        acc = part if acc is None else acc + part
    z = DEEPNORM_ALPHA * x + acc
    xo_ref[...] = _layer_norm_rows(z, lg_ref[...], lb_ref[...], LN_EPS)


def _dense_ffn(x, wg, wu, wd, ln_g, ln_b):
    t = x.shape[0]
    tm = min(WIDE_TILE_ROWS, t)
    rows = pl.BlockSpec((tm, D_MODEL), lambda i: (i, 0))
    resident = lambda a: pl.BlockSpec(a.shape, lambda i: (0,) * a.ndim, pipeline_mode=pl.Buffered(1))
    return pl.pallas_call(
        _ffn_kernel,
        out_shape=jax.ShapeDtypeStruct((t, D_MODEL), F32),
        grid=(t // tm,),
        in_specs=[rows] + [resident(a) for a in (wg, wu, wd, ln_g, ln_b)],
        out_specs=rows,
        compiler_params=_cparams("parallel"),
    )(x, wg, wu, wd, ln_g, ln_b)


ROW_TILE = (D_MODEL // LANES, LANES)
assert ROW_TILE[0] == SUBLANES


def _rows_to_tiles(x, ref3):
    ref3[...] = x.reshape(x.shape[0], *ROW_TILE)


def _tiles_to_rows(ref3):
    return ref3[...].reshape(ref3.shape[0], D_MODEL)


def _router_kernel(x_ref, wr_ref, meta_ref, wts_ref, cnt_ref, carry_ref):
    tm = x_ref.shape[0]

    @pl.when(pl.program_id(0) == 0)
    def _():
        carry_ref[...] = jnp.zeros_like(carry_ref)

    lane = lax.broadcasted_iota(I32, (tm, LANES), 1)
    neg = jnp.float32(-jnp.inf)
    logits = jnp.where(lane < N_EXPERTS, _fdot(x_ref[...], wr_ref[...]), neg)
    m1 = jnp.max(logits, axis=-1, keepdims=True)
    e1 = jnp.min(jnp.where(logits == m1, lane, LANES), axis=-1, keepdims=True)
    rest = jnp.where(lane == e1, neg, logits)
    m2 = jnp.max(rest, axis=-1, keepdims=True)
    e2 = jnp.min(jnp.where(rest == m2, lane, LANES), axis=-1, keepdims=True)
    ex = jnp.exp(m2 - m1)
    w1 = 1.0 / (1.0 + ex)
    w2 = ex / (1.0 + ex)

    hot1 = lane == e1
    hot2 = lane == e2
    onehot = jnp.where(hot1 | hot2, 1.0, 0.0)
    row, col = _tri_masks(tm)
    before = (row > col).astype(BF16)
    prefix = jnp.dot(before, onehot.astype(BF16), preferred_element_type=F32) + carry_ref[...]
    rank1 = jnp.sum(jnp.where(hot1, prefix, 0.0), axis=-1, keepdims=True).astype(I32)
    rank2 = jnp.sum(jnp.where(hot2, prefix, 0.0), axis=-1, keepdims=True).astype(I32)
    carry_ref[...] += jnp.sum(onehot, axis=0, keepdims=True)
    cnt_ref[...] = carry_ref[...]

    meta = jnp.where(lane == 0, e1, jnp.where(lane == 1, e2, jnp.where(lane == 2, rank1, rank2)))
    meta_ref[...] = jnp.where(lane < 4, meta, 0)
    wts_ref[...] = jnp.where(lane == 0, w1, jnp.where(lane == 1, w2, 0.0))


def _router(x, wr_pad):
    t = x.shape[0]
    tm = min(TILE_ROWS, t)
    return pl.pallas_call(
        _router_kernel,
        out_shape=[jax.ShapeDtypeStruct((t, LANES), I32), jax.ShapeDtypeStruct((t, LANES), F32),
                   jax.ShapeDtypeStruct((1, LANES), F32)],
        grid=(t // tm,),
        in_specs=[pl.BlockSpec((tm, D_MODEL), lambda i: (i, 0)),
                  pl.BlockSpec((D_MODEL, LANES), lambda i: (0, 0))],
        out_specs=[pl.BlockSpec((tm, LANES), lambda i: (i, 0)), pl.BlockSpec((tm, LANES), lambda i: (i, 0)),
                   pl.BlockSpec((1, LANES), lambda i: (0, 0))],
        scratch_shapes=[pltpu.VMEM((1, LANES), F32)],
        compiler_params=_cparams("arbitrary"),
    )(x, wr_pad)


def _dispatch_kernel(pad_start_ref, pad_len_ref, nused_ref, dest_ref, x_ref, *rest):
    n_w = (len(rest) - 5) // 2
    w_refs, xs_ref, wb_refs = rest[:n_w], rest[n_w], rest[n_w + 1:2 * n_w + 1]
    x3_ref, zeros_ref, sem, fill_sem = rest[-4:]
    tm = x_ref.shape[0]
    n_blocks = xs_ref.shape[0] // MOE_ROWS
    _rows_to_tiles(x_ref[...], x3_ref)

    @pl.when(pl.program_id(0) == pl.num_programs(0) - 1)
    def _():
        zeros_ref[...] = jnp.zeros_like(zeros_ref)
        pad_copy = lambda row: pltpu.make_async_copy(zeros_ref.at[0], xs_ref.at[row], fill_sem)
        blk_copy = lambda blk: pltpu.make_async_copy(
            zeros_ref, xs_ref.at[pl.ds(pl.multiple_of(blk * MOE_ROWS, MOE_ROWS), MOE_ROWS)], fill_sem)

        def for_each(fn):
            for e in range(N_EXPERTS):
                base = pad_start_ref[e]
                lax.fori_loop(0, pad_len_ref[e], lambda r, c: (fn(pad_copy(base + r)), c)[1], 0)
            lax.fori_loop(nused_ref[0], n_blocks, lambda b, c: (fn(blk_copy(b)), c)[1], 0)

        for_each(lambda cp: cp.start())
        for_each(lambda cp: cp.wait())

    def row_copy(r, j):
        return pltpu.make_async_copy(x3_ref.at[r], xs_ref.at[dest_ref[TOP_K * r + j]], sem)

    def start(r, c):
        for j in range(TOP_K):
            row_copy(r, j).start(priority=j)
        return c

    lax.fori_loop(0, tm, start, 0, unroll=8)
    for w_ref, wb_ref in zip(w_refs, wb_refs):
        wb_ref[...] = w_ref[...].astype(wb_ref.dtype)
    for j in range(TOP_K):
        pltpu.make_async_copy(x3_ref, xs_ref.at[pl.ds(0, tm)], sem).wait()


def _dispatch(x, dest_flat, pad_start, pad_len, n_used, n_slots, weights):
    t = x.shape[0]
    tm = min(WIDE_TILE_ROWS, t)
    n_steps = t // tm
    flat = [w.reshape(-1, w.shape[-1]) for w in weights]
    slab = lambda w: pl.BlockSpec((w.shape[0] // n_steps, w.shape[1]), lambda i, *_: (i, 0))
    grid_spec = pltpu.PrefetchScalarGridSpec(
        num_scalar_prefetch=3,
        grid=(n_steps,),
        in_specs=[pl.BlockSpec((tm * TOP_K,), lambda i, *_: (i,), memory_space=pltpu.SMEM),
                  pl.BlockSpec((tm, D_MODEL), lambda i, *_: (i, 0))] + [slab(w) for w in flat],
        out_specs=[pl.BlockSpec(memory_space=pl.ANY)] + [slab(w) for w in flat],
        scratch_shapes=[pltpu.VMEM((tm, *ROW_TILE), F32), pltpu.VMEM((MOE_ROWS, *ROW_TILE), F32),
                        pltpu.SemaphoreType.DMA(()), pltpu.SemaphoreType.DMA(())],
    )
    xs, *cast = pl.pallas_call(
        _dispatch_kernel,
        out_shape=[jax.ShapeDtypeStruct((n_slots, *ROW_TILE), F32)]
        + [jax.ShapeDtypeStruct(w.shape, BF16) for w in flat],
        grid_spec=grid_spec,
        compiler_params=_cparams("arbitrary"),
    )(pad_start, pad_len, n_used, dest_flat, x, *flat)
    return xs, [c.reshape(w.shape) for c, w in zip(cast, weights)]


def _expert_kernel(blk_e_ref, nused_ref, xs_ref, wg_ref, wu_ref, wd_ref, ys_ref, acc_ref, *, n_f):
    del blk_e_ref
    i = pl.program_id(0)
    f = pl.program_id(1)
    last = n_f - 1
    used = i < nused_ref[0]

    def piece():
        xb = _tiles_to_rows(xs_ref).astype(BF16)
        g = jnp.dot(xb, wg_ref[0], preferred_element_type=F32)
        u = jnp.dot(xb, wu_ref[0], preferred_element_type=F32)
        return _bdot(g * _sigmoid(g) * u, wd_ref[0])

    if n_f == 1:
        @pl.when(used)
        def _():
            _rows_to_tiles(piece(), ys_ref)
    else:
        @pl.when(used & (f == 0))
        def _():
            acc_ref[...] = piece()

        if n_f > 2:
            @pl.when(used & (f != 0) & (f != last))
            def _():
                acc_ref[...] += piece()

        @pl.when(used & (f == last))
        def _():
            _rows_to_tiles(acc_ref[...] + piece(), ys_ref)

    @pl.when(jnp.logical_not(used) & (f == last))
    def _():
        ys_ref[...] = jnp.zeros_like(ys_ref)


def _experts(xs, blk_e, n_used, wg, wu, wd):
    n_slots = xs.shape[0]
    n_blocks = n_slots // MOE_ROWS
    ff = wg.shape[2]
    n_f = 2
    tf = ff // n_f
    piece = lambda i, f: jnp.where((i & 1) == 1, n_f - 1 - f, f)
    grid_spec = pltpu.PrefetchScalarGridSpec(
        num_scalar_prefetch=2,
        grid=(n_blocks, n_f),
        in_specs=[pl.BlockSpec((MOE_ROWS, *ROW_TILE), lambda i, f, be, nu: (i, 0, 0)),
                  pl.BlockSpec((1, D_MODEL, tf), lambda i, f, be, nu: (be[i], 0, piece(i, f))),
                  pl.BlockSpec((1, D_MODEL, tf), lambda i, f, be, nu: (be[i], 0, piece(i, f))),
                  pl.BlockSpec((1, tf, D_MODEL), lambda i, f, be, nu: (be[i], piece(i, f), 0))],
        out_specs=pl.BlockSpec((MOE_ROWS, *ROW_TILE), lambda i, f, be, nu: (i, 0, 0)),
        scratch_shapes=[pltpu.VMEM((MOE_ROWS, D_MODEL), F32)],
    )
    return pl.pallas_call(
        functools.partial(_expert_kernel, n_f=n_f),
        out_shape=jax.ShapeDtypeStruct((n_slots, *ROW_TILE), F32),
        grid_spec=grid_spec,
        compiler_params=_cparams("arbitrary", "arbitrary"),
    )(blk_e, n_used, xs, wg, wu, wd)


def _combine_kernel(dest_ref, dest_next_ref, x_ref, wts_ref, ys_ref, lg_ref, lb_ref, xo_ref, buf_ref, sem):
    tm = x_ref.shape[0]
    i = pl.program_id(0)
    n = pl.num_programs(0)

    def gather(idx_ref, slot):
        def start(r, c):
            for j in range(TOP_K):
                pltpu.make_async_copy(ys_ref.at[idx_ref[TOP_K * r + j]],
                                      buf_ref.at[slot, j, r], sem.at[slot]).start(priority=j)
            return c
        lax.fori_loop(0, tm, start, 0, unroll=8)

    slot = i & 1

    @pl.when(i == 0)
    def _():
        gather(dest_ref, 0)

    @pl.when(i + 1 < n)
    def _():
        gather(dest_next_ref, 1 - slot)

    for j in range(TOP_K):
        pltpu.make_async_copy(ys_ref.at[pl.ds(0, tm)], buf_ref.at[slot, j], sem.at[slot]).wait()
    w = wts_ref[...]
    f = w[:, 0:1] * _tiles_to_rows(buf_ref.at[slot, 0]) + w[:, 1:2] * _tiles_to_rows(buf_ref.at[slot, 1])
    z = DEEPNORM_ALPHA * x_ref[...] + f
    xo_ref[...] = _layer_norm_rows(z, lg_ref[...], lb_ref[...], LN_EPS)


def _combine(x, dest_flat, wts, ys, ln_g, ln_b):
    t = x.shape[0]
    tm = min(TILE_ROWS, t)
    n_tiles = t // tm
    rows = lambda width: pl.BlockSpec((tm, width), lambda i: (i, 0))
    vec = pl.BlockSpec((1, D_MODEL), lambda i: (0, 0))
    return pl.pallas_call(
        _combine_kernel,
        out_shape=jax.ShapeDtypeStruct((t, D_MODEL), F32),
        grid=(n_tiles,),
        in_specs=[pl.BlockSpec((tm * TOP_K,), lambda i: (i,), memory_space=pltpu.SMEM),
                  pl.BlockSpec((tm * TOP_K,), lambda i: (jnp.minimum(i + 1, n_tiles - 1),),
                               memory_space=pltpu.SMEM),
                  rows(D_MODEL), rows(LANES), pl.BlockSpec(memory_space=pl.ANY), vec, vec],
        out_specs=rows(D_MODEL),
        scratch_shapes=[pltpu.VMEM((2, TOP_K, tm, *ROW_TILE), F32), pltpu.SemaphoreType.DMA((2,))],
        compiler_params=_cparams("arbitrary"),
    )(dest_flat, dest_flat, x, wts, ys, ln_g, ln_b)


def _moe_ffn(x, w_router, wg, wu, wd, ln_g, ln_b):
    t = x.shape[0]
    wr_pad = jnp.pad(w_router, ((0, 0), (0, LANES - N_EXPERTS)))
    meta, wts, cnt = _router(x, wr_pad)
    counts = cnt[0, :N_EXPERTS].astype(I32)
    padded = (counts + MOE_ROWS - 1) // MOE_ROWS * MOE_ROWS
    seg_end = jnp.cumsum(padded)
    seg_start = seg_end - padded
    n_blocks = (t * TOP_K) // MOE_ROWS + N_EXPERTS
    dest = seg_start[meta[:, 0:TOP_K]] + meta[:, TOP_K:2 * TOP_K]
    dest_flat = dest.reshape(-1).astype(I32)
    blk_start = jnp.arange(n_blocks, dtype=I32) * MOE_ROWS
    blk_e = jnp.minimum(jnp.sum(blk_start[:, None] >= seg_end[None, :], axis=1), N_EXPERTS - 1).astype(I32)
    n_used = (seg_end[-1:] // MOE_ROWS).astype(I32)
    xs, (wg, wu, wd) = _dispatch(x, dest_flat, (seg_start + counts).astype(I32), (padded - counts).astype(I32),
                                 n_used, n_blocks * MOE_ROWS, (wg, wu, wd))
    ys = _experts(xs, blk_e, n_used, wg, wu, wd)
    return _combine(x, dest_flat, wts, ys, ln_g, ln_b)


def _pad_rows(w, rows, at=0):
    out = jnp.zeros((rows, w.shape[1]), w.dtype)
    return out.at[at:at + w.shape[0]].set(w)


def _reorder_in_projection(w, b):
    sizes = (GLA_KW, GLA_KW, GLA_VW, GLA_VW, GLA_DECAY_RANK, LRU_WIDTH, LRU_WIDTH,
             3 * RWKV_WIDTH, RWKV_DECAY_RANK, RWKV_A_RANK, RWKV_GATE_RANK, N_BRANCH * D_MODEL)
    offs = [0]
    for s in sizes:
        offs.append(offs[-1] + s)
    wb = jnp.concatenate([w, b[None, :]], axis=0)
    piece = lambda i: wb[:, offs[i]:offs[i + 1]]
    zeros = lambda n: jnp.zeros((wb.shape[0], n), wb.dtype)
    q, k, v, r, dec, lx, lg, rkv, wl, al, gl, gates = (piece(i) for i in range(len(sizes)))
    small = jnp.concatenate([wl, al, gl, zeros(RW_SMALL_W - SM_GL[0] - RWKV_GATE_RANK)], axis=1)
    groups = {"gla": jnp.concatenate([q, k, v, r, dec, zeros(GLA_DEC_W - GLA_DECAY_RANK)], axis=1),
              "lru": jnp.concatenate([lx, lg], axis=1),
              "rwkv": jnp.concatenate([rkv, small], axis=1),
              "gates": gates}
    return {name: (g[:-1].astype(BF16), g[-1:]) for name, g in groups.items()}


def _block_diag(blocks):
    n, bi, bo = blocks.shape
    eye = jnp.eye(n, dtype=blocks.dtype)
    return (eye[:, None, :, None] * blocks[:, :, None, :]).reshape(n * bi, n * bo)


def kernel(x, w_in, b_in, gla_w_decay_up, gla_b_decay, gla_norm_g, gla_norm_b, lru_conv_w, lru_conv_b, lru_w_r, lru_b_r, lru_w_i, lru_b_i, lru_lambda, rwkv_mu, rwkv_w0, rwkv_w2, rwkv_a0, rwkv_a2, rwkv_g2, rwkv_k_k, rwkv_k_a, rwkv_r_k, rwkv_lnx_g, rwkv_lnx_b, p_gla, p_lru, p_rwkv, w_out, ln_mix_g, ln_mix_b, ffn_w_gate, ffn_w_up, ffn_w_down, moe_w_router, moe_w_gate, moe_w_up, moe_w_down, ln_ffn_g, ln_ffn_b):
    batch, seq, d = x.shape
    t = batch * seq
    xf = x.reshape(t, d)
    row = lambda a: a.reshape(1, -1)
    head_ones = _block_diag(jnp.ones((RWKV_HEADS, RWKV_HEAD, RWKV_HEAD), BF16))
    for l in range(DEPTH):
        proj = _reorder_in_projection(w_in[l], b_in[l])
        w_gates, b_gates = proj["gates"]

        wup = _pad_rows(gla_w_decay_up[l], GLA_DEC_W).astype(BF16)
        o_gla = _gla_branch(xf, batch, seq, *proj["gla"], wup, row(gla_b_decay[l]), row(gla_norm_g[l]),
                            row(gla_norm_b[l]))

        w_ri = jnp.concatenate([_block_diag(lru_w_r[l]), _block_diag(lru_w_i[l])], axis=1).astype(BF16)
        b_ri = jnp.concatenate([lru_b_r[l], lru_b_i[l]])[None, :]
        o_lru = _lru_branch(xf, batch, seq, *proj["lru"], lru_conv_w[l], row(lru_conv_b[l]), w_ri, b_ri,
                            row(jax.nn.softplus(-lru_lambda[l])))

        mu = rwkv_mu[l]
        mu_all = jnp.zeros((3 * RWKV_WIDTH + RW_SMALL_W,), F32).at[0:mu.shape[0]].set(mu)
        prep_params = (*proj["rwkv"], row(mu_all),
                       _pad_rows(rwkv_w2[l], SM_WA[1], 0).astype(BF16),
                       _pad_rows(rwkv_a2[l], SM_WA[1], RWKV_DECAY_RANK).astype(BF16),
                       _pad_rows(rwkv_g2[l], SM_GL[1], 0).astype(BF16),
                       row(rwkv_w0[l]), row(rwkv_a0[l]), row(rwkv_k_k[l]), row(rwkv_k_a[l]),
                       row(rwkv_r_k[l]), head_ones)
        prep = _rwkv_prep(xf, batch, seq, prep_params)
        o_rwkv = _rwkv_scan(prep, batch, seq, row(rwkv_lnx_g[l]), row(rwkv_lnx_b[l]))

        xf = _merge(o_gla, o_lru, o_rwkv, xf, w_gates, b_gates, p_gla[l].astype(BF16), p_lru[l].astype(BF16),
                    p_rwkv[l].astype(BF16), w_out[l].astype(BF16), row(ln_mix_g[l]), row(ln_mix_b[l]))
        i = l // 2
        if l % 2 == 0:
            xf = _dense_ffn(xf, ffn_w_gate[i].astype(BF16), ffn_w_up[i].astype(BF16),
                            ffn_w_down[i].astype(BF16), row(ln_ffn_g[l]), row(ln_ffn_b[l]))
        else:
            xf = _moe_ffn(xf, moe_w_router[i], moe_w_gate[i], moe_w_up[i], moe_w_down[i],
                          row(ln_ffn_g[l]), row(ln_ffn_b[l]))
    return xf.reshape(batch, seq, d)
```
